```python
import jax, jax.numpy as jnp
from jax import lax
import numpy as np

D_MODEL = 1024
BATCH = 8
SEQ = 4096
DEPTH = 4

N_MIXERS = 2
N_SSD_LAYERS = (DEPTH + 1) // 2
N_SB_LAYERS = DEPTH // 2
NORM_EPS = 1e-6

SSD_EXPAND = 2
SSD_D_INNER = SSD_EXPAND * D_MODEL
SSD_HEAD_DIM = 64
SSD_HEADS = SSD_D_INNER // SSD_HEAD_DIM
SSD_GROUPS = 8
SSD_HEADS_PER_GROUP = SSD_HEADS // SSD_GROUPS
SSD_STATE = 128
SSD_CONV = 4
SSD_CHUNK = 128
SSD_CONV_DIM = SSD_D_INNER + 2 * SSD_GROUPS * SSD_STATE
SSD_IN_DIM = SSD_D_INNER + SSD_CONV_DIM + SSD_HEADS
SSD_DT_MIN = 1e-3
SSD_DT_MAX = 1e-1

SB_HEADS = 16
SB_HEAD_DIM = D_MODEL // SB_HEADS
SB_Q_BLOCK = 128

FFN_D_FF = 2816
FFN_CONV = 3

kernel_name = "hybrid_ssd_stickbreaking_convffn"


def rms_norm(x, g):
    xf = x.astype(jnp.float32)
    y = xf * lax.rsqrt(jnp.mean(xf * xf, axis=-1, keepdims=True) + NORM_EPS)
    return (y * g.astype(jnp.float32)).astype(x.dtype)


def causal_dwconv(u, w, b):
    width = w.shape[0]
    s = u.shape[1]
    up = jnp.pad(u, ((0, 0), (width - 1, 0), (0, 0)))
    return b + sum(w[k] * up[:, k:k + s] for k in range(width))


def ssd_chunked_scan(xdt, a, bm, cm):
    bsz, s = xdt.shape[:2]
    L, G, HG, P, N = SSD_CHUNK, SSD_GROUPS, SSD_HEADS_PER_GROUP, SSD_HEAD_DIM, SSD_STATE
    nc = s // L

    def to_chunks(t):
        t = t.reshape((bsz, nc, L) + t.shape[2:])
        return jnp.moveaxis(t, 1, 0)

    xc = to_chunks(xdt.reshape(bsz, s, G, HG, P))
    ac = to_chunks(a.reshape(bsz, s, G, HG))
    bc = to_chunks(bm)
    cc = to_chunks(cm)
    causal = jnp.tril(jnp.ones((L, L), dtype=bool))[None, :, :, None, None]

    def step(state, inp):
        x_c, a_c, b_c, c_c = inp
        acum = jnp.cumsum(a_c, axis=1)
        seg = acum[:, :, None] - acum[:, None, :]
        decay = jnp.exp(jnp.where(causal, seg, -jnp.inf))
        cb = jnp.einsum('btgn,bsgn->btsg', c_c, b_c)
        scores = cb[..., None] * decay
        y_diag = jnp.einsum('btsgh,bsghp->btghp', scores, x_c)
        y_off = jnp.einsum('btgn,bghpn->btghp', c_c, state) * jnp.exp(acum)[..., None]
        last = acum[:, -1]
        w_s = jnp.exp(last[:, None] - acum)
        new_state = (state * jnp.exp(last)[..., None, None]
                     + jnp.einsum('bsgn,bsghp->bghpn', b_c, x_c * w_s[..., None]))
        return new_state, (y_diag + y_off).astype(jnp.float32)

    state0 = jnp.zeros((bsz, G, HG, P, N), jnp.float32)
    _, ys = lax.scan(step, state0, (xc, ac, bc, cc))
    return jnp.moveaxis(ys, 0, 1).reshape(bsz, s, SSD_HEADS, P)


def ssd_mixer(h, w_in, conv_w, conv_b, dt_bias, a_log, d_skip, norm_g, w_out):
    bsz, s, _ = h.shape
    proj = h @ w_in
    z, xbc, dt = jnp.split(proj, [SSD_D_INNER, SSD_D_INNER + SSD_CONV_DIM], axis=-1)
    xbc = jax.nn.silu(causal_dwconv(xbc, conv_w, conv_b))
    xs, bm, cm = jnp.split(xbc, [SSD_D_INNER, SSD_D_INNER + SSD_GROUPS * SSD_STATE], axis=-1)
    xs = xs.reshape(bsz, s, SSD_HEADS, SSD_HEAD_DIM)
    bm = bm.reshape(bsz, s, SSD_GROUPS, SSD_STATE)
    cm = cm.reshape(bsz, s, SSD_GROUPS, SSD_STATE)
    dt = jax.nn.softplus(dt.astype(jnp.float32) + dt_bias.astype(jnp.float32))
    a = -jnp.exp(a_log.astype(jnp.float32)) * dt
    y = ssd_chunked_scan(xs * dt[..., None], a, bm, cm)
    y = y + d_skip[:, None] * xs
    y = y.reshape(bsz, s, SSD_D_INNER).astype(h.dtype)
    y = rms_norm(y * jax.nn.silu(z), norm_g)
    return y @ w_out


def stick_breaking_mixer(h, w_qkv, w_out):
    bsz, s, _ = h.shape
    qkv = (h @ w_qkv).reshape(bsz, s, 3, SB_HEADS, SB_HEAD_DIM)
    q = jnp.moveaxis(qkv[:, :, 0], 1, 2)
    k = jnp.moveaxis(qkv[:, :, 1], 1, 2)
    v = jnp.moveaxis(qkv[:, :, 2], 1, 2)
    scale = SB_HEAD_DIM ** -0.5
    outs = []
    for blk in range(s // SB_Q_BLOCK):
        q0 = blk * SB_Q_BLOCK
        kv_end = q0 + SB_Q_BLOCK
        logits = jnp.einsum('bhqd,bhkd->bhqk', q[:, :, q0:kv_end], k[:, :, :kv_end]).astype(jnp.float32) * scale
        qpos = q0 + jnp.arange(SB_Q_BLOCK)[:, None]
        kpos = jnp.arange(kv_end)[None, :]
        strict = kpos < qpos
        log_beta = jax.nn.log_sigmoid(logits)
        log_fail = jnp.where(strict, jax.nn.log_sigmoid(-logits), 0.0)
        suffix = lax.cumsum(log_fail, axis=3, reverse=True) - log_fail
        weights = jnp.where(strict, jnp.exp(log_beta + suffix), 0.0)
        outs.append(jnp.einsum('bhqk,bhkd->bhqd', weights.astype(v.dtype), v[:, :, :kv_end]))
    o = jnp.concatenate(outs, axis=2)
    o = jnp.moveaxis(o, 1, 2).reshape(bsz, s, D_MODEL)
    return o @ w_out


def conv_ffn(h, w_in, conv_w, conv_b, w_out):
    u = causal_dwconv(h @ w_in, conv_w, conv_b)
    gate, up = jnp.split(u, 2, axis=-1)
    return (jax.nn.silu(gate) * up) @ w_out


def _fwd_setup_inputs(seed: int = 0) -> dict:
    key = jax.random.key(seed)
    ks = jax.random.split(key, 24)
    f32 = jnp.float32
    out_scale = (2.0 * DEPTH) ** -0.5

    def nrm(k, shape, scale):
        return jax.random.normal(k, shape, f32) * scale

    x = jax.random.normal(ks[0], (BATCH, SEQ, D_MODEL), f32)
    mix_norm = 1.0 + nrm(ks[1], (DEPTH, D_MODEL), 0.02)
    ffn_norm = 1.0 + nrm(ks[2], (DEPTH, D_MODEL), 0.02)
    final_norm = 1.0 + nrm(ks[3], (D_MODEL,), 0.02)

    ssd_w_in = nrm(ks[4], (N_SSD_LAYERS, D_MODEL, SSD_IN_DIM), D_MODEL ** -0.5)
    ssd_conv_w = nrm(ks[5], (N_SSD_LAYERS, SSD_CONV, SSD_CONV_DIM), SSD_CONV ** -0.5)
    ssd_conv_b = nrm(ks[6], (N_SSD_LAYERS, SSD_CONV_DIM), 0.02)
    u = jax.random.uniform(ks[7], (N_SSD_LAYERS, SSD_HEADS), f32)
    dt0 = jnp.exp(u * (np.log(SSD_DT_MAX) - np.log(SSD_DT_MIN)) + np.log(SSD_DT_MIN))
    ssd_dt_bias = dt0 + jnp.log(-jnp.expm1(-dt0))
    ssd_a_log = jnp.log(jax.random.uniform(ks[8], (N_SSD_LAYERS, SSD_HEADS), f32, 1.0, 16.0))
    ssd_d = 1.0 + nrm(ks[9], (N_SSD_LAYERS, SSD_HEADS), 0.1)
    ssd_norm = 1.0 + nrm(ks[10], (N_SSD_LAYERS, SSD_D_INNER), 0.02)
    ssd_w_out = nrm(ks[11], (N_SSD_LAYERS, SSD_D_INNER, D_MODEL), SSD_D_INNER ** -0.5 * out_scale)

    sb_w_qkv = nrm(ks[12], (N_SB_LAYERS, D_MODEL, 3 * D_MODEL), D_MODEL ** -0.5)
    sb_w_out = nrm(ks[13], (N_SB_LAYERS, D_MODEL, D_MODEL), D_MODEL ** -0.5 * out_scale)

    ffn_w_in = nrm(ks[14], (DEPTH, D_MODEL, 2 * FFN_D_FF), D_MODEL ** -0.5)
    ffn_conv_w = nrm(ks[15], (DEPTH, FFN_CONV, 2 * FFN_D_FF), FFN_CONV ** -0.5)
    ffn_conv_b = nrm(ks[16], (DEPTH, 2 * FFN_D_FF), 0.02)
    ffn_w_out = nrm(ks[17], (DEPTH, FFN_D_FF, D_MODEL), FFN_D_FF ** -0.5 * out_scale)

    return {"x": x, "mix_norm": mix_norm, "ffn_norm": ffn_norm, "final_norm": final_norm,
            "ssd_w_in": ssd_w_in, "ssd_conv_w": ssd_conv_w, "ssd_conv_b": ssd_conv_b,
            "ssd_dt_bias": ssd_dt_bias, "ssd_a_log": ssd_a_log, "ssd_d": ssd_d,
            "ssd_norm": ssd_norm, "ssd_w_out": ssd_w_out,
            "sb_w_qkv": sb_w_qkv, "sb_w_out": sb_w_out,
            "ffn_w_in": ffn_w_in, "ffn_conv_w": ffn_conv_w, "ffn_conv_b": ffn_conv_b,
            "ffn_w_out": ffn_w_out}


def _fwd_reference(x, mix_norm, ffn_norm, final_norm,
              ssd_w_in, ssd_conv_w, ssd_conv_b, ssd_dt_bias, ssd_a_log, ssd_d, ssd_norm, ssd_w_out,
              sb_w_qkv, sb_w_out,
              ffn_w_in, ffn_conv_w, ffn_conv_b, ffn_w_out):
    for i in range(DEPTH):
        h = rms_norm(x, mix_norm[i])
        j = i // N_MIXERS
        if i % N_MIXERS == 0:
            x = x + ssd_mixer(h, ssd_w_in[j], ssd_conv_w[j], ssd_conv_b[j], ssd_dt_bias[j],
                              ssd_a_log[j], ssd_d[j], ssd_norm[j], ssd_w_out[j])
        else:
            x = x + stick_breaking_mixer(h, sb_w_qkv[j], sb_w_out[j])
        x = x + conv_ffn(rms_norm(x, ffn_norm[i]), ffn_w_in[i], ffn_conv_w[i], ffn_conv_b[i], ffn_w_out[i])
    return rms_norm(x, final_norm)


import jax as _jax
import jax.numpy as _jnp

TWIN_FORMAT = 'train_step'
FWD_PARAMS = ['x', 'mix_norm', 'ffn_norm', 'final_norm', 'ssd_w_in', 'ssd_conv_w', 'ssd_conv_b', 'ssd_dt_bias', 'ssd_a_log', 'ssd_d', 'ssd_norm', 'ssd_w_out', 'sb_w_qkv', 'sb_w_out', 'ffn_w_in', 'ffn_conv_w', 'ffn_conv_b', 'ffn_w_out']
TWIN_WEIGHTS = ['mix_norm', 'ffn_norm', 'final_norm', 'ssd_w_in', 'ssd_conv_w', 'ssd_conv_b', 'ssd_dt_bias', 'ssd_a_log', 'ssd_d', 'ssd_norm', 'ssd_w_out', 'sb_w_qkv', 'sb_w_out', 'ffn_w_in', 'ffn_conv_w', 'ffn_conv_b', 'ffn_w_out']
TWIN_DIFF_INPUT = 'x'
TWIN_INPUTS = ['x', 'mix_norm', 'ffn_norm', 'final_norm', 'ssd_w_in', 'ssd_conv_w', 'ssd_conv_b', 'ssd_dt_bias', 'ssd_a_log', 'ssd_d', 'ssd_norm', 'ssd_w_out', 'sb_w_qkv', 'sb_w_out', 'ffn_w_in', 'ffn_conv_w', 'ffn_conv_b', 'ffn_w_out', 'loss_target', 'm_mix_norm', 'm_ffn_norm', 'm_final_norm', 'm_ssd_w_in', 'm_ssd_conv_w', 'm_ssd_conv_b', 'm_ssd_dt_bias', 'm_ssd_a_log', 'm_ssd_d', 'm_ssd_norm', 'm_ssd_w_out', 'm_sb_w_qkv', 'm_sb_w_out', 'm_ffn_w_in', 'm_ffn_conv_w', 'm_ffn_conv_b', 'm_ffn_w_out', 'v_mix_norm', 'v_ffn_norm', 'v_final_norm', 'v_ssd_w_in', 'v_ssd_conv_w', 'v_ssd_conv_b', 'v_ssd_dt_bias', 'v_ssd_a_log', 'v_ssd_d', 'v_ssd_norm', 'v_ssd_w_out', 'v_sb_w_qkv', 'v_sb_w_out', 'v_ffn_w_in', 'v_ffn_conv_w', 'v_ffn_conv_b', 'v_ffn_w_out']
TWIN_OUTPUTS = ['loss', 'grad_x', 'grad_mix_norm', 'grad_ffn_norm', 'grad_final_norm', 'grad_ssd_w_in', 'grad_ssd_conv_w', 'grad_ssd_conv_b', 'grad_ssd_dt_bias', 'grad_ssd_a_log', 'grad_ssd_d', 'grad_ssd_norm', 'grad_ssd_w_out', 'grad_sb_w_qkv', 'grad_sb_w_out', 'grad_ffn_w_in', 'grad_ffn_conv_w', 'grad_ffn_conv_b', 'grad_ffn_w_out', 'delta_mix_norm', 'delta_ffn_norm', 'delta_final_norm', 'delta_ssd_w_in', 'delta_ssd_conv_w', 'delta_ssd_conv_b', 'delta_ssd_dt_bias', 'delta_ssd_a_log', 'delta_ssd_d', 'delta_ssd_norm', 'delta_ssd_w_out', 'delta_sb_w_qkv', 'delta_sb_w_out', 'delta_ffn_w_in', 'delta_ffn_conv_w', 'delta_ffn_conv_b', 'delta_ffn_w_out', 'new_m_mix_norm', 'new_m_ffn_norm', 'new_m_final_norm', 'new_m_ssd_w_in', 'new_m_ssd_conv_w', 'new_m_ssd_conv_b', 'new_m_ssd_dt_bias', 'new_m_ssd_a_log', 'new_m_ssd_d', 'new_m_ssd_norm', 'new_m_ssd_w_out', 'new_m_sb_w_qkv', 'new_m_sb_w_out', 'new_m_ffn_w_in', 'new_m_ffn_conv_w', 'new_m_ffn_conv_b', 'new_m_ffn_w_out', 'new_v_mix_norm', 'new_v_ffn_norm', 'new_v_final_norm', 'new_v_ssd_w_in', 'new_v_ssd_conv_w', 'new_v_ssd_conv_b', 'new_v_ssd_dt_bias', 'new_v_ssd_a_log', 'new_v_ssd_d', 'new_v_ssd_norm', 'new_v_ssd_w_out', 'new_v_sb_w_qkv', 'new_v_sb_w_out', 'new_v_ffn_w_in', 'new_v_ffn_conv_w', 'new_v_ffn_conv_b', 'new_v_ffn_w_out']
TWIN_LEAF_KINDS = {'loss': 'loss', 'grad_x': 'grad_x', 'grad_mix_norm': 'grad_w', 'grad_ffn_norm': 'grad_w', 'grad_final_norm': 'grad_w', 'grad_ssd_w_in': 'grad_w', 'grad_ssd_conv_w': 'grad_w', 'grad_ssd_conv_b': 'grad_w', 'grad_ssd_dt_bias': 'grad_w', 'grad_ssd_a_log': 'grad_w', 'grad_ssd_d': 'grad_w', 'grad_ssd_norm': 'grad_w', 'grad_ssd_w_out': 'grad_w', 'grad_sb_w_qkv': 'grad_w', 'grad_sb_w_out': 'grad_w', 'grad_ffn_w_in': 'grad_w', 'grad_ffn_conv_w': 'grad_w', 'grad_ffn_conv_b': 'grad_w', 'grad_ffn_w_out': 'grad_w', 'delta_mix_norm': 'delta_w', 'delta_ffn_norm': 'delta_w', 'delta_final_norm': 'delta_w', 'delta_ssd_w_in': 'delta_w', 'delta_ssd_conv_w': 'delta_w', 'delta_ssd_conv_b': 'delta_w', 'delta_ssd_dt_bias': 'delta_w', 'delta_ssd_a_log': 'delta_w', 'delta_ssd_d': 'delta_w', 'delta_ssd_norm': 'delta_w', 'delta_ssd_w_out': 'delta_w', 'delta_sb_w_qkv': 'delta_w', 'delta_sb_w_out': 'delta_w', 'delta_ffn_w_in': 'delta_w', 'delta_ffn_conv_w': 'delta_w', 'delta_ffn_conv_b': 'delta_w', 'delta_ffn_w_out': 'delta_w', 'new_m_mix_norm': 'new_m', 'new_m_ffn_norm': 'new_m', 'new_m_final_norm': 'new_m', 'new_m_ssd_w_in': 'new_m', 'new_m_ssd_conv_w': 'new_m', 'new_m_ssd_conv_b': 'new_m', 'new_m_ssd_dt_bias': 'new_m', 'new_m_ssd_a_log': 'new_m', 'new_m_ssd_d': 'new_m', 'new_m_ssd_norm': 'new_m', 'new_m_ssd_w_out': 'new_m', 'new_m_sb_w_qkv': 'new_m', 'new_m_sb_w_out': 'new_m', 'new_m_ffn_w_in': 'new_m', 'new_m_ffn_conv_w': 'new_m', 'new_m_ffn_conv_b': 'new_m', 'new_m_ffn_w_out': 'new_m', 'new_v_mix_norm': 'new_v', 'new_v_ffn_norm': 'new_v', 'new_v_final_norm': 'new_v', 'new_v_ssd_w_in': 'new_v', 'new_v_ssd_conv_w': 'new_v', 'new_v_ssd_conv_b': 'new_v', 'new_v_ssd_dt_bias': 'new_v', 'new_v_ssd_a_log': 'new_v', 'new_v_ssd_d': 'new_v', 'new_v_ssd_norm': 'new_v', 'new_v_ssd_w_out': 'new_v', 'new_v_sb_w_qkv': 'new_v', 'new_v_sb_w_out': 'new_v', 'new_v_ffn_w_in': 'new_v', 'new_v_ffn_conv_w': 'new_v', 'new_v_ffn_conv_b': 'new_v', 'new_v_ffn_w_out': 'new_v'}


def _forward(args):
    return _fwd_reference(*[args[k] for k in FWD_PARAMS])


def _output_shape():
    out = _jax.eval_shape(lambda: _forward(_fwd_setup_inputs(0)))
    return out.shape, out.dtype

N_MICROBATCH = 1
ADAM_LR = 0.001
ADAM_B1 = 0.9
ADAM_B2 = 0.999
ADAM_EPS = 1e-08
ADAM_WD = 0.01
ADAM_STEP = 10
PER_EXAMPLE_BATCH_AXIS = {'x': 0, 'loss_target': 0}
SHARED_INPUTS = []
_WEIGHT_DTYPES = {'mix_norm': _jnp.float32, 'ffn_norm': _jnp.float32, 'final_norm': _jnp.float32, 'ssd_w_in': _jnp.float32, 'ssd_conv_w': _jnp.float32, 'ssd_conv_b': _jnp.float32, 'ssd_dt_bias': _jnp.float32, 'ssd_a_log': _jnp.float32, 'ssd_d': _jnp.float32, 'ssd_norm': _jnp.float32, 'ssd_w_out': _jnp.float32, 'sb_w_qkv': _jnp.float32, 'sb_w_out': _jnp.float32, 'ffn_w_in': _jnp.float32, 'ffn_conv_w': _jnp.float32, 'ffn_conv_b': _jnp.float32, 'ffn_w_out': _jnp.float32}
MOMENT_SCALE = {'mix_norm': 7.301789e-02, 'ffn_norm': 5.119474e-02, 'final_norm': 3.198605e+01, 'ssd_w_in': 3.787479e-02, 'ssd_conv_w': 3.383311e-02, 'ssd_conv_b': 4.974052e-02, 'ssd_dt_bias': 1.216512e-01, 'ssd_a_log': 1.837912e-01, 'ssd_d': 2.747645e-01, 'ssd_norm': 4.757064e-02, 'ssd_w_out': 1.757450e-01, 'sb_w_qkv': 2.593151e-02, 'sb_w_out': 1.072591e-01, 'ffn_w_in': 2.156896e-02, 'ffn_conv_w': 2.164177e-02, 'ffn_conv_b': 2.164268e-02, 'ffn_w_out': 9.960636e-02}


def _to_microbatches(a, axis):
    t = _jnp.moveaxis(a, axis, 0)
    t = t.reshape((N_MICROBATCH, t.shape[0] // N_MICROBATCH) + t.shape[1:])
    return _jnp.moveaxis(t, 1, axis + 1)


def setup_inputs(seed: int = 0) -> dict:
    inp = _fwd_setup_inputs(seed)
    key = _jax.random.fold_in(_jax.random.key(seed), 7919)
    shape, _ = _output_shape()
    out = dict(inp)
    out["loss_target"] = _jax.random.normal(_jax.random.fold_in(key, 0), shape, _jnp.float32)
    for i, name in enumerate(TWIN_WEIGHTS):
        w = inp[name].astype(_jnp.float32)
        if MOMENT_SCALE is None:
            s = _jnp.sqrt(_jnp.mean(_jnp.square(w)) + 1e-30)
        else:
            s = MOMENT_SCALE[name]
        km, kv = _jax.random.split(_jax.random.fold_in(key, i + 1))
        out[name] = w
        out["m_" + name] = s * _jax.random.normal(km, w.shape, _jnp.float32)
        out["v_" + name] = (s * s) * _jax.random.uniform(kv, w.shape, _jnp.float32, 0.5, 1.5)
    if N_MICROBATCH > 1:
        for name, axis in PER_EXAMPLE_BATCH_AXIS.items():
            out[name] = _to_microbatches(out[name], axis)
    return {'x': out['x'], 'mix_norm': out['mix_norm'], 'ffn_norm': out['ffn_norm'], 'final_norm': out['final_norm'], 'ssd_w_in': out['ssd_w_in'], 'ssd_conv_w': out['ssd_conv_w'], 'ssd_conv_b': out['ssd_conv_b'], 'ssd_dt_bias': out['ssd_dt_bias'], 'ssd_a_log': out['ssd_a_log'], 'ssd_d': out['ssd_d'], 'ssd_norm': out['ssd_norm'], 'ssd_w_out': out['ssd_w_out'], 'sb_w_qkv': out['sb_w_qkv'], 'sb_w_out': out['sb_w_out'], 'ffn_w_in': out['ffn_w_in'], 'ffn_conv_w': out['ffn_conv_w'], 'ffn_conv_b': out['ffn_conv_b'], 'ffn_w_out': out['ffn_w_out'], 'loss_target': out['loss_target'], 'm_mix_norm': out['m_mix_norm'], 'm_ffn_norm': out['m_ffn_norm'], 'm_final_norm': out['m_final_norm'], 'm_ssd_w_in': out['m_ssd_w_in'], 'm_ssd_conv_w': out['m_ssd_conv_w'], 'm_ssd_conv_b': out['m_ssd_conv_b'], 'm_ssd_dt_bias': out['m_ssd_dt_bias'], 'm_ssd_a_log': out['m_ssd_a_log'], 'm_ssd_d': out['m_ssd_d'], 'm_ssd_norm': out['m_ssd_norm'], 'm_ssd_w_out': out['m_ssd_w_out'], 'm_sb_w_qkv': out['m_sb_w_qkv'], 'm_sb_w_out': out['m_sb_w_out'], 'm_ffn_w_in': out['m_ffn_w_in'], 'm_ffn_conv_w': out['m_ffn_conv_w'], 'm_ffn_conv_b': out['m_ffn_conv_b'], 'm_ffn_w_out': out['m_ffn_w_out'], 'v_mix_norm': out['v_mix_norm'], 'v_ffn_norm': out['v_ffn_norm'], 'v_final_norm': out['v_final_norm'], 'v_ssd_w_in': out['v_ssd_w_in'], 'v_ssd_conv_w': out['v_ssd_conv_w'], 'v_ssd_conv_b': out['v_ssd_conv_b'], 'v_ssd_dt_bias': out['v_ssd_dt_bias'], 'v_ssd_a_log': out['v_ssd_a_log'], 'v_ssd_d': out['v_ssd_d'], 'v_ssd_norm': out['v_ssd_norm'], 'v_ssd_w_out': out['v_ssd_w_out'], 'v_sb_w_qkv': out['v_sb_w_qkv'], 'v_sb_w_out': out['v_sb_w_out'], 'v_ffn_w_in': out['v_ffn_w_in'], 'v_ffn_conv_w': out['v_ffn_conv_w'], 'v_ffn_conv_b': out['v_ffn_conv_b'], 'v_ffn_w_out': out['v_ffn_w_out']}


def _loss(weights, diff, rest, loss_target):
    with _jax.named_scope("forward"):
        args = {**rest, TWIN_DIFF_INPUT: diff, **{k: w.astype(_WEIGHT_DTYPES[k]) for k, w in weights.items()}}
        y = _forward(args)
    with _jax.named_scope("loss_head"):
        err = _jnp.square(y.astype(_jnp.float32) - loss_target)
        return 0.5 * _jnp.sum(_jnp.mean(err, axis=-1)) if err.ndim else 0.5 * err


def _adamw(w, g, m, v):
    m = ADAM_B1 * m + (1.0 - ADAM_B1) * g
    v = ADAM_B2 * v + (1.0 - ADAM_B2) * _jnp.square(g)
    m_hat = m / (1.0 - ADAM_B1 ** ADAM_STEP)
    v_hat = v / (1.0 - ADAM_B2 ** ADAM_STEP)
    delta = -ADAM_LR * (m_hat / (_jnp.sqrt(v_hat) + ADAM_EPS) + ADAM_WD * w)
    return delta, m, v


def reference(x, mix_norm, ffn_norm, final_norm, ssd_w_in, ssd_conv_w, ssd_conv_b, ssd_dt_bias, ssd_a_log, ssd_d, ssd_norm, ssd_w_out, sb_w_qkv, sb_w_out, ffn_w_in, ffn_conv_w, ffn_conv_b, ffn_w_out, loss_target, m_mix_norm, m_ffn_norm, m_final_norm, m_ssd_w_in, m_ssd_conv_w, m_ssd_conv_b, m_ssd_dt_bias, m_ssd_a_log, m_ssd_d, m_ssd_norm, m_ssd_w_out, m_sb_w_qkv, m_sb_w_out, m_ffn_w_in, m_ffn_conv_w, m_ffn_conv_b, m_ffn_w_out, v_mix_norm, v_ffn_norm, v_final_norm, v_ssd_w_in, v_ssd_conv_w, v_ssd_conv_b, v_ssd_dt_bias, v_ssd_a_log, v_ssd_d, v_ssd_norm, v_ssd_w_out, v_sb_w_qkv, v_sb_w_out, v_ffn_w_in, v_ffn_conv_w, v_ffn_conv_b, v_ffn_w_out):
    given = dict(x=x, mix_norm=mix_norm, ffn_norm=ffn_norm, final_norm=final_norm, ssd_w_in=ssd_w_in, ssd_conv_w=ssd_conv_w, ssd_conv_b=ssd_conv_b, ssd_dt_bias=ssd_dt_bias, ssd_a_log=ssd_a_log, ssd_d=ssd_d, ssd_norm=ssd_norm, ssd_w_out=ssd_w_out, sb_w_qkv=sb_w_qkv, sb_w_out=sb_w_out, ffn_w_in=ffn_w_in, ffn_conv_w=ffn_conv_w, ffn_conv_b=ffn_conv_b, ffn_w_out=ffn_w_out, loss_target=loss_target, m_mix_norm=m_mix_norm, m_ffn_norm=m_ffn_norm, m_final_norm=m_final_norm, m_ssd_w_in=m_ssd_w_in, m_ssd_conv_w=m_ssd_conv_w, m_ssd_conv_b=m_ssd_conv_b, m_ssd_dt_bias=m_ssd_dt_bias, m_ssd_a_log=m_ssd_a_log, m_ssd_d=m_ssd_d, m_ssd_norm=m_ssd_norm, m_ssd_w_out=m_ssd_w_out, m_sb_w_qkv=m_sb_w_qkv, m_sb_w_out=m_sb_w_out, m_ffn_w_in=m_ffn_w_in, m_ffn_conv_w=m_ffn_conv_w, m_ffn_conv_b=m_ffn_conv_b, m_ffn_w_out=m_ffn_w_out, v_mix_norm=v_mix_norm, v_ffn_norm=v_ffn_norm, v_final_norm=v_final_norm, v_ssd_w_in=v_ssd_w_in, v_ssd_conv_w=v_ssd_conv_w, v_ssd_conv_b=v_ssd_conv_b, v_ssd_dt_bias=v_ssd_dt_bias, v_ssd_a_log=v_ssd_a_log, v_ssd_d=v_ssd_d, v_ssd_norm=v_ssd_norm, v_ssd_w_out=v_ssd_w_out, v_sb_w_qkv=v_sb_w_qkv, v_sb_w_out=v_sb_w_out, v_ffn_w_in=v_ffn_w_in, v_ffn_conv_w=v_ffn_conv_w, v_ffn_conv_b=v_ffn_conv_b, v_ffn_w_out=v_ffn_w_out)
    weights = {n: given[n] for n in TWIN_WEIGHTS}
    shared = {n: given[n] for n in SHARED_INPUTS}
    per_example = {n: given[n] for n in ['x']}
    grad_fn = _jax.value_and_grad(_loss, argnums=(0, 1))

    def one_microbatch(ex, loss_target):
        ex = dict(ex)
        diff = ex.pop(TWIN_DIFF_INPUT)
        return grad_fn(weights, diff, {**shared, **ex}, loss_target)

    if N_MICROBATCH == 1:
        loss, (grad_w, grad_x) = one_microbatch(per_example, given["loss_target"])
    else:
        def body(carry, xs):
            loss_sum, grad_sum = carry
            l_k, (gw_k, gx_k) = one_microbatch(xs[0], xs[1])
            with _jax.named_scope("update"):
                return (loss_sum + l_k, _jax.tree.map(_jnp.add, grad_sum, gw_k)), gx_k

        init = (_jnp.zeros((), _jnp.float32), _jax.tree.map(_jnp.zeros_like, weights))
        (loss, grad_w), grad_x = _jax.lax.scan(body, init, (per_example, given["loss_target"]))
    with _jax.named_scope("update"):
        delta_w, new_m, new_v = {}, {}, {}
        for n in TWIN_WEIGHTS:
            delta_w[n], new_m[n], new_v[n] = _adamw(weights[n], grad_w[n], given["m_" + n], given["v_" + n])
    return (loss, grad_x, *[grad_w[n] for n in TWIN_WEIGHTS], *[delta_w[n] for n in TWIN_WEIGHTS],
            *[new_m[n] for n in TWIN_WEIGHTS], *[new_v[n] for n in TWIN_WEIGHTS])
```

```python
import functools

import jax
import jax.numpy as jnp
from jax import lax
from jax.experimental import pallas as pl
from jax.experimental.pallas import tpu as pltpu

F32 = jnp.float32
MXU_DTYPE = jnp.bfloat16
HIGHEST = lax.Precision.HIGHEST

D_MODEL = 1024
DEPTH = 4
NORM_EPS = 1e-6
SSD_D_INNER = 2048
SSD_HEAD_DIM = 64
SSD_HEADS = 32
SSD_GROUPS = 8
SSD_STATE = 128
SSD_CONV = 4
SSD_CHUNK = 128
SSD_CONV_DIM = 4096
SSD_IN_DIM = 6176
SSD_IN_PAD = 6272
SB_HEADS = 16
SB_HEAD_DIM = 64
FFN_D_FF = 2816
FFN_CONV = 3
ADAM_LR, ADAM_B1, ADAM_B2, ADAM_EPS, ADAM_WD, ADAM_STEP = 0.001, 0.9, 0.999, 1e-08, 0.01, 10

LANES = 128
SUBLANES = 8
VMEM_LIMIT = 56 * 1024 * 1024
MESH = pl.DeviceIdType.MESH


def _params(sem=None):
    return pltpu.CompilerParams(dimension_semantics=sem, vmem_limit_bytes=VMEM_LIMIT)


def _sds(shape, dtype):
    return jax.ShapeDtypeStruct(shape, dtype)


def _dot(a, b, dims=(((1,), (0,)), ((), ())), precision=None):
    return lax.dot_general(a, b, dims, precision=precision, preferred_element_type=F32)


_NN = (((1,), (0,)), ((), ()))
_NT = (((1,), (1,)), ((), ()))
_TN = (((0,), (0,)), ((), ()))


def _mx(a):
    return a.astype(MXU_DTYPE)


def _silu(x):
    return x * (1.0 / (1.0 + jnp.exp(-x)))


def _silu_and_grad(x):
    s = 1.0 / (1.0 + jnp.exp(-x))
    return x * s, s * (1.0 + x * (1.0 - s))


def _pick(n, cands):
    for c in cands:
        if n % c == 0:
            return c
    return n


def rms_fwd(x, g):
    s, d = x.shape
    ts = _pick(s, (512, 256, 128))

    def body(x_ref, g_ref, h_ref, r_ref):
        xv = x_ref[...]
        r = lax.rsqrt(jnp.mean(xv * xv, axis=-1, keepdims=True) + NORM_EPS)
        h_ref[...] = (xv * r * g_ref[...]).astype(h_ref.dtype)
        r_ref[...] = r

    return pl.pallas_call(
        body, name="rms_fwd", grid=(s // ts,),
        in_specs=[pl.BlockSpec((ts, d), lambda i: (i, 0)), pl.BlockSpec((1, d), lambda i: (0, 0))],
        out_specs=[pl.BlockSpec((ts, d), lambda i: (i, 0)), pl.BlockSpec((ts, 1), lambda i: (i, 0))],
        out_shape=[_sds((s, d), MXU_DTYPE), _sds((s, 1), F32)],
        compiler_params=_params(("parallel",)),
    )(x, g)


def rms_bwd(x, r, g, dh, dres):
    s, d = x.shape
    ts = _pick(s, (512, 256, 128))

    def body(x_ref, r_ref, g_ref, dh_ref, dres_ref, dx_ref, dg_ref):
        xh = x_ref[...] * r_ref[...]
        dhv = dh_ref[...]
        dxh = dhv * g_ref[...]
        dx_ref[...] = dres_ref[...] + r_ref[...] * (dxh - xh * jnp.mean(dxh * xh, axis=-1, keepdims=True))
        part = jnp.sum(dhv * xh, axis=0, keepdims=True)

        @pl.when(pl.program_id(0) == 0)
        def _():
            dg_ref[...] = part

        @pl.when(pl.program_id(0) != 0)
        def _():
            dg_ref[...] += part

    row = pl.BlockSpec((ts, d), lambda i: (i, 0))
    return pl.pallas_call(
        body, name="rms_bwd", grid=(s // ts,),
        in_specs=[row, pl.BlockSpec((ts, 1), lambda i: (i, 0)), pl.BlockSpec((1, d), lambda i: (0, 0)), row, row],
        out_specs=[row, pl.BlockSpec((1, d), lambda i: (0, 0))],
        out_shape=[_sds((s, d), F32), _sds((1, d), F32)],
        compiler_params=_params(("arbitrary",)),
    )(x, r, g, dh, dres)


def mm(a, b, mode="nn", res=None, out_dtype=F32, tm=None, tn=None, name="mm"):
    if mode == "nn":
        (m, k), (_, n) = a.shape, b.shape
    elif mode == "nt":
        (m, k), (n, _) = a.shape, b.shape
    else:
        (k, m), (_, n) = a.shape, b.shape
    tm = tm or _pick(m, (1024, 512, 256, 128))
    tn = tn or _pick(n, (512, 896, 256, 128))
    dims = {"nn": _NN, "nt": _NT, "tn": _TN}[mode]

    def body(*refs):
        a_ref, b_ref = refs[0], refs[1]
        o_ref = refs[-1]
        acc = _dot(_mx(a_ref[...]), _mx(b_ref[...]), dims)
        if res is not None:
            acc = acc + refs[2][...]
        o_ref[...] = acc.astype(o_ref.dtype)

    a_spec = pl.BlockSpec((k, tm), lambda i, j: (0, i)) if mode == "tn" else pl.BlockSpec((tm, k), lambda i, j: (i, 0))
    b_spec = pl.BlockSpec((tn, k), lambda i, j: (j, 0)) if mode == "nt" else pl.BlockSpec((k, tn), lambda i, j: (0, j))
    o_spec = pl.BlockSpec((tm, tn), lambda i, j: (i, j))
    ins, specs = [a, b], [a_spec, b_spec]
    if res is not None:
        ins.append(res)
        specs.append(o_spec)
    return pl.pallas_call(
        body, name=name, grid=(m // tm, n // tn), in_specs=specs, out_specs=o_spec,
        out_shape=_sds((m, n), out_dtype), compiler_params=_params(("parallel", "parallel")),
    )(*ins)


CONV_ROWS = 256
CONV_COLS = 128


def _row_iota8(cols):
    return lax.broadcasted_iota(jnp.int32, (SUBLANES, cols), 0)


def _shift_down(cur, prev8, k):
    if k == 0:
        return cur
    rolled = pltpu.roll(cur, k, 0)
    head = jnp.where(_row_iota8(cur.shape[1]) < k, pltpu.roll(prev8, k, 0), rolled[0:SUBLANES])
    return jnp.concatenate([head, rolled[SUBLANES:]], axis=0)


def _shift_up(cur, next8, k):
    if k == 0:
        return cur
    n = cur.shape[0]
    rolled = pltpu.roll(cur, n - k, 0)
    tail = jnp.where(_row_iota8(cur.shape[1]) >= SUBLANES - k, pltpu.roll(next8, SUBLANES - k, 0), rolled[n - SUBLANES:])
    return jnp.concatenate([rolled[:n - SUBLANES], tail], axis=0)


def _load_prev8(ref, i, rows):
    start = pl.multiple_of(jnp.maximum(i * rows - SUBLANES, 0), SUBLANES)
    p = ref[pl.ds(start, SUBLANES), :]
    return jnp.where(i > 0, p, jnp.zeros_like(p))


def _conv_rows(ref, w_ref, b_ref, i, rows, width):
    cur = ref[pl.ds(pl.multiple_of(i * rows, rows), rows), :]
    prev8 = _load_prev8(ref, i, rows)
    shifted = [_shift_down(cur, prev8, k) for k in range(width)]
    acc = b_ref[...] + w_ref[width - 1:width, :] * shifted[0]
    for k in range(1, width):
        acc = acc + w_ref[width - 1 - k:width - k, :] * shifted[k]
    return acc, shifted


def _conv_bwd_rows(du, next8, w_ref, width):
    acc = w_ref[width - 1:width, :] * du
    for k in range(1, width):
        acc = acc + w_ref[width - 1 - k:width - k, :] * _shift_up(du, next8, k)
    return acc


def ffn_mid_fwd(u0, cw, cb):
    s, f2 = u0.shape
    f = f2 // 2
    nt = f // CONV_COLS
    rows = min(CONV_ROWS, s)

    def body(ug_ref, uu_ref, wg_ref, wu_ref, bg_ref, bu_ref, a_ref):
        def step(i, carry):
            g, _ = _conv_rows(ug_ref, wg_ref, bg_ref, i, rows, FFN_CONV)
            u, _ = _conv_rows(uu_ref, wu_ref, bu_ref, i, rows, FFN_CONV)
            a_ref[pl.ds(pl.multiple_of(i * rows, rows), rows), :] = (_silu(g) * u).astype(a_ref.dtype)
            return carry

        lax.fori_loop(0, s // rows, step, 0)

    col = lambda off: pl.BlockSpec((s, CONV_COLS), lambda j: (0, j + off))
    wsp = lambda r, off: pl.BlockSpec((r, CONV_COLS), lambda j: (0, j + off))
    return pl.pallas_call(
        body, name="ffn_mid_fwd", grid=(nt,),
        in_specs=[col(0), col(nt), wsp(FFN_CONV, 0), wsp(FFN_CONV, nt), wsp(1, 0), wsp(1, nt)],
        out_specs=pl.BlockSpec((s, CONV_COLS), lambda j: (0, j)),
        out_shape=_sds((s, f), MXU_DTYPE), compiler_params=_params(("parallel",)),
    )(u0, u0, cw, cw, cb, cb)


def ffn_mid_bwd(u0, da, cw, cb):
    s, f2 = u0.shape
    f = f2 // 2
    nt = f // CONV_COLS
    rows = min(CONV_ROWS, s)
    nsteps = s // rows
    w = FFN_CONV

    def body(ug_ref, uu_ref, da_ref, wg_ref, wu_ref, bg_ref, bu_ref,
             dug_ref, duu_ref, dwg_ref, dwu_ref, dbg_ref, dbu_ref):
        zero8 = jnp.zeros((SUBLANES, CONV_COLS), F32)
        zrow = jnp.zeros((1, CONV_COLS), F32)

        def step(it, carry):
            ng, nu, accs = carry
            i = nsteps - 1 - it
            r0 = pl.multiple_of(i * rows, rows)
            g, sg = _conv_rows(ug_ref, wg_ref, bg_ref, i, rows, w)
            u, su = _conv_rows(uu_ref, wu_ref, bu_ref, i, rows, w)
            dav = da_ref[pl.ds(r0, rows), :]
            sg_val, sg_grad = _silu_and_grad(g)
            dg = dav * u * sg_grad
            du = dav * sg_val
            dug_ref[pl.ds(r0, rows), :] = _conv_bwd_rows(dg, ng, wg_ref, w).astype(dug_ref.dtype)
            duu_ref[pl.ds(r0, rows), :] = _conv_bwd_rows(du, nu, wu_ref, w).astype(duu_ref.dtype)
            new = []
            for j in range(w):
                new.append(accs[j] + jnp.sum(dg * sg[w - 1 - j], axis=0, keepdims=True))
            for j in range(w):
                new.append(accs[w + j] + jnp.sum(du * su[w - 1 - j], axis=0, keepdims=True))
            new.append(accs[2 * w] + jnp.sum(dg, axis=0, keepdims=True))
            new.append(accs[2 * w + 1] + jnp.sum(du, axis=0, keepdims=True))
            return dg[0:SUBLANES], du[0:SUBLANES], tuple(new)

        _, _, accs = lax.fori_loop(0, nsteps, step, (zero8, zero8, tuple([zrow] * (2 * w + 2))))
        dwg_ref[...] = jnp.concatenate(accs[0:w], axis=0)
        dwu_ref[...] = jnp.concatenate(accs[w:2 * w], axis=0)
        dbg_ref[...] = accs[2 * w]
        dbu_ref[...] = accs[2 * w + 1]

    col = lambda off: pl.BlockSpec((s, CONV_COLS), lambda j: (0, j + off))
    wsp = lambda r, off: pl.BlockSpec((r, CONV_COLS), lambda j: (0, j + off))
    outs = pl.pallas_call(
        body, name="ffn_mid_bwd", grid=(nt,),
        in_specs=[col(0), col(nt), col(0), wsp(w, 0), wsp(w, nt), wsp(1, 0), wsp(1, nt)],
        out_specs=[col(0), col(0), wsp(w, 0), wsp(w, 0), wsp(1, 0), wsp(1, 0)],
        out_shape=[_sds((s, f), MXU_DTYPE), _sds((s, f), MXU_DTYPE), _sds((w, f), F32), _sds((w, f), F32),
                   _sds((1, f), F32), _sds((1, f), F32)],
        compiler_params=_params(("parallel",)),
    )(u0, u0, da, cw, cw, cb, cb)
    dug, duu, dwg, dwu, dbg, dbu = outs
    return dug, duu, jnp.concatenate([dwg, dwu], axis=1), jnp.concatenate([dbg, dbu], axis=1)


SB_BLOCK = 256


def _split_hi_lo(x):
    hi = x.astype(MXU_DTYPE)
    lo = (x - hi.astype(F32)).astype(MXU_DTYPE)
    return hi, lo


def _dot_exact01(x, tri):
    hi, lo = _split_hi_lo(x)
    return _dot(hi, tri) + _dot(lo, tri)


def _sb_tile(q, k, qi, kb, blk):
    z = _dot(q, k, _NT) * (SB_HEAD_DIM ** -0.5)
    rows = qi * blk + lax.broadcasted_iota(jnp.int32, (blk, blk), 0)
    cols = kb * blk + lax.broadcasted_iota(jnp.int32, (blk, blk), 1)
    strict = cols < rows
    t = jnp.log(1.0 + jnp.exp(-jnp.abs(z)))
    lb = jnp.minimum(z, 0.0) - t
    lf = jnp.where(strict, jnp.minimum(-z, 0.0) - t, 0.0)
    return lb, lf, strict


def _tri(blk, upper):
    r = lax.broadcasted_iota(jnp.int32, (blk, blk), 0)
    c = lax.broadcasted_iota(jnp.int32, (blk, blk), 1)
    return jnp.where((r > c) if upper else (r < c), 1.0, 0.0).astype(MXU_DTYPE)


def sb_fwd(qkv):
    s = qkv.shape[0]
    blk = min(SB_BLOCK, s)
    nblk = s // blk
    npair = SB_HEADS // 2
    dh = SB_HEAD_DIM

    def body(q_ref, k_ref, v_ref, o_ref):
        suffix_tri = _tri(blk, True)

        for head in range(2):
            sl = slice(head * dh, (head + 1) * dh)

            def qstep(qi, carry):
                q = q_ref[pl.ds(pl.multiple_of(qi * blk, blk), blk), sl]

                def kstep(it, st):
                    run, acc = st
                    kb = qi - it
                    k0 = pl.multiple_of(kb * blk, blk)
                    lb, lf, strict = _sb_tile(q, k_ref[pl.ds(k0, blk), sl], qi, kb, blk)
                    sloc = _dot_exact01(lf, suffix_tri)
                    a = jnp.where(strict, jnp.exp(lb + sloc + run), 0.0)
                    acc = acc + _dot(_mx(a), v_ref[pl.ds(k0, blk), sl])
                    return run + sloc[:, 0:1] + lf[:, 0:1], acc

                _, acc = lax.fori_loop(0, qi + 1, kstep, (jnp.zeros((blk, 1), F32), jnp.zeros((blk, dh), F32)))
                o_ref[pl.ds(pl.multiple_of(qi * blk, blk), blk), sl] = acc.astype(o_ref.dtype)
                return carry

            lax.fori_loop(0, nblk, qstep, 0)

    col = lambda off: pl.BlockSpec((s, 2 * dh), lambda p: (0, p + off))
    return pl.pallas_call(
        body, name="sb_fwd", grid=(npair,), in_specs=[col(0), col(npair), col(2 * npair)], out_specs=col(0),
        out_shape=_sds((s, D_MODEL), MXU_DTYPE), compiler_params=_params(("parallel",)),
    )(qkv, qkv, qkv)


def sb_bwd(qkv, do):
    s = qkv.shape[0]
    blk = min(SB_BLOCK, s)
    nblk = s // blk
    npair = SB_HEADS // 2
    dh = SB_HEAD_DIM

    def body(q_ref, k_ref, v_ref, do_ref, dq_ref, dk_ref, dv_ref, dk_acc, dv_acc, run_ref):
        suffix_tri = _tri(blk, True)
        prefix_tri = _tri(blk, False)
        dk_acc[...] = jnp.zeros_like(dk_acc)
        dv_acc[...] = jnp.zeros_like(dv_acc)

        for head in range(2):
            sl = slice(head * dh, (head + 1) * dh)

            def qstep(qi, carry):
                q0 = pl.multiple_of(qi * blk, blk)
                q = q_ref[pl.ds(q0, blk), sl]
                dov = do_ref[pl.ds(q0, blk), sl]

                def sweep1(it, run):
                    kb = qi - it
                    run_ref[kb] = run
                    _, lf, _ = _sb_tile(q, k_ref[pl.ds(pl.multiple_of(kb * blk, blk), blk), sl], qi, kb, blk)
                    return run + jnp.sum(lf, axis=1, keepdims=True)

                lax.fori_loop(0, qi + 1, sweep1, jnp.zeros((blk, 1), F32))

                def sweep2(kb, st):
                    pg, dq = st
                    k0 = pl.multiple_of(kb * blk, blk)
                    kv = k_ref[pl.ds(k0, blk), sl]
                    lb, lf, strict = _sb_tile(q, kv, qi, kb, blk)
                    sloc = _dot_exact01(lf, suffix_tri)
                    a = jnp.where(strict, jnp.exp(lb + sloc + run_ref[kb]), 0.0)
                    g = _dot(dov, v_ref[pl.ds(k0, blk), sl], _NT) * a
                    p = pg + _dot_exact01(g, prefix_tri)
                    sig = jnp.exp(lb)
                    dz = _mx(jnp.where(strict, g * (1.0 - sig) - p * sig, 0.0) * (dh ** -0.5))
                    dk_acc[pl.ds(k0, blk), sl] += _dot(dz, q, _TN)
                    dv_acc[pl.ds(k0, blk), sl] += _dot(_mx(a), dov, _TN)
                    return pg + jnp.sum(g, axis=1, keepdims=True), dq + _dot(dz, kv)

                _, dq = lax.fori_loop(0, qi + 1, sweep2, (jnp.zeros((blk, 1), F32), jnp.zeros((blk, dh), F32)))
                dq_ref[pl.ds(q0, blk), sl] = dq.astype(dq_ref.dtype)
                return carry

            lax.fori_loop(0, nblk, qstep, 0)

        dk_ref[...] = dk_acc[...].astype(dk_ref.dtype)
        dv_ref[...] = dv_acc[...].astype(dv_ref.dtype)

    col = lambda off: pl.BlockSpec((s, 2 * dh), lambda p: (0, p + off))
    dq, dk, dv = pl.pallas_call(
        body, name="sb_bwd", grid=(npair,), in_specs=[col(0), col(npair), col(2 * npair), col(0)],
        out_specs=[col(0), col(0), col(0)], out_shape=[_sds((s, D_MODEL), MXU_DTYPE)] * 3,
        scratch_shapes=[pltpu.VMEM((s, 2 * dh), F32), pltpu.VMEM((s, 2 * dh), F32), pltpu.VMEM((nblk, blk, 1), F32)],
        compiler_params=_params(("parallel",)),
    )(qkv, qkv, qkv, do)
    return jnp.concatenate([dq, dk, dv], axis=1)


SSD_XBC_TILE0 = SSD_D_INNER // CONV_COLS


def ssd_conv_fwd(proj, cw, cb):
    s = proj.shape[0]
    rows = min(CONV_ROWS, s)

    def body(u_ref, w_ref, b_ref, o_ref):
        def step(i, carry):
            u, _ = _conv_rows(u_ref, w_ref, b_ref, i, rows, SSD_CONV)
            o_ref[pl.ds(pl.multiple_of(i * rows, rows), rows), :] = _silu(u)
            return carry

        lax.fori_loop(0, s // rows, step, 0)

    return pl.pallas_call(
        body, name="ssd_conv_fwd", grid=(SSD_CONV_DIM // CONV_COLS,),
        in_specs=[pl.BlockSpec((s, CONV_COLS), lambda j: (0, j + SSD_XBC_TILE0)),
                  pl.BlockSpec((SSD_CONV, CONV_COLS), lambda j: (0, j)), pl.BlockSpec((1, CONV_COLS), lambda j: (0, j))],
        out_specs=pl.BlockSpec((s, CONV_COLS), lambda j: (0, j)),
        out_shape=_sds((s, SSD_CONV_DIM), F32), compiler_params=_params(("parallel",)),
    )(proj, cw, cb)


def ssd_conv_bwd(proj, dact, cw, cb):
    s = proj.shape[0]
    rows = min(CONV_ROWS, s)
    nsteps = s // rows
    w = SSD_CONV

    def body(u_ref, da_ref, w_ref, b_ref, du_ref, dw_ref, db_ref):
        def step(it, carry):
            nxt, accs = carry
            i = nsteps - 1 - it
            r0 = pl.multiple_of(i * rows, rows)
            u, sh = _conv_rows(u_ref, w_ref, b_ref, i, rows, w)
            dconv = da_ref[pl.ds(r0, rows), :] * _silu_and_grad(u)[1]
            du_ref[pl.ds(r0, rows), :] = _conv_bwd_rows(dconv, nxt, w_ref, w).astype(du_ref.dtype)
            new = [accs[j] + jnp.sum(dconv * sh[w - 1 - j], axis=0, keepdims=True) for j in range(w)]
            new.append(accs[w] + jnp.sum(dconv, axis=0, keepdims=True))
            return dconv[0:SUBLANES], tuple(new)

        zrow = jnp.zeros((1, CONV_COLS), F32)
        _, accs = lax.fori_loop(0, nsteps, step, (jnp.zeros((SUBLANES, CONV_COLS), F32), tuple([zrow] * (w + 1))))
        dw_ref[...] = jnp.concatenate(accs[0:w], axis=0)
        db_ref[...] = accs[w]

    col = pl.BlockSpec((s, CONV_COLS), lambda j: (0, j))
    return pl.pallas_call(
        body, name="ssd_conv_bwd", grid=(SSD_CONV_DIM // CONV_COLS,),
        in_specs=[pl.BlockSpec((s, CONV_COLS), lambda j: (0, j + SSD_XBC_TILE0)), col,
                  pl.BlockSpec((w, CONV_COLS), lambda j: (0, j)), pl.BlockSpec((1, CONV_COLS), lambda j: (0, j))],
        out_specs=[col, pl.BlockSpec((w, CONV_COLS), lambda j: (0, j)), pl.BlockSpec((1, CONV_COLS), lambda j: (0, j))],
        out_shape=[_sds((s, SSD_CONV_DIM), MXU_DTYPE), _sds((w, SSD_CONV_DIM), F32), _sds((1, SSD_CONV_DIM), F32)],
        compiler_params=_params(("parallel",)),
    )(proj, dact, cw, cb)


def _split3(x):
    hi = x.astype(MXU_DTYPE)
    r1 = x - hi.astype(F32)
    mid = r1.astype(MXU_DTYPE)
    lo = (r1 - mid.astype(F32)).astype(MXU_DTYPE)
    return hi, mid, lo


def _dot01(x, m, dims=_NN, left=False):
    parts = _split3(x)
    if left:
        return _dot(m, parts[0], dims) + _dot(m, parts[1], dims) + _dot(m, parts[2], dims)
    return _dot(parts[0], m, dims) + _dot(parts[1], m, dims) + _dot(parts[2], m, dims)


def _softplus(x):
    return jnp.maximum(x, 0.0) + jnp.log1p(jnp.exp(-jnp.abs(x)))


def _ssd_consts(dt_bias, a_log, d_skip):
    pad = lambda v: jnp.pad(v.reshape(1, SSD_HEADS), ((0, 0), (0, LANES - SSD_HEADS)))
    head_of = jnp.arange(SSD_D_INNER) // SSD_HEAD_DIM
    expand = (jnp.arange(LANES)[:, None] == head_of[None, :]).astype(MXU_DTYPE)
    return dict(bias_w=pad(dt_bias), alog_w=pad(a_log), bias_c=dt_bias.reshape(SSD_HEADS, 1),
                alog_c=a_log.reshape(SSD_HEADS, 1), dskip=jnp.repeat(d_skip, SSD_HEAD_DIM).reshape(1, SSD_D_INNER),
                expand=expand, reduce=expand.T)


def _ssd_chunk_prep(dtp, dtp_t, bias_w, alog_w, bias_c, alog_c, expand):
    L = dtp.shape[0]
    r = lax.broadcasted_iota(jnp.int32, (L, L), 0)
    c = lax.broadcasted_iota(jnp.int32, (L, L), 1)
    tril = r >= c
    lower = jnp.where(tril, 1.0, 0.0).astype(MXU_DTYPE)
    upper = jnp.where(r <= c, 1.0, 0.0).astype(MXU_DTYPE)
    dt_col = _softplus(dtp + bias_w)
    a_col = -jnp.exp(alog_w) * dt_col
    a_row = -jnp.exp(alog_c) * _softplus(dtp_t + bias_c)
    acum_col = _dot01(a_col, lower, left=True)
    acum_row = _dot01(a_row, upper)
    acum_full = _dot01(acum_col, expand)
    dt_full = _dot01(dt_col, expand)
    return dict(tril=tril, lower=lower, upper=upper, dt_col=dt_col, a_col=a_col, acum_col=acum_col,
                acum_row=acum_row, acum_full=acum_full, dt_full=dt_full)


def _head_mask(j):
    lane = lax.broadcasted_iota(jnp.int32, (1, LANES), 1)
    return jnp.where((lane // SSD_HEAD_DIM) == j, 1.0, 0.0)


def _decay(pre, h):
    seg = pre["acum_col"][:, h:h + 1] - pre["acum_row"][h:h + 1, :]
    return jnp.exp(jnp.where(pre["tril"], seg, -1e30))


def _ssd_specs(s, nc, rev):
    L = SSD_CHUNK
    ci = (lambda i: nc - 1 - i) if rev else (lambda i: i)
    const = lambda shape: pl.BlockSpec(shape, lambda i: (0,) * len(shape))
    return dict(
        xbc=pl.BlockSpec((L, SSD_CONV_DIM), lambda i: (ci(i), 0)),
        dtp=pl.BlockSpec((L, LANES), lambda i: (ci(i), SSD_IN_PAD // LANES - 1)),
        dtp_t=pl.BlockSpec((SSD_HEADS, L), lambda i: (0, ci(i))),
        rows=pl.BlockSpec((L, SSD_D_INNER), lambda i: (ci(i), 0)),
        state=pl.BlockSpec((1, SSD_GROUPS, SSD_STATE, 4 * SSD_HEAD_DIM), lambda i: (ci(i), 0, 0, 0)),
        consts=[const((1, LANES)), const((1, LANES)), const((SSD_HEADS, 1)), const((SSD_HEADS, 1)),
                const((1, SSD_D_INNER)), const((LANES, SSD_D_INNER)), const((SSD_D_INNER, LANES))],
    )


def _const_args(cs):
    return [cs["bias_w"], cs["alog_w"], cs["bias_c"], cs["alog_c"], cs["dskip"], cs["expand"], cs["reduce"]]


def ssd_scan_fwd(act, proj, dtp_t, cs):
    s = act.shape[0]
    L = SSD_CHUNK
    nc = s // L
    G, N, GW = SSD_GROUPS, SSD_STATE, 4 * SSD_HEAD_DIM

    def body(act_ref, dtp_ref, dtpt_ref, bw_ref, aw_ref, bc_ref, ac_ref, dsk_ref, ex_ref, rd_ref, y_ref, st_out, st):
        @pl.when(pl.program_id(0) == 0)
        def _():
            st[...] = jnp.zeros_like(st)

        st_out[0] = st[...]
        pre = _ssd_chunk_prep(dtp_ref[...], dtpt_ref[...], bw_ref[...], aw_ref[...], bc_ref[...], ac_ref[...], ex_ref[...])
        acum_full = pre["acum_full"]
        last_full = acum_full[L - 1:L, :]
        for g in range(G):
            bg = _mx(act_ref[:, SSD_D_INNER + g * N:SSD_D_INNER + (g + 1) * N])
            cg = _mx(act_ref[:, SSD_D_INNER + G * N + g * N:SSD_D_INNER + G * N + (g + 1) * N])
            cb = _dot(cg, bg, _NT)
            for half in range(2):
                p = 2 * g + half
                cols = slice(p * LANES, (p + 1) * LANES)
                xs = act_ref[:, cols]
                xdt = xs * pre["dt_full"][:, cols]
                yd = jnp.zeros((L, LANES), F32)
                for j in range(2):
                    m = cb * _decay(pre, 2 * p + j)
                    yd = yd + _dot(_mx(m), _mx(xdt * _head_mask(j)))
                yoff = _dot(cg, _mx(st[g, :, half * LANES:(half + 1) * LANES])) * jnp.exp(acum_full[:, cols])
                y_ref[:, cols] = yd + yoff + dsk_ref[:, cols] * xs
                w = jnp.exp(last_full[:, cols] - acum_full[:, cols])
                st[g, :, half * LANES:(half + 1) * LANES] = (
                    st[g, :, half * LANES:(half + 1) * LANES] * jnp.exp(last_full[:, cols]) + _dot(bg, _mx(xdt * w), _TN))

    sp = _ssd_specs(s, nc, False)
    return pl.pallas_call(
        body, name="ssd_scan_fwd", grid=(nc,),
        in_specs=[sp["xbc"], sp["dtp"], sp["dtp_t"]] + sp["consts"],
        out_specs=[sp["rows"], sp["state"]],
        out_shape=[_sds((s, SSD_D_INNER), F32), _sds((nc, G, N, GW), F32)],
        scratch_shapes=[pltpu.VMEM((G, N, GW), F32)],
        compiler_params=_params(("arbitrary",)),
    )(act, proj, dtp_t, *_const_args(cs))


def ssd_scan_bwd(act, proj, dtp_t, cs, states, dy):
    s = act.shape[0]
    L = SSD_CHUNK
    nc = s // L
    G, N, GW = SSD_GROUPS, SSD_STATE, 4 * SSD_HEAD_DIM

    def body(act_ref, dtp_ref, dtpt_ref, bw_ref, aw_ref, bc_ref, ac_ref, dsk_ref, ex_ref, rd_ref, st_ref, dy_ref,
             dact_ref, ddtp_ref, dalog_ref, dbias_ref, dskip_ref, dst, dxdt_ref, dac_ref):
        first = pl.program_id(0) == 0

        @pl.when(first)
        def _():
            dst[...] = jnp.zeros_like(dst)
            dalog_ref[...] = jnp.zeros_like(dalog_ref)
            dbias_ref[...] = jnp.zeros_like(dbias_ref)
            dskip_ref[...] = jnp.zeros_like(dskip_ref)

        expand, reduce = ex_ref[...], rd_ref[...]
        pre = _ssd_chunk_prep(dtp_ref[...], dtpt_ref[...], bw_ref[...], aw_ref[...], bc_ref[...], ac_ref[...], expand)
        acum_full = pre["acum_full"]
        last_full = acum_full[L - 1:L, :]
        ones = jnp.ones((L, LANES), MXU_DTYPE)
        lane = lax.broadcasted_iota(jnp.int32, (L, LANES), 1)
        dacum_diag = jnp.zeros((L, LANES), F32)
        dlast_parts = []
        for g in range(G):
            bg = _mx(act_ref[:, SSD_D_INNER + g * N:SSD_D_INNER + (g + 1) * N])
            cg = _mx(act_ref[:, SSD_D_INNER + G * N + g * N:SSD_D_INNER + G * N + (g + 1) * N])
            cb = _dot(cg, bg, _NT)
            dcb = jnp.zeros((L, L), F32)
            dcg = jnp.zeros((L, N), F32)
            dbg = jnp.zeros((L, N), F32)
            for half in range(2):
                p = 2 * g + half
                cols = slice(p * LANES, (p + 1) * LANES)
                hcols = slice(half * LANES, (half + 1) * LANES)
                xs = act_ref[:, cols]
                xdt = xs * pre["dt_full"][:, cols]
                dyv = dy_ref[:, cols]
                dxdt = jnp.zeros((L, LANES), F32)
                for j in range(2):
                    h = 2 * p + j
                    dec = _decay(pre, h)
                    m = cb * dec
                    dyh = _mx(dyv * _head_mask(j))
                    dm = _dot(dyh, _mx(xdt), _NT)
                    dxdt = dxdt + _dot(_mx(m), dyh, _TN)
                    e = dm * m
                    ehi, elo = _split_hi_lo(e)
                    d_h = (_dot(ehi, ones) + _dot(elo, ones)) - (_dot(ehi, ones, _TN) + _dot(elo, ones, _TN))
                    dacum_diag = jnp.where(lane == h, d_h, dacum_diag)
                    dcb = dcb + dm * dec
                lam = jnp.exp(acum_full[:, cols])
                stv = _mx(st_ref[0, g, :, hcols])
                z = _dot(cg, stv)
                dz = _mx(lam * dyv)
                dcg = dcg + _dot(dz, stv, _NT)
                dst_in = _dot(cg, dz, _TN)
                dsv = dst[g, :, hcols]
                w = jnp.exp(last_full[:, cols] - acum_full[:, cols])
                q = _dot(bg, _mx(dsv))
                wq = w * q
                dxdt = dxdt + wq
                wqx = wq * xdt
                dbg = dbg + _dot(_mx(xdt * w), _mx(dsv), _NT)
                elast = jnp.exp(last_full[:, cols])
                dlast_p = jnp.sum(wqx, axis=0, keepdims=True) + elast * jnp.sum(dsv * st_ref[0, g, :, hcols], axis=0, keepdims=True)
                dac_ref[:, cols] = dyv * z * lam - wqx
                dlast_parts.append(dlast_p)
                dst[g, :, hcols] = dst_in + dsv * elast
                dxdt_ref[:, cols] = dxdt
                dact_ref[:, cols] = dxdt * pre["dt_full"][:, cols] + dsk_ref[:, cols] * dyv
            dcbm = _mx(dcb)
            dact_ref[:, SSD_D_INNER + g * N:SSD_D_INNER + (g + 1) * N] = dbg + _dot(dcbm, cg, _TN)
            dact_ref[:, SSD_D_INNER + G * N + g * N:SSD_D_INNER + G * N + (g + 1) * N] = dcg + _dot(dcbm, bg)

        xs_all = act_ref[:, 0:SSD_D_INNER]
        dacum = dacum_diag + _dot_exact01(dac_ref[...], reduce)
        dlast = _dot_exact01(jnp.concatenate(dlast_parts, axis=1), reduce)
        row = lax.broadcasted_iota(jnp.int32, (L, LANES), 0)
        dacum = dacum + jnp.where(row == L - 1, dlast, 0.0)
        da_col = _dot01(dacum, pre["upper"], left=True)
        a_w = -jnp.exp(aw_ref[...])
        ddt = a_w * da_col + _dot_exact01(dxdt_ref[...] * xs_all, reduce)
        xin = dtp_ref[...] + bw_ref[...]
        ddtp = ddt * (1.0 / (1.0 + jnp.exp(-xin)))
        valid = lane < SSD_HEADS
        ddtp = jnp.where(valid, ddtp, 0.0)
        ddtp_ref[...] = ddtp
        dbias_ref[...] += jnp.sum(ddtp, axis=0, keepdims=True)
        dalog_ref[...] += jnp.sum(jnp.where(valid, da_col * pre["a_col"], 0.0), axis=0, keepdims=True)
        dskip_ref[...] += jnp.sum(_dot_exact01(dy_ref[...] * xs_all, reduce), axis=0, keepdims=True)

    sp = _ssd_specs(s, nc, True)
    acc = pl.BlockSpec((1, LANES), lambda i: (0, 0))
    return pl.pallas_call(
        body, name="ssd_scan_bwd", grid=(nc,),
        in_specs=[sp["xbc"], sp["dtp"], sp["dtp_t"]] + sp["consts"] + [sp["state"], sp["rows"]],
        out_specs=[sp["xbc"], pl.BlockSpec((L, LANES), lambda i: (nc - 1 - i, 0)), acc, acc, acc],
        out_shape=[_sds((s, SSD_CONV_DIM), F32), _sds((s, LANES), F32)] + [_sds((1, LANES), F32)] * 3,
        scratch_shapes=[pltpu.VMEM((G, N, GW), F32), pltpu.VMEM((L, SSD_D_INNER), F32), pltpu.VMEM((L, SSD_D_INNER), F32)],
        compiler_params=_params(("arbitrary",)),
    )(act, proj, dtp_t, *_const_args(cs), states, dy)


def ssd_post_fwd(y, proj, g):
    s, d = y.shape
    ts = _pick(s, (256, 128))

    def body(y_ref, z_ref, g_ref, o_ref):
        y2 = y_ref[...] * _silu(z_ref[...])
        r = lax.rsqrt(jnp.mean(y2 * y2, axis=-1, keepdims=True) + NORM_EPS)
        o_ref[...] = (y2 * r * g_ref[...]).astype(o_ref.dtype)

    row = pl.BlockSpec((ts, d), lambda i: (i, 0))
    return pl.pallas_call(
        body, name="ssd_post_fwd", grid=(s // ts,), in_specs=[row, row, pl.BlockSpec((1, d), lambda i: (0, 0))],
        out_specs=row, out_shape=_sds((s, d), MXU_DTYPE), compiler_params=_params(("parallel",)),
    )(y, proj, g)


def ssd_post_bwd(y, proj, g, dy3):
    s, d = y.shape
    ts = _pick(s, (256, 128))

    def body(y_ref, z_ref, g_ref, d3_ref, dy_ref, dz_ref, dg_ref):
        yv, zv = y_ref[...], z_ref[...]
        sz, sgrad = _silu_and_grad(zv)
        y2 = yv * sz
        r = lax.rsqrt(jnp.mean(y2 * y2, axis=-1, keepdims=True) + NORM_EPS)
        xh = y2 * r
        d3 = d3_ref[...]
        dxh = d3 * g_ref[...]
        dy2 = r * (dxh - xh * jnp.mean(dxh * xh, axis=-1, keepdims=True))
        dy_ref[...] = dy2 * sz
        dz_ref[...] = (dy2 * yv * sgrad).astype(dz_ref.dtype)
        part = jnp.sum(d3 * xh, axis=0, keepdims=True)

        @pl.when(pl.program_id(0) == 0)
        def _():
            dg_ref[...] = part

        @pl.when(pl.program_id(0) != 0)
        def _():
            dg_ref[...] += part

    row = pl.BlockSpec((ts, d), lambda i: (i, 0))
    vec = pl.BlockSpec((1, d), lambda i: (0, 0))
    return pl.pallas_call(
        body, name="ssd_post_bwd", grid=(s // ts,), in_specs=[row, row, vec, row], out_specs=[row, row, vec],
        out_shape=[_sds((s, d), F32), _sds((s, d), MXU_DTYPE), _sds((1, d), F32)],
        compiler_params=_params(("arbitrary",)),
    )(y, proj, g, dy3)


def ssd_core_fwd(proj, cw, cb, dt_bias, a_log, d_skip, norm_g):
    cs = _ssd_consts(dt_bias, a_log, d_skip)
    act = ssd_conv_fwd(proj, cw, cb)
    dtp_t = proj[:, SSD_IN_DIM - SSD_HEADS:SSD_IN_DIM].T
    y, states = ssd_scan_fwd(act, proj, dtp_t, cs)
    y3 = ssd_post_fwd(y, proj, norm_g)
    return y3, (cs, act, dtp_t, y, states)


def ssd_core_bwd(proj, cw, cb, norm_g, saved, dy3):
    cs, act, dtp_t, y, states = saved
    dy, dz, dnorm = ssd_post_bwd(y, proj, norm_g, dy3)
    dact, ddtp, dalog, dbias, dskip = ssd_scan_bwd(act, proj, dtp_t, cs, states, dy)
    dxbc, dcw, dcb = ssd_conv_bwd(proj, dact, cw, cb)
    dproj = jnp.concatenate([dz, dxbc, ddtp.astype(MXU_DTYPE)], axis=1)
    h = SSD_HEADS
    return dproj, dcw, dcb, dbias[0, :h], dalog[0, :h], dskip[0, :h], dnorm


def ssd_core(proj, cw, cb, dt_bias, a_log, d_skip, norm_g, dy3):
    y3, saved = ssd_core_fwd(proj, cw, cb, dt_bias, a_log, d_skip, norm_g)
    return y3, ssd_core_bwd(proj, cw, cb, norm_g, saved, dy3)


def loss_head(x, g, target):
    s, d = x.shape
    ts = _pick(s, (512, 256, 128))

    def body(x_ref, g_ref, t_ref, loss_ref, dx_ref, dg_ref):
        xv = x_ref[...]
        r = lax.rsqrt(jnp.mean(xv * xv, axis=-1, keepdims=True) + NORM_EPS)
        xh = xv * r
        err = xh * g_ref[...] - t_ref[...]
        dy = err * (1.0 / d)
        dxh = dy * g_ref[...]
        dx_ref[...] = r * (dxh - xh * jnp.mean(dxh * xh, axis=-1, keepdims=True))
        part = jnp.sum(dy * xh, axis=0, keepdims=True)
        lpart = jnp.full((1, LANES), 0.5 * jnp.sum(jnp.mean(err * err, axis=-1, keepdims=True)), F32)

        @pl.when(pl.program_id(0) == 0)
        def _():
            dg_ref[...] = part
            loss_ref[...] = lpart

        @pl.when(pl.program_id(0) != 0)
        def _():
            dg_ref[...] += part
            loss_ref[...] += lpart

    row = pl.BlockSpec((ts, d), lambda i: (i, 0))
    vec = pl.BlockSpec((1, d), lambda i: (0, 0))
    return pl.pallas_call(
        body, name="loss_head", grid=(s // ts,), in_specs=[row, vec, row],
        out_specs=[pl.BlockSpec((1, LANES), lambda i: (0, 0)), row, vec],
        out_shape=[_sds((1, LANES), F32), _sds((s, d), F32), _sds((1, d), F32)],
        compiler_params=_params(("arbitrary",)),
    )(x, g, target)


def _adamw_math(w, g, m, v):
    m = ADAM_B1 * m + (1.0 - ADAM_B1) * g
    v = ADAM_B2 * v + (1.0 - ADAM_B2) * (g * g)
    m_hat = m / (1.0 - ADAM_B1 ** ADAM_STEP)
    v_hat = v / (1.0 - ADAM_B2 ** ADAM_STEP)
    return -ADAM_LR * (m_hat / (jnp.sqrt(v_hat) + ADAM_EPS) + ADAM_WD * w), m, v


def adamw(w, g, m, v, name="adamw"):
    r, c = w.shape
    tr = _pick(r, (256, 128, 64, 32, 16, 8))

    def body(w_ref, g_ref, m_ref, v_ref, d_ref, nm_ref, nv_ref):
        d_ref[...], nm_ref[...], nv_ref[...] = _adamw_math(w_ref[...], g_ref[...], m_ref[...], v_ref[...])

    blk = pl.BlockSpec((tr, c), lambda i: (i, 0))
    return pl.pallas_call(
        body, name=name, grid=(r // tr,), in_specs=[blk] * 4, out_specs=[blk] * 3,
        out_shape=[_sds((r, c), F32)] * 3, compiler_params=_params(("parallel",)),
    )(w, g, m, v)


def adamw_small(w, parts, m, v):
    n, r, c = parts.shape

    def body(w_ref, p_ref, m_ref, v_ref, g_ref, d_ref, nm_ref, nv_ref):
        g = p_ref[0]
        for k in range(1, n):
            g = g + p_ref[k]
        g_ref[...] = g
        d_ref[...], nm_ref[...], nv_ref[...] = _adamw_math(w_ref[...], g, m_ref[...], v_ref[...])

    return pl.pallas_call(
        body, name="adamw_small", out_shape=[_sds((r, c), F32)] * 4, compiler_params=_params(),
    )(w, parts, m, v)


def pair_sum(packs, recv, half):
    _, nchip, r, c = packs.shape
    tr = _pick(r, (512, 256, 128, 64, 32, 16))

    def body(h_ref, a_ref, b_ref, o_ref, ob_ref):
        sm = a_ref[0, 0] + b_ref[0]
        o_ref[0] = sm
        ob_ref[0] = sm.astype(ob_ref.dtype)

    blk = pl.BlockSpec((1, tr, c), lambda s, i, h: (s, i, 0))
    return pl.pallas_call(
        body, name="pair_sum",
        grid_spec=pltpu.PrefetchScalarGridSpec(
            num_scalar_prefetch=1, grid=(nchip, r // tr),
            in_specs=[pl.BlockSpec((1, 1, tr, c), lambda s, i, h: (h[0], s, i, 0)), blk], out_specs=[blk, blk]),
        out_shape=[_sds((nchip, r, c), F32), _sds((nchip, r, c), jnp.bfloat16)],
        compiler_params=_params(("parallel", "parallel")),
    )(half, packs, recv)


def chip_sum(own, chip, recv):
    _, r, c = own.shape
    tr = _pick(r, (512, 256, 128, 64, 32, 16))

    def body(s_ref, a_ref, b_ref, o_ref):
        o_ref[...] = a_ref[0] + b_ref[0].astype(F32) + b_ref[1].astype(F32) + b_ref[2].astype(F32)

    return pl.pallas_call(
        body, name="chip_sum",
        grid_spec=pltpu.PrefetchScalarGridSpec(
            num_scalar_prefetch=1, grid=(r // tr,),
            in_specs=[pl.BlockSpec((1, tr, c), lambda i, s: (s[0], i, 0)), pl.BlockSpec((3, tr, c), lambda i, s: (0, i, 0))],
            out_specs=pl.BlockSpec((tr, c), lambda i, s: (i, 0))),
        out_shape=_sds((r, c), F32), compiler_params=_params(("parallel",)),
    )(chip, own, recv)


ANY = pl.BlockSpec(memory_space=pl.ANY)


def _coords():
    return lax.axis_index("x"), lax.axis_index("y"), lax.axis_index("c")


def all_gather_8(halves, name):
    _, r, c = halves.shape

    def body(h_ref, out_ref, send_sems, recv_sems, local_sem):
        x, y, cc = _coords()
        x_ref = h_ref.at[cc]
        me, sibling = (x, y, cc), (x, y, 1 - cc)
        chips = [(1 - x, y), (x, 1 - y), (1 - x, 1 - y)]

        def slot(px, py, pc):
            return out_ref.at[4 * px + 2 * py + pc]

        def copy(k, blk, to, src=None):
            return pltpu.make_async_remote_copy(
                src_ref=slot(*blk) if src is None else src, dst_ref=slot(*blk),
                send_sem=send_sems.at[k], recv_sem=recv_sems.at[k], device_id=to, device_id_type=MESH)

        mine = pltpu.make_async_copy(x_ref, slot(*me), local_sem)
        mine.start()
        first = [copy(0, me, sibling, src=x_ref)]
        first += [copy(1 + j, me, (*chip, cc), src=x_ref) for j, chip in enumerate(chips)]
        for cp in first:
            cp.start()
        passed = [copy(4 + j, (*chip, cc), sibling) for j, chip in enumerate(chips)]
        for j, chip in enumerate(chips):
            copy(1 + j, (*chip, cc), me).wait_recv()
            passed[j].start()
        copy(0, sibling, me).wait_recv()
        for j, chip in enumerate(chips):
            copy(4 + j, (*chip, 1 - cc), me).wait_recv()
        for cp in first + passed:
            cp.wait_send()
        mine.wait()

    return pl.pallas_call(
        body, name=name, in_specs=[ANY], out_specs=ANY, out_shape=_sds((8, r, c), halves.dtype),
        scratch_shapes=[pltpu.SemaphoreType.DMA((7,)), pltpu.SemaphoreType.DMA((7,)), pltpu.SemaphoreType.DMA],
        compiler_params=pltpu.CompilerParams(has_side_effects=True),
    )(halves)


def sibling_swap(src, name, other_half=False):
    shape = src.shape[1:] if other_half else src.shape

    def body(s_ref, o_ref, send_sem, recv_sem):
        x, y, cc = _coords()
        cp = pltpu.make_async_remote_copy(src_ref=s_ref.at[1 - cc] if other_half else s_ref, dst_ref=o_ref, send_sem=send_sem, recv_sem=recv_sem,
                                          device_id=(x, y, 1 - cc), device_id_type=MESH)
        cp.start()
        cp.wait()

    return pl.pallas_call(
        body, name=name, in_specs=[ANY], out_specs=ANY, out_shape=_sds(shape, src.dtype),
        scratch_shapes=[pltpu.SemaphoreType.DMA, pltpu.SemaphoreType.DMA],
        compiler_params=pltpu.CompilerParams(has_side_effects=True),
    )(src)


def chip_exchange(src, name):
    _, r, c = src.shape

    def body(s_ref, o_ref, send_sems, recv_sems):
        x, y, cc = _coords()
        chips = [(1 - x, y), (x, 1 - y), (1 - x, 1 - y)]
        cps = [pltpu.make_async_remote_copy(
            src_ref=s_ref.at[2 * px + py], dst_ref=o_ref.at[k], send_sem=send_sems.at[k], recv_sem=recv_sems.at[k],
            device_id=(px, py, cc), device_id_type=MESH) for k, (px, py) in enumerate(chips)]
        for cp in cps:
            cp.start()
        for cp in cps:
            cp.wait()

    return pl.pallas_call(
        body, name=name, in_specs=[ANY], out_specs=ANY, out_shape=_sds((3, r, c), src.dtype),
        scratch_shapes=[pltpu.SemaphoreType.DMA((3,)), pltpu.SemaphoreType.DMA((3,))],
        compiler_params=pltpu.CompilerParams(has_side_effects=True),
    )(src)


N_CHIPS = 4
PACK_COLS = 1024
BIG = ("ssd_w_in", "ssd_w_out", "sb_w_qkv", "sb_w_out", "ffn_w_in", "ffn_w_out")
CONVW = ("ssd_conv_w", "ffn_conv_w")
COL_SHARDED = ("ssd_w_in", "sb_w_qkv", "ffn_w_in", "ssd_conv_w", "ffn_conv_w")
SMALL = ("mix_norm", "ffn_norm", "final_norm", "ssd_conv_b", "ssd_dt_bias", "ssd_a_log", "ssd_d", "ssd_norm", "ffn_conv_b")
WEIGHTS = ("mix_norm", "ffn_norm", "final_norm", "ssd_w_in", "ssd_conv_w", "ssd_conv_b", "ssd_dt_bias", "ssd_a_log",
           "ssd_d", "ssd_norm", "ssd_w_out", "sb_w_qkv", "sb_w_out", "ffn_w_in", "ffn_conv_w", "ffn_conv_b", "ffn_w_out")


def _to_rows(flat, multiple):
    rows = -(-flat.shape[-1] // PACK_COLS)
    rows = -(-rows // multiple) * multiple
    pad = rows * PACK_COLS - flat.shape[-1]
    flat = jnp.pad(flat, [(0, 0)] * (flat.ndim - 1) + [(0, pad)])
    return flat.reshape(flat.shape[:-1] + (rows, PACK_COLS))


def _unshard(name, stacked):
    if name in COL_SHARDED:
        n, l, a, b = stacked.shape
        return jnp.transpose(stacked, (1, 2, 0, 3)).reshape(l, a, n * b)
    n, l, a, b = stacked.shape
    return jnp.transpose(stacked, (1, 0, 2, 3)).reshape(l, n * a, b)


def _shard(name, full):
    l, a, b = full.shape
    if name in COL_SHARDED:
        return jnp.transpose(full.reshape(l, a, N_CHIPS, b // N_CHIPS), (2, 0, 1, 3))
    return jnp.transpose(full.reshape(l, N_CHIPS, a // N_CHIPS, b), (1, 0, 2, 3))


def _gather_weights(shards, names, dtype, name):
    flat = jnp.concatenate([shards[n].reshape(-1) for n in names]).astype(dtype)
    rows = _to_rows(flat, 64)
    halves = rows.reshape(2, rows.shape[0] // 2, PACK_COLS)
    got = all_gather_8(halves, name).reshape(N_CHIPS, -1)
    out, off = {}, 0
    for n in names:
        size = shards[n].size
        out[n] = _unshard(n, got[:, off:off + size].reshape((N_CHIPS,) + shards[n].shape))
        off += size
    return out


def _reduce_grads(full_grads, names, shard_shapes):
    cc = lax.axis_index("c")
    chip = 2 * lax.axis_index("x") + lax.axis_index("y")
    flat = jnp.concatenate([_shard(n, full_grads[n]).reshape(N_CHIPS, -1) for n in names], axis=1)
    rows = _to_rows(flat, 64)
    rh = rows.shape[1] // 2
    packs = jnp.transpose(rows.reshape(N_CHIPS, 2, rh, PACK_COLS), (1, 0, 2, 3))
    from_sibling = sibling_swap(packs, "grad_pair_swap", other_half=True)
    pair, pair_wire = pair_sum(packs, from_sibling, cc.reshape(1).astype(jnp.int32))
    from_chips = chip_exchange(pair_wire, "grad_chip_exchange")
    mine = chip_sum(pair, chip.reshape(1).astype(jnp.int32), from_chips)
    other = sibling_swap(mine, "grad_half_swap")
    full = jnp.where(cc == 0, jnp.concatenate([mine, other]), jnp.concatenate([other, mine])).reshape(-1)
    out, off = {}, 0
    for n in names:
        size = 1
        for d in shard_shapes[n]:
            size *= d
        out[n] = full[off:off + size].reshape(shard_shapes[n])
        off += size
    return out


def kernel(x, mix_norm, ffn_norm, final_norm, ssd_w_in, ssd_conv_w, ssd_conv_b, ssd_dt_bias, ssd_a_log, ssd_d, ssd_norm, ssd_w_out, sb_w_qkv, sb_w_out, ffn_w_in, ffn_conv_w, ffn_conv_b, ffn_w_out, loss_target, m_mix_norm, m_ffn_norm, m_final_norm, m_ssd_w_in, m_ssd_conv_w, m_ssd_conv_b, m_ssd_dt_bias, m_ssd_a_log, m_ssd_d, m_ssd_norm, m_ssd_w_out, m_sb_w_qkv, m_sb_w_out, m_ffn_w_in, m_ffn_conv_w, m_ffn_conv_b, m_ffn_w_out, v_mix_norm, v_ffn_norm, v_final_norm, v_ssd_w_in, v_ssd_conv_w, v_ssd_conv_b, v_ssd_dt_bias, v_ssd_a_log, v_ssd_d, v_ssd_norm, v_ssd_w_out, v_sb_w_qkv, v_sb_w_out, v_ffn_w_in, v_ffn_conv_w, v_ffn_conv_b, v_ffn_w_out):
    given = dict(locals())
    w = {n: given[n] for n in WEIGHTS}
    mom = {n: given["m_" + n] for n in WEIGHTS}
    var = {n: given["v_" + n] for n in WEIGHTS}

    fw = _gather_weights(w, BIG, MXU_DTYPE, "gather_weights")
    fw.update(_gather_weights(w, CONVW, F32, "gather_conv_weights"))
    w_ssd_in = jnp.pad(fw["ssd_w_in"], ((0, 0), (0, 0), (0, SSD_IN_PAD - SSD_IN_DIM)))
    row = lambda v: v.reshape(1, -1)

    xcur = x[0]
    saved = []
    for i in range(DEPTH):
        j = i // 2
        h, r = rms_fwd(xcur, row(mix_norm[i]))
        if i % 2 == 0:
            proj = mm(h, w_ssd_in[j], tn=896, name="mm_ssd_in")
            y3, core = ssd_core_fwd(proj, fw["ssd_conv_w"][j], row(ssd_conv_b[j]), ssd_dt_bias[j], ssd_a_log[j], ssd_d[j], row(ssd_norm[j]))
            x1 = mm(y3, fw["ssd_w_out"][j], res=xcur, name="mm_ssd_out")
            mix = (proj, y3, core)
        else:
            qkv = mm(h, fw["sb_w_qkv"][j], out_dtype=MXU_DTYPE, name="mm_sb_qkv")
            o = sb_fwd(qkv)
            x1 = mm(o, fw["sb_w_out"][j], res=xcur, name="mm_sb_out")
            mix = (qkv, o)
        h2, r2 = rms_fwd(x1, row(ffn_norm[i]))
        u0 = mm(h2, fw["ffn_w_in"][i], name="mm_ffn_in")
        a = ffn_mid_fwd(u0, fw["ffn_conv_w"][i], row(ffn_conv_b[i]))
        x2 = mm(a, fw["ffn_w_out"][i], res=x1, name="mm_ffn_out")
        saved.append((xcur, h, r, mix, x1, h2, r2, u0, a))
        xcur = x2
    loss_part, dx, d_final = loss_head(xcur, row(final_norm), loss_target[0])

    gl = {n: [None] * (DEPTH if n in ("mix_norm", "ffn_norm", "ffn_w_in", "ffn_conv_w", "ffn_conv_b", "ffn_w_out") else 2)
          for n in WEIGHTS if n != "final_norm"}
    for i in reversed(range(DEPTH)):
        j = i // 2
        x0, h, r, mix, x1, h2, r2, u0, a = saved[i]
        gl["ffn_w_out"][i] = mm(a, dx, "tn", name="mm_d_ffn_out")
        da = mm(dx, fw["ffn_w_out"][i], "nt", name="mm_da_ffn")
        dug, duu, gl["ffn_conv_w"][i], dcb = ffn_mid_bwd(u0, da, fw["ffn_conv_w"][i], row(ffn_conv_b[i]))
        gl["ffn_conv_b"][i] = dcb[0]
        du0 = jnp.concatenate([dug, duu], axis=1)
        gl["ffn_w_in"][i] = mm(h2, du0, "tn", name="mm_d_ffn_in")
        dh2 = mm(du0, fw["ffn_w_in"][i], "nt", name="mm_dh_ffn")
        dx1, dg = rms_bwd(x1, r2, row(ffn_norm[i]), dh2, dx)
        gl["ffn_norm"][i] = dg[0]
        if i % 2 == 0:
            proj, y3, core = mix
            gl["ssd_w_out"][j] = mm(y3, dx1, "tn", name="mm_d_ssd_out")
            dy3 = mm(dx1, fw["ssd_w_out"][j], "nt", name="mm_dy3_ssd")
            dproj, gl["ssd_conv_w"][j], dcb, gl["ssd_dt_bias"][j], gl["ssd_a_log"][j], gl["ssd_d"][j], dnorm = ssd_core_bwd(
                proj, fw["ssd_conv_w"][j], row(ssd_conv_b[j]), row(ssd_norm[j]), core, dy3)
            gl["ssd_conv_b"][j] = dcb[0]
            gl["ssd_norm"][j] = dnorm[0]
            gl["ssd_w_in"][j] = mm(h, dproj, "tn", tn=896, name="mm_d_ssd_in")[:, :SSD_IN_DIM]
            dh = mm(dproj, w_ssd_in[j], "nt", name="mm_dh_ssd")
        else:
            qkv, o = mix
            gl["sb_w_out"][j] = mm(o, dx1, "tn", name="mm_d_sb_out")
            do = mm(dx1, fw["sb_w_out"][j], "nt", out_dtype=MXU_DTYPE, name="mm_do_sb")
            dqkv = sb_bwd(qkv, do)
            gl["sb_w_qkv"][j] = mm(h, dqkv, "tn", name="mm_d_sb_qkv")
            dh = mm(dqkv, fw["sb_w_qkv"][j], "nt", name="mm_dh_sb")
        dx, dg = rms_bwd(x0, r, row(mix_norm[i]), dh, dx1)
        gl["mix_norm"][i] = dg[0]
    grads_full = {n: jnp.stack(v) for n, v in gl.items()}
    grads_full["final_norm"] = d_final[0]

    sharded = BIG + CONVW
    g = _reduce_grads(grads_full, sharded, {n: w[n].shape for n in sharded})
    delta, new_m, new_v = {}, {}, {}
    for n in sharded:
        two_d = lambda t: t.reshape(-1, t.shape[-1])
        d2, m2, v2 = adamw(two_d(w[n]), two_d(g[n]), two_d(mom[n]), two_d(var[n]), name="adamw_" + n)
        delta[n], new_m[n], new_v[n] = d2.reshape(w[n].shape), m2.reshape(w[n].shape), v2.reshape(w[n].shape)

    def small_pack(d, extra):
        return _to_rows(jnp.concatenate([d[n].reshape(-1) for n in SMALL] + [extra]), 16)

    part = small_pack(grads_full, loss_part[0, 0:1])
    parts = all_gather_8(jnp.stack([part, part]), "gather_small_grads")
    zero = jnp.zeros((1,), F32)
    gs, ds, ms, vs = adamw_small(small_pack(w, zero), parts, small_pack(mom, zero), small_pack(var, zero))
    off = 0
    for n in SMALL:
        size = w[n].size
        for dst, src in ((g, gs), (delta, ds), (new_m, ms), (new_v, vs)):
            dst[n] = src.reshape(-1)[off:off + size].reshape(w[n].shape)
        off += size
    loss = gs.reshape(-1)[off]

    return (loss, dx[None], *[g[n] for n in WEIGHTS], *[delta[n] for n in WEIGHTS],
            *[new_m[n] for n in WEIGHTS], *[new_v[n] for n in WEIGHTS])
```

```python
import functools

import jax
import jax.numpy as jnp
from jax import lax
from jax.experimental import pallas as pl
from jax.experimental.pallas import tpu as pltpu

F32 = jnp.float32
MXU_DTYPE = jnp.bfloat16
HIGHEST = lax.Precision.HIGHEST

D_MODEL = 1024
DEPTH = 4
NORM_EPS = 1e-6
SSD_D_INNER = 2048
SSD_HEAD_DIM = 64
SSD_HEADS = 32
SSD_GROUPS = 8
SSD_STATE = 128
SSD_CONV = 4
SSD_CHUNK = 128
SSD_CONV_DIM = 4096
SSD_IN_DIM = 6176
SSD_IN_PAD = 6272
SB_HEADS = 16
SB_HEAD_DIM = 64
FFN_D_FF = 2816
FFN_CONV = 3
ADAM_LR, ADAM_B1, ADAM_B2, ADAM_EPS, ADAM_WD, ADAM_STEP = 0.001, 0.9, 0.999, 1e-08, 0.01, 10

LANES = 128
SUBLANES = 8
VMEM_LIMIT = 56 * 1024 * 1024
MESH = pl.DeviceIdType.MESH


def _params(sem=None):
    return pltpu.CompilerParams(dimension_semantics=sem, vmem_limit_bytes=VMEM_LIMIT)


def _sds(shape, dtype):
    return jax.ShapeDtypeStruct(shape, dtype)


def _dot(a, b, dims=(((1,), (0,)), ((), ())), precision=None):
    return lax.dot_general(a, b, dims, precision=precision, preferred_element_type=F32)


_NN = (((1,), (0,)), ((), ()))
_NT = (((1,), (1,)), ((), ()))
_TN = (((0,), (0,)), ((), ()))


def _mx(a):
    return a.astype(MXU_DTYPE)


def _silu(x):
    return x * (1.0 / (1.0 + jnp.exp(-x)))


def _silu_and_grad(x):
    s = 1.0 / (1.0 + jnp.exp(-x))
    return x * s, s * (1.0 + x * (1.0 - s))


def _pick(n, cands):
    for c in cands:
        if n % c == 0:
            return c
    return n


def rms_fwd(x, g):
    s, d = x.shape
    ts = _pick(s, (512, 256, 128))

    def body(x_ref, g_ref, h_ref, r_ref):
        xv = x_ref[...]
        r = lax.rsqrt(jnp.mean(xv * xv, axis=-1, keepdims=True) + NORM_EPS)
        h_ref[...] = (xv * r * g_ref[...]).astype(h_ref.dtype)
        r_ref[...] = r

    return pl.pallas_call(
        body, name="rms_fwd", grid=(s // ts,),
        in_specs=[pl.BlockSpec((ts, d), lambda i: (i, 0)), pl.BlockSpec((1, d), lambda i: (0, 0))],
        out_specs=[pl.BlockSpec((ts, d), lambda i: (i, 0)), pl.BlockSpec((ts, 1), lambda i: (i, 0))],
        out_shape=[_sds((s, d), MXU_DTYPE), _sds((s, 1), F32)],
        compiler_params=_params(("parallel",)),
    )(x, g)


def rms_bwd(x, r, g, dh, dres):
    s, d = x.shape
    ts = _pick(s, (512, 256, 128))

    def body(x_ref, r_ref, g_ref, dh_ref, dres_ref, dx_ref, dg_ref):
        xh = x_ref[...] * r_ref[...]
        dhv = dh_ref[...]
        dxh = dhv * g_ref[...]
        dx_ref[...] = dres_ref[...] + r_ref[...] * (dxh - xh * jnp.mean(dxh * xh, axis=-1, keepdims=True))
        part = jnp.sum(dhv * xh, axis=0, keepdims=True)

        @pl.when(pl.program_id(0) == 0)
        def _():
            dg_ref[...] = part

        @pl.when(pl.program_id(0) != 0)
        def _():
            dg_ref[...] += part

    row = pl.BlockSpec((ts, d), lambda i: (i, 0))
    return pl.pallas_call(
        body, name="rms_bwd", grid=(s // ts,),
        in_specs=[row, pl.BlockSpec((ts, 1), lambda i: (i, 0)), pl.BlockSpec((1, d), lambda i: (0, 0)), row, row],
        out_specs=[row, pl.BlockSpec((1, d), lambda i: (0, 0))],
        out_shape=[_sds((s, d), F32), _sds((1, d), F32)],
        compiler_params=_params(("arbitrary",)),
    )(x, r, g, dh, dres)


def mm(a, b, mode="nn", res=None, out_dtype=F32, tm=None, tn=None, n_split=1, name="mm"):
    if mode == "nn":
        (m, k), (_, n) = a.shape, b.shape
    elif mode == "nt":
        (m, k), (n, _) = a.shape, b.shape
    else:
        (k, m), (_, n) = a.shape, b.shape
    tm = tm or _pick(m, (1024, 512, 256, 128))
    tn = tn or _pick(n, (512, 896, 256, 128))
    dims = {"nn": _NN, "nt": _NT, "tn": _TN}[mode]

    def body(*refs):
        a_ref, b_ref = refs[0], refs[1]
        o_ref = refs[-1]
        acc = _dot(_mx(a_ref[...]), _mx(b_ref[...]), dims)
        if res is not None:
            acc = acc + refs[2][...]
        o_ref[...] = acc.astype(o_ref.dtype)

    a_spec = pl.BlockSpec((k, tm), lambda i, j: (0, i)) if mode == "tn" else pl.BlockSpec((tm, k), lambda i, j: (i, 0))
    b_spec = pl.BlockSpec((tn, k), lambda i, j: (j, 0)) if mode == "nt" else pl.BlockSpec((k, tn), lambda i, j: (0, j))
    o_spec = pl.BlockSpec((tm, tn), lambda i, j: (i, j))
    ins, specs = [a, b], [a_spec, b_spec]
    if res is not None:
        ins.append(res)
        specs.append(o_spec)
    out_shape = _sds((m, n), out_dtype)
    if n_split > 1:
        per = n // n_split // tn
        o_spec = pl.BlockSpec((None, tm, tn), lambda i, j: (j // per, i, j % per))
        out_shape = _sds((n_split, m, n // n_split), out_dtype)
    return pl.pallas_call(
        body, name=name, grid=(m // tm, n // tn), in_specs=specs, out_specs=o_spec,
        out_shape=out_shape, compiler_params=_params(("parallel", "parallel")),
    )(*ins)


CONV_ROWS = 256
CONV_COLS = 128


def _row_iota8(cols):
    return lax.broadcasted_iota(jnp.int32, (SUBLANES, cols), 0)


def _shift_down(cur, prev8, k):
    if k == 0:
        return cur
    rolled = pltpu.roll(cur, k, 0)
    head = jnp.where(_row_iota8(cur.shape[1]) < k, pltpu.roll(prev8, k, 0), rolled[0:SUBLANES])
    return jnp.concatenate([head, rolled[SUBLANES:]], axis=0)


def _shift_up(cur, next8, k):
    if k == 0:
        return cur
    n = cur.shape[0]
    rolled = pltpu.roll(cur, n - k, 0)
    tail = jnp.where(_row_iota8(cur.shape[1]) >= SUBLANES - k, pltpu.roll(next8, SUBLANES - k, 0), rolled[n - SUBLANES:])
    return jnp.concatenate([rolled[:n - SUBLANES], tail], axis=0)


def _load_prev8(ref, i, rows):
    start = pl.multiple_of(jnp.maximum(i * rows - SUBLANES, 0), SUBLANES)
    p = ref[pl.ds(start, SUBLANES), :]
    return jnp.where(i > 0, p, jnp.zeros_like(p))


def _conv_rows(ref, w_ref, b_ref, i, rows, width):
    cur = ref[pl.ds(pl.multiple_of(i * rows, rows), rows), :]
    prev8 = _load_prev8(ref, i, rows)
    shifted = [_shift_down(cur, prev8, k) for k in range(width)]
    acc = b_ref[...] + w_ref[width - 1:width, :] * shifted[0]
    for k in range(1, width):
        acc = acc + w_ref[width - 1 - k:width - k, :] * shifted[k]
    return acc, shifted


def _conv_bwd_rows(du, next8, w_ref, width):
    acc = w_ref[width - 1:width, :] * du
    for k in range(1, width):
        acc = acc + w_ref[width - 1 - k:width - k, :] * _shift_up(du, next8, k)
    return acc


def ffn_mid_fwd(u0, cw, cb):
    s, f2 = u0.shape
    f = f2 // 2
    nt = f // CONV_COLS
    rows = min(CONV_ROWS, s)

    def body(ug_ref, uu_ref, wg_ref, wu_ref, bg_ref, bu_ref, a_ref):
        def step(i, carry):
            g, _ = _conv_rows(ug_ref, wg_ref, bg_ref, i, rows, FFN_CONV)
            u, _ = _conv_rows(uu_ref, wu_ref, bu_ref, i, rows, FFN_CONV)
            a_ref[pl.ds(pl.multiple_of(i * rows, rows), rows), :] = (_silu(g) * u).astype(a_ref.dtype)
            return carry

        lax.fori_loop(0, s // rows, step, 0)

    col = lambda off: pl.BlockSpec((s, CONV_COLS), lambda j: (0, j + off))
    wsp = lambda r, off: pl.BlockSpec((r, CONV_COLS), lambda j: (0, j + off))
    return pl.pallas_call(
        body, name="ffn_mid_fwd", grid=(nt,),
        in_specs=[col(0), col(nt), wsp(FFN_CONV, 0), wsp(FFN_CONV, nt), wsp(1, 0), wsp(1, nt)],
        out_specs=pl.BlockSpec((s, CONV_COLS), lambda j: (0, j)),
        out_shape=_sds((s, f), MXU_DTYPE), compiler_params=_params(("parallel",)),
    )(u0, u0, cw, cw, cb, cb)


def ffn_mid_bwd(u0, da, cw, cb):
    s, f2 = u0.shape
    f = f2 // 2
    nt = f // CONV_COLS
    rows = min(CONV_ROWS, s)
    nsteps = s // rows
    w = FFN_CONV

    def body(ug_ref, uu_ref, da_ref, wg_ref, wu_ref, bg_ref, bu_ref,
             dug_ref, duu_ref, dwg_ref, dwu_ref, dbg_ref, dbu_ref):
        zero8 = jnp.zeros((SUBLANES, CONV_COLS), F32)
        zrow = jnp.zeros((1, CONV_COLS), F32)

        def step(it, carry):
            ng, nu, accs = carry
            i = nsteps - 1 - it
            r0 = pl.multiple_of(i * rows, rows)
            g, sg = _conv_rows(ug_ref, wg_ref, bg_ref, i, rows, w)
            u, su = _conv_rows(uu_ref, wu_ref, bu_ref, i, rows, w)
            dav = da_ref[pl.ds(r0, rows), :]
            sg_val, sg_grad = _silu_and_grad(g)
            dg = dav * u * sg_grad
            du = dav * sg_val
            dug_ref[pl.ds(r0, rows), :] = _conv_bwd_rows(dg, ng, wg_ref, w).astype(dug_ref.dtype)
            duu_ref[pl.ds(r0, rows), :] = _conv_bwd_rows(du, nu, wu_ref, w).astype(duu_ref.dtype)
            new = []
            for j in range(w):
                new.append(accs[j] + jnp.sum(dg * sg[w - 1 - j], axis=0, keepdims=True))
            for j in range(w):
                new.append(accs[w + j] + jnp.sum(du * su[w - 1 - j], axis=0, keepdims=True))
            new.append(accs[2 * w] + jnp.sum(dg, axis=0, keepdims=True))
            new.append(accs[2 * w + 1] + jnp.sum(du, axis=0, keepdims=True))
            return dg[0:SUBLANES], du[0:SUBLANES], tuple(new)

        _, _, accs = lax.fori_loop(0, nsteps, step, (zero8, zero8, tuple([zrow] * (2 * w + 2))))
        dwg_ref[...] = jnp.concatenate(accs[0:w], axis=0)
        dwu_ref[...] = jnp.concatenate(accs[w:2 * w], axis=0)
        dbg_ref[...] = accs[2 * w]
        dbu_ref[...] = accs[2 * w + 1]

    col = lambda off: pl.BlockSpec((s, CONV_COLS), lambda j: (0, j + off))
    wsp = lambda r, off: pl.BlockSpec((r, CONV_COLS), lambda j: (0, j + off))
    outs = pl.pallas_call(
        body, name="ffn_mid_bwd", grid=(nt,),
        in_specs=[col(0), col(nt), col(0), wsp(w, 0), wsp(w, nt), wsp(1, 0), wsp(1, nt)],
        out_specs=[col(0), col(0), wsp(w, 0), wsp(w, 0), wsp(1, 0), wsp(1, 0)],
        out_shape=[_sds((s, f), MXU_DTYPE), _sds((s, f), MXU_DTYPE), _sds((w, f), F32), _sds((w, f), F32),
                   _sds((1, f), F32), _sds((1, f), F32)],
        compiler_params=_params(("parallel",)),
    )(u0, u0, da, cw, cw, cb, cb)
    dug, duu, dwg, dwu, dbg, dbu = outs
    return dug, duu, jnp.concatenate([dwg, dwu], axis=1), jnp.concatenate([dbg, dbu], axis=1)


SB_BLOCK = 128
SB_DEAD = 110.0


def _split_hi_lo(x):
    hi = x.astype(MXU_DTYPE)
    lo = (x - hi.astype(F32)).astype(MXU_DTYPE)
    return hi, lo


def _dot_exact01(x, tri):
    hi, lo = _split_hi_lo(x)
    return _dot(hi, tri) + _dot(lo, tri)


def _sb_tile(q, k, qi, kb, blk):
    z = _dot(q, k, _NT) * (SB_HEAD_DIM ** -0.5)
    rows = qi * blk + lax.broadcasted_iota(jnp.int32, (blk, blk), 0)
    cols = kb * blk + lax.broadcasted_iota(jnp.int32, (blk, blk), 1)
    strict = cols < rows
    t = jnp.log(1.0 + jnp.exp(-jnp.abs(z)))
    lb = jnp.minimum(z, 0.0) - t
    lf = jnp.where(strict, jnp.minimum(-z, 0.0) - t, 0.0)
    return lb, lf, strict


def _tri(blk, upper):
    r = lax.broadcasted_iota(jnp.int32, (blk, blk), 0)
    c = lax.broadcasted_iota(jnp.int32, (blk, blk), 1)
    return jnp.where((r > c) if upper else (r < c), 1.0, 0.0).astype(MXU_DTYPE)


def sb_fwd(qkv):
    s = qkv.shape[0]
    blk = min(SB_BLOCK, s)
    nblk = s // blk
    npair = SB_HEADS // 2
    dh = SB_HEAD_DIM

    def body(q_ref, k_ref, v_ref, o_ref):
        suffix_tri = _tri(blk, True)

        for head in range(2):
            sl = slice(head * dh, (head + 1) * dh)

            def qstep(qi, carry):
                q = q_ref[pl.ds(pl.multiple_of(qi * blk, blk), blk), sl]

                def kstep(st):
                    it, run, acc, _ = st
                    kb = qi - it
                    k0 = pl.multiple_of(kb * blk, blk)
                    lb, lf, strict = _sb_tile(q, k_ref[pl.ds(k0, blk), sl], qi, kb, blk)
                    sloc = _dot_exact01(lf, suffix_tri)
                    a = jnp.where(strict, jnp.exp(lb + sloc + run), 0.0)
                    acc = acc + _dot(_mx(a), v_ref[pl.ds(k0, blk), sl])
                    run = run + sloc[:, 0:1] + lf[:, 0:1]
                    return it + 1, run, acc, jnp.max(run) > -SB_DEAD

                _, _, acc, _ = lax.while_loop(
                    lambda st: jnp.logical_and(st[0] <= qi, st[3]), kstep,
                    (jnp.int32(0), jnp.zeros((blk, 1), F32), jnp.zeros((blk, dh), F32), jnp.bool_(True)))
                o_ref[pl.ds(pl.multiple_of(qi * blk, blk), blk), sl] = acc.astype(o_ref.dtype)
                return carry

            lax.fori_loop(0, nblk, qstep, 0)

    col = lambda off: pl.BlockSpec((s, 2 * dh), lambda p: (0, p + off))
    return pl.pallas_call(
        body, name="sb_fwd", grid=(npair,), in_specs=[col(0), col(npair), col(2 * npair)], out_specs=col(0),
        out_shape=_sds((s, D_MODEL), MXU_DTYPE), compiler_params=_params(("parallel",)),
    )(qkv, qkv, qkv)


def sb_bwd(qkv, do):
    s = qkv.shape[0]
    blk = min(SB_BLOCK, s)
    nblk = s // blk
    npair = SB_HEADS // 2
    dh = SB_HEAD_DIM

    def body(q_ref, k_ref, v_ref, do_ref, dq_ref, dk_ref, dv_ref, dk_acc, dv_acc, run_ref):
        suffix_tri = _tri(blk, True)
        prefix_tri = _tri(blk, False)
        dk_acc[...] = jnp.zeros_like(dk_acc)
        dv_acc[...] = jnp.zeros_like(dv_acc)

        for head in range(2):
            sl = slice(head * dh, (head + 1) * dh)

            def qstep(qi, carry):
                q0 = pl.multiple_of(qi * blk, blk)
                q = q_ref[pl.ds(q0, blk), sl]
                dov = do_ref[pl.ds(q0, blk), sl]

                def sweep1(st):
                    it, run, _ = st
                    kb = qi - it
                    run_ref[kb] = run
                    _, lf, _ = _sb_tile(q, k_ref[pl.ds(pl.multiple_of(kb * blk, blk), blk), sl], qi, kb, blk)
                    run = run + jnp.sum(lf, axis=1, keepdims=True)
                    return it + 1, run, jnp.max(run) > -SB_DEAD

                nlive, _, _ = lax.while_loop(
                    lambda st: jnp.logical_and(st[0] <= qi, st[2]), sweep1,
                    (jnp.int32(0), jnp.zeros((blk, 1), F32), jnp.bool_(True)))

                def sweep2(kb, st):
                    pg, dq = st
                    k0 = pl.multiple_of(kb * blk, blk)
                    kv = k_ref[pl.ds(k0, blk), sl]
                    lb, lf, strict = _sb_tile(q, kv, qi, kb, blk)
                    sloc = _dot_exact01(lf, suffix_tri)
                    a = jnp.where(strict, jnp.exp(lb + sloc + run_ref[kb]), 0.0)
                    g = _dot(dov, v_ref[pl.ds(k0, blk), sl], _NT) * a
                    p = pg + _dot_exact01(g, prefix_tri)
                    sig = jnp.exp(lb)
                    dz = _mx(jnp.where(strict, g * (1.0 - sig) - p * sig, 0.0) * (dh ** -0.5))
                    dk_acc[pl.ds(k0, blk), sl] += _dot(dz, q, _TN)
                    dv_acc[pl.ds(k0, blk), sl] += _dot(_mx(a), dov, _TN)
                    return pg + jnp.sum(g, axis=1, keepdims=True), dq + _dot(dz, kv)

                _, dq = lax.fori_loop(qi + 1 - nlive, qi + 1, sweep2, (jnp.zeros((blk, 1), F32), jnp.zeros((blk, dh), F32)))
                dq_ref[pl.ds(q0, blk), sl] = dq.astype(dq_ref.dtype)
                return carry

            lax.fori_loop(0, nblk, qstep, 0)

        dk_ref[...] = dk_acc[...].astype(dk_ref.dtype)
        dv_ref[...] = dv_acc[...].astype(dv_ref.dtype)

    col = lambda off: pl.BlockSpec((s, 2 * dh), lambda p: (0, p + off))
    dq, dk, dv = pl.pallas_call(
        body, name="sb_bwd", grid=(npair,), in_specs=[col(0), col(npair), col(2 * npair), col(0)],
        out_specs=[col(0), col(0), col(0)], out_shape=[_sds((s, D_MODEL), MXU_DTYPE)] * 3,
        scratch_shapes=[pltpu.VMEM((s, 2 * dh), F32), pltpu.VMEM((s, 2 * dh), F32), pltpu.VMEM((nblk, blk, 1), F32)],
        compiler_params=_params(("parallel",)),
    )(qkv, qkv, qkv, do)
    return jnp.concatenate([dq, dk, dv], axis=1)


SSD_XBC_TILE0 = SSD_D_INNER // CONV_COLS


def ssd_conv_fwd(proj, cw, cb):
    s = proj.shape[0]
    rows = min(CONV_ROWS, s)

    def body(u_ref, w_ref, b_ref, o_ref):
        def step(i, carry):
            u, _ = _conv_rows(u_ref, w_ref, b_ref, i, rows, SSD_CONV)
            o_ref[pl.ds(pl.multiple_of(i * rows, rows), rows), :] = _silu(u)
            return carry

        lax.fori_loop(0, s // rows, step, 0)

    return pl.pallas_call(
        body, name="ssd_conv_fwd", grid=(SSD_CONV_DIM // CONV_COLS,),
        in_specs=[pl.BlockSpec((s, CONV_COLS), lambda j: (0, j + SSD_XBC_TILE0)),
                  pl.BlockSpec((SSD_CONV, CONV_COLS), lambda j: (0, j)), pl.BlockSpec((1, CONV_COLS), lambda j: (0, j))],
        out_specs=pl.BlockSpec((s, CONV_COLS), lambda j: (0, j)),
        out_shape=_sds((s, SSD_CONV_DIM), F32), compiler_params=_params(("parallel",)),
    )(proj, cw, cb)


def ssd_conv_bwd(proj, dact, cw, cb):
    s = proj.shape[0]
    rows = min(CONV_ROWS, s)
    nsteps = s // rows
    w = SSD_CONV

    def body(u_ref, da_ref, w_ref, b_ref, du_ref, dw_ref, db_ref):
        def step(it, carry):
            nxt, accs = carry
            i = nsteps - 1 - it
            r0 = pl.multiple_of(i * rows, rows)
            u, sh = _conv_rows(u_ref, w_ref, b_ref, i, rows, w)
            dconv = da_ref[pl.ds(r0, rows), :] * _silu_and_grad(u)[1]
            du_ref[pl.ds(r0, rows), :] = _conv_bwd_rows(dconv, nxt, w_ref, w).astype(du_ref.dtype)
            new = [accs[j] + jnp.sum(dconv * sh[w - 1 - j], axis=0, keepdims=True) for j in range(w)]
            new.append(accs[w] + jnp.sum(dconv, axis=0, keepdims=True))
            return dconv[0:SUBLANES], tuple(new)

        zrow = jnp.zeros((1, CONV_COLS), F32)
        _, accs = lax.fori_loop(0, nsteps, step, (jnp.zeros((SUBLANES, CONV_COLS), F32), tuple([zrow] * (w + 1))))
        dw_ref[...] = jnp.concatenate(accs[0:w], axis=0)
        db_ref[...] = accs[w]

    col = pl.BlockSpec((s, CONV_COLS), lambda j: (0, j))
    return pl.pallas_call(
        body, name="ssd_conv_bwd", grid=(SSD_CONV_DIM // CONV_COLS,),
        in_specs=[pl.BlockSpec((s, CONV_COLS), lambda j: (0, j + SSD_XBC_TILE0)), col,
                  pl.BlockSpec((w, CONV_COLS), lambda j: (0, j)), pl.BlockSpec((1, CONV_COLS), lambda j: (0, j))],
        out_specs=[col, pl.BlockSpec((w, CONV_COLS), lambda j: (0, j)), pl.BlockSpec((1, CONV_COLS), lambda j: (0, j))],
        out_shape=[_sds((s, SSD_CONV_DIM), MXU_DTYPE), _sds((w, SSD_CONV_DIM), F32), _sds((1, SSD_CONV_DIM), F32)],
        compiler_params=_params(("parallel",)),
    )(proj, dact, cw, cb)


def _split3(x):
    hi = x.astype(MXU_DTYPE)
    r1 = x - hi.astype(F32)
    mid = r1.astype(MXU_DTYPE)
    lo = (r1 - mid.astype(F32)).astype(MXU_DTYPE)
    return hi, mid, lo


def _dot01(x, m, dims=_NN, left=False):
    parts = _split3(x)
    if left:
        return _dot(m, parts[0], dims) + _dot(m, parts[1], dims) + _dot(m, parts[2], dims)
    return _dot(parts[0], m, dims) + _dot(parts[1], m, dims) + _dot(parts[2], m, dims)


def _softplus(x):
    return jnp.maximum(x, 0.0) + jnp.log1p(jnp.exp(-jnp.abs(x)))


def _ssd_consts(dt_bias, a_log, d_skip):
    pad = lambda v: jnp.pad(v.reshape(1, SSD_HEADS), ((0, 0), (0, LANES - SSD_HEADS)))
    head_of = jnp.arange(SSD_D_INNER) // SSD_HEAD_DIM
    expand = (jnp.arange(LANES)[:, None] == head_of[None, :]).astype(MXU_DTYPE)
    return dict(bias_w=pad(dt_bias), alog_w=pad(a_log), bias_c=dt_bias.reshape(SSD_HEADS, 1),
                alog_c=a_log.reshape(SSD_HEADS, 1), dskip=jnp.repeat(d_skip, SSD_HEAD_DIM).reshape(1, SSD_D_INNER),
                expand=expand, reduce=expand.T)


def _ssd_chunk_prep(dtp, dtp_t, bias_w, alog_w, bias_c, alog_c, expand):
    L = dtp.shape[0]
    r = lax.broadcasted_iota(jnp.int32, (L, L), 0)
    c = lax.broadcasted_iota(jnp.int32, (L, L), 1)
    tril = r >= c
    lower = jnp.where(tril, 1.0, 0.0).astype(MXU_DTYPE)
    upper = jnp.where(r <= c, 1.0, 0.0).astype(MXU_DTYPE)
    dt_col = _softplus(dtp + bias_w)
    a_col = -jnp.exp(alog_w) * dt_col
    a_row = -jnp.exp(alog_c) * _softplus(dtp_t + bias_c)
    acum_col = _dot01(a_col, lower, left=True)
    acum_row = _dot01(a_row, upper)
    acum_full = _dot01(acum_col, expand)
    dt_full = _dot01(dt_col, expand)
    return dict(tril=tril, lower=lower, upper=upper, dt_col=dt_col, a_col=a_col, acum_col=acum_col,
                acum_row=acum_row, acum_full=acum_full, dt_full=dt_full)


def _head_mask(j):
    lane = lax.broadcasted_iota(jnp.int32, (1, LANES), 1)
    return jnp.where((lane // SSD_HEAD_DIM) == j, 1.0, 0.0)


def _decay(pre, h):
    seg = pre["acum_col"][:, h:h + 1] - pre["acum_row"][h:h + 1, :]
    return jnp.exp(jnp.where(pre["tril"], seg, -1e30))


def _ssd_specs(s, nc, rev):
    L = SSD_CHUNK
    ci = (lambda i: nc - 1 - i) if rev else (lambda i: i)
    const = lambda shape: pl.BlockSpec(shape, lambda i: (0,) * len(shape))
    return dict(
        xbc=pl.BlockSpec((L, SSD_CONV_DIM), lambda i: (ci(i), 0)),
        dtp=pl.BlockSpec((L, LANES), lambda i: (ci(i), SSD_IN_PAD // LANES - 1)),
        dtp_t=pl.BlockSpec((SSD_HEADS, L), lambda i: (0, ci(i))),
        rows=pl.BlockSpec((L, SSD_D_INNER), lambda i: (ci(i), 0)),
        state=pl.BlockSpec((1, SSD_GROUPS, SSD_STATE, 4 * SSD_HEAD_DIM), lambda i: (ci(i), 0, 0, 0)),
        consts=[const((1, LANES)), const((1, LANES)), const((SSD_HEADS, 1)), const((SSD_HEADS, 1)),
                const((1, SSD_D_INNER)), const((LANES, SSD_D_INNER)), const((SSD_D_INNER, LANES))],
    )


def _const_args(cs):
    return [cs["bias_w"], cs["alog_w"], cs["bias_c"], cs["alog_c"], cs["dskip"], cs["expand"], cs["reduce"]]


def ssd_scan_fwd(act, proj, dtp_t, cs):
    s = act.shape[0]
    L = SSD_CHUNK
    nc = s // L
    G, N, GW = SSD_GROUPS, SSD_STATE, 4 * SSD_HEAD_DIM

    def body(act_ref, dtp_ref, dtpt_ref, bw_ref, aw_ref, bc_ref, ac_ref, dsk_ref, ex_ref, rd_ref, y_ref, st_out, st):
        @pl.when(pl.program_id(0) == 0)
        def _():
            st[...] = jnp.zeros_like(st)

        st_out[0] = st[...]
        pre = _ssd_chunk_prep(dtp_ref[...], dtpt_ref[...], bw_ref[...], aw_ref[...], bc_ref[...], ac_ref[...], ex_ref[...])
        acum_full = pre["acum_full"]
        last_full = acum_full[L - 1:L, :]
        for g in range(G):
            bg = _mx(act_ref[:, SSD_D_INNER + g * N:SSD_D_INNER + (g + 1) * N])
            cg = _mx(act_ref[:, SSD_D_INNER + G * N + g * N:SSD_D_INNER + G * N + (g + 1) * N])
            cb = _dot(cg, bg, _NT)
            for half in range(2):
                p = 2 * g + half
                cols = slice(p * LANES, (p + 1) * LANES)
                xs = act_ref[:, cols]
                xdt = xs * pre["dt_full"][:, cols]
                yd = jnp.zeros((L, LANES), F32)
                for j in range(2):
                    m = cb * _decay(pre, 2 * p + j)
                    yd = yd + _dot(_mx(m), _mx(xdt * _head_mask(j)))
                yoff = _dot(cg, _mx(st[g, :, half * LANES:(half + 1) * LANES])) * jnp.exp(acum_full[:, cols])
                y_ref[:, cols] = yd + yoff + dsk_ref[:, cols] * xs
                w = jnp.exp(last_full[:, cols] - acum_full[:, cols])
                st[g, :, half * LANES:(half + 1) * LANES] = (
                    st[g, :, half * LANES:(half + 1) * LANES] * jnp.exp(last_full[:, cols]) + _dot(bg, _mx(xdt * w), _TN))

    sp = _ssd_specs(s, nc, False)
    return pl.pallas_call(
        body, name="ssd_scan_fwd", grid=(nc,),
        in_specs=[sp["xbc"], sp["dtp"], sp["dtp_t"]] + sp["consts"],
        out_specs=[sp["rows"], sp["state"]],
        out_shape=[_sds((s, SSD_D_INNER), F32), _sds((nc, G, N, GW), F32)],
        scratch_shapes=[pltpu.VMEM((G, N, GW), F32)],
        compiler_params=_params(("arbitrary",)),
    )(act, proj, dtp_t, *_const_args(cs))


def ssd_scan_bwd(act, proj, dtp_t, cs, states, dy):
    s = act.shape[0]
    L = SSD_CHUNK
    nc = s // L
    G, N, GW = SSD_GROUPS, SSD_STATE, 4 * SSD_HEAD_DIM

    def body(act_ref, dtp_ref, dtpt_ref, bw_ref, aw_ref, bc_ref, ac_ref, dsk_ref, ex_ref, rd_ref, st_ref, dy_ref,
             dact_ref, ddtp_ref, dalog_ref, dbias_ref, dskip_ref, dst, dxdt_ref, dac_ref):
        first = pl.program_id(0) == 0

        @pl.when(first)
        def _():
            dst[...] = jnp.zeros_like(dst)
            dalog_ref[...] = jnp.zeros_like(dalog_ref)
            dbias_ref[...] = jnp.zeros_like(dbias_ref)
            dskip_ref[...] = jnp.zeros_like(dskip_ref)

        expand, reduce = ex_ref[...], rd_ref[...]
        pre = _ssd_chunk_prep(dtp_ref[...], dtpt_ref[...], bw_ref[...], aw_ref[...], bc_ref[...], ac_ref[...], expand)
        acum_full = pre["acum_full"]
        last_full = acum_full[L - 1:L, :]
        ones = jnp.ones((L, LANES), MXU_DTYPE)
        lane = lax.broadcasted_iota(jnp.int32, (L, LANES), 1)
        dacum_diag = jnp.zeros((L, LANES), F32)
        dlast_parts = []
        for g in range(G):
            bg = _mx(act_ref[:, SSD_D_INNER + g * N:SSD_D_INNER + (g + 1) * N])
            cg = _mx(act_ref[:, SSD_D_INNER + G * N + g * N:SSD_D_INNER + G * N + (g + 1) * N])
            cb = _dot(cg, bg, _NT)
            dcb = jnp.zeros((L, L), F32)
            dcg = jnp.zeros((L, N), F32)
            dbg = jnp.zeros((L, N), F32)
            for half in range(2):
                p = 2 * g + half
                cols = slice(p * LANES, (p + 1) * LANES)
                hcols = slice(half * LANES, (half + 1) * LANES)
                xs = act_ref[:, cols]
                xdt = xs * pre["dt_full"][:, cols]
                dyv = dy_ref[:, cols]
                dxdt = jnp.zeros((L, LANES), F32)
                for j in range(2):
                    h = 2 * p + j
                    dec = _decay(pre, h)
                    m = cb * dec
                    dyh = _mx(dyv * _head_mask(j))
                    dm = _dot(dyh, _mx(xdt), _NT)
                    dxdt = dxdt + _dot(_mx(m), dyh, _TN)
                    e = dm * m
                    ehi, elo = _split_hi_lo(e)
                    d_h = (_dot(ehi, ones) + _dot(elo, ones)) - (_dot(ehi, ones, _TN) + _dot(elo, ones, _TN))
                    dacum_diag = jnp.where(lane == h, d_h, dacum_diag)
                    dcb = dcb + dm * dec
                lam = jnp.exp(acum_full[:, cols])
                stv = _mx(st_ref[0, g, :, hcols])
                z = _dot(cg, stv)
                dz = _mx(lam * dyv)
                dcg = dcg + _dot(dz, stv, _NT)
                dst_in = _dot(cg, dz, _TN)
                dsv = dst[g, :, hcols]
                w = jnp.exp(last_full[:, cols] - acum_full[:, cols])
                q = _dot(bg, _mx(dsv))
                wq = w * q
                dxdt = dxdt + wq
                wqx = wq * xdt
                dbg = dbg + _dot(_mx(xdt * w), _mx(dsv), _NT)
                elast = jnp.exp(last_full[:, cols])
                dlast_p = jnp.sum(wqx, axis=0, keepdims=True) + elast * jnp.sum(dsv * st_ref[0, g, :, hcols], axis=0, keepdims=True)
                dac_ref[:, cols] = dyv * z * lam - wqx
                dlast_parts.append(dlast_p)
                dst[g, :, hcols] = dst_in + dsv * elast
                dxdt_ref[:, cols] = dxdt
                dact_ref[:, cols] = dxdt * pre["dt_full"][:, cols] + dsk_ref[:, cols] * dyv
            dcbm = _mx(dcb)
            dact_ref[:, SSD_D_INNER + g * N:SSD_D_INNER + (g + 1) * N] = dbg + _dot(dcbm, cg, _TN)
            dact_ref[:, SSD_D_INNER + G * N + g * N:SSD_D_INNER + G * N + (g + 1) * N] = dcg + _dot(dcbm, bg)

        xs_all = act_ref[:, 0:SSD_D_INNER]
        dacum = dacum_diag + _dot_exact01(dac_ref[...], reduce)
        dlast = _dot_exact01(jnp.concatenate(dlast_parts, axis=1), reduce)
        row = lax.broadcasted_iota(jnp.int32, (L, LANES), 0)
        dacum = dacum + jnp.where(row == L - 1, dlast, 0.0)
        da_col = _dot01(dacum, pre["upper"], left=True)
        a_w = -jnp.exp(aw_ref[...])
        ddt = a_w * da_col + _dot_exact01(dxdt_ref[...] * xs_all, reduce)
        xin = dtp_ref[...] + bw_ref[...]
        ddtp = ddt * (1.0 / (1.0 + jnp.exp(-xin)))
        valid = lane < SSD_HEADS
        ddtp = jnp.where(valid, ddtp, 0.0)
        ddtp_ref[...] = ddtp
        dbias_ref[...] += jnp.sum(ddtp, axis=0, keepdims=True)
        dalog_ref[...] += jnp.sum(jnp.where(valid, da_col * pre["a_col"], 0.0), axis=0, keepdims=True)
        dskip_ref[...] += jnp.sum(_dot_exact01(dy_ref[...] * xs_all, reduce), axis=0, keepdims=True)

    sp = _ssd_specs(s, nc, True)
    acc = pl.BlockSpec((1, LANES), lambda i: (0, 0))
    return pl.pallas_call(
        body, name="ssd_scan_bwd", grid=(nc,),
        in_specs=[sp["xbc"], sp["dtp"], sp["dtp_t"]] + sp["consts"] + [sp["state"], sp["rows"]],
        out_specs=[sp["xbc"], pl.BlockSpec((L, LANES), lambda i: (nc - 1 - i, 0)), acc, acc, acc],
        out_shape=[_sds((s, SSD_CONV_DIM), F32), _sds((s, LANES), F32)] + [_sds((1, LANES), F32)] * 3,
        scratch_shapes=[pltpu.VMEM((G, N, GW), F32), pltpu.VMEM((L, SSD_D_INNER), F32), pltpu.VMEM((L, SSD_D_INNER), F32)],
        compiler_params=_params(("arbitrary",)),
    )(act, proj, dtp_t, *_const_args(cs), states, dy)


def ssd_post_fwd(y, proj, g):
    s, d = y.shape
    ts = _pick(s, (256, 128))

    def body(y_ref, z_ref, g_ref, o_ref):
        y2 = y_ref[...] * _silu(z_ref[...])
        r = lax.rsqrt(jnp.mean(y2 * y2, axis=-1, keepdims=True) + NORM_EPS)
        o_ref[...] = (y2 * r * g_ref[...]).astype(o_ref.dtype)

    row = pl.BlockSpec((ts, d), lambda i: (i, 0))
    return pl.pallas_call(
        body, name="ssd_post_fwd", grid=(s // ts,), in_specs=[row, row, pl.BlockSpec((1, d), lambda i: (0, 0))],
        out_specs=row, out_shape=_sds((s, d), MXU_DTYPE), compiler_params=_params(("parallel",)),
    )(y, proj, g)


def ssd_post_bwd(y, proj, g, dy3):
    s, d = y.shape
    ts = _pick(s, (256, 128))

    def body(y_ref, z_ref, g_ref, d3_ref, dy_ref, dz_ref, dg_ref):
        yv, zv = y_ref[...], z_ref[...]
        sz, sgrad = _silu_and_grad(zv)
        y2 = yv * sz
        r = lax.rsqrt(jnp.mean(y2 * y2, axis=-1, keepdims=True) + NORM_EPS)
        xh = y2 * r
        d3 = d3_ref[...]
        dxh = d3 * g_ref[...]
        dy2 = r * (dxh - xh * jnp.mean(dxh * xh, axis=-1, keepdims=True))
        dy_ref[...] = dy2 * sz
        dz_ref[...] = (dy2 * yv * sgrad).astype(dz_ref.dtype)
        part = jnp.sum(d3 * xh, axis=0, keepdims=True)

        @pl.when(pl.program_id(0) == 0)
        def _():
            dg_ref[...] = part

        @pl.when(pl.program_id(0) != 0)
        def _():
            dg_ref[...] += part

    row = pl.BlockSpec((ts, d), lambda i: (i, 0))
    vec = pl.BlockSpec((1, d), lambda i: (0, 0))
    return pl.pallas_call(
        body, name="ssd_post_bwd", grid=(s // ts,), in_specs=[row, row, vec, row], out_specs=[row, row, vec],
        out_shape=[_sds((s, d), F32), _sds((s, d), MXU_DTYPE), _sds((1, d), F32)],
        compiler_params=_params(("arbitrary",)),
    )(y, proj, g, dy3)


def ssd_core_fwd(proj, cw, cb, dt_bias, a_log, d_skip, norm_g):
    cs = _ssd_consts(dt_bias, a_log, d_skip)
    act = ssd_conv_fwd(proj, cw, cb)
    dtp_t = proj[:, SSD_IN_DIM - SSD_HEADS:SSD_IN_DIM].T
    y, states = ssd_scan_fwd(act, proj, dtp_t, cs)
    y3 = ssd_post_fwd(y, proj, norm_g)
    return y3, (cs, act, dtp_t, y, states)


def ssd_core_bwd(proj, cw, cb, norm_g, saved, dy3):
    cs, act, dtp_t, y, states = saved
    dy, dz, dnorm = ssd_post_bwd(y, proj, norm_g, dy3)
    dact, ddtp, dalog, dbias, dskip = ssd_scan_bwd(act, proj, dtp_t, cs, states, dy)
    dxbc, dcw, dcb = ssd_conv_bwd(proj, dact, cw, cb)
    dproj = jnp.concatenate([dz, dxbc, ddtp.astype(MXU_DTYPE)], axis=1)
    h = SSD_HEADS
    return dproj, dcw, dcb, dbias[0, :h], dalog[0, :h], dskip[0, :h], dnorm


def ssd_core(proj, cw, cb, dt_bias, a_log, d_skip, norm_g, dy3):
    y3, saved = ssd_core_fwd(proj, cw, cb, dt_bias, a_log, d_skip, norm_g)
    return y3, ssd_core_bwd(proj, cw, cb, norm_g, saved, dy3)


def loss_head(x, g, target):
    s, d = x.shape
    ts = _pick(s, (512, 256, 128))

    def body(x_ref, g_ref, t_ref, loss_ref, dx_ref, dg_ref):
        xv = x_ref[...]
        r = lax.rsqrt(jnp.mean(xv * xv, axis=-1, keepdims=True) + NORM_EPS)
        xh = xv * r
        err = xh * g_ref[...] - t_ref[...]
        dy = err * (1.0 / d)
        dxh = dy * g_ref[...]
        dx_ref[...] = r * (dxh - xh * jnp.mean(dxh * xh, axis=-1, keepdims=True))
        part = jnp.sum(dy * xh, axis=0, keepdims=True)
        lpart = jnp.full((1, LANES), 0.5 * jnp.sum(jnp.mean(err * err, axis=-1, keepdims=True)), F32)

        @pl.when(pl.program_id(0) == 0)
        def _():
            dg_ref[...] = part
            loss_ref[...] = lpart

        @pl.when(pl.program_id(0) != 0)
        def _():
            dg_ref[...] += part
            loss_ref[...] += lpart

    row = pl.BlockSpec((ts, d), lambda i: (i, 0))
    vec = pl.BlockSpec((1, d), lambda i: (0, 0))
    return pl.pallas_call(
        body, name="loss_head", grid=(s // ts,), in_specs=[row, vec, row],
        out_specs=[pl.BlockSpec((1, LANES), lambda i: (0, 0)), row, vec],
        out_shape=[_sds((1, LANES), F32), _sds((s, d), F32), _sds((1, d), F32)],
        compiler_params=_params(("arbitrary",)),
    )(x, g, target)


def _adamw_math(w, g, m, v):
    m = ADAM_B1 * m + (1.0 - ADAM_B1) * g
    v = ADAM_B2 * v + (1.0 - ADAM_B2) * (g * g)
    m_hat = m / (1.0 - ADAM_B1 ** ADAM_STEP)
    v_hat = v / (1.0 - ADAM_B2 ** ADAM_STEP)
    return -ADAM_LR * (m_hat / (jnp.sqrt(v_hat) + ADAM_EPS) + ADAM_WD * w), m, v


def adamw(w, g, m, v, name="adamw"):
    r, c = w.shape
    tr = _pick(r, (256, 128, 64, 32, 16, 8))

    def body(w_ref, g_ref, m_ref, v_ref, d_ref, nm_ref, nv_ref):
        d_ref[...], nm_ref[...], nv_ref[...] = _adamw_math(w_ref[...], g_ref[...], m_ref[...], v_ref[...])

    blk = pl.BlockSpec((tr, c), lambda i: (i, 0))
    return pl.pallas_call(
        body, name=name, grid=(r // tr,), in_specs=[blk] * 4, out_specs=[blk] * 3,
        out_shape=[_sds((r, c), F32)] * 3, compiler_params=_params(("parallel",)),
    )(w, g, m, v)


def adamw_small(w, parts, m, v):
    n, r, c = parts.shape

    def body(w_ref, p_ref, m_ref, v_ref, g_ref, d_ref, nm_ref, nv_ref):
        g = p_ref[0]
        for k in range(1, n):
            g = g + p_ref[k]
        g_ref[...] = g
        d_ref[...], nm_ref[...], nv_ref[...] = _adamw_math(w_ref[...], g, m_ref[...], v_ref[...])

    return pl.pallas_call(
        body, name="adamw_small", out_shape=[_sds((r, c), F32)] * 4, compiler_params=_params(),
    )(w, parts, m, v)


def pair_sum(unit, recv, half):
    nchip, _, r, c = unit.shape
    tr = _pick(r, (512, 256, 176, 128, 64, 32, 16))

    def body(h_ref, a_ref, b_ref, o_ref, ob_ref):
        sm = a_ref[0, 0] + b_ref[0]
        o_ref[0] = sm
        ob_ref[0] = sm.astype(ob_ref.dtype)

    blk = pl.BlockSpec((1, tr, c), lambda s, i, h: (s, i, 0))
    return pl.pallas_call(
        body, name="pair_sum",
        grid_spec=pltpu.PrefetchScalarGridSpec(
            num_scalar_prefetch=1, grid=(nchip, r // tr),
            in_specs=[pl.BlockSpec((1, 1, tr, c), lambda s, i, h: (s, h[0], i, 0)), blk], out_specs=[blk, blk]),
        out_shape=[_sds((nchip, r, c), F32), _sds((nchip, r, c), jnp.bfloat16)],
        compiler_params=_params(("parallel", "parallel")),
    )(half, unit, recv)


def chip_sum(own, chip, recv):
    _, r, c = own.shape
    tr = _pick(r, (512, 256, 176, 128, 64, 32, 16))

    def body(s_ref, a_ref, b_ref, o_ref):
        o_ref[...] = a_ref[0] + b_ref[0].astype(F32) + b_ref[1].astype(F32) + b_ref[2].astype(F32)

    return pl.pallas_call(
        body, name="chip_sum",
        grid_spec=pltpu.PrefetchScalarGridSpec(
            num_scalar_prefetch=1, grid=(r // tr,),
            in_specs=[pl.BlockSpec((1, tr, c), lambda i, s: (s[0], i, 0)), pl.BlockSpec((3, tr, c), lambda i, s: (0, i, 0))],
            out_specs=pl.BlockSpec((tr, c), lambda i, s: (i, 0))),
        out_shape=_sds((r, c), F32), compiler_params=_params(("parallel",)),
    )(chip, own, recv)


ANY = pl.BlockSpec(memory_space=pl.ANY)
COMM = pltpu.CompilerParams(has_side_effects=True)


def _coords():
    return lax.axis_index("x"), lax.axis_index("y"), lax.axis_index("c")


def _other_chips(x, y):
    return [(1 - x, y), (x, 1 - y), (1 - x, 1 - y)]


def all_gather_8(halves, name):
    _, r, c = halves.shape

    def body(h_ref, out_ref, send_sems, recv_sems, local_sem):
        x, y, cc = _coords()
        _gather_one(h_ref.at[cc], lambda px, py, pc: out_ref.at[4 * px + 2 * py + pc],
                    lambda k: send_sems.at[k], lambda k: recv_sems.at[k], local_sem)

    return pl.pallas_call(
        body, name=name, in_specs=[ANY], out_specs=ANY, out_shape=_sds((8, r, c), halves.dtype),
        scratch_shapes=[pltpu.SemaphoreType.DMA((7,)), pltpu.SemaphoreType.DMA((7,)), pltpu.SemaphoreType.DMA],
        compiler_params=COMM,
    )(halves)


def _gather_plan(x_ref, slot, send_sem, recv_sem, local_sem):
    x, y, cc = _coords()
    me, sibling = (x, y, cc), (x, y, 1 - cc)
    chips = _other_chips(x, y)

    def copy(k, blk, to, src=None):
        return pltpu.make_async_remote_copy(
            src_ref=slot(*blk) if src is None else src, dst_ref=slot(*blk),
            send_sem=send_sem(k), recv_sem=recv_sem(k), device_id=to, device_id_type=MESH)

    mine = pltpu.make_async_copy(x_ref, slot(*me), local_sem)
    first = [copy(0, me, sibling, src=x_ref)] + [copy(1 + j, me, (*chip, cc), src=x_ref) for j, chip in enumerate(chips)]
    passed = [copy(4 + j, (*chip, cc), sibling) for j, chip in enumerate(chips)]
    over_ici = [copy(1 + j, (*chip, cc), me) for j, chip in enumerate(chips)]
    from_sibling = [copy(0, sibling, me)] + [copy(4 + j, (*chip, 1 - cc), me) for j, chip in enumerate(chips)]
    return mine, first, passed, over_ici, from_sibling


def _gather_run(plans):
    for mine, first, _, _, _ in plans:
        mine.start()
        for cp in first:
            cp.start()
    for j in range(3):
        for _, _, passed, over_ici, _ in plans:
            over_ici[j].wait_recv()
            passed[j].start()
    for mine, first, passed, _, from_sibling in plans:
        for cp in from_sibling:
            cp.wait_recv()
        for cp in first + passed:
            cp.wait_send()
        mine.wait()


def _gather_one(x_ref, slot, send_sem, recv_sem, local_sem):
    _gather_run([_gather_plan(x_ref, slot, send_sem, recv_sem, local_sem)])


def gather_big(shards):
    n = len(shards)

    def body(*refs):
        ins, outs = refs[:n], refs[n:2 * n]
        send_sems, recv_sems, local_sems = refs[2 * n:]
        _, _, cc = _coords()
        plans = []
        for i in range(n):
            lh = shards[i].shape[0] // 2
            slot = lambda px, py, pc, i=i, lh=lh: outs[i].at[pl.ds(pc * lh, lh), 2 * px + py]
            plans.append(_gather_plan(ins[i].at[pl.ds(cc * lh, lh)], slot, lambda k, i=i: send_sems.at[i, k],
                                      lambda k, i=i: recv_sems.at[i, k], local_sems.at[i]))
        _gather_run(plans)

    return pl.pallas_call(
        body, name="gather_big", in_specs=[ANY] * n, out_specs=[ANY] * n,
        out_shape=[_sds((a.shape[0], N_CHIPS) + a.shape[1:], a.dtype) for a in shards],
        scratch_shapes=[pltpu.SemaphoreType.DMA((n, 7)), pltpu.SemaphoreType.DMA((n, 7)), pltpu.SemaphoreType.DMA((n,))],
        compiler_params=COMM,
    )(*shards)


def grad_pair_swap(units):
    n = len(units)

    def body(*refs):
        ins, outs, send_sems, recv_sems = refs[:n], refs[n:2 * n], refs[2 * n], refs[2 * n + 1]
        x, y, cc = _coords()
        cps = [pltpu.make_async_remote_copy(src_ref=ins[i].at[:, 1 - cc], dst_ref=outs[i], send_sem=send_sems.at[i],
                                            recv_sem=recv_sems.at[i], device_id=(x, y, 1 - cc), device_id_type=MESH)
               for i in range(n)]
        for cp in cps:
            cp.start()
        for cp in cps:
            cp.wait()

    return pl.pallas_call(
        body, name="grad_pair_swap", in_specs=[ANY] * n, out_specs=[ANY] * n,
        out_shape=[_sds((u.shape[0],) + u.shape[2:], u.dtype) for u in units],
        scratch_shapes=[pltpu.SemaphoreType.DMA((n,)), pltpu.SemaphoreType.DMA((n,))], compiler_params=COMM,
    )(*units)


def grad_chip_exchange(units):
    n = len(units)

    def body(*refs):
        ins, outs, send_sems, recv_sems = refs[:n], refs[n:2 * n], refs[2 * n], refs[2 * n + 1]
        x, y, cc = _coords()
        cps = [pltpu.make_async_remote_copy(
            src_ref=ins[i].at[2 * px + py], dst_ref=outs[i].at[k], send_sem=send_sems.at[i, k],
            recv_sem=recv_sems.at[i, k], device_id=(px, py, cc), device_id_type=MESH)
            for i in range(n) for k, (px, py) in enumerate(_other_chips(x, y))]
        for cp in cps:
            cp.start()
        for cp in cps:
            cp.wait()

    return pl.pallas_call(
        body, name="grad_chip_exchange", in_specs=[ANY] * n, out_specs=[ANY] * n,
        out_shape=[_sds((3,) + u.shape[1:], u.dtype) for u in units],
        scratch_shapes=[pltpu.SemaphoreType.DMA((n, 3)), pltpu.SemaphoreType.DMA((n, 3))], compiler_params=COMM,
    )(*units)


def grad_half_swap(halves, layout):
    n = len(halves)
    nw = 1 + max(wi for wi, _ in layout)
    layers = [1 + max(l for wi, l in layout if wi == k) for k in range(nw)]
    first = [min(i for i, (wi, _) in enumerate(layout) if wi == k) for k in range(nw)]

    def body(*refs):
        ins, outs = refs[:n], refs[n:n + nw]
        send_sems, recv_sems, local_sems = refs[n + nw:]
        x, y, cc = _coords()
        local, remote = [], []
        for i, (wi, l) in enumerate(layout):
            local.append(pltpu.make_async_copy(ins[i], outs[wi].at[l, cc], local_sems.at[i]))
            remote.append(pltpu.make_async_remote_copy(
                src_ref=ins[i], dst_ref=outs[wi].at[l, cc], send_sem=send_sems.at[i], recv_sem=recv_sems.at[i],
                device_id=(x, y, 1 - cc), device_id_type=MESH))
        for cp in local + remote:
            cp.start()
        for cp in local + remote:
            cp.wait()

    return pl.pallas_call(
        body, name="grad_half_swap", in_specs=[ANY] * n, out_specs=[ANY] * nw,
        out_shape=[_sds((layers[k], 2) + halves[first[k]].shape, F32) for k in range(nw)],
        scratch_shapes=[pltpu.SemaphoreType.DMA((n,)), pltpu.SemaphoreType.DMA((n,)), pltpu.SemaphoreType.DMA((n,))],
        compiler_params=COMM,
    )(*halves)


N_CHIPS = 4
PACK_COLS = 1024
BIG = ("ssd_w_in", "ssd_w_out", "sb_w_qkv", "sb_w_out", "ffn_w_in", "ffn_w_out")
CONVW = ("ssd_conv_w", "ffn_conv_w")
COL_SHARDED = ("ssd_w_in", "sb_w_qkv", "ffn_w_in", "ssd_conv_w", "ffn_conv_w")
SMALL = ("mix_norm", "ffn_norm", "final_norm", "ssd_conv_b", "ssd_dt_bias", "ssd_a_log", "ssd_d", "ssd_norm", "ffn_conv_b")
WEIGHTS = ("mix_norm", "ffn_norm", "final_norm", "ssd_w_in", "ssd_conv_w", "ssd_conv_b", "ssd_dt_bias", "ssd_a_log",
           "ssd_d", "ssd_norm", "ssd_w_out", "sb_w_qkv", "sb_w_out", "ffn_w_in", "ffn_conv_w", "ffn_conv_b", "ffn_w_out")


def _to_rows(flat, multiple):
    rows = -(-flat.shape[-1] // PACK_COLS)
    rows = -(-rows // multiple) * multiple
    pad = rows * PACK_COLS - flat.shape[-1]
    return jnp.pad(flat, [(0, pad)]).reshape(rows, PACK_COLS)


def _unshard(name, stacked):
    l, n, a, b = stacked.shape
    if name in COL_SHARDED:
        return jnp.transpose(stacked, (0, 2, 1, 3)).reshape(l, a, n * b)
    return stacked.reshape(l, n * a, b)


def _gather_conv_weights(w):
    flat = jnp.concatenate([w[n].reshape(-1) for n in CONVW])
    rows = _to_rows(flat, 16)
    got = all_gather_8(rows.reshape(2, rows.shape[0] // 2, PACK_COLS), "gather_conv_weights").reshape(N_CHIPS, -1)
    out, off = {}, 0
    for n in CONVW:
        l, a, b = w[n].shape
        out[n] = _unshard(n, jnp.moveaxis(got[:, off:off + w[n].size].reshape(N_CHIPS, l, a, b), 0, 1))
        off += w[n].size
    return out


def _reduce_big_grads(units, layout):
    cc = lax.axis_index("c").reshape(1).astype(jnp.int32)
    chip = (2 * lax.axis_index("x") + lax.axis_index("y")).reshape(1).astype(jnp.int32)
    from_sibling = grad_pair_swap(units)
    pairs = [pair_sum(u, r, cc) for u, r in zip(units, from_sibling)]
    from_chips = grad_chip_exchange([p[1] for p in pairs])
    halves = [chip_sum(p[0], chip, r) for p, r in zip(pairs, from_chips)]
    return grad_half_swap(halves, layout)


def kernel(x, mix_norm, ffn_norm, final_norm, ssd_w_in, ssd_conv_w, ssd_conv_b, ssd_dt_bias, ssd_a_log, ssd_d, ssd_norm, ssd_w_out, sb_w_qkv, sb_w_out, ffn_w_in, ffn_conv_w, ffn_conv_b, ffn_w_out, loss_target, m_mix_norm, m_ffn_norm, m_final_norm, m_ssd_w_in, m_ssd_conv_w, m_ssd_conv_b, m_ssd_dt_bias, m_ssd_a_log, m_ssd_d, m_ssd_norm, m_ssd_w_out, m_sb_w_qkv, m_sb_w_out, m_ffn_w_in, m_ffn_conv_w, m_ffn_conv_b, m_ffn_w_out, v_mix_norm, v_ffn_norm, v_final_norm, v_ssd_w_in, v_ssd_conv_w, v_ssd_conv_b, v_ssd_dt_bias, v_ssd_a_log, v_ssd_d, v_ssd_norm, v_ssd_w_out, v_sb_w_qkv, v_sb_w_out, v_ffn_w_in, v_ffn_conv_w, v_ffn_conv_b, v_ffn_w_out):
    given = dict(locals())
    w = {n: given[n] for n in WEIGHTS}
    mom = {n: given["m_" + n] for n in WEIGHTS}
    var = {n: given["v_" + n] for n in WEIGHTS}
    chip = 2 * lax.axis_index("x") + lax.axis_index("y")

    gathered = gather_big([w[n].astype(MXU_DTYPE) for n in BIG])
    fw = {n: _unshard(n, g) for n, g in zip(BIG, gathered)}
    fw.update(_gather_conv_weights(w))
    w_ssd_in = jnp.pad(fw["ssd_w_in"], ((0, 0), (0, 0), (0, SSD_IN_PAD - SSD_IN_DIM)))
    row = lambda v: v.reshape(1, -1)

    xcur = x[0]
    saved = []
    for i in range(DEPTH):
        j = i // 2
        h, r = rms_fwd(xcur, row(mix_norm[i]))
        if i % 2 == 0:
            proj = mm(h, w_ssd_in[j], tn=896, name="mm_ssd_in")
            y3, core = ssd_core_fwd(proj, fw["ssd_conv_w"][j], row(ssd_conv_b[j]), ssd_dt_bias[j], ssd_a_log[j], ssd_d[j], row(ssd_norm[j]))
            x1 = mm(y3, fw["ssd_w_out"][j], res=xcur, name="mm_ssd_out")
            mix = (proj, y3, core)
        else:
            qkv = mm(h, fw["sb_w_qkv"][j], out_dtype=MXU_DTYPE, name="mm_sb_qkv")
            o = sb_fwd(qkv)
            x1 = mm(o, fw["sb_w_out"][j], res=xcur, name="mm_sb_out")
            mix = (qkv, o)
        h2, r2 = rms_fwd(x1, row(ffn_norm[i]))
        u0 = mm(h2, fw["ffn_w_in"][i], name="mm_ffn_in")
        a = ffn_mid_fwd(u0, fw["ffn_conv_w"][i], row(ffn_conv_b[i]))
        x2 = mm(a, fw["ffn_w_out"][i], res=x1, name="mm_ffn_out")
        saved.append((xcur, h, r, mix, x1, h2, r2, u0, a))
        xcur = x2
    loss_part, dx, d_final = loss_head(xcur, row(final_norm), loss_target[0])

    gl = {n: [None] * w[n].shape[0] for n in WEIGHTS if n != "final_norm"}
    units = {n: [None] * w[n].shape[0] for n in BIG}

    def unit_of(g4):
        return g4.reshape(N_CHIPS, 2, g4.shape[1] // 2, g4.shape[2])

    for i in reversed(range(DEPTH)):
        j = i // 2
        x0, h, r, mix, x1, h2, r2, u0, a = saved[i]
        units["ffn_w_out"][i] = unit_of(mm(a, dx, "tn", name="mm_d_ffn_out").reshape(N_CHIPS, -1, D_MODEL))
        da = mm(dx, fw["ffn_w_out"][i], "nt", name="mm_da_ffn")
        dug, duu, gl["ffn_conv_w"][i], dcb = ffn_mid_bwd(u0, da, fw["ffn_conv_w"][i], row(ffn_conv_b[i]))
        gl["ffn_conv_b"][i] = dcb[0]
        du0 = jnp.concatenate([dug, duu], axis=1)
        units["ffn_w_in"][i] = unit_of(mm(h2, du0, "tn", tn=1408, tm=512, n_split=N_CHIPS, name="mm_d_ffn_in"))
        dh2 = mm(du0, fw["ffn_w_in"][i], "nt", name="mm_dh_ffn")
        dx1, dg = rms_bwd(x1, r2, row(ffn_norm[i]), dh2, dx)
        gl["ffn_norm"][i] = dg[0]
        if i % 2 == 0:
            proj, y3, core = mix
            units["ssd_w_out"][j] = unit_of(mm(y3, dx1, "tn", name="mm_d_ssd_out").reshape(N_CHIPS, -1, D_MODEL))
            dy3 = mm(dx1, fw["ssd_w_out"][j], "nt", name="mm_dy3_ssd")
            dproj, gl["ssd_conv_w"][j], dcb, gl["ssd_dt_bias"][j], gl["ssd_a_log"][j], gl["ssd_d"][j], dnorm = ssd_core_bwd(
                proj, fw["ssd_conv_w"][j], row(ssd_conv_b[j]), row(ssd_norm[j]), core, dy3)
            gl["ssd_conv_b"][j] = dcb[0]
            gl["ssd_norm"][j] = dnorm[0]
            dw_in = mm(h, dproj, "tn", tn=896, name="mm_d_ssd_in")[:, :SSD_IN_DIM]
            units["ssd_w_in"][j] = unit_of(jnp.transpose(dw_in.reshape(D_MODEL, N_CHIPS, -1), (1, 0, 2)))
            dh = mm(dproj, w_ssd_in[j], "nt", name="mm_dh_ssd")
        else:
            qkv, o = mix
            units["sb_w_out"][j] = unit_of(mm(o, dx1, "tn", name="mm_d_sb_out").reshape(N_CHIPS, -1, D_MODEL))
            do = mm(dx1, fw["sb_w_out"][j], "nt", out_dtype=MXU_DTYPE, name="mm_do_sb")
            dqkv = sb_bwd(qkv, do)
            units["sb_w_qkv"][j] = unit_of(mm(h, dqkv, "tn", tn=768, n_split=N_CHIPS, name="mm_d_sb_qkv"))
            dh = mm(dqkv, fw["sb_w_qkv"][j], "nt", name="mm_dh_sb")
        dx, dg = rms_bwd(x0, r, row(mix_norm[i]), dh, dx1)
        gl["mix_norm"][i] = dg[0]

    layout = [(k, l) for k, n in enumerate(BIG) for l in range(w[n].shape[0])]
    reduced = _reduce_big_grads([units[BIG[k]][l] for k, l in layout], layout)
    g, delta, new_m, new_v = {}, {}, {}, {}
    two_d = lambda t: t.reshape(-1, t.shape[-1])
    for n, red in zip(BIG, reduced):
        g[n] = red.reshape(w[n].shape)
        d2, m2, v2 = adamw(two_d(w[n]), two_d(g[n]), two_d(mom[n]), two_d(var[n]), name="adamw_" + n)
        delta[n], new_m[n], new_v[n] = d2.reshape(w[n].shape), m2.reshape(w[n].shape), v2.reshape(w[n].shape)

    small_g = {n: jnp.stack(gl[n]) for n in SMALL + CONVW if n != "final_norm"}
    small_g["final_norm"] = d_final[0]
    zeros_of = lambda n: jnp.zeros((small_g[n].size,), F32)

    def small_pack(d, extra):
        parts = [d[n].reshape(-1) for n in SMALL] + [extra]
        parts += [(d[n].reshape(-1) if d is small_g else zeros_of(n)) for n in CONVW]
        return _to_rows(jnp.concatenate(parts), 16)

    part = small_pack(small_g, loss_part[0, 0:1])
    parts = all_gather_8(jnp.stack([part, part]), "gather_small_grads")
    zero = jnp.zeros((1,), F32)
    gs, ds, ms, vs = adamw_small(small_pack(w, zero), parts, small_pack(mom, zero), small_pack(var, zero))
    gs_flat = gs.reshape(-1)
    off = 0
    for n in SMALL:
        size = w[n].size
        for dst, src in ((g, gs), (delta, ds), (new_m, ms), (new_v, vs)):
            dst[n] = src.reshape(-1)[off:off + size].reshape(w[n].shape)
        off += size
    loss = gs_flat[off]
    off += 1
    for n in CONVW:
        size = small_g[n].size
        b = w[n].shape[-1]
        g[n] = lax.dynamic_slice_in_dim(gs_flat[off:off + size].reshape(small_g[n].shape), chip * b, b, axis=2)
        d2, m2, v2 = adamw(two_d(w[n]), two_d(g[n]), two_d(mom[n]), two_d(var[n]), name="adamw_" + n)
        delta[n], new_m[n], new_v[n] = d2.reshape(w[n].shape), m2.reshape(w[n].shape), v2.reshape(w[n].shape)
        off += size

    return (loss, dx[None], *[g[n] for n in WEIGHTS], *[delta[n] for n in WEIGHTS],
            *[new_m[n] for n in WEIGHTS], *[new_v[n] for n in WEIGHTS])
```

```python
import functools

import jax
import jax.numpy as jnp
from jax import lax
from jax.experimental import pallas as pl
from jax.experimental.pallas import tpu as pltpu

F32 = jnp.float32
MXU_DTYPE = jnp.bfloat16
HIGHEST = lax.Precision.HIGHEST

D_MODEL = 1024
DEPTH = 4
NORM_EPS = 1e-6
SSD_D_INNER = 2048
SSD_HEAD_DIM = 64
SSD_HEADS = 32
SSD_GROUPS = 8
SSD_STATE = 128
SSD_CONV = 4
SSD_CHUNK = 128
SSD_CONV_DIM = 4096
SSD_IN_DIM = 6176
SSD_IN_PAD = 6272
SB_HEADS = 16
SB_HEAD_DIM = 64
FFN_D_FF = 2816
FFN_CONV = 3
ADAM_LR, ADAM_B1, ADAM_B2, ADAM_EPS, ADAM_WD, ADAM_STEP = 0.001, 0.9, 0.999, 1e-08, 0.01, 10

LANES = 128
SUBLANES = 8
VMEM_LIMIT = 56 * 1024 * 1024
MESH = pl.DeviceIdType.MESH


def _params(sem=None):
    return pltpu.CompilerParams(dimension_semantics=sem, vmem_limit_bytes=VMEM_LIMIT)


def _sds(shape, dtype):
    return jax.ShapeDtypeStruct(shape, dtype)


def _dot(a, b, dims=(((1,), (0,)), ((), ())), precision=None):
    return lax.dot_general(a, b, dims, precision=precision, preferred_element_type=F32)


_NN = (((1,), (0,)), ((), ()))
_NT = (((1,), (1,)), ((), ()))
_TN = (((0,), (0,)), ((), ()))


def _mx(a):
    return a.astype(MXU_DTYPE)


def _silu(x):
    return x * (1.0 / (1.0 + jnp.exp(-x)))


def _silu_and_grad(x):
    s = 1.0 / (1.0 + jnp.exp(-x))
    return x * s, s * (1.0 + x * (1.0 - s))


def _pick(n, cands):
    for c in cands:
        if n % c == 0:
            return c
    return n


def rms_fwd(x, g):
    s, d = x.shape
    ts = _pick(s, (512, 256, 128))

    def body(x_ref, g_ref, h_ref, r_ref):
        xv = x_ref[...]
        r = lax.rsqrt(jnp.mean(xv * xv, axis=-1, keepdims=True) + NORM_EPS)
        h_ref[...] = (xv * r * g_ref[...]).astype(h_ref.dtype)
        r_ref[...] = r

    return pl.pallas_call(
        body, name="rms_fwd", grid=(s // ts,),
        in_specs=[pl.BlockSpec((ts, d), lambda i: (i, 0)), pl.BlockSpec((1, d), lambda i: (0, 0))],
        out_specs=[pl.BlockSpec((ts, d), lambda i: (i, 0)), pl.BlockSpec((ts, 1), lambda i: (i, 0))],
        out_shape=[_sds((s, d), MXU_DTYPE), _sds((s, 1), F32)],
        compiler_params=_params(("parallel",)),
    )(x, g)


def rms_bwd(x, r, g, dh, dres):
    s, d = x.shape
    ts = _pick(s, (512, 256, 128))

    def body(x_ref, r_ref, g_ref, dh_ref, dres_ref, dx_ref, dg_ref):
        xh = x_ref[...] * r_ref[...]
        dhv = dh_ref[...]
        dxh = dhv * g_ref[...]
        dx_ref[...] = dres_ref[...] + r_ref[...] * (dxh - xh * jnp.mean(dxh * xh, axis=-1, keepdims=True))
        part = jnp.sum(dhv * xh, axis=0, keepdims=True)

        @pl.when(pl.program_id(0) == 0)
        def _():
            dg_ref[...] = part

        @pl.when(pl.program_id(0) != 0)
        def _():
            dg_ref[...] += part

    row = pl.BlockSpec((ts, d), lambda i: (i, 0))
    return pl.pallas_call(
        body, name="rms_bwd", grid=(s // ts,),
        in_specs=[row, pl.BlockSpec((ts, 1), lambda i: (i, 0)), pl.BlockSpec((1, d), lambda i: (0, 0)), row, row],
        out_specs=[row, pl.BlockSpec((1, d), lambda i: (0, 0))],
        out_shape=[_sds((s, d), F32), _sds((1, d), F32)],
        compiler_params=_params(("arbitrary",)),
    )(x, r, g, dh, dres)


def mm(a, b, mode="nn", res=None, out_dtype=F32, tm=None, tn=None, n_split=1, name="mm"):
    if mode == "nn":
        (m, k), (_, n) = a.shape, b.shape
    elif mode == "nt":
        (m, k), (n, _) = a.shape, b.shape
    else:
        (k, m), (_, n) = a.shape, b.shape
    tm = tm or _pick(m, (1024, 512, 256, 128))
    tn = tn or _pick(n, (512, 896, 256, 128))
    dims = {"nn": _NN, "nt": _NT, "tn": _TN}[mode]

    def body(*refs):
        a_ref, b_ref = refs[0], refs[1]
        o_ref = refs[-1]
        acc = _dot(_mx(a_ref[...]), _mx(b_ref[...]), dims)
        if res is not None:
            acc = acc + refs[2][...]
        o_ref[...] = acc.astype(o_ref.dtype)

    a_spec = pl.BlockSpec((k, tm), lambda i, j: (0, i)) if mode == "tn" else pl.BlockSpec((tm, k), lambda i, j: (i, 0))
    b_spec = pl.BlockSpec((tn, k), lambda i, j: (j, 0)) if mode == "nt" else pl.BlockSpec((k, tn), lambda i, j: (0, j))
    o_spec = pl.BlockSpec((tm, tn), lambda i, j: (i, j))
    ins, specs = [a, b], [a_spec, b_spec]
    if res is not None:
        ins.append(res)
        specs.append(o_spec)
    out_shape = _sds((m, n), out_dtype)
    if n_split > 1:
        per = n // n_split // tn
        o_spec = pl.BlockSpec((None, tm, tn), lambda i, j: (j // per, i, j % per))
        out_shape = _sds((n_split, m, n // n_split), out_dtype)
    return pl.pallas_call(
        body, name=name, grid=(m // tm, n // tn), in_specs=specs, out_specs=o_spec,
        out_shape=out_shape, compiler_params=_params(("parallel", "parallel")),
    )(*ins)


CONV_ROWS = 256
CONV_COLS = 128


def _row_iota8(cols):
    return lax.broadcasted_iota(jnp.int32, (SUBLANES, cols), 0)


def _shift_down(cur, prev8, k):
    if k == 0:
        return cur
    rolled = pltpu.roll(cur, k, 0)
    head = jnp.where(_row_iota8(cur.shape[1]) < k, pltpu.roll(prev8, k, 0), rolled[0:SUBLANES])
    return jnp.concatenate([head, rolled[SUBLANES:]], axis=0)


def _shift_up(cur, next8, k):
    if k == 0:
        return cur
    n = cur.shape[0]
    rolled = pltpu.roll(cur, n - k, 0)
    tail = jnp.where(_row_iota8(cur.shape[1]) >= SUBLANES - k, pltpu.roll(next8, SUBLANES - k, 0), rolled[n - SUBLANES:])
    return jnp.concatenate([rolled[:n - SUBLANES], tail], axis=0)


def _load_prev8(ref, i, rows):
    start = pl.multiple_of(jnp.maximum(i * rows - SUBLANES, 0), SUBLANES)
    p = ref[pl.ds(start, SUBLANES), :]
    return jnp.where(i > 0, p, jnp.zeros_like(p))


def _conv_rows(ref, w_ref, b_ref, i, rows, width):
    cur = ref[pl.ds(pl.multiple_of(i * rows, rows), rows), :]
    prev8 = _load_prev8(ref, i, rows)
    shifted = [_shift_down(cur, prev8, k) for k in range(width)]
    acc = b_ref[...] + w_ref[width - 1:width, :] * shifted[0]
    for k in range(1, width):
        acc = acc + w_ref[width - 1 - k:width - k, :] * shifted[k]
    return acc, shifted


def _conv_bwd_rows(du, next8, w_ref, width):
    acc = w_ref[width - 1:width, :] * du
    for k in range(1, width):
        acc = acc + w_ref[width - 1 - k:width - k, :] * _shift_up(du, next8, k)
    return acc


def ffn_mid_fwd(u0, cw, cb):
    s, f2 = u0.shape
    f = f2 // 2
    nt = f // CONV_COLS
    rows = min(CONV_ROWS, s)

    def body(ug_ref, uu_ref, wg_ref, wu_ref, bg_ref, bu_ref, a_ref):
        def step(i, carry):
            g, _ = _conv_rows(ug_ref, wg_ref, bg_ref, i, rows, FFN_CONV)
            u, _ = _conv_rows(uu_ref, wu_ref, bu_ref, i, rows, FFN_CONV)
            a_ref[pl.ds(pl.multiple_of(i * rows, rows), rows), :] = (_silu(g) * u).astype(a_ref.dtype)
            return carry

        lax.fori_loop(0, s // rows, step, 0)

    col = lambda off: pl.BlockSpec((s, CONV_COLS), lambda j: (0, j + off))
    wsp = lambda r, off: pl.BlockSpec((r, CONV_COLS), lambda j: (0, j + off))
    return pl.pallas_call(
        body, name="ffn_mid_fwd", grid=(nt,),
        in_specs=[col(0), col(nt), wsp(FFN_CONV, 0), wsp(FFN_CONV, nt), wsp(1, 0), wsp(1, nt)],
        out_specs=pl.BlockSpec((s, CONV_COLS), lambda j: (0, j)),
        out_shape=_sds((s, f), MXU_DTYPE), compiler_params=_params(("parallel",)),
    )(u0, u0, cw, cw, cb, cb)


def ffn_mid_bwd(u0, da, cw, cb):
    s, f2 = u0.shape
    f = f2 // 2
    nt = f // CONV_COLS
    rows = min(CONV_ROWS, s)
    nsteps = s // rows
    w = FFN_CONV

    def body(ug_ref, uu_ref, da_ref, wg_ref, wu_ref, bg_ref, bu_ref,
             dug_ref, duu_ref, dwg_ref, dwu_ref, dbg_ref, dbu_ref):
        zero8 = jnp.zeros((SUBLANES, CONV_COLS), F32)
        zrow = jnp.zeros((1, CONV_COLS), F32)

        def step(it, carry):
            ng, nu, accs = carry
            i = nsteps - 1 - it
            r0 = pl.multiple_of(i * rows, rows)
            g, sg = _conv_rows(ug_ref, wg_ref, bg_ref, i, rows, w)
            u, su = _conv_rows(uu_ref, wu_ref, bu_ref, i, rows, w)
            dav = da_ref[pl.ds(r0, rows), :]
            sg_val, sg_grad = _silu_and_grad(g)
            dg = dav * u * sg_grad
            du = dav * sg_val
            dug_ref[pl.ds(r0, rows), :] = _conv_bwd_rows(dg, ng, wg_ref, w).astype(dug_ref.dtype)
            duu_ref[pl.ds(r0, rows), :] = _conv_bwd_rows(du, nu, wu_ref, w).astype(duu_ref.dtype)
            new = []
            for j in range(w):
                new.append(accs[j] + jnp.sum(dg * sg[w - 1 - j], axis=0, keepdims=True))
            for j in range(w):
                new.append(accs[w + j] + jnp.sum(du * su[w - 1 - j], axis=0, keepdims=True))
            new.append(accs[2 * w] + jnp.sum(dg, axis=0, keepdims=True))
            new.append(accs[2 * w + 1] + jnp.sum(du, axis=0, keepdims=True))
            return dg[0:SUBLANES], du[0:SUBLANES], tuple(new)

        _, _, accs = lax.fori_loop(0, nsteps, step, (zero8, zero8, tuple([zrow] * (2 * w + 2))))
        dwg_ref[...] = jnp.concatenate(accs[0:w], axis=0)
        dwu_ref[...] = jnp.concatenate(accs[w:2 * w], axis=0)
        dbg_ref[...] = accs[2 * w]
        dbu_ref[...] = accs[2 * w + 1]

    col = lambda off: pl.BlockSpec((s, CONV_COLS), lambda j: (0, j + off))
    wsp = lambda r, off: pl.BlockSpec((r, CONV_COLS), lambda j: (0, j + off))
    outs = pl.pallas_call(
        body, name="ffn_mid_bwd", grid=(nt,),
        in_specs=[col(0), col(nt), col(0), wsp(w, 0), wsp(w, nt), wsp(1, 0), wsp(1, nt)],
        out_specs=[col(0), col(0), wsp(w, 0), wsp(w, 0), wsp(1, 0), wsp(1, 0)],
        out_shape=[_sds((s, f), MXU_DTYPE), _sds((s, f), MXU_DTYPE), _sds((w, f), F32), _sds((w, f), F32),
                   _sds((1, f), F32), _sds((1, f), F32)],
        compiler_params=_params(("parallel",)),
    )(u0, u0, da, cw, cw, cb, cb)
    dug, duu, dwg, dwu, dbg, dbu = outs
    return dug, duu, jnp.concatenate([dwg, dwu], axis=1), jnp.concatenate([dbg, dbu], axis=1)


SB_BLOCK = 128
SB_DEAD = 110.0
SB_HEADS_PER_STEP = 4


def _split_hi_lo(x):
    hi = x.astype(MXU_DTYPE)
    lo = (x - hi.astype(F32)).astype(MXU_DTYPE)
    return hi, lo


def _dot_exact01(x, tri):
    hi, lo = _split_hi_lo(x)
    return _dot(hi, tri) + _dot(lo, tri)


def _stack_heads(pair, lane_lo):
    zero = jnp.zeros_like(pair)
    return jnp.concatenate([jnp.where(lane_lo, pair, zero), jnp.where(lane_lo, zero, pair)], axis=0)


def _unstack_heads(tall, lane_lo):
    n = tall.shape[0] // 2
    return jnp.where(lane_lo, tall[:n], tall[n:])


def _sb_logits(stacked_q, k_ref, k0, pair_cols, blk):
    z = [_dot(sq, k_ref[pl.ds(k0, blk), cols], _NT) for sq, cols in zip(stacked_q, pair_cols)]
    return jnp.concatenate(z, axis=0) * (SB_HEAD_DIM ** -0.5)


def _sb_logs(z, qi, kb, blk):
    rows = qi * blk + (lax.broadcasted_iota(jnp.int32, z.shape, 0) & (blk - 1))
    cols = kb * blk + lax.broadcasted_iota(jnp.int32, z.shape, 1)
    strict = cols < rows
    t = jnp.log(1.0 + jnp.exp(-jnp.abs(z)))
    lb = jnp.minimum(z, 0.0) - t
    lf = jnp.where(strict, jnp.minimum(-z, 0.0) - t, 0.0)
    return lb, lf, strict


def _tri(blk, upper):
    r = lax.broadcasted_iota(jnp.int32, (blk, blk), 0)
    c = lax.broadcasted_iota(jnp.int32, (blk, blk), 1)
    return jnp.where((r > c) if upper else (r < c), 1.0, 0.0).astype(MXU_DTYPE)


def _tri_sum(x, tri2):
    hi, lo = _split_hi_lo(x)
    return _dot(jnp.concatenate([hi, lo], axis=1), tri2)


def sb_fwd(qkv):
    s = qkv.shape[0]
    blk = min(SB_BLOCK, s)
    nblk = s // blk
    nh = SB_HEADS_PER_STEP
    nstep = SB_HEADS // nh
    dh = SB_HEAD_DIM

    def body(q_ref, k_ref, v_ref, o_ref):
        suffix_tri2 = jnp.concatenate([_tri(blk, True)] * 2, axis=0)
        lane_lo = lax.broadcasted_iota(jnp.int32, (1, LANES), 1) < dh
        pairs = [slice(p * LANES, (p + 1) * LANES) for p in range(nh // 2)]

        def qstep(qi, carry):
            q0 = pl.multiple_of(qi * blk, blk)
            qst = [_stack_heads(q_ref[pl.ds(q0, blk), cols], lane_lo) for cols in pairs]

            def kstep(st):
                it, run, accs, _ = st
                kb = qi - it
                k0 = pl.multiple_of(kb * blk, blk)
                lb, lf, strict = _sb_logs(_sb_logits(qst, k_ref, k0, pairs, blk), qi, kb, blk)
                sloc = _tri_sum(lf, suffix_tri2)
                a = _mx(jnp.where(strict, jnp.exp(lb + sloc + run), 0.0))
                accs = tuple(
                    acc + _unstack_heads(_dot(a[2 * blk * p:2 * blk * (p + 1)], v_ref[pl.ds(k0, blk), cols]), lane_lo)
                    for p, (acc, cols) in enumerate(zip(accs, pairs)))
                run = run + sloc[:, 0:1] + lf[:, 0:1]
                return it + 1, run, accs, jnp.max(run) > -SB_DEAD

            _, _, accs, _ = lax.while_loop(
                lambda st: jnp.logical_and(st[0] <= qi, st[3]), kstep,
                (jnp.int32(0), jnp.zeros((nh * blk, 1), F32), tuple([jnp.zeros((blk, LANES), F32)] * len(pairs)),
                 jnp.bool_(True)))
            for acc, cols in zip(accs, pairs):
                o_ref[pl.ds(q0, blk), cols] = acc.astype(o_ref.dtype)
            return carry

        lax.fori_loop(0, nblk, qstep, 0)

    col = lambda off: pl.BlockSpec((s, nh * dh), lambda p: (0, p + off))
    return pl.pallas_call(
        body, name="sb_fwd", grid=(nstep,), in_specs=[col(0), col(nstep), col(2 * nstep)], out_specs=col(0),
        out_shape=_sds((s, D_MODEL), MXU_DTYPE), compiler_params=_params(("parallel",)),
    )(qkv, qkv, qkv)


def sb_bwd(qkv, do):
    s = qkv.shape[0]
    blk = min(SB_BLOCK, s)
    nblk = s // blk
    nh = SB_HEADS_PER_STEP
    nstep = SB_HEADS // nh
    dh = SB_HEAD_DIM

    def body(q_ref, k_ref, v_ref, do_ref, dq_ref, dk_ref, dv_ref, dk_acc, dv_acc, run_ref):
        suffix_tri2 = jnp.concatenate([_tri(blk, True)] * 2, axis=0)
        prefix_tri2 = jnp.concatenate([_tri(blk, False)] * 2, axis=0)
        dk_acc[...] = jnp.zeros_like(dk_acc)
        dv_acc[...] = jnp.zeros_like(dv_acc)
        lane_lo = lax.broadcasted_iota(jnp.int32, (1, LANES), 1) < dh
        pairs = [slice(p * LANES, (p + 1) * LANES) for p in range(nh // 2)]

        def qstep(qi, carry):
            q0 = pl.multiple_of(qi * blk, blk)
            qst = [_stack_heads(q_ref[pl.ds(q0, blk), cols], lane_lo) for cols in pairs]
            dost = [_stack_heads(do_ref[pl.ds(q0, blk), cols], lane_lo) for cols in pairs]

            def sweep1(st):
                it, run, _ = st
                kb = qi - it
                run_ref[kb] = run
                _, lf, _ = _sb_logs(_sb_logits(qst, k_ref, pl.multiple_of(kb * blk, blk), pairs, blk), qi, kb, blk)
                run = run + jnp.sum(lf, axis=1, keepdims=True)
                return it + 1, run, jnp.max(run) > -SB_DEAD

            nlive, _, _ = lax.while_loop(
                lambda st: jnp.logical_and(st[0] <= qi, st[2]), sweep1,
                (jnp.int32(0), jnp.zeros((nh * blk, 1), F32), jnp.bool_(True)))

            def sweep2(kb, st):
                pg, dqs = st
                k0 = pl.multiple_of(kb * blk, blk)
                lb, lf, strict = _sb_logs(_sb_logits(qst, k_ref, k0, pairs, blk), qi, kb, blk)
                sloc = _tri_sum(lf, suffix_tri2)
                a = jnp.where(strict, jnp.exp(lb + sloc + run_ref[kb]), 0.0)
                da = jnp.concatenate([_dot(d, v_ref[pl.ds(k0, blk), cols], _NT) for d, cols in zip(dost, pairs)], axis=0)
                g = da * a
                p = pg + _tri_sum(g, prefix_tri2)
                sig = jnp.exp(lb)
                dz = _mx(jnp.where(strict, g * (1.0 - sig) - p * sig, 0.0) * (dh ** -0.5))
                am = _mx(a)
                new_dqs = []
                for i, cols in enumerate(pairs):
                    rows = slice(2 * blk * i, 2 * blk * (i + 1))
                    new_dqs.append(dqs[i] + _unstack_heads(_dot(dz[rows], k_ref[pl.ds(k0, blk), cols]), lane_lo))
                    dk_acc[pl.ds(k0, blk), cols] += _dot(dz[rows], qst[i], _TN)
                    dv_acc[pl.ds(k0, blk), cols] += _dot(am[rows], dost[i], _TN)
                return pg + jnp.sum(g, axis=1, keepdims=True), tuple(new_dqs)

            _, dqs = lax.fori_loop(qi + 1 - nlive, qi + 1, sweep2,
                                   (jnp.zeros((nh * blk, 1), F32), tuple([jnp.zeros((blk, LANES), F32)] * len(pairs))))
            for dq, cols in zip(dqs, pairs):
                dq_ref[pl.ds(q0, blk), cols] = dq.astype(dq_ref.dtype)
            return carry

        lax.fori_loop(0, nblk, qstep, 0)
        dk_ref[...] = dk_acc[...].astype(dk_ref.dtype)
        dv_ref[...] = dv_acc[...].astype(dv_ref.dtype)

    col = lambda off: pl.BlockSpec((s, nh * dh), lambda p: (0, p + off))
    dq, dk, dv = pl.pallas_call(
        body, name="sb_bwd", grid=(nstep,), in_specs=[col(0), col(nstep), col(2 * nstep), col(0)],
        out_specs=[col(0), col(0), col(0)], out_shape=[_sds((s, D_MODEL), MXU_DTYPE)] * 3,
        scratch_shapes=[pltpu.VMEM((s, nh * dh), F32), pltpu.VMEM((s, nh * dh), F32), pltpu.VMEM((nblk, nh * blk, 1), F32)],
        compiler_params=_params(("parallel",)),
    )(qkv, qkv, qkv, do)
    return jnp.concatenate([dq, dk, dv], axis=1)


SSD_XBC_TILE0 = SSD_D_INNER // CONV_COLS


def ssd_conv_fwd(proj, cw, cb):
    s = proj.shape[0]
    rows = min(CONV_ROWS, s)

    def body(u_ref, w_ref, b_ref, o_ref):
        def step(i, carry):
            u, _ = _conv_rows(u_ref, w_ref, b_ref, i, rows, SSD_CONV)
            o_ref[pl.ds(pl.multiple_of(i * rows, rows), rows), :] = _silu(u)
            return carry

        lax.fori_loop(0, s // rows, step, 0)

    return pl.pallas_call(
        body, name="ssd_conv_fwd", grid=(SSD_CONV_DIM // CONV_COLS,),
        in_specs=[pl.BlockSpec((s, CONV_COLS), lambda j: (0, j + SSD_XBC_TILE0)),
                  pl.BlockSpec((SSD_CONV, CONV_COLS), lambda j: (0, j)), pl.BlockSpec((1, CONV_COLS), lambda j: (0, j))],
        out_specs=pl.BlockSpec((s, CONV_COLS), lambda j: (0, j)),
        out_shape=_sds((s, SSD_CONV_DIM), F32), compiler_params=_params(("parallel",)),
    )(proj, cw, cb)


def ssd_conv_bwd(proj, dact, cw, cb):
    s = proj.shape[0]
    rows = min(CONV_ROWS, s)
    nsteps = s // rows
    w = SSD_CONV

    def body(u_ref, da_ref, w_ref, b_ref, du_ref, dw_ref, db_ref):
        def step(it, carry):
            nxt, accs = carry
            i = nsteps - 1 - it
            r0 = pl.multiple_of(i * rows, rows)
            u, sh = _conv_rows(u_ref, w_ref, b_ref, i, rows, w)
            dconv = da_ref[pl.ds(r0, rows), :] * _silu_and_grad(u)[1]
            du_ref[pl.ds(r0, rows), :] = _conv_bwd_rows(dconv, nxt, w_ref, w).astype(du_ref.dtype)
            new = [accs[j] + jnp.sum(dconv * sh[w - 1 - j], axis=0, keepdims=True) for j in range(w)]
            new.append(accs[w] + jnp.sum(dconv, axis=0, keepdims=True))
            return dconv[0:SUBLANES], tuple(new)

        zrow = jnp.zeros((1, CONV_COLS), F32)
        _, accs = lax.fori_loop(0, nsteps, step, (jnp.zeros((SUBLANES, CONV_COLS), F32), tuple([zrow] * (w + 1))))
        dw_ref[...] = jnp.concatenate(accs[0:w], axis=0)
        db_ref[...] = accs[w]

    col = pl.BlockSpec((s, CONV_COLS), lambda j: (0, j))
    return pl.pallas_call(
        body, name="ssd_conv_bwd", grid=(SSD_CONV_DIM // CONV_COLS,),
        in_specs=[pl.BlockSpec((s, CONV_COLS), lambda j: (0, j + SSD_XBC_TILE0)), col,
                  pl.BlockSpec((w, CONV_COLS), lambda j: (0, j)), pl.BlockSpec((1, CONV_COLS), lambda j: (0, j))],
        out_specs=[col, pl.BlockSpec((w, CONV_COLS), lambda j: (0, j)), pl.BlockSpec((1, CONV_COLS), lambda j: (0, j))],
        out_shape=[_sds((s, SSD_CONV_DIM), MXU_DTYPE), _sds((w, SSD_CONV_DIM), F32), _sds((1, SSD_CONV_DIM), F32)],
        compiler_params=_params(("parallel",)),
    )(proj, dact, cw, cb)


def _split3(x):
    hi = x.astype(MXU_DTYPE)
    r1 = x - hi.astype(F32)
    mid = r1.astype(MXU_DTYPE)
    lo = (r1 - mid.astype(F32)).astype(MXU_DTYPE)
    return hi, mid, lo


def _dot01(x, m, dims=_NN, left=False):
    parts = _split3(x)
    if left:
        return _dot(m, parts[0], dims) + _dot(m, parts[1], dims) + _dot(m, parts[2], dims)
    return _dot(parts[0], m, dims) + _dot(parts[1], m, dims) + _dot(parts[2], m, dims)


def _softplus(x):
    return jnp.maximum(x, 0.0) + jnp.log1p(jnp.exp(-jnp.abs(x)))


def _ssd_consts(dt_bias, a_log, d_skip):
    pad = lambda v: jnp.pad(v.reshape(1, SSD_HEADS), ((0, 0), (0, LANES - SSD_HEADS)))
    head_of = jnp.arange(SSD_D_INNER) // SSD_HEAD_DIM
    expand = (jnp.arange(LANES)[:, None] == head_of[None, :]).astype(MXU_DTYPE)
    return dict(bias_w=pad(dt_bias), alog_w=pad(a_log), bias_c=dt_bias.reshape(SSD_HEADS, 1),
                alog_c=a_log.reshape(SSD_HEADS, 1), dskip=jnp.repeat(d_skip, SSD_HEAD_DIM).reshape(1, SSD_D_INNER),
                expand=expand, reduce=expand.T)


def _ssd_chunk_prep(dtp, dtp_t, bias_w, alog_w, bias_c, alog_c, expand):
    L = dtp.shape[0]
    r = lax.broadcasted_iota(jnp.int32, (L, L), 0)
    c = lax.broadcasted_iota(jnp.int32, (L, L), 1)
    tril = r >= c
    lower = jnp.where(tril, 1.0, 0.0).astype(MXU_DTYPE)
    upper = jnp.where(r <= c, 1.0, 0.0).astype(MXU_DTYPE)
    dt_col = _softplus(dtp + bias_w)
    a_col = -jnp.exp(alog_w) * dt_col
    a_row = -jnp.exp(alog_c) * _softplus(dtp_t + bias_c)
    acum_col = _dot01(a_col, lower, left=True)
    acum_row = _dot01(a_row, upper)
    acum_full = _dot01(acum_col, expand)
    dt_full = _dot01(dt_col, expand)
    return dict(tril=tril, lower=lower, upper=upper, dt_col=dt_col, a_col=a_col, acum_col=acum_col,
                acum_row=acum_row, acum_full=acum_full, dt_full=dt_full)


def _head_mask(j):
    lane = lax.broadcasted_iota(jnp.int32, (1, LANES), 1)
    return jnp.where((lane // SSD_HEAD_DIM) == j, 1.0, 0.0)


def _decay(pre, h):
    seg = pre["acum_col"][:, h:h + 1] - pre["acum_row"][h:h + 1, :]
    return jnp.exp(jnp.where(pre["tril"], seg, -1e30))


def _ssd_specs(s, nc, rev):
    L = SSD_CHUNK
    ci = (lambda i: nc - 1 - i) if rev else (lambda i: i)
    const = lambda shape: pl.BlockSpec(shape, lambda i: (0,) * len(shape))
    return dict(
        xbc=pl.BlockSpec((L, SSD_CONV_DIM), lambda i: (ci(i), 0)),
        dtp=pl.BlockSpec((L, LANES), lambda i: (ci(i), SSD_IN_PAD // LANES - 1)),
        dtp_t=pl.BlockSpec((SSD_HEADS, L), lambda i: (0, ci(i))),
        rows=pl.BlockSpec((L, SSD_D_INNER), lambda i: (ci(i), 0)),
        state=pl.BlockSpec((1, SSD_GROUPS, SSD_STATE, 4 * SSD_HEAD_DIM), lambda i: (ci(i), 0, 0, 0)),
        consts=[const((1, LANES)), const((1, LANES)), const((SSD_HEADS, 1)), const((SSD_HEADS, 1)),
                const((1, SSD_D_INNER)), const((LANES, SSD_D_INNER)), const((SSD_D_INNER, LANES))],
    )


def _const_args(cs):
    return [cs["bias_w"], cs["alog_w"], cs["bias_c"], cs["alog_c"], cs["dskip"], cs["expand"], cs["reduce"]]


def ssd_scan_fwd(act, proj, dtp_t, cs):
    s = act.shape[0]
    L = SSD_CHUNK
    nc = s // L
    G, N, GW = SSD_GROUPS, SSD_STATE, 4 * SSD_HEAD_DIM

    def body(act_ref, dtp_ref, dtpt_ref, bw_ref, aw_ref, bc_ref, ac_ref, dsk_ref, ex_ref, rd_ref, y_ref, st_out, st):
        @pl.when(pl.program_id(0) == 0)
        def _():
            st[...] = jnp.zeros_like(st)

        st_out[0] = st[...]
        pre = _ssd_chunk_prep(dtp_ref[...], dtpt_ref[...], bw_ref[...], aw_ref[...], bc_ref[...], ac_ref[...], ex_ref[...])
        acum_full = pre["acum_full"]
        last_full = acum_full[L - 1:L, :]
        for g in range(G):
            bg = _mx(act_ref[:, SSD_D_INNER + g * N:SSD_D_INNER + (g + 1) * N])
            cg = _mx(act_ref[:, SSD_D_INNER + G * N + g * N:SSD_D_INNER + G * N + (g + 1) * N])
            cb = _dot(cg, bg, _NT)
            for half in range(2):
                p = 2 * g + half
                cols = slice(p * LANES, (p + 1) * LANES)
                xs = act_ref[:, cols]
                xdt = xs * pre["dt_full"][:, cols]
                yd = jnp.zeros((L, LANES), F32)
                for j in range(2):
                    m = cb * _decay(pre, 2 * p + j)
                    yd = yd + _dot(_mx(m), _mx(xdt * _head_mask(j)))
                yoff = _dot(cg, _mx(st[g, :, half * LANES:(half + 1) * LANES])) * jnp.exp(acum_full[:, cols])
                y_ref[:, cols] = yd + yoff + dsk_ref[:, cols] * xs
                w = jnp.exp(last_full[:, cols] - acum_full[:, cols])
                st[g, :, half * LANES:(half + 1) * LANES] = (
                    st[g, :, half * LANES:(half + 1) * LANES] * jnp.exp(last_full[:, cols]) + _dot(bg, _mx(xdt * w), _TN))

    sp = _ssd_specs(s, nc, False)
    return pl.pallas_call(
        body, name="ssd_scan_fwd", grid=(nc,),
        in_specs=[sp["xbc"], sp["dtp"], sp["dtp_t"]] + sp["consts"],
        out_specs=[sp["rows"], sp["state"]],
        out_shape=[_sds((s, SSD_D_INNER), F32), _sds((nc, G, N, GW), F32)],
        scratch_shapes=[pltpu.VMEM((G, N, GW), F32)],
        compiler_params=_params(("arbitrary",)),
    )(act, proj, dtp_t, *_const_args(cs))


def ssd_scan_bwd(act, proj, dtp_t, cs, states, dy):
    s = act.shape[0]
    L = SSD_CHUNK
    nc = s // L
    G, N, GW = SSD_GROUPS, SSD_STATE, 4 * SSD_HEAD_DIM

    def body(act_ref, dtp_ref, dtpt_ref, bw_ref, aw_ref, bc_ref, ac_ref, dsk_ref, ex_ref, rd_ref, st_ref, dy_ref,
             dact_ref, ddtp_ref, dalog_ref, dbias_ref, dskip_ref, dst, dxdt_ref, dac_ref):
        first = pl.program_id(0) == 0

        @pl.when(first)
        def _():
            dst[...] = jnp.zeros_like(dst)
            dalog_ref[...] = jnp.zeros_like(dalog_ref)
            dbias_ref[...] = jnp.zeros_like(dbias_ref)
            dskip_ref[...] = jnp.zeros_like(dskip_ref)

        expand, reduce = ex_ref[...], rd_ref[...]
        pre = _ssd_chunk_prep(dtp_ref[...], dtpt_ref[...], bw_ref[...], aw_ref[...], bc_ref[...], ac_ref[...], expand)
        acum_full = pre["acum_full"]
        last_full = acum_full[L - 1:L, :]
        ones = jnp.ones((L, LANES), MXU_DTYPE)
        lane = lax.broadcasted_iota(jnp.int32, (L, LANES), 1)
        dacum_diag = jnp.zeros((L, LANES), F32)
        dlast_parts = []
        for g in range(G):
            bg = _mx(act_ref[:, SSD_D_INNER + g * N:SSD_D_INNER + (g + 1) * N])
            cg = _mx(act_ref[:, SSD_D_INNER + G * N + g * N:SSD_D_INNER + G * N + (g + 1) * N])
            cb = _dot(cg, bg, _NT)
            dcb = jnp.zeros((L, L), F32)
            dcg = jnp.zeros((L, N), F32)
            dbg = jnp.zeros((L, N), F32)
            for half in range(2):
                p = 2 * g + half
                cols = slice(p * LANES, (p + 1) * LANES)
                hcols = slice(half * LANES, (half + 1) * LANES)
                xs = act_ref[:, cols]
                xdt = xs * pre["dt_full"][:, cols]
                dyv = dy_ref[:, cols]
                dxdt = jnp.zeros((L, LANES), F32)
                for j in range(2):
                    h = 2 * p + j
                    dec = _decay(pre, h)
                    m = cb * dec
                    dyh = _mx(dyv * _head_mask(j))
                    dm = _dot(dyh, _mx(xdt), _NT)
                    dxdt = dxdt + _dot(_mx(m), dyh, _TN)
                    e = dm * m
                    ehi, elo = _split_hi_lo(e)
                    d_h = (_dot(ehi, ones) + _dot(elo, ones)) - (_dot(ehi, ones, _TN) + _dot(elo, ones, _TN))
                    dacum_diag = jnp.where(lane == h, d_h, dacum_diag)
                    dcb = dcb + dm * dec
                lam = jnp.exp(acum_full[:, cols])
                stv = _mx(st_ref[0, g, :, hcols])
                z = _dot(cg, stv)
                dz = _mx(lam * dyv)
                dcg = dcg + _dot(dz, stv, _NT)
                dst_in = _dot(cg, dz, _TN)
                dsv = dst[g, :, hcols]
                w = jnp.exp(last_full[:, cols] - acum_full[:, cols])
                q = _dot(bg, _mx(dsv))
                wq = w * q
                dxdt = dxdt + wq
                wqx = wq * xdt
                dbg = dbg + _dot(_mx(xdt * w), _mx(dsv), _NT)
                elast = jnp.exp(last_full[:, cols])
                dlast_p = jnp.sum(wqx, axis=0, keepdims=True) + elast * jnp.sum(dsv * st_ref[0, g, :, hcols], axis=0, keepdims=True)
                dac_ref[:, cols] = dyv * z * lam - wqx
                dlast_parts.append(dlast_p)
                dst[g, :, hcols] = dst_in + dsv * elast
                dxdt_ref[:, cols] = dxdt
                dact_ref[:, cols] = dxdt * pre["dt_full"][:, cols] + dsk_ref[:, cols] * dyv
            dcbm = _mx(dcb)
            dact_ref[:, SSD_D_INNER + g * N:SSD_D_INNER + (g + 1) * N] = dbg + _dot(dcbm, cg, _TN)
            dact_ref[:, SSD_D_INNER + G * N + g * N:SSD_D_INNER + G * N + (g + 1) * N] = dcg + _dot(dcbm, bg)

        xs_all = act_ref[:, 0:SSD_D_INNER]
        dacum = dacum_diag + _dot_exact01(dac_ref[...], reduce)
        dlast = _dot_exact01(jnp.concatenate(dlast_parts, axis=1), reduce)
        row = lax.broadcasted_iota(jnp.int32, (L, LANES), 0)
        dacum = dacum + jnp.where(row == L - 1, dlast, 0.0)
        da_col = _dot01(dacum, pre["upper"], left=True)
        a_w = -jnp.exp(aw_ref[...])
        ddt = a_w * da_col + _dot_exact01(dxdt_ref[...] * xs_all, reduce)
        xin = dtp_ref[...] + bw_ref[...]
        ddtp = ddt * (1.0 / (1.0 + jnp.exp(-xin)))
        valid = lane < SSD_HEADS
        ddtp = jnp.where(valid, ddtp, 0.0)
        ddtp_ref[...] = ddtp
        dbias_ref[...] += jnp.sum(ddtp, axis=0, keepdims=True)
        dalog_ref[...] += jnp.sum(jnp.where(valid, da_col * pre["a_col"], 0.0), axis=0, keepdims=True)
        dskip_ref[...] += jnp.sum(_dot_exact01(dy_ref[...] * xs_all, reduce), axis=0, keepdims=True)

    sp = _ssd_specs(s, nc, True)
    acc = pl.BlockSpec((1, LANES), lambda i: (0, 0))
    return pl.pallas_call(
        body, name="ssd_scan_bwd", grid=(nc,),
        in_specs=[sp["xbc"], sp["dtp"], sp["dtp_t"]] + sp["consts"] + [sp["state"], sp["rows"]],
        out_specs=[sp["xbc"], pl.BlockSpec((L, LANES), lambda i: (nc - 1 - i, 0)), acc, acc, acc],
        out_shape=[_sds((s, SSD_CONV_DIM), F32), _sds((s, LANES), F32)] + [_sds((1, LANES), F32)] * 3,
        scratch_shapes=[pltpu.VMEM((G, N, GW), F32), pltpu.VMEM((L, SSD_D_INNER), F32), pltpu.VMEM((L, SSD_D_INNER), F32)],
        compiler_params=_params(("arbitrary",)),
    )(act, proj, dtp_t, *_const_args(cs), states, dy)


def ssd_post_fwd(y, proj, g):
    s, d = y.shape
    ts = _pick(s, (256, 128))

    def body(y_ref, z_ref, g_ref, o_ref):
        y2 = y_ref[...] * _silu(z_ref[...])
        r = lax.rsqrt(jnp.mean(y2 * y2, axis=-1, keepdims=True) + NORM_EPS)
        o_ref[...] = (y2 * r * g_ref[...]).astype(o_ref.dtype)

    row = pl.BlockSpec((ts, d), lambda i: (i, 0))
    return pl.pallas_call(
        body, name="ssd_post_fwd", grid=(s // ts,), in_specs=[row, row, pl.BlockSpec((1, d), lambda i: (0, 0))],
        out_specs=row, out_shape=_sds((s, d), MXU_DTYPE), compiler_params=_params(("parallel",)),
    )(y, proj, g)


def ssd_post_bwd(y, proj, g, dy3):
    s, d = y.shape
    ts = _pick(s, (256, 128))

    def body(y_ref, z_ref, g_ref, d3_ref, dy_ref, dz_ref, dg_ref):
        yv, zv = y_ref[...], z_ref[...]
        sz, sgrad = _silu_and_grad(zv)
        y2 = yv * sz
        r = lax.rsqrt(jnp.mean(y2 * y2, axis=-1, keepdims=True) + NORM_EPS)
        xh = y2 * r
        d3 = d3_ref[...]
        dxh = d3 * g_ref[...]
        dy2 = r * (dxh - xh * jnp.mean(dxh * xh, axis=-1, keepdims=True))
        dy_ref[...] = dy2 * sz
        dz_ref[...] = (dy2 * yv * sgrad).astype(dz_ref.dtype)
        part = jnp.sum(d3 * xh, axis=0, keepdims=True)

        @pl.when(pl.program_id(0) == 0)
        def _():
            dg_ref[...] = part

        @pl.when(pl.program_id(0) != 0)
        def _():
            dg_ref[...] += part

    row = pl.BlockSpec((ts, d), lambda i: (i, 0))
    vec = pl.BlockSpec((1, d), lambda i: (0, 0))
    return pl.pallas_call(
        body, name="ssd_post_bwd", grid=(s // ts,), in_specs=[row, row, vec, row], out_specs=[row, row, vec],
        out_shape=[_sds((s, d), F32), _sds((s, d), MXU_DTYPE), _sds((1, d), F32)],
        compiler_params=_params(("arbitrary",)),
    )(y, proj, g, dy3)


def ssd_core_fwd(proj, cw, cb, dt_bias, a_log, d_skip, norm_g):
    cs = _ssd_consts(dt_bias, a_log, d_skip)
    act = ssd_conv_fwd(proj, cw, cb)
    dtp_t = proj[:, SSD_IN_DIM - SSD_HEADS:SSD_IN_DIM].T
    y, states = ssd_scan_fwd(act, proj, dtp_t, cs)
    y3 = ssd_post_fwd(y, proj, norm_g)
    return y3, (cs, act, dtp_t, y, states)


def ssd_core_bwd(proj, cw, cb, norm_g, saved, dy3):
    cs, act, dtp_t, y, states = saved
    dy, dz, dnorm = ssd_post_bwd(y, proj, norm_g, dy3)
    dact, ddtp, dalog, dbias, dskip = ssd_scan_bwd(act, proj, dtp_t, cs, states, dy)
    dxbc, dcw, dcb = ssd_conv_bwd(proj, dact, cw, cb)
    dproj = jnp.concatenate([dz, dxbc, ddtp.astype(MXU_DTYPE)], axis=1)
    h = SSD_HEADS
    return dproj, dcw, dcb, dbias[0, :h], dalog[0, :h], dskip[0, :h], dnorm


def ssd_core(proj, cw, cb, dt_bias, a_log, d_skip, norm_g, dy3):
    y3, saved = ssd_core_fwd(proj, cw, cb, dt_bias, a_log, d_skip, norm_g)
    return y3, ssd_core_bwd(proj, cw, cb, norm_g, saved, dy3)


def loss_head(x, g, target):
    s, d = x.shape
    ts = _pick(s, (512, 256, 128))

    def body(x_ref, g_ref, t_ref, loss_ref, dx_ref, dg_ref):
        xv = x_ref[...]
        r = lax.rsqrt(jnp.mean(xv * xv, axis=-1, keepdims=True) + NORM_EPS)
        xh = xv * r
        err = xh * g_ref[...] - t_ref[...]
        dy = err * (1.0 / d)
        dxh = dy * g_ref[...]
        dx_ref[...] = r * (dxh - xh * jnp.mean(dxh * xh, axis=-1, keepdims=True))
        part = jnp.sum(dy * xh, axis=0, keepdims=True)
        lpart = jnp.full((1, LANES), 0.5 * jnp.sum(jnp.mean(err * err, axis=-1, keepdims=True)), F32)

        @pl.when(pl.program_id(0) == 0)
        def _():
            dg_ref[...] = part
            loss_ref[...] = lpart

        @pl.when(pl.program_id(0) != 0)
        def _():
            dg_ref[...] += part
            loss_ref[...] += lpart

    row = pl.BlockSpec((ts, d), lambda i: (i, 0))
    vec = pl.BlockSpec((1, d), lambda i: (0, 0))
    return pl.pallas_call(
        body, name="loss_head", grid=(s // ts,), in_specs=[row, vec, row],
        out_specs=[pl.BlockSpec((1, LANES), lambda i: (0, 0)), row, vec],
        out_shape=[_sds((1, LANES), F32), _sds((s, d), F32), _sds((1, d), F32)],
        compiler_params=_params(("arbitrary",)),
    )(x, g, target)


def _adamw_math(w, g, m, v):
    m = ADAM_B1 * m + (1.0 - ADAM_B1) * g
    v = ADAM_B2 * v + (1.0 - ADAM_B2) * (g * g)
    m_hat = m / (1.0 - ADAM_B1 ** ADAM_STEP)
    v_hat = v / (1.0 - ADAM_B2 ** ADAM_STEP)
    return -ADAM_LR * (m_hat / (jnp.sqrt(v_hat) + ADAM_EPS) + ADAM_WD * w), m, v


def adamw(w, g, m, v, name="adamw"):
    r, c = w.shape
    tr = _pick(r, (256, 128, 64, 32, 16, 8))

    def body(w_ref, g_ref, m_ref, v_ref, d_ref, nm_ref, nv_ref):
        d_ref[...], nm_ref[...], nv_ref[...] = _adamw_math(w_ref[...], g_ref[...], m_ref[...], v_ref[...])

    blk = pl.BlockSpec((tr, c), lambda i: (i, 0))
    return pl.pallas_call(
        body, name=name, grid=(r // tr,), in_specs=[blk] * 4, out_specs=[blk] * 3,
        out_shape=[_sds((r, c), F32)] * 3, compiler_params=_params(("parallel",)),
    )(w, g, m, v)


def adamw_small(w, parts, m, v):
    n, r, c = parts.shape

    def body(w_ref, p_ref, m_ref, v_ref, g_ref, d_ref, nm_ref, nv_ref):
        g = p_ref[0]
        for k in range(1, n):
            g = g + p_ref[k]
        g_ref[...] = g
        d_ref[...], nm_ref[...], nv_ref[...] = _adamw_math(w_ref[...], g, m_ref[...], v_ref[...])

    return pl.pallas_call(
        body, name="adamw_small", out_shape=[_sds((r, c), F32)] * 4, compiler_params=_params(),
    )(w, parts, m, v)


def pair_sum(unit, recv, half):
    nchip, _, r, c = unit.shape
    tr = _pick(r, (512, 256, 176, 128, 64, 32, 16))

    def body(h_ref, a_ref, b_ref, o_ref, ob_ref):
        sm = a_ref[0, 0] + b_ref[0]
        o_ref[0] = sm
        ob_ref[0] = sm.astype(ob_ref.dtype)

    blk = pl.BlockSpec((1, tr, c), lambda s, i, h: (s, i, 0))
    return pl.pallas_call(
        body, name="pair_sum",
        grid_spec=pltpu.PrefetchScalarGridSpec(
            num_scalar_prefetch=1, grid=(nchip, r // tr),
            in_specs=[pl.BlockSpec((1, 1, tr, c), lambda s, i, h: (s, h[0], i, 0)), blk], out_specs=[blk, blk]),
        out_shape=[_sds((nchip, r, c), F32), _sds((nchip, r, c), jnp.bfloat16)],
        compiler_params=_params(("parallel", "parallel")),
    )(half, unit, recv)


def chip_sum(own, where, recv, layer, layers, prev=None):
    _, r, c = own.shape
    tr = _pick(r, (512, 256, 176, 128, 64, 32, 16))

    def body(s_ref, a_ref, b_ref, *rest):
        rest[-1][...] = a_ref[0] + b_ref[0].astype(F32) + b_ref[1].astype(F32) + b_ref[2].astype(F32)

    in_specs = [pl.BlockSpec((1, tr, c), lambda i, s: (s[0], i, 0)), pl.BlockSpec((3, tr, c), lambda i, s: (0, i, 0))]
    args = [where, own, recv]
    if prev is not None:
        in_specs.append(ANY)
        args.append(prev)
    return pl.pallas_call(
        body, name="chip_sum",
        grid_spec=pltpu.PrefetchScalarGridSpec(
            num_scalar_prefetch=1, grid=(r // tr,), in_specs=in_specs,
            out_specs=pl.BlockSpec((None, None, tr, c), lambda i, s: (layer, s[1], i, 0))),
        out_shape=_sds((layers, 2, r, c), F32), input_output_aliases={} if prev is None else {3: 0},
        compiler_params=_params(("parallel",)),
    )(*args)


def place_cast(w, chip):
    l, a, b = w.shape
    ta = _pick(a, (512, 352, 256, 128))

    def body(c_ref, w_ref, o_ref):
        o_ref[...] = w_ref[...].astype(o_ref.dtype)

    return pl.pallas_call(
        body, name="place_cast",
        grid_spec=pltpu.PrefetchScalarGridSpec(
            num_scalar_prefetch=1, grid=(l, a // ta),
            in_specs=[pl.BlockSpec((None, ta, b), lambda li, i, c: (li, i, 0))],
            out_specs=pl.BlockSpec((None, None, ta, b), lambda li, i, c: (li, c[0], i, 0))),
        out_shape=_sds((l, N_CHIPS, a, b), MXU_DTYPE), compiler_params=_params(("parallel", "parallel")),
    )(chip, w)


ANY = pl.BlockSpec(memory_space=pl.ANY)
COMM = pltpu.CompilerParams(has_side_effects=True)


def _coords():
    return lax.axis_index("x"), lax.axis_index("y"), lax.axis_index("c")


def _other_chips(x, y):
    return [(1 - x, y), (x, 1 - y), (1 - x, 1 - y)]


def all_gather_8(halves, name):
    _, r, c = halves.shape

    def body(h_ref, out_ref, send_sems, recv_sems, local_sem):
        x, y, cc = _coords()
        _gather_one(h_ref.at[cc], lambda px, py, pc: out_ref.at[4 * px + 2 * py + pc],
                    lambda k: send_sems.at[k], lambda k: recv_sems.at[k], local_sem)

    return pl.pallas_call(
        body, name=name, in_specs=[ANY], out_specs=ANY, out_shape=_sds((8, r, c), halves.dtype),
        scratch_shapes=[pltpu.SemaphoreType.DMA((7,)), pltpu.SemaphoreType.DMA((7,)), pltpu.SemaphoreType.DMA],
        compiler_params=COMM,
    )(halves)


def _gather_plan(x_ref, slot, send_sem, recv_sem, local_sem):
    x, y, cc = _coords()
    me, sibling = (x, y, cc), (x, y, 1 - cc)
    chips = _other_chips(x, y)

    def copy(k, blk, to, src=None):
        return pltpu.make_async_remote_copy(
            src_ref=slot(*blk) if src is None else src, dst_ref=slot(*blk),
            send_sem=send_sem(k), recv_sem=recv_sem(k), device_id=to, device_id_type=MESH)

    mine = pltpu.make_async_copy(x_ref, slot(*me), local_sem)
    first = [copy(0, me, sibling, src=x_ref)] + [copy(1 + j, me, (*chip, cc), src=x_ref) for j, chip in enumerate(chips)]
    passed = [copy(4 + j, (*chip, cc), sibling) for j, chip in enumerate(chips)]
    over_ici = [copy(1 + j, (*chip, cc), me) for j, chip in enumerate(chips)]
    from_sibling = [copy(0, sibling, me)] + [copy(4 + j, (*chip, 1 - cc), me) for j, chip in enumerate(chips)]
    return mine, first, passed, over_ici, from_sibling


def _gather_run(plans):
    for mine, first, _, _, _ in plans:
        mine.start()
        for cp in first:
            cp.start()
    for j in range(3):
        for _, _, passed, over_ici, _ in plans:
            over_ici[j].wait_recv()
            passed[j].start()
    for mine, first, passed, _, from_sibling in plans:
        for cp in from_sibling:
            cp.wait_recv()
        for cp in first + passed:
            cp.wait_send()
        mine.wait()


def _gather_one(x_ref, slot, send_sem, recv_sem, local_sem):
    _gather_run([_gather_plan(x_ref, slot, send_sem, recv_sem, local_sem)])


def gather_big(placed):
    n = len(placed)

    def body(*refs):
        outs = refs[n:2 * n]
        send_sems, recv_sems = refs[2 * n:]
        x, y, cc = _coords()
        chips = _other_chips(x, y)
        sibling = (x, y, 1 - cc)

        def copy(i, k, px, py, pc, to):
            lh = placed[i].shape[0] // 2
            blk = outs[i].at[pl.ds(pc * lh, lh), 2 * px + py]
            return pltpu.make_async_remote_copy(src_ref=blk, dst_ref=blk, send_sem=send_sems.at[i, k],
                                                recv_sem=recv_sems.at[i, k], device_id=to, device_id_type=MESH)

        first = [copy(i, j, x, y, cc, (*chip, cc)) for i in range(n) for j, chip in enumerate(chips)]
        for cp in first:
            cp.start()
        passed = []
        for j, chip in enumerate(chips):
            for i in range(n):
                copy(i, j, *chip, cc, (x, y, cc)).wait_recv()
                passed.append(copy(i, 3 + j, *chip, cc, sibling))
                passed[-1].start()
        for j, chip in enumerate(chips):
            for i in range(n):
                copy(i, 3 + j, *chip, 1 - cc, (x, y, cc)).wait_recv()
        for cp in first + passed:
            cp.wait_send()

    return pl.pallas_call(
        body, name="gather_big", in_specs=[ANY] * n, out_specs=[ANY] * n,
        out_shape=[_sds(a.shape, a.dtype) for a in placed], input_output_aliases={i: i for i in range(n)},
        scratch_shapes=[pltpu.SemaphoreType.DMA((n, 6)), pltpu.SemaphoreType.DMA((n, 6))],
        compiler_params=COMM,
    )(*placed)


def grad_pair_swap(units):
    n = len(units)

    def body(*refs):
        ins, outs, send_sems, recv_sems = refs[:n], refs[n:2 * n], refs[2 * n], refs[2 * n + 1]
        x, y, cc = _coords()
        cps = [pltpu.make_async_remote_copy(src_ref=ins[i].at[:, 1 - cc], dst_ref=outs[i], send_sem=send_sems.at[i],
                                            recv_sem=recv_sems.at[i], device_id=(x, y, 1 - cc), device_id_type=MESH)
               for i in range(n)]
        for cp in cps:
            cp.start()
        for cp in cps:
            cp.wait()

    return pl.pallas_call(
        body, name="grad_pair_swap", in_specs=[ANY] * n, out_specs=[ANY] * n,
        out_shape=[_sds((u.shape[0],) + u.shape[2:], u.dtype) for u in units],
        scratch_shapes=[pltpu.SemaphoreType.DMA((n,)), pltpu.SemaphoreType.DMA((n,))], compiler_params=COMM,
    )(*units)


def grad_chip_exchange(units):
    n = len(units)

    def body(*refs):
        ins, outs, send_sems, recv_sems = refs[:n], refs[n:2 * n], refs[2 * n], refs[2 * n + 1]
        x, y, cc = _coords()
        cps = [pltpu.make_async_remote_copy(
            src_ref=ins[i].at[2 * px + py], dst_ref=outs[i].at[k], send_sem=send_sems.at[i, k],
            recv_sem=recv_sems.at[i, k], device_id=(px, py, cc), device_id_type=MESH)
            for i in range(n) for k, (px, py) in enumerate(_other_chips(x, y))]
        for cp in cps:
            cp.start()
        for cp in cps:
            cp.wait()

    return pl.pallas_call(
        body, name="grad_chip_exchange", in_specs=[ANY] * n, out_specs=[ANY] * n,
        out_shape=[_sds((3,) + u.shape[1:], u.dtype) for u in units],
        scratch_shapes=[pltpu.SemaphoreType.DMA((n, 3)), pltpu.SemaphoreType.DMA((n, 3))], compiler_params=COMM,
    )(*units)


def grad_half_swap(grads):
    n = len(grads)

    def body(*refs):
        outs, send_sems, recv_sems = refs[n:2 * n], refs[2 * n], refs[2 * n + 1]
        x, y, cc = _coords()
        cps = [pltpu.make_async_remote_copy(
            src_ref=outs[i].at[:, cc], dst_ref=outs[i].at[:, cc], send_sem=send_sems.at[i], recv_sem=recv_sems.at[i],
            device_id=(x, y, 1 - cc), device_id_type=MESH) for i in range(n)]
        for cp in cps:
            cp.start()
        for i, cp in enumerate(cps):
            cp.wait_send()
            pltpu.make_async_remote_copy(
                src_ref=outs[i].at[:, 1 - cc], dst_ref=outs[i].at[:, 1 - cc], send_sem=send_sems.at[i],
                recv_sem=recv_sems.at[i], device_id=(x, y, 1 - cc), device_id_type=MESH).wait_recv()

    return pl.pallas_call(
        body, name="grad_half_swap", in_specs=[ANY] * n, out_specs=[ANY] * n,
        out_shape=[_sds(g.shape, g.dtype) for g in grads], input_output_aliases={i: i for i in range(n)},
        scratch_shapes=[pltpu.SemaphoreType.DMA((n,)), pltpu.SemaphoreType.DMA((n,))], compiler_params=COMM,
    )(*grads)


N_CHIPS = 4
PACK_COLS = 1024
BIG = ("ssd_w_in", "ssd_w_out", "sb_w_qkv", "sb_w_out", "ffn_w_in", "ffn_w_out")
CONVW = ("ssd_conv_w", "ffn_conv_w")
COL_SHARDED = ("ssd_w_in", "sb_w_qkv", "ffn_w_in", "ssd_conv_w", "ffn_conv_w")
SMALL = ("mix_norm", "ffn_norm", "final_norm", "ssd_conv_b", "ssd_dt_bias", "ssd_a_log", "ssd_d", "ssd_norm", "ffn_conv_b")
WEIGHTS = ("mix_norm", "ffn_norm", "final_norm", "ssd_w_in", "ssd_conv_w", "ssd_conv_b", "ssd_dt_bias", "ssd_a_log",
           "ssd_d", "ssd_norm", "ssd_w_out", "sb_w_qkv", "sb_w_out", "ffn_w_in", "ffn_conv_w", "ffn_conv_b", "ffn_w_out")


def _to_rows(flat, multiple):
    rows = -(-flat.shape[-1] // PACK_COLS)
    rows = -(-rows // multiple) * multiple
    pad = rows * PACK_COLS - flat.shape[-1]
    return jnp.pad(flat, [(0, pad)]).reshape(rows, PACK_COLS)


def _unshard(name, stacked):
    l, n, a, b = stacked.shape
    if name in COL_SHARDED:
        return jnp.transpose(stacked, (0, 2, 1, 3)).reshape(l, a, n * b)
    return stacked.reshape(l, n * a, b)


def _gather_conv_weights(w):
    flat = jnp.concatenate([w[n].reshape(-1) for n in CONVW])
    rows = _to_rows(flat, 16)
    got = all_gather_8(rows.reshape(2, rows.shape[0] // 2, PACK_COLS), "gather_conv_weights").reshape(N_CHIPS, -1)
    out, off = {}, 0
    for n in CONVW:
        l, a, b = w[n].shape
        out[n] = _unshard(n, jnp.moveaxis(got[:, off:off + w[n].size].reshape(N_CHIPS, l, a, b), 0, 1))
        off += w[n].size
    return out


def _reduce_big_grads(units, layout):
    cc = lax.axis_index("c").astype(jnp.int32)
    chip = (2 * lax.axis_index("x") + lax.axis_index("y")).astype(jnp.int32)
    from_sibling = grad_pair_swap(units)
    pairs = [pair_sum(u, r, cc.reshape(1)) for u, r in zip(units, from_sibling)]
    from_chips = grad_chip_exchange([p[1] for p in pairs])
    where = jnp.stack([chip, cc])
    nlayers = [1 + max(l for k, l in layout if k == wi) for wi in range(1 + max(k for k, _ in layout))]
    grads = [None] * len(nlayers)
    for (wi, l), p, r in zip(layout, pairs, from_chips):
        grads[wi] = chip_sum(p[0], where, r, l, nlayers[wi], grads[wi])
    return grad_half_swap(grads)


def kernel(x, mix_norm, ffn_norm, final_norm, ssd_w_in, ssd_conv_w, ssd_conv_b, ssd_dt_bias, ssd_a_log, ssd_d, ssd_norm, ssd_w_out, sb_w_qkv, sb_w_out, ffn_w_in, ffn_conv_w, ffn_conv_b, ffn_w_out, loss_target, m_mix_norm, m_ffn_norm, m_final_norm, m_ssd_w_in, m_ssd_conv_w, m_ssd_conv_b, m_ssd_dt_bias, m_ssd_a_log, m_ssd_d, m_ssd_norm, m_ssd_w_out, m_sb_w_qkv, m_sb_w_out, m_ffn_w_in, m_ffn_conv_w, m_ffn_conv_b, m_ffn_w_out, v_mix_norm, v_ffn_norm, v_final_norm, v_ssd_w_in, v_ssd_conv_w, v_ssd_conv_b, v_ssd_dt_bias, v_ssd_a_log, v_ssd_d, v_ssd_norm, v_ssd_w_out, v_sb_w_qkv, v_sb_w_out, v_ffn_w_in, v_ffn_conv_w, v_ffn_conv_b, v_ffn_w_out):
    given = dict(locals())
    w = {n: given[n] for n in WEIGHTS}
    mom = {n: given["m_" + n] for n in WEIGHTS}
    var = {n: given["v_" + n] for n in WEIGHTS}
    chip = 2 * lax.axis_index("x") + lax.axis_index("y")

    chip1 = chip.reshape(1).astype(jnp.int32)
    gathered = gather_big([place_cast(w[n], chip1) for n in BIG])
    fw = {n: _unshard(n, g) for n, g in zip(BIG, gathered)}
    fw.update(_gather_conv_weights(w))
    w_ssd_in = jnp.pad(fw["ssd_w_in"], ((0, 0), (0, 0), (0, SSD_IN_PAD - SSD_IN_DIM)))
    row = lambda v: v.reshape(1, -1)

    xcur = x[0]
    saved = []
    for i in range(DEPTH):
        j = i // 2
        h, r = rms_fwd(xcur, row(mix_norm[i]))
        if i % 2 == 0:
            proj = mm(h, w_ssd_in[j], tn=896, name="mm_ssd_in")
            y3, core = ssd_core_fwd(proj, fw["ssd_conv_w"][j], row(ssd_conv_b[j]), ssd_dt_bias[j], ssd_a_log[j], ssd_d[j], row(ssd_norm[j]))
            x1 = mm(y3, fw["ssd_w_out"][j], res=xcur, name="mm_ssd_out")
            mix = (proj, y3, core)
        else:
            qkv = mm(h, fw["sb_w_qkv"][j], out_dtype=MXU_DTYPE, name="mm_sb_qkv")
            o = sb_fwd(qkv)
            x1 = mm(o, fw["sb_w_out"][j], res=xcur, name="mm_sb_out")
            mix = (qkv, o)
        h2, r2 = rms_fwd(x1, row(ffn_norm[i]))
        u0 = mm(h2, fw["ffn_w_in"][i], name="mm_ffn_in")
        a = ffn_mid_fwd(u0, fw["ffn_conv_w"][i], row(ffn_conv_b[i]))
        x2 = mm(a, fw["ffn_w_out"][i], res=x1, name="mm_ffn_out")
        saved.append((xcur, h, r, mix, x1, h2, r2, u0, a))
        xcur = x2
    loss_part, dx, d_final = loss_head(xcur, row(final_norm), loss_target[0])

    gl = {n: [None] * w[n].shape[0] for n in WEIGHTS if n != "final_norm"}
    units = {n: [None] * w[n].shape[0] for n in BIG}

    def unit_of(g4):
        return g4.reshape(N_CHIPS, 2, g4.shape[1] // 2, g4.shape[2])

    for i in reversed(range(DEPTH)):
        j = i // 2
        x0, h, r, mix, x1, h2, r2, u0, a = saved[i]
        units["ffn_w_out"][i] = unit_of(mm(a, dx, "tn", name="mm_d_ffn_out").reshape(N_CHIPS, -1, D_MODEL))
        da = mm(dx, fw["ffn_w_out"][i], "nt", name="mm_da_ffn")
        dug, duu, gl["ffn_conv_w"][i], dcb = ffn_mid_bwd(u0, da, fw["ffn_conv_w"][i], row(ffn_conv_b[i]))
        gl["ffn_conv_b"][i] = dcb[0]
        du0 = jnp.concatenate([dug, duu], axis=1)
        units["ffn_w_in"][i] = unit_of(mm(h2, du0, "tn", tn=1408, tm=512, n_split=N_CHIPS, name="mm_d_ffn_in"))
        dh2 = mm(du0, fw["ffn_w_in"][i], "nt", name="mm_dh_ffn")
        dx1, dg = rms_bwd(x1, r2, row(ffn_norm[i]), dh2, dx)
        gl["ffn_norm"][i] = dg[0]
        if i % 2 == 0:
            proj, y3, core = mix
            units["ssd_w_out"][j] = unit_of(mm(y3, dx1, "tn", name="mm_d_ssd_out").reshape(N_CHIPS, -1, D_MODEL))
            dy3 = mm(dx1, fw["ssd_w_out"][j], "nt", name="mm_dy3_ssd")
            dproj, gl["ssd_conv_w"][j], dcb, gl["ssd_dt_bias"][j], gl["ssd_a_log"][j], gl["ssd_d"][j], dnorm = ssd_core_bwd(
                proj, fw["ssd_conv_w"][j], row(ssd_conv_b[j]), row(ssd_norm[j]), core, dy3)
            gl["ssd_conv_b"][j] = dcb[0]
            gl["ssd_norm"][j] = dnorm[0]
            dw_in = mm(h, dproj, "tn", tn=896, name="mm_d_ssd_in")[:, :SSD_IN_DIM]
            units["ssd_w_in"][j] = unit_of(jnp.transpose(dw_in.reshape(D_MODEL, N_CHIPS, -1), (1, 0, 2)))
            dh = mm(dproj, w_ssd_in[j], "nt", name="mm_dh_ssd")
        else:
            qkv, o = mix
            units["sb_w_out"][j] = unit_of(mm(o, dx1, "tn", name="mm_d_sb_out").reshape(N_CHIPS, -1, D_MODEL))
            do = mm(dx1, fw["sb_w_out"][j], "nt", out_dtype=MXU_DTYPE, name="mm_do_sb")
            dqkv = sb_bwd(qkv, do)
            units["sb_w_qkv"][j] = unit_of(mm(h, dqkv, "tn", tn=768, n_split=N_CHIPS, name="mm_d_sb_qkv"))
            dh = mm(dqkv, fw["sb_w_qkv"][j], "nt", name="mm_dh_sb")
        dx, dg = rms_bwd(x0, r, row(mix_norm[i]), dh, dx1)
        gl["mix_norm"][i] = dg[0]

    layout = [(k, l) for k, n in enumerate(BIG) for l in range(w[n].shape[0])]
    reduced = _reduce_big_grads([units[BIG[k]][l] for k, l in layout], layout)
    g, delta, new_m, new_v = {}, {}, {}, {}
    two_d = lambda t: t.reshape(-1, t.shape[-1])
    for n, red in zip(BIG, reduced):
        g[n] = red.reshape(w[n].shape)
        d2, m2, v2 = adamw(two_d(w[n]), two_d(g[n]), two_d(mom[n]), two_d(var[n]), name="adamw_" + n)
        delta[n], new_m[n], new_v[n] = d2.reshape(w[n].shape), m2.reshape(w[n].shape), v2.reshape(w[n].shape)

    small_g = {n: jnp.stack(gl[n]) for n in SMALL + CONVW if n != "final_norm"}
    small_g["final_norm"] = d_final[0]
    zeros_of = lambda n: jnp.zeros((small_g[n].size,), F32)

    def small_pack(d, extra):
        parts = [d[n].reshape(-1) for n in SMALL] + [extra]
        parts += [(d[n].reshape(-1) if d is small_g else zeros_of(n)) for n in CONVW]
        return _to_rows(jnp.concatenate(parts), 16)

    part = small_pack(small_g, loss_part[0, 0:1])
    parts = all_gather_8(jnp.stack([part, part]), "gather_small_grads")
    zero = jnp.zeros((1,), F32)
    gs, ds, ms, vs = adamw_small(small_pack(w, zero), parts, small_pack(mom, zero), small_pack(var, zero))
    gs_flat = gs.reshape(-1)
    off = 0
    for n in SMALL:
        size = w[n].size
        for dst, src in ((g, gs), (delta, ds), (new_m, ms), (new_v, vs)):
            dst[n] = src.reshape(-1)[off:off + size].reshape(w[n].shape)
        off += size
    loss = gs_flat[off]
    off += 1
    for n in CONVW:
        size = small_g[n].size
        b = w[n].shape[-1]
        g[n] = lax.dynamic_slice_in_dim(gs_flat[off:off + size].reshape(small_g[n].shape), chip * b, b, axis=2)
        d2, m2, v2 = adamw(two_d(w[n]), two_d(g[n]), two_d(mom[n]), two_d(var[n]), name="adamw_" + n)
        delta[n], new_m[n], new_v[n] = d2.reshape(w[n].shape), m2.reshape(w[n].shape), v2.reshape(w[n].shape)
        off += size

    return (loss, dx[None], *[g[n] for n in WEIGHTS], *[delta[n] for n in WEIGHTS],
            *[new_m[n] for n in WEIGHTS], *[new_v[n] for n in WEIGHTS])
```

```python
import functools

import jax
import jax.numpy as jnp
from jax import lax
from jax.experimental import pallas as pl
from jax.experimental.pallas import tpu as pltpu

F32 = jnp.float32
MXU_DTYPE = jnp.bfloat16
HIGHEST = lax.Precision.HIGHEST

D_MODEL = 1024
DEPTH = 4
NORM_EPS = 1e-6
SSD_D_INNER = 2048
SSD_HEAD_DIM = 64
SSD_HEADS = 32
SSD_GROUPS = 8
SSD_STATE = 128
SSD_CONV = 4
SSD_CHUNK = 128
SSD_CONV_DIM = 4096
SSD_IN_DIM = 6176
SSD_IN_PAD = 6272
SB_HEADS = 16
SB_HEAD_DIM = 64
FFN_D_FF = 2816
FFN_CONV = 3
ADAM_LR, ADAM_B1, ADAM_B2, ADAM_EPS, ADAM_WD, ADAM_STEP = 0.001, 0.9, 0.999, 1e-08, 0.01, 10

LANES = 128
SUBLANES = 8
VMEM_LIMIT = 56 * 1024 * 1024
MESH = pl.DeviceIdType.MESH


def _params(sem=None):
    return pltpu.CompilerParams(dimension_semantics=sem, vmem_limit_bytes=VMEM_LIMIT)


def _sds(shape, dtype):
    return jax.ShapeDtypeStruct(shape, dtype)


def _dot(a, b, dims=(((1,), (0,)), ((), ())), precision=None):
    return lax.dot_general(a, b, dims, precision=precision, preferred_element_type=F32)


_NN = (((1,), (0,)), ((), ()))
_NT = (((1,), (1,)), ((), ()))
_TN = (((0,), (0,)), ((), ()))


def _mx(a):
    return a.astype(MXU_DTYPE)


def _silu(x):
    return x * (1.0 / (1.0 + jnp.exp(-x)))


def _silu_and_grad(x):
    s = 1.0 / (1.0 + jnp.exp(-x))
    return x * s, s * (1.0 + x * (1.0 - s))


def _pick(n, cands):
    for c in cands:
        if n % c == 0:
            return c
    return n


def rms_fwd(x, g):
    s, d = x.shape
    ts = _pick(s, (512, 256, 128))

    def body(x_ref, g_ref, h_ref, r_ref):
        xv = x_ref[...]
        r = lax.rsqrt(jnp.mean(xv * xv, axis=-1, keepdims=True) + NORM_EPS)
        h_ref[...] = (xv * r * g_ref[...]).astype(h_ref.dtype)
        r_ref[...] = r

    return pl.pallas_call(
        body, name="rms_fwd", grid=(s // ts,),
        in_specs=[pl.BlockSpec((ts, d), lambda i: (i, 0)), pl.BlockSpec((1, d), lambda i: (0, 0))],
        out_specs=[pl.BlockSpec((ts, d), lambda i: (i, 0)), pl.BlockSpec((ts, 1), lambda i: (i, 0))],
        out_shape=[_sds((s, d), MXU_DTYPE), _sds((s, 1), F32)],
        compiler_params=_params(("parallel",)),
    )(x, g)


def rms_bwd(x, r, g, dh, dres):
    s, d = x.shape
    ts = _pick(s, (512, 256, 128))

    def body(x_ref, r_ref, g_ref, dh_ref, dres_ref, dx_ref, dxm_ref, dg_ref):
        xh = x_ref[...] * r_ref[...]
        dhv = dh_ref[...]
        dxh = dhv * g_ref[...]
        dx = dres_ref[...] + r_ref[...] * (dxh - xh * jnp.mean(dxh * xh, axis=-1, keepdims=True))
        dx_ref[...] = dx
        dxm_ref[...] = dx.astype(dxm_ref.dtype)
        part = jnp.sum(dhv * xh, axis=0, keepdims=True)

        @pl.when(pl.program_id(0) == 0)
        def _():
            dg_ref[...] = part

        @pl.when(pl.program_id(0) != 0)
        def _():
            dg_ref[...] += part

    row = pl.BlockSpec((ts, d), lambda i: (i, 0))
    return pl.pallas_call(
        body, name="rms_bwd", grid=(s // ts,),
        in_specs=[row, pl.BlockSpec((ts, 1), lambda i: (i, 0)), pl.BlockSpec((1, d), lambda i: (0, 0)), row, row],
        out_specs=[row, row, pl.BlockSpec((1, d), lambda i: (0, 0))],
        out_shape=[_sds((s, d), F32), _sds((s, d), MXU_DTYPE), _sds((1, d), F32)],
        compiler_params=_params(("arbitrary",)),
    )(x, r, g, dh, dres)


def mm(a, b, mode="nn", res=None, out_dtype=F32, tm=None, tn=None, n_split=1, name="mm", hook=None):
    if mode == "nn":
        (m, k), (_, n) = a.shape, b.shape
    elif mode == "nt":
        (m, k), (n, _) = a.shape, b.shape
    else:
        (k, m), (_, n) = a.shape, b.shape
    tm = tm or _pick(m, (1024, 512, 256, 128))
    tn = tn or _pick(n, (512, 896, 256, 128))
    dims = {"nn": _NN, "nt": _NT, "tn": _TN}[mode]

    def body(*refs):
        a_ref, b_ref = refs[0], refs[1]
        o_ref = refs[-1]
        acc = _dot(_mx(a_ref[...]), _mx(b_ref[...]), dims)
        if res is not None:
            acc = acc + refs[2][...]
        o_ref[...] = acc.astype(o_ref.dtype)

    a_spec = pl.BlockSpec((k, tm), lambda i, j: (0, i)) if mode == "tn" else pl.BlockSpec((tm, k), lambda i, j: (i, 0))
    b_spec = pl.BlockSpec((tn, k), lambda i, j: (j, 0)) if mode == "nt" else pl.BlockSpec((k, tn), lambda i, j: (0, j))
    o_spec = pl.BlockSpec((tm, tn), lambda i, j: (i, j))
    ins, specs = [a, b], [a_spec, b_spec]
    if res is not None:
        ins.append(res)
        specs.append(o_spec)
    out_shape = _sds((m, n), out_dtype)
    if n_split > 1:
        per = n // n_split // tn
        o_spec = pl.BlockSpec((None, tm, tn), lambda i, j: (j // per, i, j % per))
        out_shape = _sds((n_split, m, n // n_split), out_dtype)
    out, carried = hosted_call(body, hook, name, (m // tm, n // tn), specs, o_spec, out_shape, [],
                               ("parallel", "parallel"), ins)
    return out if hook is None else (out, carried)


CONV_ROWS = 256
CONV_COLS = 128


def _row_iota8(cols):
    return lax.broadcasted_iota(jnp.int32, (SUBLANES, cols), 0)


def _shift_down(cur, prev8, k):
    if k == 0:
        return cur
    rolled = pltpu.roll(cur, k, 0)
    head = jnp.where(_row_iota8(cur.shape[1]) < k, pltpu.roll(prev8, k, 0), rolled[0:SUBLANES])
    return jnp.concatenate([head, rolled[SUBLANES:]], axis=0)


def _shift_up(cur, next8, k):
    if k == 0:
        return cur
    n = cur.shape[0]
    rolled = pltpu.roll(cur, n - k, 0)
    tail = jnp.where(_row_iota8(cur.shape[1]) >= SUBLANES - k, pltpu.roll(next8, SUBLANES - k, 0), rolled[n - SUBLANES:])
    return jnp.concatenate([rolled[:n - SUBLANES], tail], axis=0)


def _load_prev8(ref, i, rows):
    start = pl.multiple_of(jnp.maximum(i * rows - SUBLANES, 0), SUBLANES)
    p = ref[pl.ds(start, SUBLANES), :]
    return jnp.where(i > 0, p, jnp.zeros_like(p))


def _conv_rows(ref, w_ref, b_ref, i, rows, width):
    cur = ref[pl.ds(pl.multiple_of(i * rows, rows), rows), :]
    prev8 = _load_prev8(ref, i, rows)
    shifted = [_shift_down(cur, prev8, k) for k in range(width)]
    acc = b_ref[...] + w_ref[width - 1:width, :] * shifted[0]
    for k in range(1, width):
        acc = acc + w_ref[width - 1 - k:width - k, :] * shifted[k]
    return acc, shifted


def _conv_bwd_rows(du, next8, w_ref, width):
    acc = w_ref[width - 1:width, :] * du
    for k in range(1, width):
        acc = acc + w_ref[width - 1 - k:width - k, :] * _shift_up(du, next8, k)
    return acc


def ffn_mid_fwd(u0, cw, cb):
    s, f2 = u0.shape
    f = f2 // 2
    nt = f // CONV_COLS
    rows = min(CONV_ROWS, s)

    def body(ug_ref, uu_ref, wg_ref, wu_ref, bg_ref, bu_ref, a_ref):
        def step(i, carry):
            g, _ = _conv_rows(ug_ref, wg_ref, bg_ref, i, rows, FFN_CONV)
            u, _ = _conv_rows(uu_ref, wu_ref, bu_ref, i, rows, FFN_CONV)
            a_ref[pl.ds(pl.multiple_of(i * rows, rows), rows), :] = (_silu(g) * u).astype(a_ref.dtype)
            return carry

        lax.fori_loop(0, s // rows, step, 0)

    col = lambda off: pl.BlockSpec((s, CONV_COLS), lambda j: (0, j + off))
    wsp = lambda r, off: pl.BlockSpec((r, CONV_COLS), lambda j: (0, j + off))
    return pl.pallas_call(
        body, name="ffn_mid_fwd", grid=(nt,),
        in_specs=[col(0), col(nt), wsp(FFN_CONV, 0), wsp(FFN_CONV, nt), wsp(1, 0), wsp(1, nt)],
        out_specs=pl.BlockSpec((s, CONV_COLS), lambda j: (0, j)),
        out_shape=_sds((s, f), MXU_DTYPE), compiler_params=_params(("parallel",)),
    )(u0, u0, cw, cw, cb, cb)


def ffn_mid_bwd(u0, da, cw, cb):
    s, f2 = u0.shape
    f = f2 // 2
    nt = f // CONV_COLS
    rows = min(CONV_ROWS, s)
    nsteps = s // rows
    w = FFN_CONV

    def body(ug_ref, uu_ref, da_ref, wg_ref, wu_ref, bg_ref, bu_ref,
             dug_ref, duu_ref, dwg_ref, dwu_ref, dbg_ref, dbu_ref):
        zero8 = jnp.zeros((SUBLANES, CONV_COLS), F32)
        zrow = jnp.zeros((1, CONV_COLS), F32)

        def step(it, carry):
            ng, nu, accs = carry
            i = nsteps - 1 - it
            r0 = pl.multiple_of(i * rows, rows)
            g, sg = _conv_rows(ug_ref, wg_ref, bg_ref, i, rows, w)
            u, su = _conv_rows(uu_ref, wu_ref, bu_ref, i, rows, w)
            dav = da_ref[pl.ds(r0, rows), :]
            sg_val, sg_grad = _silu_and_grad(g)
            dg = dav * u * sg_grad
            du = dav * sg_val
            dug_ref[pl.ds(r0, rows), :] = _conv_bwd_rows(dg, ng, wg_ref, w).astype(dug_ref.dtype)
            duu_ref[pl.ds(r0, rows), :] = _conv_bwd_rows(du, nu, wu_ref, w).astype(duu_ref.dtype)
            new = []
            for j in range(w):
                new.append(accs[j] + jnp.sum(dg * sg[w - 1 - j], axis=0, keepdims=True))
            for j in range(w):
                new.append(accs[w + j] + jnp.sum(du * su[w - 1 - j], axis=0, keepdims=True))
            new.append(accs[2 * w] + jnp.sum(dg, axis=0, keepdims=True))
            new.append(accs[2 * w + 1] + jnp.sum(du, axis=0, keepdims=True))
            return dg[0:SUBLANES], du[0:SUBLANES], tuple(new)

        _, _, accs = lax.fori_loop(0, nsteps, step, (zero8, zero8, tuple([zrow] * (2 * w + 2))))
        dwg_ref[...] = jnp.concatenate(accs[0:w], axis=0)
        dwu_ref[...] = jnp.concatenate(accs[w:2 * w], axis=0)
        dbg_ref[...] = accs[2 * w]
        dbu_ref[...] = accs[2 * w + 1]

    col = lambda off: pl.BlockSpec((s, CONV_COLS), lambda j: (0, j + off))
    wsp = lambda r, off: pl.BlockSpec((r, CONV_COLS), lambda j: (0, j + off))
    outs = pl.pallas_call(
        body, name="ffn_mid_bwd", grid=(nt,),
        in_specs=[col(0), col(nt), col(0), wsp(w, 0), wsp(w, nt), wsp(1, 0), wsp(1, nt)],
        out_specs=[col(0), col(0), wsp(w, 0), wsp(w, 0), wsp(1, 0), wsp(1, 0)],
        out_shape=[_sds((s, f), MXU_DTYPE), _sds((s, f), MXU_DTYPE), _sds((w, f), F32), _sds((w, f), F32),
                   _sds((1, f), F32), _sds((1, f), F32)],
        compiler_params=_params(("parallel",)),
    )(u0, u0, da, cw, cw, cb, cb)
    dug, duu, dwg, dwu, dbg, dbu = outs
    return dug, duu, jnp.concatenate([dwg, dwu], axis=1), jnp.concatenate([dbg, dbu], axis=1)


SB_BLOCK = 128
SB_DEAD = 110.0
SB_HEADS_PER_STEP = 4


def _split_hi_lo(x):
    hi = x.astype(MXU_DTYPE)
    lo = (x - hi.astype(F32)).astype(MXU_DTYPE)
    return hi, lo


def _dot_exact01(x, tri):
    hi, lo = _split_hi_lo(x)
    return _dot(hi, tri) + _dot(lo, tri)


def _stack_heads(pair, lane_lo):
    zero = jnp.zeros_like(pair)
    return jnp.concatenate([jnp.where(lane_lo, pair, zero), jnp.where(lane_lo, zero, pair)], axis=0)


def _unstack_heads(tall, lane_lo):
    n = tall.shape[0] // 2
    return jnp.where(lane_lo, tall[:n], tall[n:])


def _sb_logits(stacked_q, k_ref, k0, pair_cols, blk):
    z = [_dot(sq, k_ref[pl.ds(k0, blk), cols], _NT) for sq, cols in zip(stacked_q, pair_cols)]
    return jnp.concatenate(z, axis=0) * (SB_HEAD_DIM ** -0.5)


def _sb_logs(z, qi, kb, blk):
    rows = qi * blk + (lax.broadcasted_iota(jnp.int32, z.shape, 0) & (blk - 1))
    cols = kb * blk + lax.broadcasted_iota(jnp.int32, z.shape, 1)
    strict = cols < rows
    t = jnp.log(1.0 + jnp.exp(-jnp.abs(z)))
    lb = jnp.minimum(z, 0.0) - t
    lf = jnp.where(strict, jnp.minimum(-z, 0.0) - t, 0.0)
    return lb, lf, strict


def _tri(blk, upper):
    r = lax.broadcasted_iota(jnp.int32, (blk, blk), 0)
    c = lax.broadcasted_iota(jnp.int32, (blk, blk), 1)
    return jnp.where((r > c) if upper else (r < c), 1.0, 0.0).astype(MXU_DTYPE)


def _tri_sum(x, tri2):
    hi, lo = _split_hi_lo(x)
    return _dot(jnp.concatenate([hi, lo], axis=1), tri2)


def sb_fwd(qkv, hook=None):
    s = qkv.shape[0]
    blk = min(SB_BLOCK, s)
    nblk = s // blk
    nh = SB_HEADS_PER_STEP
    nstep = SB_HEADS // nh
    dh = SB_HEAD_DIM

    def body(q_ref, k_ref, v_ref, o_ref):
        suffix_tri2 = jnp.concatenate([_tri(blk, True)] * 2, axis=0)
        lane_lo = lax.broadcasted_iota(jnp.int32, (1, LANES), 1) < dh
        pairs = [slice(p * LANES, (p + 1) * LANES) for p in range(nh // 2)]

        def qstep(qi, carry):
            q0 = pl.multiple_of(qi * blk, blk)
            qst = [_stack_heads(q_ref[pl.ds(q0, blk), cols], lane_lo) for cols in pairs]

            def kstep(st):
                it, run, accs, _ = st
                kb = qi - it
                k0 = pl.multiple_of(kb * blk, blk)
                lb, lf, strict = _sb_logs(_sb_logits(qst, k_ref, k0, pairs, blk), qi, kb, blk)
                sloc = _tri_sum(lf, suffix_tri2)
                a = _mx(jnp.where(strict, jnp.exp(lb + sloc + run), 0.0))
                accs = tuple(
                    acc + _unstack_heads(_dot(a[2 * blk * p:2 * blk * (p + 1)], v_ref[pl.ds(k0, blk), cols]), lane_lo)
                    for p, (acc, cols) in enumerate(zip(accs, pairs)))
                run = run + sloc[:, 0:1] + lf[:, 0:1]
                return it + 1, run, accs, jnp.max(run) > -SB_DEAD

            _, _, accs, _ = lax.while_loop(
                lambda st: jnp.logical_and(st[0] <= qi, st[3]), kstep,
                (jnp.int32(0), jnp.zeros((nh * blk, 1), F32), tuple([jnp.zeros((blk, LANES), F32)] * len(pairs)),
                 jnp.bool_(True)))
            for acc, cols in zip(accs, pairs):
                o_ref[pl.ds(q0, blk), cols] = acc.astype(o_ref.dtype)
            return carry

        lax.fori_loop(0, nblk, qstep, 0)

    col = lambda off: pl.BlockSpec((s, nh * dh), lambda p: (0, p + off))
    out, carried = hosted_call(body, hook, "sb_fwd", (nstep,), [col(0), col(nstep), col(2 * nstep)], col(0),
                               _sds((s, D_MODEL), MXU_DTYPE), [], ("parallel",), (qkv, qkv, qkv))
    return out if hook is None else (out, carried)


def sb_bwd(qkv, do):
    s = qkv.shape[0]
    blk = min(SB_BLOCK, s)
    nblk = s // blk
    nh = SB_HEADS_PER_STEP
    nstep = SB_HEADS // nh
    dh = SB_HEAD_DIM

    def body(q_ref, k_ref, v_ref, do_ref, dq_ref, dk_ref, dv_ref, dk_acc, dv_acc, run_ref):
        suffix_tri2 = jnp.concatenate([_tri(blk, True)] * 2, axis=0)
        prefix_tri2 = jnp.concatenate([_tri(blk, False)] * 2, axis=0)
        dk_acc[...] = jnp.zeros_like(dk_acc)
        dv_acc[...] = jnp.zeros_like(dv_acc)
        lane_lo = lax.broadcasted_iota(jnp.int32, (1, LANES), 1) < dh
        pairs = [slice(p * LANES, (p + 1) * LANES) for p in range(nh // 2)]

        def qstep(qi, carry):
            q0 = pl.multiple_of(qi * blk, blk)
            qst = [_stack_heads(q_ref[pl.ds(q0, blk), cols], lane_lo) for cols in pairs]
            dost = [_stack_heads(do_ref[pl.ds(q0, blk), cols], lane_lo) for cols in pairs]

            def sweep1(st):
                it, run, _ = st
                kb = qi - it
                run_ref[kb] = run
                _, lf, _ = _sb_logs(_sb_logits(qst, k_ref, pl.multiple_of(kb * blk, blk), pairs, blk), qi, kb, blk)
                run = run + jnp.sum(lf, axis=1, keepdims=True)
                return it + 1, run, jnp.max(run) > -SB_DEAD

            nlive, _, _ = lax.while_loop(
                lambda st: jnp.logical_and(st[0] <= qi, st[2]), sweep1,
                (jnp.int32(0), jnp.zeros((nh * blk, 1), F32), jnp.bool_(True)))

            def sweep2(kb, st):
                pg, dqs = st
                k0 = pl.multiple_of(kb * blk, blk)
                lb, lf, strict = _sb_logs(_sb_logits(qst, k_ref, k0, pairs, blk), qi, kb, blk)
                sloc = _tri_sum(lf, suffix_tri2)
                a = jnp.where(strict, jnp.exp(lb + sloc + run_ref[kb]), 0.0)
                da = jnp.concatenate([_dot(d, v_ref[pl.ds(k0, blk), cols], _NT) for d, cols in zip(dost, pairs)], axis=0)
                g = da * a
                p = pg + _tri_sum(g, prefix_tri2)
                sig = jnp.exp(lb)
                dz = _mx(jnp.where(strict, g * (1.0 - sig) - p * sig, 0.0) * (dh ** -0.5))
                am = _mx(a)
                new_dqs = []
                for i, cols in enumerate(pairs):
                    rows = slice(2 * blk * i, 2 * blk * (i + 1))
                    new_dqs.append(dqs[i] + _unstack_heads(_dot(dz[rows], k_ref[pl.ds(k0, blk), cols]), lane_lo))
                    dk_acc[pl.ds(k0, blk), cols] += _dot(dz[rows], qst[i], _TN)
                    dv_acc[pl.ds(k0, blk), cols] += _dot(am[rows], dost[i], _TN)
                return pg + jnp.sum(g, axis=1, keepdims=True), tuple(new_dqs)

            _, dqs = lax.fori_loop(qi + 1 - nlive, qi + 1, sweep2,
                                   (jnp.zeros((nh * blk, 1), F32), tuple([jnp.zeros((blk, LANES), F32)] * len(pairs))))
            for dq, cols in zip(dqs, pairs):
                dq_ref[pl.ds(q0, blk), cols] = dq.astype(dq_ref.dtype)
            return carry

        lax.fori_loop(0, nblk, qstep, 0)
        dk_ref[...] = dk_acc[...].astype(dk_ref.dtype)
        dv_ref[...] = dv_acc[...].astype(dv_ref.dtype)

    col = lambda off: pl.BlockSpec((s, nh * dh), lambda p: (0, p + off))
    dq, dk, dv = pl.pallas_call(
        body, name="sb_bwd", grid=(nstep,), in_specs=[col(0), col(nstep), col(2 * nstep), col(0)],
        out_specs=[col(0), col(0), col(0)], out_shape=[_sds((s, D_MODEL), MXU_DTYPE)] * 3,
        scratch_shapes=[pltpu.VMEM((s, nh * dh), F32), pltpu.VMEM((s, nh * dh), F32), pltpu.VMEM((nblk, nh * blk, 1), F32)],
        compiler_params=_params(("parallel",)),
    )(qkv, qkv, qkv, do)
    return jnp.concatenate([dq, dk, dv], axis=1)


SSD_XBC_TILE0 = SSD_D_INNER // CONV_COLS


def ssd_conv_fwd(proj, cw, cb):
    s = proj.shape[0]
    rows = min(CONV_ROWS, s)

    def body(u_ref, w_ref, b_ref, o_ref):
        def step(i, carry):
            u, _ = _conv_rows(u_ref, w_ref, b_ref, i, rows, SSD_CONV)
            o_ref[pl.ds(pl.multiple_of(i * rows, rows), rows), :] = _silu(u)
            return carry

        lax.fori_loop(0, s // rows, step, 0)

    return pl.pallas_call(
        body, name="ssd_conv_fwd", grid=(SSD_CONV_DIM // CONV_COLS,),
        in_specs=[pl.BlockSpec((s, CONV_COLS), lambda j: (0, j + SSD_XBC_TILE0)),
                  pl.BlockSpec((SSD_CONV, CONV_COLS), lambda j: (0, j)), pl.BlockSpec((1, CONV_COLS), lambda j: (0, j))],
        out_specs=pl.BlockSpec((s, CONV_COLS), lambda j: (0, j)),
        out_shape=_sds((s, SSD_CONV_DIM), F32), compiler_params=_params(("parallel",)),
    )(proj, cw, cb)


def ssd_conv_bwd(proj, dact, cw, cb):
    s = proj.shape[0]
    rows = min(CONV_ROWS, s)
    nsteps = s // rows
    w = SSD_CONV

    def body(u_ref, da_ref, w_ref, b_ref, du_ref, dw_ref, db_ref):
        def step(it, carry):
            nxt, accs = carry
            i = nsteps - 1 - it
            r0 = pl.multiple_of(i * rows, rows)
            u, sh = _conv_rows(u_ref, w_ref, b_ref, i, rows, w)
            dconv = da_ref[pl.ds(r0, rows), :] * _silu_and_grad(u)[1]
            du_ref[pl.ds(r0, rows), :] = _conv_bwd_rows(dconv, nxt, w_ref, w).astype(du_ref.dtype)
            new = [accs[j] + jnp.sum(dconv * sh[w - 1 - j], axis=0, keepdims=True) for j in range(w)]
            new.append(accs[w] + jnp.sum(dconv, axis=0, keepdims=True))
            return dconv[0:SUBLANES], tuple(new)

        zrow = jnp.zeros((1, CONV_COLS), F32)
        _, accs = lax.fori_loop(0, nsteps, step, (jnp.zeros((SUBLANES, CONV_COLS), F32), tuple([zrow] * (w + 1))))
        dw_ref[...] = jnp.concatenate(accs[0:w], axis=0)
        db_ref[...] = accs[w]

    col = pl.BlockSpec((s, CONV_COLS), lambda j: (0, j))
    return pl.pallas_call(
        body, name="ssd_conv_bwd", grid=(SSD_CONV_DIM // CONV_COLS,),
        in_specs=[pl.BlockSpec((s, CONV_COLS), lambda j: (0, j + SSD_XBC_TILE0)), col,
                  pl.BlockSpec((w, CONV_COLS), lambda j: (0, j)), pl.BlockSpec((1, CONV_COLS), lambda j: (0, j))],
        out_specs=[col, pl.BlockSpec((w, CONV_COLS), lambda j: (0, j)), pl.BlockSpec((1, CONV_COLS), lambda j: (0, j))],
        out_shape=[_sds((s, SSD_CONV_DIM), MXU_DTYPE), _sds((w, SSD_CONV_DIM), F32), _sds((1, SSD_CONV_DIM), F32)],
        compiler_params=_params(("parallel",)),
    )(proj, dact, cw, cb)


def _split3(x):
    hi = x.astype(MXU_DTYPE)
    r1 = x - hi.astype(F32)
    mid = r1.astype(MXU_DTYPE)
    lo = (r1 - mid.astype(F32)).astype(MXU_DTYPE)
    return hi, mid, lo


def _dot01(x, m, dims=_NN, left=False):
    parts = _split3(x)
    if left:
        return _dot(m, parts[0], dims) + _dot(m, parts[1], dims) + _dot(m, parts[2], dims)
    return _dot(parts[0], m, dims) + _dot(parts[1], m, dims) + _dot(parts[2], m, dims)


def _softplus(x):
    return jnp.maximum(x, 0.0) + jnp.log1p(jnp.exp(-jnp.abs(x)))


def _ssd_consts(dt_bias, a_log, d_skip):
    pad = lambda v: jnp.pad(v.reshape(1, SSD_HEADS), ((0, 0), (0, LANES - SSD_HEADS)))
    head_of = jnp.arange(SSD_D_INNER) // SSD_HEAD_DIM
    expand = (jnp.arange(LANES)[:, None] == head_of[None, :]).astype(MXU_DTYPE)
    return dict(bias_w=pad(dt_bias), alog_w=pad(a_log), bias_c=dt_bias.reshape(SSD_HEADS, 1),
                alog_c=a_log.reshape(SSD_HEADS, 1), dskip=jnp.repeat(d_skip, SSD_HEAD_DIM).reshape(1, SSD_D_INNER),
                expand=expand, reduce=expand.T)


def _ssd_chunk_prep(dtp, dtp_t, bias_w, alog_w, bias_c, alog_c, expand):
    L = dtp.shape[0]
    r = lax.broadcasted_iota(jnp.int32, (L, L), 0)
    c = lax.broadcasted_iota(jnp.int32, (L, L), 1)
    tril = r >= c
    lower = jnp.where(tril, 1.0, 0.0).astype(MXU_DTYPE)
    upper = jnp.where(r <= c, 1.0, 0.0).astype(MXU_DTYPE)
    dt_col = _softplus(dtp + bias_w)
    a_col = -jnp.exp(alog_w) * dt_col
    a_row = -jnp.exp(alog_c) * _softplus(dtp_t + bias_c)
    acum_col = _dot01(a_col, lower, left=True)
    acum_row = _dot01(a_row, upper)
    acum_full = _dot01(acum_col, expand)
    dt_full = _dot01(dt_col, expand)
    return dict(tril=tril, lower=lower, upper=upper, dt_col=dt_col, a_col=a_col, acum_col=acum_col,
                acum_row=acum_row, acum_full=acum_full, dt_full=dt_full)


def _head_mask(j):
    lane = lax.broadcasted_iota(jnp.int32, (1, LANES), 1)
    return jnp.where((lane // SSD_HEAD_DIM) == j, 1.0, 0.0)


def _decay(pre, h):
    seg = pre["acum_col"][:, h:h + 1] - pre["acum_row"][h:h + 1, :]
    return jnp.exp(jnp.where(pre["tril"], seg, -1e30))


def _ssd_specs(s, nc, rev):
    L = SSD_CHUNK
    ci = (lambda i: nc - 1 - i) if rev else (lambda i: i)
    const = lambda shape: pl.BlockSpec(shape, lambda i: (0,) * len(shape))
    return dict(
        xbc=pl.BlockSpec((L, SSD_CONV_DIM), lambda i: (ci(i), 0)),
        dtp=pl.BlockSpec((L, LANES), lambda i: (ci(i), SSD_IN_PAD // LANES - 1)),
        dtp_t=pl.BlockSpec((SSD_HEADS, L), lambda i: (0, ci(i))),
        rows=pl.BlockSpec((L, SSD_D_INNER), lambda i: (ci(i), 0)),
        state=pl.BlockSpec((1, SSD_GROUPS, SSD_STATE, 4 * SSD_HEAD_DIM), lambda i: (ci(i), 0, 0, 0)),
        consts=[const((1, LANES)), const((1, LANES)), const((SSD_HEADS, 1)), const((SSD_HEADS, 1)),
                const((1, SSD_D_INNER)), const((LANES, SSD_D_INNER)), const((SSD_D_INNER, LANES))],
    )


def _const_args(cs):
    return [cs["bias_w"], cs["alog_w"], cs["bias_c"], cs["alog_c"], cs["dskip"], cs["expand"], cs["reduce"]]


def ssd_scan_fwd(act, proj, dtp_t, cs, hook=None):
    s = act.shape[0]
    L = SSD_CHUNK
    nc = s // L
    G, N, GW = SSD_GROUPS, SSD_STATE, 4 * SSD_HEAD_DIM

    def body(act_ref, dtp_ref, dtpt_ref, bw_ref, aw_ref, bc_ref, ac_ref, dsk_ref, ex_ref, rd_ref, y_ref, st_out, st):
        @pl.when(pl.program_id(0) == 0)
        def _():
            st[...] = jnp.zeros_like(st)

        st_out[0] = st[...]
        pre = _ssd_chunk_prep(dtp_ref[...], dtpt_ref[...], bw_ref[...], aw_ref[...], bc_ref[...], ac_ref[...], ex_ref[...])
        acum_full = pre["acum_full"]
        last_full = acum_full[L - 1:L, :]
        for g in range(G):
            bg = _mx(act_ref[:, SSD_D_INNER + g * N:SSD_D_INNER + (g + 1) * N])
            cg = _mx(act_ref[:, SSD_D_INNER + G * N + g * N:SSD_D_INNER + G * N + (g + 1) * N])
            cb = _dot(cg, bg, _NT)
            for half in range(2):
                p = 2 * g + half
                cols = slice(p * LANES, (p + 1) * LANES)
                xs = act_ref[:, cols]
                xdt = xs * pre["dt_full"][:, cols]
                yd = jnp.zeros((L, LANES), F32)
                for j in range(2):
                    m = cb * _decay(pre, 2 * p + j)
                    yd = yd + _dot(_mx(m), _mx(xdt * _head_mask(j)))
                yoff = _dot(cg, _mx(st[g, :, half * LANES:(half + 1) * LANES])) * jnp.exp(acum_full[:, cols])
                y_ref[:, cols] = yd + yoff + dsk_ref[:, cols] * xs
                w = jnp.exp(last_full[:, cols] - acum_full[:, cols])
                st[g, :, half * LANES:(half + 1) * LANES] = (
                    st[g, :, half * LANES:(half + 1) * LANES] * jnp.exp(last_full[:, cols]) + _dot(bg, _mx(xdt * w), _TN))

    sp = _ssd_specs(s, nc, False)
    (y, states), carried = hosted_call(
        body, hook, "ssd_scan_fwd", (nc,), [sp["xbc"], sp["dtp"], sp["dtp_t"]] + sp["consts"],
        [sp["rows"], sp["state"]], [_sds((s, SSD_D_INNER), F32), _sds((nc, G, N, GW), F32)],
        [pltpu.VMEM((G, N, GW), F32)], ("arbitrary",), (act, proj, dtp_t, *_const_args(cs)))
    return y, states, carried


def ssd_scan_bwd(act, proj, dtp_t, cs, states, dy):
    s = act.shape[0]
    L = SSD_CHUNK
    nc = s // L
    G, N, GW = SSD_GROUPS, SSD_STATE, 4 * SSD_HEAD_DIM

    def body(act_ref, dtp_ref, dtpt_ref, bw_ref, aw_ref, bc_ref, ac_ref, dsk_ref, ex_ref, rd_ref, st_ref, dy_ref,
             dact_ref, ddtp_ref, dalog_ref, dbias_ref, dskip_ref, dst, dxdt_ref, dac_ref):
        first = pl.program_id(0) == 0

        @pl.when(first)
        def _():
            dst[...] = jnp.zeros_like(dst)
            dalog_ref[...] = jnp.zeros_like(dalog_ref)
            dbias_ref[...] = jnp.zeros_like(dbias_ref)
            dskip_ref[...] = jnp.zeros_like(dskip_ref)

        expand, reduce = ex_ref[...], rd_ref[...]
        pre = _ssd_chunk_prep(dtp_ref[...], dtpt_ref[...], bw_ref[...], aw_ref[...], bc_ref[...], ac_ref[...], expand)
        acum_full = pre["acum_full"]
        last_full = acum_full[L - 1:L, :]
        ones = jnp.ones((L, LANES), MXU_DTYPE)
        lane = lax.broadcasted_iota(jnp.int32, (L, LANES), 1)
        dacum_diag = jnp.zeros((L, LANES), F32)
        dlast_parts = []
        for g in range(G):
            bg = _mx(act_ref[:, SSD_D_INNER + g * N:SSD_D_INNER + (g + 1) * N])
            cg = _mx(act_ref[:, SSD_D_INNER + G * N + g * N:SSD_D_INNER + G * N + (g + 1) * N])
            cb = _dot(cg, bg, _NT)
            dcb = jnp.zeros((L, L), F32)
            dcg = jnp.zeros((L, N), F32)
            dbg = jnp.zeros((L, N), F32)
            for half in range(2):
                p = 2 * g + half
                cols = slice(p * LANES, (p + 1) * LANES)
                hcols = slice(half * LANES, (half + 1) * LANES)
                xs = act_ref[:, cols]
                xdt = xs * pre["dt_full"][:, cols]
                dyv = dy_ref[:, cols]
                dxdt = jnp.zeros((L, LANES), F32)
                for j in range(2):
                    h = 2 * p + j
                    dec = _decay(pre, h)
                    m = cb * dec
                    dyh = _mx(dyv * _head_mask(j))
                    dm = _dot(dyh, _mx(xdt), _NT)
                    dxdt = dxdt + _dot(_mx(m), dyh, _TN)
                    e = dm * m
                    ehi, elo = _split_hi_lo(e)
                    d_h = (_dot(ehi, ones) + _dot(elo, ones)) - (_dot(ehi, ones, _TN) + _dot(elo, ones, _TN))
                    dacum_diag = jnp.where(lane == h, d_h, dacum_diag)
                    dcb = dcb + dm * dec
                lam = jnp.exp(acum_full[:, cols])
                stv = _mx(st_ref[0, g, :, hcols])
                z = _dot(cg, stv)
                dz = _mx(lam * dyv)
                dcg = dcg + _dot(dz, stv, _NT)
                dst_in = _dot(cg, dz, _TN)
                dsv = dst[g, :, hcols]
                w = jnp.exp(last_full[:, cols] - acum_full[:, cols])
                q = _dot(bg, _mx(dsv))
                wq = w * q
                dxdt = dxdt + wq
                wqx = wq * xdt
                dbg = dbg + _dot(_mx(xdt * w), _mx(dsv), _NT)
                elast = jnp.exp(last_full[:, cols])
                dlast_p = jnp.sum(wqx, axis=0, keepdims=True) + elast * jnp.sum(dsv * st_ref[0, g, :, hcols], axis=0, keepdims=True)
                dac_ref[:, cols] = dyv * z * lam - wqx
                dlast_parts.append(dlast_p)
                dst[g, :, hcols] = dst_in + dsv * elast
                dxdt_ref[:, cols] = dxdt
                dact_ref[:, cols] = dxdt * pre["dt_full"][:, cols] + dsk_ref[:, cols] * dyv
            dcbm = _mx(dcb)
            dact_ref[:, SSD_D_INNER + g * N:SSD_D_INNER + (g + 1) * N] = dbg + _dot(dcbm, cg, _TN)
            dact_ref[:, SSD_D_INNER + G * N + g * N:SSD_D_INNER + G * N + (g + 1) * N] = dcg + _dot(dcbm, bg)

        xs_all = act_ref[:, 0:SSD_D_INNER]
        dacum = dacum_diag + _dot_exact01(dac_ref[...], reduce)
        dlast = _dot_exact01(jnp.concatenate(dlast_parts, axis=1), reduce)
        row = lax.broadcasted_iota(jnp.int32, (L, LANES), 0)
        dacum = dacum + jnp.where(row == L - 1, dlast, 0.0)
        da_col = _dot01(dacum, pre["upper"], left=True)
        a_w = -jnp.exp(aw_ref[...])
        ddt = a_w * da_col + _dot_exact01(dxdt_ref[...] * xs_all, reduce)
        xin = dtp_ref[...] + bw_ref[...]
        ddtp = ddt * (1.0 / (1.0 + jnp.exp(-xin)))
        valid = lane < SSD_HEADS
        ddtp = jnp.where(valid, ddtp, 0.0)
        ddtp_ref[...] = ddtp
        dbias_ref[...] += jnp.sum(ddtp, axis=0, keepdims=True)
        dalog_ref[...] += jnp.sum(jnp.where(valid, da_col * pre["a_col"], 0.0), axis=0, keepdims=True)
        dskip_ref[...] += jnp.sum(_dot_exact01(dy_ref[...] * xs_all, reduce), axis=0, keepdims=True)

    sp = _ssd_specs(s, nc, True)
    acc = pl.BlockSpec((1, LANES), lambda i: (0, 0))
    return pl.pallas_call(
        body, name="ssd_scan_bwd", grid=(nc,),
        in_specs=[sp["xbc"], sp["dtp"], sp["dtp_t"]] + sp["consts"] + [sp["state"], sp["rows"]],
        out_specs=[sp["xbc"], pl.BlockSpec((L, LANES), lambda i: (nc - 1 - i, 0)), acc, acc, acc],
        out_shape=[_sds((s, SSD_CONV_DIM), F32), _sds((s, LANES), F32)] + [_sds((1, LANES), F32)] * 3,
        scratch_shapes=[pltpu.VMEM((G, N, GW), F32), pltpu.VMEM((L, SSD_D_INNER), F32), pltpu.VMEM((L, SSD_D_INNER), F32)],
        compiler_params=_params(("arbitrary",)),
    )(act, proj, dtp_t, *_const_args(cs), states, dy)


def ssd_post_fwd(y, proj, g):
    s, d = y.shape
    ts = _pick(s, (256, 128))

    def body(y_ref, z_ref, g_ref, o_ref):
        y2 = y_ref[...] * _silu(z_ref[...])
        r = lax.rsqrt(jnp.mean(y2 * y2, axis=-1, keepdims=True) + NORM_EPS)
        o_ref[...] = (y2 * r * g_ref[...]).astype(o_ref.dtype)

    row = pl.BlockSpec((ts, d), lambda i: (i, 0))
    return pl.pallas_call(
        body, name="ssd_post_fwd", grid=(s // ts,), in_specs=[row, row, pl.BlockSpec((1, d), lambda i: (0, 0))],
        out_specs=row, out_shape=_sds((s, d), MXU_DTYPE), compiler_params=_params(("parallel",)),
    )(y, proj, g)


def ssd_post_bwd(y, proj, g, dy3):
    s, d = y.shape
    ts = _pick(s, (256, 128))

    def body(y_ref, z_ref, g_ref, d3_ref, dy_ref, dz_ref, dg_ref):
        yv, zv = y_ref[...], z_ref[...]
        sz, sgrad = _silu_and_grad(zv)
        y2 = yv * sz
        r = lax.rsqrt(jnp.mean(y2 * y2, axis=-1, keepdims=True) + NORM_EPS)
        xh = y2 * r
        d3 = d3_ref[...]
        dxh = d3 * g_ref[...]
        dy2 = r * (dxh - xh * jnp.mean(dxh * xh, axis=-1, keepdims=True))
        dy_ref[...] = dy2 * sz
        dz_ref[...] = (dy2 * yv * sgrad).astype(dz_ref.dtype)
        part = jnp.sum(d3 * xh, axis=0, keepdims=True)

        @pl.when(pl.program_id(0) == 0)
        def _():
            dg_ref[...] = part

        @pl.when(pl.program_id(0) != 0)
        def _():
            dg_ref[...] += part

    row = pl.BlockSpec((ts, d), lambda i: (i, 0))
    vec = pl.BlockSpec((1, d), lambda i: (0, 0))
    return pl.pallas_call(
        body, name="ssd_post_bwd", grid=(s // ts,), in_specs=[row, row, vec, row], out_specs=[row, row, vec],
        out_shape=[_sds((s, d), F32), _sds((s, d), MXU_DTYPE), _sds((1, d), F32)],
        compiler_params=_params(("arbitrary",)),
    )(y, proj, g, dy3)


def ssd_core_fwd(proj, cw, cb, dt_bias, a_log, d_skip, norm_g, hook=None):
    cs = _ssd_consts(dt_bias, a_log, d_skip)
    act = ssd_conv_fwd(proj, cw, cb)
    dtp_t = proj[:, SSD_IN_DIM - SSD_HEADS:SSD_IN_DIM].T
    y, states, carried = ssd_scan_fwd(act, proj, dtp_t, cs, hook)
    y3 = ssd_post_fwd(y, proj, norm_g)
    return y3, (cs, act, dtp_t, y, states), carried


def ssd_core_bwd(proj, cw, cb, norm_g, saved, dy3):
    cs, act, dtp_t, y, states = saved
    dy, dz, dnorm = ssd_post_bwd(y, proj, norm_g, dy3)
    dact, ddtp, dalog, dbias, dskip = ssd_scan_bwd(act, proj, dtp_t, cs, states, dy)
    dxbc, dcw, dcb = ssd_conv_bwd(proj, dact, cw, cb)
    dproj = jnp.concatenate([dz, dxbc, ddtp.astype(MXU_DTYPE)], axis=1)
    h = SSD_HEADS
    return dproj, dcw, dcb, dbias[0, :h], dalog[0, :h], dskip[0, :h], dnorm


def ssd_core(proj, cw, cb, dt_bias, a_log, d_skip, norm_g, dy3):
    y3, saved, _ = ssd_core_fwd(proj, cw, cb, dt_bias, a_log, d_skip, norm_g)
    return y3, ssd_core_bwd(proj, cw, cb, norm_g, saved, dy3)


def loss_head(x, g, target):
    s, d = x.shape
    ts = _pick(s, (512, 256, 128))

    def body(x_ref, g_ref, t_ref, loss_ref, dx_ref, dxm_ref, dg_ref):
        xv = x_ref[...]
        r = lax.rsqrt(jnp.mean(xv * xv, axis=-1, keepdims=True) + NORM_EPS)
        xh = xv * r
        err = xh * g_ref[...] - t_ref[...]
        dy = err * (1.0 / d)
        dxh = dy * g_ref[...]
        dx = r * (dxh - xh * jnp.mean(dxh * xh, axis=-1, keepdims=True))
        dx_ref[...] = dx
        dxm_ref[...] = dx.astype(dxm_ref.dtype)
        part = jnp.sum(dy * xh, axis=0, keepdims=True)
        lpart = jnp.full((1, LANES), 0.5 * jnp.sum(jnp.mean(err * err, axis=-1, keepdims=True)), F32)

        @pl.when(pl.program_id(0) == 0)
        def _():
            dg_ref[...] = part
            loss_ref[...] = lpart

        @pl.when(pl.program_id(0) != 0)
        def _():
            dg_ref[...] += part
            loss_ref[...] += lpart

    row = pl.BlockSpec((ts, d), lambda i: (i, 0))
    vec = pl.BlockSpec((1, d), lambda i: (0, 0))
    return pl.pallas_call(
        body, name="loss_head", grid=(s // ts,), in_specs=[row, vec, row],
        out_specs=[pl.BlockSpec((1, LANES), lambda i: (0, 0)), row, row, vec],
        out_shape=[_sds((1, LANES), F32), _sds((s, d), F32), _sds((s, d), MXU_DTYPE), _sds((1, d), F32)],
        compiler_params=_params(("arbitrary",)),
    )(x, g, target)


def _adamw_math(w, g, m, v):
    m = ADAM_B1 * m + (1.0 - ADAM_B1) * g
    v = ADAM_B2 * v + (1.0 - ADAM_B2) * (g * g)
    m_hat = m / (1.0 - ADAM_B1 ** ADAM_STEP)
    v_hat = v / (1.0 - ADAM_B2 ** ADAM_STEP)
    return -ADAM_LR * (m_hat / (jnp.sqrt(v_hat) + ADAM_EPS) + ADAM_WD * w), m, v


def adamw(w, g, m, v, name="adamw"):
    r, c = w.shape
    tr = _pick(r, (256, 128, 64, 32, 16, 8))

    def body(w_ref, g_ref, m_ref, v_ref, d_ref, nm_ref, nv_ref):
        d_ref[...], nm_ref[...], nv_ref[...] = _adamw_math(w_ref[...], g_ref[...], m_ref[...], v_ref[...])

    blk = pl.BlockSpec((tr, c), lambda i: (i, 0))
    return pl.pallas_call(
        body, name=name, grid=(r // tr,), in_specs=[blk] * 4, out_specs=[blk] * 3,
        out_shape=[_sds((r, c), F32)] * 3, compiler_params=_params(("parallel",)),
    )(w, g, m, v)


def adamw_small(w, parts, m, v):
    n, r, c = parts.shape

    def body(w_ref, p_ref, m_ref, v_ref, g_ref, d_ref, nm_ref, nv_ref):
        g = p_ref[0]
        for k in range(1, n):
            g = g + p_ref[k]
        g_ref[...] = g
        d_ref[...], nm_ref[...], nv_ref[...] = _adamw_math(w_ref[...], g, m_ref[...], v_ref[...])

    return pl.pallas_call(
        body, name="adamw_small", out_shape=[_sds((r, c), F32)] * 4, compiler_params=_params(),
    )(w, parts, m, v)


def pair_sum(unit, recv, half):
    nchip, _, r, c = unit.shape
    tr = _pick(r, (512, 256, 176, 128, 64, 32, 16))

    def body(h_ref, a_ref, b_ref, o_ref, ob_ref):
        sm = a_ref[0, 0] + b_ref[0]
        o_ref[0] = sm
        ob_ref[0] = sm.astype(ob_ref.dtype)

    blk = pl.BlockSpec((1, tr, c), lambda s, i, h: (s, i, 0))
    return pl.pallas_call(
        body, name="pair_sum",
        grid_spec=pltpu.PrefetchScalarGridSpec(
            num_scalar_prefetch=1, grid=(nchip, r // tr),
            in_specs=[pl.BlockSpec((1, 1, tr, c), lambda s, i, h: (s, h[0], i, 0)), blk], out_specs=[blk, blk]),
        out_shape=[_sds((nchip, r, c), F32), _sds((nchip, r, c), jnp.bfloat16)],
        compiler_params=_params(("parallel", "parallel")),
    )(half, unit, recv)


def chip_sum(own, where, recv, layer, layers, prev=None):
    _, r, c = own.shape
    tr = _pick(r, (512, 256, 176, 128, 64, 32, 16))

    def body(s_ref, a_ref, b_ref, *rest):
        rest[-1][...] = a_ref[0] + b_ref[0].astype(F32) + b_ref[1].astype(F32) + b_ref[2].astype(F32)

    in_specs = [pl.BlockSpec((1, tr, c), lambda i, s: (s[0], i, 0)), pl.BlockSpec((3, tr, c), lambda i, s: (0, i, 0))]
    args = [where, own, recv]
    if prev is not None:
        in_specs.append(ANY)
        args.append(prev)
    return pl.pallas_call(
        body, name="chip_sum",
        grid_spec=pltpu.PrefetchScalarGridSpec(
            num_scalar_prefetch=1, grid=(r // tr,), in_specs=in_specs,
            out_specs=pl.BlockSpec((None, None, tr, c), lambda i, s: (layer, s[1], i, 0))),
        out_shape=_sds((layers, 2, r, c), F32), input_output_aliases={} if prev is None else {3: 0},
        compiler_params=_params(("parallel",)),
    )(*args)


def place_cast(w, layer, chip):
    _, a, b = w.shape
    ta = _pick(a, (512, 352, 256, 128))

    def body(c_ref, w_ref, o_ref):
        o_ref[...] = w_ref[...].astype(o_ref.dtype)

    return pl.pallas_call(
        body, name="place_cast",
        grid_spec=pltpu.PrefetchScalarGridSpec(
            num_scalar_prefetch=1, grid=(a // ta,),
            in_specs=[pl.BlockSpec((None, ta, b), lambda i, c: (layer, i, 0))],
            out_specs=pl.BlockSpec((None, ta, b), lambda i, c: (c[0], i, 0))),
        out_shape=_sds((N_CHIPS, a, b), MXU_DTYPE), compiler_params=_params(("parallel",)),
    )(chip, w)


ANY = pl.BlockSpec(memory_space=pl.ANY)
COMM = pltpu.CompilerParams(has_side_effects=True)


def _coords():
    return lax.axis_index("x"), lax.axis_index("y"), lax.axis_index("c")


def _other_chips(x, y):
    return [(1 - x, y), (x, 1 - y), (1 - x, 1 - y)]


def all_gather_8(halves, name):
    _, r, c = halves.shape

    def body(h_ref, out_ref, send_sems, recv_sems, local_sem):
        x, y, cc = _coords()
        _gather_one(h_ref.at[cc], lambda px, py, pc: out_ref.at[4 * px + 2 * py + pc],
                    lambda k: send_sems.at[k], lambda k: recv_sems.at[k], local_sem)

    return pl.pallas_call(
        body, name=name, in_specs=[ANY], out_specs=ANY, out_shape=_sds((8, r, c), halves.dtype),
        scratch_shapes=[pltpu.SemaphoreType.DMA((7,)), pltpu.SemaphoreType.DMA((7,)), pltpu.SemaphoreType.DMA],
        compiler_params=COMM,
    )(halves)


def _gather_plan(x_ref, slot, send_sem, recv_sem, local_sem):
    x, y, cc = _coords()
    me, sibling = (x, y, cc), (x, y, 1 - cc)
    chips = _other_chips(x, y)

    def copy(k, blk, to, src=None):
        return pltpu.make_async_remote_copy(
            src_ref=slot(*blk) if src is None else src, dst_ref=slot(*blk),
            send_sem=send_sem(k), recv_sem=recv_sem(k), device_id=to, device_id_type=MESH)

    mine = pltpu.make_async_copy(x_ref, slot(*me), local_sem)
    first = [copy(0, me, sibling, src=x_ref)] + [copy(1 + j, me, (*chip, cc), src=x_ref) for j, chip in enumerate(chips)]
    passed = [copy(4 + j, (*chip, cc), sibling) for j, chip in enumerate(chips)]
    over_ici = [copy(1 + j, (*chip, cc), me) for j, chip in enumerate(chips)]
    from_sibling = [copy(0, sibling, me)] + [copy(4 + j, (*chip, 1 - cc), me) for j, chip in enumerate(chips)]
    return mine, first, passed, over_ici, from_sibling


def _gather_run(plans):
    for mine, first, _, _, _ in plans:
        mine.start()
        for cp in first:
            cp.start()
    for j in range(3):
        for _, _, passed, over_ici, _ in plans:
            over_ici[j].wait_recv()
            passed[j].start()
    for mine, first, passed, _, from_sibling in plans:
        for cp in from_sibling:
            cp.wait_recv()
        for cp in first + passed:
            cp.wait_send()
        mine.wait()


def _gather_one(x_ref, slot, send_sem, recv_sem, local_sem):
    _gather_run([_gather_plan(x_ref, slot, send_sem, recv_sem, local_sem)])


def gather_hook(items):
    n = len(items)

    def plan(refs, send_sems, recv_sems):
        x, y, cc = _coords()

        def copy(i, k, px, py, pc, to):
            blk = refs[i].at[2 * px + py, pc]
            return pltpu.make_async_remote_copy(src_ref=blk, dst_ref=blk, send_sem=send_sems.at[i, k],
                                                recv_sem=recv_sems.at[i, k], device_id=to, device_id_type=MESH)

        chips = _other_chips(x, y)
        first = [copy(i, j, x, y, cc, (*chip, cc)) for i in range(n) for j, chip in enumerate(chips)]
        return copy, chips, first, (x, y, cc)

    def start(refs, sems):
        for cp in plan(refs, *sems)[2]:
            cp.start()

    def finish(refs, sems):
        copy, chips, first, (x, y, cc) = plan(refs, *sems)
        passed = []
        for j, chip in enumerate(chips):
            for i in range(n):
                copy(i, j, *chip, cc, (x, y, cc)).wait_recv()
                passed.append(copy(i, 3 + j, *chip, cc, (x, y, 1 - cc)))
                passed[-1].start()
        for j, chip in enumerate(chips):
            for i in range(n):
                copy(i, 3 + j, *chip, 1 - cc, (x, y, cc)).wait_recv()
        for cp in first + passed:
            cp.wait_send()

    return dict(arrays=list(items), start=start, finish=finish,
                sems=[pltpu.SemaphoreType.DMA((n, 6)), pltpu.SemaphoreType.DMA((n, 6))])


def hosted_call(body, hook, name, grid, in_specs, out_specs, out_shape, scratch_shapes, sem, args):
    single = not isinstance(out_shape, (list, tuple))
    out_specs_l = [out_specs] if single else list(out_specs)
    out_shape_l = [out_shape] if single else list(out_shape)
    if hook is None:
        res = pl.pallas_call(body, name=name, grid=grid, in_specs=list(in_specs), out_specs=out_specs, out_shape=out_shape,
                             scratch_shapes=list(scratch_shapes), compiler_params=_params(sem))(*args)
        return res, []
    items = hook["arrays"]
    k, n_in, n_out, n_scr = len(items), len(in_specs), len(out_specs_l), len(scratch_shapes)

    def full(*refs):
        ins = refs[:n_in]
        outs = refs[n_in + k:n_in + k + n_out]
        hrefs = refs[n_in + k + n_out:n_in + 2 * k + n_out]
        scr = refs[n_in + 2 * k + n_out:n_in + 2 * k + n_out + n_scr]
        sems = refs[n_in + 2 * k + n_out + n_scr:]
        ids = [pl.program_id(d) for d in range(len(grid))]
        first = functools.reduce(jnp.logical_and, [i == 0 for i in ids])
        last = functools.reduce(jnp.logical_and, [i == g - 1 for i, g in zip(ids, grid)])

        @pl.when(first)
        def _():
            hook["start"](hrefs, sems)

        body(*ins, *outs, *scr)

        @pl.when(last)
        def _():
            hook["finish"](hrefs, sems)

    res = pl.pallas_call(
        full, name=name, grid=grid, in_specs=list(in_specs) + [ANY] * k, out_specs=out_specs_l + [ANY] * k,
        out_shape=out_shape_l + [_sds(a.shape, a.dtype) for a in items],
        input_output_aliases={n_in + i: n_out + i for i in range(k)},
        scratch_shapes=list(scratch_shapes) + hook["sems"],
        compiler_params=pltpu.CompilerParams(dimension_semantics=("arbitrary",) * len(grid),
                                             vmem_limit_bytes=VMEM_LIMIT, has_side_effects=True),
    )(*args, *items)
    return (res[0] if single else list(res[:n_out])), list(res[n_out:])


def comm_call(hook, name):
    k = len(hook["arrays"])

    def body(*refs):
        hook["start"](refs[k:2 * k], refs[2 * k:])
        hook["finish"](refs[k:2 * k], refs[2 * k:])

    return list(pl.pallas_call(
        body, name=name, in_specs=[ANY] * k, out_specs=[ANY] * k,
        out_shape=[_sds(a.shape, a.dtype) for a in hook["arrays"]], input_output_aliases={i: i for i in range(k)},
        scratch_shapes=hook["sems"], compiler_params=COMM,
    )(*hook["arrays"]))


def grad_pair_swap(units):
    n = len(units)

    def body(*refs):
        ins, outs, send_sems, recv_sems = refs[:n], refs[n:2 * n], refs[2 * n], refs[2 * n + 1]
        x, y, cc = _coords()
        cps = [pltpu.make_async_remote_copy(src_ref=ins[i].at[:, 1 - cc], dst_ref=outs[i], send_sem=send_sems.at[i],
                                            recv_sem=recv_sems.at[i], device_id=(x, y, 1 - cc), device_id_type=MESH)
               for i in range(n)]
        for cp in cps:
            cp.start()
        for cp in cps:
            cp.wait()

    return pl.pallas_call(
        body, name="grad_pair_swap", in_specs=[ANY] * n, out_specs=[ANY] * n,
        out_shape=[_sds((u.shape[0],) + u.shape[2:], u.dtype) for u in units],
        scratch_shapes=[pltpu.SemaphoreType.DMA((n,)), pltpu.SemaphoreType.DMA((n,))], compiler_params=COMM,
    )(*units)


def grad_chip_exchange(units):
    n = len(units)

    def body(*refs):
        ins, outs, send_sems, recv_sems = refs[:n], refs[n:2 * n], refs[2 * n], refs[2 * n + 1]
        x, y, cc = _coords()
        cps = [pltpu.make_async_remote_copy(
            src_ref=ins[i].at[2 * px + py], dst_ref=outs[i].at[k], send_sem=send_sems.at[i, k],
            recv_sem=recv_sems.at[i, k], device_id=(px, py, cc), device_id_type=MESH)
            for i in range(n) for k, (px, py) in enumerate(_other_chips(x, y))]
        for cp in cps:
            cp.start()
        for cp in cps:
            cp.wait()

    return pl.pallas_call(
        body, name="grad_chip_exchange", in_specs=[ANY] * n, out_specs=[ANY] * n,
        out_shape=[_sds((3,) + u.shape[1:], u.dtype) for u in units],
        scratch_shapes=[pltpu.SemaphoreType.DMA((n, 3)), pltpu.SemaphoreType.DMA((n, 3))], compiler_params=COMM,
    )(*units)


def grad_half_swap(grads):
    n = len(grads)

    def body(*refs):
        outs, send_sems, recv_sems = refs[n:2 * n], refs[2 * n], refs[2 * n + 1]
        x, y, cc = _coords()
        cps = [pltpu.make_async_remote_copy(
            src_ref=outs[i].at[:, cc], dst_ref=outs[i].at[:, cc], send_sem=send_sems.at[i], recv_sem=recv_sems.at[i],
            device_id=(x, y, 1 - cc), device_id_type=MESH) for i in range(n)]
        for cp in cps:
            cp.start()
        for i, cp in enumerate(cps):
            cp.wait_send()
            pltpu.make_async_remote_copy(
                src_ref=outs[i].at[:, 1 - cc], dst_ref=outs[i].at[:, 1 - cc], send_sem=send_sems.at[i],
                recv_sem=recv_sems.at[i], device_id=(x, y, 1 - cc), device_id_type=MESH).wait_recv()

    return pl.pallas_call(
        body, name="grad_half_swap", in_specs=[ANY] * n, out_specs=[ANY] * n,
        out_shape=[_sds(g.shape, g.dtype) for g in grads], input_output_aliases={i: i for i in range(n)},
        scratch_shapes=[pltpu.SemaphoreType.DMA((n,)), pltpu.SemaphoreType.DMA((n,))], compiler_params=COMM,
    )(*grads)


N_CHIPS = 4
PACK_COLS = 1024
BIG = ("ssd_w_in", "ssd_w_out", "sb_w_qkv", "sb_w_out", "ffn_w_in", "ffn_w_out")
CONVW = ("ssd_conv_w", "ffn_conv_w")
COL_SHARDED = ("ssd_w_in", "sb_w_qkv", "ffn_w_in", "ssd_conv_w", "ffn_conv_w")
SMALL = ("mix_norm", "ffn_norm", "final_norm", "ssd_conv_b", "ssd_dt_bias", "ssd_a_log", "ssd_d", "ssd_norm", "ffn_conv_b")
WEIGHTS = ("mix_norm", "ffn_norm", "final_norm", "ssd_w_in", "ssd_conv_w", "ssd_conv_b", "ssd_dt_bias", "ssd_a_log",
           "ssd_d", "ssd_norm", "ssd_w_out", "sb_w_qkv", "sb_w_out", "ffn_w_in", "ffn_conv_w", "ffn_conv_b", "ffn_w_out")


def _to_rows(flat, multiple):
    rows = -(-flat.shape[-1] // PACK_COLS)
    rows = -(-rows // multiple) * multiple
    pad = rows * PACK_COLS - flat.shape[-1]
    return jnp.pad(flat, [(0, pad)]).reshape(rows, PACK_COLS)


def _unshard(name, stacked):
    l, n, a, b = stacked.shape
    if name in COL_SHARDED:
        return jnp.transpose(stacked, (0, 2, 1, 3)).reshape(l, a, n * b)
    return stacked.reshape(l, n * a, b)


def _gather_conv_weights(w):
    flat = jnp.concatenate([w[n].reshape(-1) for n in CONVW])
    rows = _to_rows(flat, 16)
    got = all_gather_8(rows.reshape(2, rows.shape[0] // 2, PACK_COLS), "gather_conv_weights").reshape(N_CHIPS, -1)
    out, off = {}, 0
    for n in CONVW:
        l, a, b = w[n].shape
        out[n] = _unshard(n, jnp.moveaxis(got[:, off:off + w[n].size].reshape(N_CHIPS, l, a, b), 0, 1))
        off += w[n].size
    return out


def _reduce_big_grads(units, layout):
    cc = lax.axis_index("c").astype(jnp.int32)
    chip = (2 * lax.axis_index("x") + lax.axis_index("y")).astype(jnp.int32)
    from_sibling = grad_pair_swap(units)
    pairs = [pair_sum(u, r, cc.reshape(1)) for u, r in zip(units, from_sibling)]
    from_chips = grad_chip_exchange([p[1] for p in pairs])
    where = jnp.stack([chip, cc])
    nlayers = [1 + max(l for k, l in layout if k == wi) for wi in range(1 + max(k for k, _ in layout))]
    grads = [None] * len(nlayers)
    for (wi, l), p, r in zip(layout, pairs, from_chips):
        grads[wi] = chip_sum(p[0], where, r, l, nlayers[wi], grads[wi])
    return grad_half_swap(grads)


def kernel(x, mix_norm, ffn_norm, final_norm, ssd_w_in, ssd_conv_w, ssd_conv_b, ssd_dt_bias, ssd_a_log, ssd_d, ssd_norm, ssd_w_out, sb_w_qkv, sb_w_out, ffn_w_in, ffn_conv_w, ffn_conv_b, ffn_w_out, loss_target, m_mix_norm, m_ffn_norm, m_final_norm, m_ssd_w_in, m_ssd_conv_w, m_ssd_conv_b, m_ssd_dt_bias, m_ssd_a_log, m_ssd_d, m_ssd_norm, m_ssd_w_out, m_sb_w_qkv, m_sb_w_out, m_ffn_w_in, m_ffn_conv_w, m_ffn_conv_b, m_ffn_w_out, v_mix_norm, v_ffn_norm, v_final_norm, v_ssd_w_in, v_ssd_conv_w, v_ssd_conv_b, v_ssd_dt_bias, v_ssd_a_log, v_ssd_d, v_ssd_norm, v_ssd_w_out, v_sb_w_qkv, v_sb_w_out, v_ffn_w_in, v_ffn_conv_w, v_ffn_conv_b, v_ffn_w_out):
    given = dict(locals())
    w = {n: given[n] for n in WEIGHTS}
    mom = {n: given["m_" + n] for n in WEIGHTS}
    var = {n: given["v_" + n] for n in WEIGHTS}
    chip = 2 * lax.axis_index("x") + lax.axis_index("y")

    chip1 = chip.reshape(1).astype(jnp.int32)
    fw = _gather_conv_weights(w)
    row = lambda v: v.reshape(1, -1)

    def placed(n, l):
        _, a, b = w[n].shape
        return place_cast(w[n], l, chip1).reshape(N_CHIPS, 2, a // 2, b)

    def mixer_items(i):
        return [(n, i // 2) for n in (("ssd_w_in", "ssd_w_out") if i % 2 == 0 else ("sb_w_qkv", "sb_w_out"))]

    def ffn_items(i):
        return [("ffn_w_in", i), ("ffn_w_out", i)]

    def hook_for(items):
        return gather_hook([placed(n, l) for n, l in items]) if items else None

    lw = {}

    def arrived(items, arrays):
        for (n, l), arr in zip(items, arrays):
            g4 = arr.reshape(N_CHIPS, -1, arr.shape[-1])
            if n in COL_SHARDED:
                full = jnp.transpose(g4, (1, 0, 2)).reshape(g4.shape[1], -1)
            else:
                full = g4.reshape(-1, g4.shape[2])
            if n == "ssd_w_in":
                full = jnp.pad(full, ((0, 0), (0, SSD_IN_PAD - SSD_IN_DIM)))
            lw[(n, l)] = full

    first_items = mixer_items(0) + ffn_items(0)
    arrived(first_items, comm_call(hook_for(first_items), "gather_layer0"))

    xcur = x[0]
    saved = []
    for i in range(DEPTH):
        j = i // 2
        nxt = i + 1 < DEPTH
        h, r = rms_fwd(xcur, row(mix_norm[i]))
        if i % 2 == 0:
            items_a, items_b = (mixer_items(i + 1), ffn_items(i + 1)) if nxt else ([], [])
            proj = mm(h, lw[("ssd_w_in", j)], tn=896, name="mm_ssd_in", hook=hook_for(items_a))
            if items_a:
                proj, got = proj
                arrived(items_a, got)
            y3, core, got = ssd_core_fwd(proj, fw["ssd_conv_w"][j], row(ssd_conv_b[j]), ssd_dt_bias[j], ssd_a_log[j],
                                         ssd_d[j], row(ssd_norm[j]), hook_for(items_b))
            arrived(items_b, got)
            x1 = mm(y3, lw[("ssd_w_out", j)], res=xcur, name="mm_ssd_out")
            mix = (proj, y3, core)
        else:
            items = mixer_items(i + 1) + ffn_items(i + 1) if nxt else []
            qkv = mm(h, lw[("sb_w_qkv", j)], out_dtype=MXU_DTYPE, name="mm_sb_qkv")
            o = sb_fwd(qkv, hook_for(items))
            if items:
                o, got = o
                arrived(items, got)
            x1 = mm(o, lw[("sb_w_out", j)], res=xcur, name="mm_sb_out")
            mix = (qkv, o)
        h2, r2 = rms_fwd(x1, row(ffn_norm[i]))
        u0 = mm(h2, lw[("ffn_w_in", i)], name="mm_ffn_in")
        a = ffn_mid_fwd(u0, fw["ffn_conv_w"][i], row(ffn_conv_b[i]))
        x2 = mm(a, lw[("ffn_w_out", i)], res=x1, name="mm_ffn_out")
        saved.append((xcur, h, r, mix, x1, h2, r2, u0, a))
        xcur = x2
    loss_part, dx, dxm, d_final = loss_head(xcur, row(final_norm), loss_target[0])

    gl = {n: [None] * w[n].shape[0] for n in WEIGHTS if n != "final_norm"}
    units = {n: [None] * w[n].shape[0] for n in BIG}

    def unit_of(g4):
        return g4.reshape(N_CHIPS, 2, g4.shape[1] // 2, g4.shape[2])

    for i in reversed(range(DEPTH)):
        j = i // 2
        x0, h, r, mix, x1, h2, r2, u0, a = saved[i]
        units["ffn_w_out"][i] = unit_of(mm(a, dxm, "tn", name="mm_d_ffn_out").reshape(N_CHIPS, -1, D_MODEL))
        da = mm(dxm, lw[("ffn_w_out", i)], "nt", name="mm_da_ffn")
        dug, duu, gl["ffn_conv_w"][i], dcb = ffn_mid_bwd(u0, da, fw["ffn_conv_w"][i], row(ffn_conv_b[i]))
        gl["ffn_conv_b"][i] = dcb[0]
        du0 = jnp.concatenate([dug, duu], axis=1)
        units["ffn_w_in"][i] = unit_of(mm(h2, du0, "tn", tn=1408, tm=512, n_split=N_CHIPS, name="mm_d_ffn_in"))
        dh2 = mm(du0, lw[("ffn_w_in", i)], "nt", name="mm_dh_ffn")
        dx1, dx1m, dg = rms_bwd(x1, r2, row(ffn_norm[i]), dh2, dx)
        gl["ffn_norm"][i] = dg[0]
        if i % 2 == 0:
            proj, y3, core = mix
            units["ssd_w_out"][j] = unit_of(mm(y3, dx1m, "tn", name="mm_d_ssd_out").reshape(N_CHIPS, -1, D_MODEL))
            dy3 = mm(dx1m, lw[("ssd_w_out", j)], "nt", name="mm_dy3_ssd")
            dproj, gl["ssd_conv_w"][j], dcb, gl["ssd_dt_bias"][j], gl["ssd_a_log"][j], gl["ssd_d"][j], dnorm = ssd_core_bwd(
                proj, fw["ssd_conv_w"][j], row(ssd_conv_b[j]), row(ssd_norm[j]), core, dy3)
            gl["ssd_conv_b"][j] = dcb[0]
            gl["ssd_norm"][j] = dnorm[0]
            dw_in = mm(h, dproj, "tn", tn=896, name="mm_d_ssd_in")[:, :SSD_IN_DIM]
            units["ssd_w_in"][j] = unit_of(jnp.transpose(dw_in.reshape(D_MODEL, N_CHIPS, -1), (1, 0, 2)))
            dh = mm(dproj, lw[("ssd_w_in", j)], "nt", name="mm_dh_ssd")
        else:
            qkv, o = mix
            units["sb_w_out"][j] = unit_of(mm(o, dx1m, "tn", name="mm_d_sb_out").reshape(N_CHIPS, -1, D_MODEL))
            do = mm(dx1m, lw[("sb_w_out", j)], "nt", out_dtype=MXU_DTYPE, name="mm_do_sb")
            dqkv = sb_bwd(qkv, do)
            units["sb_w_qkv"][j] = unit_of(mm(h, dqkv, "tn", tn=768, n_split=N_CHIPS, name="mm_d_sb_qkv"))
            dh = mm(dqkv, lw[("sb_w_qkv", j)], "nt", name="mm_dh_sb")
        dx, dxm, dg = rms_bwd(x0, r, row(mix_norm[i]), dh, dx1)
        gl["mix_norm"][i] = dg[0]

    layout = [(k, l) for k, n in enumerate(BIG) for l in range(w[n].shape[0])]
    reduced = _reduce_big_grads([units[BIG[k]][l] for k, l in layout], layout)
    g, delta, new_m, new_v = {}, {}, {}, {}
    two_d = lambda t: t.reshape(-1, t.shape[-1])
    for n, red in zip(BIG, reduced):
        g[n] = red.reshape(w[n].shape)
        d2, m2, v2 = adamw(two_d(w[n]), two_d(g[n]), two_d(mom[n]), two_d(var[n]), name="adamw_" + n)
        delta[n], new_m[n], new_v[n] = d2.reshape(w[n].shape), m2.reshape(w[n].shape), v2.reshape(w[n].shape)

    small_g = {n: jnp.stack(gl[n]) for n in SMALL + CONVW if n != "final_norm"}
    small_g["final_norm"] = d_final[0]
    zeros_of = lambda n: jnp.zeros((small_g[n].size,), F32)

    def small_pack(d, extra):
        parts = [d[n].reshape(-1) for n in SMALL] + [extra]
        parts += [(d[n].reshape(-1) if d is small_g else zeros_of(n)) for n in CONVW]
        return _to_rows(jnp.concatenate(parts), 16)

    part = small_pack(small_g, loss_part[0, 0:1])
    parts = all_gather_8(jnp.stack([part, part]), "gather_small_grads")
    zero = jnp.zeros((1,), F32)
    gs, ds, ms, vs = adamw_small(small_pack(w, zero), parts, small_pack(mom, zero), small_pack(var, zero))
    gs_flat = gs.reshape(-1)
    off = 0
    for n in SMALL:
        size = w[n].size
        for dst, src in ((g, gs), (delta, ds), (new_m, ms), (new_v, vs)):
            dst[n] = src.reshape(-1)[off:off + size].reshape(w[n].shape)
        off += size
    loss = gs_flat[off]
    off += 1
    for n in CONVW:
        size = small_g[n].size
        b = w[n].shape[-1]
        g[n] = lax.dynamic_slice_in_dim(gs_flat[off:off + size].reshape(small_g[n].shape), chip * b, b, axis=2)
        d2, m2, v2 = adamw(two_d(w[n]), two_d(g[n]), two_d(mom[n]), two_d(var[n]), name="adamw_" + n)
        delta[n], new_m[n], new_v[n] = d2.reshape(w[n].shape), m2.reshape(w[n].shape), v2.reshape(w[n].shape)
        off += size

    return (loss, dx[None], *[g[n] for n in WEIGHTS], *[delta[n] for n in WEIGHTS],
            *[new_m[n] for n in WEIGHTS], *[new_v[n] for n in WEIGHTS])
```

```python
import functools

import jax
import jax.numpy as jnp
from jax import lax
from jax.experimental import pallas as pl
from jax.experimental.pallas import tpu as pltpu

F32 = jnp.float32
MXU_DTYPE = jnp.bfloat16
HIGHEST = lax.Precision.HIGHEST

D_MODEL = 1024
DEPTH = 4
NORM_EPS = 1e-6
SSD_D_INNER = 2048
SSD_HEAD_DIM = 64
SSD_HEADS = 32
SSD_GROUPS = 8
SSD_STATE = 128
SSD_CONV = 4
SSD_CHUNK = 128
SSD_CONV_DIM = 4096
SSD_IN_DIM = 6176
SSD_IN_PAD = 6272
SB_HEADS = 16
SB_HEAD_DIM = 64
FFN_D_FF = 2816
FFN_CONV = 3
ADAM_LR, ADAM_B1, ADAM_B2, ADAM_EPS, ADAM_WD, ADAM_STEP = 0.001, 0.9, 0.999, 1e-08, 0.01, 10

LANES = 128
SUBLANES = 8
VMEM_LIMIT = 56 * 1024 * 1024
MESH = pl.DeviceIdType.MESH


def _params(sem=None):
    return pltpu.CompilerParams(dimension_semantics=sem, vmem_limit_bytes=VMEM_LIMIT)


def _sds(shape, dtype):
    return jax.ShapeDtypeStruct(shape, dtype)


def _dot(a, b, dims=(((1,), (0,)), ((), ())), precision=None):
    return lax.dot_general(a, b, dims, precision=precision, preferred_element_type=F32)


_NN = (((1,), (0,)), ((), ()))
_NT = (((1,), (1,)), ((), ()))
_TN = (((0,), (0,)), ((), ()))


def _mx(a):
    return a.astype(MXU_DTYPE)


def _silu(x):
    return x * (1.0 / (1.0 + jnp.exp(-x)))


def _silu_and_grad(x):
    s = 1.0 / (1.0 + jnp.exp(-x))
    return x * s, s * (1.0 + x * (1.0 - s))


def _pick(n, cands):
    for c in cands:
        if n % c == 0:
            return c
    return n


def rms_fwd(x, g):
    s, d = x.shape
    ts = _pick(s, (512, 256, 128))

    def body(x_ref, g_ref, h_ref, r_ref):
        xv = x_ref[...]
        r = lax.rsqrt(jnp.mean(xv * xv, axis=-1, keepdims=True) + NORM_EPS)
        h_ref[...] = (xv * r * g_ref[...]).astype(h_ref.dtype)
        r_ref[...] = r

    return pl.pallas_call(
        body, name="rms_fwd", grid=(s // ts,),
        in_specs=[pl.BlockSpec((ts, d), lambda i: (i, 0)), pl.BlockSpec((1, d), lambda i: (0, 0))],
        out_specs=[pl.BlockSpec((ts, d), lambda i: (i, 0)), pl.BlockSpec((ts, 1), lambda i: (i, 0))],
        out_shape=[_sds((s, d), MXU_DTYPE), _sds((s, 1), F32)],
        compiler_params=_params(("parallel",)),
    )(x, g)


def rms_bwd(x, r, g, dh, dres):
    s, d = x.shape
    ts = _pick(s, (512, 256, 128))

    def body(x_ref, r_ref, g_ref, dh_ref, dres_ref, dx_ref, dxm_ref, dg_ref):
        xh = x_ref[...] * r_ref[...]
        dhv = dh_ref[...]
        dxh = dhv * g_ref[...]
        dx = dres_ref[...] + r_ref[...] * (dxh - xh * jnp.mean(dxh * xh, axis=-1, keepdims=True))
        dx_ref[...] = dx
        dxm_ref[...] = dx.astype(dxm_ref.dtype)
        part = jnp.sum(dhv * xh, axis=0, keepdims=True)

        @pl.when(pl.program_id(0) == 0)
        def _():
            dg_ref[...] = part

        @pl.when(pl.program_id(0) != 0)
        def _():
            dg_ref[...] += part

    row = pl.BlockSpec((ts, d), lambda i: (i, 0))
    return pl.pallas_call(
        body, name="rms_bwd", grid=(s // ts,),
        in_specs=[row, pl.BlockSpec((ts, 1), lambda i: (i, 0)), pl.BlockSpec((1, d), lambda i: (0, 0)), row, row],
        out_specs=[row, row, pl.BlockSpec((1, d), lambda i: (0, 0))],
        out_shape=[_sds((s, d), F32), _sds((s, d), MXU_DTYPE), _sds((1, d), F32)],
        compiler_params=_params(("arbitrary",)),
    )(x, r, g, dh, dres)


def mm(a, b, mode="nn", res=None, out_dtype=F32, tm=None, tn=None, n_split=1, name="mm", hook=None):
    if mode == "nn":
        (m, k), (_, n) = a.shape, b.shape
    elif mode == "nt":
        (m, k), (n, _) = a.shape, b.shape
    else:
        (k, m), (_, n) = a.shape, b.shape
    tm = tm or _pick(m, (1024, 512, 256, 128))
    tn = tn or _pick(n, (512, 896, 256, 128))
    dims = {"nn": _NN, "nt": _NT, "tn": _TN}[mode]

    def body(*refs):
        a_ref, b_ref = refs[0], refs[1]
        o_ref = refs[-1]
        acc = _dot(_mx(a_ref[...]), _mx(b_ref[...]), dims)
        if res is not None:
            acc = acc + refs[2][...]
        o_ref[...] = acc.astype(o_ref.dtype)

    a_spec = pl.BlockSpec((k, tm), lambda i, j: (0, i)) if mode == "tn" else pl.BlockSpec((tm, k), lambda i, j: (i, 0))
    b_spec = pl.BlockSpec((tn, k), lambda i, j: (j, 0)) if mode == "nt" else pl.BlockSpec((k, tn), lambda i, j: (0, j))
    o_spec = pl.BlockSpec((tm, tn), lambda i, j: (i, j))
    ins, specs = [a, b], [a_spec, b_spec]
    if res is not None:
        ins.append(res)
        specs.append(o_spec)
    out_shape = _sds((m, n), out_dtype)
    if n_split > 1:
        per = n // n_split // tn
        o_spec = pl.BlockSpec((None, tm, tn), lambda i, j: (j // per, i, j % per))
        out_shape = _sds((n_split, m, n // n_split), out_dtype)
    out, carried = hosted_call(body, hook, name, (m // tm, n // tn), specs, o_spec, out_shape, [],
                               ("parallel", "parallel"), ins)
    return out if hook is None else (out, carried)


CONV_ROWS = 256
CONV_COLS = 128


def _row_iota8(cols):
    return lax.broadcasted_iota(jnp.int32, (SUBLANES, cols), 0)


def _shift_down(cur, prev8, k):
    if k == 0:
        return cur
    rolled = pltpu.roll(cur, k, 0)
    head = jnp.where(_row_iota8(cur.shape[1]) < k, pltpu.roll(prev8, k, 0), rolled[0:SUBLANES])
    return jnp.concatenate([head, rolled[SUBLANES:]], axis=0)


def _shift_up(cur, next8, k):
    if k == 0:
        return cur
    n = cur.shape[0]
    rolled = pltpu.roll(cur, n - k, 0)
    tail = jnp.where(_row_iota8(cur.shape[1]) >= SUBLANES - k, pltpu.roll(next8, SUBLANES - k, 0), rolled[n - SUBLANES:])
    return jnp.concatenate([rolled[:n - SUBLANES], tail], axis=0)


def _load_prev8(ref, i, rows):
    start = pl.multiple_of(jnp.maximum(i * rows - SUBLANES, 0), SUBLANES)
    p = ref[pl.ds(start, SUBLANES), :]
    return jnp.where(i > 0, p, jnp.zeros_like(p))


def _conv_rows(ref, w_ref, b_ref, i, rows, width):
    cur = ref[pl.ds(pl.multiple_of(i * rows, rows), rows), :]
    prev8 = _load_prev8(ref, i, rows)
    shifted = [_shift_down(cur, prev8, k) for k in range(width)]
    acc = b_ref[...] + w_ref[width - 1:width, :] * shifted[0]
    for k in range(1, width):
        acc = acc + w_ref[width - 1 - k:width - k, :] * shifted[k]
    return acc, shifted


def _conv_bwd_rows(du, next8, w_ref, width):
    acc = w_ref[width - 1:width, :] * du
    for k in range(1, width):
        acc = acc + w_ref[width - 1 - k:width - k, :] * _shift_up(du, next8, k)
    return acc


def ffn_mid_fwd(u0, cw, cb):
    s, f2 = u0.shape
    f = f2 // 2
    nt = f // CONV_COLS
    rows = min(CONV_ROWS, s)

    def body(ug_ref, uu_ref, wg_ref, wu_ref, bg_ref, bu_ref, a_ref):
        def step(i, carry):
            g, _ = _conv_rows(ug_ref, wg_ref, bg_ref, i, rows, FFN_CONV)
            u, _ = _conv_rows(uu_ref, wu_ref, bu_ref, i, rows, FFN_CONV)
            a_ref[pl.ds(pl.multiple_of(i * rows, rows), rows), :] = (_silu(g) * u).astype(a_ref.dtype)
            return carry

        lax.fori_loop(0, s // rows, step, 0)

    col = lambda off: pl.BlockSpec((s, CONV_COLS), lambda j: (0, j + off))
    wsp = lambda r, off: pl.BlockSpec((r, CONV_COLS), lambda j: (0, j + off))
    return pl.pallas_call(
        body, name="ffn_mid_fwd", grid=(nt,),
        in_specs=[col(0), col(nt), wsp(FFN_CONV, 0), wsp(FFN_CONV, nt), wsp(1, 0), wsp(1, nt)],
        out_specs=pl.BlockSpec((s, CONV_COLS), lambda j: (0, j)),
        out_shape=_sds((s, f), MXU_DTYPE), compiler_params=_params(("parallel",)),
    )(u0, u0, cw, cw, cb, cb)


def ffn_mid_bwd(u0, da, cw, cb):
    s, f2 = u0.shape
    f = f2 // 2
    nt = f // CONV_COLS
    rows = min(CONV_ROWS, s)
    nsteps = s // rows
    w = FFN_CONV

    def body(ug_ref, uu_ref, da_ref, wg_ref, wu_ref, bg_ref, bu_ref,
             dug_ref, duu_ref, dwg_ref, dwu_ref, dbg_ref, dbu_ref):
        zero8 = jnp.zeros((SUBLANES, CONV_COLS), F32)
        zrow = jnp.zeros((1, CONV_COLS), F32)

        def step(it, carry):
            ng, nu, accs = carry
            i = nsteps - 1 - it
            r0 = pl.multiple_of(i * rows, rows)
            g, sg = _conv_rows(ug_ref, wg_ref, bg_ref, i, rows, w)
            u, su = _conv_rows(uu_ref, wu_ref, bu_ref, i, rows, w)
            dav = da_ref[pl.ds(r0, rows), :]
            sg_val, sg_grad = _silu_and_grad(g)
            dg = dav * u * sg_grad
            du = dav * sg_val
            dug_ref[pl.ds(r0, rows), :] = _conv_bwd_rows(dg, ng, wg_ref, w).astype(dug_ref.dtype)
            duu_ref[pl.ds(r0, rows), :] = _conv_bwd_rows(du, nu, wu_ref, w).astype(duu_ref.dtype)
            new = []
            for j in range(w):
                new.append(accs[j] + jnp.sum(dg * sg[w - 1 - j], axis=0, keepdims=True))
            for j in range(w):
                new.append(accs[w + j] + jnp.sum(du * su[w - 1 - j], axis=0, keepdims=True))
            new.append(accs[2 * w] + jnp.sum(dg, axis=0, keepdims=True))
            new.append(accs[2 * w + 1] + jnp.sum(du, axis=0, keepdims=True))
            return dg[0:SUBLANES], du[0:SUBLANES], tuple(new)

        _, _, accs = lax.fori_loop(0, nsteps, step, (zero8, zero8, tuple([zrow] * (2 * w + 2))))
        dwg_ref[...] = jnp.concatenate(accs[0:w], axis=0)
        dwu_ref[...] = jnp.concatenate(accs[w:2 * w], axis=0)
        dbg_ref[...] = accs[2 * w]
        dbu_ref[...] = accs[2 * w + 1]

    col = lambda off: pl.BlockSpec((s, CONV_COLS), lambda j: (0, j + off))
    wsp = lambda r, off: pl.BlockSpec((r, CONV_COLS), lambda j: (0, j + off))
    outs = pl.pallas_call(
        body, name="ffn_mid_bwd", grid=(nt,),
        in_specs=[col(0), col(nt), col(0), wsp(w, 0), wsp(w, nt), wsp(1, 0), wsp(1, nt)],
        out_specs=[col(0), col(0), wsp(w, 0), wsp(w, 0), wsp(1, 0), wsp(1, 0)],
        out_shape=[_sds((s, f), MXU_DTYPE), _sds((s, f), MXU_DTYPE), _sds((w, f), F32), _sds((w, f), F32),
                   _sds((1, f), F32), _sds((1, f), F32)],
        compiler_params=_params(("parallel",)),
    )(u0, u0, da, cw, cw, cb, cb)
    dug, duu, dwg, dwu, dbg, dbu = outs
    return dug, duu, jnp.concatenate([dwg, dwu], axis=1), jnp.concatenate([dbg, dbu], axis=1)


SB_BLOCK = 128
SB_DEAD = 110.0
SB_HEADS_PER_STEP = 4


def _split_hi_lo(x):
    hi = x.astype(MXU_DTYPE)
    lo = (x - hi.astype(F32)).astype(MXU_DTYPE)
    return hi, lo


def _dot_exact01(x, tri):
    hi, lo = _split_hi_lo(x)
    return _dot(hi, tri) + _dot(lo, tri)


def _stack_heads(pair, lane_lo):
    zero = jnp.zeros_like(pair)
    return jnp.concatenate([jnp.where(lane_lo, pair, zero), jnp.where(lane_lo, zero, pair)], axis=0)


def _unstack_heads(tall, lane_lo):
    n = tall.shape[0] // 2
    return jnp.where(lane_lo, tall[:n], tall[n:])


def _sb_logits(stacked_q, k_ref, k0, pair_cols, blk):
    z = [_dot(sq, k_ref[pl.ds(k0, blk), cols], _NT) for sq, cols in zip(stacked_q, pair_cols)]
    return jnp.concatenate(z, axis=0) * (SB_HEAD_DIM ** -0.5)


def _sb_logs(z, qi, kb, blk):
    rows = qi * blk + (lax.broadcasted_iota(jnp.int32, z.shape, 0) & (blk - 1))
    cols = kb * blk + lax.broadcasted_iota(jnp.int32, z.shape, 1)
    strict = cols < rows
    t = jnp.log(1.0 + jnp.exp(-jnp.abs(z)))
    lb = jnp.minimum(z, 0.0) - t
    lf = jnp.where(strict, jnp.minimum(-z, 0.0) - t, 0.0)
    return lb, lf, strict


def _tri(blk, upper):
    r = lax.broadcasted_iota(jnp.int32, (blk, blk), 0)
    c = lax.broadcasted_iota(jnp.int32, (blk, blk), 1)
    return jnp.where((r > c) if upper else (r < c), 1.0, 0.0).astype(MXU_DTYPE)


def _tri_sum(x, tri2):
    hi, lo = _split_hi_lo(x)
    return _dot(jnp.concatenate([hi, lo], axis=1), tri2)


def sb_fwd(qkv, hook=None):
    s = qkv.shape[0]
    blk = min(SB_BLOCK, s)
    nblk = s // blk
    nh = SB_HEADS_PER_STEP
    nstep = SB_HEADS // nh
    dh = SB_HEAD_DIM

    def body(q_ref, k_ref, v_ref, o_ref):
        suffix_tri2 = jnp.concatenate([_tri(blk, True)] * 2, axis=0)
        lane_lo = lax.broadcasted_iota(jnp.int32, (1, LANES), 1) < dh
        pairs = [slice(p * LANES, (p + 1) * LANES) for p in range(nh // 2)]

        def qstep(qi, carry):
            q0 = pl.multiple_of(qi * blk, blk)
            qst = [_stack_heads(q_ref[pl.ds(q0, blk), cols], lane_lo) for cols in pairs]

            def kstep(st):
                it, run, accs, _ = st
                kb = qi - it
                k0 = pl.multiple_of(kb * blk, blk)
                lb, lf, strict = _sb_logs(_sb_logits(qst, k_ref, k0, pairs, blk), qi, kb, blk)
                sloc = _tri_sum(lf, suffix_tri2)
                a = _mx(jnp.where(strict, jnp.exp(lb + sloc + run), 0.0))
                accs = tuple(
                    acc + _unstack_heads(_dot(a[2 * blk * p:2 * blk * (p + 1)], v_ref[pl.ds(k0, blk), cols]), lane_lo)
                    for p, (acc, cols) in enumerate(zip(accs, pairs)))
                run = run + sloc[:, 0:1] + lf[:, 0:1]
                return it + 1, run, accs, jnp.max(run) > -SB_DEAD

            _, _, accs, _ = lax.while_loop(
                lambda st: jnp.logical_and(st[0] <= qi, st[3]), kstep,
                (jnp.int32(0), jnp.zeros((nh * blk, 1), F32), tuple([jnp.zeros((blk, LANES), F32)] * len(pairs)),
                 jnp.bool_(True)))
            for acc, cols in zip(accs, pairs):
                o_ref[pl.ds(q0, blk), cols] = acc.astype(o_ref.dtype)
            return carry

        lax.fori_loop(0, nblk, qstep, 0)

    col = lambda off: pl.BlockSpec((s, nh * dh), lambda p: (0, p + off))
    out, carried = hosted_call(body, hook, "sb_fwd", (nstep,), [col(0), col(nstep), col(2 * nstep)], col(0),
                               _sds((s, D_MODEL), MXU_DTYPE), [], ("parallel",), (qkv, qkv, qkv))
    return out if hook is None else (out, carried)


def sb_bwd(qkv, do, hook=None):
    s = qkv.shape[0]
    blk = min(SB_BLOCK, s)
    nblk = s // blk
    nh = SB_HEADS_PER_STEP
    nstep = SB_HEADS // nh
    dh = SB_HEAD_DIM

    def body(q_ref, k_ref, v_ref, do_ref, dq_ref, dk_ref, dv_ref, dk_acc, dv_acc, run_ref):
        suffix_tri2 = jnp.concatenate([_tri(blk, True)] * 2, axis=0)
        prefix_tri2 = jnp.concatenate([_tri(blk, False)] * 2, axis=0)
        dk_acc[...] = jnp.zeros_like(dk_acc)
        dv_acc[...] = jnp.zeros_like(dv_acc)
        lane_lo = lax.broadcasted_iota(jnp.int32, (1, LANES), 1) < dh
        pairs = [slice(p * LANES, (p + 1) * LANES) for p in range(nh // 2)]

        def qstep(qi, carry):
            q0 = pl.multiple_of(qi * blk, blk)
            qst = [_stack_heads(q_ref[pl.ds(q0, blk), cols], lane_lo) for cols in pairs]
            dost = [_stack_heads(do_ref[pl.ds(q0, blk), cols], lane_lo) for cols in pairs]

            def sweep1(st):
                it, run, _ = st
                kb = qi - it
                run_ref[kb] = run
                _, lf, _ = _sb_logs(_sb_logits(qst, k_ref, pl.multiple_of(kb * blk, blk), pairs, blk), qi, kb, blk)
                run = run + jnp.sum(lf, axis=1, keepdims=True)
                return it + 1, run, jnp.max(run) > -SB_DEAD

            nlive, _, _ = lax.while_loop(
                lambda st: jnp.logical_and(st[0] <= qi, st[2]), sweep1,
                (jnp.int32(0), jnp.zeros((nh * blk, 1), F32), jnp.bool_(True)))

            def sweep2(kb, st):
                pg, dqs = st
                k0 = pl.multiple_of(kb * blk, blk)
                lb, lf, strict = _sb_logs(_sb_logits(qst, k_ref, k0, pairs, blk), qi, kb, blk)
                sloc = _tri_sum(lf, suffix_tri2)
                a = jnp.where(strict, jnp.exp(lb + sloc + run_ref[kb]), 0.0)
                da = jnp.concatenate([_dot(d, v_ref[pl.ds(k0, blk), cols], _NT) for d, cols in zip(dost, pairs)], axis=0)
                g = da * a
                p = pg + _tri_sum(g, prefix_tri2)
                sig = jnp.exp(lb)
                dz = _mx(jnp.where(strict, g * (1.0 - sig) - p * sig, 0.0) * (dh ** -0.5))
                am = _mx(a)
                new_dqs = []
                for i, cols in enumerate(pairs):
                    rows = slice(2 * blk * i, 2 * blk * (i + 1))
                    new_dqs.append(dqs[i] + _unstack_heads(_dot(dz[rows], k_ref[pl.ds(k0, blk), cols]), lane_lo))
                    dk_acc[pl.ds(k0, blk), cols] += _dot(dz[rows], qst[i], _TN)
                    dv_acc[pl.ds(k0, blk), cols] += _dot(am[rows], dost[i], _TN)
                return pg + jnp.sum(g, axis=1, keepdims=True), tuple(new_dqs)

            _, dqs = lax.fori_loop(qi + 1 - nlive, qi + 1, sweep2,
                                   (jnp.zeros((nh * blk, 1), F32), tuple([jnp.zeros((blk, LANES), F32)] * len(pairs))))
            for dq, cols in zip(dqs, pairs):
                dq_ref[pl.ds(q0, blk), cols] = dq.astype(dq_ref.dtype)
            return carry

        lax.fori_loop(0, nblk, qstep, 0)
        dk_ref[...] = dk_acc[...].astype(dk_ref.dtype)
        dv_ref[...] = dv_acc[...].astype(dv_ref.dtype)

    col = lambda off: pl.BlockSpec((s, nh * dh), lambda p: (0, p + off))
    (dq, dk, dv), carried = hosted_call(
        body, hook, "sb_bwd", (nstep,), [col(0), col(nstep), col(2 * nstep), col(0)], [col(0), col(0), col(0)],
        [_sds((s, D_MODEL), MXU_DTYPE)] * 3,
        [pltpu.VMEM((s, nh * dh), F32), pltpu.VMEM((s, nh * dh), F32), pltpu.VMEM((nblk, nh * blk, 1), F32)],
        ("parallel",), (qkv, qkv, qkv, do))
    return jnp.concatenate([dq, dk, dv], axis=1), carried


SSD_XBC_TILE0 = SSD_D_INNER // CONV_COLS


def ssd_conv_fwd(proj, cw, cb):
    s = proj.shape[0]
    rows = min(CONV_ROWS, s)

    def body(u_ref, w_ref, b_ref, o_ref):
        def step(i, carry):
            u, _ = _conv_rows(u_ref, w_ref, b_ref, i, rows, SSD_CONV)
            o_ref[pl.ds(pl.multiple_of(i * rows, rows), rows), :] = _silu(u)
            return carry

        lax.fori_loop(0, s // rows, step, 0)

    return pl.pallas_call(
        body, name="ssd_conv_fwd", grid=(SSD_CONV_DIM // CONV_COLS,),
        in_specs=[pl.BlockSpec((s, CONV_COLS), lambda j: (0, j + SSD_XBC_TILE0)),
                  pl.BlockSpec((SSD_CONV, CONV_COLS), lambda j: (0, j)), pl.BlockSpec((1, CONV_COLS), lambda j: (0, j))],
        out_specs=pl.BlockSpec((s, CONV_COLS), lambda j: (0, j)),
        out_shape=_sds((s, SSD_CONV_DIM), F32), compiler_params=_params(("parallel",)),
    )(proj, cw, cb)


def ssd_conv_bwd(proj, dact, cw, cb):
    s = proj.shape[0]
    rows = min(CONV_ROWS, s)
    nsteps = s // rows
    w = SSD_CONV

    def body(u_ref, da_ref, w_ref, b_ref, du_ref, dw_ref, db_ref):
        def step(it, carry):
            nxt, accs = carry
            i = nsteps - 1 - it
            r0 = pl.multiple_of(i * rows, rows)
            u, sh = _conv_rows(u_ref, w_ref, b_ref, i, rows, w)
            dconv = da_ref[pl.ds(r0, rows), :] * _silu_and_grad(u)[1]
            du_ref[pl.ds(r0, rows), :] = _conv_bwd_rows(dconv, nxt, w_ref, w).astype(du_ref.dtype)
            new = [accs[j] + jnp.sum(dconv * sh[w - 1 - j], axis=0, keepdims=True) for j in range(w)]
            new.append(accs[w] + jnp.sum(dconv, axis=0, keepdims=True))
            return dconv[0:SUBLANES], tuple(new)

        zrow = jnp.zeros((1, CONV_COLS), F32)
        _, accs = lax.fori_loop(0, nsteps, step, (jnp.zeros((SUBLANES, CONV_COLS), F32), tuple([zrow] * (w + 1))))
        dw_ref[...] = jnp.concatenate(accs[0:w], axis=0)
        db_ref[...] = accs[w]

    col = pl.BlockSpec((s, CONV_COLS), lambda j: (0, j))
    return pl.pallas_call(
        body, name="ssd_conv_bwd", grid=(SSD_CONV_DIM // CONV_COLS,),
        in_specs=[pl.BlockSpec((s, CONV_COLS), lambda j: (0, j + SSD_XBC_TILE0)), col,
                  pl.BlockSpec((w, CONV_COLS), lambda j: (0, j)), pl.BlockSpec((1, CONV_COLS), lambda j: (0, j))],
        out_specs=[col, pl.BlockSpec((w, CONV_COLS), lambda j: (0, j)), pl.BlockSpec((1, CONV_COLS), lambda j: (0, j))],
        out_shape=[_sds((s, SSD_CONV_DIM), MXU_DTYPE), _sds((w, SSD_CONV_DIM), F32), _sds((1, SSD_CONV_DIM), F32)],
        compiler_params=_params(("parallel",)),
    )(proj, dact, cw, cb)


def _split3(x):
    hi = x.astype(MXU_DTYPE)
    r1 = x - hi.astype(F32)
    mid = r1.astype(MXU_DTYPE)
    lo = (r1 - mid.astype(F32)).astype(MXU_DTYPE)
    return hi, mid, lo


def _dot01(x, m, dims=_NN, left=False):
    parts = _split3(x)
    if left:
        return _dot(m, parts[0], dims) + _dot(m, parts[1], dims) + _dot(m, parts[2], dims)
    return _dot(parts[0], m, dims) + _dot(parts[1], m, dims) + _dot(parts[2], m, dims)


def _softplus(x):
    return jnp.maximum(x, 0.0) + jnp.log1p(jnp.exp(-jnp.abs(x)))


def _ssd_consts(dt_bias, a_log, d_skip):
    pad = lambda v: jnp.pad(v.reshape(1, SSD_HEADS), ((0, 0), (0, LANES - SSD_HEADS)))
    head_of = jnp.arange(SSD_D_INNER) // SSD_HEAD_DIM
    expand = (jnp.arange(LANES)[:, None] == head_of[None, :]).astype(MXU_DTYPE)
    return dict(bias_w=pad(dt_bias), alog_w=pad(a_log), bias_c=dt_bias.reshape(SSD_HEADS, 1),
                alog_c=a_log.reshape(SSD_HEADS, 1), dskip=jnp.repeat(d_skip, SSD_HEAD_DIM).reshape(1, SSD_D_INNER),
                expand=expand, reduce=expand.T)


def _ssd_chunk_prep(dtp, dtp_t, bias_w, alog_w, bias_c, alog_c, expand):
    L = dtp.shape[0]
    r = lax.broadcasted_iota(jnp.int32, (L, L), 0)
    c = lax.broadcasted_iota(jnp.int32, (L, L), 1)
    tril = r >= c
    lower = jnp.where(tril, 1.0, 0.0).astype(MXU_DTYPE)
    upper = jnp.where(r <= c, 1.0, 0.0).astype(MXU_DTYPE)
    dt_col = _softplus(dtp + bias_w)
    a_col = -jnp.exp(alog_w) * dt_col
    a_row = -jnp.exp(alog_c) * _softplus(dtp_t + bias_c)
    acum_col = _dot01(a_col, lower, left=True)
    acum_row = _dot01(a_row, upper)
    acum_full = _dot01(acum_col, expand)
    dt_full = _dot01(dt_col, expand)
    return dict(tril=tril, lower=lower, upper=upper, dt_col=dt_col, a_col=a_col, acum_col=acum_col,
                acum_row=acum_row, acum_full=acum_full, dt_full=dt_full)


def _head_mask(j):
    lane = lax.broadcasted_iota(jnp.int32, (1, LANES), 1)
    return jnp.where((lane // SSD_HEAD_DIM) == j, 1.0, 0.0)


def _decay(pre, h):
    seg = pre["acum_col"][:, h:h + 1] - pre["acum_row"][h:h + 1, :]
    return jnp.exp(jnp.where(pre["tril"], seg, -1e30))


def _ssd_specs(s, nc, rev):
    L = SSD_CHUNK
    ci = (lambda i: nc - 1 - i) if rev else (lambda i: i)
    const = lambda shape: pl.BlockSpec(shape, lambda i: (0,) * len(shape))
    return dict(
        xbc=pl.BlockSpec((L, SSD_CONV_DIM), lambda i: (ci(i), 0)),
        dtp=pl.BlockSpec((L, LANES), lambda i: (ci(i), SSD_IN_PAD // LANES - 1)),
        dtp_t=pl.BlockSpec((SSD_HEADS, L), lambda i: (0, ci(i))),
        rows=pl.BlockSpec((L, SSD_D_INNER), lambda i: (ci(i), 0)),
        state=pl.BlockSpec((1, SSD_GROUPS, SSD_STATE, 4 * SSD_HEAD_DIM), lambda i: (ci(i), 0, 0, 0)),
        consts=[const((1, LANES)), const((1, LANES)), const((SSD_HEADS, 1)), const((SSD_HEADS, 1)),
                const((1, SSD_D_INNER)), const((LANES, SSD_D_INNER)), const((SSD_D_INNER, LANES))],
    )


def _const_args(cs):
    return [cs["bias_w"], cs["alog_w"], cs["bias_c"], cs["alog_c"], cs["dskip"], cs["expand"], cs["reduce"]]


def ssd_scan_fwd(act, proj, dtp_t, cs, hook=None):
    s = act.shape[0]
    L = SSD_CHUNK
    nc = s // L
    G, N, GW = SSD_GROUPS, SSD_STATE, 4 * SSD_HEAD_DIM

    def body(act_ref, dtp_ref, dtpt_ref, bw_ref, aw_ref, bc_ref, ac_ref, dsk_ref, ex_ref, rd_ref, y_ref, st_out, st):
        @pl.when(pl.program_id(0) == 0)
        def _():
            st[...] = jnp.zeros_like(st)

        st_out[0] = st[...]
        pre = _ssd_chunk_prep(dtp_ref[...], dtpt_ref[...], bw_ref[...], aw_ref[...], bc_ref[...], ac_ref[...], ex_ref[...])
        acum_full = pre["acum_full"]
        last_full = acum_full[L - 1:L, :]
        for g in range(G):
            bg = _mx(act_ref[:, SSD_D_INNER + g * N:SSD_D_INNER + (g + 1) * N])
            cg = _mx(act_ref[:, SSD_D_INNER + G * N + g * N:SSD_D_INNER + G * N + (g + 1) * N])
            cb = _dot(cg, bg, _NT)
            for half in range(2):
                p = 2 * g + half
                cols = slice(p * LANES, (p + 1) * LANES)
                xs = act_ref[:, cols]
                xdt = xs * pre["dt_full"][:, cols]
                yd = jnp.zeros((L, LANES), F32)
                for j in range(2):
                    m = cb * _decay(pre, 2 * p + j)
                    yd = yd + _dot(_mx(m), _mx(xdt * _head_mask(j)))
                yoff = _dot(cg, _mx(st[g, :, half * LANES:(half + 1) * LANES])) * jnp.exp(acum_full[:, cols])
                y_ref[:, cols] = yd + yoff + dsk_ref[:, cols] * xs
                w = jnp.exp(last_full[:, cols] - acum_full[:, cols])
                st[g, :, half * LANES:(half + 1) * LANES] = (
                    st[g, :, half * LANES:(half + 1) * LANES] * jnp.exp(last_full[:, cols]) + _dot(bg, _mx(xdt * w), _TN))

    sp = _ssd_specs(s, nc, False)
    (y, states), carried = hosted_call(
        body, hook, "ssd_scan_fwd", (nc,), [sp["xbc"], sp["dtp"], sp["dtp_t"]] + sp["consts"],
        [sp["rows"], sp["state"]], [_sds((s, SSD_D_INNER), F32), _sds((nc, G, N, GW), F32)],
        [pltpu.VMEM((G, N, GW), F32)], ("arbitrary",), (act, proj, dtp_t, *_const_args(cs)))
    return y, states, carried


def ssd_scan_bwd(act, proj, dtp_t, cs, states, dy, hook=None):
    s = act.shape[0]
    L = SSD_CHUNK
    nc = s // L
    G, N, GW = SSD_GROUPS, SSD_STATE, 4 * SSD_HEAD_DIM

    def body(act_ref, dtp_ref, dtpt_ref, bw_ref, aw_ref, bc_ref, ac_ref, dsk_ref, ex_ref, rd_ref, st_ref, dy_ref,
             dact_ref, ddtp_ref, dalog_ref, dbias_ref, dskip_ref, dst, dxdt_ref, dac_ref):
        first = pl.program_id(0) == 0

        @pl.when(first)
        def _():
            dst[...] = jnp.zeros_like(dst)
            dalog_ref[...] = jnp.zeros_like(dalog_ref)
            dbias_ref[...] = jnp.zeros_like(dbias_ref)
            dskip_ref[...] = jnp.zeros_like(dskip_ref)

        expand, reduce = ex_ref[...], rd_ref[...]
        pre = _ssd_chunk_prep(dtp_ref[...], dtpt_ref[...], bw_ref[...], aw_ref[...], bc_ref[...], ac_ref[...], expand)
        acum_full = pre["acum_full"]
        last_full = acum_full[L - 1:L, :]
        ones = jnp.ones((L, LANES), MXU_DTYPE)
        lane = lax.broadcasted_iota(jnp.int32, (L, LANES), 1)
        dacum_diag = jnp.zeros((L, LANES), F32)
        dlast_parts = []
        for g in range(G):
            bg = _mx(act_ref[:, SSD_D_INNER + g * N:SSD_D_INNER + (g + 1) * N])
            cg = _mx(act_ref[:, SSD_D_INNER + G * N + g * N:SSD_D_INNER + G * N + (g + 1) * N])
            cb = _dot(cg, bg, _NT)
            dcb = jnp.zeros((L, L), F32)
            dcg = jnp.zeros((L, N), F32)
            dbg = jnp.zeros((L, N), F32)
            for half in range(2):
                p = 2 * g + half
                cols = slice(p * LANES, (p + 1) * LANES)
                hcols = slice(half * LANES, (half + 1) * LANES)
                xs = act_ref[:, cols]
                xdt = xs * pre["dt_full"][:, cols]
                dyv = dy_ref[:, cols]
                dxdt = jnp.zeros((L, LANES), F32)
                for j in range(2):
                    h = 2 * p + j
                    dec = _decay(pre, h)
                    m = cb * dec
                    dyh = _mx(dyv * _head_mask(j))
                    dm = _dot(dyh, _mx(xdt), _NT)
                    dxdt = dxdt + _dot(_mx(m), dyh, _TN)
                    e = dm * m
                    ehi, elo = _split_hi_lo(e)
                    d_h = (_dot(ehi, ones) + _dot(elo, ones)) - (_dot(ehi, ones, _TN) + _dot(elo, ones, _TN))
                    dacum_diag = jnp.where(lane == h, d_h, dacum_diag)
                    dcb = dcb + dm * dec
                lam = jnp.exp(acum_full[:, cols])
                stv = _mx(st_ref[0, g, :, hcols])
                z = _dot(cg, stv)
                dz = _mx(lam * dyv)
                dcg = dcg + _dot(dz, stv, _NT)
                dst_in = _dot(cg, dz, _TN)
                dsv = dst[g, :, hcols]
                w = jnp.exp(last_full[:, cols] - acum_full[:, cols])
                q = _dot(bg, _mx(dsv))
                wq = w * q
                dxdt = dxdt + wq
                wqx = wq * xdt
                dbg = dbg + _dot(_mx(xdt * w), _mx(dsv), _NT)
                elast = jnp.exp(last_full[:, cols])
                dlast_p = jnp.sum(wqx, axis=0, keepdims=True) + elast * jnp.sum(dsv * st_ref[0, g, :, hcols], axis=0, keepdims=True)
                dac_ref[:, cols] = dyv * z * lam - wqx
                dlast_parts.append(dlast_p)
                dst[g, :, hcols] = dst_in + dsv * elast
                dxdt_ref[:, cols] = dxdt
                dact_ref[:, cols] = dxdt * pre["dt_full"][:, cols] + dsk_ref[:, cols] * dyv
            dcbm = _mx(dcb)
            dact_ref[:, SSD_D_INNER + g * N:SSD_D_INNER + (g + 1) * N] = dbg + _dot(dcbm, cg, _TN)
            dact_ref[:, SSD_D_INNER + G * N + g * N:SSD_D_INNER + G * N + (g + 1) * N] = dcg + _dot(dcbm, bg)

        xs_all = act_ref[:, 0:SSD_D_INNER]
        dacum = dacum_diag + _dot_exact01(dac_ref[...], reduce)
        dlast = _dot_exact01(jnp.concatenate(dlast_parts, axis=1), reduce)
        row = lax.broadcasted_iota(jnp.int32, (L, LANES), 0)
        dacum = dacum + jnp.where(row == L - 1, dlast, 0.0)
        da_col = _dot01(dacum, pre["upper"], left=True)
        a_w = -jnp.exp(aw_ref[...])
        ddt = a_w * da_col + _dot_exact01(dxdt_ref[...] * xs_all, reduce)
        xin = dtp_ref[...] + bw_ref[...]
        ddtp = ddt * (1.0 / (1.0 + jnp.exp(-xin)))
        valid = lane < SSD_HEADS
        ddtp = jnp.where(valid, ddtp, 0.0)
        ddtp_ref[...] = ddtp
        dbias_ref[...] += jnp.sum(ddtp, axis=0, keepdims=True)
        dalog_ref[...] += jnp.sum(jnp.where(valid, da_col * pre["a_col"], 0.0), axis=0, keepdims=True)
        dskip_ref[...] += jnp.sum(_dot_exact01(dy_ref[...] * xs_all, reduce), axis=0, keepdims=True)

    sp = _ssd_specs(s, nc, True)
    acc = pl.BlockSpec((1, LANES), lambda i: (0, 0))
    outs, carried = hosted_call(
        body, hook, "ssd_scan_bwd", (nc,),
        [sp["xbc"], sp["dtp"], sp["dtp_t"]] + sp["consts"] + [sp["state"], sp["rows"]],
        [sp["xbc"], pl.BlockSpec((L, LANES), lambda i: (nc - 1 - i, 0)), acc, acc, acc],
        [_sds((s, SSD_CONV_DIM), F32), _sds((s, LANES), F32)] + [_sds((1, LANES), F32)] * 3,
        [pltpu.VMEM((G, N, GW), F32), pltpu.VMEM((L, SSD_D_INNER), F32), pltpu.VMEM((L, SSD_D_INNER), F32)],
        ("arbitrary",), (act, proj, dtp_t, *_const_args(cs), states, dy))
    return (*outs, carried)


def ssd_post_fwd(y, proj, g):
    s, d = y.shape
    ts = _pick(s, (256, 128))

    def body(y_ref, z_ref, g_ref, o_ref):
        y2 = y_ref[...] * _silu(z_ref[...])
        r = lax.rsqrt(jnp.mean(y2 * y2, axis=-1, keepdims=True) + NORM_EPS)
        o_ref[...] = (y2 * r * g_ref[...]).astype(o_ref.dtype)

    row = pl.BlockSpec((ts, d), lambda i: (i, 0))
    return pl.pallas_call(
        body, name="ssd_post_fwd", grid=(s // ts,), in_specs=[row, row, pl.BlockSpec((1, d), lambda i: (0, 0))],
        out_specs=row, out_shape=_sds((s, d), MXU_DTYPE), compiler_params=_params(("parallel",)),
    )(y, proj, g)


def ssd_post_bwd(y, proj, g, dy3):
    s, d = y.shape
    ts = _pick(s, (256, 128))

    def body(y_ref, z_ref, g_ref, d3_ref, dy_ref, dz_ref, dg_ref):
        yv, zv = y_ref[...], z_ref[...]
        sz, sgrad = _silu_and_grad(zv)
        y2 = yv * sz
        r = lax.rsqrt(jnp.mean(y2 * y2, axis=-1, keepdims=True) + NORM_EPS)
        xh = y2 * r
        d3 = d3_ref[...]
        dxh = d3 * g_ref[...]
        dy2 = r * (dxh - xh * jnp.mean(dxh * xh, axis=-1, keepdims=True))
        dy_ref[...] = dy2 * sz
        dz_ref[...] = (dy2 * yv * sgrad).astype(dz_ref.dtype)
        part = jnp.sum(d3 * xh, axis=0, keepdims=True)

        @pl.when(pl.program_id(0) == 0)
        def _():
            dg_ref[...] = part

        @pl.when(pl.program_id(0) != 0)
        def _():
            dg_ref[...] += part

    row = pl.BlockSpec((ts, d), lambda i: (i, 0))
    vec = pl.BlockSpec((1, d), lambda i: (0, 0))
    return pl.pallas_call(
        body, name="ssd_post_bwd", grid=(s // ts,), in_specs=[row, row, vec, row], out_specs=[row, row, vec],
        out_shape=[_sds((s, d), F32), _sds((s, d), MXU_DTYPE), _sds((1, d), F32)],
        compiler_params=_params(("arbitrary",)),
    )(y, proj, g, dy3)


def ssd_core_fwd(proj, cw, cb, dt_bias, a_log, d_skip, norm_g, hook=None):
    cs = _ssd_consts(dt_bias, a_log, d_skip)
    act = ssd_conv_fwd(proj, cw, cb)
    dtp_t = proj[:, SSD_IN_DIM - SSD_HEADS:SSD_IN_DIM].T
    y, states, carried = ssd_scan_fwd(act, proj, dtp_t, cs, hook)
    y3 = ssd_post_fwd(y, proj, norm_g)
    return y3, (cs, act, dtp_t, y, states), carried


def ssd_core_bwd(proj, cw, cb, norm_g, saved, dy3, hook=None):
    cs, act, dtp_t, y, states = saved
    dy, dz, dnorm = ssd_post_bwd(y, proj, norm_g, dy3)
    dact, ddtp, dalog, dbias, dskip, carried = ssd_scan_bwd(act, proj, dtp_t, cs, states, dy, hook)
    dxbc, dcw, dcb = ssd_conv_bwd(proj, dact, cw, cb)
    dproj = jnp.concatenate([dz, dxbc, ddtp.astype(MXU_DTYPE)], axis=1)
    h = SSD_HEADS
    return dproj, dcw, dcb, dbias[0, :h], dalog[0, :h], dskip[0, :h], dnorm, carried


def ssd_core(proj, cw, cb, dt_bias, a_log, d_skip, norm_g, dy3):
    y3, saved, _ = ssd_core_fwd(proj, cw, cb, dt_bias, a_log, d_skip, norm_g)
    return y3, ssd_core_bwd(proj, cw, cb, norm_g, saved, dy3)


def loss_head(x, g, target):
    s, d = x.shape
    ts = _pick(s, (512, 256, 128))

    def body(x_ref, g_ref, t_ref, loss_ref, dx_ref, dxm_ref, dg_ref):
        xv = x_ref[...]
        r = lax.rsqrt(jnp.mean(xv * xv, axis=-1, keepdims=True) + NORM_EPS)
        xh = xv * r
        err = xh * g_ref[...] - t_ref[...]
        dy = err * (1.0 / d)
        dxh = dy * g_ref[...]
        dx = r * (dxh - xh * jnp.mean(dxh * xh, axis=-1, keepdims=True))
        dx_ref[...] = dx
        dxm_ref[...] = dx.astype(dxm_ref.dtype)
        part = jnp.sum(dy * xh, axis=0, keepdims=True)
        lpart = jnp.full((1, LANES), 0.5 * jnp.sum(jnp.mean(err * err, axis=-1, keepdims=True)), F32)

        @pl.when(pl.program_id(0) == 0)
        def _():
            dg_ref[...] = part
            loss_ref[...] = lpart

        @pl.when(pl.program_id(0) != 0)
        def _():
            dg_ref[...] += part
            loss_ref[...] += lpart

    row = pl.BlockSpec((ts, d), lambda i: (i, 0))
    vec = pl.BlockSpec((1, d), lambda i: (0, 0))
    return pl.pallas_call(
        body, name="loss_head", grid=(s // ts,), in_specs=[row, vec, row],
        out_specs=[pl.BlockSpec((1, LANES), lambda i: (0, 0)), row, row, vec],
        out_shape=[_sds((1, LANES), F32), _sds((s, d), F32), _sds((s, d), MXU_DTYPE), _sds((1, d), F32)],
        compiler_params=_params(("arbitrary",)),
    )(x, g, target)


def _adamw_math(w, g, m, v):
    m = ADAM_B1 * m + (1.0 - ADAM_B1) * g
    v = ADAM_B2 * v + (1.0 - ADAM_B2) * (g * g)
    m_hat = m / (1.0 - ADAM_B1 ** ADAM_STEP)
    v_hat = v / (1.0 - ADAM_B2 ** ADAM_STEP)
    return -ADAM_LR * (m_hat / (jnp.sqrt(v_hat) + ADAM_EPS) + ADAM_WD * w), m, v


def adamw(w, g, m, v, name="adamw"):
    r, c = w.shape
    tr = _pick(r, (256, 128, 64, 32, 16, 8))

    def body(w_ref, g_ref, m_ref, v_ref, d_ref, nm_ref, nv_ref):
        d_ref[...], nm_ref[...], nv_ref[...] = _adamw_math(w_ref[...], g_ref[...], m_ref[...], v_ref[...])

    blk = pl.BlockSpec((tr, c), lambda i: (i, 0))
    return pl.pallas_call(
        body, name=name, grid=(r // tr,), in_specs=[blk] * 4, out_specs=[blk] * 3,
        out_shape=[_sds((r, c), F32)] * 3, compiler_params=_params(("parallel",)),
    )(w, g, m, v)


def adamw_small(w, parts, m, v):
    n, r, c = parts.shape

    def body(w_ref, p_ref, m_ref, v_ref, g_ref, d_ref, nm_ref, nv_ref):
        g = p_ref[0]
        for k in range(1, n):
            g = g + p_ref[k]
        g_ref[...] = g
        d_ref[...], nm_ref[...], nv_ref[...] = _adamw_math(w_ref[...], g, m_ref[...], v_ref[...])

    return pl.pallas_call(
        body, name="adamw_small", out_shape=[_sds((r, c), F32)] * 4, compiler_params=_params(),
    )(w, parts, m, v)


def pair_sum(unit, recv, half):
    nchip, _, r, c = unit.shape
    tr = _pick(r, (512, 256, 176, 128, 64, 32, 16))

    def body(h_ref, a_ref, b_ref, o_ref, ob_ref):
        sm = a_ref[0, 0] + b_ref[0]
        o_ref[0] = sm
        ob_ref[0] = sm.astype(ob_ref.dtype)

    blk = pl.BlockSpec((1, tr, c), lambda s, i, h: (s, i, 0))
    return pl.pallas_call(
        body, name="pair_sum",
        grid_spec=pltpu.PrefetchScalarGridSpec(
            num_scalar_prefetch=1, grid=(nchip, r // tr),
            in_specs=[pl.BlockSpec((1, 1, tr, c), lambda s, i, h: (s, h[0], i, 0)), blk], out_specs=[blk, blk]),
        out_shape=[_sds((nchip, r, c), F32), _sds((nchip, r, c), jnp.bfloat16)],
        compiler_params=_params(("parallel", "parallel")),
    )(half, unit, recv)


def chip_sum(own, where, recv, layer, layers, prev=None):
    _, r, c = own.shape
    tr = _pick(r, (512, 256, 176, 128, 64, 32, 16))

    def body(s_ref, a_ref, b_ref, *rest):
        rest[-1][...] = a_ref[0] + b_ref[0].astype(F32) + b_ref[1].astype(F32) + b_ref[2].astype(F32)

    in_specs = [pl.BlockSpec((1, tr, c), lambda i, s: (s[0], i, 0)), pl.BlockSpec((3, tr, c), lambda i, s: (0, i, 0))]
    args = [where, own, recv]
    if prev is not None:
        in_specs.append(ANY)
        args.append(prev)
    return pl.pallas_call(
        body, name="chip_sum",
        grid_spec=pltpu.PrefetchScalarGridSpec(
            num_scalar_prefetch=1, grid=(r // tr,), in_specs=in_specs,
            out_specs=pl.BlockSpec((None, None, tr, c), lambda i, s: (layer, s[1], i, 0))),
        out_shape=_sds((layers, 2, r, c), F32), input_output_aliases={} if prev is None else {3: 0},
        compiler_params=_params(("parallel",)),
    )(*args)


def place_cast(w, layer, chip):
    _, a, b = w.shape
    ta = _pick(a, (512, 352, 256, 128))

    def body(c_ref, w_ref, o_ref):
        o_ref[...] = w_ref[...].astype(o_ref.dtype)

    return pl.pallas_call(
        body, name="place_cast",
        grid_spec=pltpu.PrefetchScalarGridSpec(
            num_scalar_prefetch=1, grid=(a // ta,),
            in_specs=[pl.BlockSpec((None, ta, b), lambda i, c: (layer, i, 0))],
            out_specs=pl.BlockSpec((None, ta, b), lambda i, c: (c[0], i, 0))),
        out_shape=_sds((N_CHIPS, a, b), MXU_DTYPE), compiler_params=_params(("parallel",)),
    )(chip, w)


ANY = pl.BlockSpec(memory_space=pl.ANY)
COMM = pltpu.CompilerParams(has_side_effects=True)


def _coords():
    return lax.axis_index("x"), lax.axis_index("y"), lax.axis_index("c")


def _other_chips(x, y):
    return [(1 - x, y), (x, 1 - y), (1 - x, 1 - y)]


def all_gather_8(halves, name):
    _, r, c = halves.shape

    def body(h_ref, out_ref, send_sems, recv_sems, local_sem):
        x, y, cc = _coords()
        _gather_one(h_ref.at[cc], lambda px, py, pc: out_ref.at[4 * px + 2 * py + pc],
                    lambda k: send_sems.at[k], lambda k: recv_sems.at[k], local_sem)

    return pl.pallas_call(
        body, name=name, in_specs=[ANY], out_specs=ANY, out_shape=_sds((8, r, c), halves.dtype),
        scratch_shapes=[pltpu.SemaphoreType.DMA((7,)), pltpu.SemaphoreType.DMA((7,)), pltpu.SemaphoreType.DMA],
        compiler_params=COMM,
    )(halves)


def _gather_plan(x_ref, slot, send_sem, recv_sem, local_sem):
    x, y, cc = _coords()
    me, sibling = (x, y, cc), (x, y, 1 - cc)
    chips = _other_chips(x, y)

    def copy(k, blk, to, src=None):
        return pltpu.make_async_remote_copy(
            src_ref=slot(*blk) if src is None else src, dst_ref=slot(*blk),
            send_sem=send_sem(k), recv_sem=recv_sem(k), device_id=to, device_id_type=MESH)

    mine = pltpu.make_async_copy(x_ref, slot(*me), local_sem)
    first = [copy(0, me, sibling, src=x_ref)] + [copy(1 + j, me, (*chip, cc), src=x_ref) for j, chip in enumerate(chips)]
    passed = [copy(4 + j, (*chip, cc), sibling) for j, chip in enumerate(chips)]
    over_ici = [copy(1 + j, (*chip, cc), me) for j, chip in enumerate(chips)]
    from_sibling = [copy(0, sibling, me)] + [copy(4 + j, (*chip, 1 - cc), me) for j, chip in enumerate(chips)]
    return mine, first, passed, over_ici, from_sibling


def _gather_run(plans):
    for mine, first, _, _, _ in plans:
        mine.start()
        for cp in first:
            cp.start()
    for j in range(3):
        for _, _, passed, over_ici, _ in plans:
            over_ici[j].wait_recv()
            passed[j].start()
    for mine, first, passed, _, from_sibling in plans:
        for cp in from_sibling:
            cp.wait_recv()
        for cp in first + passed:
            cp.wait_send()
        mine.wait()


def _gather_one(x_ref, slot, send_sem, recv_sem, local_sem):
    _gather_run([_gather_plan(x_ref, slot, send_sem, recv_sem, local_sem)])


def gather_hook(items):
    n = len(items)

    def plan(refs, send_sems, recv_sems):
        x, y, cc = _coords()

        def copy(i, k, px, py, pc, to):
            blk = refs[i].at[2 * px + py, pc]
            return pltpu.make_async_remote_copy(src_ref=blk, dst_ref=blk, send_sem=send_sems.at[i, k],
                                                recv_sem=recv_sems.at[i, k], device_id=to, device_id_type=MESH)

        chips = _other_chips(x, y)
        first = [copy(i, j, x, y, cc, (*chip, cc)) for i in range(n) for j, chip in enumerate(chips)]
        return copy, chips, first, (x, y, cc)

    def start(refs, new, sems):
        for cp in plan(refs, *sems)[2]:
            cp.start()

    def finish(refs, new, sems):
        copy, chips, first, (x, y, cc) = plan(refs, *sems)
        passed = []
        for j, chip in enumerate(chips):
            for i in range(n):
                copy(i, j, *chip, cc, (x, y, cc)).wait_recv()
                passed.append(copy(i, 3 + j, *chip, cc, (x, y, 1 - cc)))
                passed[-1].start()
        for j, chip in enumerate(chips):
            for i in range(n):
                copy(i, 3 + j, *chip, 1 - cc, (x, y, cc)).wait_recv()
        for cp in first + passed:
            cp.wait_send()

    return dict(arrays=list(items), new=[], start=start, finish=finish,
                sems=[pltpu.SemaphoreType.DMA((n, 6)), pltpu.SemaphoreType.DMA((n, 6))])


def hosted_call(body, hook, name, grid, in_specs, out_specs, out_shape, scratch_shapes, sem, args):
    single = not isinstance(out_shape, (list, tuple))
    out_specs_l = [out_specs] if single else list(out_specs)
    out_shape_l = [out_shape] if single else list(out_shape)
    if hook is None:
        res = pl.pallas_call(body, name=name, grid=grid, in_specs=list(in_specs), out_specs=out_specs, out_shape=out_shape,
                             scratch_shapes=list(scratch_shapes), compiler_params=_params(sem))(*args)
        return res, []
    items, new = hook["arrays"], hook["new"]
    k, kn, n_in, n_out, n_scr = len(items), len(new), len(in_specs), len(out_specs_l), len(scratch_shapes)

    def full(*refs):
        ins = refs[:n_in]
        base = n_in + k
        outs = refs[base:base + n_out]
        hrefs = refs[base + n_out:base + n_out + k]
        nrefs = refs[base + n_out + k:base + n_out + k + kn]
        scr = refs[base + n_out + k + kn:base + n_out + k + kn + n_scr]
        sems = refs[base + n_out + k + kn + n_scr:]
        ids = [pl.program_id(d) for d in range(len(grid))]
        first = functools.reduce(jnp.logical_and, [i == 0 for i in ids])
        last = functools.reduce(jnp.logical_and, [i == g - 1 for i, g in zip(ids, grid)])

        @pl.when(first)
        def _():
            hook["start"](hrefs, nrefs, sems)

        body(*ins, *outs, *scr)

        @pl.when(last)
        def _():
            hook["finish"](hrefs, nrefs, sems)

    res = pl.pallas_call(
        full, name=name, grid=grid, in_specs=list(in_specs) + [ANY] * k, out_specs=out_specs_l + [ANY] * (k + kn),
        out_shape=out_shape_l + [_sds(a.shape, a.dtype) for a in items] + list(new),
        input_output_aliases={n_in + i: n_out + i for i in range(k)},
        scratch_shapes=list(scratch_shapes) + hook["sems"],
        compiler_params=pltpu.CompilerParams(dimension_semantics=("arbitrary",) * len(grid),
                                             vmem_limit_bytes=VMEM_LIMIT, has_side_effects=True),
    )(*args, *items)
    return (res[0] if single else list(res[:n_out])), list(res[n_out:])


def comm_call(hook, name):
    k, kn = len(hook["arrays"]), len(hook["new"])

    def body(*refs):
        hook["start"](refs[k:2 * k], refs[2 * k:2 * k + kn], refs[2 * k + kn:])
        hook["finish"](refs[k:2 * k], refs[2 * k:2 * k + kn], refs[2 * k + kn:])

    return list(pl.pallas_call(
        body, name=name, in_specs=[ANY] * k, out_specs=[ANY] * (k + kn),
        out_shape=[_sds(a.shape, a.dtype) for a in hook["arrays"]] + list(hook["new"]),
        input_output_aliases={i: i for i in range(k)}, scratch_shapes=hook["sems"], compiler_params=COMM,
    )(*hook["arrays"]))


def merge_hooks(hooks):
    hooks = [h for h in hooks if h is not None]
    if len(hooks) < 2:
        return hooks[0] if hooks else None

    def parts(refs, new, sems):
        out, a, b, c = [], 0, 0, 0
        for h in hooks:
            na, nn, ns = len(h["arrays"]), len(h["new"]), len(h["sems"])
            out.append((refs[a:a + na], new[b:b + nn], sems[c:c + ns]))
            a, b, c = a + na, b + nn, c + ns
        return out

    def start(refs, new, sems):
        for h, p in zip(hooks, parts(refs, new, sems)):
            h["start"](*p)

    def finish(refs, new, sems):
        for h, p in zip(hooks, parts(refs, new, sems)):
            h["finish"](*p)

    return dict(arrays=[a for h in hooks for a in h["arrays"]], new=[a for h in hooks for a in h["new"]],
                sems=[a for h in hooks for a in h["sems"]], start=start, finish=finish)


def split_carried(hooks, carried):
    hooks = [h for h in hooks if h is not None]
    off = sum(len(h["arrays"]) for h in hooks)
    out = []
    for h in hooks:
        out.append(carried[off:off + len(h["new"])])
        off += len(h["new"])
    return out


def pair_swap_hook(units):
    n = len(units)

    def plan(refs, new, send_sems, recv_sems):
        x, y, cc = _coords()
        return [pltpu.make_async_remote_copy(src_ref=refs[i].at[:, 1 - cc], dst_ref=new[i], send_sem=send_sems.at[i],
                                             recv_sem=recv_sems.at[i], device_id=(x, y, 1 - cc), device_id_type=MESH)
                for i in range(n)]

    def start(refs, new, sems):
        for cp in plan(refs, new, *sems):
            cp.start()

    def finish(refs, new, sems):
        for cp in plan(refs, new, *sems):
            cp.wait()

    return dict(arrays=list(units), new=[_sds((u.shape[0],) + u.shape[2:], u.dtype) for u in units], start=start,
                finish=finish, sems=[pltpu.SemaphoreType.DMA((n,)), pltpu.SemaphoreType.DMA((n,))])


def chip_exchange_hook(units):
    n = len(units)

    def plan(refs, new, send_sems, recv_sems):
        x, y, cc = _coords()
        return [pltpu.make_async_remote_copy(
            src_ref=refs[i].at[2 * px + py], dst_ref=new[i].at[k], send_sem=send_sems.at[i, k],
            recv_sem=recv_sems.at[i, k], device_id=(px, py, cc), device_id_type=MESH)
            for i in range(n) for k, (px, py) in enumerate(_other_chips(x, y))]

    def start(refs, new, sems):
        for cp in plan(refs, new, *sems):
            cp.start()

    def finish(refs, new, sems):
        for cp in plan(refs, new, *sems):
            cp.wait()

    return dict(arrays=list(units), new=[_sds((3,) + u.shape[1:], u.dtype) for u in units], start=start,
                finish=finish, sems=[pltpu.SemaphoreType.DMA((n, 3)), pltpu.SemaphoreType.DMA((n, 3))])


def grad_half_swap(grads):
    n = len(grads)

    def body(*refs):
        outs, send_sems, recv_sems = refs[n:2 * n], refs[2 * n], refs[2 * n + 1]
        x, y, cc = _coords()
        cps = [pltpu.make_async_remote_copy(
            src_ref=outs[i].at[:, cc], dst_ref=outs[i].at[:, cc], send_sem=send_sems.at[i], recv_sem=recv_sems.at[i],
            device_id=(x, y, 1 - cc), device_id_type=MESH) for i in range(n)]
        for cp in cps:
            cp.start()
        for i, cp in enumerate(cps):
            cp.wait_send()
            pltpu.make_async_remote_copy(
                src_ref=outs[i].at[:, 1 - cc], dst_ref=outs[i].at[:, 1 - cc], send_sem=send_sems.at[i],
                recv_sem=recv_sems.at[i], device_id=(x, y, 1 - cc), device_id_type=MESH).wait_recv()

    return pl.pallas_call(
        body, name="grad_half_swap", in_specs=[ANY] * n, out_specs=[ANY] * n,
        out_shape=[_sds(g.shape, g.dtype) for g in grads], input_output_aliases={i: i for i in range(n)},
        scratch_shapes=[pltpu.SemaphoreType.DMA((n,)), pltpu.SemaphoreType.DMA((n,))], compiler_params=COMM,
    )(*grads)


N_CHIPS = 4
PACK_COLS = 1024
BIG = ("ssd_w_in", "ssd_w_out", "sb_w_qkv", "sb_w_out", "ffn_w_in", "ffn_w_out")
CONVW = ("ssd_conv_w", "ffn_conv_w")
COL_SHARDED = ("ssd_w_in", "sb_w_qkv", "ffn_w_in", "ssd_conv_w", "ffn_conv_w")
SMALL = ("mix_norm", "ffn_norm", "final_norm", "ssd_conv_b", "ssd_dt_bias", "ssd_a_log", "ssd_d", "ssd_norm", "ffn_conv_b")
WEIGHTS = ("mix_norm", "ffn_norm", "final_norm", "ssd_w_in", "ssd_conv_w", "ssd_conv_b", "ssd_dt_bias", "ssd_a_log",
           "ssd_d", "ssd_norm", "ssd_w_out", "sb_w_qkv", "sb_w_out", "ffn_w_in", "ffn_conv_w", "ffn_conv_b", "ffn_w_out")


def _to_rows(flat, multiple):
    rows = -(-flat.shape[-1] // PACK_COLS)
    rows = -(-rows // multiple) * multiple
    pad = rows * PACK_COLS - flat.shape[-1]
    return jnp.pad(flat, [(0, pad)]).reshape(rows, PACK_COLS)


def _unshard(name, stacked):
    l, n, a, b = stacked.shape
    if name in COL_SHARDED:
        return jnp.transpose(stacked, (0, 2, 1, 3)).reshape(l, a, n * b)
    return stacked.reshape(l, n * a, b)


def _gather_conv_weights(w):
    flat = jnp.concatenate([w[n].reshape(-1) for n in CONVW])
    rows = _to_rows(flat, 16)
    got = all_gather_8(rows.reshape(2, rows.shape[0] // 2, PACK_COLS), "gather_conv_weights").reshape(N_CHIPS, -1)
    out, off = {}, 0
    for n in CONVW:
        l, a, b = w[n].shape
        out[n] = _unshard(n, jnp.moveaxis(got[:, off:off + w[n].size].reshape(N_CHIPS, l, a, b), 0, 1))
        off += w[n].size
    return out


def _finish_big_grads(pair_sums, from_chips, layout):
    cc = lax.axis_index("c").astype(jnp.int32)
    chip = (2 * lax.axis_index("x") + lax.axis_index("y")).astype(jnp.int32)
    where = jnp.stack([chip, cc])
    nlayers = [1 + max(l for k, l in layout if k == wi) for wi in range(1 + max(k for k, _ in layout))]
    grads = [None] * len(nlayers)
    for (wi, l), p, r in zip(layout, pair_sums, from_chips):
        grads[wi] = chip_sum(p, where, r, l, nlayers[wi], grads[wi])
    return grad_half_swap(grads)


def kernel(x, mix_norm, ffn_norm, final_norm, ssd_w_in, ssd_conv_w, ssd_conv_b, ssd_dt_bias, ssd_a_log, ssd_d, ssd_norm, ssd_w_out, sb_w_qkv, sb_w_out, ffn_w_in, ffn_conv_w, ffn_conv_b, ffn_w_out, loss_target, m_mix_norm, m_ffn_norm, m_final_norm, m_ssd_w_in, m_ssd_conv_w, m_ssd_conv_b, m_ssd_dt_bias, m_ssd_a_log, m_ssd_d, m_ssd_norm, m_ssd_w_out, m_sb_w_qkv, m_sb_w_out, m_ffn_w_in, m_ffn_conv_w, m_ffn_conv_b, m_ffn_w_out, v_mix_norm, v_ffn_norm, v_final_norm, v_ssd_w_in, v_ssd_conv_w, v_ssd_conv_b, v_ssd_dt_bias, v_ssd_a_log, v_ssd_d, v_ssd_norm, v_ssd_w_out, v_sb_w_qkv, v_sb_w_out, v_ffn_w_in, v_ffn_conv_w, v_ffn_conv_b, v_ffn_w_out):
    given = dict(locals())
    w = {n: given[n] for n in WEIGHTS}
    mom = {n: given["m_" + n] for n in WEIGHTS}
    var = {n: given["v_" + n] for n in WEIGHTS}
    chip = 2 * lax.axis_index("x") + lax.axis_index("y")

    chip1 = chip.reshape(1).astype(jnp.int32)
    fw = _gather_conv_weights(w)
    row = lambda v: v.reshape(1, -1)

    def placed(n, l):
        _, a, b = w[n].shape
        return place_cast(w[n], l, chip1).reshape(N_CHIPS, 2, a // 2, b)

    def mixer_items(i):
        return [(n, i // 2) for n in (("ssd_w_in", "ssd_w_out") if i % 2 == 0 else ("sb_w_qkv", "sb_w_out"))]

    def ffn_items(i):
        return [("ffn_w_in", i), ("ffn_w_out", i)]

    def hook_for(items):
        return gather_hook([placed(n, l) for n, l in items]) if items else None

    lw = {}

    def arrived(items, arrays):
        for (n, l), arr in zip(items, arrays):
            g4 = arr.reshape(N_CHIPS, -1, arr.shape[-1])
            if n in COL_SHARDED:
                full = jnp.transpose(g4, (1, 0, 2)).reshape(g4.shape[1], -1)
            else:
                full = g4.reshape(-1, g4.shape[2])
            if n == "ssd_w_in":
                full = jnp.pad(full, ((0, 0), (0, SSD_IN_PAD - SSD_IN_DIM)))
            lw[(n, l)] = full

    first_items = mixer_items(0) + ffn_items(0)
    arrived(first_items, comm_call(hook_for(first_items), "gather_layer0"))

    xcur = x[0]
    saved = []
    for i in range(DEPTH):
        j = i // 2
        nxt = i + 1 < DEPTH
        h, r = rms_fwd(xcur, row(mix_norm[i]))
        if i % 2 == 0:
            items_a, items_b = (mixer_items(i + 1), ffn_items(i + 1)) if nxt else ([], [])
            proj = mm(h, lw[("ssd_w_in", j)], tn=896, name="mm_ssd_in", hook=hook_for(items_a))
            if items_a:
                proj, got = proj
                arrived(items_a, got)
            y3, core, got = ssd_core_fwd(proj, fw["ssd_conv_w"][j], row(ssd_conv_b[j]), ssd_dt_bias[j], ssd_a_log[j],
                                         ssd_d[j], row(ssd_norm[j]), hook_for(items_b))
            arrived(items_b, got)
            x1 = mm(y3, lw[("ssd_w_out", j)], res=xcur, name="mm_ssd_out")
            mix = (proj, y3, core)
        else:
            items = mixer_items(i + 1) + ffn_items(i + 1) if nxt else []
            qkv = mm(h, lw[("sb_w_qkv", j)], out_dtype=MXU_DTYPE, name="mm_sb_qkv")
            o = sb_fwd(qkv, hook_for(items))
            if items:
                o, got = o
                arrived(items, got)
            x1 = mm(o, lw[("sb_w_out", j)], res=xcur, name="mm_sb_out")
            mix = (qkv, o)
        h2, r2 = rms_fwd(x1, row(ffn_norm[i]))
        u0 = mm(h2, lw[("ffn_w_in", i)], name="mm_ffn_in")
        a = ffn_mid_fwd(u0, fw["ffn_conv_w"][i], row(ffn_conv_b[i]))
        x2 = mm(a, lw[("ffn_w_out", i)], res=x1, name="mm_ffn_out")
        saved.append((xcur, h, r, mix, x1, h2, r2, u0, a))
        xcur = x2
    loss_part, dx, dxm, d_final = loss_head(xcur, row(final_norm), loss_target[0])

    gl = {n: [None] * w[n].shape[0] for n in WEIGHTS if n != "final_norm"}
    units = {n: [None] * w[n].shape[0] for n in BIG}

    def unit_of(g4):
        return g4.reshape(N_CHIPS, 2, g4.shape[1] // 2, g4.shape[2])

    core1 = lax.axis_index("c").reshape(1).astype(jnp.int32)
    pair_f32, wire, from_chips = {}, {}, {}

    def pair_sums(keys, swapped):
        for (n, l), got in zip(keys, swapped):
            pair_f32[(n, l)], wire[(n, l)] = pair_sum(units[n][l], got, core1)

    for i in reversed(range(DEPTH)):
        j = i // 2
        x0, h, r, mix, x1, h2, r2, u0, a = saved[i]
        units["ffn_w_out"][i] = unit_of(mm(a, dxm, "tn", name="mm_d_ffn_out").reshape(N_CHIPS, -1, D_MODEL))
        da = mm(dxm, lw[("ffn_w_out", i)], "nt", name="mm_da_ffn")
        dug, duu, gl["ffn_conv_w"][i], dcb = ffn_mid_bwd(u0, da, fw["ffn_conv_w"][i], row(ffn_conv_b[i]))
        gl["ffn_conv_b"][i] = dcb[0]
        du0 = jnp.concatenate([dug, duu], axis=1)
        units["ffn_w_in"][i] = unit_of(mm(h2, du0, "tn", tn=1408, tm=512, n_split=N_CHIPS, name="mm_d_ffn_in"))
        keys_f = ffn_items(i)
        swap = pair_swap_hook([units[n][l] for n, l in keys_f])
        dh2, carried = mm(du0, lw[("ffn_w_in", i)], "nt", name="mm_dh_ffn", hook=swap)
        pair_sums(keys_f, split_carried([swap], carried)[0])
        dx1, dx1m, dg = rms_bwd(x1, r2, row(ffn_norm[i]), dh2, dx)
        gl["ffn_norm"][i] = dg[0]
        keys_up = mixer_items(i + 1) if i + 1 < DEPTH else []
        exchanges = [chip_exchange_hook([wire[k] for k in keys_f]),
                     chip_exchange_hook([wire[k] for k in keys_up]) if keys_up else None]
        if i % 2 == 0:
            proj, y3, core = mix
            units["ssd_w_out"][j] = unit_of(mm(y3, dx1m, "tn", name="mm_d_ssd_out").reshape(N_CHIPS, -1, D_MODEL))
            dy3 = mm(dx1m, lw[("ssd_w_out", j)], "nt", name="mm_dy3_ssd")
            (dproj, gl["ssd_conv_w"][j], dcb, gl["ssd_dt_bias"][j], gl["ssd_a_log"][j], gl["ssd_d"][j], dnorm,
             carried) = ssd_core_bwd(proj, fw["ssd_conv_w"][j], row(ssd_conv_b[j]), row(ssd_norm[j]), core, dy3,
                                     merge_hooks(exchanges))
            gl["ssd_conv_b"][j] = dcb[0]
            gl["ssd_norm"][j] = dnorm[0]
            dw_in = mm(h, dproj, "tn", tn=896, name="mm_d_ssd_in")[:, :SSD_IN_DIM]
            units["ssd_w_in"][j] = unit_of(jnp.transpose(dw_in.reshape(D_MODEL, N_CHIPS, -1), (1, 0, 2)))
            dmix, w_in, dh_name = dproj, lw[("ssd_w_in", j)], "mm_dh_ssd"
        else:
            qkv, o = mix
            units["sb_w_out"][j] = unit_of(mm(o, dx1m, "tn", name="mm_d_sb_out").reshape(N_CHIPS, -1, D_MODEL))
            do = mm(dx1m, lw[("sb_w_out", j)], "nt", out_dtype=MXU_DTYPE, name="mm_do_sb")
            dqkv, carried = sb_bwd(qkv, do, merge_hooks(exchanges))
            units["sb_w_qkv"][j] = unit_of(mm(h, dqkv, "tn", tn=768, n_split=N_CHIPS, name="mm_d_sb_qkv"))
            dmix, w_in, dh_name = dqkv, lw[("sb_w_qkv", j)], "mm_dh_sb"
        got = split_carried(exchanges, carried)
        from_chips.update(zip(keys_f, got[0]))
        if keys_up:
            from_chips.update(zip(keys_up, got[1]))
        keys_m = mixer_items(i)
        swap = pair_swap_hook([units[n][l] for n, l in keys_m])
        dh, carried = mm(dmix, w_in, "nt", name=dh_name, hook=swap)
        pair_sums(keys_m, split_carried([swap], carried)[0])
        dx, dxm, dg = rms_bwd(x0, r, row(mix_norm[i]), dh, dx1)
        gl["mix_norm"][i] = dg[0]
    last = chip_exchange_hook([wire[k] for k in mixer_items(0)])
    from_chips.update(zip(mixer_items(0), comm_call(last, "grad_exchange_last")[len(last["arrays"]):]))

    layout = [(k, l) for k, n in enumerate(BIG) for l in range(w[n].shape[0])]
    reduced = _finish_big_grads([pair_f32[(BIG[k], l)] for k, l in layout], [from_chips[(BIG[k], l)] for k, l in layout],
                                layout)
    g, delta, new_m, new_v = {}, {}, {}, {}
    two_d = lambda t: t.reshape(-1, t.shape[-1])
    for n, red in zip(BIG, reduced):
        g[n] = red.reshape(w[n].shape)
        d2, m2, v2 = adamw(two_d(w[n]), two_d(g[n]), two_d(mom[n]), two_d(var[n]), name="adamw_" + n)
        delta[n], new_m[n], new_v[n] = d2.reshape(w[n].shape), m2.reshape(w[n].shape), v2.reshape(w[n].shape)

    small_g = {n: jnp.stack(gl[n]) for n in SMALL + CONVW if n != "final_norm"}
    small_g["final_norm"] = d_final[0]
    zeros_of = lambda n: jnp.zeros((small_g[n].size,), F32)

    def small_pack(d, extra):
        parts = [d[n].reshape(-1) for n in SMALL] + [extra]
        parts += [(d[n].reshape(-1) if d is small_g else zeros_of(n)) for n in CONVW]
        return _to_rows(jnp.concatenate(parts), 16)

    part = small_pack(small_g, loss_part[0, 0:1])
    parts = all_gather_8(jnp.stack([part, part]), "gather_small_grads")
    zero = jnp.zeros((1,), F32)
    gs, ds, ms, vs = adamw_small(small_pack(w, zero), parts, small_pack(mom, zero), small_pack(var, zero))
    gs_flat = gs.reshape(-1)
    off = 0
    for n in SMALL:
        size = w[n].size
        for dst, src in ((g, gs), (delta, ds), (new_m, ms), (new_v, vs)):
            dst[n] = src.reshape(-1)[off:off + size].reshape(w[n].shape)
        off += size
    loss = gs_flat[off]
    off += 1
    for n in CONVW:
        size = small_g[n].size
        b = w[n].shape[-1]
        g[n] = lax.dynamic_slice_in_dim(gs_flat[off:off + size].reshape(small_g[n].shape), chip * b, b, axis=2)
        d2, m2, v2 = adamw(two_d(w[n]), two_d(g[n]), two_d(mom[n]), two_d(var[n]), name="adamw_" + n)
        delta[n], new_m[n], new_v[n] = d2.reshape(w[n].shape), m2.reshape(w[n].shape), v2.reshape(w[n].shape)
        off += size

    return (loss, dx[None], *[g[n] for n in WEIGHTS], *[delta[n] for n in WEIGHTS],
            *[new_m[n] for n in WEIGHTS], *[new_v[n] for n in WEIGHTS])
```

```python
import functools

import jax
import jax.numpy as jnp
from jax import lax
from jax.experimental import pallas as pl
from jax.experimental.pallas import tpu as pltpu

F32 = jnp.float32
MXU_DTYPE = jnp.bfloat16
HIGHEST = lax.Precision.HIGHEST

D_MODEL = 1024
DEPTH = 4
NORM_EPS = 1e-6
SSD_D_INNER = 2048
SSD_HEAD_DIM = 64
SSD_HEADS = 32
SSD_GROUPS = 8
SSD_STATE = 128
SSD_CONV = 4
SSD_CHUNK = 128
SSD_CONV_DIM = 4096
SSD_IN_DIM = 6176
SSD_IN_PAD = 6272
SB_HEADS = 16
SB_HEAD_DIM = 64
FFN_D_FF = 2816
FFN_CONV = 3
ADAM_LR, ADAM_B1, ADAM_B2, ADAM_EPS, ADAM_WD, ADAM_STEP = 0.001, 0.9, 0.999, 1e-08, 0.01, 10

LANES = 128
SUBLANES = 8
VMEM_LIMIT = 56 * 1024 * 1024
MESH = pl.DeviceIdType.MESH


def _params(sem=None):
    return pltpu.CompilerParams(dimension_semantics=sem, vmem_limit_bytes=VMEM_LIMIT)


def _sds(shape, dtype):
    return jax.ShapeDtypeStruct(shape, dtype)


def _dot(a, b, dims=(((1,), (0,)), ((), ())), precision=None):
    return lax.dot_general(a, b, dims, precision=precision, preferred_element_type=F32)


_NN = (((1,), (0,)), ((), ()))
_NT = (((1,), (1,)), ((), ()))
_TN = (((0,), (0,)), ((), ()))


def _mx(a):
    return a.astype(MXU_DTYPE)


def _silu(x):
    return x * (1.0 / (1.0 + jnp.exp(-x)))


def _silu_and_grad(x):
    s = 1.0 / (1.0 + jnp.exp(-x))
    return x * s, s * (1.0 + x * (1.0 - s))


def _pick(n, cands):
    for c in cands:
        if n % c == 0:
            return c
    return n


def rms_fwd(x, g):
    s, d = x.shape
    ts = _pick(s, (512, 256, 128))

    def body(x_ref, g_ref, h_ref, r_ref):
        xv = x_ref[...]
        r = lax.rsqrt(jnp.mean(xv * xv, axis=-1, keepdims=True) + NORM_EPS)
        h_ref[...] = (xv * r * g_ref[...]).astype(h_ref.dtype)
        r_ref[...] = r

    return pl.pallas_call(
        body, name="rms_fwd", grid=(s // ts,),
        in_specs=[pl.BlockSpec((ts, d), lambda i: (i, 0)), pl.BlockSpec((1, d), lambda i: (0, 0))],
        out_specs=[pl.BlockSpec((ts, d), lambda i: (i, 0)), pl.BlockSpec((ts, 1), lambda i: (i, 0))],
        out_shape=[_sds((s, d), MXU_DTYPE), _sds((s, 1), F32)],
        compiler_params=_params(("parallel",)),
    )(x, g)


def rms_bwd(x, r, g, dh, dres):
    s, d = x.shape
    ts = _pick(s, (512, 256, 128))

    def body(x_ref, r_ref, g_ref, dh_ref, dres_ref, dx_ref, dxm_ref, dg_ref):
        xh = x_ref[...] * r_ref[...]
        dhv = dh_ref[...]
        dxh = dhv * g_ref[...]
        dx = dres_ref[...] + r_ref[...] * (dxh - xh * jnp.mean(dxh * xh, axis=-1, keepdims=True))
        dx_ref[...] = dx
        dxm_ref[...] = dx.astype(dxm_ref.dtype)
        part = jnp.sum(dhv * xh, axis=0, keepdims=True)

        @pl.when(pl.program_id(0) == 0)
        def _():
            dg_ref[...] = part

        @pl.when(pl.program_id(0) != 0)
        def _():
            dg_ref[...] += part

    row = pl.BlockSpec((ts, d), lambda i: (i, 0))
    return pl.pallas_call(
        body, name="rms_bwd", grid=(s // ts,),
        in_specs=[row, pl.BlockSpec((ts, 1), lambda i: (i, 0)), pl.BlockSpec((1, d), lambda i: (0, 0)), row, row],
        out_specs=[row, row, pl.BlockSpec((1, d), lambda i: (0, 0))],
        out_shape=[_sds((s, d), F32), _sds((s, d), MXU_DTYPE), _sds((1, d), F32)],
        compiler_params=_params(("arbitrary",)),
    )(x, r, g, dh, dres)


def mm(a, b, mode="nn", res=None, out_dtype=F32, tm=None, tn=None, n_split=1, name="mm", hook=None):
    if mode == "nn":
        (m, k), (_, n) = a.shape, b.shape
    elif mode == "nt":
        (m, k), (n, _) = a.shape, b.shape
    else:
        (k, m), (_, n) = a.shape, b.shape
    tm = tm or _pick(m, (1024, 512, 256, 128))
    tn = tn or _pick(n, (512, 896, 256, 128))
    dims = {"nn": _NN, "nt": _NT, "tn": _TN}[mode]

    def body(*refs):
        a_ref, b_ref = refs[0], refs[1]
        o_ref = refs[-1]
        acc = _dot(_mx(a_ref[...]), _mx(b_ref[...]), dims)
        if res is not None:
            acc = acc + refs[2][...]
        o_ref[...] = acc.astype(o_ref.dtype)

    a_spec = pl.BlockSpec((k, tm), lambda i, j: (0, i)) if mode == "tn" else pl.BlockSpec((tm, k), lambda i, j: (i, 0))
    b_spec = pl.BlockSpec((tn, k), lambda i, j: (j, 0)) if mode == "nt" else pl.BlockSpec((k, tn), lambda i, j: (0, j))
    o_spec = pl.BlockSpec((tm, tn), lambda i, j: (i, j))
    ins, specs = [a, b], [a_spec, b_spec]
    if res is not None:
        ins.append(res)
        specs.append(o_spec)
    out_shape = _sds((m, n), out_dtype)
    if n_split > 1:
        per = n // n_split // tn
        o_spec = pl.BlockSpec((None, tm, tn), lambda i, j: (j // per, i, j % per))
        out_shape = _sds((n_split, m, n // n_split), out_dtype)
    out, carried = hosted_call(body, hook, name, (m // tm, n // tn), specs, o_spec, out_shape, [],
                               ("parallel", "parallel"), ins)
    return out if hook is None else (out, carried)


CONV_ROWS = 256
CONV_COLS = 128


def _row_iota8(cols):
    return lax.broadcasted_iota(jnp.int32, (SUBLANES, cols), 0)


def _shift_down(cur, prev8, k):
    if k == 0:
        return cur
    rolled = pltpu.roll(cur, k, 0)
    head = jnp.where(_row_iota8(cur.shape[1]) < k, pltpu.roll(prev8, k, 0), rolled[0:SUBLANES])
    return jnp.concatenate([head, rolled[SUBLANES:]], axis=0)


def _shift_up(cur, next8, k):
    if k == 0:
        return cur
    n = cur.shape[0]
    rolled = pltpu.roll(cur, n - k, 0)
    tail = jnp.where(_row_iota8(cur.shape[1]) >= SUBLANES - k, pltpu.roll(next8, SUBLANES - k, 0), rolled[n - SUBLANES:])
    return jnp.concatenate([rolled[:n - SUBLANES], tail], axis=0)


def _load_prev8(ref, i, rows):
    start = pl.multiple_of(jnp.maximum(i * rows - SUBLANES, 0), SUBLANES)
    p = ref[pl.ds(start, SUBLANES), :]
    return jnp.where(i > 0, p, jnp.zeros_like(p))


def _conv_rows(ref, w_ref, b_ref, i, rows, width):
    cur = ref[pl.ds(pl.multiple_of(i * rows, rows), rows), :]
    prev8 = _load_prev8(ref, i, rows)
    shifted = [_shift_down(cur, prev8, k) for k in range(width)]
    acc = b_ref[...] + w_ref[width - 1:width, :] * shifted[0]
    for k in range(1, width):
        acc = acc + w_ref[width - 1 - k:width - k, :] * shifted[k]
    return acc, shifted


def _conv_bwd_rows(du, next8, w_ref, width):
    acc = w_ref[width - 1:width, :] * du
    for k in range(1, width):
        acc = acc + w_ref[width - 1 - k:width - k, :] * _shift_up(du, next8, k)
    return acc


def ffn_mid_fwd(u0, cw, cb):
    s, f2 = u0.shape
    f = f2 // 2
    nt = f // CONV_COLS
    rows = min(CONV_ROWS, s)

    def body(ug_ref, uu_ref, wg_ref, wu_ref, bg_ref, bu_ref, a_ref):
        def step(i, carry):
            g, _ = _conv_rows(ug_ref, wg_ref, bg_ref, i, rows, FFN_CONV)
            u, _ = _conv_rows(uu_ref, wu_ref, bu_ref, i, rows, FFN_CONV)
            a_ref[pl.ds(pl.multiple_of(i * rows, rows), rows), :] = (_silu(g) * u).astype(a_ref.dtype)
            return carry

        lax.fori_loop(0, s // rows, step, 0)

    col = lambda off: pl.BlockSpec((s, CONV_COLS), lambda j: (0, j + off))
    wsp = lambda r, off: pl.BlockSpec((r, CONV_COLS), lambda j: (0, j + off))
    return pl.pallas_call(
        body, name="ffn_mid_fwd", grid=(nt,),
        in_specs=[col(0), col(nt), wsp(FFN_CONV, 0), wsp(FFN_CONV, nt), wsp(1, 0), wsp(1, nt)],
        out_specs=pl.BlockSpec((s, CONV_COLS), lambda j: (0, j)),
        out_shape=_sds((s, f), MXU_DTYPE), compiler_params=_params(("parallel",)),
    )(u0, u0, cw, cw, cb, cb)


def ffn_mid_bwd(u0, da, cw, cb):
    s, f2 = u0.shape
    f = f2 // 2
    nt = f // CONV_COLS
    rows = min(CONV_ROWS, s)
    nsteps = s // rows
    w = FFN_CONV

    def body(ug_ref, uu_ref, da_ref, wg_ref, wu_ref, bg_ref, bu_ref,
             dug_ref, duu_ref, dwg_ref, dwu_ref, dbg_ref, dbu_ref):
        zero8 = jnp.zeros((SUBLANES, CONV_COLS), F32)
        zrow = jnp.zeros((1, CONV_COLS), F32)

        def step(it, carry):
            ng, nu, accs = carry
            i = nsteps - 1 - it
            r0 = pl.multiple_of(i * rows, rows)
            g, sg = _conv_rows(ug_ref, wg_ref, bg_ref, i, rows, w)
            u, su = _conv_rows(uu_ref, wu_ref, bu_ref, i, rows, w)
            dav = da_ref[pl.ds(r0, rows), :]
            sg_val, sg_grad = _silu_and_grad(g)
            dg = dav * u * sg_grad
            du = dav * sg_val
            dug_ref[pl.ds(r0, rows), :] = _conv_bwd_rows(dg, ng, wg_ref, w).astype(dug_ref.dtype)
            duu_ref[pl.ds(r0, rows), :] = _conv_bwd_rows(du, nu, wu_ref, w).astype(duu_ref.dtype)
            new = []
            for j in range(w):
                new.append(accs[j] + jnp.sum(dg * sg[w - 1 - j], axis=0, keepdims=True))
            for j in range(w):
                new.append(accs[w + j] + jnp.sum(du * su[w - 1 - j], axis=0, keepdims=True))
            new.append(accs[2 * w] + jnp.sum(dg, axis=0, keepdims=True))
            new.append(accs[2 * w + 1] + jnp.sum(du, axis=0, keepdims=True))
            return dg[0:SUBLANES], du[0:SUBLANES], tuple(new)

        _, _, accs = lax.fori_loop(0, nsteps, step, (zero8, zero8, tuple([zrow] * (2 * w + 2))))
        dwg_ref[...] = jnp.concatenate(accs[0:w], axis=0)
        dwu_ref[...] = jnp.concatenate(accs[w:2 * w], axis=0)
        dbg_ref[...] = accs[2 * w]
        dbu_ref[...] = accs[2 * w + 1]

    col = lambda off: pl.BlockSpec((s, CONV_COLS), lambda j: (0, j + off))
    wsp = lambda r, off: pl.BlockSpec((r, CONV_COLS), lambda j: (0, j + off))
    outs = pl.pallas_call(
        body, name="ffn_mid_bwd", grid=(nt,),
        in_specs=[col(0), col(nt), col(0), wsp(w, 0), wsp(w, nt), wsp(1, 0), wsp(1, nt)],
        out_specs=[col(0), col(0), wsp(w, 0), wsp(w, 0), wsp(1, 0), wsp(1, 0)],
        out_shape=[_sds((s, f), MXU_DTYPE), _sds((s, f), MXU_DTYPE), _sds((w, f), F32), _sds((w, f), F32),
                   _sds((1, f), F32), _sds((1, f), F32)],
        compiler_params=_params(("parallel",)),
    )(u0, u0, da, cw, cw, cb, cb)
    dug, duu, dwg, dwu, dbg, dbu = outs
    return dug, duu, jnp.concatenate([dwg, dwu], axis=1), jnp.concatenate([dbg, dbu], axis=1)


SB_BLOCK = 128
SB_DEAD = 110.0
SB_HEADS_PER_STEP = 4


def _split_hi_lo(x):
    hi = x.astype(MXU_DTYPE)
    lo = (x - hi.astype(F32)).astype(MXU_DTYPE)
    return hi, lo


def _dot_exact01(x, tri):
    hi, lo = _split_hi_lo(x)
    return _dot(hi, tri) + _dot(lo, tri)


def _stack_heads(pair, lane_lo):
    zero = jnp.zeros_like(pair)
    return jnp.concatenate([jnp.where(lane_lo, pair, zero), jnp.where(lane_lo, zero, pair)], axis=0)


def _unstack_heads(tall, lane_lo):
    n = tall.shape[0] // 2
    return jnp.where(lane_lo, tall[:n], tall[n:])


def _sb_logits(stacked_q, k_ref, k0, pair_cols, blk):
    z = [_dot(sq, k_ref[pl.ds(k0, blk), cols], _NT) for sq, cols in zip(stacked_q, pair_cols)]
    return jnp.concatenate(z, axis=0) * (SB_HEAD_DIM ** -0.5)


def _sb_logs(z, qi, kb, blk):
    rows = qi * blk + (lax.broadcasted_iota(jnp.int32, z.shape, 0) & (blk - 1))
    cols = kb * blk + lax.broadcasted_iota(jnp.int32, z.shape, 1)
    strict = cols < rows
    t = jnp.log(1.0 + jnp.exp(-jnp.abs(z)))
    lb = jnp.minimum(z, 0.0) - t
    lf = jnp.where(strict, jnp.minimum(-z, 0.0) - t, 0.0)
    return lb, lf, strict


def _tri(blk, upper):
    r = lax.broadcasted_iota(jnp.int32, (blk, blk), 0)
    c = lax.broadcasted_iota(jnp.int32, (blk, blk), 1)
    return jnp.where((r > c) if upper else (r < c), 1.0, 0.0).astype(MXU_DTYPE)


def _tri_sum(x, tri2):
    hi, lo = _split_hi_lo(x)
    return _dot(jnp.concatenate([hi, lo], axis=1), tri2)


def sb_fwd(qkv, hook=None):
    s = qkv.shape[0]
    blk = min(SB_BLOCK, s)
    nblk = s // blk
    nh = SB_HEADS_PER_STEP
    nstep = SB_HEADS // nh
    dh = SB_HEAD_DIM

    def body(q_ref, k_ref, v_ref, o_ref):
        suffix_tri2 = jnp.concatenate([_tri(blk, True)] * 2, axis=0)
        lane_lo = lax.broadcasted_iota(jnp.int32, (1, LANES), 1) < dh
        pairs = [slice(p * LANES, (p + 1) * LANES) for p in range(nh // 2)]

        def qstep(qi, carry):
            q0 = pl.multiple_of(qi * blk, blk)
            qst = [_stack_heads(q_ref[pl.ds(q0, blk), cols], lane_lo) for cols in pairs]

            def kstep(st):
                it, run, accs, _ = st
                kb = qi - it
                k0 = pl.multiple_of(kb * blk, blk)
                lb, lf, strict = _sb_logs(_sb_logits(qst, k_ref, k0, pairs, blk), qi, kb, blk)
                sloc = _tri_sum(lf, suffix_tri2)
                a = _mx(jnp.where(strict, jnp.exp(lb + sloc + run), 0.0))
                accs = tuple(
                    acc + _unstack_heads(_dot(a[2 * blk * p:2 * blk * (p + 1)], v_ref[pl.ds(k0, blk), cols]), lane_lo)
                    for p, (acc, cols) in enumerate(zip(accs, pairs)))
                run = run + sloc[:, 0:1] + lf[:, 0:1]
                return it + 1, run, accs, jnp.max(run) > -SB_DEAD

            _, _, accs, _ = lax.while_loop(
                lambda st: jnp.logical_and(st[0] <= qi, st[3]), kstep,
                (jnp.int32(0), jnp.zeros((nh * blk, 1), F32), tuple([jnp.zeros((blk, LANES), F32)] * len(pairs)),
                 jnp.bool_(True)))
            for acc, cols in zip(accs, pairs):
                o_ref[pl.ds(q0, blk), cols] = acc.astype(o_ref.dtype)
            return carry

        lax.fori_loop(0, nblk, qstep, 0)

    col = lambda off: pl.BlockSpec((s, nh * dh), lambda p: (0, p + off))
    out, carried = hosted_call(body, hook, "sb_fwd", (nstep,), [col(0), col(nstep), col(2 * nstep)], col(0),
                               _sds((s, D_MODEL), MXU_DTYPE), [], ("parallel",), (qkv, qkv, qkv))
    return out if hook is None else (out, carried)


def sb_bwd(qkv, do, hook=None):
    s = qkv.shape[0]
    blk = min(SB_BLOCK, s)
    nblk = s // blk
    nh = SB_HEADS_PER_STEP
    nstep = SB_HEADS // nh
    dh = SB_HEAD_DIM

    def body(q_ref, k_ref, v_ref, do_ref, dq_ref, dk_ref, dv_ref, dk_acc, dv_acc, run_ref):
        suffix_tri2 = jnp.concatenate([_tri(blk, True)] * 2, axis=0)
        prefix_tri2 = jnp.concatenate([_tri(blk, False)] * 2, axis=0)
        dk_acc[...] = jnp.zeros_like(dk_acc)
        dv_acc[...] = jnp.zeros_like(dv_acc)
        lane_lo = lax.broadcasted_iota(jnp.int32, (1, LANES), 1) < dh
        pairs = [slice(p * LANES, (p + 1) * LANES) for p in range(nh // 2)]

        def qstep(qi, carry):
            q0 = pl.multiple_of(qi * blk, blk)
            qst = [_stack_heads(q_ref[pl.ds(q0, blk), cols], lane_lo) for cols in pairs]
            dost = [_stack_heads(do_ref[pl.ds(q0, blk), cols], lane_lo) for cols in pairs]

            def sweep1(st):
                it, run, _ = st
                kb = qi - it
                run_ref[kb] = run
                _, lf, _ = _sb_logs(_sb_logits(qst, k_ref, pl.multiple_of(kb * blk, blk), pairs, blk), qi, kb, blk)
                run = run + jnp.sum(lf, axis=1, keepdims=True)
                return it + 1, run, jnp.max(run) > -SB_DEAD

            nlive, _, _ = lax.while_loop(
                lambda st: jnp.logical_and(st[0] <= qi, st[2]), sweep1,
                (jnp.int32(0), jnp.zeros((nh * blk, 1), F32), jnp.bool_(True)))

            def sweep2(kb, st):
                pg, dqs = st
                k0 = pl.multiple_of(kb * blk, blk)
                lb, lf, strict = _sb_logs(_sb_logits(qst, k_ref, k0, pairs, blk), qi, kb, blk)
                sloc = _tri_sum(lf, suffix_tri2)
                a = jnp.where(strict, jnp.exp(lb + sloc + run_ref[kb]), 0.0)
                da = jnp.concatenate([_dot(d, v_ref[pl.ds(k0, blk), cols], _NT) for d, cols in zip(dost, pairs)], axis=0)
                g = da * a
                p = pg + _tri_sum(g, prefix_tri2)
                sig = jnp.exp(lb)
                dz = _mx(jnp.where(strict, g * (1.0 - sig) - p * sig, 0.0) * (dh ** -0.5))
                am = _mx(a)
                new_dqs = []
                for i, cols in enumerate(pairs):
                    rows = slice(2 * blk * i, 2 * blk * (i + 1))
                    new_dqs.append(dqs[i] + _unstack_heads(_dot(dz[rows], k_ref[pl.ds(k0, blk), cols]), lane_lo))
                    dk_acc[pl.ds(k0, blk), cols] += _dot(dz[rows], qst[i], _TN)
                    dv_acc[pl.ds(k0, blk), cols] += _dot(am[rows], dost[i], _TN)
                return pg + jnp.sum(g, axis=1, keepdims=True), tuple(new_dqs)

            _, dqs = lax.fori_loop(qi + 1 - nlive, qi + 1, sweep2,
                                   (jnp.zeros((nh * blk, 1), F32), tuple([jnp.zeros((blk, LANES), F32)] * len(pairs))))
            for dq, cols in zip(dqs, pairs):
                dq_ref[pl.ds(q0, blk), cols] = dq.astype(dq_ref.dtype)
            return carry

        lax.fori_loop(0, nblk, qstep, 0)
        dk_ref[...] = dk_acc[...].astype(dk_ref.dtype)
        dv_ref[...] = dv_acc[...].astype(dv_ref.dtype)

    col = lambda off: pl.BlockSpec((s, nh * dh), lambda p: (0, p + off))
    (dq, dk, dv), carried = hosted_call(
        body, hook, "sb_bwd", (nstep,), [col(0), col(nstep), col(2 * nstep), col(0)], [col(0), col(0), col(0)],
        [_sds((s, D_MODEL), MXU_DTYPE)] * 3,
        [pltpu.VMEM((s, nh * dh), F32), pltpu.VMEM((s, nh * dh), F32), pltpu.VMEM((nblk, nh * blk, 1), F32)],
        ("parallel",), (qkv, qkv, qkv, do))
    return jnp.concatenate([dq, dk, dv], axis=1), carried


SSD_XBC_TILE0 = SSD_D_INNER // CONV_COLS


def ssd_conv_fwd(proj, cw, cb):
    s = proj.shape[0]
    rows = min(CONV_ROWS, s)

    def body(u_ref, w_ref, b_ref, o_ref):
        def step(i, carry):
            u, _ = _conv_rows(u_ref, w_ref, b_ref, i, rows, SSD_CONV)
            o_ref[pl.ds(pl.multiple_of(i * rows, rows), rows), :] = _silu(u)
            return carry

        lax.fori_loop(0, s // rows, step, 0)

    return pl.pallas_call(
        body, name="ssd_conv_fwd", grid=(SSD_CONV_DIM // CONV_COLS,),
        in_specs=[pl.BlockSpec((s, CONV_COLS), lambda j: (0, j + SSD_XBC_TILE0)),
                  pl.BlockSpec((SSD_CONV, CONV_COLS), lambda j: (0, j)), pl.BlockSpec((1, CONV_COLS), lambda j: (0, j))],
        out_specs=pl.BlockSpec((s, CONV_COLS), lambda j: (0, j)),
        out_shape=_sds((s, SSD_CONV_DIM), F32), compiler_params=_params(("parallel",)),
    )(proj, cw, cb)


def ssd_conv_bwd(proj, dact, cw, cb):
    s = proj.shape[0]
    rows = min(CONV_ROWS, s)
    nsteps = s // rows
    w = SSD_CONV

    def body(u_ref, da_ref, w_ref, b_ref, du_ref, dw_ref, db_ref):
        def step(it, carry):
            nxt, accs = carry
            i = nsteps - 1 - it
            r0 = pl.multiple_of(i * rows, rows)
            u, sh = _conv_rows(u_ref, w_ref, b_ref, i, rows, w)
            dconv = da_ref[pl.ds(r0, rows), :] * _silu_and_grad(u)[1]
            du_ref[pl.ds(r0, rows), :] = _conv_bwd_rows(dconv, nxt, w_ref, w).astype(du_ref.dtype)
            new = [accs[j] + jnp.sum(dconv * sh[w - 1 - j], axis=0, keepdims=True) for j in range(w)]
            new.append(accs[w] + jnp.sum(dconv, axis=0, keepdims=True))
            return dconv[0:SUBLANES], tuple(new)

        zrow = jnp.zeros((1, CONV_COLS), F32)
        _, accs = lax.fori_loop(0, nsteps, step, (jnp.zeros((SUBLANES, CONV_COLS), F32), tuple([zrow] * (w + 1))))
        dw_ref[...] = jnp.concatenate(accs[0:w], axis=0)
        db_ref[...] = accs[w]

    col = pl.BlockSpec((s, CONV_COLS), lambda j: (0, j))
    return pl.pallas_call(
        body, name="ssd_conv_bwd", grid=(SSD_CONV_DIM // CONV_COLS,),
        in_specs=[pl.BlockSpec((s, CONV_COLS), lambda j: (0, j + SSD_XBC_TILE0)), col,
                  pl.BlockSpec((w, CONV_COLS), lambda j: (0, j)), pl.BlockSpec((1, CONV_COLS), lambda j: (0, j))],
        out_specs=[col, pl.BlockSpec((w, CONV_COLS), lambda j: (0, j)), pl.BlockSpec((1, CONV_COLS), lambda j: (0, j))],
        out_shape=[_sds((s, SSD_CONV_DIM), MXU_DTYPE), _sds((w, SSD_CONV_DIM), F32), _sds((1, SSD_CONV_DIM), F32)],
        compiler_params=_params(("parallel",)),
    )(proj, dact, cw, cb)


def _split3(x):
    hi = x.astype(MXU_DTYPE)
    r1 = x - hi.astype(F32)
    mid = r1.astype(MXU_DTYPE)
    lo = (r1 - mid.astype(F32)).astype(MXU_DTYPE)
    return hi, mid, lo


def _dot01(x, m, dims=_NN, left=False):
    parts = _split3(x)
    if left:
        return _dot(m, parts[0], dims) + _dot(m, parts[1], dims) + _dot(m, parts[2], dims)
    return _dot(parts[0], m, dims) + _dot(parts[1], m, dims) + _dot(parts[2], m, dims)


def _softplus(x):
    return jnp.maximum(x, 0.0) + jnp.log1p(jnp.exp(-jnp.abs(x)))


def _ssd_consts(dt_bias, a_log, d_skip):
    pad = lambda v: jnp.pad(v.reshape(1, SSD_HEADS), ((0, 0), (0, LANES - SSD_HEADS)))
    head_of = jnp.arange(SSD_D_INNER) // SSD_HEAD_DIM
    expand = (jnp.arange(LANES)[:, None] == head_of[None, :]).astype(MXU_DTYPE)
    return dict(bias_w=pad(dt_bias), alog_w=pad(a_log), bias_c=dt_bias.reshape(SSD_HEADS, 1),
                alog_c=a_log.reshape(SSD_HEADS, 1), dskip=jnp.repeat(d_skip, SSD_HEAD_DIM).reshape(1, SSD_D_INNER),
                expand=expand, reduce=expand.T)


def _ssd_chunk_prep(dtp, dtp_t, bias_w, alog_w, bias_c, alog_c, expand):
    L = dtp.shape[0]
    r = lax.broadcasted_iota(jnp.int32, (L, L), 0)
    c = lax.broadcasted_iota(jnp.int32, (L, L), 1)
    tril = r >= c
    lower = jnp.where(tril, 1.0, 0.0).astype(MXU_DTYPE)
    upper = jnp.where(r <= c, 1.0, 0.0).astype(MXU_DTYPE)
    dt_col = _softplus(dtp + bias_w)
    a_col = -jnp.exp(alog_w) * dt_col
    a_row = -jnp.exp(alog_c) * _softplus(dtp_t + bias_c)
    acum_col = _dot01(a_col, lower, left=True)
    acum_row = _dot01(a_row, upper)
    acum_full = _dot01(acum_col, expand)
    dt_full = _dot01(dt_col, expand)
    return dict(tril=tril, lower=lower, upper=upper, dt_col=dt_col, a_col=a_col, acum_col=acum_col,
                acum_row=acum_row, acum_full=acum_full, dt_full=dt_full)


def _head_mask(j):
    lane = lax.broadcasted_iota(jnp.int32, (1, LANES), 1)
    return jnp.where((lane // SSD_HEAD_DIM) == j, 1.0, 0.0)


def _decay(pre, h):
    seg = pre["acum_col"][:, h:h + 1] - pre["acum_row"][h:h + 1, :]
    return jnp.exp(jnp.where(pre["tril"], seg, -1e30))


def _ssd_specs(s, nc, rev):
    L = SSD_CHUNK
    ci = (lambda i: nc - 1 - i) if rev else (lambda i: i)
    const = lambda shape: pl.BlockSpec(shape, lambda i: (0,) * len(shape))
    return dict(
        xbc=pl.BlockSpec((L, SSD_CONV_DIM), lambda i: (ci(i), 0)),
        dtp=pl.BlockSpec((L, LANES), lambda i: (ci(i), SSD_IN_PAD // LANES - 1)),
        dtp_t=pl.BlockSpec((SSD_HEADS, L), lambda i: (0, ci(i))),
        rows=pl.BlockSpec((L, SSD_D_INNER), lambda i: (ci(i), 0)),
        state=pl.BlockSpec((1, SSD_GROUPS, SSD_STATE, 4 * SSD_HEAD_DIM), lambda i: (ci(i), 0, 0, 0)),
        consts=[const((1, LANES)), const((1, LANES)), const((SSD_HEADS, 1)), const((SSD_HEADS, 1)),
                const((1, SSD_D_INNER)), const((LANES, SSD_D_INNER)), const((SSD_D_INNER, LANES))],
    )


def _const_args(cs):
    return [cs["bias_w"], cs["alog_w"], cs["bias_c"], cs["alog_c"], cs["dskip"], cs["expand"], cs["reduce"]]


def ssd_scan_fwd(act, proj, dtp_t, cs, hook=None):
    s = act.shape[0]
    L = SSD_CHUNK
    nc = s // L
    G, N, GW = SSD_GROUPS, SSD_STATE, 4 * SSD_HEAD_DIM

    def body(act_ref, dtp_ref, dtpt_ref, bw_ref, aw_ref, bc_ref, ac_ref, dsk_ref, ex_ref, rd_ref, y_ref, st_out, st):
        @pl.when(pl.program_id(0) == 0)
        def _():
            st[...] = jnp.zeros_like(st)

        st_out[0] = st[...]
        pre = _ssd_chunk_prep(dtp_ref[...], dtpt_ref[...], bw_ref[...], aw_ref[...], bc_ref[...], ac_ref[...], ex_ref[...])
        acum_full = pre["acum_full"]
        last_full = acum_full[L - 1:L, :]
        for g in range(G):
            bg = _mx(act_ref[:, SSD_D_INNER + g * N:SSD_D_INNER + (g + 1) * N])
            cg = _mx(act_ref[:, SSD_D_INNER + G * N + g * N:SSD_D_INNER + G * N + (g + 1) * N])
            cb = _dot(cg, bg, _NT)
            for half in range(2):
                p = 2 * g + half
                cols = slice(p * LANES, (p + 1) * LANES)
                xs = act_ref[:, cols]
                xdt = xs * pre["dt_full"][:, cols]
                yd = jnp.zeros((L, LANES), F32)
                for j in range(2):
                    m = cb * _decay(pre, 2 * p + j)
                    yd = yd + _dot(_mx(m), _mx(xdt * _head_mask(j)))
                yoff = _dot(cg, _mx(st[g, :, half * LANES:(half + 1) * LANES])) * jnp.exp(acum_full[:, cols])
                y_ref[:, cols] = yd + yoff + dsk_ref[:, cols] * xs
                w = jnp.exp(last_full[:, cols] - acum_full[:, cols])
                st[g, :, half * LANES:(half + 1) * LANES] = (
                    st[g, :, half * LANES:(half + 1) * LANES] * jnp.exp(last_full[:, cols]) + _dot(bg, _mx(xdt * w), _TN))

    sp = _ssd_specs(s, nc, False)
    (y, states), carried = hosted_call(
        body, hook, "ssd_scan_fwd", (nc,), [sp["xbc"], sp["dtp"], sp["dtp_t"]] + sp["consts"],
        [sp["rows"], sp["state"]], [_sds((s, SSD_D_INNER), F32), _sds((nc, G, N, GW), F32)],
        [pltpu.VMEM((G, N, GW), F32)], ("arbitrary",), (act, proj, dtp_t, *_const_args(cs)))
    return y, states, carried


def ssd_scan_bwd(act, proj, dtp_t, cs, states, dy, hook=None):
    s = act.shape[0]
    L = SSD_CHUNK
    nc = s // L
    G, N, GW = SSD_GROUPS, SSD_STATE, 4 * SSD_HEAD_DIM

    def body(act_ref, dtp_ref, dtpt_ref, bw_ref, aw_ref, bc_ref, ac_ref, dsk_ref, ex_ref, rd_ref, st_ref, dy_ref,
             dact_ref, ddtp_ref, dalog_ref, dbias_ref, dskip_ref, dst, dxdt_ref, dac_ref):
        first = pl.program_id(0) == 0

        @pl.when(first)
        def _():
            dst[...] = jnp.zeros_like(dst)
            dalog_ref[...] = jnp.zeros_like(dalog_ref)
            dbias_ref[...] = jnp.zeros_like(dbias_ref)
            dskip_ref[...] = jnp.zeros_like(dskip_ref)

        expand, reduce = ex_ref[...], rd_ref[...]
        pre = _ssd_chunk_prep(dtp_ref[...], dtpt_ref[...], bw_ref[...], aw_ref[...], bc_ref[...], ac_ref[...], expand)
        acum_full = pre["acum_full"]
        last_full = acum_full[L - 1:L, :]
        ones = jnp.ones((2 * L, LANES), MXU_DTYPE)
        lane = lax.broadcasted_iota(jnp.int32, (L, LANES), 1)
        dacum_diag = jnp.zeros((L, LANES), F32)
        dlast_parts = []
        for g in range(G):
            bg = _mx(act_ref[:, SSD_D_INNER + g * N:SSD_D_INNER + (g + 1) * N])
            cg = _mx(act_ref[:, SSD_D_INNER + G * N + g * N:SSD_D_INNER + G * N + (g + 1) * N])
            cb = _dot(cg, bg, _NT)
            dcb = jnp.zeros((L, L), F32)
            dcg = jnp.zeros((L, N), F32)
            dbg = jnp.zeros((L, N), F32)
            for half in range(2):
                p = 2 * g + half
                cols = slice(p * LANES, (p + 1) * LANES)
                hcols = slice(half * LANES, (half + 1) * LANES)
                xs = act_ref[:, cols]
                xdt = xs * pre["dt_full"][:, cols]
                dyv = dy_ref[:, cols]
                dxdt = jnp.zeros((L, LANES), F32)
                for j in range(2):
                    h = 2 * p + j
                    dec = _decay(pre, h)
                    m = cb * dec
                    dyh = _mx(dyv * _head_mask(j))
                    dm = _dot(dyh, _mx(xdt), _NT)
                    dxdt = dxdt + _dot(_mx(m), dyh, _TN)
                    e = dm * m
                    ehi, elo = _split_hi_lo(e)
                    d_h = (_dot(jnp.concatenate([ehi, elo], axis=1), ones)
                           - _dot(jnp.concatenate([ehi, elo], axis=0), ones, _TN))
                    dacum_diag = jnp.where(lane == h, d_h, dacum_diag)
                    dcb = dcb + dm * dec
                lam = jnp.exp(acum_full[:, cols])
                stv = _mx(st_ref[0, g, :, hcols])
                z = _dot(cg, stv)
                dz = _mx(lam * dyv)
                dcg = dcg + _dot(dz, stv, _NT)
                dst_in = _dot(cg, dz, _TN)
                dsv = dst[g, :, hcols]
                w = jnp.exp(last_full[:, cols] - acum_full[:, cols])
                q = _dot(bg, _mx(dsv))
                wq = w * q
                dxdt = dxdt + wq
                wqx = wq * xdt
                dbg = dbg + _dot(_mx(xdt * w), _mx(dsv), _NT)
                elast = jnp.exp(last_full[:, cols])
                dlast_p = jnp.sum(wqx, axis=0, keepdims=True) + elast * jnp.sum(dsv * st_ref[0, g, :, hcols], axis=0, keepdims=True)
                dac_ref[:, cols] = dyv * z * lam - wqx
                dlast_parts.append(dlast_p)
                dst[g, :, hcols] = dst_in + dsv * elast
                dxdt_ref[:, cols] = dxdt
                dact_ref[:, cols] = dxdt * pre["dt_full"][:, cols] + dsk_ref[:, cols] * dyv
            dcbm = _mx(dcb)
            dact_ref[:, SSD_D_INNER + g * N:SSD_D_INNER + (g + 1) * N] = dbg + _dot(dcbm, cg, _TN)
            dact_ref[:, SSD_D_INNER + G * N + g * N:SSD_D_INNER + G * N + (g + 1) * N] = dcg + _dot(dcbm, bg)

        xs_all = act_ref[:, 0:SSD_D_INNER]
        dacum = dacum_diag + _dot_exact01(dac_ref[...], reduce)
        dlast = _dot_exact01(jnp.concatenate(dlast_parts, axis=1), reduce)
        row = lax.broadcasted_iota(jnp.int32, (L, LANES), 0)
        dacum = dacum + jnp.where(row == L - 1, dlast, 0.0)
        da_col = _dot01(dacum, pre["upper"], left=True)
        a_w = -jnp.exp(aw_ref[...])
        ddt = a_w * da_col + _dot_exact01(dxdt_ref[...] * xs_all, reduce)
        xin = dtp_ref[...] + bw_ref[...]
        ddtp = ddt * (1.0 / (1.0 + jnp.exp(-xin)))
        valid = lane < SSD_HEADS
        ddtp = jnp.where(valid, ddtp, 0.0)
        ddtp_ref[...] = ddtp
        dbias_ref[...] += jnp.sum(ddtp, axis=0, keepdims=True)
        dalog_ref[...] += jnp.sum(jnp.where(valid, da_col * pre["a_col"], 0.0), axis=0, keepdims=True)
        dskip_ref[...] += jnp.sum(_dot_exact01(dy_ref[...] * xs_all, reduce), axis=0, keepdims=True)

    sp = _ssd_specs(s, nc, True)
    acc = pl.BlockSpec((1, LANES), lambda i: (0, 0))
    outs, carried = hosted_call(
        body, hook, "ssd_scan_bwd", (nc,),
        [sp["xbc"], sp["dtp"], sp["dtp_t"]] + sp["consts"] + [sp["state"], sp["rows"]],
        [sp["xbc"], pl.BlockSpec((L, LANES), lambda i: (nc - 1 - i, 0)), acc, acc, acc],
        [_sds((s, SSD_CONV_DIM), F32), _sds((s, LANES), F32)] + [_sds((1, LANES), F32)] * 3,
        [pltpu.VMEM((G, N, GW), F32), pltpu.VMEM((L, SSD_D_INNER), F32), pltpu.VMEM((L, SSD_D_INNER), F32)],
        ("arbitrary",), (act, proj, dtp_t, *_const_args(cs), states, dy))
    return (*outs, carried)


def ssd_post_fwd(y, proj, g):
    s, d = y.shape
    ts = _pick(s, (256, 128))

    def body(y_ref, z_ref, g_ref, o_ref):
        y2 = y_ref[...] * _silu(z_ref[...])
        r = lax.rsqrt(jnp.mean(y2 * y2, axis=-1, keepdims=True) + NORM_EPS)
        o_ref[...] = (y2 * r * g_ref[...]).astype(o_ref.dtype)

    row = pl.BlockSpec((ts, d), lambda i: (i, 0))
    return pl.pallas_call(
        body, name="ssd_post_fwd", grid=(s // ts,), in_specs=[row, row, pl.BlockSpec((1, d), lambda i: (0, 0))],
        out_specs=row, out_shape=_sds((s, d), MXU_DTYPE), compiler_params=_params(("parallel",)),
    )(y, proj, g)


def ssd_post_bwd(y, proj, g, dy3):
    s, d = y.shape
    ts = _pick(s, (256, 128))

    def body(y_ref, z_ref, g_ref, d3_ref, dy_ref, dz_ref, dg_ref):
        yv, zv = y_ref[...], z_ref[...]
        sz, sgrad = _silu_and_grad(zv)
        y2 = yv * sz
        r = lax.rsqrt(jnp.mean(y2 * y2, axis=-1, keepdims=True) + NORM_EPS)
        xh = y2 * r
        d3 = d3_ref[...]
        dxh = d3 * g_ref[...]
        dy2 = r * (dxh - xh * jnp.mean(dxh * xh, axis=-1, keepdims=True))
        dy_ref[...] = dy2 * sz
        dz_ref[...] = (dy2 * yv * sgrad).astype(dz_ref.dtype)
        part = jnp.sum(d3 * xh, axis=0, keepdims=True)

        @pl.when(pl.program_id(0) == 0)
        def _():
            dg_ref[...] = part

        @pl.when(pl.program_id(0) != 0)
        def _():
            dg_ref[...] += part

    row = pl.BlockSpec((ts, d), lambda i: (i, 0))
    vec = pl.BlockSpec((1, d), lambda i: (0, 0))
    return pl.pallas_call(
        body, name="ssd_post_bwd", grid=(s // ts,), in_specs=[row, row, vec, row], out_specs=[row, row, vec],
        out_shape=[_sds((s, d), F32), _sds((s, d), MXU_DTYPE), _sds((1, d), F32)],
        compiler_params=_params(("arbitrary",)),
    )(y, proj, g, dy3)


def dt_transpose(proj):
    s = proj.shape[0]
    ts = _pick(s, (512, 256, 128))

    def body(p_ref, o_ref):
        o_ref[...] = p_ref[...].T

    return pl.pallas_call(
        body, name="dt_transpose", grid=(s // ts,),
        in_specs=[pl.BlockSpec((ts, LANES), lambda i: (i, SSD_IN_PAD // LANES - 1))],
        out_specs=pl.BlockSpec((LANES, ts), lambda i: (0, i)), out_shape=_sds((LANES, s), F32),
        compiler_params=_params(("parallel",)),
    )(proj)


def ssd_core_fwd(proj, cw, cb, dt_bias, a_log, d_skip, norm_g, hook=None):
    cs = _ssd_consts(dt_bias, a_log, d_skip)
    act = ssd_conv_fwd(proj, cw, cb)
    dtp_t = dt_transpose(proj)
    y, states, carried = ssd_scan_fwd(act, proj, dtp_t, cs, hook)
    y3 = ssd_post_fwd(y, proj, norm_g)
    return y3, (cs, act, dtp_t, y, states), carried


def ssd_core_bwd(proj, cw, cb, norm_g, saved, dy3, hook=None):
    cs, act, dtp_t, y, states = saved
    dy, dz, dnorm = ssd_post_bwd(y, proj, norm_g, dy3)
    dact, ddtp, dalog, dbias, dskip, carried = ssd_scan_bwd(act, proj, dtp_t, cs, states, dy, hook)
    dxbc, dcw, dcb = ssd_conv_bwd(proj, dact, cw, cb)
    dproj = jnp.concatenate([dz, dxbc, ddtp.astype(MXU_DTYPE)], axis=1)
    h = SSD_HEADS
    return dproj, dcw, dcb, dbias[0, :h], dalog[0, :h], dskip[0, :h], dnorm, carried


def ssd_core(proj, cw, cb, dt_bias, a_log, d_skip, norm_g, dy3):
    y3, saved, _ = ssd_core_fwd(proj, cw, cb, dt_bias, a_log, d_skip, norm_g)
    return y3, ssd_core_bwd(proj, cw, cb, norm_g, saved, dy3)


def loss_head(x, g, target):
    s, d = x.shape
    ts = _pick(s, (512, 256, 128))

    def body(x_ref, g_ref, t_ref, loss_ref, dx_ref, dxm_ref, dg_ref):
        xv = x_ref[...]
        r = lax.rsqrt(jnp.mean(xv * xv, axis=-1, keepdims=True) + NORM_EPS)
        xh = xv * r
        err = xh * g_ref[...] - t_ref[...]
        dy = err * (1.0 / d)
        dxh = dy * g_ref[...]
        dx = r * (dxh - xh * jnp.mean(dxh * xh, axis=-1, keepdims=True))
        dx_ref[...] = dx
        dxm_ref[...] = dx.astype(dxm_ref.dtype)
        part = jnp.sum(dy * xh, axis=0, keepdims=True)
        lpart = jnp.full((1, LANES), 0.5 * jnp.sum(jnp.mean(err * err, axis=-1, keepdims=True)), F32)

        @pl.when(pl.program_id(0) == 0)
        def _():
            dg_ref[...] = part
            loss_ref[...] = lpart

        @pl.when(pl.program_id(0) != 0)
        def _():
            dg_ref[...] += part
            loss_ref[...] += lpart

    row = pl.BlockSpec((ts, d), lambda i: (i, 0))
    vec = pl.BlockSpec((1, d), lambda i: (0, 0))
    return pl.pallas_call(
        body, name="loss_head", grid=(s // ts,), in_specs=[row, vec, row],
        out_specs=[pl.BlockSpec((1, LANES), lambda i: (0, 0)), row, row, vec],
        out_shape=[_sds((1, LANES), F32), _sds((s, d), F32), _sds((s, d), MXU_DTYPE), _sds((1, d), F32)],
        compiler_params=_params(("arbitrary",)),
    )(x, g, target)


def _adamw_math(w, g, m, v):
    m = ADAM_B1 * m + (1.0 - ADAM_B1) * g
    v = ADAM_B2 * v + (1.0 - ADAM_B2) * (g * g)
    m_hat = m / (1.0 - ADAM_B1 ** ADAM_STEP)
    v_hat = v / (1.0 - ADAM_B2 ** ADAM_STEP)
    return -ADAM_LR * (m_hat / (jnp.sqrt(v_hat) + ADAM_EPS) + ADAM_WD * w), m, v


def adamw(w, g, m, v, name="adamw"):
    r, c = w.shape
    tr = _pick(r, (256, 128, 64, 32, 16, 8))

    def body(w_ref, g_ref, m_ref, v_ref, d_ref, nm_ref, nv_ref):
        d_ref[...], nm_ref[...], nv_ref[...] = _adamw_math(w_ref[...], g_ref[...], m_ref[...], v_ref[...])

    blk = pl.BlockSpec((tr, c), lambda i: (i, 0))
    return pl.pallas_call(
        body, name=name, grid=(r // tr,), in_specs=[blk] * 4, out_specs=[blk] * 3,
        out_shape=[_sds((r, c), F32)] * 3, compiler_params=_params(("parallel",)),
    )(w, g, m, v)


def adamw_small(w, parts, m, v):
    n, r, c = parts.shape

    def body(w_ref, p_ref, m_ref, v_ref, g_ref, d_ref, nm_ref, nv_ref):
        g = p_ref[0]
        for k in range(1, n):
            g = g + p_ref[k]
        g_ref[...] = g
        d_ref[...], nm_ref[...], nv_ref[...] = _adamw_math(w_ref[...], g, m_ref[...], v_ref[...])

    return pl.pallas_call(
        body, name="adamw_small", out_shape=[_sds((r, c), F32)] * 4, compiler_params=_params(),
    )(w, parts, m, v)


def pair_sum(unit, recv, half):
    nchip, _, r, c = unit.shape
    tr = _pick(r, (512, 256, 176, 128, 64, 32, 16))

    def body(h_ref, a_ref, b_ref, o_ref, ob_ref):
        sm = a_ref[0, 0] + b_ref[0]
        o_ref[0] = sm
        ob_ref[0] = sm.astype(ob_ref.dtype)

    blk = pl.BlockSpec((1, tr, c), lambda s, i, h: (s, i, 0))
    return pl.pallas_call(
        body, name="pair_sum",
        grid_spec=pltpu.PrefetchScalarGridSpec(
            num_scalar_prefetch=1, grid=(nchip, r // tr),
            in_specs=[pl.BlockSpec((1, 1, tr, c), lambda s, i, h: (s, h[0], i, 0)), blk], out_specs=[blk, blk]),
        out_shape=[_sds((nchip, r, c), F32), _sds((nchip, r, c), jnp.bfloat16)],
        compiler_params=_params(("parallel", "parallel")),
    )(half, unit, recv)


def chip_sum(own, where, recv, layer, layers, prev=None):
    _, r, c = own.shape
    tr = _pick(r, (512, 256, 176, 128, 64, 32, 16))

    def body(s_ref, a_ref, b_ref, *rest):
        rest[-1][...] = a_ref[0] + b_ref[0].astype(F32) + b_ref[1].astype(F32) + b_ref[2].astype(F32)

    in_specs = [pl.BlockSpec((1, tr, c), lambda i, s: (s[0], i, 0)), pl.BlockSpec((3, tr, c), lambda i, s: (0, i, 0))]
    args = [where, own, recv]
    if prev is not None:
        in_specs.append(ANY)
        args.append(prev)
    return pl.pallas_call(
        body, name="chip_sum",
        grid_spec=pltpu.PrefetchScalarGridSpec(
            num_scalar_prefetch=1, grid=(r // tr,), in_specs=in_specs,
            out_specs=pl.BlockSpec((None, None, tr, c), lambda i, s: (layer, s[1], i, 0))),
        out_shape=_sds((layers, 2, r, c), F32), input_output_aliases={} if prev is None else {3: 0},
        compiler_params=_params(("parallel",)),
    )(*args)


def place_cast(w, layer, chip):
    _, a, b = w.shape
    ta = _pick(a, (512, 352, 256, 128))

    def body(c_ref, w_ref, o_ref):
        o_ref[...] = w_ref[...].astype(o_ref.dtype)

    return pl.pallas_call(
        body, name="place_cast",
        grid_spec=pltpu.PrefetchScalarGridSpec(
            num_scalar_prefetch=1, grid=(a // ta,),
            in_specs=[pl.BlockSpec((None, ta, b), lambda i, c: (layer, i, 0))],
            out_specs=pl.BlockSpec((None, ta, b), lambda i, c: (c[0], i, 0))),
        out_shape=_sds((N_CHIPS, a, b), MXU_DTYPE), compiler_params=_params(("parallel",)),
    )(chip, w)


ANY = pl.BlockSpec(memory_space=pl.ANY)
COMM = pltpu.CompilerParams(has_side_effects=True)


def _coords():
    return lax.axis_index("x"), lax.axis_index("y"), lax.axis_index("c")


def _other_chips(x, y):
    return [(1 - x, y), (x, 1 - y), (1 - x, 1 - y)]


def all_gather_8(halves, name):
    _, r, c = halves.shape

    def body(h_ref, out_ref, send_sems, recv_sems, local_sem):
        x, y, cc = _coords()
        _gather_one(h_ref.at[cc], lambda px, py, pc: out_ref.at[4 * px + 2 * py + pc],
                    lambda k: send_sems.at[k], lambda k: recv_sems.at[k], local_sem)

    return pl.pallas_call(
        body, name=name, in_specs=[ANY], out_specs=ANY, out_shape=_sds((8, r, c), halves.dtype),
        scratch_shapes=[pltpu.SemaphoreType.DMA((7,)), pltpu.SemaphoreType.DMA((7,)), pltpu.SemaphoreType.DMA],
        compiler_params=COMM,
    )(halves)


def _gather_plan(x_ref, slot, send_sem, recv_sem, local_sem):
    x, y, cc = _coords()
    me, sibling = (x, y, cc), (x, y, 1 - cc)
    chips = _other_chips(x, y)

    def copy(k, blk, to, src=None):
        return pltpu.make_async_remote_copy(
            src_ref=slot(*blk) if src is None else src, dst_ref=slot(*blk),
            send_sem=send_sem(k), recv_sem=recv_sem(k), device_id=to, device_id_type=MESH)

    mine = pltpu.make_async_copy(x_ref, slot(*me), local_sem)
    first = [copy(0, me, sibling, src=x_ref)] + [copy(1 + j, me, (*chip, cc), src=x_ref) for j, chip in enumerate(chips)]
    passed = [copy(4 + j, (*chip, cc), sibling) for j, chip in enumerate(chips)]
    over_ici = [copy(1 + j, (*chip, cc), me) for j, chip in enumerate(chips)]
    from_sibling = [copy(0, sibling, me)] + [copy(4 + j, (*chip, 1 - cc), me) for j, chip in enumerate(chips)]
    return mine, first, passed, over_ici, from_sibling


def _gather_run(plans):
    for mine, first, _, _, _ in plans:
        mine.start()
        for cp in first:
            cp.start()
    for j in range(3):
        for _, _, passed, over_ici, _ in plans:
            over_ici[j].wait_recv()
            passed[j].start()
    for mine, first, passed, _, from_sibling in plans:
        for cp in from_sibling:
            cp.wait_recv()
        for cp in first + passed:
            cp.wait_send()
        mine.wait()


def _gather_one(x_ref, slot, send_sem, recv_sem, local_sem):
    _gather_run([_gather_plan(x_ref, slot, send_sem, recv_sem, local_sem)])


def gather_hook(items):
    n = len(items)

    def plan(refs, send_sems, recv_sems):
        x, y, cc = _coords()

        def copy(i, k, px, py, pc, to):
            blk = refs[i].at[2 * px + py, pc]
            return pltpu.make_async_remote_copy(src_ref=blk, dst_ref=blk, send_sem=send_sems.at[i, k],
                                                recv_sem=recv_sems.at[i, k], device_id=to, device_id_type=MESH)

        chips = _other_chips(x, y)
        first = [copy(i, j, x, y, cc, (*chip, cc)) for i in range(n) for j, chip in enumerate(chips)]
        return copy, chips, first, (x, y, cc)

    def start(refs, new, sems):
        for cp in plan(refs, *sems)[2]:
            cp.start()

    def finish(refs, new, sems):
        copy, chips, first, (x, y, cc) = plan(refs, *sems)
        passed = []
        for j, chip in enumerate(chips):
            for i in range(n):
                copy(i, j, *chip, cc, (x, y, cc)).wait_recv()
                passed.append(copy(i, 3 + j, *chip, cc, (x, y, 1 - cc)))
                passed[-1].start()
        for j, chip in enumerate(chips):
            for i in range(n):
                copy(i, 3 + j, *chip, 1 - cc, (x, y, cc)).wait_recv()
        for cp in first + passed:
            cp.wait_send()

    return dict(arrays=list(items), new=[], start=start, finish=finish, in_place=True,
                sems=[pltpu.SemaphoreType.DMA((n, 6)), pltpu.SemaphoreType.DMA((n, 6))])


def hosted_call(body, hook, name, grid, in_specs, out_specs, out_shape, scratch_shapes, sem, args):
    single = not isinstance(out_shape, (list, tuple))
    out_specs_l = [out_specs] if single else list(out_specs)
    out_shape_l = [out_shape] if single else list(out_shape)
    if hook is None:
        res = pl.pallas_call(body, name=name, grid=grid, in_specs=list(in_specs), out_specs=out_specs, out_shape=out_shape,
                             scratch_shapes=list(scratch_shapes), compiler_params=_params(sem))(*args)
        return res, []
    items, new = hook["arrays"], hook["new"]
    k, kn, n_in, n_out, n_scr = len(items), len(new), len(in_specs), len(out_specs_l), len(scratch_shapes)
    ka = k if hook["in_place"] else 0

    def full(*refs):
        ins = refs[:n_in]
        base = n_in + k
        outs = refs[base:base + n_out]
        hrefs = refs[base + n_out:base + n_out + ka] if ka else refs[n_in:base]
        nrefs = refs[base + n_out + ka:base + n_out + ka + kn]
        scr = refs[base + n_out + ka + kn:base + n_out + ka + kn + n_scr]
        sems = refs[base + n_out + ka + kn + n_scr:]
        ids = [pl.program_id(d) for d in range(len(grid))]
        first = functools.reduce(jnp.logical_and, [i == 0 for i in ids])
        last = functools.reduce(jnp.logical_and, [i == g - 1 for i, g in zip(ids, grid)])

        @pl.when(first)
        def _():
            hook["start"](hrefs, nrefs, sems)

        body(*ins, *outs, *scr)

        @pl.when(last)
        def _():
            hook["finish"](hrefs, nrefs, sems)

    res = pl.pallas_call(
        full, name=name, grid=grid, in_specs=list(in_specs) + [ANY] * k, out_specs=out_specs_l + [ANY] * (ka + kn),
        out_shape=out_shape_l + [_sds(a.shape, a.dtype) for a in items[:ka]] + list(new),
        input_output_aliases={n_in + i: n_out + i for i in range(ka)},
        scratch_shapes=list(scratch_shapes) + hook["sems"],
        compiler_params=pltpu.CompilerParams(dimension_semantics=("arbitrary",) * len(grid),
                                             vmem_limit_bytes=VMEM_LIMIT, has_side_effects=True),
    )(*args, *items)
    return (res[0] if single else list(res[:n_out])), list(res[n_out:])


def comm_call(hook, name):
    k, kn = len(hook["arrays"]), len(hook["new"])
    ka = k if hook["in_place"] else 0

    def body(*refs):
        hrefs = refs[k:k + ka] if ka else refs[:k]
        hook["start"](hrefs, refs[k + ka:k + ka + kn], refs[k + ka + kn:])
        hook["finish"](hrefs, refs[k + ka:k + ka + kn], refs[k + ka + kn:])

    return list(pl.pallas_call(
        body, name=name, in_specs=[ANY] * k, out_specs=[ANY] * (ka + kn),
        out_shape=[_sds(a.shape, a.dtype) for a in hook["arrays"][:ka]] + list(hook["new"]),
        input_output_aliases={i: i for i in range(ka)}, scratch_shapes=hook["sems"], compiler_params=COMM,
    )(*hook["arrays"]))


def merge_hooks(hooks):
    hooks = [h for h in hooks if h is not None]
    if len(hooks) < 2:
        return hooks[0] if hooks else None

    def parts(refs, new, sems):
        out, a, b, c = [], 0, 0, 0
        for h in hooks:
            na, nn, ns = len(h["arrays"]), len(h["new"]), len(h["sems"])
            out.append((refs[a:a + na], new[b:b + nn], sems[c:c + ns]))
            a, b, c = a + na, b + nn, c + ns
        return out

    def start(refs, new, sems):
        for h, p in zip(hooks, parts(refs, new, sems)):
            h["start"](*p)

    def finish(refs, new, sems):
        for h, p in zip(hooks, parts(refs, new, sems)):
            h["finish"](*p)

    assert len({h["in_place"] for h in hooks}) == 1
    return dict(arrays=[a for h in hooks for a in h["arrays"]], new=[a for h in hooks for a in h["new"]],
                sems=[a for h in hooks for a in h["sems"]], start=start, finish=finish, in_place=hooks[0]["in_place"])


def split_carried(hooks, carried):
    hooks = [h for h in hooks if h is not None]
    off = sum(len(h["arrays"]) for h in hooks if h["in_place"])
    out = []
    for h in hooks:
        out.append(carried[off:off + len(h["new"])])
        off += len(h["new"])
    return out


def pair_swap_hook(units):
    n = len(units)

    def plan(refs, new, send_sems, recv_sems):
        x, y, cc = _coords()
        return [pltpu.make_async_remote_copy(src_ref=refs[i].at[:, 1 - cc], dst_ref=new[i], send_sem=send_sems.at[i],
                                             recv_sem=recv_sems.at[i], device_id=(x, y, 1 - cc), device_id_type=MESH)
                for i in range(n)]

    def start(refs, new, sems):
        for cp in plan(refs, new, *sems):
            cp.start()

    def finish(refs, new, sems):
        for cp in plan(refs, new, *sems):
            cp.wait()

    return dict(arrays=list(units), new=[_sds((u.shape[0],) + u.shape[2:], u.dtype) for u in units], start=start,
                finish=finish, in_place=False, sems=[pltpu.SemaphoreType.DMA((n,)), pltpu.SemaphoreType.DMA((n,))])


def chip_exchange_hook(units):
    n = len(units)

    def plan(refs, new, send_sems, recv_sems):
        x, y, cc = _coords()
        return [pltpu.make_async_remote_copy(
            src_ref=refs[i].at[2 * px + py], dst_ref=new[i].at[k], send_sem=send_sems.at[i, k],
            recv_sem=recv_sems.at[i, k], device_id=(px, py, cc), device_id_type=MESH)
            for i in range(n) for k, (px, py) in enumerate(_other_chips(x, y))]

    def start(refs, new, sems):
        for cp in plan(refs, new, *sems):
            cp.start()

    def finish(refs, new, sems):
        for cp in plan(refs, new, *sems):
            cp.wait()

    return dict(arrays=list(units), new=[_sds((3,) + u.shape[1:], u.dtype) for u in units], start=start,
                finish=finish, in_place=False, sems=[pltpu.SemaphoreType.DMA((n, 3)), pltpu.SemaphoreType.DMA((n, 3))])


def grad_half_swap(grads):
    n = len(grads)

    def body(*refs):
        outs, send_sems, recv_sems = refs[n:2 * n], refs[2 * n], refs[2 * n + 1]
        x, y, cc = _coords()
        cps = [pltpu.make_async_remote_copy(
            src_ref=outs[i].at[:, cc], dst_ref=outs[i].at[:, cc], send_sem=send_sems.at[i], recv_sem=recv_sems.at[i],
            device_id=(x, y, 1 - cc), device_id_type=MESH) for i in range(n)]
        for cp in cps:
            cp.start()
        for i, cp in enumerate(cps):
            cp.wait_send()
            pltpu.make_async_remote_copy(
                src_ref=outs[i].at[:, 1 - cc], dst_ref=outs[i].at[:, 1 - cc], send_sem=send_sems.at[i],
                recv_sem=recv_sems.at[i], device_id=(x, y, 1 - cc), device_id_type=MESH).wait_recv()

    return pl.pallas_call(
        body, name="grad_half_swap", in_specs=[ANY] * n, out_specs=[ANY] * n,
        out_shape=[_sds(g.shape, g.dtype) for g in grads], input_output_aliases={i: i for i in range(n)},
        scratch_shapes=[pltpu.SemaphoreType.DMA((n,)), pltpu.SemaphoreType.DMA((n,))], compiler_params=COMM,
    )(*grads)


N_CHIPS = 4
PACK_COLS = 1024
BIG = ("ssd_w_in", "ssd_w_out", "sb_w_qkv", "sb_w_out", "ffn_w_in", "ffn_w_out")
CONVW = ("ssd_conv_w", "ffn_conv_w")
COL_SHARDED = ("ssd_w_in", "sb_w_qkv", "ffn_w_in", "ssd_conv_w", "ffn_conv_w")
SMALL = ("mix_norm", "ffn_norm", "final_norm", "ssd_conv_b", "ssd_dt_bias", "ssd_a_log", "ssd_d", "ssd_norm", "ffn_conv_b")
WEIGHTS = ("mix_norm", "ffn_norm", "final_norm", "ssd_w_in", "ssd_conv_w", "ssd_conv_b", "ssd_dt_bias", "ssd_a_log",
           "ssd_d", "ssd_norm", "ssd_w_out", "sb_w_qkv", "sb_w_out", "ffn_w_in", "ffn_conv_w", "ffn_conv_b", "ffn_w_out")


def _to_rows(flat, multiple):
    rows = -(-flat.shape[-1] // PACK_COLS)
    rows = -(-rows // multiple) * multiple
    pad = rows * PACK_COLS - flat.shape[-1]
    return jnp.pad(flat, [(0, pad)]).reshape(rows, PACK_COLS)


def _unshard(name, stacked):
    l, n, a, b = stacked.shape
    if name in COL_SHARDED:
        return jnp.transpose(stacked, (0, 2, 1, 3)).reshape(l, a, n * b)
    return stacked.reshape(l, n * a, b)


def _gather_conv_weights(w):
    flat = jnp.concatenate([w[n].reshape(-1) for n in CONVW])
    rows = _to_rows(flat, 16)
    got = all_gather_8(rows.reshape(2, rows.shape[0] // 2, PACK_COLS), "gather_conv_weights").reshape(N_CHIPS, -1)
    out, off = {}, 0
    for n in CONVW:
        l, a, b = w[n].shape
        out[n] = _unshard(n, jnp.moveaxis(got[:, off:off + w[n].size].reshape(N_CHIPS, l, a, b), 0, 1))
        off += w[n].size
    return out


def _finish_big_grads(pair_sums, from_chips, layout):
    cc = lax.axis_index("c").astype(jnp.int32)
    chip = (2 * lax.axis_index("x") + lax.axis_index("y")).astype(jnp.int32)
    where = jnp.stack([chip, cc])
    nlayers = [1 + max(l for k, l in layout if k == wi) for wi in range(1 + max(k for k, _ in layout))]
    grads = [None] * len(nlayers)
    for (wi, l), p, r in zip(layout, pair_sums, from_chips):
        grads[wi] = chip_sum(p, where, r, l, nlayers[wi], grads[wi])
    return grad_half_swap(grads)


def kernel(x, mix_norm, ffn_norm, final_norm, ssd_w_in, ssd_conv_w, ssd_conv_b, ssd_dt_bias, ssd_a_log, ssd_d, ssd_norm, ssd_w_out, sb_w_qkv, sb_w_out, ffn_w_in, ffn_conv_w, ffn_conv_b, ffn_w_out, loss_target, m_mix_norm, m_ffn_norm, m_final_norm, m_ssd_w_in, m_ssd_conv_w, m_ssd_conv_b, m_ssd_dt_bias, m_ssd_a_log, m_ssd_d, m_ssd_norm, m_ssd_w_out, m_sb_w_qkv, m_sb_w_out, m_ffn_w_in, m_ffn_conv_w, m_ffn_conv_b, m_ffn_w_out, v_mix_norm, v_ffn_norm, v_final_norm, v_ssd_w_in, v_ssd_conv_w, v_ssd_conv_b, v_ssd_dt_bias, v_ssd_a_log, v_ssd_d, v_ssd_norm, v_ssd_w_out, v_sb_w_qkv, v_sb_w_out, v_ffn_w_in, v_ffn_conv_w, v_ffn_conv_b, v_ffn_w_out):
    given = dict(locals())
    w = {n: given[n] for n in WEIGHTS}
    mom = {n: given["m_" + n] for n in WEIGHTS}
    var = {n: given["v_" + n] for n in WEIGHTS}
    chip = 2 * lax.axis_index("x") + lax.axis_index("y")

    chip1 = chip.reshape(1).astype(jnp.int32)
    fw = _gather_conv_weights(w)
    row = lambda v: v.reshape(1, -1)

    def placed(n, l):
        _, a, b = w[n].shape
        return place_cast(w[n], l, chip1).reshape(N_CHIPS, 2, a // 2, b)

    def mixer_items(i):
        return [(n, i // 2) for n in (("ssd_w_in", "ssd_w_out") if i % 2 == 0 else ("sb_w_qkv", "sb_w_out"))]

    def ffn_items(i):
        return [("ffn_w_in", i), ("ffn_w_out", i)]

    def hook_for(items):
        return gather_hook([placed(n, l) for n, l in items]) if items else None

    lw = {}

    def arrived(items, arrays):
        for (n, l), arr in zip(items, arrays):
            g4 = arr.reshape(N_CHIPS, -1, arr.shape[-1])
            if n in COL_SHARDED:
                full = jnp.transpose(g4, (1, 0, 2)).reshape(g4.shape[1], -1)
            else:
                full = g4.reshape(-1, g4.shape[2])
            if n == "ssd_w_in":
                full = jnp.pad(full, ((0, 0), (0, SSD_IN_PAD - SSD_IN_DIM)))
            lw[(n, l)] = full

    first_items = mixer_items(0) + ffn_items(0)
    arrived(first_items, comm_call(hook_for(first_items), "gather_layer0"))

    xcur = x[0]
    saved = []
    for i in range(DEPTH):
        j = i // 2
        nxt = i + 1 < DEPTH
        h, r = rms_fwd(xcur, row(mix_norm[i]))
        if i % 2 == 0:
            items_a, items_b = (mixer_items(i + 1), ffn_items(i + 1)) if nxt else ([], [])
            proj = mm(h, lw[("ssd_w_in", j)], tn=896, name="mm_ssd_in", hook=hook_for(items_a))
            if items_a:
                proj, got = proj
                arrived(items_a, got)
            y3, core, got = ssd_core_fwd(proj, fw["ssd_conv_w"][j], row(ssd_conv_b[j]), ssd_dt_bias[j], ssd_a_log[j],
                                         ssd_d[j], row(ssd_norm[j]), hook_for(items_b))
            arrived(items_b, got)
            x1 = mm(y3, lw[("ssd_w_out", j)], res=xcur, name="mm_ssd_out")
            mix = (proj, y3, core)
        else:
            items = mixer_items(i + 1) + ffn_items(i + 1) if nxt else []
            qkv = mm(h, lw[("sb_w_qkv", j)], out_dtype=MXU_DTYPE, name="mm_sb_qkv")
            o = sb_fwd(qkv, hook_for(items))
            if items:
                o, got = o
                arrived(items, got)
            x1 = mm(o, lw[("sb_w_out", j)], res=xcur, name="mm_sb_out")
            mix = (qkv, o)
        h2, r2 = rms_fwd(x1, row(ffn_norm[i]))
        u0 = mm(h2, lw[("ffn_w_in", i)], name="mm_ffn_in")
        a = ffn_mid_fwd(u0, fw["ffn_conv_w"][i], row(ffn_conv_b[i]))
        x2 = mm(a, lw[("ffn_w_out", i)], res=x1, name="mm_ffn_out")
        saved.append((xcur, h, r, mix, x1, h2, r2, u0, a))
        xcur = x2
    loss_part, dx, dxm, d_final = loss_head(xcur, row(final_norm), loss_target[0])

    gl = {n: [None] * w[n].shape[0] for n in WEIGHTS if n != "final_norm"}
    units = {n: [None] * w[n].shape[0] for n in BIG}

    def unit_of(g4):
        return g4.reshape(N_CHIPS, 2, g4.shape[1] // 2, g4.shape[2])

    core1 = lax.axis_index("c").reshape(1).astype(jnp.int32)
    pair_f32, wire, from_chips = {}, {}, {}

    def pair_sums(keys, swapped):
        for (n, l), got in zip(keys, swapped):
            pair_f32[(n, l)], wire[(n, l)] = pair_sum(units[n][l], got, core1)

    for i in reversed(range(DEPTH)):
        j = i // 2
        x0, h, r, mix, x1, h2, r2, u0, a = saved[i]
        units["ffn_w_out"][i] = unit_of(mm(a, dxm, "tn", name="mm_d_ffn_out").reshape(N_CHIPS, -1, D_MODEL))
        da = mm(dxm, lw[("ffn_w_out", i)], "nt", name="mm_da_ffn")
        dug, duu, gl["ffn_conv_w"][i], dcb = ffn_mid_bwd(u0, da, fw["ffn_conv_w"][i], row(ffn_conv_b[i]))
        gl["ffn_conv_b"][i] = dcb[0]
        du0 = jnp.concatenate([dug, duu], axis=1)
        units["ffn_w_in"][i] = unit_of(mm(h2, du0, "tn", tn=1408, tm=512, n_split=N_CHIPS, name="mm_d_ffn_in"))
        keys_f = ffn_items(i)
        swap = pair_swap_hook([units[n][l] for n, l in keys_f])
        dh2, carried = mm(du0, lw[("ffn_w_in", i)], "nt", name="mm_dh_ffn", hook=swap)
        pair_sums(keys_f, split_carried([swap], carried)[0])
        dx1, dx1m, dg = rms_bwd(x1, r2, row(ffn_norm[i]), dh2, dx)
        gl["ffn_norm"][i] = dg[0]
        keys_up = mixer_items(i + 1) if i + 1 < DEPTH else []
        exchanges = [chip_exchange_hook([wire[k] for k in keys_f]),
                     chip_exchange_hook([wire[k] for k in keys_up]) if keys_up else None]
        if i % 2 == 0:
            proj, y3, core = mix
            units["ssd_w_out"][j] = unit_of(mm(y3, dx1m, "tn", name="mm_d_ssd_out").reshape(N_CHIPS, -1, D_MODEL))
            dy3 = mm(dx1m, lw[("ssd_w_out", j)], "nt", name="mm_dy3_ssd")
            (dproj, gl["ssd_conv_w"][j], dcb, gl["ssd_dt_bias"][j], gl["ssd_a_log"][j], gl["ssd_d"][j], dnorm,
             carried) = ssd_core_bwd(proj, fw["ssd_conv_w"][j], row(ssd_conv_b[j]), row(ssd_norm[j]), core, dy3,
                                     merge_hooks(exchanges))
            gl["ssd_conv_b"][j] = dcb[0]
            gl["ssd_norm"][j] = dnorm[0]
            dw_in = mm(h, dproj, "tn", tn=896, name="mm_d_ssd_in")[:, :SSD_IN_DIM]
            units["ssd_w_in"][j] = unit_of(jnp.transpose(dw_in.reshape(D_MODEL, N_CHIPS, -1), (1, 0, 2)))
            dmix, w_in, dh_name = dproj, lw[("ssd_w_in", j)], "mm_dh_ssd"
        else:
            qkv, o = mix
            units["sb_w_out"][j] = unit_of(mm(o, dx1m, "tn", name="mm_d_sb_out").reshape(N_CHIPS, -1, D_MODEL))
            do = mm(dx1m, lw[("sb_w_out", j)], "nt", out_dtype=MXU_DTYPE, name="mm_do_sb")
            dqkv, carried = sb_bwd(qkv, do, merge_hooks(exchanges))
            units["sb_w_qkv"][j] = unit_of(mm(h, dqkv, "tn", tn=768, n_split=N_CHIPS, name="mm_d_sb_qkv"))
            dmix, w_in, dh_name = dqkv, lw[("sb_w_qkv", j)], "mm_dh_sb"
        got = split_carried(exchanges, carried)
        from_chips.update(zip(keys_f, got[0]))
        if keys_up:
            from_chips.update(zip(keys_up, got[1]))
        keys_m = mixer_items(i)
        swap = pair_swap_hook([units[n][l] for n, l in keys_m])
        dh, carried = mm(dmix, w_in, "nt", name=dh_name, hook=swap)
        pair_sums(keys_m, split_carried([swap], carried)[0])
        dx, dxm, dg = rms_bwd(x0, r, row(mix_norm[i]), dh, dx1)
        gl["mix_norm"][i] = dg[0]
    last = chip_exchange_hook([wire[k] for k in mixer_items(0)])
    from_chips.update(zip(mixer_items(0), comm_call(last, "grad_exchange_last")))

    layout = [(k, l) for k, n in enumerate(BIG) for l in range(w[n].shape[0])]
    reduced = _finish_big_grads([pair_f32[(BIG[k], l)] for k, l in layout], [from_chips[(BIG[k], l)] for k, l in layout],
                                layout)
    g, delta, new_m, new_v = {}, {}, {}, {}
    two_d = lambda t: t.reshape(-1, t.shape[-1])
    for n, red in zip(BIG, reduced):
        g[n] = red.reshape(w[n].shape)
        d2, m2, v2 = adamw(two_d(w[n]), two_d(g[n]), two_d(mom[n]), two_d(var[n]), name="adamw_" + n)
        delta[n], new_m[n], new_v[n] = d2.reshape(w[n].shape), m2.reshape(w[n].shape), v2.reshape(w[n].shape)

    small_g = {n: jnp.stack(gl[n]) for n in SMALL + CONVW if n != "final_norm"}
    small_g["final_norm"] = d_final[0]
    zeros_of = lambda n: jnp.zeros((small_g[n].size,), F32)

    def small_pack(d, extra):
        parts = [d[n].reshape(-1) for n in SMALL] + [extra]
        parts += [(d[n].reshape(-1) if d is small_g else zeros_of(n)) for n in CONVW]
        return _to_rows(jnp.concatenate(parts), 16)

    part = small_pack(small_g, loss_part[0, 0:1])
    parts = all_gather_8(jnp.stack([part, part]), "gather_small_grads")
    zero = jnp.zeros((1,), F32)
    gs, ds, ms, vs = adamw_small(small_pack(w, zero), parts, small_pack(mom, zero), small_pack(var, zero))
    gs_flat = gs.reshape(-1)
    off = 0
    for n in SMALL:
        size = w[n].size
        for dst, src in ((g, gs), (delta, ds), (new_m, ms), (new_v, vs)):
            dst[n] = src.reshape(-1)[off:off + size].reshape(w[n].shape)
        off += size
    loss = gs_flat[off]
    off += 1
    for n in CONVW:
        size = small_g[n].size
        b = w[n].shape[-1]
        g[n] = lax.dynamic_slice_in_dim(gs_flat[off:off + size].reshape(small_g[n].shape), chip * b, b, axis=2)
        d2, m2, v2 = adamw(two_d(w[n]), two_d(g[n]), two_d(mom[n]), two_d(var[n]), name="adamw_" + n)
        delta[n], new_m[n], new_v[n] = d2.reshape(w[n].shape), m2.reshape(w[n].shape), v2.reshape(w[n].shape)
        off += size

    return (loss, dx[None], *[g[n] for n in WEIGHTS], *[delta[n] for n in WEIGHTS],
            *[new_m[n] for n in WEIGHTS], *[new_v[n] for n in WEIGHTS])
```

```python
import functools

import jax
import jax.numpy as jnp
from jax import lax
from jax.experimental import pallas as pl
from jax.experimental.pallas import tpu as pltpu

F32 = jnp.float32
MXU_DTYPE = jnp.bfloat16
HIGHEST = lax.Precision.HIGHEST

D_MODEL = 1024
DEPTH = 4
NORM_EPS = 1e-6
SSD_D_INNER = 2048
SSD_HEAD_DIM = 64
SSD_HEADS = 32
SSD_GROUPS = 8
SSD_STATE = 128
SSD_CONV = 4
SSD_CHUNK = 128
SSD_CONV_DIM = 4096
SSD_IN_DIM = 6176
SSD_IN_PAD = 6272
SB_HEADS = 16
SB_HEAD_DIM = 64
FFN_D_FF = 2816
FFN_CONV = 3
ADAM_LR, ADAM_B1, ADAM_B2, ADAM_EPS, ADAM_WD, ADAM_STEP = 0.001, 0.9, 0.999, 1e-08, 0.01, 10

LANES = 128
SUBLANES = 8
VMEM_LIMIT = 56 * 1024 * 1024
MESH = pl.DeviceIdType.MESH


def _params(sem=None):
    return pltpu.CompilerParams(dimension_semantics=sem, vmem_limit_bytes=VMEM_LIMIT)


def _sds(shape, dtype):
    return jax.ShapeDtypeStruct(shape, dtype)


def _dot(a, b, dims=(((1,), (0,)), ((), ())), precision=None):
    return lax.dot_general(a, b, dims, precision=precision, preferred_element_type=F32)


_NN = (((1,), (0,)), ((), ()))
_NT = (((1,), (1,)), ((), ()))
_TN = (((0,), (0,)), ((), ()))


def _mx(a):
    return a.astype(MXU_DTYPE)


def _silu(x):
    return x * (1.0 / (1.0 + jnp.exp(-x)))


def _silu_and_grad(x):
    s = 1.0 / (1.0 + jnp.exp(-x))
    return x * s, s * (1.0 + x * (1.0 - s))


def _pick(n, cands):
    for c in cands:
        if n % c == 0:
            return c
    return n


def rms_fwd(x, g):
    s, d = x.shape
    ts = _pick(s, (512, 256, 128))

    def body(x_ref, g_ref, h_ref, r_ref):
        xv = x_ref[...]
        r = lax.rsqrt(jnp.mean(xv * xv, axis=-1, keepdims=True) + NORM_EPS)
        h_ref[...] = (xv * r * g_ref[...]).astype(h_ref.dtype)
        r_ref[...] = r

    return pl.pallas_call(
        body, name="rms_fwd", grid=(s // ts,),
        in_specs=[pl.BlockSpec((ts, d), lambda i: (i, 0)), pl.BlockSpec((1, d), lambda i: (0, 0))],
        out_specs=[pl.BlockSpec((ts, d), lambda i: (i, 0)), pl.BlockSpec((ts, 1), lambda i: (i, 0))],
        out_shape=[_sds((s, d), MXU_DTYPE), _sds((s, 1), F32)],
        compiler_params=_params(("parallel",)),
    )(x, g)


def rms_bwd(x, r, g, dh, dres):
    s, d = x.shape
    ts = _pick(s, (512, 256, 128))

    def body(x_ref, r_ref, g_ref, dh_ref, dres_ref, dx_ref, dxm_ref, dg_ref):
        xh = x_ref[...] * r_ref[...]
        dhv = dh_ref[...]
        dxh = dhv * g_ref[...]
        dx = dres_ref[...] + r_ref[...] * (dxh - xh * jnp.mean(dxh * xh, axis=-1, keepdims=True))
        dx_ref[...] = dx
        dxm_ref[...] = dx.astype(dxm_ref.dtype)
        part = jnp.sum(dhv * xh, axis=0, keepdims=True)

        @pl.when(pl.program_id(0) == 0)
        def _():
            dg_ref[...] = part

        @pl.when(pl.program_id(0) != 0)
        def _():
            dg_ref[...] += part

    row = pl.BlockSpec((ts, d), lambda i: (i, 0))
    return pl.pallas_call(
        body, name="rms_bwd", grid=(s // ts,),
        in_specs=[row, pl.BlockSpec((ts, 1), lambda i: (i, 0)), pl.BlockSpec((1, d), lambda i: (0, 0)), row, row],
        out_specs=[row, row, pl.BlockSpec((1, d), lambda i: (0, 0))],
        out_shape=[_sds((s, d), F32), _sds((s, d), MXU_DTYPE), _sds((1, d), F32)],
        compiler_params=_params(("arbitrary",)),
    )(x, r, g, dh, dres)


def mm(a, b, mode="nn", res=None, out_dtype=F32, tm=None, tn=None, n_split=1, name="mm", hook=None):
    if mode == "nn":
        (m, k), (_, n) = a.shape, b.shape
    elif mode == "nt":
        (m, k), (n, _) = a.shape, b.shape
    else:
        (k, m), (_, n) = a.shape, b.shape
    tm = min(tm, m) if tm else _pick(m, (1024, 512, 256, 128))
    tn = min(tn, n) if tn else _pick(n, (512, 896, 256, 128))
    assert m % tm == 0 and n % tn == 0, (m, tm, n, tn)
    dims = {"nn": _NN, "nt": _NT, "tn": _TN}[mode]

    def body(*refs):
        a_ref, b_ref = refs[0], refs[1]
        o_ref = refs[-1]
        acc = _dot(_mx(a_ref[...]), _mx(b_ref[...]), dims)
        if res is not None:
            acc = acc + refs[2][...]
        o_ref[...] = acc.astype(o_ref.dtype)

    a_spec = pl.BlockSpec((k, tm), lambda i, j: (0, i)) if mode == "tn" else pl.BlockSpec((tm, k), lambda i, j: (i, 0))
    b_spec = pl.BlockSpec((tn, k), lambda i, j: (j, 0)) if mode == "nt" else pl.BlockSpec((k, tn), lambda i, j: (0, j))
    o_spec = pl.BlockSpec((tm, tn), lambda i, j: (i, j))
    ins, specs = [a, b], [a_spec, b_spec]
    if res is not None:
        ins.append(res)
        specs.append(o_spec)
    out_shape = _sds((m, n), out_dtype)
    if n_split > 1:
        per = n // n_split // tn
        o_spec = pl.BlockSpec((None, tm, tn), lambda i, j: (j // per, i, j % per))
        out_shape = _sds((n_split, m, n // n_split), out_dtype)
    out, carried = hosted_call(body, hook, name, (m // tm, n // tn), specs, o_spec, out_shape, [],
                               ("parallel", "parallel"), ins)
    return out if hook is None else (out, carried)


CONV_ROWS = 256
CONV_COLS = 128


def _row_iota8(cols):
    return lax.broadcasted_iota(jnp.int32, (SUBLANES, cols), 0)


def _shift_down(cur, prev8, k):
    if k == 0:
        return cur
    rolled = pltpu.roll(cur, k, 0)
    head = jnp.where(_row_iota8(cur.shape[1]) < k, pltpu.roll(prev8, k, 0), rolled[0:SUBLANES])
    return jnp.concatenate([head, rolled[SUBLANES:]], axis=0)


def _shift_up(cur, next8, k):
    if k == 0:
        return cur
    n = cur.shape[0]
    rolled = pltpu.roll(cur, n - k, 0)
    tail = jnp.where(_row_iota8(cur.shape[1]) >= SUBLANES - k, pltpu.roll(next8, SUBLANES - k, 0), rolled[n - SUBLANES:])
    return jnp.concatenate([rolled[:n - SUBLANES], tail], axis=0)


def _load_prev8(ref, i, rows):
    start = pl.multiple_of(jnp.maximum(i * rows - SUBLANES, 0), SUBLANES)
    p = ref[pl.ds(start, SUBLANES), :]
    return jnp.where(i > 0, p, jnp.zeros_like(p))


def _conv_rows(ref, w_ref, b_ref, i, rows, width):
    cur = ref[pl.ds(pl.multiple_of(i * rows, rows), rows), :]
    prev8 = _load_prev8(ref, i, rows)
    shifted = [_shift_down(cur, prev8, k) for k in range(width)]
    acc = b_ref[...] + w_ref[width - 1:width, :] * shifted[0]
    for k in range(1, width):
        acc = acc + w_ref[width - 1 - k:width - k, :] * shifted[k]
    return acc, shifted


def _conv_bwd_rows(du, next8, w_ref, width):
    acc = w_ref[width - 1:width, :] * du
    for k in range(1, width):
        acc = acc + w_ref[width - 1 - k:width - k, :] * _shift_up(du, next8, k)
    return acc


def ffn_mid_fwd(u0, cw, cb):
    s, f2 = u0.shape
    f = f2 // 2
    nt = f // CONV_COLS
    rows = min(CONV_ROWS, s)

    def body(ug_ref, uu_ref, wg_ref, wu_ref, bg_ref, bu_ref, a_ref):
        def step(i, carry):
            g, _ = _conv_rows(ug_ref, wg_ref, bg_ref, i, rows, FFN_CONV)
            u, _ = _conv_rows(uu_ref, wu_ref, bu_ref, i, rows, FFN_CONV)
            a_ref[pl.ds(pl.multiple_of(i * rows, rows), rows), :] = (_silu(g) * u).astype(a_ref.dtype)
            return carry

        lax.fori_loop(0, s // rows, step, 0)

    col = lambda off: pl.BlockSpec((s, CONV_COLS), lambda j: (0, j + off))
    wsp = lambda r, off: pl.BlockSpec((r, CONV_COLS), lambda j: (0, j + off))
    return pl.pallas_call(
        body, name="ffn_mid_fwd", grid=(nt,),
        in_specs=[col(0), col(nt), wsp(FFN_CONV, 0), wsp(FFN_CONV, nt), wsp(1, 0), wsp(1, nt)],
        out_specs=pl.BlockSpec((s, CONV_COLS), lambda j: (0, j)),
        out_shape=_sds((s, f), MXU_DTYPE), compiler_params=_params(("parallel",)),
    )(u0, u0, cw, cw, cb, cb)


def ffn_mid_bwd(u0, da, cw, cb):
    s, f2 = u0.shape
    f = f2 // 2
    nt = f // CONV_COLS
    rows = min(CONV_ROWS, s)
    nsteps = s // rows
    w = FFN_CONV

    def body(ug_ref, uu_ref, da_ref, wg_ref, wu_ref, bg_ref, bu_ref,
             dug_ref, duu_ref, dwg_ref, dwu_ref, dbg_ref, dbu_ref):
        zero8 = jnp.zeros((SUBLANES, CONV_COLS), F32)
        zrow = jnp.zeros((1, CONV_COLS), F32)

        def step(it, carry):
            ng, nu, accs = carry
            i = nsteps - 1 - it
            r0 = pl.multiple_of(i * rows, rows)
            g, sg = _conv_rows(ug_ref, wg_ref, bg_ref, i, rows, w)
            u, su = _conv_rows(uu_ref, wu_ref, bu_ref, i, rows, w)
            dav = da_ref[pl.ds(r0, rows), :]
            sg_val, sg_grad = _silu_and_grad(g)
            dg = dav * u * sg_grad
            du = dav * sg_val
            dug_ref[pl.ds(r0, rows), :] = _conv_bwd_rows(dg, ng, wg_ref, w).astype(dug_ref.dtype)
            duu_ref[pl.ds(r0, rows), :] = _conv_bwd_rows(du, nu, wu_ref, w).astype(duu_ref.dtype)
            new = []
            for j in range(w):
                new.append(accs[j] + jnp.sum(dg * sg[w - 1 - j], axis=0, keepdims=True))
            for j in range(w):
                new.append(accs[w + j] + jnp.sum(du * su[w - 1 - j], axis=0, keepdims=True))
            new.append(accs[2 * w] + jnp.sum(dg, axis=0, keepdims=True))
            new.append(accs[2 * w + 1] + jnp.sum(du, axis=0, keepdims=True))
            return dg[0:SUBLANES], du[0:SUBLANES], tuple(new)

        _, _, accs = lax.fori_loop(0, nsteps, step, (zero8, zero8, tuple([zrow] * (2 * w + 2))))
        dwg_ref[...] = jnp.concatenate(accs[0:w], axis=0)
        dwu_ref[...] = jnp.concatenate(accs[w:2 * w], axis=0)
        dbg_ref[...] = accs[2 * w]
        dbu_ref[...] = accs[2 * w + 1]

    col = lambda off: pl.BlockSpec((s, CONV_COLS), lambda j: (0, j + off))
    wsp = lambda r, off: pl.BlockSpec((r, CONV_COLS), lambda j: (0, j + off))
    outs = pl.pallas_call(
        body, name="ffn_mid_bwd", grid=(nt,),
        in_specs=[col(0), col(nt), col(0), wsp(w, 0), wsp(w, nt), wsp(1, 0), wsp(1, nt)],
        out_specs=[col(0), col(0), wsp(w, 0), wsp(w, 0), wsp(1, 0), wsp(1, 0)],
        out_shape=[_sds((s, f), MXU_DTYPE), _sds((s, f), MXU_DTYPE), _sds((w, f), F32), _sds((w, f), F32),
                   _sds((1, f), F32), _sds((1, f), F32)],
        compiler_params=_params(("parallel",)),
    )(u0, u0, da, cw, cw, cb, cb)
    dug, duu, dwg, dwu, dbg, dbu = outs
    return dug, duu, jnp.concatenate([dwg, dwu], axis=1), jnp.concatenate([dbg, dbu], axis=1)


SB_BLOCK = 128
SB_DEAD = 110.0
SB_HEADS_PER_STEP = 4


def _split_hi_lo(x):
    hi = x.astype(MXU_DTYPE)
    lo = (x - hi.astype(F32)).astype(MXU_DTYPE)
    return hi, lo


def _dot_exact01(x, tri):
    hi, lo = _split_hi_lo(x)
    return _dot(hi, tri) + _dot(lo, tri)


def _stack_heads(pair, lane_lo):
    zero = jnp.zeros_like(pair)
    return jnp.concatenate([jnp.where(lane_lo, pair, zero), jnp.where(lane_lo, zero, pair)], axis=0)


def _unstack_heads(tall, lane_lo):
    n = tall.shape[0] // 2
    return jnp.where(lane_lo, tall[:n], tall[n:])


def _sb_logits(stacked_q, k_ref, k0, pair_cols, blk):
    z = [_dot(sq, k_ref[pl.ds(k0, blk), cols], _NT) for sq, cols in zip(stacked_q, pair_cols)]
    return jnp.concatenate(z, axis=0) * (SB_HEAD_DIM ** -0.5)


def _sb_logs(z, qi, kb, blk):
    rows = qi * blk + (lax.broadcasted_iota(jnp.int32, z.shape, 0) & (blk - 1))
    cols = kb * blk + lax.broadcasted_iota(jnp.int32, z.shape, 1)
    strict = cols < rows
    t = jnp.log(1.0 + jnp.exp(-jnp.abs(z)))
    lb = jnp.minimum(z, 0.0) - t
    lf = jnp.where(strict, jnp.minimum(-z, 0.0) - t, 0.0)
    return lb, lf, strict


def _tri(blk, upper):
    r = lax.broadcasted_iota(jnp.int32, (blk, blk), 0)
    c = lax.broadcasted_iota(jnp.int32, (blk, blk), 1)
    return jnp.where((r > c) if upper else (r < c), 1.0, 0.0).astype(MXU_DTYPE)


def _tri_sum(x, tri2):
    hi, lo = _split_hi_lo(x)
    return _dot(jnp.concatenate([hi, lo], axis=1), tri2)


def sb_fwd(qkv, hook=None):
    s = qkv.shape[0]
    blk = min(SB_BLOCK, s)
    nblk = s // blk
    nh = SB_HEADS_PER_STEP
    nstep = SB_HEADS // nh
    dh = SB_HEAD_DIM

    def body(q_ref, k_ref, v_ref, o_ref):
        suffix_tri2 = jnp.concatenate([_tri(blk, True)] * 2, axis=0)
        lane_lo = lax.broadcasted_iota(jnp.int32, (1, LANES), 1) < dh
        pairs = [slice(p * LANES, (p + 1) * LANES) for p in range(nh // 2)]

        def qstep(qi, carry):
            q0 = pl.multiple_of(qi * blk, blk)
            qst = [_stack_heads(q_ref[pl.ds(q0, blk), cols], lane_lo) for cols in pairs]

            def kstep(st):
                it, run, accs, _ = st
                kb = qi - it
                k0 = pl.multiple_of(kb * blk, blk)
                lb, lf, strict = _sb_logs(_sb_logits(qst, k_ref, k0, pairs, blk), qi, kb, blk)
                sloc = _tri_sum(lf, suffix_tri2)
                a = _mx(jnp.where(strict, jnp.exp(lb + sloc + run), 0.0))
                accs = tuple(
                    acc + _unstack_heads(_dot(a[2 * blk * p:2 * blk * (p + 1)], v_ref[pl.ds(k0, blk), cols]), lane_lo)
                    for p, (acc, cols) in enumerate(zip(accs, pairs)))
                run = run + sloc[:, 0:1] + lf[:, 0:1]
                return it + 1, run, accs, jnp.max(run) > -SB_DEAD

            _, _, accs, _ = lax.while_loop(
                lambda st: jnp.logical_and(st[0] <= qi, st[3]), kstep,
                (jnp.int32(0), jnp.zeros((nh * blk, 1), F32), tuple([jnp.zeros((blk, LANES), F32)] * len(pairs)),
                 jnp.bool_(True)))
            for acc, cols in zip(accs, pairs):
                o_ref[pl.ds(q0, blk), cols] = acc.astype(o_ref.dtype)
            return carry

        lax.fori_loop(0, nblk, qstep, 0)

    col = lambda off: pl.BlockSpec((s, nh * dh), lambda p: (0, p + off))
    out, carried = hosted_call(body, hook, "sb_fwd", (nstep,), [col(0), col(nstep), col(2 * nstep)], col(0),
                               _sds((s, D_MODEL), MXU_DTYPE), [], ("parallel",), (qkv, qkv, qkv))
    return out if hook is None else (out, carried)


def sb_bwd(qkv, do, hook=None):
    s = qkv.shape[0]
    blk = min(SB_BLOCK, s)
    nblk = s // blk
    nh = SB_HEADS_PER_STEP
    nstep = SB_HEADS // nh
    dh = SB_HEAD_DIM

    def body(q_ref, k_ref, v_ref, do_ref, dq_ref, dk_ref, dv_ref, dk_acc, dv_acc, run_ref):
        suffix_tri2 = jnp.concatenate([_tri(blk, True)] * 2, axis=0)
        prefix_tri2 = jnp.concatenate([_tri(blk, False)] * 2, axis=0)
        dk_acc[...] = jnp.zeros_like(dk_acc)
        dv_acc[...] = jnp.zeros_like(dv_acc)
        lane_lo = lax.broadcasted_iota(jnp.int32, (1, LANES), 1) < dh
        pairs = [slice(p * LANES, (p + 1) * LANES) for p in range(nh // 2)]

        def qstep(qi, carry):
            q0 = pl.multiple_of(qi * blk, blk)
            qst = [_stack_heads(q_ref[pl.ds(q0, blk), cols], lane_lo) for cols in pairs]
            dost = [_stack_heads(do_ref[pl.ds(q0, blk), cols], lane_lo) for cols in pairs]

            def sweep1(st):
                it, run, _ = st
                kb = qi - it
                run_ref[kb] = run
                _, lf, _ = _sb_logs(_sb_logits(qst, k_ref, pl.multiple_of(kb * blk, blk), pairs, blk), qi, kb, blk)
                run = run + jnp.sum(lf, axis=1, keepdims=True)
                return it + 1, run, jnp.max(run) > -SB_DEAD

            nlive, _, _ = lax.while_loop(
                lambda st: jnp.logical_and(st[0] <= qi, st[2]), sweep1,
                (jnp.int32(0), jnp.zeros((nh * blk, 1), F32), jnp.bool_(True)))

            def sweep2(kb, st):
                pg, dqs = st
                k0 = pl.multiple_of(kb * blk, blk)
                lb, lf, strict = _sb_logs(_sb_logits(qst, k_ref, k0, pairs, blk), qi, kb, blk)
                sloc = _tri_sum(lf, suffix_tri2)
                a = jnp.where(strict, jnp.exp(lb + sloc + run_ref[kb]), 0.0)
                da = jnp.concatenate([_dot(d, v_ref[pl.ds(k0, blk), cols], _NT) for d, cols in zip(dost, pairs)], axis=0)
                g = da * a
                p = pg + _tri_sum(g, prefix_tri2)
                sig = jnp.exp(lb)
                dz = _mx(jnp.where(strict, g * (1.0 - sig) - p * sig, 0.0) * (dh ** -0.5))
                am = _mx(a)
                new_dqs = []
                for i, cols in enumerate(pairs):
                    rows = slice(2 * blk * i, 2 * blk * (i + 1))
                    new_dqs.append(dqs[i] + _unstack_heads(_dot(dz[rows], k_ref[pl.ds(k0, blk), cols]), lane_lo))
                    dk_acc[pl.ds(k0, blk), cols] += _dot(dz[rows], qst[i], _TN)
                    dv_acc[pl.ds(k0, blk), cols] += _dot(am[rows], dost[i], _TN)
                return pg + jnp.sum(g, axis=1, keepdims=True), tuple(new_dqs)

            _, dqs = lax.fori_loop(qi + 1 - nlive, qi + 1, sweep2,
                                   (jnp.zeros((nh * blk, 1), F32), tuple([jnp.zeros((blk, LANES), F32)] * len(pairs))))
            for dq, cols in zip(dqs, pairs):
                dq_ref[pl.ds(q0, blk), cols] = dq.astype(dq_ref.dtype)
            return carry

        lax.fori_loop(0, nblk, qstep, 0)
        dk_ref[...] = dk_acc[...].astype(dk_ref.dtype)
        dv_ref[...] = dv_acc[...].astype(dv_ref.dtype)

    col = lambda off: pl.BlockSpec((s, nh * dh), lambda p: (0, p + off))
    (dq, dk, dv), carried = hosted_call(
        body, hook, "sb_bwd", (nstep,), [col(0), col(nstep), col(2 * nstep), col(0)], [col(0), col(0), col(0)],
        [_sds((s, D_MODEL), MXU_DTYPE)] * 3,
        [pltpu.VMEM((s, nh * dh), F32), pltpu.VMEM((s, nh * dh), F32), pltpu.VMEM((nblk, nh * blk, 1), F32)],
        ("parallel",), (qkv, qkv, qkv, do))
    return jnp.concatenate([dq, dk, dv], axis=1), carried


SSD_XBC_TILE0 = SSD_D_INNER // CONV_COLS


def ssd_conv_fwd(proj, cw, cb):
    s = proj.shape[0]
    rows = min(CONV_ROWS, s)

    def body(u_ref, w_ref, b_ref, o_ref):
        def step(i, carry):
            u, _ = _conv_rows(u_ref, w_ref, b_ref, i, rows, SSD_CONV)
            o_ref[pl.ds(pl.multiple_of(i * rows, rows), rows), :] = _silu(u)
            return carry

        lax.fori_loop(0, s // rows, step, 0)

    return pl.pallas_call(
        body, name="ssd_conv_fwd", grid=(SSD_CONV_DIM // CONV_COLS,),
        in_specs=[pl.BlockSpec((s, CONV_COLS), lambda j: (0, j + SSD_XBC_TILE0)),
                  pl.BlockSpec((SSD_CONV, CONV_COLS), lambda j: (0, j)), pl.BlockSpec((1, CONV_COLS), lambda j: (0, j))],
        out_specs=pl.BlockSpec((s, CONV_COLS), lambda j: (0, j)),
        out_shape=_sds((s, SSD_CONV_DIM), F32), compiler_params=_params(("parallel",)),
    )(proj, cw, cb)


def ssd_conv_bwd(proj, dact, cw, cb):
    s = proj.shape[0]
    rows = min(CONV_ROWS, s)
    nsteps = s // rows
    w = SSD_CONV

    def body(u_ref, da_ref, w_ref, b_ref, du_ref, dw_ref, db_ref):
        def step(it, carry):
            nxt, accs = carry
            i = nsteps - 1 - it
            r0 = pl.multiple_of(i * rows, rows)
            u, sh = _conv_rows(u_ref, w_ref, b_ref, i, rows, w)
            dconv = da_ref[pl.ds(r0, rows), :] * _silu_and_grad(u)[1]
            du_ref[pl.ds(r0, rows), :] = _conv_bwd_rows(dconv, nxt, w_ref, w).astype(du_ref.dtype)
            new = [accs[j] + jnp.sum(dconv * sh[w - 1 - j], axis=0, keepdims=True) for j in range(w)]
            new.append(accs[w] + jnp.sum(dconv, axis=0, keepdims=True))
            return dconv[0:SUBLANES], tuple(new)

        zrow = jnp.zeros((1, CONV_COLS), F32)
        _, accs = lax.fori_loop(0, nsteps, step, (jnp.zeros((SUBLANES, CONV_COLS), F32), tuple([zrow] * (w + 1))))
        dw_ref[...] = jnp.concatenate(accs[0:w], axis=0)
        db_ref[...] = accs[w]

    col = pl.BlockSpec((s, CONV_COLS), lambda j: (0, j))
    return pl.pallas_call(
        body, name="ssd_conv_bwd", grid=(SSD_CONV_DIM // CONV_COLS,),
        in_specs=[pl.BlockSpec((s, CONV_COLS), lambda j: (0, j + SSD_XBC_TILE0)), col,
                  pl.BlockSpec((w, CONV_COLS), lambda j: (0, j)), pl.BlockSpec((1, CONV_COLS), lambda j: (0, j))],
        out_specs=[col, pl.BlockSpec((w, CONV_COLS), lambda j: (0, j)), pl.BlockSpec((1, CONV_COLS), lambda j: (0, j))],
        out_shape=[_sds((s, SSD_CONV_DIM), MXU_DTYPE), _sds((w, SSD_CONV_DIM), F32), _sds((1, SSD_CONV_DIM), F32)],
        compiler_params=_params(("parallel",)),
    )(proj, dact, cw, cb)


def _split3(x):
    hi = x.astype(MXU_DTYPE)
    r1 = x - hi.astype(F32)
    mid = r1.astype(MXU_DTYPE)
    lo = (r1 - mid.astype(F32)).astype(MXU_DTYPE)
    return hi, mid, lo


def _dot01(x, m, dims=_NN, left=False):
    parts = _split3(x)
    if left:
        return _dot(m, parts[0], dims) + _dot(m, parts[1], dims) + _dot(m, parts[2], dims)
    return _dot(parts[0], m, dims) + _dot(parts[1], m, dims) + _dot(parts[2], m, dims)


def _softplus(x):
    return jnp.maximum(x, 0.0) + jnp.log1p(jnp.exp(-jnp.abs(x)))


def _ssd_consts(dt_bias, a_log, d_skip):
    pad = lambda v: jnp.pad(v.reshape(1, SSD_HEADS), ((0, 0), (0, LANES - SSD_HEADS)))
    head_of = jnp.arange(SSD_D_INNER) // SSD_HEAD_DIM
    expand = (jnp.arange(LANES)[:, None] == head_of[None, :]).astype(MXU_DTYPE)
    return dict(bias_w=pad(dt_bias), alog_w=pad(a_log), bias_c=dt_bias.reshape(SSD_HEADS, 1),
                alog_c=a_log.reshape(SSD_HEADS, 1), dskip=jnp.repeat(d_skip, SSD_HEAD_DIM).reshape(1, SSD_D_INNER),
                expand=expand, reduce=expand.T)


def _ssd_chunk_prep(dtp, dtp_t, bias_w, alog_w, bias_c, alog_c, expand):
    L = dtp.shape[0]
    r = lax.broadcasted_iota(jnp.int32, (L, L), 0)
    c = lax.broadcasted_iota(jnp.int32, (L, L), 1)
    tril = r >= c
    lower = jnp.where(tril, 1.0, 0.0).astype(MXU_DTYPE)
    upper = jnp.where(r <= c, 1.0, 0.0).astype(MXU_DTYPE)
    dt_col = _softplus(dtp + bias_w)
    a_col = -jnp.exp(alog_w) * dt_col
    a_row = -jnp.exp(alog_c) * _softplus(dtp_t + bias_c)
    acum_col = _dot01(a_col, lower, left=True)
    acum_row = _dot01(a_row, upper)
    acum_full = _dot01(acum_col, expand)
    dt_full = _dot01(dt_col, expand)
    return dict(tril=tril, lower=lower, upper=upper, dt_col=dt_col, a_col=a_col, acum_col=acum_col,
                acum_row=acum_row, acum_full=acum_full, dt_full=dt_full)


def _head_mask(j):
    lane = lax.broadcasted_iota(jnp.int32, (1, LANES), 1)
    return jnp.where((lane // SSD_HEAD_DIM) == j, 1.0, 0.0)


def _decay(pre, h):
    seg = pre["acum_col"][:, h:h + 1] - pre["acum_row"][h:h + 1, :]
    return jnp.exp(jnp.where(pre["tril"], seg, -1e30))


def _ssd_specs(s, nc, rev):
    L = SSD_CHUNK
    ci = (lambda i: nc - 1 - i) if rev else (lambda i: i)
    const = lambda shape: pl.BlockSpec(shape, lambda i: (0,) * len(shape))
    return dict(
        xbc=pl.BlockSpec((L, SSD_CONV_DIM), lambda i: (ci(i), 0)),
        dtp=pl.BlockSpec((L, LANES), lambda i: (ci(i), SSD_IN_PAD // LANES - 1)),
        dtp_t=pl.BlockSpec((SSD_HEADS, L), lambda i: (0, ci(i))),
        rows=pl.BlockSpec((L, SSD_D_INNER), lambda i: (ci(i), 0)),
        state=pl.BlockSpec((1, SSD_GROUPS, SSD_STATE, 4 * SSD_HEAD_DIM), lambda i: (ci(i), 0, 0, 0)),
        consts=[const((1, LANES)), const((1, LANES)), const((SSD_HEADS, 1)), const((SSD_HEADS, 1)),
                const((1, SSD_D_INNER)), const((LANES, SSD_D_INNER)), const((SSD_D_INNER, LANES))],
    )


def _const_args(cs):
    return [cs["bias_w"], cs["alog_w"], cs["bias_c"], cs["alog_c"], cs["dskip"], cs["expand"], cs["reduce"]]


def ssd_scan_fwd(act, proj, dtp_t, cs, hook=None):
    s = act.shape[0]
    L = SSD_CHUNK
    nc = s // L
    G, N, GW = SSD_GROUPS, SSD_STATE, 4 * SSD_HEAD_DIM

    def body(act_ref, dtp_ref, dtpt_ref, bw_ref, aw_ref, bc_ref, ac_ref, dsk_ref, ex_ref, rd_ref, y_ref, st_out, st):
        @pl.when(pl.program_id(0) == 0)
        def _():
            st[...] = jnp.zeros_like(st)

        st_out[0] = st[...]
        pre = _ssd_chunk_prep(dtp_ref[...], dtpt_ref[...], bw_ref[...], aw_ref[...], bc_ref[...], ac_ref[...], ex_ref[...])
        acum_full = pre["acum_full"]
        last_full = acum_full[L - 1:L, :]
        for g in range(G):
            bg = _mx(act_ref[:, SSD_D_INNER + g * N:SSD_D_INNER + (g + 1) * N])
            cg = _mx(act_ref[:, SSD_D_INNER + G * N + g * N:SSD_D_INNER + G * N + (g + 1) * N])
            cb = _dot(cg, bg, _NT)
            for half in range(2):
                p = 2 * g + half
                cols = slice(p * LANES, (p + 1) * LANES)
                xs = act_ref[:, cols]
                xdt = xs * pre["dt_full"][:, cols]
                yd = jnp.zeros((L, LANES), F32)
                for j in range(2):
                    m = cb * _decay(pre, 2 * p + j)
                    yd = yd + _dot(_mx(m), _mx(xdt * _head_mask(j)))
                yoff = _dot(cg, _mx(st[g, :, half * LANES:(half + 1) * LANES])) * jnp.exp(acum_full[:, cols])
                y_ref[:, cols] = yd + yoff + dsk_ref[:, cols] * xs
                w = jnp.exp(last_full[:, cols] - acum_full[:, cols])
                st[g, :, half * LANES:(half + 1) * LANES] = (
                    st[g, :, half * LANES:(half + 1) * LANES] * jnp.exp(last_full[:, cols]) + _dot(bg, _mx(xdt * w), _TN))

    sp = _ssd_specs(s, nc, False)
    (y, states), carried = hosted_call(
        body, hook, "ssd_scan_fwd", (nc,), [sp["xbc"], sp["dtp"], sp["dtp_t"]] + sp["consts"],
        [sp["rows"], sp["state"]], [_sds((s, SSD_D_INNER), F32), _sds((nc, G, N, GW), F32)],
        [pltpu.VMEM((G, N, GW), F32)], ("arbitrary",), (act, proj, dtp_t, *_const_args(cs)))
    return y, states, carried


def ssd_scan_bwd(act, proj, dtp_t, cs, states, dy, hook=None):
    s = act.shape[0]
    L = SSD_CHUNK
    nc = s // L
    G, N, GW = SSD_GROUPS, SSD_STATE, 4 * SSD_HEAD_DIM

    def body(act_ref, dtp_ref, dtpt_ref, bw_ref, aw_ref, bc_ref, ac_ref, dsk_ref, ex_ref, rd_ref, st_ref, dy_ref,
             dact_ref, ddtp_ref, dalog_ref, dbias_ref, dskip_ref, dst, dxdt_ref, dac_ref):
        first = pl.program_id(0) == 0

        @pl.when(first)
        def _():
            dst[...] = jnp.zeros_like(dst)
            dalog_ref[...] = jnp.zeros_like(dalog_ref)
            dbias_ref[...] = jnp.zeros_like(dbias_ref)
            dskip_ref[...] = jnp.zeros_like(dskip_ref)

        expand, reduce = ex_ref[...], rd_ref[...]
        pre = _ssd_chunk_prep(dtp_ref[...], dtpt_ref[...], bw_ref[...], aw_ref[...], bc_ref[...], ac_ref[...], expand)
        acum_full = pre["acum_full"]
        last_full = acum_full[L - 1:L, :]
        ones = jnp.ones((2 * L, LANES), MXU_DTYPE)
        lane = lax.broadcasted_iota(jnp.int32, (L, LANES), 1)
        dacum_diag = jnp.zeros((L, LANES), F32)
        dlast_parts = []
        for g in range(G):
            bg = _mx(act_ref[:, SSD_D_INNER + g * N:SSD_D_INNER + (g + 1) * N])
            cg = _mx(act_ref[:, SSD_D_INNER + G * N + g * N:SSD_D_INNER + G * N + (g + 1) * N])
            cb = _dot(cg, bg, _NT)
            dcb = jnp.zeros((L, L), F32)
            dcg = jnp.zeros((L, N), F32)
            dbg = jnp.zeros((L, N), F32)
            for half in range(2):
                p = 2 * g + half
                cols = slice(p * LANES, (p + 1) * LANES)
                hcols = slice(half * LANES, (half + 1) * LANES)
                xs = act_ref[:, cols]
                xdt = xs * pre["dt_full"][:, cols]
                dyv = dy_ref[:, cols]
                dxdt = jnp.zeros((L, LANES), F32)
                for j in range(2):
                    h = 2 * p + j
                    dec = _decay(pre, h)
                    m = cb * dec
                    dyh = _mx(dyv * _head_mask(j))
                    dm = _dot(dyh, _mx(xdt), _NT)
                    dxdt = dxdt + _dot(_mx(m), dyh, _TN)
                    e = dm * m
                    ehi, elo = _split_hi_lo(e)
                    d_h = (_dot(jnp.concatenate([ehi, elo], axis=1), ones)
                           - _dot(jnp.concatenate([ehi, elo], axis=0), ones, _TN))
                    dacum_diag = jnp.where(lane == h, d_h, dacum_diag)
                    dcb = dcb + dm * dec
                lam = jnp.exp(acum_full[:, cols])
                stv = _mx(st_ref[0, g, :, hcols])
                z = _dot(cg, stv)
                dz = _mx(lam * dyv)
                dcg = dcg + _dot(dz, stv, _NT)
                dst_in = _dot(cg, dz, _TN)
                dsv = dst[g, :, hcols]
                w = jnp.exp(last_full[:, cols] - acum_full[:, cols])
                q = _dot(bg, _mx(dsv))
                wq = w * q
                dxdt = dxdt + wq
                wqx = wq * xdt
                dbg = dbg + _dot(_mx(xdt * w), _mx(dsv), _NT)
                elast = jnp.exp(last_full[:, cols])
                dlast_p = jnp.sum(wqx, axis=0, keepdims=True) + elast * jnp.sum(dsv * st_ref[0, g, :, hcols], axis=0, keepdims=True)
                dac_ref[:, cols] = dyv * z * lam - wqx
                dlast_parts.append(dlast_p)
                dst[g, :, hcols] = dst_in + dsv * elast
                dxdt_ref[:, cols] = dxdt
                dact_ref[:, cols] = dxdt * pre["dt_full"][:, cols] + dsk_ref[:, cols] * dyv
            dcbm = _mx(dcb)
            dact_ref[:, SSD_D_INNER + g * N:SSD_D_INNER + (g + 1) * N] = dbg + _dot(dcbm, cg, _TN)
            dact_ref[:, SSD_D_INNER + G * N + g * N:SSD_D_INNER + G * N + (g + 1) * N] = dcg + _dot(dcbm, bg)

        xs_all = act_ref[:, 0:SSD_D_INNER]
        dacum = dacum_diag + _dot_exact01(dac_ref[...], reduce)
        dlast = _dot_exact01(jnp.concatenate(dlast_parts, axis=1), reduce)
        row = lax.broadcasted_iota(jnp.int32, (L, LANES), 0)
        dacum = dacum + jnp.where(row == L - 1, dlast, 0.0)
        da_col = _dot01(dacum, pre["upper"], left=True)
        a_w = -jnp.exp(aw_ref[...])
        ddt = a_w * da_col + _dot_exact01(dxdt_ref[...] * xs_all, reduce)
        xin = dtp_ref[...] + bw_ref[...]
        ddtp = ddt * (1.0 / (1.0 + jnp.exp(-xin)))
        valid = lane < SSD_HEADS
        ddtp = jnp.where(valid, ddtp, 0.0)
        ddtp_ref[...] = ddtp
        dbias_ref[...] += jnp.sum(ddtp, axis=0, keepdims=True)
        dalog_ref[...] += jnp.sum(jnp.where(valid, da_col * pre["a_col"], 0.0), axis=0, keepdims=True)
        dskip_ref[...] += jnp.sum(_dot_exact01(dy_ref[...] * xs_all, reduce), axis=0, keepdims=True)

    sp = _ssd_specs(s, nc, True)
    acc = pl.BlockSpec((1, LANES), lambda i: (0, 0))
    outs, carried = hosted_call(
        body, hook, "ssd_scan_bwd", (nc,),
        [sp["xbc"], sp["dtp"], sp["dtp_t"]] + sp["consts"] + [sp["state"], sp["rows"]],
        [sp["xbc"], pl.BlockSpec((L, LANES), lambda i: (nc - 1 - i, 0)), acc, acc, acc],
        [_sds((s, SSD_CONV_DIM), F32), _sds((s, LANES), F32)] + [_sds((1, LANES), F32)] * 3,
        [pltpu.VMEM((G, N, GW), F32), pltpu.VMEM((L, SSD_D_INNER), F32), pltpu.VMEM((L, SSD_D_INNER), F32)],
        ("arbitrary",), (act, proj, dtp_t, *_const_args(cs), states, dy))
    return (*outs, carried)


def ssd_post_fwd(y, proj, g):
    s, d = y.shape
    ts = _pick(s, (256, 128))

    def body(y_ref, z_ref, g_ref, o_ref):
        y2 = y_ref[...] * _silu(z_ref[...])
        r = lax.rsqrt(jnp.mean(y2 * y2, axis=-1, keepdims=True) + NORM_EPS)
        o_ref[...] = (y2 * r * g_ref[...]).astype(o_ref.dtype)

    row = pl.BlockSpec((ts, d), lambda i: (i, 0))
    return pl.pallas_call(
        body, name="ssd_post_fwd", grid=(s // ts,), in_specs=[row, row, pl.BlockSpec((1, d), lambda i: (0, 0))],
        out_specs=row, out_shape=_sds((s, d), MXU_DTYPE), compiler_params=_params(("parallel",)),
    )(y, proj, g)


def ssd_post_bwd(y, proj, g, dy3):
    s, d = y.shape
    ts = _pick(s, (256, 128))

    def body(y_ref, z_ref, g_ref, d3_ref, dy_ref, dz_ref, dg_ref):
        yv, zv = y_ref[...], z_ref[...]
        sz, sgrad = _silu_and_grad(zv)
        y2 = yv * sz
        r = lax.rsqrt(jnp.mean(y2 * y2, axis=-1, keepdims=True) + NORM_EPS)
        xh = y2 * r
        d3 = d3_ref[...]
        dxh = d3 * g_ref[...]
        dy2 = r * (dxh - xh * jnp.mean(dxh * xh, axis=-1, keepdims=True))
        dy_ref[...] = dy2 * sz
        dz_ref[...] = (dy2 * yv * sgrad).astype(dz_ref.dtype)
        part = jnp.sum(d3 * xh, axis=0, keepdims=True)

        @pl.when(pl.program_id(0) == 0)
        def _():
            dg_ref[...] = part

        @pl.when(pl.program_id(0) != 0)
        def _():
            dg_ref[...] += part

    row = pl.BlockSpec((ts, d), lambda i: (i, 0))
    vec = pl.BlockSpec((1, d), lambda i: (0, 0))
    return pl.pallas_call(
        body, name="ssd_post_bwd", grid=(s // ts,), in_specs=[row, row, vec, row], out_specs=[row, row, vec],
        out_shape=[_sds((s, d), F32), _sds((s, d), MXU_DTYPE), _sds((1, d), F32)],
        compiler_params=_params(("arbitrary",)),
    )(y, proj, g, dy3)


def dt_transpose(proj):
    s = proj.shape[0]
    ts = _pick(s, (512, 256, 128))

    def body(p_ref, o_ref):
        o_ref[...] = p_ref[...].T

    return pl.pallas_call(
        body, name="dt_transpose", grid=(s // ts,),
        in_specs=[pl.BlockSpec((ts, LANES), lambda i: (i, SSD_IN_PAD // LANES - 1))],
        out_specs=pl.BlockSpec((LANES, ts), lambda i: (0, i)), out_shape=_sds((LANES, s), F32),
        compiler_params=_params(("parallel",)),
    )(proj)


def ssd_core_fwd(proj, cw, cb, dt_bias, a_log, d_skip, norm_g, hook=None):
    cs = _ssd_consts(dt_bias, a_log, d_skip)
    act = ssd_conv_fwd(proj, cw, cb)
    dtp_t = dt_transpose(proj)
    y, states, carried = ssd_scan_fwd(act, proj, dtp_t, cs, hook)
    y3 = ssd_post_fwd(y, proj, norm_g)
    return y3, (cs, act, dtp_t, y, states), carried


def ssd_core_bwd(proj, cw, cb, norm_g, saved, dy3, hook=None):
    cs, act, dtp_t, y, states = saved
    dy, dz, dnorm = ssd_post_bwd(y, proj, norm_g, dy3)
    dact, ddtp, dalog, dbias, dskip, carried = ssd_scan_bwd(act, proj, dtp_t, cs, states, dy, hook)
    dxbc, dcw, dcb = ssd_conv_bwd(proj, dact, cw, cb)
    dproj = jnp.concatenate([dz, dxbc, ddtp.astype(MXU_DTYPE)], axis=1)
    h = SSD_HEADS
    return dproj, dcw, dcb, dbias[0, :h], dalog[0, :h], dskip[0, :h], dnorm, carried


def ssd_core(proj, cw, cb, dt_bias, a_log, d_skip, norm_g, dy3):
    y3, saved, _ = ssd_core_fwd(proj, cw, cb, dt_bias, a_log, d_skip, norm_g)
    return y3, ssd_core_bwd(proj, cw, cb, norm_g, saved, dy3)


def loss_head(x, g, target):
    s, d = x.shape
    ts = _pick(s, (512, 256, 128))

    def body(x_ref, g_ref, t_ref, loss_ref, dx_ref, dxm_ref, dg_ref):
        xv = x_ref[...]
        r = lax.rsqrt(jnp.mean(xv * xv, axis=-1, keepdims=True) + NORM_EPS)
        xh = xv * r
        err = xh * g_ref[...] - t_ref[...]
        dy = err * (1.0 / d)
        dxh = dy * g_ref[...]
        dx = r * (dxh - xh * jnp.mean(dxh * xh, axis=-1, keepdims=True))
        dx_ref[...] = dx
        dxm_ref[...] = dx.astype(dxm_ref.dtype)
        part = jnp.sum(dy * xh, axis=0, keepdims=True)
        lpart = jnp.full((1, LANES), 0.5 * jnp.sum(jnp.mean(err * err, axis=-1, keepdims=True)), F32)

        @pl.when(pl.program_id(0) == 0)
        def _():
            dg_ref[...] = part
            loss_ref[...] = lpart

        @pl.when(pl.program_id(0) != 0)
        def _():
            dg_ref[...] += part
            loss_ref[...] += lpart

    row = pl.BlockSpec((ts, d), lambda i: (i, 0))
    vec = pl.BlockSpec((1, d), lambda i: (0, 0))
    return pl.pallas_call(
        body, name="loss_head", grid=(s // ts,), in_specs=[row, vec, row],
        out_specs=[pl.BlockSpec((1, LANES), lambda i: (0, 0)), row, row, vec],
        out_shape=[_sds((1, LANES), F32), _sds((s, d), F32), _sds((s, d), MXU_DTYPE), _sds((1, d), F32)],
        compiler_params=_params(("arbitrary",)),
    )(x, g, target)


def _adamw_math(w, g, m, v):
    m = ADAM_B1 * m + (1.0 - ADAM_B1) * g
    v = ADAM_B2 * v + (1.0 - ADAM_B2) * (g * g)
    m_hat = m / (1.0 - ADAM_B1 ** ADAM_STEP)
    v_hat = v / (1.0 - ADAM_B2 ** ADAM_STEP)
    return -ADAM_LR * (m_hat / (jnp.sqrt(v_hat) + ADAM_EPS) + ADAM_WD * w), m, v


def adamw(w, g, m, v, name="adamw"):
    r, c = w.shape
    tr = _pick(r, (256, 128, 64, 32, 16, 8))

    def body(w_ref, g_ref, m_ref, v_ref, d_ref, nm_ref, nv_ref):
        d_ref[...], nm_ref[...], nv_ref[...] = _adamw_math(w_ref[...], g_ref[...], m_ref[...], v_ref[...])

    blk = pl.BlockSpec((tr, c), lambda i: (i, 0))
    return pl.pallas_call(
        body, name=name, grid=(r // tr,), in_specs=[blk] * 4, out_specs=[blk] * 3,
        out_shape=[_sds((r, c), F32)] * 3, compiler_params=_params(("parallel",)),
    )(w, g, m, v)


def adamw_small(w, parts, m, v):
    n, r, c = parts.shape

    def body(w_ref, p_ref, m_ref, v_ref, g_ref, d_ref, nm_ref, nv_ref):
        g = p_ref[0]
        for k in range(1, n):
            g = g + p_ref[k]
        g_ref[...] = g
        d_ref[...], nm_ref[...], nv_ref[...] = _adamw_math(w_ref[...], g, m_ref[...], v_ref[...])

    return pl.pallas_call(
        body, name="adamw_small", out_shape=[_sds((r, c), F32)] * 4, compiler_params=_params(),
    )(w, parts, m, v)


def pair_sum(unit, recv, where):
    nchip, _, r, c = unit.shape
    tr = _pick(r, (512, 256, 176, 128, 64, 32, 16))

    def body(w_ref, a_ref, b_ref, o_ref, ob_ref):
        sm = a_ref[0, 0] + b_ref[0]
        ob_ref[0] = sm.astype(ob_ref.dtype)

        @pl.when(pl.program_id(1) == w_ref[0])
        def _():
            o_ref[...] = sm

    blk = pl.BlockSpec((1, tr, c), lambda i, s, w: (s, i, 0))
    return pl.pallas_call(
        body, name="pair_sum",
        grid_spec=pltpu.PrefetchScalarGridSpec(
            num_scalar_prefetch=1, grid=(r // tr, nchip),
            in_specs=[pl.BlockSpec((1, 1, tr, c), lambda i, s, w: (s, w[1], i, 0)), blk],
            out_specs=[pl.BlockSpec((tr, c), lambda i, s, w: (i, 0)), blk]),
        out_shape=[_sds((r, c), F32), _sds((nchip, r, c), jnp.bfloat16)],
        compiler_params=_params(("parallel", "arbitrary")),
    )(where, unit, recv)


def chip_sum(own, where, recv, layer, layers, prev=None):
    r, c = own.shape
    tr = _pick(r, (512, 256, 176, 128, 64, 32, 16))

    def body(s_ref, a_ref, b_ref, *rest):
        rest[-1][...] = a_ref[...] + b_ref[0].astype(F32) + b_ref[1].astype(F32) + b_ref[2].astype(F32)

    in_specs = [pl.BlockSpec((tr, c), lambda i, s: (i, 0)), pl.BlockSpec((3, tr, c), lambda i, s: (0, i, 0))]
    args = [where, own, recv]
    if prev is not None:
        in_specs.append(ANY)
        args.append(prev)
    return pl.pallas_call(
        body, name="chip_sum",
        grid_spec=pltpu.PrefetchScalarGridSpec(
            num_scalar_prefetch=1, grid=(r // tr,), in_specs=in_specs,
            out_specs=pl.BlockSpec((None, None, tr, c), lambda i, s: (layer, s[1], i, 0))),
        out_shape=_sds((layers, 2, r, c), F32), input_output_aliases={} if prev is None else {3: 0},
        compiler_params=_params(("parallel",)),
    )(*args)


def place_cast(w, layer, chip):
    _, a, b = w.shape
    ta = _pick(a, (512, 352, 256, 128))

    def body(c_ref, w_ref, o_ref):
        o_ref[...] = w_ref[...].astype(o_ref.dtype)

    return pl.pallas_call(
        body, name="place_cast",
        grid_spec=pltpu.PrefetchScalarGridSpec(
            num_scalar_prefetch=1, grid=(a // ta,),
            in_specs=[pl.BlockSpec((None, ta, b), lambda i, c: (layer, i, 0))],
            out_specs=pl.BlockSpec((None, ta, b), lambda i, c: (c[0], i, 0))),
        out_shape=_sds((N_CHIPS, a, b), MXU_DTYPE), compiler_params=_params(("parallel",)),
    )(chip, w)


ANY = pl.BlockSpec(memory_space=pl.ANY)
COMM = pltpu.CompilerParams(has_side_effects=True)


def _coords():
    return lax.axis_index("x"), lax.axis_index("y"), lax.axis_index("c")


def _other_chips(x, y):
    return [(1 - x, y), (x, 1 - y), (1 - x, 1 - y)]


def all_gather_8(halves, name):
    _, r, c = halves.shape

    def body(h_ref, out_ref, send_sems, recv_sems, local_sem):
        x, y, cc = _coords()
        _gather_one(h_ref.at[cc], lambda px, py, pc: out_ref.at[4 * px + 2 * py + pc],
                    lambda k: send_sems.at[k], lambda k: recv_sems.at[k], local_sem)

    return pl.pallas_call(
        body, name=name, in_specs=[ANY], out_specs=ANY, out_shape=_sds((8, r, c), halves.dtype),
        scratch_shapes=[pltpu.SemaphoreType.DMA((7,)), pltpu.SemaphoreType.DMA((7,)), pltpu.SemaphoreType.DMA],
        compiler_params=COMM,
    )(halves)


def _gather_plan(x_ref, slot, send_sem, recv_sem, local_sem):
    x, y, cc = _coords()
    me, sibling = (x, y, cc), (x, y, 1 - cc)
    chips = _other_chips(x, y)

    def copy(k, blk, to, src=None):
        return pltpu.make_async_remote_copy(
            src_ref=slot(*blk) if src is None else src, dst_ref=slot(*blk),
            send_sem=send_sem(k), recv_sem=recv_sem(k), device_id=to, device_id_type=MESH)

    mine = pltpu.make_async_copy(x_ref, slot(*me), local_sem)
    first = [copy(0, me, sibling, src=x_ref)] + [copy(1 + j, me, (*chip, cc), src=x_ref) for j, chip in enumerate(chips)]
    passed = [copy(4 + j, (*chip, cc), sibling) for j, chip in enumerate(chips)]
    over_ici = [copy(1 + j, (*chip, cc), me) for j, chip in enumerate(chips)]
    from_sibling = [copy(0, sibling, me)] + [copy(4 + j, (*chip, 1 - cc), me) for j, chip in enumerate(chips)]
    return mine, first, passed, over_ici, from_sibling


def _gather_run(plans):
    for mine, first, _, _, _ in plans:
        mine.start()
        for cp in first:
            cp.start()
    for j in range(3):
        for _, _, passed, over_ici, _ in plans:
            over_ici[j].wait_recv()
            passed[j].start()
    for mine, first, passed, _, from_sibling in plans:
        for cp in from_sibling:
            cp.wait_recv()
        for cp in first + passed:
            cp.wait_send()
        mine.wait()


def _gather_one(x_ref, slot, send_sem, recv_sem, local_sem):
    _gather_run([_gather_plan(x_ref, slot, send_sem, recv_sem, local_sem)])


def gather_hook(items):
    n = len(items)

    def plan(refs, send_sems, recv_sems):
        x, y, cc = _coords()

        def copy(i, k, px, py, pc, to):
            blk = refs[i].at[2 * px + py, pc]
            return pltpu.make_async_remote_copy(src_ref=blk, dst_ref=blk, send_sem=send_sems.at[i, k],
                                                recv_sem=recv_sems.at[i, k], device_id=to, device_id_type=MESH)

        chips = _other_chips(x, y)
        first = [copy(i, j, x, y, cc, (*chip, cc)) for i in range(n) for j, chip in enumerate(chips)]
        return copy, chips, first, (x, y, cc)

    def start(refs, new, sems):
        for cp in plan(refs, *sems)[2]:
            cp.start()

    def finish(refs, new, sems):
        copy, chips, first, (x, y, cc) = plan(refs, *sems)
        passed = []
        for j, chip in enumerate(chips):
            for i in range(n):
                copy(i, j, *chip, cc, (x, y, cc)).wait_recv()
                passed.append(copy(i, 3 + j, *chip, cc, (x, y, 1 - cc)))
                passed[-1].start()
        for j, chip in enumerate(chips):
            for i in range(n):
                copy(i, 3 + j, *chip, 1 - cc, (x, y, cc)).wait_recv()
        for cp in first + passed:
            cp.wait_send()

    return dict(arrays=list(items), new=[], start=start, finish=finish, in_place=True,
                sems=[pltpu.SemaphoreType.DMA((n, 6)), pltpu.SemaphoreType.DMA((n, 6))])


def hosted_call(body, hook, name, grid, in_specs, out_specs, out_shape, scratch_shapes, sem, args):
    single = not isinstance(out_shape, (list, tuple))
    out_specs_l = [out_specs] if single else list(out_specs)
    out_shape_l = [out_shape] if single else list(out_shape)
    if hook is None:
        res = pl.pallas_call(body, name=name, grid=grid, in_specs=list(in_specs), out_specs=out_specs, out_shape=out_shape,
                             scratch_shapes=list(scratch_shapes), compiler_params=_params(sem))(*args)
        return res, []
    items, new = hook["arrays"], hook["new"]
    k, kn, n_in, n_out, n_scr = len(items), len(new), len(in_specs), len(out_specs_l), len(scratch_shapes)
    ka = k if hook["in_place"] else 0

    def full(*refs):
        ins = refs[:n_in]
        base = n_in + k
        outs = refs[base:base + n_out]
        hrefs = refs[base + n_out:base + n_out + ka] if ka else refs[n_in:base]
        nrefs = refs[base + n_out + ka:base + n_out + ka + kn]
        scr = refs[base + n_out + ka + kn:base + n_out + ka + kn + n_scr]
        sems = refs[base + n_out + ka + kn + n_scr:]
        ids = [pl.program_id(d) for d in range(len(grid))]
        first = functools.reduce(jnp.logical_and, [i == 0 for i in ids])
        last = functools.reduce(jnp.logical_and, [i == g - 1 for i, g in zip(ids, grid)])

        @pl.when(first)
        def _():
            hook["start"](hrefs, nrefs, sems)

        body(*ins, *outs, *scr)

        @pl.when(last)
        def _():
            hook["finish"](hrefs, nrefs, sems)

    res = pl.pallas_call(
        full, name=name, grid=grid, in_specs=list(in_specs) + [ANY] * k, out_specs=out_specs_l + [ANY] * (ka + kn),
        out_shape=out_shape_l + [_sds(a.shape, a.dtype) for a in items[:ka]] + list(new),
        input_output_aliases={n_in + i: n_out + i for i in range(ka)},
        scratch_shapes=list(scratch_shapes) + hook["sems"],
        compiler_params=pltpu.CompilerParams(dimension_semantics=("arbitrary",) * len(grid),
                                             vmem_limit_bytes=VMEM_LIMIT, has_side_effects=True),
    )(*args, *items)
    return (res[0] if single else list(res[:n_out])), list(res[n_out:])


def comm_call(hook, name):
    k, kn = len(hook["arrays"]), len(hook["new"])
    ka = k if hook["in_place"] else 0

    def body(*refs):
        hrefs = refs[k:k + ka] if ka else refs[:k]
        hook["start"](hrefs, refs[k + ka:k + ka + kn], refs[k + ka + kn:])
        hook["finish"](hrefs, refs[k + ka:k + ka + kn], refs[k + ka + kn:])

    return list(pl.pallas_call(
        body, name=name, in_specs=[ANY] * k, out_specs=[ANY] * (ka + kn),
        out_shape=[_sds(a.shape, a.dtype) for a in hook["arrays"][:ka]] + list(hook["new"]),
        input_output_aliases={i: i for i in range(ka)}, scratch_shapes=hook["sems"], compiler_params=COMM,
    )(*hook["arrays"]))


def merge_hooks(hooks):
    hooks = [h for h in hooks if h is not None]
    if len(hooks) < 2:
        return hooks[0] if hooks else None

    def parts(refs, new, sems):
        out, a, b, c = [], 0, 0, 0
        for h in hooks:
            na, nn, ns = len(h["arrays"]), len(h["new"]), len(h["sems"])
            out.append((refs[a:a + na], new[b:b + nn], sems[c:c + ns]))
            a, b, c = a + na, b + nn, c + ns
        return out

    def start(refs, new, sems):
        for h, p in zip(hooks, parts(refs, new, sems)):
            h["start"](*p)

    def finish(refs, new, sems):
        for h, p in zip(hooks, parts(refs, new, sems)):
            h["finish"](*p)

    assert len({h["in_place"] for h in hooks}) == 1
    return dict(arrays=[a for h in hooks for a in h["arrays"]], new=[a for h in hooks for a in h["new"]],
                sems=[a for h in hooks for a in h["sems"]], start=start, finish=finish, in_place=hooks[0]["in_place"])


def split_carried(hooks, carried):
    hooks = [h for h in hooks if h is not None]
    off = sum(len(h["arrays"]) for h in hooks if h["in_place"])
    out = []
    for h in hooks:
        out.append(carried[off:off + len(h["new"])])
        off += len(h["new"])
    return out


def pair_swap_hook(units):
    n = len(units)

    def plan(refs, new, send_sems, recv_sems):
        x, y, cc = _coords()
        return [pltpu.make_async_remote_copy(src_ref=refs[i].at[:, 1 - cc], dst_ref=new[i], send_sem=send_sems.at[i],
                                             recv_sem=recv_sems.at[i], device_id=(x, y, 1 - cc), device_id_type=MESH)
                for i in range(n)]

    def start(refs, new, sems):
        for cp in plan(refs, new, *sems):
            cp.start()

    def finish(refs, new, sems):
        for cp in plan(refs, new, *sems):
            cp.wait()

    return dict(arrays=list(units), new=[_sds((u.shape[0],) + u.shape[2:], u.dtype) for u in units], start=start,
                finish=finish, in_place=False, sems=[pltpu.SemaphoreType.DMA((n,)), pltpu.SemaphoreType.DMA((n,))])


def chip_exchange_hook(units):
    n = len(units)

    def plan(refs, new, send_sems, recv_sems):
        x, y, cc = _coords()
        return [pltpu.make_async_remote_copy(
            src_ref=refs[i].at[2 * px + py], dst_ref=new[i].at[k], send_sem=send_sems.at[i, k],
            recv_sem=recv_sems.at[i, k], device_id=(px, py, cc), device_id_type=MESH)
            for i in range(n) for k, (px, py) in enumerate(_other_chips(x, y))]

    def start(refs, new, sems):
        for cp in plan(refs, new, *sems):
            cp.start()

    def finish(refs, new, sems):
        for cp in plan(refs, new, *sems):
            cp.wait()

    return dict(arrays=list(units), new=[_sds((3,) + u.shape[1:], u.dtype) for u in units], start=start,
                finish=finish, in_place=False, sems=[pltpu.SemaphoreType.DMA((n, 3)), pltpu.SemaphoreType.DMA((n, 3))])


def grad_half_swap(grads):
    n = len(grads)

    def body(*refs):
        outs, send_sems, recv_sems = refs[n:2 * n], refs[2 * n], refs[2 * n + 1]
        x, y, cc = _coords()
        cps = [pltpu.make_async_remote_copy(
            src_ref=outs[i].at[:, cc], dst_ref=outs[i].at[:, cc], send_sem=send_sems.at[i], recv_sem=recv_sems.at[i],
            device_id=(x, y, 1 - cc), device_id_type=MESH) for i in range(n)]
        for cp in cps:
            cp.start()
        for i, cp in enumerate(cps):
            cp.wait_send()
            pltpu.make_async_remote_copy(
                src_ref=outs[i].at[:, 1 - cc], dst_ref=outs[i].at[:, 1 - cc], send_sem=send_sems.at[i],
                recv_sem=recv_sems.at[i], device_id=(x, y, 1 - cc), device_id_type=MESH).wait_recv()

    return pl.pallas_call(
        body, name="grad_half_swap", in_specs=[ANY] * n, out_specs=[ANY] * n,
        out_shape=[_sds(g.shape, g.dtype) for g in grads], input_output_aliases={i: i for i in range(n)},
        scratch_shapes=[pltpu.SemaphoreType.DMA((n,)), pltpu.SemaphoreType.DMA((n,))], compiler_params=COMM,
    )(*grads)


N_CHIPS = 4
PACK_COLS = 1024
BIG = ("ssd_w_in", "ssd_w_out", "sb_w_qkv", "sb_w_out", "ffn_w_in", "ffn_w_out")
CONVW = ("ssd_conv_w", "ffn_conv_w")
COL_SHARDED = ("ssd_w_in", "sb_w_qkv", "ffn_w_in", "ssd_conv_w", "ffn_conv_w")
SMALL = ("mix_norm", "ffn_norm", "final_norm", "ssd_conv_b", "ssd_dt_bias", "ssd_a_log", "ssd_d", "ssd_norm", "ffn_conv_b")
WEIGHTS = ("mix_norm", "ffn_norm", "final_norm", "ssd_w_in", "ssd_conv_w", "ssd_conv_b", "ssd_dt_bias", "ssd_a_log",
           "ssd_d", "ssd_norm", "ssd_w_out", "sb_w_qkv", "sb_w_out", "ffn_w_in", "ffn_conv_w", "ffn_conv_b", "ffn_w_out")


def _to_rows(flat, multiple):
    rows = -(-flat.shape[-1] // PACK_COLS)
    rows = -(-rows // multiple) * multiple
    pad = rows * PACK_COLS - flat.shape[-1]
    return jnp.pad(flat, [(0, pad)]).reshape(rows, PACK_COLS)


def _unshard(name, stacked):
    l, n, a, b = stacked.shape
    if name in COL_SHARDED:
        return jnp.transpose(stacked, (0, 2, 1, 3)).reshape(l, a, n * b)
    return stacked.reshape(l, n * a, b)


def _gather_conv_weights(w):
    flat = jnp.concatenate([w[n].reshape(-1) for n in CONVW])
    rows = _to_rows(flat, 16)
    got = all_gather_8(rows.reshape(2, rows.shape[0] // 2, PACK_COLS), "gather_conv_weights").reshape(N_CHIPS, -1)
    out, off = {}, 0
    for n in CONVW:
        l, a, b = w[n].shape
        out[n] = _unshard(n, jnp.moveaxis(got[:, off:off + w[n].size].reshape(N_CHIPS, l, a, b), 0, 1))
        off += w[n].size
    return out


def _finish_big_grads(pair_sums, from_chips, layout):
    cc = lax.axis_index("c").astype(jnp.int32)
    chip = (2 * lax.axis_index("x") + lax.axis_index("y")).astype(jnp.int32)
    where = jnp.stack([chip, cc])
    nlayers = [1 + max(l for k, l in layout if k == wi) for wi in range(1 + max(k for k, _ in layout))]
    grads = [None] * len(nlayers)
    for (wi, l), p, r in zip(layout, pair_sums, from_chips):
        grads[wi] = chip_sum(p, where, r, l, nlayers[wi], grads[wi])
    return grad_half_swap(grads)


def kernel(x, mix_norm, ffn_norm, final_norm, ssd_w_in, ssd_conv_w, ssd_conv_b, ssd_dt_bias, ssd_a_log, ssd_d, ssd_norm, ssd_w_out, sb_w_qkv, sb_w_out, ffn_w_in, ffn_conv_w, ffn_conv_b, ffn_w_out, loss_target, m_mix_norm, m_ffn_norm, m_final_norm, m_ssd_w_in, m_ssd_conv_w, m_ssd_conv_b, m_ssd_dt_bias, m_ssd_a_log, m_ssd_d, m_ssd_norm, m_ssd_w_out, m_sb_w_qkv, m_sb_w_out, m_ffn_w_in, m_ffn_conv_w, m_ffn_conv_b, m_ffn_w_out, v_mix_norm, v_ffn_norm, v_final_norm, v_ssd_w_in, v_ssd_conv_w, v_ssd_conv_b, v_ssd_dt_bias, v_ssd_a_log, v_ssd_d, v_ssd_norm, v_ssd_w_out, v_sb_w_qkv, v_sb_w_out, v_ffn_w_in, v_ffn_conv_w, v_ffn_conv_b, v_ffn_w_out):
    given = dict(locals())
    w = {n: given[n] for n in WEIGHTS}
    mom = {n: given["m_" + n] for n in WEIGHTS}
    var = {n: given["v_" + n] for n in WEIGHTS}
    chip = 2 * lax.axis_index("x") + lax.axis_index("y")

    chip1 = chip.reshape(1).astype(jnp.int32)
    fw = _gather_conv_weights(w)
    row = lambda v: v.reshape(1, -1)

    def placed(n, l):
        _, a, b = w[n].shape
        return place_cast(w[n], l, chip1).reshape(N_CHIPS, 2, a // 2, b)

    def mixer_items(i):
        return [(n, i // 2) for n in (("ssd_w_in", "ssd_w_out") if i % 2 == 0 else ("sb_w_qkv", "sb_w_out"))]

    def ffn_items(i):
        return [("ffn_w_in", i), ("ffn_w_out", i)]

    def hook_for(items):
        return gather_hook([placed(n, l) for n, l in items]) if items else None

    lw = {}

    def arrived(items, arrays):
        for (n, l), arr in zip(items, arrays):
            g4 = arr.reshape(N_CHIPS, -1, arr.shape[-1])
            if n in COL_SHARDED:
                full = jnp.transpose(g4, (1, 0, 2)).reshape(g4.shape[1], -1)
            else:
                full = g4.reshape(-1, g4.shape[2])
            if n == "ssd_w_in":
                full = jnp.pad(full, ((0, 0), (0, SSD_IN_PAD - SSD_IN_DIM)))
            lw[(n, l)] = full

    first_items = [("ssd_w_in", 0)]
    carry = {
        (0, "mm_in"): [("ssd_w_out", 0), ("ffn_w_in", 0)],
        (0, "scan"): [("ffn_w_out", 0)] + mixer_items(1),
        (0, "ffn_in"): ffn_items(1),
        (1, "sb"): mixer_items(2) + ffn_items(2),
        (2, "mm_in"): mixer_items(3),
        (2, "scan"): ffn_items(3),
    }
    arrived(first_items, comm_call(hook_for(first_items), "gather_first"))

    def carrying(i, slot, call):
        items = carry.get((i, slot), [])
        if not items:
            return call(None)
        out, got = call(hook_for(items))
        arrived(items, got)
        return out

    xcur = x[0]
    saved = []
    for i in range(DEPTH):
        j = i // 2
        h, r = rms_fwd(xcur, row(mix_norm[i]))
        if i % 2 == 0:
            proj = carrying(i, "mm_in", lambda hk: mm(h, lw[("ssd_w_in", j)], tm=2048, tn=896, name="mm_ssd_in", hook=hk))
            items = carry.get((i, "scan"), [])
            y3, core, got = ssd_core_fwd(proj, fw["ssd_conv_w"][j], row(ssd_conv_b[j]), ssd_dt_bias[j], ssd_a_log[j],
                                         ssd_d[j], row(ssd_norm[j]), hook_for(items))
            arrived(items, got)
            x1 = mm(y3, lw[("ssd_w_out", j)], res=xcur, name="mm_ssd_out")
            mix = (proj, y3, core)
        else:
            qkv = mm(h, lw[("sb_w_qkv", j)], out_dtype=MXU_DTYPE, tm=2048, name="mm_sb_qkv")
            o = carrying(i, "sb", lambda hk: sb_fwd(qkv, hk))
            x1 = mm(o, lw[("sb_w_out", j)], res=xcur, name="mm_sb_out")
            mix = (qkv, o)
        h2, r2 = rms_fwd(x1, row(ffn_norm[i]))
        u0 = carrying(i, "ffn_in", lambda hk: mm(h2, lw[("ffn_w_in", i)], tm=2048, name="mm_ffn_in", hook=hk))
        a = ffn_mid_fwd(u0, fw["ffn_conv_w"][i], row(ffn_conv_b[i]))
        x2 = mm(a, lw[("ffn_w_out", i)], res=x1, name="mm_ffn_out")
        saved.append((xcur, h, r, mix, x1, h2, r2, u0, a))
        xcur = x2
    loss_part, dx, dxm, d_final = loss_head(xcur, row(final_norm), loss_target[0])

    gl = {n: [None] * w[n].shape[0] for n in WEIGHTS if n != "final_norm"}
    units = {n: [None] * w[n].shape[0] for n in BIG}

    def unit_of(g4):
        return g4.reshape(N_CHIPS, 2, g4.shape[1] // 2, g4.shape[2])

    where = jnp.stack([chip, lax.axis_index("c")]).astype(jnp.int32)
    pair_f32, wire, from_chips = {}, {}, {}

    def pair_sums(keys, swapped):
        for (n, l), got in zip(keys, swapped):
            pair_f32[(n, l)], wire[(n, l)] = pair_sum(units[n][l], got, where)

    for i in reversed(range(DEPTH)):
        j = i // 2
        x0, h, r, mix, x1, h2, r2, u0, a = saved[i]
        units["ffn_w_out"][i] = unit_of(mm(a, dxm, "tn", tm=1408, name="mm_d_ffn_out").reshape(N_CHIPS, -1, D_MODEL))
        da = mm(dxm, lw[("ffn_w_out", i)], "nt", tn=1408, name="mm_da_ffn")
        dug, duu, gl["ffn_conv_w"][i], dcb = ffn_mid_bwd(u0, da, fw["ffn_conv_w"][i], row(ffn_conv_b[i]))
        gl["ffn_conv_b"][i] = dcb[0]
        du0 = jnp.concatenate([dug, duu], axis=1)
        units["ffn_w_in"][i] = unit_of(mm(h2, du0, "tn", tn=1408, tm=512, n_split=N_CHIPS, name="mm_d_ffn_in"))
        keys_f = ffn_items(i)
        swap = pair_swap_hook([units[n][l] for n, l in keys_f])
        dh2, carried = mm(du0, lw[("ffn_w_in", i)], "nt", name="mm_dh_ffn", hook=swap)
        pair_sums(keys_f, split_carried([swap], carried)[0])
        dx1, dx1m, dg = rms_bwd(x1, r2, row(ffn_norm[i]), dh2, dx)
        gl["ffn_norm"][i] = dg[0]
        keys_up = mixer_items(i + 1) if i + 1 < DEPTH else []
        exchanges = [chip_exchange_hook([wire[k] for k in keys_f]),
                     chip_exchange_hook([wire[k] for k in keys_up]) if keys_up else None]
        if i % 2 == 0:
            proj, y3, core = mix
            units["ssd_w_out"][j] = unit_of(mm(y3, dx1m, "tn", name="mm_d_ssd_out").reshape(N_CHIPS, -1, D_MODEL))
            dy3 = mm(dx1m, lw[("ssd_w_out", j)], "nt", name="mm_dy3_ssd")
            (dproj, gl["ssd_conv_w"][j], dcb, gl["ssd_dt_bias"][j], gl["ssd_a_log"][j], gl["ssd_d"][j], dnorm,
             carried) = ssd_core_bwd(proj, fw["ssd_conv_w"][j], row(ssd_conv_b[j]), row(ssd_norm[j]), core, dy3,
                                     merge_hooks(exchanges))
            gl["ssd_conv_b"][j] = dcb[0]
            gl["ssd_norm"][j] = dnorm[0]
            dw_in = mm(h, dproj, "tn", tn=896, name="mm_d_ssd_in")[:, :SSD_IN_DIM]
            units["ssd_w_in"][j] = unit_of(jnp.transpose(dw_in.reshape(D_MODEL, N_CHIPS, -1), (1, 0, 2)))
            dmix, w_in, dh_name = dproj, lw[("ssd_w_in", j)], "mm_dh_ssd"
        else:
            qkv, o = mix
            units["sb_w_out"][j] = unit_of(mm(o, dx1m, "tn", name="mm_d_sb_out").reshape(N_CHIPS, -1, D_MODEL))
            do = mm(dx1m, lw[("sb_w_out", j)], "nt", out_dtype=MXU_DTYPE, name="mm_do_sb")
            dqkv, carried = sb_bwd(qkv, do, merge_hooks(exchanges))
            units["sb_w_qkv"][j] = unit_of(mm(h, dqkv, "tn", tn=768, n_split=N_CHIPS, name="mm_d_sb_qkv"))
            dmix, w_in, dh_name = dqkv, lw[("sb_w_qkv", j)], "mm_dh_sb"
        got = split_carried(exchanges, carried)
        from_chips.update(zip(keys_f, got[0]))
        if keys_up:
            from_chips.update(zip(keys_up, got[1]))
        keys_m = mixer_items(i)
        swap = pair_swap_hook([units[n][l] for n, l in keys_m])
        dh, carried = mm(dmix, w_in, "nt", name=dh_name, hook=swap)
        pair_sums(keys_m, split_carried([swap], carried)[0])
        dx, dxm, dg = rms_bwd(x0, r, row(mix_norm[i]), dh, dx1)
        gl["mix_norm"][i] = dg[0]
    last = chip_exchange_hook([wire[k] for k in mixer_items(0)])
    from_chips.update(zip(mixer_items(0), comm_call(last, "grad_exchange_last")))

    layout = [(k, l) for k, n in enumerate(BIG) for l in range(w[n].shape[0])]
    reduced = _finish_big_grads([pair_f32[(BIG[k], l)] for k, l in layout], [from_chips[(BIG[k], l)] for k, l in layout],
                                layout)
    g, delta, new_m, new_v = {}, {}, {}, {}
    two_d = lambda t: t.reshape(-1, t.shape[-1])
    for n, red in zip(BIG, reduced):
        g[n] = red.reshape(w[n].shape)
        d2, m2, v2 = adamw(two_d(w[n]), two_d(g[n]), two_d(mom[n]), two_d(var[n]), name="adamw_" + n)
        delta[n], new_m[n], new_v[n] = d2.reshape(w[n].shape), m2.reshape(w[n].shape), v2.reshape(w[n].shape)

    small_g = {n: jnp.stack(gl[n]) for n in SMALL + CONVW if n != "final_norm"}
    small_g["final_norm"] = d_final[0]
    zeros_of = lambda n: jnp.zeros((small_g[n].size,), F32)

    def small_pack(d, extra):
        parts = [d[n].reshape(-1) for n in SMALL] + [extra]
        parts += [(d[n].reshape(-1) if d is small_g else zeros_of(n)) for n in CONVW]
        return _to_rows(jnp.concatenate(parts), 16)

    part = small_pack(small_g, loss_part[0, 0:1])
    parts = all_gather_8(jnp.stack([part, part]), "gather_small_grads")
    zero = jnp.zeros((1,), F32)
    gs, ds, ms, vs = adamw_small(small_pack(w, zero), parts, small_pack(mom, zero), small_pack(var, zero))
    gs_flat = gs.reshape(-1)
    off = 0
    for n in SMALL:
        size = w[n].size
        for dst, src in ((g, gs), (delta, ds), (new_m, ms), (new_v, vs)):
            dst[n] = src.reshape(-1)[off:off + size].reshape(w[n].shape)
        off += size
    loss = gs_flat[off]
    off += 1
    for n in CONVW:
        size = small_g[n].size
        b = w[n].shape[-1]
        g[n] = lax.dynamic_slice_in_dim(gs_flat[off:off + size].reshape(small_g[n].shape), chip * b, b, axis=2)
        d2, m2, v2 = adamw(two_d(w[n]), two_d(g[n]), two_d(mom[n]), two_d(var[n]), name="adamw_" + n)
        delta[n], new_m[n], new_v[n] = d2.reshape(w[n].shape), m2.reshape(w[n].shape), v2.reshape(w[n].shape)
        off += size

    return (loss, dx[None], *[g[n] for n in WEIGHTS], *[delta[n] for n in WEIGHTS],
            *[new_m[n] for n in WEIGHTS], *[new_v[n] for n in WEIGHTS])
```

```python
import functools

import jax
import jax.numpy as jnp
from jax import lax
from jax.experimental import pallas as pl
from jax.experimental.pallas import tpu as pltpu

F32 = jnp.float32
MXU_DTYPE = jnp.bfloat16
HIGHEST = lax.Precision.HIGHEST

D_MODEL = 1024
DEPTH = 4
NORM_EPS = 1e-6
SSD_D_INNER = 2048
SSD_HEAD_DIM = 64
SSD_HEADS = 32
SSD_GROUPS = 8
SSD_STATE = 128
SSD_CONV = 4
SSD_CHUNK = 128
SSD_CONV_DIM = 4096
SSD_IN_DIM = 6176
SSD_IN_PAD = 6272
SB_HEADS = 16
SB_HEAD_DIM = 64
FFN_D_FF = 2816
FFN_CONV = 3
ADAM_LR, ADAM_B1, ADAM_B2, ADAM_EPS, ADAM_WD, ADAM_STEP = 0.001, 0.9, 0.999, 1e-08, 0.01, 10

LANES = 128
SUBLANES = 8
VMEM_LIMIT = 56 * 1024 * 1024
MESH = pl.DeviceIdType.MESH


def _params(sem=None):
    return pltpu.CompilerParams(dimension_semantics=sem, vmem_limit_bytes=VMEM_LIMIT)


def _sds(shape, dtype):
    return jax.ShapeDtypeStruct(shape, dtype)


def _dot(a, b, dims=(((1,), (0,)), ((), ())), precision=None):
    return lax.dot_general(a, b, dims, precision=precision, preferred_element_type=F32)


_NN = (((1,), (0,)), ((), ()))
_NT = (((1,), (1,)), ((), ()))
_TN = (((0,), (0,)), ((), ()))


def _mx(a):
    return a.astype(MXU_DTYPE)


def _silu(x):
    return x * (1.0 / (1.0 + jnp.exp(-x)))


def _silu_and_grad(x):
    s = 1.0 / (1.0 + jnp.exp(-x))
    return x * s, s * (1.0 + x * (1.0 - s))


def _pick(n, cands):
    for c in cands:
        if n % c == 0:
            return c
    return n


def rms_fwd(x, g):
    s, d = x.shape
    ts = _pick(s, (512, 256, 128))

    def body(x_ref, g_ref, h_ref, r_ref):
        xv = x_ref[...]
        r = lax.rsqrt(jnp.mean(xv * xv, axis=-1, keepdims=True) + NORM_EPS)
        h_ref[...] = (xv * r * g_ref[...]).astype(h_ref.dtype)
        r_ref[...] = r

    return pl.pallas_call(
        body, name="rms_fwd", grid=(s // ts,),
        in_specs=[pl.BlockSpec((ts, d), lambda i: (i, 0)), pl.BlockSpec((1, d), lambda i: (0, 0))],
        out_specs=[pl.BlockSpec((ts, d), lambda i: (i, 0)), pl.BlockSpec((ts, 1), lambda i: (i, 0))],
        out_shape=[_sds((s, d), MXU_DTYPE), _sds((s, 1), F32)],
        compiler_params=_params(("parallel",)),
    )(x, g)


def rms_bwd(x, r, g, dh, dres):
    s, d = x.shape
    ts = _pick(s, (512, 256, 128))

    def body(x_ref, r_ref, g_ref, dh_ref, dres_ref, dx_ref, dxm_ref, dg_ref):
        xh = x_ref[...] * r_ref[...]
        dhv = dh_ref[...]
        dxh = dhv * g_ref[...]
        dx = dres_ref[...] + r_ref[...] * (dxh - xh * jnp.mean(dxh * xh, axis=-1, keepdims=True))
        dx_ref[...] = dx
        dxm_ref[...] = dx.astype(dxm_ref.dtype)
        part = jnp.sum(dhv * xh, axis=0, keepdims=True)

        @pl.when(pl.program_id(0) == 0)
        def _():
            dg_ref[...] = part

        @pl.when(pl.program_id(0) != 0)
        def _():
            dg_ref[...] += part

    row = pl.BlockSpec((ts, d), lambda i: (i, 0))
    return pl.pallas_call(
        body, name="rms_bwd", grid=(s // ts,),
        in_specs=[row, pl.BlockSpec((ts, 1), lambda i: (i, 0)), pl.BlockSpec((1, d), lambda i: (0, 0)), row, row],
        out_specs=[row, row, pl.BlockSpec((1, d), lambda i: (0, 0))],
        out_shape=[_sds((s, d), F32), _sds((s, d), MXU_DTYPE), _sds((1, d), F32)],
        compiler_params=_params(("arbitrary",)),
    )(x, r, g, dh, dres)


def mm(a, b, mode="nn", res=None, out_dtype=F32, tm=None, tn=None, n_split=1, name="mm", hook=None):
    halves_a = mode == "nt" and a.ndim == 3
    halves_b = mode == "tn" and b.ndim == 3
    if halves_a:
        a_shape = (a.shape[1], 2 * a.shape[2])
    else:
        a_shape = a.shape
    b_shape = (b.shape[1], 2 * b.shape[2]) if halves_b else b.shape
    if mode == "nn":
        (m, k), (_, n) = a_shape, b_shape
    elif mode == "nt":
        (m, k), (n, _) = a_shape, b_shape
    else:
        (k, m), (_, n) = a_shape, b_shape
    tm = min(tm, m) if tm else _pick(m, (1024, 512, 256, 128))
    tn = min(tn, n) if tn else _pick(n, (512, 896, 256, 128))
    assert m % tm == 0 and n % tn == 0, (m, tm, n, tn)
    dims = {"nn": _NN, "nt": _NT, "tn": _TN}[mode]

    def body(*refs):
        a_ref, b_ref = refs[0], refs[1]
        o_ref = refs[-1]
        if halves_a:
            kh = k // 2
            acc = _dot(_mx(a_ref[0]), _mx(b_ref[:, :kh]), dims) + _dot(_mx(a_ref[1]), _mx(b_ref[:, kh:]), dims)
        else:
            acc = _dot(_mx(a_ref[...]), _mx(b_ref[...]), dims)
        if res is not None:
            acc = acc + refs[2][...]
        o_ref[...] = acc.astype(o_ref.dtype)

    a_spec = pl.BlockSpec((k, tm), lambda i, j: (0, i)) if mode == "tn" else pl.BlockSpec((tm, k), lambda i, j: (i, 0))
    b_spec = pl.BlockSpec((tn, k), lambda i, j: (j, 0)) if mode == "nt" else pl.BlockSpec((k, tn), lambda i, j: (0, j))
    if halves_a:
        a_spec = pl.BlockSpec((2, tm, k // 2), lambda i, j: (0, i, 0))
    if halves_b:
        per_half = n // 2 // tn
        assert per_half * tn * 2 == n
        b_spec = pl.BlockSpec((None, k, tn), lambda i, j: (j // per_half, 0, j % per_half))
    o_spec = pl.BlockSpec((tm, tn), lambda i, j: (i, j))
    ins, specs = [a, b], [a_spec, b_spec]
    if res is not None:
        ins.append(res)
        specs.append(o_spec)
    out_shape = _sds((m, n), out_dtype)
    if n_split > 1:
        per = n // n_split // tn
        o_spec = pl.BlockSpec((None, tm, tn), lambda i, j: (j // per, i, j % per))
        out_shape = _sds((n_split, m, n // n_split), out_dtype)
    out, carried = hosted_call(body, hook, name, (m // tm, n // tn), specs, o_spec, out_shape, [],
                               ("parallel", "parallel"), ins)
    return out if hook is None else (out, carried)


CONV_ROWS = 256
CONV_COLS = 128


def _row_iota8(cols):
    return lax.broadcasted_iota(jnp.int32, (SUBLANES, cols), 0)


def _shift_down(cur, prev8, k):
    if k == 0:
        return cur
    rolled = pltpu.roll(cur, k, 0)
    head = jnp.where(_row_iota8(cur.shape[1]) < k, pltpu.roll(prev8, k, 0), rolled[0:SUBLANES])
    return jnp.concatenate([head, rolled[SUBLANES:]], axis=0)


def _shift_up(cur, next8, k):
    if k == 0:
        return cur
    n = cur.shape[0]
    rolled = pltpu.roll(cur, n - k, 0)
    tail = jnp.where(_row_iota8(cur.shape[1]) >= SUBLANES - k, pltpu.roll(next8, SUBLANES - k, 0), rolled[n - SUBLANES:])
    return jnp.concatenate([rolled[:n - SUBLANES], tail], axis=0)


def _load_prev8(ref, i, rows):
    start = pl.multiple_of(jnp.maximum(i * rows - SUBLANES, 0), SUBLANES)
    p = ref[pl.ds(start, SUBLANES), :]
    return jnp.where(i > 0, p, jnp.zeros_like(p))


def _conv_rows(ref, w_ref, b_ref, i, rows, width):
    cur = ref[pl.ds(pl.multiple_of(i * rows, rows), rows), :]
    prev8 = _load_prev8(ref, i, rows)
    shifted = [_shift_down(cur, prev8, k) for k in range(width)]
    acc = b_ref[...] + w_ref[width - 1:width, :] * shifted[0]
    for k in range(1, width):
        acc = acc + w_ref[width - 1 - k:width - k, :] * shifted[k]
    return acc, shifted


def _conv_bwd_rows(du, next8, w_ref, width):
    acc = w_ref[width - 1:width, :] * du
    for k in range(1, width):
        acc = acc + w_ref[width - 1 - k:width - k, :] * _shift_up(du, next8, k)
    return acc


def ffn_mid_fwd(u0, cw, cb):
    s, f2 = u0.shape
    f = f2 // 2
    nt = f // CONV_COLS
    rows = min(CONV_ROWS, s)

    def body(ug_ref, uu_ref, wg_ref, wu_ref, bg_ref, bu_ref, a_ref):
        def step(i, carry):
            g, _ = _conv_rows(ug_ref, wg_ref, bg_ref, i, rows, FFN_CONV)
            u, _ = _conv_rows(uu_ref, wu_ref, bu_ref, i, rows, FFN_CONV)
            a_ref[pl.ds(pl.multiple_of(i * rows, rows), rows), :] = (_silu(g) * u).astype(a_ref.dtype)
            return carry

        lax.fori_loop(0, s // rows, step, 0)

    col = lambda off: pl.BlockSpec((s, CONV_COLS), lambda j: (0, j + off))
    wsp = lambda r, off: pl.BlockSpec((r, CONV_COLS), lambda j: (0, j + off))
    return pl.pallas_call(
        body, name="ffn_mid_fwd", grid=(nt,),
        in_specs=[col(0), col(nt), wsp(FFN_CONV, 0), wsp(FFN_CONV, nt), wsp(1, 0), wsp(1, nt)],
        out_specs=pl.BlockSpec((s, CONV_COLS), lambda j: (0, j)),
        out_shape=_sds((s, f), MXU_DTYPE), compiler_params=_params(("parallel",)),
    )(u0, u0, cw, cw, cb, cb)


def ffn_mid_bwd(u0, da, cw, cb):
    s, f2 = u0.shape
    f = f2 // 2
    nt = f // CONV_COLS
    rows = min(CONV_ROWS, s)
    nsteps = s // rows
    w = FFN_CONV

    def body(ug_ref, uu_ref, da_ref, wg_ref, wu_ref, bg_ref, bu_ref,
             du0_ref, dwg_ref, dwu_ref, dbg_ref, dbu_ref):
        zero8 = jnp.zeros((SUBLANES, CONV_COLS), F32)
        zrow = jnp.zeros((1, CONV_COLS), F32)

        def step(it, carry):
            ng, nu, accs = carry
            i = nsteps - 1 - it
            r0 = pl.multiple_of(i * rows, rows)
            g, sg = _conv_rows(ug_ref, wg_ref, bg_ref, i, rows, w)
            u, su = _conv_rows(uu_ref, wu_ref, bu_ref, i, rows, w)
            dav = da_ref[pl.ds(r0, rows), :]
            sg_val, sg_grad = _silu_and_grad(g)
            dg = dav * u * sg_grad
            du = dav * sg_val
            du0_ref[0, pl.ds(r0, rows), :] = _conv_bwd_rows(dg, ng, wg_ref, w).astype(du0_ref.dtype)
            du0_ref[1, pl.ds(r0, rows), :] = _conv_bwd_rows(du, nu, wu_ref, w).astype(du0_ref.dtype)
            new = []
            for j in range(w):
                new.append(accs[j] + jnp.sum(dg * sg[w - 1 - j], axis=0, keepdims=True))
            for j in range(w):
                new.append(accs[w + j] + jnp.sum(du * su[w - 1 - j], axis=0, keepdims=True))
            new.append(accs[2 * w] + jnp.sum(dg, axis=0, keepdims=True))
            new.append(accs[2 * w + 1] + jnp.sum(du, axis=0, keepdims=True))
            return dg[0:SUBLANES], du[0:SUBLANES], tuple(new)

        _, _, accs = lax.fori_loop(0, nsteps, step, (zero8, zero8, tuple([zrow] * (2 * w + 2))))
        dwg_ref[...] = jnp.concatenate(accs[0:w], axis=0)
        dwu_ref[...] = jnp.concatenate(accs[w:2 * w], axis=0)
        dbg_ref[...] = accs[2 * w]
        dbu_ref[...] = accs[2 * w + 1]

    col = lambda off: pl.BlockSpec((s, CONV_COLS), lambda j: (0, j + off))
    wsp = lambda r, off: pl.BlockSpec((r, CONV_COLS), lambda j: (0, j + off))
    outs = pl.pallas_call(
        body, name="ffn_mid_bwd", grid=(nt,),
        in_specs=[col(0), col(nt), col(0), wsp(w, 0), wsp(w, nt), wsp(1, 0), wsp(1, nt)],
        out_specs=[pl.BlockSpec((2, s, CONV_COLS), lambda j: (0, 0, j)), wsp(w, 0), wsp(w, 0), wsp(1, 0), wsp(1, 0)],
        out_shape=[_sds((2, s, f), MXU_DTYPE), _sds((w, f), F32), _sds((w, f), F32), _sds((1, f), F32), _sds((1, f), F32)],
        compiler_params=_params(("parallel",)),
    )(u0, u0, da, cw, cw, cb, cb)
    du0, dwg, dwu, dbg, dbu = outs
    return du0, jnp.concatenate([dwg, dwu], axis=1), jnp.concatenate([dbg, dbu], axis=1)


SB_BLOCK = 128
SB_DEAD = 110.0
SB_HEADS_PER_STEP = 4


def _split_hi_lo(x):
    hi = x.astype(MXU_DTYPE)
    lo = (x - hi.astype(F32)).astype(MXU_DTYPE)
    return hi, lo


def _dot_exact01(x, tri):
    hi, lo = _split_hi_lo(x)
    return _dot(hi, tri) + _dot(lo, tri)


def _stack_heads(pair, lane_lo):
    zero = jnp.zeros_like(pair)
    return jnp.concatenate([jnp.where(lane_lo, pair, zero), jnp.where(lane_lo, zero, pair)], axis=0)


def _unstack_heads(tall, lane_lo):
    n = tall.shape[0] // 2
    return jnp.where(lane_lo, tall[:n], tall[n:])


def _sb_logits(stacked_q, k_ref, k0, pair_cols, blk):
    z = [_dot(sq, k_ref[pl.ds(k0, blk), cols], _NT) for sq, cols in zip(stacked_q, pair_cols)]
    return jnp.concatenate(z, axis=0) * (SB_HEAD_DIM ** -0.5)


def _sb_logs(z, blk, diagonal):
    t = jnp.log(1.0 + jnp.exp(-jnp.abs(z)))
    lb = jnp.minimum(z, 0.0) - t
    lf = jnp.minimum(-z, 0.0) - t
    if not diagonal:
        return lb, lf, None
    strict = lax.broadcasted_iota(jnp.int32, z.shape, 1) < (lax.broadcasted_iota(jnp.int32, z.shape, 0) & (blk - 1))
    return lb, jnp.where(strict, lf, 0.0), strict


def _keep(strict, x):
    return x if strict is None else jnp.where(strict, x, 0.0)


def _tri(blk, upper):
    r = lax.broadcasted_iota(jnp.int32, (blk, blk), 0)
    c = lax.broadcasted_iota(jnp.int32, (blk, blk), 1)
    return jnp.where((r > c) if upper else (r < c), 1.0, 0.0).astype(MXU_DTYPE)


def _tri_sum(x, tri2):
    hi, lo = _split_hi_lo(x)
    return _dot(jnp.concatenate([hi, lo], axis=1), tri2)


def sb_fwd(qkv, hook=None):
    s = qkv.shape[0]
    blk = min(SB_BLOCK, s)
    nblk = s // blk
    nh = SB_HEADS_PER_STEP
    nstep = SB_HEADS // nh
    dh = SB_HEAD_DIM

    def body(q_ref, k_ref, v_ref, o_ref):
        suffix_tri2 = jnp.concatenate([_tri(blk, True)] * 2, axis=0)
        lane_lo = lax.broadcasted_iota(jnp.int32, (1, LANES), 1) < dh
        pairs = [slice(p * LANES, (p + 1) * LANES) for p in range(nh // 2)]

        def qstep(qi, carry):
            q0 = pl.multiple_of(qi * blk, blk)
            qst = [_stack_heads(q_ref[pl.ds(q0, blk), cols], lane_lo) for cols in pairs]

            def tile(kb, run, accs, diagonal):
                k0 = pl.multiple_of(kb * blk, blk)
                lb, lf, strict = _sb_logs(_sb_logits(qst, k_ref, k0, pairs, blk), blk, diagonal)
                sloc = _tri_sum(lf, suffix_tri2)
                a = _mx(_keep(strict, jnp.exp(lb + sloc + run)))
                accs = tuple(
                    acc + _unstack_heads(_dot(a[2 * blk * p:2 * blk * (p + 1)], v_ref[pl.ds(k0, blk), cols]), lane_lo)
                    for p, (acc, cols) in enumerate(zip(accs, pairs)))
                run = run + sloc[:, 0:1] + lf[:, 0:1]
                return run, accs, jnp.max(run) > -SB_DEAD

            def kstep(st):
                it, run, accs, _ = st
                return (it + 1, *tile(qi - it, run, accs, False))

            first = tile(qi, jnp.zeros((nh * blk, 1), F32), tuple([jnp.zeros((blk, LANES), F32)] * len(pairs)), True)
            _, _, accs, _ = lax.while_loop(lambda st: jnp.logical_and(st[0] <= qi, st[3]), kstep, (jnp.int32(1), *first))
            for acc, cols in zip(accs, pairs):
                o_ref[pl.ds(q0, blk), cols] = acc.astype(o_ref.dtype)
            return carry

        lax.fori_loop(0, nblk, qstep, 0)

    col = lambda off: pl.BlockSpec((s, nh * dh), lambda p: (0, p + off))
    out, carried = hosted_call(body, hook, "sb_fwd", (nstep,), [col(0), col(nstep), col(2 * nstep)], col(0),
                               _sds((s, D_MODEL), MXU_DTYPE), [], ("parallel",), (qkv, qkv, qkv))
    return out if hook is None else (out, carried)


def sb_bwd(qkv, do, hook=None):
    s = qkv.shape[0]
    blk = min(SB_BLOCK, s)
    nblk = s // blk
    nh = SB_HEADS_PER_STEP
    nstep = SB_HEADS // nh
    dh = SB_HEAD_DIM

    def body(q_ref, k_ref, v_ref, do_ref, dq_ref, dk_ref, dv_ref, dk_acc, dv_acc, run_ref):
        suffix_tri2 = jnp.concatenate([_tri(blk, True)] * 2, axis=0)
        prefix_tri2 = jnp.concatenate([_tri(blk, False)] * 2, axis=0)
        dk_acc[...] = jnp.zeros_like(dk_acc)
        dv_acc[...] = jnp.zeros_like(dv_acc)
        lane_lo = lax.broadcasted_iota(jnp.int32, (1, LANES), 1) < dh
        pairs = [slice(p * LANES, (p + 1) * LANES) for p in range(nh // 2)]

        def qstep(qi, carry):
            q0 = pl.multiple_of(qi * blk, blk)
            qst = [_stack_heads(q_ref[pl.ds(q0, blk), cols], lane_lo) for cols in pairs]
            dost = [_stack_heads(do_ref[pl.ds(q0, blk), cols], lane_lo) for cols in pairs]

            def enter(kb, run, diagonal):
                run_ref[kb] = run
                _, lf, _ = _sb_logs(_sb_logits(qst, k_ref, pl.multiple_of(kb * blk, blk), pairs, blk), blk, diagonal)
                run = run + jnp.sum(lf, axis=1, keepdims=True)
                return run, jnp.max(run) > -SB_DEAD

            def sweep1(st):
                it, run, _ = st
                return (it + 1, *enter(qi - it, run, False))

            nlive, _, _ = lax.while_loop(lambda st: jnp.logical_and(st[0] <= qi, st[2]), sweep1,
                                         (jnp.int32(1), *enter(qi, jnp.zeros((nh * blk, 1), F32), True)))

            def tile(kb, pg, dqs, diagonal):
                k0 = pl.multiple_of(kb * blk, blk)
                lb, lf, strict = _sb_logs(_sb_logits(qst, k_ref, k0, pairs, blk), blk, diagonal)
                sloc = _tri_sum(lf, suffix_tri2)
                a = _keep(strict, jnp.exp(lb + sloc + run_ref[kb]))
                da = jnp.concatenate([_dot(d, v_ref[pl.ds(k0, blk), cols], _NT) for d, cols in zip(dost, pairs)], axis=0)
                g = da * a
                p = pg + _tri_sum(g, prefix_tri2)
                sig = jnp.exp(lb)
                dz = _mx(_keep(strict, g * (1.0 - sig) - p * sig) * (dh ** -0.5))
                am = _mx(a)
                new_dqs = []
                for i, cols in enumerate(pairs):
                    rows = slice(2 * blk * i, 2 * blk * (i + 1))
                    new_dqs.append(dqs[i] + _unstack_heads(_dot(dz[rows], k_ref[pl.ds(k0, blk), cols]), lane_lo))
                    dk_acc[pl.ds(k0, blk), cols] += _dot(dz[rows], qst[i], _TN)
                    dv_acc[pl.ds(k0, blk), cols] += _dot(am[rows], dost[i], _TN)
                return pg + jnp.sum(g, axis=1, keepdims=True), tuple(new_dqs)

            pg, dqs = lax.fori_loop(qi + 1 - nlive, qi, lambda kb, st: tile(kb, *st, False),
                                    (jnp.zeros((nh * blk, 1), F32), tuple([jnp.zeros((blk, LANES), F32)] * len(pairs))))
            _, dqs = tile(qi, pg, dqs, True)
            for dq, cols in zip(dqs, pairs):
                dq_ref[pl.ds(q0, blk), cols] = dq.astype(dq_ref.dtype)
            return carry

        lax.fori_loop(0, nblk, qstep, 0)
        dk_ref[...] = dk_acc[...].astype(dk_ref.dtype)
        dv_ref[...] = dv_acc[...].astype(dv_ref.dtype)

    col = lambda off: pl.BlockSpec((s, nh * dh), lambda p: (0, p + off))
    (dq, dk, dv), carried = hosted_call(
        body, hook, "sb_bwd", (nstep,), [col(0), col(nstep), col(2 * nstep), col(0)], [col(0), col(0), col(0)],
        [_sds((s, D_MODEL), MXU_DTYPE)] * 3,
        [pltpu.VMEM((s, nh * dh), F32), pltpu.VMEM((s, nh * dh), F32), pltpu.VMEM((nblk, nh * blk, 1), F32)],
        ("parallel",), (qkv, qkv, qkv, do))
    return jnp.concatenate([dq, dk, dv], axis=1), carried


SSD_XBC_TILE0 = SSD_D_INNER // CONV_COLS


def ssd_conv_fwd(proj, cw, cb):
    s = proj.shape[0]
    rows = min(CONV_ROWS, s)

    def body(u_ref, w_ref, b_ref, o_ref):
        def step(i, carry):
            u, _ = _conv_rows(u_ref, w_ref, b_ref, i, rows, SSD_CONV)
            o_ref[pl.ds(pl.multiple_of(i * rows, rows), rows), :] = _silu(u)
            return carry

        lax.fori_loop(0, s // rows, step, 0)

    return pl.pallas_call(
        body, name="ssd_conv_fwd", grid=(SSD_CONV_DIM // CONV_COLS,),
        in_specs=[pl.BlockSpec((s, CONV_COLS), lambda j: (0, j + SSD_XBC_TILE0)),
                  pl.BlockSpec((SSD_CONV, CONV_COLS), lambda j: (0, j)), pl.BlockSpec((1, CONV_COLS), lambda j: (0, j))],
        out_specs=pl.BlockSpec((s, CONV_COLS), lambda j: (0, j)),
        out_shape=_sds((s, SSD_CONV_DIM), F32), compiler_params=_params(("parallel",)),
    )(proj, cw, cb)


def ssd_conv_bwd(proj, dact, cw, cb):
    s = proj.shape[0]
    rows = min(CONV_ROWS, s)
    nsteps = s // rows
    w = SSD_CONV

    def body(u_ref, da_ref, w_ref, b_ref, du_ref, dw_ref, db_ref):
        def step(it, carry):
            nxt, accs = carry
            i = nsteps - 1 - it
            r0 = pl.multiple_of(i * rows, rows)
            u, sh = _conv_rows(u_ref, w_ref, b_ref, i, rows, w)
            dconv = da_ref[pl.ds(r0, rows), :] * _silu_and_grad(u)[1]
            du_ref[pl.ds(r0, rows), :] = _conv_bwd_rows(dconv, nxt, w_ref, w).astype(du_ref.dtype)
            new = [accs[j] + jnp.sum(dconv * sh[w - 1 - j], axis=0, keepdims=True) for j in range(w)]
            new.append(accs[w] + jnp.sum(dconv, axis=0, keepdims=True))
            return dconv[0:SUBLANES], tuple(new)

        zrow = jnp.zeros((1, CONV_COLS), F32)
        _, accs = lax.fori_loop(0, nsteps, step, (jnp.zeros((SUBLANES, CONV_COLS), F32), tuple([zrow] * (w + 1))))
        dw_ref[...] = jnp.concatenate(accs[0:w], axis=0)
        db_ref[...] = accs[w]

    col = pl.BlockSpec((s, CONV_COLS), lambda j: (0, j))
    return pl.pallas_call(
        body, name="ssd_conv_bwd", grid=(SSD_CONV_DIM // CONV_COLS,),
        in_specs=[pl.BlockSpec((s, CONV_COLS), lambda j: (0, j + SSD_XBC_TILE0)), col,
                  pl.BlockSpec((w, CONV_COLS), lambda j: (0, j)), pl.BlockSpec((1, CONV_COLS), lambda j: (0, j))],
        out_specs=[col, pl.BlockSpec((w, CONV_COLS), lambda j: (0, j)), pl.BlockSpec((1, CONV_COLS), lambda j: (0, j))],
        out_shape=[_sds((s, SSD_CONV_DIM), MXU_DTYPE), _sds((w, SSD_CONV_DIM), F32), _sds((1, SSD_CONV_DIM), F32)],
        compiler_params=_params(("parallel",)),
    )(proj, dact, cw, cb)


def _split3(x):
    hi = x.astype(MXU_DTYPE)
    r1 = x - hi.astype(F32)
    mid = r1.astype(MXU_DTYPE)
    lo = (r1 - mid.astype(F32)).astype(MXU_DTYPE)
    return hi, mid, lo


def _dot01(x, m, dims=_NN, left=False):
    parts = _split3(x)
    if left:
        return _dot(m, parts[0], dims) + _dot(m, parts[1], dims) + _dot(m, parts[2], dims)
    return _dot(parts[0], m, dims) + _dot(parts[1], m, dims) + _dot(parts[2], m, dims)


def _softplus(x):
    return jnp.maximum(x, 0.0) + jnp.log1p(jnp.exp(-jnp.abs(x)))


def _ssd_consts(dt_bias, a_log, d_skip):
    pad = lambda v: jnp.pad(v.reshape(1, SSD_HEADS), ((0, 0), (0, LANES - SSD_HEADS)))
    head_of = jnp.arange(SSD_D_INNER) // SSD_HEAD_DIM
    expand = (jnp.arange(LANES)[:, None] == head_of[None, :]).astype(MXU_DTYPE)
    return dict(bias_w=pad(dt_bias), alog_w=pad(a_log), bias_c=dt_bias.reshape(SSD_HEADS, 1),
                alog_c=a_log.reshape(SSD_HEADS, 1), dskip=jnp.repeat(d_skip, SSD_HEAD_DIM).reshape(1, SSD_D_INNER),
                expand=expand, reduce=expand.T)


def _ssd_chunk_prep(dtp, dtp_t, bias_w, alog_w, bias_c, alog_c, expand):
    L = dtp.shape[0]
    r = lax.broadcasted_iota(jnp.int32, (L, L), 0)
    c = lax.broadcasted_iota(jnp.int32, (L, L), 1)
    tril = r >= c
    lower = jnp.where(tril, 1.0, 0.0).astype(MXU_DTYPE)
    upper = jnp.where(r <= c, 1.0, 0.0).astype(MXU_DTYPE)
    dt_col = _softplus(dtp + bias_w)
    a_col = -jnp.exp(alog_w) * dt_col
    a_row = -jnp.exp(alog_c) * _softplus(dtp_t + bias_c)
    acum_col = _dot01(a_col, lower, left=True)
    acum_row = _dot01(a_row, upper)
    acum_full = _dot01(acum_col, expand)
    dt_full = _dot01(dt_col, expand)
    return dict(tril=tril, lower=lower, upper=upper, dt_col=dt_col, a_col=a_col, acum_col=acum_col,
                acum_row=acum_row, acum_full=acum_full, dt_full=dt_full)


def _head_mask(j):
    lane = lax.broadcasted_iota(jnp.int32, (1, LANES), 1)
    return jnp.where((lane // SSD_HEAD_DIM) == j, 1.0, 0.0)


def _decay(pre, h):
    seg = pre["acum_col"][:, h:h + 1] - pre["acum_row"][h:h + 1, :]
    return jnp.exp(jnp.where(pre["tril"], seg, -1e30))


def _ssd_specs(s, nc, rev):
    L = SSD_CHUNK
    ci = (lambda i: nc - 1 - i) if rev else (lambda i: i)
    const = lambda shape: pl.BlockSpec(shape, lambda i: (0,) * len(shape))
    return dict(
        xbc=pl.BlockSpec((L, SSD_CONV_DIM), lambda i: (ci(i), 0)),
        dtp=pl.BlockSpec((L, LANES), lambda i: (ci(i), SSD_IN_PAD // LANES - 1)),
        dtp_t=pl.BlockSpec((SSD_HEADS, L), lambda i: (0, ci(i))),
        rows=pl.BlockSpec((L, SSD_D_INNER), lambda i: (ci(i), 0)),
        state=pl.BlockSpec((1, SSD_GROUPS, SSD_STATE, 4 * SSD_HEAD_DIM), lambda i: (ci(i), 0, 0, 0)),
        consts=[const((1, LANES)), const((1, LANES)), const((SSD_HEADS, 1)), const((SSD_HEADS, 1)),
                const((1, SSD_D_INNER)), const((LANES, SSD_D_INNER)), const((SSD_D_INNER, LANES))],
    )


def _const_args(cs):
    return [cs["bias_w"], cs["alog_w"], cs["bias_c"], cs["alog_c"], cs["dskip"], cs["expand"], cs["reduce"]]


def ssd_scan_fwd(act, proj, dtp_t, cs, hook=None):
    s = act.shape[0]
    L = SSD_CHUNK
    nc = s // L
    G, N, GW = SSD_GROUPS, SSD_STATE, 4 * SSD_HEAD_DIM

    def body(act_ref, dtp_ref, dtpt_ref, bw_ref, aw_ref, bc_ref, ac_ref, dsk_ref, ex_ref, rd_ref, y_ref, st_out, st):
        @pl.when(pl.program_id(0) == 0)
        def _():
            st[...] = jnp.zeros_like(st)

        st_out[0] = st[...]
        pre = _ssd_chunk_prep(dtp_ref[...], dtpt_ref[...], bw_ref[...], aw_ref[...], bc_ref[...], ac_ref[...], ex_ref[...])
        acum_full = pre["acum_full"]
        last_full = acum_full[L - 1:L, :]
        for g in range(G):
            bg = _mx(act_ref[:, SSD_D_INNER + g * N:SSD_D_INNER + (g + 1) * N])
            cg = _mx(act_ref[:, SSD_D_INNER + G * N + g * N:SSD_D_INNER + G * N + (g + 1) * N])
            cb = _dot(cg, bg, _NT)
            for half in range(2):
                p = 2 * g + half
                cols = slice(p * LANES, (p + 1) * LANES)
                xs = act_ref[:, cols]
                xdt = xs * pre["dt_full"][:, cols]
                yd = jnp.zeros((L, LANES), F32)
                for j in range(2):
                    m = cb * _decay(pre, 2 * p + j)
                    yd = yd + _dot(_mx(m), _mx(xdt * _head_mask(j)))
                yoff = _dot(cg, _mx(st[g, :, half * LANES:(half + 1) * LANES])) * jnp.exp(acum_full[:, cols])
                y_ref[:, cols] = yd + yoff + dsk_ref[:, cols] * xs
                w = jnp.exp(last_full[:, cols] - acum_full[:, cols])
                st[g, :, half * LANES:(half + 1) * LANES] = (
                    st[g, :, half * LANES:(half + 1) * LANES] * jnp.exp(last_full[:, cols]) + _dot(bg, _mx(xdt * w), _TN))

    sp = _ssd_specs(s, nc, False)
    (y, states), carried = hosted_call(
        body, hook, "ssd_scan_fwd", (nc,), [sp["xbc"], sp["dtp"], sp["dtp_t"]] + sp["consts"],
        [sp["rows"], sp["state"]], [_sds((s, SSD_D_INNER), F32), _sds((nc, G, N, GW), F32)],
        [pltpu.VMEM((G, N, GW), F32)], ("arbitrary",), (act, proj, dtp_t, *_const_args(cs)))
    return y, states, carried


def ssd_scan_bwd(act, proj, dtp_t, cs, states, dy, hook=None):
    s = act.shape[0]
    L = SSD_CHUNK
    nc = s // L
    G, N, GW = SSD_GROUPS, SSD_STATE, 4 * SSD_HEAD_DIM

    def body(act_ref, dtp_ref, dtpt_ref, bw_ref, aw_ref, bc_ref, ac_ref, dsk_ref, ex_ref, rd_ref, st_ref, dy_ref,
             dact_ref, ddtp_ref, dalog_ref, dbias_ref, dskip_ref, dst, dxdt_ref, dac_ref):
        first = pl.program_id(0) == 0

        @pl.when(first)
        def _():
            dst[...] = jnp.zeros_like(dst)
            dalog_ref[...] = jnp.zeros_like(dalog_ref)
            dbias_ref[...] = jnp.zeros_like(dbias_ref)
            dskip_ref[...] = jnp.zeros_like(dskip_ref)

        expand, reduce = ex_ref[...], rd_ref[...]
        pre = _ssd_chunk_prep(dtp_ref[...], dtpt_ref[...], bw_ref[...], aw_ref[...], bc_ref[...], ac_ref[...], expand)
        acum_full = pre["acum_full"]
        last_full = acum_full[L - 1:L, :]
        ones = jnp.ones((2 * L, LANES), MXU_DTYPE)
        lane = lax.broadcasted_iota(jnp.int32, (L, LANES), 1)
        dacum_diag = jnp.zeros((L, LANES), F32)
        dlast_parts = []
        for g in range(G):
            bg = _mx(act_ref[:, SSD_D_INNER + g * N:SSD_D_INNER + (g + 1) * N])
            cg = _mx(act_ref[:, SSD_D_INNER + G * N + g * N:SSD_D_INNER + G * N + (g + 1) * N])
            cb = _dot(cg, bg, _NT)
            dcb = jnp.zeros((L, L), F32)
            dcg = jnp.zeros((L, N), F32)
            dbg = jnp.zeros((L, N), F32)
            for half in range(2):
                p = 2 * g + half
                cols = slice(p * LANES, (p + 1) * LANES)
                hcols = slice(half * LANES, (half + 1) * LANES)
                xs = act_ref[:, cols]
                xdt = xs * pre["dt_full"][:, cols]
                dyv = dy_ref[:, cols]
                dxdt = jnp.zeros((L, LANES), F32)
                for j in range(2):
                    h = 2 * p + j
                    dec = _decay(pre, h)
                    m = cb * dec
                    dyh = _mx(dyv * _head_mask(j))
                    dm = _dot(dyh, _mx(xdt), _NT)
                    dxdt = dxdt + _dot(_mx(m), dyh, _TN)
                    e = dm * m
                    ehi, elo = _split_hi_lo(e)
                    d_h = (_dot(jnp.concatenate([ehi, elo], axis=1), ones)
                           - _dot(jnp.concatenate([ehi, elo], axis=0), ones, _TN))
                    dacum_diag = jnp.where(lane == h, d_h, dacum_diag)
                    dcb = dcb + dm * dec
                lam = jnp.exp(acum_full[:, cols])
                stv = _mx(st_ref[0, g, :, hcols])
                z = _dot(cg, stv)
                dz = _mx(lam * dyv)
                dcg = dcg + _dot(dz, stv, _NT)
                dst_in = _dot(cg, dz, _TN)
                dsv = dst[g, :, hcols]
                w = jnp.exp(last_full[:, cols] - acum_full[:, cols])
                q = _dot(bg, _mx(dsv))
                wq = w * q
                dxdt = dxdt + wq
                wqx = wq * xdt
                dbg = dbg + _dot(_mx(xdt * w), _mx(dsv), _NT)
                elast = jnp.exp(last_full[:, cols])
                dlast_p = jnp.sum(wqx, axis=0, keepdims=True) + elast * jnp.sum(dsv * st_ref[0, g, :, hcols], axis=0, keepdims=True)
                dac_ref[:, cols] = dyv * z * lam - wqx
                dlast_parts.append(dlast_p)
                dst[g, :, hcols] = dst_in + dsv * elast
                dxdt_ref[:, cols] = dxdt
                dact_ref[:, cols] = dxdt * pre["dt_full"][:, cols] + dsk_ref[:, cols] * dyv
            dcbm = _mx(dcb)
            dact_ref[:, SSD_D_INNER + g * N:SSD_D_INNER + (g + 1) * N] = dbg + _dot(dcbm, cg, _TN)
            dact_ref[:, SSD_D_INNER + G * N + g * N:SSD_D_INNER + G * N + (g + 1) * N] = dcg + _dot(dcbm, bg)

        xs_all = act_ref[:, 0:SSD_D_INNER]
        dacum = dacum_diag + _dot_exact01(dac_ref[...], reduce)
        dlast = _dot_exact01(jnp.concatenate(dlast_parts, axis=1), reduce)
        row = lax.broadcasted_iota(jnp.int32, (L, LANES), 0)
        dacum = dacum + jnp.where(row == L - 1, dlast, 0.0)
        da_col = _dot01(dacum, pre["upper"], left=True)
        a_w = -jnp.exp(aw_ref[...])
        ddt = a_w * da_col + _dot_exact01(dxdt_ref[...] * xs_all, reduce)
        xin = dtp_ref[...] + bw_ref[...]
        ddtp = ddt * (1.0 / (1.0 + jnp.exp(-xin)))
        valid = lane < SSD_HEADS
        ddtp = jnp.where(valid, ddtp, 0.0)
        ddtp_ref[...] = ddtp
        dbias_ref[...] += jnp.sum(ddtp, axis=0, keepdims=True)
        dalog_ref[...] += jnp.sum(jnp.where(valid, da_col * pre["a_col"], 0.0), axis=0, keepdims=True)
        dskip_ref[...] += jnp.sum(_dot_exact01(dy_ref[...] * xs_all, reduce), axis=0, keepdims=True)

    sp = _ssd_specs(s, nc, True)
    acc = pl.BlockSpec((1, LANES), lambda i: (0, 0))
    outs, carried = hosted_call(
        body, hook, "ssd_scan_bwd", (nc,),
        [sp["xbc"], sp["dtp"], sp["dtp_t"]] + sp["consts"] + [sp["state"], sp["rows"]],
        [sp["xbc"], pl.BlockSpec((L, LANES), lambda i: (nc - 1 - i, 0)), acc, acc, acc],
        [_sds((s, SSD_CONV_DIM), F32), _sds((s, LANES), F32)] + [_sds((1, LANES), F32)] * 3,
        [pltpu.VMEM((G, N, GW), F32), pltpu.VMEM((L, SSD_D_INNER), F32), pltpu.VMEM((L, SSD_D_INNER), F32)],
        ("arbitrary",), (act, proj, dtp_t, *_const_args(cs), states, dy))
    return (*outs, carried)


def ssd_post_fwd(y, proj, g):
    s, d = y.shape
    ts = _pick(s, (256, 128))

    def body(y_ref, z_ref, g_ref, o_ref):
        y2 = y_ref[...] * _silu(z_ref[...])
        r = lax.rsqrt(jnp.mean(y2 * y2, axis=-1, keepdims=True) + NORM_EPS)
        o_ref[...] = (y2 * r * g_ref[...]).astype(o_ref.dtype)

    row = pl.BlockSpec((ts, d), lambda i: (i, 0))
    return pl.pallas_call(
        body, name="ssd_post_fwd", grid=(s // ts,), in_specs=[row, row, pl.BlockSpec((1, d), lambda i: (0, 0))],
        out_specs=row, out_shape=_sds((s, d), MXU_DTYPE), compiler_params=_params(("parallel",)),
    )(y, proj, g)


def ssd_post_bwd(y, proj, g, dy3):
    s, d = y.shape
    ts = _pick(s, (256, 128))

    def body(y_ref, z_ref, g_ref, d3_ref, dy_ref, dz_ref, dg_ref):
        yv, zv = y_ref[...], z_ref[...]
        sz, sgrad = _silu_and_grad(zv)
        y2 = yv * sz
        r = lax.rsqrt(jnp.mean(y2 * y2, axis=-1, keepdims=True) + NORM_EPS)
        xh = y2 * r
        d3 = d3_ref[...]
        dxh = d3 * g_ref[...]
        dy2 = r * (dxh - xh * jnp.mean(dxh * xh, axis=-1, keepdims=True))
        dy_ref[...] = dy2 * sz
        dz_ref[...] = (dy2 * yv * sgrad).astype(dz_ref.dtype)
        part = jnp.sum(d3 * xh, axis=0, keepdims=True)

        @pl.when(pl.program_id(0) == 0)
        def _():
            dg_ref[...] = part

        @pl.when(pl.program_id(0) != 0)
        def _():
            dg_ref[...] += part

    row = pl.BlockSpec((ts, d), lambda i: (i, 0))
    vec = pl.BlockSpec((1, d), lambda i: (0, 0))
    return pl.pallas_call(
        body, name="ssd_post_bwd", grid=(s // ts,), in_specs=[row, row, vec, row], out_specs=[row, row, vec],
        out_shape=[_sds((s, d), F32), _sds((s, d), MXU_DTYPE), _sds((1, d), F32)],
        compiler_params=_params(("arbitrary",)),
    )(y, proj, g, dy3)


def dt_transpose(proj):
    s = proj.shape[0]
    ts = _pick(s, (512, 256, 128))

    def body(p_ref, o_ref):
        o_ref[...] = p_ref[...].T

    return pl.pallas_call(
        body, name="dt_transpose", grid=(s // ts,),
        in_specs=[pl.BlockSpec((ts, LANES), lambda i: (i, SSD_IN_PAD // LANES - 1))],
        out_specs=pl.BlockSpec((LANES, ts), lambda i: (0, i)), out_shape=_sds((LANES, s), F32),
        compiler_params=_params(("parallel",)),
    )(proj)


def ssd_core_fwd(proj, cw, cb, dt_bias, a_log, d_skip, norm_g, hook=None):
    cs = _ssd_consts(dt_bias, a_log, d_skip)
    act = ssd_conv_fwd(proj, cw, cb)
    dtp_t = dt_transpose(proj)
    y, states, carried = ssd_scan_fwd(act, proj, dtp_t, cs, hook)
    y3 = ssd_post_fwd(y, proj, norm_g)
    return y3, (cs, act, dtp_t, y, states), carried


def ssd_core_bwd(proj, cw, cb, norm_g, saved, dy3, hook=None):
    cs, act, dtp_t, y, states = saved
    dy, dz, dnorm = ssd_post_bwd(y, proj, norm_g, dy3)
    dact, ddtp, dalog, dbias, dskip, carried = ssd_scan_bwd(act, proj, dtp_t, cs, states, dy, hook)
    dxbc, dcw, dcb = ssd_conv_bwd(proj, dact, cw, cb)
    dproj = jnp.concatenate([dz, dxbc, ddtp.astype(MXU_DTYPE)], axis=1)
    h = SSD_HEADS
    return dproj, dcw, dcb, dbias[0, :h], dalog[0, :h], dskip[0, :h], dnorm, carried


def ssd_core(proj, cw, cb, dt_bias, a_log, d_skip, norm_g, dy3):
    y3, saved, _ = ssd_core_fwd(proj, cw, cb, dt_bias, a_log, d_skip, norm_g)
    return y3, ssd_core_bwd(proj, cw, cb, norm_g, saved, dy3)


def loss_head(x, g, target):
    s, d = x.shape
    ts = _pick(s, (512, 256, 128))

    def body(x_ref, g_ref, t_ref, loss_ref, dx_ref, dxm_ref, dg_ref):
        xv = x_ref[...]
        r = lax.rsqrt(jnp.mean(xv * xv, axis=-1, keepdims=True) + NORM_EPS)
        xh = xv * r
        err = xh * g_ref[...] - t_ref[...]
        dy = err * (1.0 / d)
        dxh = dy * g_ref[...]
        dx = r * (dxh - xh * jnp.mean(dxh * xh, axis=-1, keepdims=True))
        dx_ref[...] = dx
        dxm_ref[...] = dx.astype(dxm_ref.dtype)
        part = jnp.sum(dy * xh, axis=0, keepdims=True)
        lpart = jnp.full((1, LANES), 0.5 * jnp.sum(jnp.mean(err * err, axis=-1, keepdims=True)), F32)

        @pl.when(pl.program_id(0) == 0)
        def _():
            dg_ref[...] = part
            loss_ref[...] = lpart

        @pl.when(pl.program_id(0) != 0)
        def _():
            dg_ref[...] += part
            loss_ref[...] += lpart

    row = pl.BlockSpec((ts, d), lambda i: (i, 0))
    vec = pl.BlockSpec((1, d), lambda i: (0, 0))
    return pl.pallas_call(
        body, name="loss_head", grid=(s // ts,), in_specs=[row, vec, row],
        out_specs=[pl.BlockSpec((1, LANES), lambda i: (0, 0)), row, row, vec],
        out_shape=[_sds((1, LANES), F32), _sds((s, d), F32), _sds((s, d), MXU_DTYPE), _sds((1, d), F32)],
        compiler_params=_params(("arbitrary",)),
    )(x, g, target)


def _adamw_math(w, g, m, v):
    m = ADAM_B1 * m + (1.0 - ADAM_B1) * g
    v = ADAM_B2 * v + (1.0 - ADAM_B2) * (g * g)
    m_hat = m / (1.0 - ADAM_B1 ** ADAM_STEP)
    v_hat = v / (1.0 - ADAM_B2 ** ADAM_STEP)
    return -ADAM_LR * (m_hat / (jnp.sqrt(v_hat) + ADAM_EPS) + ADAM_WD * w), m, v


def adamw(w, g, m, v, name="adamw"):
    r, c = w.shape
    tr = _pick(r, (256, 128, 64, 32, 16, 8))

    def body(w_ref, g_ref, m_ref, v_ref, d_ref, nm_ref, nv_ref):
        d_ref[...], nm_ref[...], nv_ref[...] = _adamw_math(w_ref[...], g_ref[...], m_ref[...], v_ref[...])

    blk = pl.BlockSpec((tr, c), lambda i: (i, 0))
    return pl.pallas_call(
        body, name=name, grid=(r // tr,), in_specs=[blk] * 4, out_specs=[blk] * 3,
        out_shape=[_sds((r, c), F32)] * 3, compiler_params=_params(("parallel",)),
    )(w, g, m, v)


def adamw_small(w, parts, m, v):
    n, r, c = parts.shape

    def body(w_ref, p_ref, m_ref, v_ref, g_ref, d_ref, nm_ref, nv_ref):
        g = p_ref[0]
        for k in range(1, n):
            g = g + p_ref[k]
        g_ref[...] = g
        d_ref[...], nm_ref[...], nv_ref[...] = _adamw_math(w_ref[...], g, m_ref[...], v_ref[...])

    return pl.pallas_call(
        body, name="adamw_small", out_shape=[_sds((r, c), F32)] * 4, compiler_params=_params(),
    )(w, parts, m, v)


def pair_sum(unit, recv, where):
    nchip, _, r, c = unit.shape
    tr = _pick(r, (512, 256, 176, 128, 64, 32, 16))

    def body(w_ref, a_ref, b_ref, o_ref, ob_ref):
        sm = a_ref[0, 0] + b_ref[0]
        ob_ref[0] = sm.astype(ob_ref.dtype)

        @pl.when(pl.program_id(1) == w_ref[0])
        def _():
            o_ref[...] = sm

    blk = pl.BlockSpec((1, tr, c), lambda i, s, w: (s, i, 0))
    return pl.pallas_call(
        body, name="pair_sum",
        grid_spec=pltpu.PrefetchScalarGridSpec(
            num_scalar_prefetch=1, grid=(r // tr, nchip),
            in_specs=[pl.BlockSpec((1, 1, tr, c), lambda i, s, w: (s, w[1], i, 0)), blk],
            out_specs=[pl.BlockSpec((tr, c), lambda i, s, w: (i, 0)), blk]),
        out_shape=[_sds((r, c), F32), _sds((nchip, r, c), jnp.bfloat16)],
        compiler_params=_params(("parallel", "arbitrary")),
    )(where, unit, recv)


def chip_sum(own, where, recv, layer, layers, prev=None):
    r, c = own.shape
    tr = _pick(r, (512, 256, 176, 128, 64, 32, 16))

    def body(s_ref, a_ref, b_ref, *rest):
        rest[-1][...] = a_ref[...] + b_ref[0].astype(F32) + b_ref[1].astype(F32) + b_ref[2].astype(F32)

    in_specs = [pl.BlockSpec((tr, c), lambda i, s: (i, 0)), pl.BlockSpec((3, tr, c), lambda i, s: (0, i, 0))]
    args = [where, own, recv]
    if prev is not None:
        in_specs.append(ANY)
        args.append(prev)
    return pl.pallas_call(
        body, name="chip_sum",
        grid_spec=pltpu.PrefetchScalarGridSpec(
            num_scalar_prefetch=1, grid=(r // tr,), in_specs=in_specs,
            out_specs=pl.BlockSpec((None, None, tr, c), lambda i, s: (layer, s[1], i, 0))),
        out_shape=_sds((layers, 2, r, c), F32), input_output_aliases={} if prev is None else {3: 0},
        compiler_params=_params(("parallel",)),
    )(*args)


def place_cast(w, layer, chip):
    _, a, b = w.shape
    ta = _pick(a, (512, 352, 256, 128))

    def body(c_ref, w_ref, o_ref):
        o_ref[...] = w_ref[...].astype(o_ref.dtype)

    return pl.pallas_call(
        body, name="place_cast",
        grid_spec=pltpu.PrefetchScalarGridSpec(
            num_scalar_prefetch=1, grid=(a // ta,),
            in_specs=[pl.BlockSpec((None, ta, b), lambda i, c: (layer, i, 0))],
            out_specs=pl.BlockSpec((None, ta, b), lambda i, c: (c[0], i, 0))),
        out_shape=_sds((N_CHIPS, a, b), MXU_DTYPE), compiler_params=_params(("parallel",)),
    )(chip, w)


ANY = pl.BlockSpec(memory_space=pl.ANY)
COMM = pltpu.CompilerParams(has_side_effects=True)


def _coords():
    return lax.axis_index("x"), lax.axis_index("y"), lax.axis_index("c")


def _other_chips(x, y):
    return [(1 - x, y), (x, 1 - y), (1 - x, 1 - y)]


def all_gather_8(halves, name):
    _, r, c = halves.shape

    def body(h_ref, out_ref, send_sems, recv_sems, local_sem):
        x, y, cc = _coords()
        _gather_one(h_ref.at[cc], lambda px, py, pc: out_ref.at[4 * px + 2 * py + pc],
                    lambda k: send_sems.at[k], lambda k: recv_sems.at[k], local_sem)

    return pl.pallas_call(
        body, name=name, in_specs=[ANY], out_specs=ANY, out_shape=_sds((8, r, c), halves.dtype),
        scratch_shapes=[pltpu.SemaphoreType.DMA((7,)), pltpu.SemaphoreType.DMA((7,)), pltpu.SemaphoreType.DMA],
        compiler_params=COMM,
    )(halves)


def _gather_plan(x_ref, slot, send_sem, recv_sem, local_sem):
    x, y, cc = _coords()
    me, sibling = (x, y, cc), (x, y, 1 - cc)
    chips = _other_chips(x, y)

    def copy(k, blk, to, src=None):
        return pltpu.make_async_remote_copy(
            src_ref=slot(*blk) if src is None else src, dst_ref=slot(*blk),
            send_sem=send_sem(k), recv_sem=recv_sem(k), device_id=to, device_id_type=MESH)

    mine = pltpu.make_async_copy(x_ref, slot(*me), local_sem)
    first = [copy(0, me, sibling, src=x_ref)] + [copy(1 + j, me, (*chip, cc), src=x_ref) for j, chip in enumerate(chips)]
    passed = [copy(4 + j, (*chip, cc), sibling) for j, chip in enumerate(chips)]
    over_ici = [copy(1 + j, (*chip, cc), me) for j, chip in enumerate(chips)]
    from_sibling = [copy(0, sibling, me)] + [copy(4 + j, (*chip, 1 - cc), me) for j, chip in enumerate(chips)]
    return mine, first, passed, over_ici, from_sibling


def _gather_run(plans):
    for mine, first, _, _, _ in plans:
        mine.start()
        for cp in first:
            cp.start()
    for j in range(3):
        for _, _, passed, over_ici, _ in plans:
            over_ici[j].wait_recv()
            passed[j].start()
    for mine, first, passed, _, from_sibling in plans:
        for cp in from_sibling:
            cp.wait_recv()
        for cp in first + passed:
            cp.wait_send()
        mine.wait()


def _gather_one(x_ref, slot, send_sem, recv_sem, local_sem):
    _gather_run([_gather_plan(x_ref, slot, send_sem, recv_sem, local_sem)])


def gather_hook(items):
    n = len(items)

    def plan(refs, send_sems, recv_sems):
        x, y, cc = _coords()

        def copy(i, k, px, py, pc, to):
            blk = refs[i].at[2 * px + py, pc]
            return pltpu.make_async_remote_copy(src_ref=blk, dst_ref=blk, send_sem=send_sems.at[i, k],
                                                recv_sem=recv_sems.at[i, k], device_id=to, device_id_type=MESH)

        chips = _other_chips(x, y)
        first = [copy(i, j, x, y, cc, (*chip, cc)) for i in range(n) for j, chip in enumerate(chips)]
        return copy, chips, first, (x, y, cc)

    def start(refs, new, sems):
        for cp in plan(refs, *sems)[2]:
            cp.start()

    def finish(refs, new, sems):
        copy, chips, first, (x, y, cc) = plan(refs, *sems)
        passed = []
        for j, chip in enumerate(chips):
            for i in range(n):
                copy(i, j, *chip, cc, (x, y, cc)).wait_recv()
                passed.append(copy(i, 3 + j, *chip, cc, (x, y, 1 - cc)))
                passed[-1].start()
        for j, chip in enumerate(chips):
            for i in range(n):
                copy(i, 3 + j, *chip, 1 - cc, (x, y, cc)).wait_recv()
        for cp in first + passed:
            cp.wait_send()

    return dict(arrays=list(items), new=[], start=start, finish=finish, in_place=True,
                sems=[pltpu.SemaphoreType.DMA((n, 6)), pltpu.SemaphoreType.DMA((n, 6))])


def hosted_call(body, hook, name, grid, in_specs, out_specs, out_shape, scratch_shapes, sem, args):
    single = not isinstance(out_shape, (list, tuple))
    out_specs_l = [out_specs] if single else list(out_specs)
    out_shape_l = [out_shape] if single else list(out_shape)
    if hook is None:
        res = pl.pallas_call(body, name=name, grid=grid, in_specs=list(in_specs), out_specs=out_specs, out_shape=out_shape,
                             scratch_shapes=list(scratch_shapes), compiler_params=_params(sem))(*args)
        return res, []
    items, new = hook["arrays"], hook["new"]
    k, kn, n_in, n_out, n_scr = len(items), len(new), len(in_specs), len(out_specs_l), len(scratch_shapes)
    ka = k if hook["in_place"] else 0

    def full(*refs):
        ins = refs[:n_in]
        base = n_in + k
        outs = refs[base:base + n_out]
        hrefs = refs[base + n_out:base + n_out + ka] if ka else refs[n_in:base]
        nrefs = refs[base + n_out + ka:base + n_out + ka + kn]
        scr = refs[base + n_out + ka + kn:base + n_out + ka + kn + n_scr]
        sems = refs[base + n_out + ka + kn + n_scr:]
        ids = [pl.program_id(d) for d in range(len(grid))]
        first = functools.reduce(jnp.logical_and, [i == 0 for i in ids])
        last = functools.reduce(jnp.logical_and, [i == g - 1 for i, g in zip(ids, grid)])

        @pl.when(first)
        def _():
            hook["start"](hrefs, nrefs, sems)

        body(*ins, *outs, *scr)

        @pl.when(last)
        def _():
            hook["finish"](hrefs, nrefs, sems)

    res = pl.pallas_call(
        full, name=name, grid=grid, in_specs=list(in_specs) + [ANY] * k, out_specs=out_specs_l + [ANY] * (ka + kn),
        out_shape=out_shape_l + [_sds(a.shape, a.dtype) for a in items[:ka]] + list(new),
        input_output_aliases={n_in + i: n_out + i for i in range(ka)},
        scratch_shapes=list(scratch_shapes) + hook["sems"],
        compiler_params=pltpu.CompilerParams(dimension_semantics=("arbitrary",) * len(grid),
                                             vmem_limit_bytes=VMEM_LIMIT, has_side_effects=True),
    )(*args, *items)
    return (res[0] if single else list(res[:n_out])), list(res[n_out:])


def comm_call(hook, name):
    k, kn = len(hook["arrays"]), len(hook["new"])
    ka = k if hook["in_place"] else 0

    def body(*refs):
        hrefs = refs[k:k + ka] if ka else refs[:k]
        hook["start"](hrefs, refs[k + ka:k + ka + kn], refs[k + ka + kn:])
        hook["finish"](hrefs, refs[k + ka:k + ka + kn], refs[k + ka + kn:])

    return list(pl.pallas_call(
        body, name=name, in_specs=[ANY] * k, out_specs=[ANY] * (ka + kn),
        out_shape=[_sds(a.shape, a.dtype) for a in hook["arrays"][:ka]] + list(hook["new"]),
        input_output_aliases={i: i for i in range(ka)}, scratch_shapes=hook["sems"], compiler_params=COMM,
    )(*hook["arrays"]))


def merge_hooks(hooks):
    hooks = [h for h in hooks if h is not None]
    if len(hooks) < 2:
        return hooks[0] if hooks else None

    def parts(refs, new, sems):
        out, a, b, c = [], 0, 0, 0
        for h in hooks:
            na, nn, ns = len(h["arrays"]), len(h["new"]), len(h["sems"])
            out.append((refs[a:a + na], new[b:b + nn], sems[c:c + ns]))
            a, b, c = a + na, b + nn, c + ns
        return out

    def start(refs, new, sems):
        for h, p in zip(hooks, parts(refs, new, sems)):
            h["start"](*p)

    def finish(refs, new, sems):
        for h, p in zip(hooks, parts(refs, new, sems)):
            h["finish"](*p)

    assert len({h["in_place"] for h in hooks}) == 1
    return dict(arrays=[a for h in hooks for a in h["arrays"]], new=[a for h in hooks for a in h["new"]],
                sems=[a for h in hooks for a in h["sems"]], start=start, finish=finish, in_place=hooks[0]["in_place"])


def split_carried(hooks, carried):
    hooks = [h for h in hooks if h is not None]
    off = sum(len(h["arrays"]) for h in hooks if h["in_place"])
    out = []
    for h in hooks:
        out.append(carried[off:off + len(h["new"])])
        off += len(h["new"])
    return out


def pair_swap_hook(units):
    n = len(units)

    def plan(refs, new, send_sems, recv_sems):
        x, y, cc = _coords()
        return [pltpu.make_async_remote_copy(src_ref=refs[i].at[:, 1 - cc], dst_ref=new[i], send_sem=send_sems.at[i],
                                             recv_sem=recv_sems.at[i], device_id=(x, y, 1 - cc), device_id_type=MESH)
                for i in range(n)]

    def start(refs, new, sems):
        for cp in plan(refs, new, *sems):
            cp.start()

    def finish(refs, new, sems):
        for cp in plan(refs, new, *sems):
            cp.wait()

    return dict(arrays=list(units), new=[_sds((u.shape[0],) + u.shape[2:], u.dtype) for u in units], start=start,
                finish=finish, in_place=False, sems=[pltpu.SemaphoreType.DMA((n,)), pltpu.SemaphoreType.DMA((n,))])


def chip_exchange_hook(units):
    n = len(units)

    def plan(refs, new, send_sems, recv_sems):
        x, y, cc = _coords()
        return [pltpu.make_async_remote_copy(
            src_ref=refs[i].at[2 * px + py], dst_ref=new[i].at[k], send_sem=send_sems.at[i, k],
            recv_sem=recv_sems.at[i, k], device_id=(px, py, cc), device_id_type=MESH)
            for i in range(n) for k, (px, py) in enumerate(_other_chips(x, y))]

    def start(refs, new, sems):
        for cp in plan(refs, new, *sems):
            cp.start()

    def finish(refs, new, sems):
        for cp in plan(refs, new, *sems):
            cp.wait()

    return dict(arrays=list(units), new=[_sds((3,) + u.shape[1:], u.dtype) for u in units], start=start,
                finish=finish, in_place=False, sems=[pltpu.SemaphoreType.DMA((n, 3)), pltpu.SemaphoreType.DMA((n, 3))])


def grad_half_swap(grads):
    n = len(grads)

    def body(*refs):
        outs, send_sems, recv_sems = refs[n:2 * n], refs[2 * n], refs[2 * n + 1]
        x, y, cc = _coords()
        cps = [pltpu.make_async_remote_copy(
            src_ref=outs[i].at[:, cc], dst_ref=outs[i].at[:, cc], send_sem=send_sems.at[i], recv_sem=recv_sems.at[i],
            device_id=(x, y, 1 - cc), device_id_type=MESH) for i in range(n)]
        for cp in cps:
            cp.start()
        for i, cp in enumerate(cps):
            cp.wait_send()
            pltpu.make_async_remote_copy(
                src_ref=outs[i].at[:, 1 - cc], dst_ref=outs[i].at[:, 1 - cc], send_sem=send_sems.at[i],
                recv_sem=recv_sems.at[i], device_id=(x, y, 1 - cc), device_id_type=MESH).wait_recv()

    return pl.pallas_call(
        body, name="grad_half_swap", in_specs=[ANY] * n, out_specs=[ANY] * n,
        out_shape=[_sds(g.shape, g.dtype) for g in grads], input_output_aliases={i: i for i in range(n)},
        scratch_shapes=[pltpu.SemaphoreType.DMA((n,)), pltpu.SemaphoreType.DMA((n,))], compiler_params=COMM,
    )(*grads)


N_CHIPS = 4
PACK_COLS = 1024
BIG = ("ssd_w_in", "ssd_w_out", "sb_w_qkv", "sb_w_out", "ffn_w_in", "ffn_w_out")
CONVW = ("ssd_conv_w", "ffn_conv_w")
COL_SHARDED = ("ssd_w_in", "sb_w_qkv", "ffn_w_in", "ssd_conv_w", "ffn_conv_w")
SMALL = ("mix_norm", "ffn_norm", "final_norm", "ssd_conv_b", "ssd_dt_bias", "ssd_a_log", "ssd_d", "ssd_norm", "ffn_conv_b")
WEIGHTS = ("mix_norm", "ffn_norm", "final_norm", "ssd_w_in", "ssd_conv_w", "ssd_conv_b", "ssd_dt_bias", "ssd_a_log",
           "ssd_d", "ssd_norm", "ssd_w_out", "sb_w_qkv", "sb_w_out", "ffn_w_in", "ffn_conv_w", "ffn_conv_b", "ffn_w_out")


def _to_rows(flat, multiple):
    rows = -(-flat.shape[-1] // PACK_COLS)
    rows = -(-rows // multiple) * multiple
    pad = rows * PACK_COLS - flat.shape[-1]
    return jnp.pad(flat, [(0, pad)]).reshape(rows, PACK_COLS)


def _unshard(name, stacked):
    l, n, a, b = stacked.shape
    if name in COL_SHARDED:
        return jnp.transpose(stacked, (0, 2, 1, 3)).reshape(l, a, n * b)
    return stacked.reshape(l, n * a, b)


def _gather_conv_weights(w):
    flat = jnp.concatenate([w[n].reshape(-1) for n in CONVW])
    rows = _to_rows(flat, 16)
    got = all_gather_8(rows.reshape(2, rows.shape[0] // 2, PACK_COLS), "gather_conv_weights").reshape(N_CHIPS, -1)
    out, off = {}, 0
    for n in CONVW:
        l, a, b = w[n].shape
        out[n] = _unshard(n, jnp.moveaxis(got[:, off:off + w[n].size].reshape(N_CHIPS, l, a, b), 0, 1))
        off += w[n].size
    return out


def _finish_big_grads(pair_sums, from_chips, layout):
    cc = lax.axis_index("c").astype(jnp.int32)
    chip = (2 * lax.axis_index("x") + lax.axis_index("y")).astype(jnp.int32)
    where = jnp.stack([chip, cc])
    nlayers = [1 + max(l for k, l in layout if k == wi) for wi in range(1 + max(k for k, _ in layout))]
    grads = [None] * len(nlayers)
    for (wi, l), p, r in zip(layout, pair_sums, from_chips):
        grads[wi] = chip_sum(p, where, r, l, nlayers[wi], grads[wi])
    return grad_half_swap(grads)


def kernel(x, mix_norm, ffn_norm, final_norm, ssd_w_in, ssd_conv_w, ssd_conv_b, ssd_dt_bias, ssd_a_log, ssd_d, ssd_norm, ssd_w_out, sb_w_qkv, sb_w_out, ffn_w_in, ffn_conv_w, ffn_conv_b, ffn_w_out, loss_target, m_mix_norm, m_ffn_norm, m_final_norm, m_ssd_w_in, m_ssd_conv_w, m_ssd_conv_b, m_ssd_dt_bias, m_ssd_a_log, m_ssd_d, m_ssd_norm, m_ssd_w_out, m_sb_w_qkv, m_sb_w_out, m_ffn_w_in, m_ffn_conv_w, m_ffn_conv_b, m_ffn_w_out, v_mix_norm, v_ffn_norm, v_final_norm, v_ssd_w_in, v_ssd_conv_w, v_ssd_conv_b, v_ssd_dt_bias, v_ssd_a_log, v_ssd_d, v_ssd_norm, v_ssd_w_out, v_sb_w_qkv, v_sb_w_out, v_ffn_w_in, v_ffn_conv_w, v_ffn_conv_b, v_ffn_w_out):
    given = dict(locals())
    w = {n: given[n] for n in WEIGHTS}
    mom = {n: given["m_" + n] for n in WEIGHTS}
    var = {n: given["v_" + n] for n in WEIGHTS}
    chip = 2 * lax.axis_index("x") + lax.axis_index("y")

    chip1 = chip.reshape(1).astype(jnp.int32)
    fw = _gather_conv_weights(w)
    row = lambda v: v.reshape(1, -1)

    def placed(n, l):
        _, a, b = w[n].shape
        return place_cast(w[n], l, chip1).reshape(N_CHIPS, 2, a // 2, b)

    def mixer_items(i):
        return [(n, i // 2) for n in (("ssd_w_in", "ssd_w_out") if i % 2 == 0 else ("sb_w_qkv", "sb_w_out"))]

    def ffn_items(i):
        return [("ffn_w_in", i), ("ffn_w_out", i)]

    def hook_for(items):
        return gather_hook([placed(n, l) for n, l in items]) if items else None

    lw = {}

    def arrived(items, arrays):
        for (n, l), arr in zip(items, arrays):
            g4 = arr.reshape(N_CHIPS, -1, arr.shape[-1])
            if n in COL_SHARDED:
                full = jnp.transpose(g4, (1, 0, 2)).reshape(g4.shape[1], -1)
            else:
                full = g4.reshape(-1, g4.shape[2])
            if n == "ssd_w_in":
                full = jnp.pad(full, ((0, 0), (0, SSD_IN_PAD - SSD_IN_DIM)))
            lw[(n, l)] = full

    first_items = [("ssd_w_in", 0)]
    carry = {
        (0, "mm_in"): [("ssd_w_out", 0), ("ffn_w_in", 0)],
        (0, "scan"): [("ffn_w_out", 0)] + mixer_items(1),
        (0, "ffn_in"): ffn_items(1),
        (1, "sb"): mixer_items(2) + ffn_items(2),
        (2, "mm_in"): mixer_items(3),
        (2, "scan"): ffn_items(3),
    }
    arrived(first_items, comm_call(hook_for(first_items), "gather_first"))

    def carrying(i, slot, call):
        items = carry.get((i, slot), [])
        if not items:
            return call(None)
        out, got = call(hook_for(items))
        arrived(items, got)
        return out

    xcur = x[0]
    saved = []
    for i in range(DEPTH):
        j = i // 2
        h, r = rms_fwd(xcur, row(mix_norm[i]))
        if i % 2 == 0:
            proj = carrying(i, "mm_in", lambda hk: mm(h, lw[("ssd_w_in", j)], tm=2048, tn=896, name="mm_ssd_in", hook=hk))
            items = carry.get((i, "scan"), [])
            y3, core, got = ssd_core_fwd(proj, fw["ssd_conv_w"][j], row(ssd_conv_b[j]), ssd_dt_bias[j], ssd_a_log[j],
                                         ssd_d[j], row(ssd_norm[j]), hook_for(items))
            arrived(items, got)
            x1 = mm(y3, lw[("ssd_w_out", j)], res=xcur, name="mm_ssd_out")
            mix = (proj, y3, core)
        else:
            qkv = mm(h, lw[("sb_w_qkv", j)], out_dtype=MXU_DTYPE, tm=2048, name="mm_sb_qkv")
            o = carrying(i, "sb", lambda hk: sb_fwd(qkv, hk))
            x1 = mm(o, lw[("sb_w_out", j)], res=xcur, name="mm_sb_out")
            mix = (qkv, o)
        h2, r2 = rms_fwd(x1, row(ffn_norm[i]))
        u0 = carrying(i, "ffn_in", lambda hk: mm(h2, lw[("ffn_w_in", i)], tm=2048, name="mm_ffn_in", hook=hk))
        a = ffn_mid_fwd(u0, fw["ffn_conv_w"][i], row(ffn_conv_b[i]))
        x2 = mm(a, lw[("ffn_w_out", i)], res=x1, name="mm_ffn_out")
        saved.append((xcur, h, r, mix, x1, h2, r2, u0, a))
        xcur = x2
    loss_part, dx, dxm, d_final = loss_head(xcur, row(final_norm), loss_target[0])

    gl = {n: [None] * w[n].shape[0] for n in WEIGHTS if n != "final_norm"}
    units = {n: [None] * w[n].shape[0] for n in BIG}

    def unit_of(g4):
        return g4.reshape(N_CHIPS, 2, g4.shape[1] // 2, g4.shape[2])

    where = jnp.stack([chip, lax.axis_index("c")]).astype(jnp.int32)
    pair_f32, wire, from_chips = {}, {}, {}

    def pair_sums(keys, swapped):
        for (n, l), got in zip(keys, swapped):
            pair_f32[(n, l)], wire[(n, l)] = pair_sum(units[n][l], got, where)

    for i in reversed(range(DEPTH)):
        j = i // 2
        x0, h, r, mix, x1, h2, r2, u0, a = saved[i]
        units["ffn_w_out"][i] = unit_of(mm(a, dxm, "tn", tm=1408, name="mm_d_ffn_out").reshape(N_CHIPS, -1, D_MODEL))
        da = mm(dxm, lw[("ffn_w_out", i)], "nt", tn=1408, name="mm_da_ffn")
        du0, gl["ffn_conv_w"][i], dcb = ffn_mid_bwd(u0, da, fw["ffn_conv_w"][i], row(ffn_conv_b[i]))
        gl["ffn_conv_b"][i] = dcb[0]
        units["ffn_w_in"][i] = unit_of(mm(h2, du0, "tn", tn=1408, tm=512, n_split=N_CHIPS, name="mm_d_ffn_in"))
        keys_f = ffn_items(i)
        swap = pair_swap_hook([units[n][l] for n, l in keys_f])
        dh2, carried = mm(du0, lw[("ffn_w_in", i)], "nt", name="mm_dh_ffn", hook=swap)
        pair_sums(keys_f, split_carried([swap], carried)[0])
        dx1, dx1m, dg = rms_bwd(x1, r2, row(ffn_norm[i]), dh2, dx)
        gl["ffn_norm"][i] = dg[0]
        keys_up = mixer_items(i + 1) if i + 1 < DEPTH else []
        exchanges = [chip_exchange_hook([wire[k] for k in keys_f]),
                     chip_exchange_hook([wire[k] for k in keys_up]) if keys_up else None]
        if i % 2 == 0:
            proj, y3, core = mix
            units["ssd_w_out"][j] = unit_of(mm(y3, dx1m, "tn", name="mm_d_ssd_out").reshape(N_CHIPS, -1, D_MODEL))
            dy3 = mm(dx1m, lw[("ssd_w_out", j)], "nt", name="mm_dy3_ssd")
            (dproj, gl["ssd_conv_w"][j], dcb, gl["ssd_dt_bias"][j], gl["ssd_a_log"][j], gl["ssd_d"][j], dnorm,
             carried) = ssd_core_bwd(proj, fw["ssd_conv_w"][j], row(ssd_conv_b[j]), row(ssd_norm[j]), core, dy3,
                                     merge_hooks(exchanges))
            gl["ssd_conv_b"][j] = dcb[0]
            gl["ssd_norm"][j] = dnorm[0]
            dw_in = mm(h, dproj, "tn", tn=896, name="mm_d_ssd_in")[:, :SSD_IN_DIM]
            units["ssd_w_in"][j] = unit_of(jnp.transpose(dw_in.reshape(D_MODEL, N_CHIPS, -1), (1, 0, 2)))
            dmix, w_in, dh_name = dproj, lw[("ssd_w_in", j)], "mm_dh_ssd"
        else:
            qkv, o = mix
            units["sb_w_out"][j] = unit_of(mm(o, dx1m, "tn", name="mm_d_sb_out").reshape(N_CHIPS, -1, D_MODEL))
            do = mm(dx1m, lw[("sb_w_out", j)], "nt", out_dtype=MXU_DTYPE, name="mm_do_sb")
            dqkv, carried = sb_bwd(qkv, do, merge_hooks(exchanges))
            units["sb_w_qkv"][j] = unit_of(mm(h, dqkv, "tn", tn=768, n_split=N_CHIPS, name="mm_d_sb_qkv"))
            dmix, w_in, dh_name = dqkv, lw[("sb_w_qkv", j)], "mm_dh_sb"
        got = split_carried(exchanges, carried)
        from_chips.update(zip(keys_f, got[0]))
        if keys_up:
            from_chips.update(zip(keys_up, got[1]))
        keys_m = mixer_items(i)
        swap = pair_swap_hook([units[n][l] for n, l in keys_m])
        dh, carried = mm(dmix, w_in, "nt", name=dh_name, hook=swap)
        pair_sums(keys_m, split_carried([swap], carried)[0])
        dx, dxm, dg = rms_bwd(x0, r, row(mix_norm[i]), dh, dx1)
        gl["mix_norm"][i] = dg[0]
    last = chip_exchange_hook([wire[k] for k in mixer_items(0)])
    from_chips.update(zip(mixer_items(0), comm_call(last, "grad_exchange_last")))

    layout = [(k, l) for k, n in enumerate(BIG) for l in range(w[n].shape[0])]
    reduced = _finish_big_grads([pair_f32[(BIG[k], l)] for k, l in layout], [from_chips[(BIG[k], l)] for k, l in layout],
                                layout)
    g, delta, new_m, new_v = {}, {}, {}, {}
    two_d = lambda t: t.reshape(-1, t.shape[-1])
    for n, red in zip(BIG, reduced):
        g[n] = red.reshape(w[n].shape)
        d2, m2, v2 = adamw(two_d(w[n]), two_d(g[n]), two_d(mom[n]), two_d(var[n]), name="adamw_" + n)
        delta[n], new_m[n], new_v[n] = d2.reshape(w[n].shape), m2.reshape(w[n].shape), v2.reshape(w[n].shape)

    small_g = {n: jnp.stack(gl[n]) for n in SMALL + CONVW if n != "final_norm"}
    small_g["final_norm"] = d_final[0]
    zeros_of = lambda n: jnp.zeros((small_g[n].size,), F32)

    def small_pack(d, extra):
        parts = [d[n].reshape(-1) for n in SMALL] + [extra]
        parts += [(d[n].reshape(-1) if d is small_g else zeros_of(n)) for n in CONVW]
        return _to_rows(jnp.concatenate(parts), 16)

    part = small_pack(small_g, loss_part[0, 0:1])
    parts = all_gather_8(jnp.stack([part, part]), "gather_small_grads")
    zero = jnp.zeros((1,), F32)
    gs, ds, ms, vs = adamw_small(small_pack(w, zero), parts, small_pack(mom, zero), small_pack(var, zero))
    gs_flat = gs.reshape(-1)
    off = 0
    for n in SMALL:
        size = w[n].size
        for dst, src in ((g, gs), (delta, ds), (new_m, ms), (new_v, vs)):
            dst[n] = src.reshape(-1)[off:off + size].reshape(w[n].shape)
        off += size
    loss = gs_flat[off]
    off += 1
    for n in CONVW:
        size = small_g[n].size
        b = w[n].shape[-1]
        g[n] = lax.dynamic_slice_in_dim(gs_flat[off:off + size].reshape(small_g[n].shape), chip * b, b, axis=2)
        d2, m2, v2 = adamw(two_d(w[n]), two_d(g[n]), two_d(mom[n]), two_d(var[n]), name="adamw_" + n)
        delta[n], new_m[n], new_v[n] = d2.reshape(w[n].shape), m2.reshape(w[n].shape), v2.reshape(w[n].shape)
        off += size

    return (loss, dx[None], *[g[n] for n in WEIGHTS], *[delta[n] for n in WEIGHTS],
            *[new_m[n] for n in WEIGHTS], *[new_v[n] for n in WEIGHTS])
```

```python
import functools

import jax
import jax.numpy as jnp
from jax import lax
from jax.experimental import pallas as pl
from jax.experimental.pallas import tpu as pltpu

F32 = jnp.float32
MXU_DTYPE = jnp.bfloat16
HIGHEST = lax.Precision.HIGHEST

D_MODEL = 1024
DEPTH = 4
NORM_EPS = 1e-6
SSD_D_INNER = 2048
SSD_HEAD_DIM = 64
SSD_HEADS = 32
SSD_GROUPS = 8
SSD_STATE = 128
SSD_CONV = 4
SSD_CHUNK = 128
SSD_CONV_DIM = 4096
SSD_IN_DIM = 6176
SSD_IN_PAD = 6272
SB_HEADS = 16
SB_HEAD_DIM = 64
FFN_D_FF = 2816
FFN_CONV = 3
ADAM_LR, ADAM_B1, ADAM_B2, ADAM_EPS, ADAM_WD, ADAM_STEP = 0.001, 0.9, 0.999, 1e-08, 0.01, 10

LANES = 128
SUBLANES = 8
VMEM_LIMIT = 56 * 1024 * 1024
MESH = pl.DeviceIdType.MESH


def _params(sem=None):
    return pltpu.CompilerParams(dimension_semantics=sem, vmem_limit_bytes=VMEM_LIMIT)


def _sds(shape, dtype):
    return jax.ShapeDtypeStruct(shape, dtype)


def _dot(a, b, dims=(((1,), (0,)), ((), ())), precision=None):
    return lax.dot_general(a, b, dims, precision=precision, preferred_element_type=F32)


_NN = (((1,), (0,)), ((), ()))
_NT = (((1,), (1,)), ((), ()))
_TN = (((0,), (0,)), ((), ()))


def _mx(a):
    return a.astype(MXU_DTYPE)


def _silu(x):
    return x * (1.0 / (1.0 + jnp.exp(-x)))


def _silu_and_grad(x):
    s = 1.0 / (1.0 + jnp.exp(-x))
    return x * s, s * (1.0 + x * (1.0 - s))


def _pick(n, cands):
    for c in cands:
        if n % c == 0:
            return c
    return n


def rms_fwd(x, g):
    s, d = x.shape
    ts = _pick(s, (512, 256, 128))

    def body(x_ref, g_ref, h_ref, r_ref):
        xv = x_ref[...]
        r = lax.rsqrt(jnp.mean(xv * xv, axis=-1, keepdims=True) + NORM_EPS)
        h_ref[...] = (xv * r * g_ref[...]).astype(h_ref.dtype)
        r_ref[...] = r

    return pl.pallas_call(
        body, name="rms_fwd", grid=(s // ts,),
        in_specs=[pl.BlockSpec((ts, d), lambda i: (i, 0)), pl.BlockSpec((1, d), lambda i: (0, 0))],
        out_specs=[pl.BlockSpec((ts, d), lambda i: (i, 0)), pl.BlockSpec((ts, 1), lambda i: (i, 0))],
        out_shape=[_sds((s, d), MXU_DTYPE), _sds((s, 1), F32)],
        compiler_params=_params(("parallel",)),
    )(x, g)


def rms_bwd(x, r, g, dh, dres):
    s, d = x.shape
    ts = _pick(s, (512, 256, 128))

    def body(x_ref, r_ref, g_ref, dh_ref, dres_ref, dx_ref, dxm_ref, dg_ref):
        xh = x_ref[...] * r_ref[...]
        dhv = dh_ref[...]
        dxh = dhv * g_ref[...]
        dx = dres_ref[...] + r_ref[...] * (dxh - xh * jnp.mean(dxh * xh, axis=-1, keepdims=True))
        dx_ref[...] = dx
        dxm_ref[...] = dx.astype(dxm_ref.dtype)
        part = jnp.sum(dhv * xh, axis=0, keepdims=True)

        @pl.when(pl.program_id(0) == 0)
        def _():
            dg_ref[...] = part

        @pl.when(pl.program_id(0) != 0)
        def _():
            dg_ref[...] += part

    row = pl.BlockSpec((ts, d), lambda i: (i, 0))
    return pl.pallas_call(
        body, name="rms_bwd", grid=(s // ts,),
        in_specs=[row, pl.BlockSpec((ts, 1), lambda i: (i, 0)), pl.BlockSpec((1, d), lambda i: (0, 0)), row, row],
        out_specs=[row, row, pl.BlockSpec((1, d), lambda i: (0, 0))],
        out_shape=[_sds((s, d), F32), _sds((s, d), MXU_DTYPE), _sds((1, d), F32)],
        compiler_params=_params(("arbitrary",)),
    )(x, r, g, dh, dres)


def mm(a, b, mode="nn", res=None, out_dtype=F32, tm=None, tn=None, n_split=1, name="mm", hook=None):
    halves_a = mode == "nt" and a.ndim == 3
    halves_b = mode == "tn" and b.ndim == 3
    if halves_a:
        a_shape = (a.shape[1], 2 * a.shape[2])
    else:
        a_shape = a.shape
    b_shape = (b.shape[1], 2 * b.shape[2]) if halves_b else b.shape
    if mode == "nn":
        (m, k), (_, n) = a_shape, b_shape
    elif mode == "nt":
        (m, k), (n, _) = a_shape, b_shape
    else:
        (k, m), (_, n) = a_shape, b_shape
    tm = min(tm, m) if tm else _pick(m, (1024, 512, 256, 128))
    tn = min(tn, n) if tn else _pick(n, (512, 896, 256, 128))
    assert m % tm == 0 and n % tn == 0, (m, tm, n, tn)
    dims = {"nn": _NN, "nt": _NT, "tn": _TN}[mode]

    def body(*refs):
        a_ref, b_ref = refs[0], refs[1]
        o_ref = refs[-1]
        if halves_a:
            kh = k // 2
            acc = _dot(_mx(a_ref[0]), _mx(b_ref[:, :kh]), dims) + _dot(_mx(a_ref[1]), _mx(b_ref[:, kh:]), dims)
        else:
            acc = _dot(_mx(a_ref[...]), _mx(b_ref[...]), dims)
        if res is not None:
            acc = acc + refs[2][...]
        o_ref[...] = acc.astype(o_ref.dtype)

    a_spec = pl.BlockSpec((k, tm), lambda i, j: (0, i)) if mode == "tn" else pl.BlockSpec((tm, k), lambda i, j: (i, 0))
    b_spec = pl.BlockSpec((tn, k), lambda i, j: (j, 0)) if mode == "nt" else pl.BlockSpec((k, tn), lambda i, j: (0, j))
    if halves_a:
        a_spec = pl.BlockSpec((2, tm, k // 2), lambda i, j: (0, i, 0))
    if halves_b:
        per_half = n // 2 // tn
        assert per_half * tn * 2 == n
        b_spec = pl.BlockSpec((None, k, tn), lambda i, j: (j // per_half, 0, j % per_half))
    o_spec = pl.BlockSpec((tm, tn), lambda i, j: (i, j))
    ins, specs = [a, b], [a_spec, b_spec]
    if res is not None:
        ins.append(res)
        specs.append(o_spec)
    out_shape = _sds((m, n), out_dtype)
    if n_split > 1:
        per = n // n_split // tn
        o_spec = pl.BlockSpec((None, tm, tn), lambda i, j: (j // per, i, j % per))
        out_shape = _sds((n_split, m, n // n_split), out_dtype)
    out, carried = hosted_call(body, hook, name, (m // tm, n // tn), specs, o_spec, out_shape, [],
                               ("parallel", "parallel"), ins)
    return out if hook is None else (out, carried)


CONV_ROWS = 256
CONV_COLS = 128


def _row_iota8(cols):
    return lax.broadcasted_iota(jnp.int32, (SUBLANES, cols), 0)


def _shift_down(cur, prev8, k):
    if k == 0:
        return cur
    rolled = pltpu.roll(cur, k, 0)
    head = jnp.where(_row_iota8(cur.shape[1]) < k, pltpu.roll(prev8, k, 0), rolled[0:SUBLANES])
    return jnp.concatenate([head, rolled[SUBLANES:]], axis=0)


def _shift_up(cur, next8, k):
    if k == 0:
        return cur
    n = cur.shape[0]
    rolled = pltpu.roll(cur, n - k, 0)
    tail = jnp.where(_row_iota8(cur.shape[1]) >= SUBLANES - k, pltpu.roll(next8, SUBLANES - k, 0), rolled[n - SUBLANES:])
    return jnp.concatenate([rolled[:n - SUBLANES], tail], axis=0)


def _load_prev8(ref, i, rows):
    start = pl.multiple_of(jnp.maximum(i * rows - SUBLANES, 0), SUBLANES)
    p = ref[pl.ds(start, SUBLANES), :]
    return jnp.where(i > 0, p, jnp.zeros_like(p))


def _conv_rows(ref, w_ref, b_ref, i, rows, width):
    cur = ref[pl.ds(pl.multiple_of(i * rows, rows), rows), :]
    prev8 = _load_prev8(ref, i, rows)
    shifted = [_shift_down(cur, prev8, k) for k in range(width)]
    acc = b_ref[...] + w_ref[width - 1:width, :] * shifted[0]
    for k in range(1, width):
        acc = acc + w_ref[width - 1 - k:width - k, :] * shifted[k]
    return acc, shifted


def _conv_bwd_rows(du, next8, w_ref, width):
    acc = w_ref[width - 1:width, :] * du
    for k in range(1, width):
        acc = acc + w_ref[width - 1 - k:width - k, :] * _shift_up(du, next8, k)
    return acc


def ffn_mid_fwd(u0, cw, cb):
    s, f2 = u0.shape
    f = f2 // 2
    nt = f // CONV_COLS
    rows = min(CONV_ROWS, s)

    def body(ug_ref, uu_ref, wg_ref, wu_ref, bg_ref, bu_ref, a_ref):
        def step(i, carry):
            g, _ = _conv_rows(ug_ref, wg_ref, bg_ref, i, rows, FFN_CONV)
            u, _ = _conv_rows(uu_ref, wu_ref, bu_ref, i, rows, FFN_CONV)
            a_ref[pl.ds(pl.multiple_of(i * rows, rows), rows), :] = (_silu(g) * u).astype(a_ref.dtype)
            return carry

        lax.fori_loop(0, s // rows, step, 0)

    col = lambda off: pl.BlockSpec((s, CONV_COLS), lambda j: (0, j + off))
    wsp = lambda r, off: pl.BlockSpec((r, CONV_COLS), lambda j: (0, j + off))
    return pl.pallas_call(
        body, name="ffn_mid_fwd", grid=(nt,),
        in_specs=[col(0), col(nt), wsp(FFN_CONV, 0), wsp(FFN_CONV, nt), wsp(1, 0), wsp(1, nt)],
        out_specs=pl.BlockSpec((s, CONV_COLS), lambda j: (0, j)),
        out_shape=_sds((s, f), MXU_DTYPE), compiler_params=_params(("parallel",)),
    )(u0, u0, cw, cw, cb, cb)


def ffn_mid_bwd(u0, da, cw, cb):
    s, f2 = u0.shape
    f = f2 // 2
    nt = f // CONV_COLS
    rows = min(CONV_ROWS, s)
    nsteps = s // rows
    w = FFN_CONV

    def body(ug_ref, uu_ref, da_ref, wg_ref, wu_ref, bg_ref, bu_ref,
             du0_ref, dwg_ref, dwu_ref, dbg_ref, dbu_ref):
        zero8 = jnp.zeros((SUBLANES, CONV_COLS), F32)
        zrow = jnp.zeros((1, CONV_COLS), F32)

        def step(it, carry):
            ng, nu, accs = carry
            i = nsteps - 1 - it
            r0 = pl.multiple_of(i * rows, rows)
            g, sg = _conv_rows(ug_ref, wg_ref, bg_ref, i, rows, w)
            u, su = _conv_rows(uu_ref, wu_ref, bu_ref, i, rows, w)
            dav = da_ref[pl.ds(r0, rows), :]
            sg_val, sg_grad = _silu_and_grad(g)
            dg = dav * u * sg_grad
            du = dav * sg_val
            du0_ref[0, pl.ds(r0, rows), :] = _conv_bwd_rows(dg, ng, wg_ref, w).astype(du0_ref.dtype)
            du0_ref[1, pl.ds(r0, rows), :] = _conv_bwd_rows(du, nu, wu_ref, w).astype(du0_ref.dtype)
            new = []
            for j in range(w):
                new.append(accs[j] + jnp.sum(dg * sg[w - 1 - j], axis=0, keepdims=True))
            for j in range(w):
                new.append(accs[w + j] + jnp.sum(du * su[w - 1 - j], axis=0, keepdims=True))
            new.append(accs[2 * w] + jnp.sum(dg, axis=0, keepdims=True))
            new.append(accs[2 * w + 1] + jnp.sum(du, axis=0, keepdims=True))
            return dg[0:SUBLANES], du[0:SUBLANES], tuple(new)

        _, _, accs = lax.fori_loop(0, nsteps, step, (zero8, zero8, tuple([zrow] * (2 * w + 2))))
        dwg_ref[...] = jnp.concatenate(accs[0:w], axis=0)
        dwu_ref[...] = jnp.concatenate(accs[w:2 * w], axis=0)
        dbg_ref[...] = accs[2 * w]
        dbu_ref[...] = accs[2 * w + 1]

    col = lambda off: pl.BlockSpec((s, CONV_COLS), lambda j: (0, j + off))
    wsp = lambda r, off: pl.BlockSpec((r, CONV_COLS), lambda j: (0, j + off))
    outs = pl.pallas_call(
        body, name="ffn_mid_bwd", grid=(nt,),
        in_specs=[col(0), col(nt), col(0), wsp(w, 0), wsp(w, nt), wsp(1, 0), wsp(1, nt)],
        out_specs=[pl.BlockSpec((2, s, CONV_COLS), lambda j: (0, 0, j)), wsp(w, 0), wsp(w, 0), wsp(1, 0), wsp(1, 0)],
        out_shape=[_sds((2, s, f), MXU_DTYPE), _sds((w, f), F32), _sds((w, f), F32), _sds((1, f), F32), _sds((1, f), F32)],
        compiler_params=_params(("parallel",)),
    )(u0, u0, da, cw, cw, cb, cb)
    du0, dwg, dwu, dbg, dbu = outs
    return du0, jnp.concatenate([dwg, dwu], axis=1), jnp.concatenate([dbg, dbu], axis=1)


SB_BLOCK = 128
SB_DEAD = 110.0
SB_HEADS_PER_STEP = 4


def _split_hi_lo(x):
    hi = x.astype(MXU_DTYPE)
    lo = (x - hi.astype(F32)).astype(MXU_DTYPE)
    return hi, lo


def _dot_exact01(x, tri):
    hi, lo = _split_hi_lo(x)
    return _dot(hi, tri) + _dot(lo, tri)


def _stack_heads(pair, lane_lo):
    zero = jnp.zeros_like(pair)
    return jnp.concatenate([jnp.where(lane_lo, pair, zero), jnp.where(lane_lo, zero, pair)], axis=0)


def _unstack_heads(tall, lane_lo):
    n = tall.shape[0] // 2
    return jnp.where(lane_lo, tall[:n], tall[n:])


def _sb_logits(stacked_q, k_ref, k0, pair_cols, blk):
    z = [_dot(sq, k_ref[pl.ds(k0, blk), cols], _NT) for sq, cols in zip(stacked_q, pair_cols)]
    return jnp.concatenate(z, axis=0) * (SB_HEAD_DIM ** -0.5)


def _sb_logs(z, blk, diagonal):
    t = jnp.log(1.0 + jnp.exp(-jnp.abs(z)))
    lb = jnp.minimum(z, 0.0) - t
    lf = jnp.minimum(-z, 0.0) - t
    if not diagonal:
        return lb, lf, None
    strict = lax.broadcasted_iota(jnp.int32, z.shape, 1) < (lax.broadcasted_iota(jnp.int32, z.shape, 0) & (blk - 1))
    return lb, jnp.where(strict, lf, 0.0), strict


def _keep(strict, x):
    return x if strict is None else jnp.where(strict, x, 0.0)


def _tri(blk, upper):
    r = lax.broadcasted_iota(jnp.int32, (blk, blk), 0)
    c = lax.broadcasted_iota(jnp.int32, (blk, blk), 1)
    return jnp.where((r > c) if upper else (r < c), 1.0, 0.0).astype(MXU_DTYPE)


def _tri_sum(x, tri2):
    hi, lo = _split_hi_lo(x)
    return _dot(jnp.concatenate([hi, lo], axis=1), tri2)


def sb_fwd(qkv, hook=None):
    s = qkv.shape[0]
    blk = min(SB_BLOCK, s)
    nblk = s // blk
    nh = SB_HEADS_PER_STEP
    nstep = SB_HEADS // nh
    dh = SB_HEAD_DIM

    def body(q_ref, k_ref, v_ref, o_ref):
        suffix_tri2 = jnp.concatenate([_tri(blk, True)] * 2, axis=0)
        lane_lo = lax.broadcasted_iota(jnp.int32, (1, LANES), 1) < dh
        pairs = [slice(p * LANES, (p + 1) * LANES) for p in range(nh // 2)]

        def qstep(qi, carry):
            q0 = pl.multiple_of(qi * blk, blk)
            qst = [_stack_heads(q_ref[pl.ds(q0, blk), cols], lane_lo) for cols in pairs]

            def tile(kb, run, accs, diagonal):
                k0 = pl.multiple_of(kb * blk, blk)
                lb, lf, strict = _sb_logs(_sb_logits(qst, k_ref, k0, pairs, blk), blk, diagonal)
                sloc = _tri_sum(lf, suffix_tri2)
                a = _mx(_keep(strict, jnp.exp(lb + sloc + run)))
                accs = tuple(
                    acc + _unstack_heads(_dot(a[2 * blk * p:2 * blk * (p + 1)], v_ref[pl.ds(k0, blk), cols]), lane_lo)
                    for p, (acc, cols) in enumerate(zip(accs, pairs)))
                run = run + sloc[:, 0:1] + lf[:, 0:1]
                return run, accs, jnp.max(run) > -SB_DEAD

            def kstep(st):
                it, run, accs, _ = st
                return (it + 1, *tile(qi - it, run, accs, False))

            first = tile(qi, jnp.zeros((nh * blk, 1), F32), tuple([jnp.zeros((blk, LANES), F32)] * len(pairs)), True)
            _, _, accs, _ = lax.while_loop(lambda st: jnp.logical_and(st[0] <= qi, st[3]), kstep, (jnp.int32(1), *first))
            for acc, cols in zip(accs, pairs):
                o_ref[pl.ds(q0, blk), cols] = acc.astype(o_ref.dtype)
            return carry

        lax.fori_loop(0, nblk, qstep, 0)

    col = lambda off: pl.BlockSpec((s, nh * dh), lambda p: (0, p + off))
    out, carried = hosted_call(body, hook, "sb_fwd", (nstep,), [col(0), col(nstep), col(2 * nstep)], col(0),
                               _sds((s, D_MODEL), MXU_DTYPE), [], ("parallel",), (qkv, qkv, qkv))
    return out if hook is None else (out, carried)


def sb_bwd(qkv, do, hook=None):
    s = qkv.shape[0]
    blk = min(SB_BLOCK, s)
    nblk = s // blk
    nh = SB_HEADS_PER_STEP
    nstep = SB_HEADS // nh
    dh = SB_HEAD_DIM

    def body(q_ref, k_ref, v_ref, do_ref, dq_ref, dk_ref, dv_ref, dk_acc, dv_acc, run_ref):
        suffix_tri2 = jnp.concatenate([_tri(blk, True)] * 2, axis=0)
        prefix_tri2 = jnp.concatenate([_tri(blk, False)] * 2, axis=0)
        dk_acc[...] = jnp.zeros_like(dk_acc)
        dv_acc[...] = jnp.zeros_like(dv_acc)
        lane_lo = lax.broadcasted_iota(jnp.int32, (1, LANES), 1) < dh
        pairs = [slice(p * LANES, (p + 1) * LANES) for p in range(nh // 2)]

        def qstep(qi, carry):
            q0 = pl.multiple_of(qi * blk, blk)
            qst = [_stack_heads(q_ref[pl.ds(q0, blk), cols], lane_lo) for cols in pairs]
            dost = [_stack_heads(do_ref[pl.ds(q0, blk), cols], lane_lo) for cols in pairs]

            def enter(kb, run, diagonal):
                run_ref[kb] = run
                _, lf, _ = _sb_logs(_sb_logits(qst, k_ref, pl.multiple_of(kb * blk, blk), pairs, blk), blk, diagonal)
                run = run + jnp.sum(lf, axis=1, keepdims=True)
                return run, jnp.max(run) > -SB_DEAD

            def sweep1(st):
                it, run, _ = st
                return (it + 1, *enter(qi - it, run, False))

            nlive, _, _ = lax.while_loop(lambda st: jnp.logical_and(st[0] <= qi, st[2]), sweep1,
                                         (jnp.int32(1), *enter(qi, jnp.zeros((nh * blk, 1), F32), True)))

            def tile(kb, pg, dqs, diagonal):
                k0 = pl.multiple_of(kb * blk, blk)
                lb, lf, strict = _sb_logs(_sb_logits(qst, k_ref, k0, pairs, blk), blk, diagonal)
                sloc = _tri_sum(lf, suffix_tri2)
                a = _keep(strict, jnp.exp(lb + sloc + run_ref[kb]))
                da = jnp.concatenate([_dot(d, v_ref[pl.ds(k0, blk), cols], _NT) for d, cols in zip(dost, pairs)], axis=0)
                g = da * a
                p = pg + _tri_sum(g, prefix_tri2)
                sig = jnp.exp(lb)
                dz = _mx(_keep(strict, g * (1.0 - sig) - p * sig) * (dh ** -0.5))
                am = _mx(a)
                new_dqs = []
                for i, cols in enumerate(pairs):
                    rows = slice(2 * blk * i, 2 * blk * (i + 1))
                    new_dqs.append(dqs[i] + _unstack_heads(_dot(dz[rows], k_ref[pl.ds(k0, blk), cols]), lane_lo))
                    dk_acc[pl.ds(k0, blk), cols] += _dot(dz[rows], qst[i], _TN)
                    dv_acc[pl.ds(k0, blk), cols] += _dot(am[rows], dost[i], _TN)
                return pg + jnp.sum(g, axis=1, keepdims=True), tuple(new_dqs)

            pg, dqs = lax.fori_loop(qi + 1 - nlive, qi, lambda kb, st: tile(kb, *st, False),
                                    (jnp.zeros((nh * blk, 1), F32), tuple([jnp.zeros((blk, LANES), F32)] * len(pairs))))
            _, dqs = tile(qi, pg, dqs, True)
            for dq, cols in zip(dqs, pairs):
                dq_ref[pl.ds(q0, blk), cols] = dq.astype(dq_ref.dtype)
            return carry

        lax.fori_loop(0, nblk, qstep, 0)
        dk_ref[...] = dk_acc[...].astype(dk_ref.dtype)
        dv_ref[...] = dv_acc[...].astype(dv_ref.dtype)

    col = lambda off: pl.BlockSpec((s, nh * dh), lambda p: (0, p + off))
    (dq, dk, dv), carried = hosted_call(
        body, hook, "sb_bwd", (nstep,), [col(0), col(nstep), col(2 * nstep), col(0)], [col(0), col(0), col(0)],
        [_sds((s, D_MODEL), MXU_DTYPE)] * 3,
        [pltpu.VMEM((s, nh * dh), F32), pltpu.VMEM((s, nh * dh), F32), pltpu.VMEM((nblk, nh * blk, 1), F32)],
        ("parallel",), (qkv, qkv, qkv, do))
    return jnp.concatenate([dq, dk, dv], axis=1), carried


SSD_XBC_TILE0 = SSD_D_INNER // CONV_COLS


def ssd_conv_fwd(proj, cw, cb):
    s = proj.shape[0]
    rows = min(CONV_ROWS, s)

    def body(u_ref, w_ref, b_ref, o_ref):
        def step(i, carry):
            u, _ = _conv_rows(u_ref, w_ref, b_ref, i, rows, SSD_CONV)
            o_ref[pl.ds(pl.multiple_of(i * rows, rows), rows), :] = _silu(u)
            return carry

        lax.fori_loop(0, s // rows, step, 0)

    return pl.pallas_call(
        body, name="ssd_conv_fwd", grid=(SSD_CONV_DIM // CONV_COLS,),
        in_specs=[pl.BlockSpec((s, CONV_COLS), lambda j: (0, j + SSD_XBC_TILE0)),
                  pl.BlockSpec((SSD_CONV, CONV_COLS), lambda j: (0, j)), pl.BlockSpec((1, CONV_COLS), lambda j: (0, j))],
        out_specs=pl.BlockSpec((s, CONV_COLS), lambda j: (0, j)),
        out_shape=_sds((s, SSD_CONV_DIM), F32), compiler_params=_params(("parallel",)),
    )(proj, cw, cb)


def ssd_conv_bwd(proj, dact, cw, cb):
    s = proj.shape[0]
    rows = min(CONV_ROWS, s)
    nsteps = s // rows
    w = SSD_CONV

    def body(u_ref, da_ref, w_ref, b_ref, du_ref, dw_ref, db_ref):
        def step(it, carry):
            nxt, accs = carry
            i = nsteps - 1 - it
            r0 = pl.multiple_of(i * rows, rows)
            u, sh = _conv_rows(u_ref, w_ref, b_ref, i, rows, w)
            dconv = da_ref[pl.ds(r0, rows), :] * _silu_and_grad(u)[1]
            du_ref[pl.ds(r0, rows), :] = _conv_bwd_rows(dconv, nxt, w_ref, w).astype(du_ref.dtype)
            new = [accs[j] + jnp.sum(dconv * sh[w - 1 - j], axis=0, keepdims=True) for j in range(w)]
            new.append(accs[w] + jnp.sum(dconv, axis=0, keepdims=True))
            return dconv[0:SUBLANES], tuple(new)

        zrow = jnp.zeros((1, CONV_COLS), F32)
        _, accs = lax.fori_loop(0, nsteps, step, (jnp.zeros((SUBLANES, CONV_COLS), F32), tuple([zrow] * (w + 1))))
        dw_ref[...] = jnp.concatenate(accs[0:w], axis=0)
        db_ref[...] = accs[w]

    col = pl.BlockSpec((s, CONV_COLS), lambda j: (0, j))
    return pl.pallas_call(
        body, name="ssd_conv_bwd", grid=(SSD_CONV_DIM // CONV_COLS,),
        in_specs=[pl.BlockSpec((s, CONV_COLS), lambda j: (0, j + SSD_XBC_TILE0)), col,
                  pl.BlockSpec((w, CONV_COLS), lambda j: (0, j)), pl.BlockSpec((1, CONV_COLS), lambda j: (0, j))],
        out_specs=[col, pl.BlockSpec((w, CONV_COLS), lambda j: (0, j)), pl.BlockSpec((1, CONV_COLS), lambda j: (0, j))],
        out_shape=[_sds((s, SSD_CONV_DIM), MXU_DTYPE), _sds((w, SSD_CONV_DIM), F32), _sds((1, SSD_CONV_DIM), F32)],
        compiler_params=_params(("parallel",)),
    )(proj, dact, cw, cb)


def _split3(x):
    hi = x.astype(MXU_DTYPE)
    r1 = x - hi.astype(F32)
    mid = r1.astype(MXU_DTYPE)
    lo = (r1 - mid.astype(F32)).astype(MXU_DTYPE)
    return hi, mid, lo


def _dot01(x, m, dims=_NN, left=False):
    parts = _split3(x)
    if left:
        return _dot(m, parts[0], dims) + _dot(m, parts[1], dims) + _dot(m, parts[2], dims)
    return _dot(parts[0], m, dims) + _dot(parts[1], m, dims) + _dot(parts[2], m, dims)


def _softplus(x):
    return jnp.maximum(x, 0.0) + jnp.log1p(jnp.exp(-jnp.abs(x)))


def _ssd_consts(dt_bias, a_log, d_skip):
    pad = lambda v: jnp.pad(v.reshape(1, SSD_HEADS), ((0, 0), (0, LANES - SSD_HEADS)))
    head_of = jnp.arange(SSD_D_INNER) // SSD_HEAD_DIM
    expand = (jnp.arange(LANES)[:, None] == head_of[None, :]).astype(MXU_DTYPE)
    return dict(bias_w=pad(dt_bias), alog_w=pad(a_log), bias_c=dt_bias.reshape(SSD_HEADS, 1),
                alog_c=a_log.reshape(SSD_HEADS, 1), dskip=jnp.repeat(d_skip, SSD_HEAD_DIM).reshape(1, SSD_D_INNER),
                expand=expand, reduce=expand.T)


def _expand_heads(x):
    first = lax.broadcasted_iota(jnp.int32, (1, LANES), 1) < SSD_HEAD_DIM
    shape = (x.shape[0], LANES)
    tiles = [jnp.where(first, jnp.broadcast_to(x[:, 2 * p:2 * p + 1], shape), jnp.broadcast_to(x[:, 2 * p + 1:2 * p + 2], shape))
             for p in range(SSD_HEADS // 2)]
    return jnp.concatenate(tiles, axis=1)


def _ssd_chunk_prep(dtp, dtp_t, bias_w, alog_w, bias_c, alog_c, expand):
    L = dtp.shape[0]
    r = lax.broadcasted_iota(jnp.int32, (L, L), 0)
    c = lax.broadcasted_iota(jnp.int32, (L, L), 1)
    tril = r >= c
    lower = jnp.where(tril, 1.0, 0.0).astype(MXU_DTYPE)
    upper = jnp.where(r <= c, 1.0, 0.0).astype(MXU_DTYPE)
    dt_col = _softplus(dtp + bias_w)
    a_col = -jnp.exp(alog_w) * dt_col
    a_row = -jnp.exp(alog_c) * _softplus(dtp_t + bias_c)
    acum_col = _dot01(a_col, lower, left=True)
    acum_row = _dot01(a_row, upper)
    acum_full = _expand_heads(acum_col)
    dt_full = _expand_heads(dt_col)
    return dict(tril=tril, lower=lower, upper=upper, dt_col=dt_col, a_col=a_col, acum_col=acum_col,
                acum_row=acum_row, acum_full=acum_full, dt_full=dt_full)


def _head_mask(j):
    lane = lax.broadcasted_iota(jnp.int32, (1, LANES), 1)
    return jnp.where((lane // SSD_HEAD_DIM) == j, 1.0, 0.0)


def _decay(pre, h):
    seg = pre["acum_col"][:, h:h + 1] - pre["acum_row"][h:h + 1, :]
    return jnp.exp(jnp.where(pre["tril"], seg, -1e30))


def _ssd_specs(s, nc, rev):
    L = SSD_CHUNK
    ci = (lambda i: nc - 1 - i) if rev else (lambda i: i)
    const = lambda shape: pl.BlockSpec(shape, lambda i: (0,) * len(shape))
    return dict(
        xbc=pl.BlockSpec((L, SSD_CONV_DIM), lambda i: (ci(i), 0)),
        dtp=pl.BlockSpec((L, LANES), lambda i: (ci(i), SSD_IN_PAD // LANES - 1)),
        dtp_t=pl.BlockSpec((SSD_HEADS, L), lambda i: (0, ci(i))),
        rows=pl.BlockSpec((L, SSD_D_INNER), lambda i: (ci(i), 0)),
        state=pl.BlockSpec((1, SSD_GROUPS, SSD_STATE, 4 * SSD_HEAD_DIM), lambda i: (ci(i), 0, 0, 0)),
        consts=[const((1, LANES)), const((1, LANES)), const((SSD_HEADS, 1)), const((SSD_HEADS, 1)),
                const((1, SSD_D_INNER)), const((LANES, SSD_D_INNER)), const((SSD_D_INNER, LANES))],
    )


def _const_args(cs):
    return [cs["bias_w"], cs["alog_w"], cs["bias_c"], cs["alog_c"], cs["dskip"], cs["expand"], cs["reduce"]]


def ssd_scan_fwd(act, proj, dtp_t, cs, hook=None):
    s = act.shape[0]
    L = SSD_CHUNK
    nc = s // L
    G, N, GW = SSD_GROUPS, SSD_STATE, 4 * SSD_HEAD_DIM

    def body(act_ref, dtp_ref, dtpt_ref, bw_ref, aw_ref, bc_ref, ac_ref, dsk_ref, ex_ref, rd_ref, y_ref, st_out, st):
        @pl.when(pl.program_id(0) == 0)
        def _():
            st[...] = jnp.zeros_like(st)

        st_out[0] = st[...]
        pre = _ssd_chunk_prep(dtp_ref[...], dtpt_ref[...], bw_ref[...], aw_ref[...], bc_ref[...], ac_ref[...], ex_ref[...])
        acum_full = pre["acum_full"]
        last_full = acum_full[L - 1:L, :]
        for g in range(G):
            bg = _mx(act_ref[:, SSD_D_INNER + g * N:SSD_D_INNER + (g + 1) * N])
            cg = _mx(act_ref[:, SSD_D_INNER + G * N + g * N:SSD_D_INNER + G * N + (g + 1) * N])
            cb = _dot(cg, bg, _NT)
            for half in range(2):
                p = 2 * g + half
                cols = slice(p * LANES, (p + 1) * LANES)
                xs = act_ref[:, cols]
                xdt = xs * pre["dt_full"][:, cols]
                yd = jnp.zeros((L, LANES), F32)
                for j in range(2):
                    m = cb * _decay(pre, 2 * p + j)
                    yd = yd + _dot(_mx(m), _mx(xdt * _head_mask(j)))
                yoff = _dot(cg, _mx(st[g, :, half * LANES:(half + 1) * LANES])) * jnp.exp(acum_full[:, cols])
                y_ref[:, cols] = yd + yoff + dsk_ref[:, cols] * xs
                w = jnp.exp(last_full[:, cols] - acum_full[:, cols])
                st[g, :, half * LANES:(half + 1) * LANES] = (
                    st[g, :, half * LANES:(half + 1) * LANES] * jnp.exp(last_full[:, cols]) + _dot(bg, _mx(xdt * w), _TN))

    sp = _ssd_specs(s, nc, False)
    (y, states), carried = hosted_call(
        body, hook, "ssd_scan_fwd", (nc,), [sp["xbc"], sp["dtp"], sp["dtp_t"]] + sp["consts"],
        [sp["rows"], sp["state"]], [_sds((s, SSD_D_INNER), F32), _sds((nc, G, N, GW), F32)],
        [pltpu.VMEM((G, N, GW), F32)], ("arbitrary",), (act, proj, dtp_t, *_const_args(cs)))
    return y, states, carried


def ssd_scan_bwd(act, proj, dtp_t, cs, states, dy, hook=None):
    s = act.shape[0]
    L = SSD_CHUNK
    nc = s // L
    G, N, GW = SSD_GROUPS, SSD_STATE, 4 * SSD_HEAD_DIM

    def body(act_ref, dtp_ref, dtpt_ref, bw_ref, aw_ref, bc_ref, ac_ref, dsk_ref, ex_ref, rd_ref, st_ref, dy_ref,
             dact_ref, ddtp_ref, dalog_ref, dbias_ref, dskip_ref, dst, dxdt_ref, dac_ref):
        first = pl.program_id(0) == 0

        @pl.when(first)
        def _():
            dst[...] = jnp.zeros_like(dst)
            dalog_ref[...] = jnp.zeros_like(dalog_ref)
            dbias_ref[...] = jnp.zeros_like(dbias_ref)
            dskip_ref[...] = jnp.zeros_like(dskip_ref)

        expand, reduce = ex_ref[...], rd_ref[...]
        pre = _ssd_chunk_prep(dtp_ref[...], dtpt_ref[...], bw_ref[...], aw_ref[...], bc_ref[...], ac_ref[...], expand)
        acum_full = pre["acum_full"]
        last_full = acum_full[L - 1:L, :]
        ones = jnp.ones((2 * L, LANES), MXU_DTYPE)
        lane = lax.broadcasted_iota(jnp.int32, (L, LANES), 1)
        dacum_diag = jnp.zeros((L, LANES), F32)
        dlast_parts = []
        for g in range(G):
            bg = _mx(act_ref[:, SSD_D_INNER + g * N:SSD_D_INNER + (g + 1) * N])
            cg = _mx(act_ref[:, SSD_D_INNER + G * N + g * N:SSD_D_INNER + G * N + (g + 1) * N])
            cb = _dot(cg, bg, _NT)
            dcb = jnp.zeros((L, L), F32)
            dcg = jnp.zeros((L, N), F32)
            dbg = jnp.zeros((L, N), F32)
            for half in range(2):
                p = 2 * g + half
                cols = slice(p * LANES, (p + 1) * LANES)
                hcols = slice(half * LANES, (half + 1) * LANES)
                xs = act_ref[:, cols]
                xdt = xs * pre["dt_full"][:, cols]
                dyv = dy_ref[:, cols]
                dxdt = jnp.zeros((L, LANES), F32)
                parts = []
                for j in range(2):
                    h = 2 * p + j
                    dec = _decay(pre, h)
                    m = cb * dec
                    dyh = _mx(dyv * _head_mask(j))
                    dm = _dot(dyh, _mx(xdt), _NT)
                    dxdt = dxdt + _dot(_mx(m), dyh, _TN)
                    parts.append(_split_hi_lo(dm * m))
                    dcb = dcb + dm * dec
                (ahi, alo), (bhi, blo) = parts
                rows = _dot(jnp.concatenate([jnp.concatenate([ahi, alo], axis=1), jnp.concatenate([bhi, blo], axis=1)], axis=0), ones)
                cols_ = _dot(jnp.concatenate([jnp.concatenate([ahi, bhi], axis=1), jnp.concatenate([alo, blo], axis=1)], axis=0),
                             ones, _TN)
                d_pair = rows - cols_
                dacum_diag = jnp.where(lane == 2 * p, d_pair[:L], jnp.where(lane == 2 * p + 1, d_pair[L:], dacum_diag))
                lam = jnp.exp(acum_full[:, cols])
                stv = _mx(st_ref[0, g, :, hcols])
                z = _dot(cg, stv)
                dz = _mx(lam * dyv)
                dcg = dcg + _dot(dz, stv, _NT)
                dst_in = _dot(cg, dz, _TN)
                dsv = dst[g, :, hcols]
                w = jnp.exp(last_full[:, cols] - acum_full[:, cols])
                q = _dot(bg, _mx(dsv))
                wq = w * q
                dxdt = dxdt + wq
                wqx = wq * xdt
                dbg = dbg + _dot(_mx(xdt * w), _mx(dsv), _NT)
                elast = jnp.exp(last_full[:, cols])
                dlast_p = jnp.sum(wqx, axis=0, keepdims=True) + elast * jnp.sum(dsv * st_ref[0, g, :, hcols], axis=0, keepdims=True)
                dac_ref[:, cols] = dyv * z * lam - wqx
                dlast_parts.append(dlast_p)
                dst[g, :, hcols] = dst_in + dsv * elast
                dxdt_ref[:, cols] = dxdt
                dact_ref[:, cols] = dxdt * pre["dt_full"][:, cols] + dsk_ref[:, cols] * dyv
            dcbm = _mx(dcb)
            dact_ref[:, SSD_D_INNER + g * N:SSD_D_INNER + (g + 1) * N] = dbg + _dot(dcbm, cg, _TN)
            dact_ref[:, SSD_D_INNER + G * N + g * N:SSD_D_INNER + G * N + (g + 1) * N] = dcg + _dot(dcbm, bg)

        xs_all = act_ref[:, 0:SSD_D_INNER]
        dacum = dacum_diag + _dot_exact01(dac_ref[...], reduce)
        dlast = _dot_exact01(jnp.concatenate(dlast_parts, axis=1), reduce)
        row = lax.broadcasted_iota(jnp.int32, (L, LANES), 0)
        dacum = dacum + jnp.where(row == L - 1, dlast, 0.0)
        da_col = _dot01(dacum, pre["upper"], left=True)
        a_w = -jnp.exp(aw_ref[...])
        ddt = a_w * da_col + _dot_exact01(dxdt_ref[...] * xs_all, reduce)
        xin = dtp_ref[...] + bw_ref[...]
        ddtp = ddt * (1.0 / (1.0 + jnp.exp(-xin)))
        valid = lane < SSD_HEADS
        ddtp = jnp.where(valid, ddtp, 0.0)
        ddtp_ref[...] = ddtp
        dbias_ref[...] += jnp.sum(ddtp, axis=0, keepdims=True)
        dalog_ref[...] += jnp.sum(jnp.where(valid, da_col * pre["a_col"], 0.0), axis=0, keepdims=True)
        dskip_ref[...] += jnp.sum(_dot_exact01(dy_ref[...] * xs_all, reduce), axis=0, keepdims=True)

    sp = _ssd_specs(s, nc, True)
    acc = pl.BlockSpec((1, LANES), lambda i: (0, 0))
    outs, carried = hosted_call(
        body, hook, "ssd_scan_bwd", (nc,),
        [sp["xbc"], sp["dtp"], sp["dtp_t"]] + sp["consts"] + [sp["state"], sp["rows"]],
        [sp["xbc"], pl.BlockSpec((L, LANES), lambda i: (nc - 1 - i, 0)), acc, acc, acc],
        [_sds((s, SSD_CONV_DIM), F32), _sds((s, LANES), F32)] + [_sds((1, LANES), F32)] * 3,
        [pltpu.VMEM((G, N, GW), F32), pltpu.VMEM((L, SSD_D_INNER), F32), pltpu.VMEM((L, SSD_D_INNER), F32)],
        ("arbitrary",), (act, proj, dtp_t, *_const_args(cs), states, dy))
    return (*outs, carried)


def ssd_post_fwd(y, proj, g):
    s, d = y.shape
    ts = _pick(s, (256, 128))

    def body(y_ref, z_ref, g_ref, o_ref):
        y2 = y_ref[...] * _silu(z_ref[...])
        r = lax.rsqrt(jnp.mean(y2 * y2, axis=-1, keepdims=True) + NORM_EPS)
        o_ref[...] = (y2 * r * g_ref[...]).astype(o_ref.dtype)

    row = pl.BlockSpec((ts, d), lambda i: (i, 0))
    return pl.pallas_call(
        body, name="ssd_post_fwd", grid=(s // ts,), in_specs=[row, row, pl.BlockSpec((1, d), lambda i: (0, 0))],
        out_specs=row, out_shape=_sds((s, d), MXU_DTYPE), compiler_params=_params(("parallel",)),
    )(y, proj, g)


def ssd_post_bwd(y, proj, g, dy3):
    s, d = y.shape
    ts = _pick(s, (256, 128))

    def body(y_ref, z_ref, g_ref, d3_ref, dy_ref, dz_ref, dg_ref):
        yv, zv = y_ref[...], z_ref[...]
        sz, sgrad = _silu_and_grad(zv)
        y2 = yv * sz
        r = lax.rsqrt(jnp.mean(y2 * y2, axis=-1, keepdims=True) + NORM_EPS)
        xh = y2 * r
        d3 = d3_ref[...]
        dxh = d3 * g_ref[...]
        dy2 = r * (dxh - xh * jnp.mean(dxh * xh, axis=-1, keepdims=True))
        dy_ref[...] = dy2 * sz
        dz_ref[...] = (dy2 * yv * sgrad).astype(dz_ref.dtype)
        part = jnp.sum(d3 * xh, axis=0, keepdims=True)

        @pl.when(pl.program_id(0) == 0)
        def _():
            dg_ref[...] = part

        @pl.when(pl.program_id(0) != 0)
        def _():
            dg_ref[...] += part

    row = pl.BlockSpec((ts, d), lambda i: (i, 0))
    vec = pl.BlockSpec((1, d), lambda i: (0, 0))
    return pl.pallas_call(
        body, name="ssd_post_bwd", grid=(s // ts,), in_specs=[row, row, vec, row], out_specs=[row, row, vec],
        out_shape=[_sds((s, d), F32), _sds((s, d), MXU_DTYPE), _sds((1, d), F32)],
        compiler_params=_params(("arbitrary",)),
    )(y, proj, g, dy3)


def dt_transpose(proj):
    s = proj.shape[0]
    ts = _pick(s, (512, 256, 128))

    def body(p_ref, o_ref):
        o_ref[...] = p_ref[...].T

    return pl.pallas_call(
        body, name="dt_transpose", grid=(s // ts,),
        in_specs=[pl.BlockSpec((ts, LANES), lambda i: (i, SSD_IN_PAD // LANES - 1))],
        out_specs=pl.BlockSpec((LANES, ts), lambda i: (0, i)), out_shape=_sds((LANES, s), F32),
        compiler_params=_params(("parallel",)),
    )(proj)


def ssd_core_fwd(proj, cw, cb, dt_bias, a_log, d_skip, norm_g, hook=None):
    cs = _ssd_consts(dt_bias, a_log, d_skip)
    act = ssd_conv_fwd(proj, cw, cb)
    dtp_t = dt_transpose(proj)
    y, states, carried = ssd_scan_fwd(act, proj, dtp_t, cs, hook)
    y3 = ssd_post_fwd(y, proj, norm_g)
    return y3, (cs, act, dtp_t, y, states), carried


def ssd_core_bwd(proj, cw, cb, norm_g, saved, dy3, hook=None):
    cs, act, dtp_t, y, states = saved
    dy, dz, dnorm = ssd_post_bwd(y, proj, norm_g, dy3)
    dact, ddtp, dalog, dbias, dskip, carried = ssd_scan_bwd(act, proj, dtp_t, cs, states, dy, hook)
    dxbc, dcw, dcb = ssd_conv_bwd(proj, dact, cw, cb)
    dproj = jnp.concatenate([dz, dxbc, ddtp.astype(MXU_DTYPE)], axis=1)
    h = SSD_HEADS
    return dproj, dcw, dcb, dbias[0, :h], dalog[0, :h], dskip[0, :h], dnorm, carried


def ssd_core(proj, cw, cb, dt_bias, a_log, d_skip, norm_g, dy3):
    y3, saved, _ = ssd_core_fwd(proj, cw, cb, dt_bias, a_log, d_skip, norm_g)
    return y3, ssd_core_bwd(proj, cw, cb, norm_g, saved, dy3)


def loss_head(x, g, target):
    s, d = x.shape
    ts = _pick(s, (512, 256, 128))

    def body(x_ref, g_ref, t_ref, loss_ref, dx_ref, dxm_ref, dg_ref):
        xv = x_ref[...]
        r = lax.rsqrt(jnp.mean(xv * xv, axis=-1, keepdims=True) + NORM_EPS)
        xh = xv * r
        err = xh * g_ref[...] - t_ref[...]
        dy = err * (1.0 / d)
        dxh = dy * g_ref[...]
        dx = r * (dxh - xh * jnp.mean(dxh * xh, axis=-1, keepdims=True))
        dx_ref[...] = dx
        dxm_ref[...] = dx.astype(dxm_ref.dtype)
        part = jnp.sum(dy * xh, axis=0, keepdims=True)
        lpart = jnp.full((1, LANES), 0.5 * jnp.sum(jnp.mean(err * err, axis=-1, keepdims=True)), F32)

        @pl.when(pl.program_id(0) == 0)
        def _():
            dg_ref[...] = part
            loss_ref[...] = lpart

        @pl.when(pl.program_id(0) != 0)
        def _():
            dg_ref[...] += part
            loss_ref[...] += lpart

    row = pl.BlockSpec((ts, d), lambda i: (i, 0))
    vec = pl.BlockSpec((1, d), lambda i: (0, 0))
    return pl.pallas_call(
        body, name="loss_head", grid=(s // ts,), in_specs=[row, vec, row],
        out_specs=[pl.BlockSpec((1, LANES), lambda i: (0, 0)), row, row, vec],
        out_shape=[_sds((1, LANES), F32), _sds((s, d), F32), _sds((s, d), MXU_DTYPE), _sds((1, d), F32)],
        compiler_params=_params(("arbitrary",)),
    )(x, g, target)


def _adamw_math(w, g, m, v):
    m = ADAM_B1 * m + (1.0 - ADAM_B1) * g
    v = ADAM_B2 * v + (1.0 - ADAM_B2) * (g * g)
    m_hat = m / (1.0 - ADAM_B1 ** ADAM_STEP)
    v_hat = v / (1.0 - ADAM_B2 ** ADAM_STEP)
    return -ADAM_LR * (m_hat / (jnp.sqrt(v_hat) + ADAM_EPS) + ADAM_WD * w), m, v


def adamw(w, g, m, v, name="adamw"):
    r, c = w.shape
    tr = _pick(r, (256, 128, 64, 32, 16, 8))

    def body(w_ref, g_ref, m_ref, v_ref, d_ref, nm_ref, nv_ref):
        d_ref[...], nm_ref[...], nv_ref[...] = _adamw_math(w_ref[...], g_ref[...], m_ref[...], v_ref[...])

    blk = pl.BlockSpec((tr, c), lambda i: (i, 0))
    return pl.pallas_call(
        body, name=name, grid=(r // tr,), in_specs=[blk] * 4, out_specs=[blk] * 3,
        out_shape=[_sds((r, c), F32)] * 3, compiler_params=_params(("parallel",)),
    )(w, g, m, v)


def adamw_small(w, parts, m, v):
    n, r, c = parts.shape

    def body(w_ref, p_ref, m_ref, v_ref, g_ref, d_ref, nm_ref, nv_ref):
        g = p_ref[0]
        for k in range(1, n):
            g = g + p_ref[k]
        g_ref[...] = g
        d_ref[...], nm_ref[...], nv_ref[...] = _adamw_math(w_ref[...], g, m_ref[...], v_ref[...])

    return pl.pallas_call(
        body, name="adamw_small", out_shape=[_sds((r, c), F32)] * 4, compiler_params=_params(),
    )(w, parts, m, v)


def pair_sum(unit, recv, where):
    nchip, _, r, c = unit.shape
    tr = _pick(r, (512, 256, 176, 128, 64, 32, 16))

    def body(w_ref, a_ref, b_ref, o_ref, ob_ref):
        sm = a_ref[0, 0] + b_ref[0]
        ob_ref[0] = sm.astype(ob_ref.dtype)

        @pl.when(pl.program_id(1) == w_ref[0])
        def _():
            o_ref[...] = sm

    blk = pl.BlockSpec((1, tr, c), lambda i, s, w: (s, i, 0))
    return pl.pallas_call(
        body, name="pair_sum",
        grid_spec=pltpu.PrefetchScalarGridSpec(
            num_scalar_prefetch=1, grid=(r // tr, nchip),
            in_specs=[pl.BlockSpec((1, 1, tr, c), lambda i, s, w: (s, w[1], i, 0)), blk],
            out_specs=[pl.BlockSpec((tr, c), lambda i, s, w: (i, 0)), blk]),
        out_shape=[_sds((r, c), F32), _sds((nchip, r, c), jnp.bfloat16)],
        compiler_params=_params(("parallel", "arbitrary")),
    )(where, unit, recv)


def chip_sum(own, where, recv, layer, layers, prev=None):
    r, c = own.shape
    tr = _pick(r, (512, 256, 176, 128, 64, 32, 16))

    def body(s_ref, a_ref, b_ref, *rest):
        rest[-1][...] = a_ref[...] + b_ref[0].astype(F32) + b_ref[1].astype(F32) + b_ref[2].astype(F32)

    in_specs = [pl.BlockSpec((tr, c), lambda i, s: (i, 0)), pl.BlockSpec((3, tr, c), lambda i, s: (0, i, 0))]
    args = [where, own, recv]
    if prev is not None:
        in_specs.append(ANY)
        args.append(prev)
    return pl.pallas_call(
        body, name="chip_sum",
        grid_spec=pltpu.PrefetchScalarGridSpec(
            num_scalar_prefetch=1, grid=(r // tr,), in_specs=in_specs,
            out_specs=pl.BlockSpec((None, None, tr, c), lambda i, s: (layer, s[1], i, 0))),
        out_shape=_sds((layers, 2, r, c), F32), input_output_aliases={} if prev is None else {3: 0},
        compiler_params=_params(("parallel",)),
    )(*args)


def place_cast(w, layer, chip):
    _, a, b = w.shape
    ta = _pick(a, (512, 352, 256, 128))

    def body(c_ref, w_ref, o_ref):
        o_ref[...] = w_ref[...].astype(o_ref.dtype)

    return pl.pallas_call(
        body, name="place_cast",
        grid_spec=pltpu.PrefetchScalarGridSpec(
            num_scalar_prefetch=1, grid=(a // ta,),
            in_specs=[pl.BlockSpec((None, ta, b), lambda i, c: (layer, i, 0))],
            out_specs=pl.BlockSpec((None, ta, b), lambda i, c: (c[0], i, 0))),
        out_shape=_sds((N_CHIPS, a, b), MXU_DTYPE), compiler_params=_params(("parallel",)),
    )(chip, w)


ANY = pl.BlockSpec(memory_space=pl.ANY)
COMM = pltpu.CompilerParams(has_side_effects=True)


def _coords():
    return lax.axis_index("x"), lax.axis_index("y"), lax.axis_index("c")


def _other_chips(x, y):
    return [(1 - x, y), (x, 1 - y), (1 - x, 1 - y)]


def all_gather_8(halves, name):
    _, r, c = halves.shape

    def body(h_ref, out_ref, send_sems, recv_sems, local_sem):
        x, y, cc = _coords()
        _gather_one(h_ref.at[cc], lambda px, py, pc: out_ref.at[4 * px + 2 * py + pc],
                    lambda k: send_sems.at[k], lambda k: recv_sems.at[k], local_sem)

    return pl.pallas_call(
        body, name=name, in_specs=[ANY], out_specs=ANY, out_shape=_sds((8, r, c), halves.dtype),
        scratch_shapes=[pltpu.SemaphoreType.DMA((7,)), pltpu.SemaphoreType.DMA((7,)), pltpu.SemaphoreType.DMA],
        compiler_params=COMM,
    )(halves)


def _gather_plan(x_ref, slot, send_sem, recv_sem, local_sem):
    x, y, cc = _coords()
    me, sibling = (x, y, cc), (x, y, 1 - cc)
    chips = _other_chips(x, y)

    def copy(k, blk, to, src=None):
        return pltpu.make_async_remote_copy(
            src_ref=slot(*blk) if src is None else src, dst_ref=slot(*blk),
            send_sem=send_sem(k), recv_sem=recv_sem(k), device_id=to, device_id_type=MESH)

    mine = pltpu.make_async_copy(x_ref, slot(*me), local_sem)
    first = [copy(0, me, sibling, src=x_ref)] + [copy(1 + j, me, (*chip, cc), src=x_ref) for j, chip in enumerate(chips)]
    passed = [copy(4 + j, (*chip, cc), sibling) for j, chip in enumerate(chips)]
    over_ici = [copy(1 + j, (*chip, cc), me) for j, chip in enumerate(chips)]
    from_sibling = [copy(0, sibling, me)] + [copy(4 + j, (*chip, 1 - cc), me) for j, chip in enumerate(chips)]
    return mine, first, passed, over_ici, from_sibling


def _gather_run(plans):
    for mine, first, _, _, _ in plans:
        mine.start()
        for cp in first:
            cp.start()
    for j in range(3):
        for _, _, passed, over_ici, _ in plans:
            over_ici[j].wait_recv()
            passed[j].start()
    for mine, first, passed, _, from_sibling in plans:
        for cp in from_sibling:
            cp.wait_recv()
        for cp in first + passed:
            cp.wait_send()
        mine.wait()


def _gather_one(x_ref, slot, send_sem, recv_sem, local_sem):
    _gather_run([_gather_plan(x_ref, slot, send_sem, recv_sem, local_sem)])


def gather_hook(items):
    n = len(items)

    def plan(refs, send_sems, recv_sems):
        x, y, cc = _coords()

        def copy(i, k, px, py, pc, to):
            blk = refs[i].at[2 * px + py, pc]
            return pltpu.make_async_remote_copy(src_ref=blk, dst_ref=blk, send_sem=send_sems.at[i, k],
                                                recv_sem=recv_sems.at[i, k], device_id=to, device_id_type=MESH)

        chips = _other_chips(x, y)
        first = [copy(i, j, x, y, cc, (*chip, cc)) for i in range(n) for j, chip in enumerate(chips)]
        return copy, chips, first, (x, y, cc)

    def start(refs, new, sems):
        for cp in plan(refs, *sems)[2]:
            cp.start()

    def finish(refs, new, sems):
        copy, chips, first, (x, y, cc) = plan(refs, *sems)
        passed = []
        for j, chip in enumerate(chips):
            for i in range(n):
                copy(i, j, *chip, cc, (x, y, cc)).wait_recv()
                passed.append(copy(i, 3 + j, *chip, cc, (x, y, 1 - cc)))
                passed[-1].start()
        for j, chip in enumerate(chips):
            for i in range(n):
                copy(i, 3 + j, *chip, 1 - cc, (x, y, cc)).wait_recv()
        for cp in first + passed:
            cp.wait_send()

    return dict(arrays=list(items), new=[], start=start, finish=finish, in_place=True,
                sems=[pltpu.SemaphoreType.DMA((n, 6)), pltpu.SemaphoreType.DMA((n, 6))])


def hosted_call(body, hook, name, grid, in_specs, out_specs, out_shape, scratch_shapes, sem, args):
    single = not isinstance(out_shape, (list, tuple))
    out_specs_l = [out_specs] if single else list(out_specs)
    out_shape_l = [out_shape] if single else list(out_shape)
    if hook is None:
        res = pl.pallas_call(body, name=name, grid=grid, in_specs=list(in_specs), out_specs=out_specs, out_shape=out_shape,
                             scratch_shapes=list(scratch_shapes), compiler_params=_params(sem))(*args)
        return res, []
    items, new = hook["arrays"], hook["new"]
    k, kn, n_in, n_out, n_scr = len(items), len(new), len(in_specs), len(out_specs_l), len(scratch_shapes)
    ka = k if hook["in_place"] else 0

    def full(*refs):
        ins = refs[:n_in]
        base = n_in + k
        outs = refs[base:base + n_out]
        hrefs = refs[base + n_out:base + n_out + ka] if ka else refs[n_in:base]
        nrefs = refs[base + n_out + ka:base + n_out + ka + kn]
        scr = refs[base + n_out + ka + kn:base + n_out + ka + kn + n_scr]
        sems = refs[base + n_out + ka + kn + n_scr:]
        ids = [pl.program_id(d) for d in range(len(grid))]
        first = functools.reduce(jnp.logical_and, [i == 0 for i in ids])
        last = functools.reduce(jnp.logical_and, [i == g - 1 for i, g in zip(ids, grid)])

        @pl.when(first)
        def _():
            hook["start"](hrefs, nrefs, sems)

        body(*ins, *outs, *scr)

        @pl.when(last)
        def _():
            hook["finish"](hrefs, nrefs, sems)

    res = pl.pallas_call(
        full, name=name, grid=grid, in_specs=list(in_specs) + [ANY] * k, out_specs=out_specs_l + [ANY] * (ka + kn),
        out_shape=out_shape_l + [_sds(a.shape, a.dtype) for a in items[:ka]] + list(new),
        input_output_aliases={n_in + i: n_out + i for i in range(ka)},
        scratch_shapes=list(scratch_shapes) + hook["sems"],
        compiler_params=pltpu.CompilerParams(dimension_semantics=("arbitrary",) * len(grid),
                                             vmem_limit_bytes=VMEM_LIMIT, has_side_effects=True),
    )(*args, *items)
    return (res[0] if single else list(res[:n_out])), list(res[n_out:])


def comm_call(hook, name):
    k, kn = len(hook["arrays"]), len(hook["new"])
    ka = k if hook["in_place"] else 0

    def body(*refs):
        hrefs = refs[k:k + ka] if ka else refs[:k]
        hook["start"](hrefs, refs[k + ka:k + ka + kn], refs[k + ka + kn:])
        hook["finish"](hrefs, refs[k + ka:k + ka + kn], refs[k + ka + kn:])

    return list(pl.pallas_call(
        body, name=name, in_specs=[ANY] * k, out_specs=[ANY] * (ka + kn),
        out_shape=[_sds(a.shape, a.dtype) for a in hook["arrays"][:ka]] + list(hook["new"]),
        input_output_aliases={i: i for i in range(ka)}, scratch_shapes=hook["sems"], compiler_params=COMM,
    )(*hook["arrays"]))


def merge_hooks(hooks):
    hooks = [h for h in hooks if h is not None]
    if len(hooks) < 2:
        return hooks[0] if hooks else None

    def parts(refs, new, sems):
        out, a, b, c = [], 0, 0, 0
        for h in hooks:
            na, nn, ns = len(h["arrays"]), len(h["new"]), len(h["sems"])
            out.append((refs[a:a + na], new[b:b + nn], sems[c:c + ns]))
            a, b, c = a + na, b + nn, c + ns
        return out

    def start(refs, new, sems):
        for h, p in zip(hooks, parts(refs, new, sems)):
            h["start"](*p)

    def finish(refs, new, sems):
        for h, p in zip(hooks, parts(refs, new, sems)):
            h["finish"](*p)

    assert len({h["in_place"] for h in hooks}) == 1
    return dict(arrays=[a for h in hooks for a in h["arrays"]], new=[a for h in hooks for a in h["new"]],
                sems=[a for h in hooks for a in h["sems"]], start=start, finish=finish, in_place=hooks[0]["in_place"])


def split_carried(hooks, carried):
    hooks = [h for h in hooks if h is not None]
    off = sum(len(h["arrays"]) for h in hooks if h["in_place"])
    out = []
    for h in hooks:
        out.append(carried[off:off + len(h["new"])])
        off += len(h["new"])
    return out


def pair_swap_hook(units):
    n = len(units)

    def plan(refs, new, send_sems, recv_sems):
        x, y, cc = _coords()
        return [pltpu.make_async_remote_copy(src_ref=refs[i].at[:, 1 - cc], dst_ref=new[i], send_sem=send_sems.at[i],
                                             recv_sem=recv_sems.at[i], device_id=(x, y, 1 - cc), device_id_type=MESH)
                for i in range(n)]

    def start(refs, new, sems):
        for cp in plan(refs, new, *sems):
            cp.start()

    def finish(refs, new, sems):
        for cp in plan(refs, new, *sems):
            cp.wait()

    return dict(arrays=list(units), new=[_sds((u.shape[0],) + u.shape[2:], u.dtype) for u in units], start=start,
                finish=finish, in_place=False, sems=[pltpu.SemaphoreType.DMA((n,)), pltpu.SemaphoreType.DMA((n,))])


def chip_exchange_hook(units):
    n = len(units)

    def plan(refs, new, send_sems, recv_sems):
        x, y, cc = _coords()
        return [pltpu.make_async_remote_copy(
            src_ref=refs[i].at[2 * px + py], dst_ref=new[i].at[k], send_sem=send_sems.at[i, k],
            recv_sem=recv_sems.at[i, k], device_id=(px, py, cc), device_id_type=MESH)
            for i in range(n) for k, (px, py) in enumerate(_other_chips(x, y))]

    def start(refs, new, sems):
        for cp in plan(refs, new, *sems):
            cp.start()

    def finish(refs, new, sems):
        for cp in plan(refs, new, *sems):
            cp.wait()

    return dict(arrays=list(units), new=[_sds((3,) + u.shape[1:], u.dtype) for u in units], start=start,
                finish=finish, in_place=False, sems=[pltpu.SemaphoreType.DMA((n, 3)), pltpu.SemaphoreType.DMA((n, 3))])


def grad_half_swap(grads):
    n = len(grads)

    def body(*refs):
        outs, send_sems, recv_sems = refs[n:2 * n], refs[2 * n], refs[2 * n + 1]
        x, y, cc = _coords()
        cps = [pltpu.make_async_remote_copy(
            src_ref=outs[i].at[:, cc], dst_ref=outs[i].at[:, cc], send_sem=send_sems.at[i], recv_sem=recv_sems.at[i],
            device_id=(x, y, 1 - cc), device_id_type=MESH) for i in range(n)]
        for cp in cps:
            cp.start()
        for i, cp in enumerate(cps):
            cp.wait_send()
            pltpu.make_async_remote_copy(
                src_ref=outs[i].at[:, 1 - cc], dst_ref=outs[i].at[:, 1 - cc], send_sem=send_sems.at[i],
                recv_sem=recv_sems.at[i], device_id=(x, y, 1 - cc), device_id_type=MESH).wait_recv()

    return pl.pallas_call(
        body, name="grad_half_swap", in_specs=[ANY] * n, out_specs=[ANY] * n,
        out_shape=[_sds(g.shape, g.dtype) for g in grads], input_output_aliases={i: i for i in range(n)},
        scratch_shapes=[pltpu.SemaphoreType.DMA((n,)), pltpu.SemaphoreType.DMA((n,))], compiler_params=COMM,
    )(*grads)


N_CHIPS = 4
PACK_COLS = 1024
BIG = ("ssd_w_in", "ssd_w_out", "sb_w_qkv", "sb_w_out", "ffn_w_in", "ffn_w_out")
CONVW = ("ssd_conv_w", "ffn_conv_w")
COL_SHARDED = ("ssd_w_in", "sb_w_qkv", "ffn_w_in", "ssd_conv_w", "ffn_conv_w")
SMALL = ("mix_norm", "ffn_norm", "final_norm", "ssd_conv_b", "ssd_dt_bias", "ssd_a_log", "ssd_d", "ssd_norm", "ffn_conv_b")
WEIGHTS = ("mix_norm", "ffn_norm", "final_norm", "ssd_w_in", "ssd_conv_w", "ssd_conv_b", "ssd_dt_bias", "ssd_a_log",
           "ssd_d", "ssd_norm", "ssd_w_out", "sb_w_qkv", "sb_w_out", "ffn_w_in", "ffn_conv_w", "ffn_conv_b", "ffn_w_out")


def _to_rows(flat, multiple):
    rows = -(-flat.shape[-1] // PACK_COLS)
    rows = -(-rows // multiple) * multiple
    pad = rows * PACK_COLS - flat.shape[-1]
    return jnp.pad(flat, [(0, pad)]).reshape(rows, PACK_COLS)


def _unshard(name, stacked):
    l, n, a, b = stacked.shape
    if name in COL_SHARDED:
        return jnp.transpose(stacked, (0, 2, 1, 3)).reshape(l, a, n * b)
    return stacked.reshape(l, n * a, b)


def _gather_conv_weights(w):
    flat = jnp.concatenate([w[n].reshape(-1) for n in CONVW])
    rows = _to_rows(flat, 16)
    got = all_gather_8(rows.reshape(2, rows.shape[0] // 2, PACK_COLS), "gather_conv_weights").reshape(N_CHIPS, -1)
    out, off = {}, 0
    for n in CONVW:
        l, a, b = w[n].shape
        out[n] = _unshard(n, jnp.moveaxis(got[:, off:off + w[n].size].reshape(N_CHIPS, l, a, b), 0, 1))
        off += w[n].size
    return out


def _finish_big_grads(pair_sums, from_chips, layout):
    cc = lax.axis_index("c").astype(jnp.int32)
    chip = (2 * lax.axis_index("x") + lax.axis_index("y")).astype(jnp.int32)
    where = jnp.stack([chip, cc])
    nlayers = [1 + max(l for k, l in layout if k == wi) for wi in range(1 + max(k for k, _ in layout))]
    grads = [None] * len(nlayers)
    for (wi, l), p, r in zip(layout, pair_sums, from_chips):
        grads[wi] = chip_sum(p, where, r, l, nlayers[wi], grads[wi])
    return grad_half_swap(grads)


def kernel(x, mix_norm, ffn_norm, final_norm, ssd_w_in, ssd_conv_w, ssd_conv_b, ssd_dt_bias, ssd_a_log, ssd_d, ssd_norm, ssd_w_out, sb_w_qkv, sb_w_out, ffn_w_in, ffn_conv_w, ffn_conv_b, ffn_w_out, loss_target, m_mix_norm, m_ffn_norm, m_final_norm, m_ssd_w_in, m_ssd_conv_w, m_ssd_conv_b, m_ssd_dt_bias, m_ssd_a_log, m_ssd_d, m_ssd_norm, m_ssd_w_out, m_sb_w_qkv, m_sb_w_out, m_ffn_w_in, m_ffn_conv_w, m_ffn_conv_b, m_ffn_w_out, v_mix_norm, v_ffn_norm, v_final_norm, v_ssd_w_in, v_ssd_conv_w, v_ssd_conv_b, v_ssd_dt_bias, v_ssd_a_log, v_ssd_d, v_ssd_norm, v_ssd_w_out, v_sb_w_qkv, v_sb_w_out, v_ffn_w_in, v_ffn_conv_w, v_ffn_conv_b, v_ffn_w_out):
    given = dict(locals())
    w = {n: given[n] for n in WEIGHTS}
    mom = {n: given["m_" + n] for n in WEIGHTS}
    var = {n: given["v_" + n] for n in WEIGHTS}
    chip = 2 * lax.axis_index("x") + lax.axis_index("y")

    chip1 = chip.reshape(1).astype(jnp.int32)
    fw = _gather_conv_weights(w)
    row = lambda v: v.reshape(1, -1)

    def placed(n, l):
        _, a, b = w[n].shape
        return place_cast(w[n], l, chip1).reshape(N_CHIPS, 2, a // 2, b)

    def mixer_items(i):
        return [(n, i // 2) for n in (("ssd_w_in", "ssd_w_out") if i % 2 == 0 else ("sb_w_qkv", "sb_w_out"))]

    def ffn_items(i):
        return [("ffn_w_in", i), ("ffn_w_out", i)]

    def hook_for(items):
        return gather_hook([placed(n, l) for n, l in items]) if items else None

    lw = {}

    def arrived(items, arrays):
        for (n, l), arr in zip(items, arrays):
            g4 = arr.reshape(N_CHIPS, -1, arr.shape[-1])
            if n in COL_SHARDED:
                full = jnp.transpose(g4, (1, 0, 2)).reshape(g4.shape[1], -1)
            else:
                full = g4.reshape(-1, g4.shape[2])
            if n == "ssd_w_in":
                full = jnp.pad(full, ((0, 0), (0, SSD_IN_PAD - SSD_IN_DIM)))
            lw[(n, l)] = full

    first_items = [("ssd_w_in", 0)]
    carry = {
        (0, "mm_in"): [("ssd_w_out", 0), ("ffn_w_in", 0)],
        (0, "scan"): [("ffn_w_out", 0)] + mixer_items(1),
        (0, "ffn_in"): ffn_items(1),
        (1, "sb"): mixer_items(2) + ffn_items(2),
        (2, "mm_in"): mixer_items(3),
        (2, "scan"): ffn_items(3),
    }
    arrived(first_items, comm_call(hook_for(first_items), "gather_first"))

    def carrying(i, slot, call):
        items = carry.get((i, slot), [])
        if not items:
            return call(None)
        out, got = call(hook_for(items))
        arrived(items, got)
        return out

    xcur = x[0]
    saved = []
    for i in range(DEPTH):
        j = i // 2
        h, r = rms_fwd(xcur, row(mix_norm[i]))
        if i % 2 == 0:
            proj = carrying(i, "mm_in", lambda hk: mm(h, lw[("ssd_w_in", j)], tm=2048, tn=896, name="mm_ssd_in", hook=hk))
            items = carry.get((i, "scan"), [])
            y3, core, got = ssd_core_fwd(proj, fw["ssd_conv_w"][j], row(ssd_conv_b[j]), ssd_dt_bias[j], ssd_a_log[j],
                                         ssd_d[j], row(ssd_norm[j]), hook_for(items))
            arrived(items, got)
            x1 = mm(y3, lw[("ssd_w_out", j)], res=xcur, name="mm_ssd_out")
            mix = (proj, y3, core)
        else:
            qkv = mm(h, lw[("sb_w_qkv", j)], out_dtype=MXU_DTYPE, tm=2048, name="mm_sb_qkv")
            o = carrying(i, "sb", lambda hk: sb_fwd(qkv, hk))
            x1 = mm(o, lw[("sb_w_out", j)], res=xcur, name="mm_sb_out")
            mix = (qkv, o)
        h2, r2 = rms_fwd(x1, row(ffn_norm[i]))
        u0 = carrying(i, "ffn_in", lambda hk: mm(h2, lw[("ffn_w_in", i)], tm=2048, name="mm_ffn_in", hook=hk))
        a = ffn_mid_fwd(u0, fw["ffn_conv_w"][i], row(ffn_conv_b[i]))
        x2 = mm(a, lw[("ffn_w_out", i)], res=x1, name="mm_ffn_out")
        saved.append((xcur, h, r, mix, x1, h2, r2, u0, a))
        xcur = x2
    loss_part, dx, dxm, d_final = loss_head(xcur, row(final_norm), loss_target[0])

    gl = {n: [None] * w[n].shape[0] for n in WEIGHTS if n != "final_norm"}
    units = {n: [None] * w[n].shape[0] for n in BIG}

    def unit_of(g4):
        return g4.reshape(N_CHIPS, 2, g4.shape[1] // 2, g4.shape[2])

    where = jnp.stack([chip, lax.axis_index("c")]).astype(jnp.int32)
    pair_f32, wire, from_chips = {}, {}, {}

    def pair_sums(keys, swapped):
        for (n, l), got in zip(keys, swapped):
            pair_f32[(n, l)], wire[(n, l)] = pair_sum(units[n][l], got, where)

    for i in reversed(range(DEPTH)):
        j = i // 2
        x0, h, r, mix, x1, h2, r2, u0, a = saved[i]
        units["ffn_w_out"][i] = unit_of(mm(a, dxm, "tn", tm=1408, name="mm_d_ffn_out").reshape(N_CHIPS, -1, D_MODEL))
        da = mm(dxm, lw[("ffn_w_out", i)], "nt", tn=1408, name="mm_da_ffn")
        du0, gl["ffn_conv_w"][i], dcb = ffn_mid_bwd(u0, da, fw["ffn_conv_w"][i], row(ffn_conv_b[i]))
        gl["ffn_conv_b"][i] = dcb[0]
        units["ffn_w_in"][i] = unit_of(mm(h2, du0, "tn", tn=1408, tm=512, n_split=N_CHIPS, name="mm_d_ffn_in"))
        keys_f = ffn_items(i)
        swap = pair_swap_hook([units[n][l] for n, l in keys_f])
        dh2, carried = mm(du0, lw[("ffn_w_in", i)], "nt", name="mm_dh_ffn", hook=swap)
        pair_sums(keys_f, split_carried([swap], carried)[0])
        dx1, dx1m, dg = rms_bwd(x1, r2, row(ffn_norm[i]), dh2, dx)
        gl["ffn_norm"][i] = dg[0]
        keys_up = mixer_items(i + 1) if i + 1 < DEPTH else []
        exchanges = [chip_exchange_hook([wire[k] for k in keys_f]),
                     chip_exchange_hook([wire[k] for k in keys_up]) if keys_up else None]
        if i % 2 == 0:
            proj, y3, core = mix
            units["ssd_w_out"][j] = unit_of(mm(y3, dx1m, "tn", name="mm_d_ssd_out").reshape(N_CHIPS, -1, D_MODEL))
            dy3 = mm(dx1m, lw[("ssd_w_out", j)], "nt", name="mm_dy3_ssd")
            (dproj, gl["ssd_conv_w"][j], dcb, gl["ssd_dt_bias"][j], gl["ssd_a_log"][j], gl["ssd_d"][j], dnorm,
             carried) = ssd_core_bwd(proj, fw["ssd_conv_w"][j], row(ssd_conv_b[j]), row(ssd_norm[j]), core, dy3,
                                     merge_hooks(exchanges))
            gl["ssd_conv_b"][j] = dcb[0]
            gl["ssd_norm"][j] = dnorm[0]
            dw_in = mm(h, dproj, "tn", tn=896, name="mm_d_ssd_in")[:, :SSD_IN_DIM]
            units["ssd_w_in"][j] = unit_of(jnp.transpose(dw_in.reshape(D_MODEL, N_CHIPS, -1), (1, 0, 2)))
            dmix, w_in, dh_name = dproj, lw[("ssd_w_in", j)], "mm_dh_ssd"
        else:
            qkv, o = mix
            units["sb_w_out"][j] = unit_of(mm(o, dx1m, "tn", name="mm_d_sb_out").reshape(N_CHIPS, -1, D_MODEL))
            do = mm(dx1m, lw[("sb_w_out", j)], "nt", out_dtype=MXU_DTYPE, name="mm_do_sb")
            dqkv, carried = sb_bwd(qkv, do, merge_hooks(exchanges))
            units["sb_w_qkv"][j] = unit_of(mm(h, dqkv, "tn", tn=768, n_split=N_CHIPS, name="mm_d_sb_qkv"))
            dmix, w_in, dh_name = dqkv, lw[("sb_w_qkv", j)], "mm_dh_sb"
        got = split_carried(exchanges, carried)
        from_chips.update(zip(keys_f, got[0]))
        if keys_up:
            from_chips.update(zip(keys_up, got[1]))
        keys_m = mixer_items(i)
        swap = pair_swap_hook([units[n][l] for n, l in keys_m])
        dh, carried = mm(dmix, w_in, "nt", name=dh_name, hook=swap)
        pair_sums(keys_m, split_carried([swap], carried)[0])
        dx, dxm, dg = rms_bwd(x0, r, row(mix_norm[i]), dh, dx1)
        gl["mix_norm"][i] = dg[0]
    last = chip_exchange_hook([wire[k] for k in mixer_items(0)])
    from_chips.update(zip(mixer_items(0), comm_call(last, "grad_exchange_last")))

    layout = [(k, l) for k, n in enumerate(BIG) for l in range(w[n].shape[0])]
    reduced = _finish_big_grads([pair_f32[(BIG[k], l)] for k, l in layout], [from_chips[(BIG[k], l)] for k, l in layout],
                                layout)
    g, delta, new_m, new_v = {}, {}, {}, {}
    two_d = lambda t: t.reshape(-1, t.shape[-1])
    for n, red in zip(BIG, reduced):
        g[n] = red.reshape(w[n].shape)
        d2, m2, v2 = adamw(two_d(w[n]), two_d(g[n]), two_d(mom[n]), two_d(var[n]), name="adamw_" + n)
        delta[n], new_m[n], new_v[n] = d2.reshape(w[n].shape), m2.reshape(w[n].shape), v2.reshape(w[n].shape)

    small_g = {n: jnp.stack(gl[n]) for n in SMALL + CONVW if n != "final_norm"}
    small_g["final_norm"] = d_final[0]
    zeros_of = lambda n: jnp.zeros((small_g[n].size,), F32)

    def small_pack(d, extra):
        parts = [d[n].reshape(-1) for n in SMALL] + [extra]
        parts += [(d[n].reshape(-1) if d is small_g else zeros_of(n)) for n in CONVW]
        return _to_rows(jnp.concatenate(parts), 16)

    part = small_pack(small_g, loss_part[0, 0:1])
    parts = all_gather_8(jnp.stack([part, part]), "gather_small_grads")
    zero = jnp.zeros((1,), F32)
    gs, ds, ms, vs = adamw_small(small_pack(w, zero), parts, small_pack(mom, zero), small_pack(var, zero))
    gs_flat = gs.reshape(-1)
    off = 0
    for n in SMALL:
        size = w[n].size
        for dst, src in ((g, gs), (delta, ds), (new_m, ms), (new_v, vs)):
            dst[n] = src.reshape(-1)[off:off + size].reshape(w[n].shape)
        off += size
    loss = gs_flat[off]
    off += 1
    for n in CONVW:
        size = small_g[n].size
        b = w[n].shape[-1]
        g[n] = lax.dynamic_slice_in_dim(gs_flat[off:off + size].reshape(small_g[n].shape), chip * b, b, axis=2)
        d2, m2, v2 = adamw(two_d(w[n]), two_d(g[n]), two_d(mom[n]), two_d(var[n]), name="adamw_" + n)
        delta[n], new_m[n], new_v[n] = d2.reshape(w[n].shape), m2.reshape(w[n].shape), v2.reshape(w[n].shape)
        off += size

    return (loss, dx[None], *[g[n] for n in WEIGHTS], *[delta[n] for n in WEIGHTS],
            *[new_m[n] for n in WEIGHTS], *[new_v[n] for n in WEIGHTS])
```

```python
import functools

import jax
import jax.numpy as jnp
from jax import lax
from jax.experimental import pallas as pl
from jax.experimental.pallas import tpu as pltpu

F32 = jnp.float32
MXU_DTYPE = jnp.bfloat16
HIGHEST = lax.Precision.HIGHEST

D_MODEL = 1024
DEPTH = 4
NORM_EPS = 1e-6
SSD_D_INNER = 2048
SSD_HEAD_DIM = 64
SSD_HEADS = 32
SSD_GROUPS = 8
SSD_STATE = 128
SSD_CONV = 4
SSD_CHUNK = 128
SSD_CONV_DIM = 4096
SSD_IN_DIM = 6176
SSD_IN_PAD = 6272
SB_HEADS = 16
SB_HEAD_DIM = 64
FFN_D_FF = 2816
FFN_CONV = 3
ADAM_LR, ADAM_B1, ADAM_B2, ADAM_EPS, ADAM_WD, ADAM_STEP = 0.001, 0.9, 0.999, 1e-08, 0.01, 10

LANES = 128
SUBLANES = 8
VMEM_LIMIT = 56 * 1024 * 1024
MESH = pl.DeviceIdType.MESH


def _params(sem=None):
    return pltpu.CompilerParams(dimension_semantics=sem, vmem_limit_bytes=VMEM_LIMIT)


def _sds(shape, dtype):
    return jax.ShapeDtypeStruct(shape, dtype)


def _dot(a, b, dims=(((1,), (0,)), ((), ())), precision=None):
    return lax.dot_general(a, b, dims, precision=precision, preferred_element_type=F32)


_NN = (((1,), (0,)), ((), ()))
_NT = (((1,), (1,)), ((), ()))
_TN = (((0,), (0,)), ((), ()))


def _mx(a):
    return a.astype(MXU_DTYPE)


def _silu(x):
    return x * (1.0 / (1.0 + jnp.exp(-x)))


def _silu_and_grad(x):
    s = 1.0 / (1.0 + jnp.exp(-x))
    return x * s, s * (1.0 + x * (1.0 - s))


def _pick(n, cands):
    for c in cands:
        if n % c == 0:
            return c
    return n


def rms_fwd(x, g):
    s, d = x.shape
    ts = _pick(s, (512, 256, 128))

    def body(x_ref, g_ref, h_ref, r_ref):
        xv = x_ref[...]
        r = lax.rsqrt(jnp.mean(xv * xv, axis=-1, keepdims=True) + NORM_EPS)
        h_ref[...] = (xv * r * g_ref[...]).astype(h_ref.dtype)
        r_ref[...] = r

    return pl.pallas_call(
        body, name="rms_fwd", grid=(s // ts,),
        in_specs=[pl.BlockSpec((ts, d), lambda i: (i, 0)), pl.BlockSpec((1, d), lambda i: (0, 0))],
        out_specs=[pl.BlockSpec((ts, d), lambda i: (i, 0)), pl.BlockSpec((ts, 1), lambda i: (i, 0))],
        out_shape=[_sds((s, d), MXU_DTYPE), _sds((s, 1), F32)],
        compiler_params=_params(("parallel",)),
    )(x, g)


def rms_bwd(x, r, g, dh, dres):
    s, d = x.shape
    ts = _pick(s, (512, 256, 128))

    def body(x_ref, r_ref, g_ref, dh_ref, dres_ref, dx_ref, dxm_ref, dg_ref):
        xh = x_ref[...] * r_ref[...]
        dhv = dh_ref[...]
        dxh = dhv * g_ref[...]
        dx = dres_ref[...] + r_ref[...] * (dxh - xh * jnp.mean(dxh * xh, axis=-1, keepdims=True))
        dx_ref[...] = dx
        dxm_ref[...] = dx.astype(dxm_ref.dtype)
        part = jnp.sum(dhv * xh, axis=0, keepdims=True)

        @pl.when(pl.program_id(0) == 0)
        def _():
            dg_ref[...] = part

        @pl.when(pl.program_id(0) != 0)
        def _():
            dg_ref[...] += part

    row = pl.BlockSpec((ts, d), lambda i: (i, 0))
    return pl.pallas_call(
        body, name="rms_bwd", grid=(s // ts,),
        in_specs=[row, pl.BlockSpec((ts, 1), lambda i: (i, 0)), pl.BlockSpec((1, d), lambda i: (0, 0)), row, row],
        out_specs=[row, row, pl.BlockSpec((1, d), lambda i: (0, 0))],
        out_shape=[_sds((s, d), F32), _sds((s, d), MXU_DTYPE), _sds((1, d), F32)],
        compiler_params=_params(("arbitrary",)),
    )(x, r, g, dh, dres)


def mm(a, b, mode="nn", res=None, out_dtype=F32, tm=None, tn=None, n_split=1, name="mm", hook=None):
    halves_a = mode == "nt" and a.ndim == 3
    halves_b = mode == "tn" and b.ndim == 3
    if halves_a:
        a_shape = (a.shape[1], 2 * a.shape[2])
    else:
        a_shape = a.shape
    b_shape = (b.shape[1], 2 * b.shape[2]) if halves_b else b.shape
    if mode == "nn":
        (m, k), (_, n) = a_shape, b_shape
    elif mode == "nt":
        (m, k), (n, _) = a_shape, b_shape
    else:
        (k, m), (_, n) = a_shape, b_shape
    tm = min(tm, m) if tm else _pick(m, (1024, 512, 256, 128))
    tn = min(tn, n) if tn else _pick(n, (512, 896, 256, 128))
    assert m % tm == 0 and n % tn == 0, (m, tm, n, tn)
    dims = {"nn": _NN, "nt": _NT, "tn": _TN}[mode]

    def body(*refs):
        a_ref, b_ref = refs[0], refs[1]
        o_ref = refs[-1]
        if halves_a:
            kh = k // 2
            acc = _dot(_mx(a_ref[0]), _mx(b_ref[:, :kh]), dims) + _dot(_mx(a_ref[1]), _mx(b_ref[:, kh:]), dims)
        else:
            acc = _dot(_mx(a_ref[...]), _mx(b_ref[...]), dims)
        if res is not None:
            acc = acc + refs[2][...]
        o_ref[...] = acc.astype(o_ref.dtype)

    a_spec = pl.BlockSpec((k, tm), lambda i, j: (0, i)) if mode == "tn" else pl.BlockSpec((tm, k), lambda i, j: (i, 0))
    b_spec = pl.BlockSpec((tn, k), lambda i, j: (j, 0)) if mode == "nt" else pl.BlockSpec((k, tn), lambda i, j: (0, j))
    if halves_a:
        a_spec = pl.BlockSpec((2, tm, k // 2), lambda i, j: (0, i, 0))
    if halves_b:
        per_half = n // 2 // tn
        assert per_half * tn * 2 == n
        b_spec = pl.BlockSpec((None, k, tn), lambda i, j: (j // per_half, 0, j % per_half))
    o_spec = pl.BlockSpec((tm, tn), lambda i, j: (i, j))
    ins, specs = [a, b], [a_spec, b_spec]
    if res is not None:
        ins.append(res)
        specs.append(o_spec)
    out_shape = _sds((m, n), out_dtype)
    if n_split > 1:
        per = n // n_split // tn
        o_spec = pl.BlockSpec((None, tm, tn), lambda i, j: (j // per, i, j % per))
        out_shape = _sds((n_split, m, n // n_split), out_dtype)
    out, carried = hosted_call(body, hook, name, (m // tm, n // tn), specs, o_spec, out_shape, [],
                               ("parallel", "parallel"), ins)
    return out if hook is None else (out, carried)


CONV_ROWS = 256
CONV_COLS = 128


def _row_iota8(cols):
    return lax.broadcasted_iota(jnp.int32, (SUBLANES, cols), 0)


def _shift_down(cur, prev8, k):
    if k == 0:
        return cur
    rolled = pltpu.roll(cur, k, 0)
    head = jnp.where(_row_iota8(cur.shape[1]) < k, pltpu.roll(prev8, k, 0), rolled[0:SUBLANES])
    return jnp.concatenate([head, rolled[SUBLANES:]], axis=0)


def _shift_up(cur, next8, k):
    if k == 0:
        return cur
    n = cur.shape[0]
    rolled = pltpu.roll(cur, n - k, 0)
    tail = jnp.where(_row_iota8(cur.shape[1]) >= SUBLANES - k, pltpu.roll(next8, SUBLANES - k, 0), rolled[n - SUBLANES:])
    return jnp.concatenate([rolled[:n - SUBLANES], tail], axis=0)


def _load_prev8(ref, i, rows):
    start = pl.multiple_of(jnp.maximum(i * rows - SUBLANES, 0), SUBLANES)
    p = ref[pl.ds(start, SUBLANES), :]
    return jnp.where(i > 0, p, jnp.zeros_like(p))


def _conv_rows(ref, w_ref, b_ref, i, rows, width):
    cur = ref[pl.ds(pl.multiple_of(i * rows, rows), rows), :]
    prev8 = _load_prev8(ref, i, rows)
    shifted = [_shift_down(cur, prev8, k) for k in range(width)]
    acc = b_ref[...] + w_ref[width - 1:width, :] * shifted[0]
    for k in range(1, width):
        acc = acc + w_ref[width - 1 - k:width - k, :] * shifted[k]
    return acc, shifted


def _conv_bwd_rows(du, next8, w_ref, width):
    acc = w_ref[width - 1:width, :] * du
    for k in range(1, width):
        acc = acc + w_ref[width - 1 - k:width - k, :] * _shift_up(du, next8, k)
    return acc


def ffn_mid_fwd(u0, cw, cb):
    s, f2 = u0.shape
    f = f2 // 2
    nt = f // CONV_COLS
    rows = min(CONV_ROWS, s)

    def body(ug_ref, uu_ref, wg_ref, wu_ref, bg_ref, bu_ref, a_ref):
        def step(i, carry):
            g, _ = _conv_rows(ug_ref, wg_ref, bg_ref, i, rows, FFN_CONV)
            u, _ = _conv_rows(uu_ref, wu_ref, bu_ref, i, rows, FFN_CONV)
            a_ref[pl.ds(pl.multiple_of(i * rows, rows), rows), :] = (_silu(g) * u).astype(a_ref.dtype)
            return carry

        lax.fori_loop(0, s // rows, step, 0)

    col = lambda off: pl.BlockSpec((s, CONV_COLS), lambda j: (0, j + off))
    wsp = lambda r, off: pl.BlockSpec((r, CONV_COLS), lambda j: (0, j + off))
    return pl.pallas_call(
        body, name="ffn_mid_fwd", grid=(nt,),
        in_specs=[col(0), col(nt), wsp(FFN_CONV, 0), wsp(FFN_CONV, nt), wsp(1, 0), wsp(1, nt)],
        out_specs=pl.BlockSpec((s, CONV_COLS), lambda j: (0, j)),
        out_shape=_sds((s, f), MXU_DTYPE), compiler_params=_params(("parallel",)),
    )(u0, u0, cw, cw, cb, cb)


def ffn_mid_bwd(u0, da, cw, cb):
    s, f2 = u0.shape
    f = f2 // 2
    nt = f // CONV_COLS
    rows = min(CONV_ROWS, s)
    nsteps = s // rows
    w = FFN_CONV

    def body(ug_ref, uu_ref, da_ref, wg_ref, wu_ref, bg_ref, bu_ref,
             du0_ref, dwg_ref, dwu_ref, dbg_ref, dbu_ref):
        zero8 = jnp.zeros((SUBLANES, CONV_COLS), F32)
        zrow = jnp.zeros((1, CONV_COLS), F32)

        def step(it, carry):
            ng, nu, accs = carry
            i = nsteps - 1 - it
            r0 = pl.multiple_of(i * rows, rows)
            g, sg = _conv_rows(ug_ref, wg_ref, bg_ref, i, rows, w)
            u, su = _conv_rows(uu_ref, wu_ref, bu_ref, i, rows, w)
            dav = da_ref[pl.ds(r0, rows), :]
            sg_val, sg_grad = _silu_and_grad(g)
            dg = dav * u * sg_grad
            du = dav * sg_val
            du0_ref[0, pl.ds(r0, rows), :] = _conv_bwd_rows(dg, ng, wg_ref, w).astype(du0_ref.dtype)
            du0_ref[1, pl.ds(r0, rows), :] = _conv_bwd_rows(du, nu, wu_ref, w).astype(du0_ref.dtype)
            new = []
            for j in range(w):
                new.append(accs[j] + jnp.sum(dg * sg[w - 1 - j], axis=0, keepdims=True))
            for j in range(w):
                new.append(accs[w + j] + jnp.sum(du * su[w - 1 - j], axis=0, keepdims=True))
            new.append(accs[2 * w] + jnp.sum(dg, axis=0, keepdims=True))
            new.append(accs[2 * w + 1] + jnp.sum(du, axis=0, keepdims=True))
            return dg[0:SUBLANES], du[0:SUBLANES], tuple(new)

        _, _, accs = lax.fori_loop(0, nsteps, step, (zero8, zero8, tuple([zrow] * (2 * w + 2))))
        dwg_ref[...] = jnp.concatenate(accs[0:w], axis=0)
        dwu_ref[...] = jnp.concatenate(accs[w:2 * w], axis=0)
        dbg_ref[...] = accs[2 * w]
        dbu_ref[...] = accs[2 * w + 1]

    col = lambda off: pl.BlockSpec((s, CONV_COLS), lambda j: (0, j + off))
    wsp = lambda r, off: pl.BlockSpec((r, CONV_COLS), lambda j: (0, j + off))
    outs = pl.pallas_call(
        body, name="ffn_mid_bwd", grid=(nt,),
        in_specs=[col(0), col(nt), col(0), wsp(w, 0), wsp(w, nt), wsp(1, 0), wsp(1, nt)],
        out_specs=[pl.BlockSpec((2, s, CONV_COLS), lambda j: (0, 0, j)), wsp(w, 0), wsp(w, 0), wsp(1, 0), wsp(1, 0)],
        out_shape=[_sds((2, s, f), MXU_DTYPE), _sds((w, f), F32), _sds((w, f), F32), _sds((1, f), F32), _sds((1, f), F32)],
        compiler_params=_params(("parallel",)),
    )(u0, u0, da, cw, cw, cb, cb)
    du0, dwg, dwu, dbg, dbu = outs
    return du0, jnp.concatenate([dwg, dwu], axis=1), jnp.concatenate([dbg, dbu], axis=1)


SB_BLOCK = 128
SB_DEAD = 88.0
SB_HEADS_PER_STEP = 4


def _split_hi_lo(x):
    hi = x.astype(MXU_DTYPE)
    lo = (x - hi.astype(F32)).astype(MXU_DTYPE)
    return hi, lo


def _dot_exact01(x, tri):
    hi, lo = _split_hi_lo(x)
    return _dot(hi, tri) + _dot(lo, tri)


def _stack_heads(pair, lane_lo):
    zero = jnp.zeros_like(pair)
    return jnp.concatenate([jnp.where(lane_lo, pair, zero), jnp.where(lane_lo, zero, pair)], axis=0)


def _unstack_heads(tall, lane_lo):
    n = tall.shape[0] // 2
    return jnp.where(lane_lo, tall[:n], tall[n:])


def _sb_logits(stacked_q, k_ref, k0, pair_cols, blk):
    z = [_dot(sq, k_ref[pl.ds(k0, blk), cols], _NT) for sq, cols in zip(stacked_q, pair_cols)]
    return jnp.concatenate(z, axis=0) * (SB_HEAD_DIM ** -0.5)


def _sb_logs(z, blk, diagonal):
    t = jnp.log(1.0 + jnp.exp(-jnp.abs(z)))
    lb = jnp.minimum(z, 0.0) - t
    lf = jnp.minimum(-z, 0.0) - t
    if not diagonal:
        return lb, lf, None
    strict = lax.broadcasted_iota(jnp.int32, z.shape, 1) < (lax.broadcasted_iota(jnp.int32, z.shape, 0) & (blk - 1))
    return lb, jnp.where(strict, lf, 0.0), strict


def _keep(strict, x):
    return x if strict is None else jnp.where(strict, x, 0.0)


def _tri(blk, upper):
    r = lax.broadcasted_iota(jnp.int32, (blk, blk), 0)
    c = lax.broadcasted_iota(jnp.int32, (blk, blk), 1)
    return jnp.where((r > c) if upper else (r < c), 1.0, 0.0).astype(MXU_DTYPE)


def _tri_sum(x, tri2):
    hi, lo = _split_hi_lo(x)
    return _dot(jnp.concatenate([hi, lo], axis=1), tri2)


def sb_fwd(qkv, hook=None):
    s = qkv.shape[0]
    blk = min(SB_BLOCK, s)
    nblk = s // blk
    nh = SB_HEADS_PER_STEP
    nstep = SB_HEADS // nh
    dh = SB_HEAD_DIM

    def body(q_ref, k_ref, v_ref, o_ref):
        suffix_tri2 = jnp.concatenate([_tri(blk, True)] * 2, axis=0)
        lane_lo = lax.broadcasted_iota(jnp.int32, (1, LANES), 1) < dh
        pairs = [slice(p * LANES, (p + 1) * LANES) for p in range(nh // 2)]

        def qstep(qi, carry):
            q0 = pl.multiple_of(qi * blk, blk)
            qst = [_stack_heads(q_ref[pl.ds(q0, blk), cols], lane_lo) for cols in pairs]

            def tile(kb, run, accs, diagonal):
                k0 = pl.multiple_of(kb * blk, blk)
                lb, lf, strict = _sb_logs(_sb_logits(qst, k_ref, k0, pairs, blk), blk, diagonal)
                sloc = _tri_sum(lf, suffix_tri2)
                a = _mx(_keep(strict, jnp.exp(lb + sloc + run)))
                accs = tuple(
                    acc + _unstack_heads(_dot(a[2 * blk * p:2 * blk * (p + 1)], v_ref[pl.ds(k0, blk), cols]), lane_lo)
                    for p, (acc, cols) in enumerate(zip(accs, pairs)))
                run = run + sloc[:, 0:1] + lf[:, 0:1]
                return run, accs, jnp.max(run) > -SB_DEAD

            def kstep(st):
                it, run, accs, _ = st
                return (it + 1, *tile(qi - it, run, accs, False))

            first = tile(qi, jnp.zeros((nh * blk, 1), F32), tuple([jnp.zeros((blk, LANES), F32)] * len(pairs)), True)
            _, _, accs, _ = lax.while_loop(lambda st: jnp.logical_and(st[0] <= qi, st[3]), kstep, (jnp.int32(1), *first))
            for acc, cols in zip(accs, pairs):
                o_ref[pl.ds(q0, blk), cols] = acc.astype(o_ref.dtype)
            return carry

        lax.fori_loop(0, nblk, qstep, 0)

    col = lambda off: pl.BlockSpec((s, nh * dh), lambda p: (0, p + off))
    out, carried = hosted_call(body, hook, "sb_fwd", (nstep,), [col(0), col(nstep), col(2 * nstep)], col(0),
                               _sds((s, D_MODEL), MXU_DTYPE), [], ("parallel",), (qkv, qkv, qkv))
    return out if hook is None else (out, carried)


def sb_bwd(qkv, do, hook=None):
    s = qkv.shape[0]
    blk = min(SB_BLOCK, s)
    nblk = s // blk
    nh = SB_HEADS_PER_STEP
    nstep = SB_HEADS // nh
    dh = SB_HEAD_DIM

    def body(q_ref, k_ref, v_ref, do_ref, dq_ref, dk_ref, dv_ref, dk_acc, dv_acc, run_ref):
        suffix_tri2 = jnp.concatenate([_tri(blk, True)] * 2, axis=0)
        prefix_tri2 = jnp.concatenate([_tri(blk, False)] * 2, axis=0)
        dk_acc[...] = jnp.zeros_like(dk_acc)
        dv_acc[...] = jnp.zeros_like(dv_acc)
        lane_lo = lax.broadcasted_iota(jnp.int32, (1, LANES), 1) < dh
        pairs = [slice(p * LANES, (p + 1) * LANES) for p in range(nh // 2)]

        def qstep(qi, carry):
            q0 = pl.multiple_of(qi * blk, blk)
            qst = [_stack_heads(q_ref[pl.ds(q0, blk), cols], lane_lo) for cols in pairs]
            dost = [_stack_heads(do_ref[pl.ds(q0, blk), cols], lane_lo) for cols in pairs]

            def enter(kb, run, diagonal):
                run_ref[kb] = run
                _, lf, _ = _sb_logs(_sb_logits(qst, k_ref, pl.multiple_of(kb * blk, blk), pairs, blk), blk, diagonal)
                run = run + jnp.sum(lf, axis=1, keepdims=True)
                return run, jnp.max(run) > -SB_DEAD

            def sweep1(st):
                it, run, _ = st
                return (it + 1, *enter(qi - it, run, False))

            nlive, _, _ = lax.while_loop(lambda st: jnp.logical_and(st[0] <= qi, st[2]), sweep1,
                                         (jnp.int32(1), *enter(qi, jnp.zeros((nh * blk, 1), F32), True)))

            def tile(kb, pg, dqs, diagonal):
                k0 = pl.multiple_of(kb * blk, blk)
                lb, lf, strict = _sb_logs(_sb_logits(qst, k_ref, k0, pairs, blk), blk, diagonal)
                sloc = _tri_sum(lf, suffix_tri2)
                a = _keep(strict, jnp.exp(lb + sloc + run_ref[kb]))
                da = jnp.concatenate([_dot(d, v_ref[pl.ds(k0, blk), cols], _NT) for d, cols in zip(dost, pairs)], axis=0)
                g = da * a
                p = pg + _tri_sum(g, prefix_tri2)
                sig = jnp.exp(lb)
                dz = _mx(_keep(strict, g * (1.0 - sig) - p * sig) * (dh ** -0.5))
                am = _mx(a)
                new_dqs = []
                for i, cols in enumerate(pairs):
                    rows = slice(2 * blk * i, 2 * blk * (i + 1))
                    new_dqs.append(dqs[i] + _unstack_heads(_dot(dz[rows], k_ref[pl.ds(k0, blk), cols]), lane_lo))
                    dk_acc[pl.ds(k0, blk), cols] += _dot(dz[rows], qst[i], _TN)
                    dv_acc[pl.ds(k0, blk), cols] += _dot(am[rows], dost[i], _TN)
                return pg + jnp.sum(g, axis=1, keepdims=True), tuple(new_dqs)

            pg, dqs = lax.fori_loop(qi + 1 - nlive, qi, lambda kb, st: tile(kb, *st, False),
                                    (jnp.zeros((nh * blk, 1), F32), tuple([jnp.zeros((blk, LANES), F32)] * len(pairs))))
            _, dqs = tile(qi, pg, dqs, True)
            for dq, cols in zip(dqs, pairs):
                dq_ref[pl.ds(q0, blk), cols] = dq.astype(dq_ref.dtype)
            return carry

        lax.fori_loop(0, nblk, qstep, 0)
        dk_ref[...] = dk_acc[...].astype(dk_ref.dtype)
        dv_ref[...] = dv_acc[...].astype(dv_ref.dtype)

    col = lambda off: pl.BlockSpec((s, nh * dh), lambda p: (0, p + off))
    (dq, dk, dv), carried = hosted_call(
        body, hook, "sb_bwd", (nstep,), [col(0), col(nstep), col(2 * nstep), col(0)], [col(0), col(0), col(0)],
        [_sds((s, D_MODEL), MXU_DTYPE)] * 3,
        [pltpu.VMEM((s, nh * dh), F32), pltpu.VMEM((s, nh * dh), F32), pltpu.VMEM((nblk, nh * blk, 1), F32)],
        ("parallel",), (qkv, qkv, qkv, do))
    return jnp.concatenate([dq, dk, dv], axis=1), carried


SSD_XBC_TILE0 = SSD_D_INNER // CONV_COLS


def ssd_conv_fwd(proj, cw, cb):
    s = proj.shape[0]
    rows = min(CONV_ROWS, s)

    def body(u_ref, w_ref, b_ref, o_ref):
        def step(i, carry):
            u, _ = _conv_rows(u_ref, w_ref, b_ref, i, rows, SSD_CONV)
            o_ref[pl.ds(pl.multiple_of(i * rows, rows), rows), :] = _silu(u)
            return carry

        lax.fori_loop(0, s // rows, step, 0)

    return pl.pallas_call(
        body, name="ssd_conv_fwd", grid=(SSD_CONV_DIM // CONV_COLS,),
        in_specs=[pl.BlockSpec((s, CONV_COLS), lambda j: (0, j + SSD_XBC_TILE0)),
                  pl.BlockSpec((SSD_CONV, CONV_COLS), lambda j: (0, j)), pl.BlockSpec((1, CONV_COLS), lambda j: (0, j))],
        out_specs=pl.BlockSpec((s, CONV_COLS), lambda j: (0, j)),
        out_shape=_sds((s, SSD_CONV_DIM), F32), compiler_params=_params(("parallel",)),
    )(proj, cw, cb)


def ssd_conv_bwd(proj, dact, cw, cb):
    s = proj.shape[0]
    rows = min(CONV_ROWS, s)
    nsteps = s // rows
    w = SSD_CONV

    def body(u_ref, da_ref, w_ref, b_ref, du_ref, dw_ref, db_ref):
        def step(it, carry):
            nxt, accs = carry
            i = nsteps - 1 - it
            r0 = pl.multiple_of(i * rows, rows)
            u, sh = _conv_rows(u_ref, w_ref, b_ref, i, rows, w)
            dconv = da_ref[pl.ds(r0, rows), :] * _silu_and_grad(u)[1]
            du_ref[pl.ds(r0, rows), :] = _conv_bwd_rows(dconv, nxt, w_ref, w).astype(du_ref.dtype)
            new = [accs[j] + jnp.sum(dconv * sh[w - 1 - j], axis=0, keepdims=True) for j in range(w)]
            new.append(accs[w] + jnp.sum(dconv, axis=0, keepdims=True))
            return dconv[0:SUBLANES], tuple(new)

        zrow = jnp.zeros((1, CONV_COLS), F32)
        _, accs = lax.fori_loop(0, nsteps, step, (jnp.zeros((SUBLANES, CONV_COLS), F32), tuple([zrow] * (w + 1))))
        dw_ref[...] = jnp.concatenate(accs[0:w], axis=0)
        db_ref[...] = accs[w]

    col = pl.BlockSpec((s, CONV_COLS), lambda j: (0, j))
    return pl.pallas_call(
        body, name="ssd_conv_bwd", grid=(SSD_CONV_DIM // CONV_COLS,),
        in_specs=[pl.BlockSpec((s, CONV_COLS), lambda j: (0, j + SSD_XBC_TILE0)), col,
                  pl.BlockSpec((w, CONV_COLS), lambda j: (0, j)), pl.BlockSpec((1, CONV_COLS), lambda j: (0, j))],
        out_specs=[col, pl.BlockSpec((w, CONV_COLS), lambda j: (0, j)), pl.BlockSpec((1, CONV_COLS), lambda j: (0, j))],
        out_shape=[_sds((s, SSD_CONV_DIM), MXU_DTYPE), _sds((w, SSD_CONV_DIM), F32), _sds((1, SSD_CONV_DIM), F32)],
        compiler_params=_params(("parallel",)),
    )(proj, dact, cw, cb)


def _split3(x):
    hi = x.astype(MXU_DTYPE)
    r1 = x - hi.astype(F32)
    mid = r1.astype(MXU_DTYPE)
    lo = (r1 - mid.astype(F32)).astype(MXU_DTYPE)
    return hi, mid, lo


def _dot01(x, m, dims=_NN, left=False):
    parts = _split3(x)
    if left:
        return _dot(m, parts[0], dims) + _dot(m, parts[1], dims) + _dot(m, parts[2], dims)
    return _dot(parts[0], m, dims) + _dot(parts[1], m, dims) + _dot(parts[2], m, dims)


def _softplus(x):
    return jnp.maximum(x, 0.0) + jnp.log1p(jnp.exp(-jnp.abs(x)))


def _ssd_consts(dt_bias, a_log, d_skip):
    pad = lambda v: jnp.pad(v.reshape(1, SSD_HEADS), ((0, 0), (0, LANES - SSD_HEADS)))
    head_of = jnp.arange(SSD_D_INNER) // SSD_HEAD_DIM
    expand = (jnp.arange(LANES)[:, None] == head_of[None, :]).astype(MXU_DTYPE)
    return dict(bias_w=pad(dt_bias), alog_w=pad(a_log), bias_c=dt_bias.reshape(SSD_HEADS, 1),
                alog_c=a_log.reshape(SSD_HEADS, 1), dskip=jnp.repeat(d_skip, SSD_HEAD_DIM).reshape(1, SSD_D_INNER),
                expand=expand, reduce=expand.T)


def _expand_heads(x):
    first = lax.broadcasted_iota(jnp.int32, (1, LANES), 1) < SSD_HEAD_DIM
    shape = (x.shape[0], LANES)
    tiles = [jnp.where(first, jnp.broadcast_to(x[:, 2 * p:2 * p + 1], shape), jnp.broadcast_to(x[:, 2 * p + 1:2 * p + 2], shape))
             for p in range(SSD_HEADS // 2)]
    return jnp.concatenate(tiles, axis=1)


def _ssd_chunk_prep(dtp, dtp_t, bias_w, alog_w, bias_c, alog_c, expand):
    L = dtp.shape[0]
    r = lax.broadcasted_iota(jnp.int32, (L, L), 0)
    c = lax.broadcasted_iota(jnp.int32, (L, L), 1)
    tril = r >= c
    lower = jnp.where(tril, 1.0, 0.0).astype(MXU_DTYPE)
    upper = jnp.where(r <= c, 1.0, 0.0).astype(MXU_DTYPE)
    dt_col = _softplus(dtp + bias_w)
    a_col = -jnp.exp(alog_w) * dt_col
    a_row = -jnp.exp(alog_c) * _softplus(dtp_t + bias_c)
    acum_col = _dot01(a_col, lower, left=True)
    acum_row = _dot01(a_row, upper)
    acum_full = _expand_heads(acum_col)
    dt_full = _expand_heads(dt_col)
    return dict(tril=tril, lower=lower, upper=upper, dt_col=dt_col, a_col=a_col, acum_col=acum_col,
                acum_row=acum_row, acum_full=acum_full, dt_full=dt_full)


def _head_mask(j):
    lane = lax.broadcasted_iota(jnp.int32, (1, LANES), 1)
    return jnp.where((lane // SSD_HEAD_DIM) == j, 1.0, 0.0)


def _decay(pre, h):
    seg = pre["acum_col"][:, h:h + 1] - pre["acum_row"][h:h + 1, :]
    return jnp.exp(jnp.where(pre["tril"], seg, -1e30))


def _ssd_specs(s, nc, rev):
    L = SSD_CHUNK
    ci = (lambda i: nc - 1 - i) if rev else (lambda i: i)
    const = lambda shape: pl.BlockSpec(shape, lambda i: (0,) * len(shape))
    return dict(
        xbc=pl.BlockSpec((L, SSD_CONV_DIM), lambda i: (ci(i), 0)),
        dtp=pl.BlockSpec((L, LANES), lambda i: (ci(i), SSD_IN_PAD // LANES - 1)),
        dtp_t=pl.BlockSpec((SSD_HEADS, L), lambda i: (0, ci(i))),
        rows=pl.BlockSpec((L, SSD_D_INNER), lambda i: (ci(i), 0)),
        state=pl.BlockSpec((1, SSD_GROUPS, SSD_STATE, 4 * SSD_HEAD_DIM), lambda i: (ci(i), 0, 0, 0)),
        consts=[const((1, LANES)), const((1, LANES)), const((SSD_HEADS, 1)), const((SSD_HEADS, 1)),
                const((1, SSD_D_INNER)), const((LANES, SSD_D_INNER)), const((SSD_D_INNER, LANES))],
    )


def _const_args(cs):
    return [cs["bias_w"], cs["alog_w"], cs["bias_c"], cs["alog_c"], cs["dskip"], cs["expand"], cs["reduce"]]


def ssd_scan_fwd(act, proj, dtp_t, cs, hook=None):
    s = act.shape[0]
    L = SSD_CHUNK
    nc = s // L
    G, N, GW = SSD_GROUPS, SSD_STATE, 4 * SSD_HEAD_DIM

    def body(act_ref, dtp_ref, dtpt_ref, bw_ref, aw_ref, bc_ref, ac_ref, dsk_ref, ex_ref, rd_ref, y_ref, st_out, st):
        @pl.when(pl.program_id(0) == 0)
        def _():
            st[...] = jnp.zeros_like(st)

        st_out[0] = st[...]
        pre = _ssd_chunk_prep(dtp_ref[...], dtpt_ref[...], bw_ref[...], aw_ref[...], bc_ref[...], ac_ref[...], ex_ref[...])
        acum_full = pre["acum_full"]
        last_full = acum_full[L - 1:L, :]
        for g in range(G):
            bg = _mx(act_ref[:, SSD_D_INNER + g * N:SSD_D_INNER + (g + 1) * N])
            cg = _mx(act_ref[:, SSD_D_INNER + G * N + g * N:SSD_D_INNER + G * N + (g + 1) * N])
            cb = _dot(cg, bg, _NT)
            for half in range(2):
                p = 2 * g + half
                cols = slice(p * LANES, (p + 1) * LANES)
                xs = act_ref[:, cols]
                xdt = xs * pre["dt_full"][:, cols]
                yd = jnp.zeros((L, LANES), F32)
                for j in range(2):
                    m = cb * _decay(pre, 2 * p + j)
                    yd = yd + _dot(_mx(m), _mx(xdt * _head_mask(j)))
                yoff = _dot(cg, _mx(st[g, :, half * LANES:(half + 1) * LANES])) * jnp.exp(acum_full[:, cols])
                y_ref[:, cols] = yd + yoff + dsk_ref[:, cols] * xs
                w = jnp.exp(last_full[:, cols] - acum_full[:, cols])
                st[g, :, half * LANES:(half + 1) * LANES] = (
                    st[g, :, half * LANES:(half + 1) * LANES] * jnp.exp(last_full[:, cols]) + _dot(bg, _mx(xdt * w), _TN))

    sp = _ssd_specs(s, nc, False)
    (y, states), carried = hosted_call(
        body, hook, "ssd_scan_fwd", (nc,), [sp["xbc"], sp["dtp"], sp["dtp_t"]] + sp["consts"],
        [sp["rows"], sp["state"]], [_sds((s, SSD_D_INNER), F32), _sds((nc, G, N, GW), F32)],
        [pltpu.VMEM((G, N, GW), F32)], ("arbitrary",), (act, proj, dtp_t, *_const_args(cs)))
    return y, states, carried


def ssd_scan_bwd(act, proj, dtp_t, cs, states, dy, hook=None):
    s = act.shape[0]
    L = SSD_CHUNK
    nc = s // L
    G, N, GW = SSD_GROUPS, SSD_STATE, 4 * SSD_HEAD_DIM

    def body(act_ref, dtp_ref, dtpt_ref, bw_ref, aw_ref, bc_ref, ac_ref, dsk_ref, ex_ref, rd_ref, st_ref, dy_ref,
             dact_ref, ddtp_ref, dalog_ref, dbias_ref, dskip_ref, dst, dxdt_ref, dac_ref):
        first = pl.program_id(0) == 0

        @pl.when(first)
        def _():
            dst[...] = jnp.zeros_like(dst)
            dalog_ref[...] = jnp.zeros_like(dalog_ref)
            dbias_ref[...] = jnp.zeros_like(dbias_ref)
            dskip_ref[...] = jnp.zeros_like(dskip_ref)

        expand, reduce = ex_ref[...], rd_ref[...]
        pre = _ssd_chunk_prep(dtp_ref[...], dtpt_ref[...], bw_ref[...], aw_ref[...], bc_ref[...], ac_ref[...], expand)
        acum_full = pre["acum_full"]
        last_full = acum_full[L - 1:L, :]
        ones = jnp.ones((2 * L, LANES), MXU_DTYPE)
        lane = lax.broadcasted_iota(jnp.int32, (L, LANES), 1)
        dacum_diag = jnp.zeros((L, LANES), F32)
        dlast_parts = []
        for g in range(G):
            bg = _mx(act_ref[:, SSD_D_INNER + g * N:SSD_D_INNER + (g + 1) * N])
            cg = _mx(act_ref[:, SSD_D_INNER + G * N + g * N:SSD_D_INNER + G * N + (g + 1) * N])
            cb = _dot(cg, bg, _NT)
            dcb = jnp.zeros((L, L), F32)
            dcg = jnp.zeros((L, N), F32)
            dbg = jnp.zeros((L, N), F32)
            for half in range(2):
                p = 2 * g + half
                cols = slice(p * LANES, (p + 1) * LANES)
                hcols = slice(half * LANES, (half + 1) * LANES)
                xs = act_ref[:, cols]
                xdt = xs * pre["dt_full"][:, cols]
                dyv = dy_ref[:, cols]
                dxdt = jnp.zeros((L, LANES), F32)
                parts = []
                for j in range(2):
                    h = 2 * p + j
                    dec = _decay(pre, h)
                    m = cb * dec
                    dyh = _mx(dyv * _head_mask(j))
                    dm = _dot(dyh, _mx(xdt), _NT)
                    dxdt = dxdt + _dot(_mx(m), dyh, _TN)
                    parts.append(_split_hi_lo(dm * m))
                    dcb = dcb + dm * dec
                (ahi, alo), (bhi, blo) = parts
                rows = _dot(jnp.concatenate([jnp.concatenate([ahi, alo], axis=1), jnp.concatenate([bhi, blo], axis=1)], axis=0), ones)
                cols_ = _dot(jnp.concatenate([jnp.concatenate([ahi, bhi], axis=1), jnp.concatenate([alo, blo], axis=1)], axis=0),
                             ones, _TN)
                d_pair = rows - cols_
                dacum_diag = jnp.where(lane == 2 * p, d_pair[:L], jnp.where(lane == 2 * p + 1, d_pair[L:], dacum_diag))
                lam = jnp.exp(acum_full[:, cols])
                stv = _mx(st_ref[0, g, :, hcols])
                z = _dot(cg, stv)
                dz = _mx(lam * dyv)
                dcg = dcg + _dot(dz, stv, _NT)
                dst_in = _dot(cg, dz, _TN)
                dsv = dst[g, :, hcols]
                w = jnp.exp(last_full[:, cols] - acum_full[:, cols])
                q = _dot(bg, _mx(dsv))
                wq = w * q
                dxdt = dxdt + wq
                wqx = wq * xdt
                dbg = dbg + _dot(_mx(xdt * w), _mx(dsv), _NT)
                elast = jnp.exp(last_full[:, cols])
                dlast_p = jnp.sum(wqx, axis=0, keepdims=True) + elast * jnp.sum(dsv * st_ref[0, g, :, hcols], axis=0, keepdims=True)
                dac_ref[:, cols] = dyv * z * lam - wqx
                dlast_parts.append(dlast_p)
                dst[g, :, hcols] = dst_in + dsv * elast
                dxdt_ref[:, cols] = dxdt
                dact_ref[:, cols] = dxdt * pre["dt_full"][:, cols] + dsk_ref[:, cols] * dyv
            dcbm = _mx(dcb)
            dact_ref[:, SSD_D_INNER + g * N:SSD_D_INNER + (g + 1) * N] = dbg + _dot(dcbm, cg, _TN)
            dact_ref[:, SSD_D_INNER + G * N + g * N:SSD_D_INNER + G * N + (g + 1) * N] = dcg + _dot(dcbm, bg)

        xs_all = act_ref[:, 0:SSD_D_INNER]
        dacum = dacum_diag + _dot_exact01(dac_ref[...], reduce)
        dlast = _dot_exact01(jnp.concatenate(dlast_parts, axis=1), reduce)
        row = lax.broadcasted_iota(jnp.int32, (L, LANES), 0)
        dacum = dacum + jnp.where(row == L - 1, dlast, 0.0)
        da_col = _dot01(dacum, pre["upper"], left=True)
        a_w = -jnp.exp(aw_ref[...])
        ddt = a_w * da_col + _dot_exact01(dxdt_ref[...] * xs_all, reduce)
        xin = dtp_ref[...] + bw_ref[...]
        ddtp = ddt * (1.0 / (1.0 + jnp.exp(-xin)))
        valid = lane < SSD_HEADS
        ddtp = jnp.where(valid, ddtp, 0.0)
        ddtp_ref[...] = ddtp
        dbias_ref[...] += jnp.sum(ddtp, axis=0, keepdims=True)
        dalog_ref[...] += jnp.sum(jnp.where(valid, da_col * pre["a_col"], 0.0), axis=0, keepdims=True)
        dskip_ref[...] += jnp.sum(_dot_exact01(dy_ref[...] * xs_all, reduce), axis=0, keepdims=True)

    sp = _ssd_specs(s, nc, True)
    acc = pl.BlockSpec((1, LANES), lambda i: (0, 0))
    outs, carried = hosted_call(
        body, hook, "ssd_scan_bwd", (nc,),
        [sp["xbc"], sp["dtp"], sp["dtp_t"]] + sp["consts"] + [sp["state"], sp["rows"]],
        [sp["xbc"], pl.BlockSpec((L, LANES), lambda i: (nc - 1 - i, 0)), acc, acc, acc],
        [_sds((s, SSD_CONV_DIM), F32), _sds((s, LANES), F32)] + [_sds((1, LANES), F32)] * 3,
        [pltpu.VMEM((G, N, GW), F32), pltpu.VMEM((L, SSD_D_INNER), F32), pltpu.VMEM((L, SSD_D_INNER), F32)],
        ("arbitrary",), (act, proj, dtp_t, *_const_args(cs), states, dy))
    return (*outs, carried)


def ssd_post_fwd(y, proj, g):
    s, d = y.shape
    ts = _pick(s, (256, 128))

    def body(y_ref, z_ref, g_ref, o_ref):
        y2 = y_ref[...] * _silu(z_ref[...])
        r = lax.rsqrt(jnp.mean(y2 * y2, axis=-1, keepdims=True) + NORM_EPS)
        o_ref[...] = (y2 * r * g_ref[...]).astype(o_ref.dtype)

    row = pl.BlockSpec((ts, d), lambda i: (i, 0))
    return pl.pallas_call(
        body, name="ssd_post_fwd", grid=(s // ts,), in_specs=[row, row, pl.BlockSpec((1, d), lambda i: (0, 0))],
        out_specs=row, out_shape=_sds((s, d), MXU_DTYPE), compiler_params=_params(("parallel",)),
    )(y, proj, g)


def ssd_post_bwd(y, proj, g, dy3):
    s, d = y.shape
    ts = _pick(s, (256, 128))

    def body(y_ref, z_ref, g_ref, d3_ref, dy_ref, dz_ref, dg_ref):
        yv, zv = y_ref[...], z_ref[...]
        sz, sgrad = _silu_and_grad(zv)
        y2 = yv * sz
        r = lax.rsqrt(jnp.mean(y2 * y2, axis=-1, keepdims=True) + NORM_EPS)
        xh = y2 * r
        d3 = d3_ref[...]
        dxh = d3 * g_ref[...]
        dy2 = r * (dxh - xh * jnp.mean(dxh * xh, axis=-1, keepdims=True))
        dy_ref[...] = dy2 * sz
        dz_ref[...] = (dy2 * yv * sgrad).astype(dz_ref.dtype)
        part = jnp.sum(d3 * xh, axis=0, keepdims=True)

        @pl.when(pl.program_id(0) == 0)
        def _():
            dg_ref[...] = part

        @pl.when(pl.program_id(0) != 0)
        def _():
            dg_ref[...] += part

    row = pl.BlockSpec((ts, d), lambda i: (i, 0))
    vec = pl.BlockSpec((1, d), lambda i: (0, 0))
    return pl.pallas_call(
        body, name="ssd_post_bwd", grid=(s // ts,), in_specs=[row, row, vec, row], out_specs=[row, row, vec],
        out_shape=[_sds((s, d), F32), _sds((s, d), MXU_DTYPE), _sds((1, d), F32)],
        compiler_params=_params(("arbitrary",)),
    )(y, proj, g, dy3)


def dt_transpose(proj):
    s = proj.shape[0]
    ts = _pick(s, (512, 256, 128))

    def body(p_ref, o_ref):
        o_ref[...] = p_ref[...].T

    return pl.pallas_call(
        body, name="dt_transpose", grid=(s // ts,),
        in_specs=[pl.BlockSpec((ts, LANES), lambda i: (i, SSD_IN_PAD // LANES - 1))],
        out_specs=pl.BlockSpec((LANES, ts), lambda i: (0, i)), out_shape=_sds((LANES, s), F32),
        compiler_params=_params(("parallel",)),
    )(proj)


def ssd_core_fwd(proj, cw, cb, dt_bias, a_log, d_skip, norm_g, hook=None):
    cs = _ssd_consts(dt_bias, a_log, d_skip)
    act = ssd_conv_fwd(proj, cw, cb)
    dtp_t = dt_transpose(proj)
    y, states, carried = ssd_scan_fwd(act, proj, dtp_t, cs, hook)
    y3 = ssd_post_fwd(y, proj, norm_g)
    return y3, (cs, act, dtp_t, y, states), carried


def ssd_core_bwd(proj, cw, cb, norm_g, saved, dy3, hook=None):
    cs, act, dtp_t, y, states = saved
    dy, dz, dnorm = ssd_post_bwd(y, proj, norm_g, dy3)
    dact, ddtp, dalog, dbias, dskip, carried = ssd_scan_bwd(act, proj, dtp_t, cs, states, dy, hook)
    dxbc, dcw, dcb = ssd_conv_bwd(proj, dact, cw, cb)
    dproj = jnp.concatenate([dz, dxbc, ddtp.astype(MXU_DTYPE)], axis=1)
    h = SSD_HEADS
    return dproj, dcw, dcb, dbias[0, :h], dalog[0, :h], dskip[0, :h], dnorm, carried


def ssd_core(proj, cw, cb, dt_bias, a_log, d_skip, norm_g, dy3):
    y3, saved, _ = ssd_core_fwd(proj, cw, cb, dt_bias, a_log, d_skip, norm_g)
    return y3, ssd_core_bwd(proj, cw, cb, norm_g, saved, dy3)


def loss_head(x, g, target):
    s, d = x.shape
    ts = _pick(s, (512, 256, 128))

    def body(x_ref, g_ref, t_ref, loss_ref, dx_ref, dxm_ref, dg_ref):
        xv = x_ref[...]
        r = lax.rsqrt(jnp.mean(xv * xv, axis=-1, keepdims=True) + NORM_EPS)
        xh = xv * r
        err = xh * g_ref[...] - t_ref[...]
        dy = err * (1.0 / d)
        dxh = dy * g_ref[...]
        dx = r * (dxh - xh * jnp.mean(dxh * xh, axis=-1, keepdims=True))
        dx_ref[...] = dx
        dxm_ref[...] = dx.astype(dxm_ref.dtype)
        part = jnp.sum(dy * xh, axis=0, keepdims=True)
        lpart = jnp.full((1, LANES), 0.5 * jnp.sum(jnp.mean(err * err, axis=-1, keepdims=True)), F32)

        @pl.when(pl.program_id(0) == 0)
        def _():
            dg_ref[...] = part
            loss_ref[...] = lpart

        @pl.when(pl.program_id(0) != 0)
        def _():
            dg_ref[...] += part
            loss_ref[...] += lpart

    row = pl.BlockSpec((ts, d), lambda i: (i, 0))
    vec = pl.BlockSpec((1, d), lambda i: (0, 0))
    return pl.pallas_call(
        body, name="loss_head", grid=(s // ts,), in_specs=[row, vec, row],
        out_specs=[pl.BlockSpec((1, LANES), lambda i: (0, 0)), row, row, vec],
        out_shape=[_sds((1, LANES), F32), _sds((s, d), F32), _sds((s, d), MXU_DTYPE), _sds((1, d), F32)],
        compiler_params=_params(("arbitrary",)),
    )(x, g, target)


def _adamw_math(w, g, m, v):
    m = ADAM_B1 * m + (1.0 - ADAM_B1) * g
    v = ADAM_B2 * v + (1.0 - ADAM_B2) * (g * g)
    m_hat = m / (1.0 - ADAM_B1 ** ADAM_STEP)
    v_hat = v / (1.0 - ADAM_B2 ** ADAM_STEP)
    return -ADAM_LR * (m_hat / (jnp.sqrt(v_hat) + ADAM_EPS) + ADAM_WD * w), m, v


def adamw(w, g, m, v, name="adamw"):
    r, c = w.shape
    tr = _pick(r, (256, 128, 64, 32, 16, 8))

    def body(w_ref, g_ref, m_ref, v_ref, d_ref, nm_ref, nv_ref):
        d_ref[...], nm_ref[...], nv_ref[...] = _adamw_math(w_ref[...], g_ref[...], m_ref[...], v_ref[...])

    blk = pl.BlockSpec((tr, c), lambda i: (i, 0))
    return pl.pallas_call(
        body, name=name, grid=(r // tr,), in_specs=[blk] * 4, out_specs=[blk] * 3,
        out_shape=[_sds((r, c), F32)] * 3, compiler_params=_params(("parallel",)),
    )(w, g, m, v)


def adamw_small(w, parts, m, v):
    n, r, c = parts.shape

    def body(w_ref, p_ref, m_ref, v_ref, g_ref, d_ref, nm_ref, nv_ref):
        g = p_ref[0]
        for k in range(1, n):
            g = g + p_ref[k]
        g_ref[...] = g
        d_ref[...], nm_ref[...], nv_ref[...] = _adamw_math(w_ref[...], g, m_ref[...], v_ref[...])

    return pl.pallas_call(
        body, name="adamw_small", out_shape=[_sds((r, c), F32)] * 4, compiler_params=_params(),
    )(w, parts, m, v)


def pair_sum(unit, recv, where):
    nchip, _, r, c = unit.shape
    tr = _pick(r, (512, 256, 176, 128, 64, 32, 16))

    def body(w_ref, a_ref, b_ref, o_ref, ob_ref):
        sm = a_ref[0, 0] + b_ref[0]
        ob_ref[0] = sm.astype(ob_ref.dtype)

        @pl.when(pl.program_id(1) == w_ref[0])
        def _():
            o_ref[...] = sm

    blk = pl.BlockSpec((1, tr, c), lambda i, s, w: (s, i, 0))
    return pl.pallas_call(
        body, name="pair_sum",
        grid_spec=pltpu.PrefetchScalarGridSpec(
            num_scalar_prefetch=1, grid=(r // tr, nchip),
            in_specs=[pl.BlockSpec((1, 1, tr, c), lambda i, s, w: (s, w[1], i, 0)), blk],
            out_specs=[pl.BlockSpec((tr, c), lambda i, s, w: (i, 0)), blk]),
        out_shape=[_sds((r, c), F32), _sds((nchip, r, c), jnp.bfloat16)],
        compiler_params=_params(("parallel", "arbitrary")),
    )(where, unit, recv)


def chip_sum(own, where, recv, layer, layers, prev=None):
    r, c = own.shape
    tr = _pick(r, (512, 256, 176, 128, 64, 32, 16))

    def body(s_ref, a_ref, b_ref, *rest):
        rest[-1][...] = a_ref[...] + b_ref[0].astype(F32) + b_ref[1].astype(F32) + b_ref[2].astype(F32)

    in_specs = [pl.BlockSpec((tr, c), lambda i, s: (i, 0)), pl.BlockSpec((3, tr, c), lambda i, s: (0, i, 0))]
    args = [where, own, recv]
    if prev is not None:
        in_specs.append(ANY)
        args.append(prev)
    return pl.pallas_call(
        body, name="chip_sum",
        grid_spec=pltpu.PrefetchScalarGridSpec(
            num_scalar_prefetch=1, grid=(r // tr,), in_specs=in_specs,
            out_specs=pl.BlockSpec((None, None, tr, c), lambda i, s: (layer, s[1], i, 0))),
        out_shape=_sds((layers, 2, r, c), F32), input_output_aliases={} if prev is None else {3: 0},
        compiler_params=_params(("parallel",)),
    )(*args)


def place_cast(w, layer, chip):
    _, a, b = w.shape
    ta = _pick(a, (512, 352, 256, 128))

    def body(c_ref, w_ref, o_ref):
        o_ref[...] = w_ref[...].astype(o_ref.dtype)

    return pl.pallas_call(
        body, name="place_cast",
        grid_spec=pltpu.PrefetchScalarGridSpec(
            num_scalar_prefetch=1, grid=(a // ta,),
            in_specs=[pl.BlockSpec((None, ta, b), lambda i, c: (layer, i, 0))],
            out_specs=pl.BlockSpec((None, ta, b), lambda i, c: (c[0], i, 0))),
        out_shape=_sds((N_CHIPS, a, b), MXU_DTYPE), compiler_params=_params(("parallel",)),
    )(chip, w)


ANY = pl.BlockSpec(memory_space=pl.ANY)
COMM = pltpu.CompilerParams(has_side_effects=True)


def _coords():
    return lax.axis_index("x"), lax.axis_index("y"), lax.axis_index("c")


def _other_chips(x, y):
    return [(1 - x, y), (x, 1 - y), (1 - x, 1 - y)]


def all_gather_8(halves, name):
    _, r, c = halves.shape

    def body(h_ref, out_ref, send_sems, recv_sems, local_sem):
        x, y, cc = _coords()
        _gather_one(h_ref.at[cc], lambda px, py, pc: out_ref.at[4 * px + 2 * py + pc],
                    lambda k: send_sems.at[k], lambda k: recv_sems.at[k], local_sem)

    return pl.pallas_call(
        body, name=name, in_specs=[ANY], out_specs=ANY, out_shape=_sds((8, r, c), halves.dtype),
        scratch_shapes=[pltpu.SemaphoreType.DMA((7,)), pltpu.SemaphoreType.DMA((7,)), pltpu.SemaphoreType.DMA],
        compiler_params=COMM,
    )(halves)


def _gather_plan(x_ref, slot, send_sem, recv_sem, local_sem):
    x, y, cc = _coords()
    me, sibling = (x, y, cc), (x, y, 1 - cc)
    chips = _other_chips(x, y)

    def copy(k, blk, to, src=None):
        return pltpu.make_async_remote_copy(
            src_ref=slot(*blk) if src is None else src, dst_ref=slot(*blk),
            send_sem=send_sem(k), recv_sem=recv_sem(k), device_id=to, device_id_type=MESH)

    mine = pltpu.make_async_copy(x_ref, slot(*me), local_sem)
    first = [copy(0, me, sibling, src=x_ref)] + [copy(1 + j, me, (*chip, cc), src=x_ref) for j, chip in enumerate(chips)]
    passed = [copy(4 + j, (*chip, cc), sibling) for j, chip in enumerate(chips)]
    over_ici = [copy(1 + j, (*chip, cc), me) for j, chip in enumerate(chips)]
    from_sibling = [copy(0, sibling, me)] + [copy(4 + j, (*chip, 1 - cc), me) for j, chip in enumerate(chips)]
    return mine, first, passed, over_ici, from_sibling


def _gather_run(plans):
    for mine, first, _, _, _ in plans:
        mine.start()
        for cp in first:
            cp.start()
    for j in range(3):
        for _, _, passed, over_ici, _ in plans:
            over_ici[j].wait_recv()
            passed[j].start()
    for mine, first, passed, _, from_sibling in plans:
        for cp in from_sibling:
            cp.wait_recv()
        for cp in first + passed:
            cp.wait_send()
        mine.wait()


def _gather_one(x_ref, slot, send_sem, recv_sem, local_sem):
    _gather_run([_gather_plan(x_ref, slot, send_sem, recv_sem, local_sem)])


def gather_hook(items):
    n = len(items)

    def plan(refs, send_sems, recv_sems):
        x, y, cc = _coords()

        def copy(i, k, px, py, pc, to):
            blk = refs[i].at[2 * px + py, pc]
            return pltpu.make_async_remote_copy(src_ref=blk, dst_ref=blk, send_sem=send_sems.at[i, k],
                                                recv_sem=recv_sems.at[i, k], device_id=to, device_id_type=MESH)

        chips = _other_chips(x, y)
        first = [copy(i, j, x, y, cc, (*chip, cc)) for i in range(n) for j, chip in enumerate(chips)]
        return copy, chips, first, (x, y, cc)

    def start(refs, new, sems):
        for cp in plan(refs, *sems)[2]:
            cp.start()

    def finish(refs, new, sems):
        copy, chips, first, (x, y, cc) = plan(refs, *sems)
        passed = []
        for j, chip in enumerate(chips):
            for i in range(n):
                copy(i, j, *chip, cc, (x, y, cc)).wait_recv()
                passed.append(copy(i, 3 + j, *chip, cc, (x, y, 1 - cc)))
                passed[-1].start()
        for j, chip in enumerate(chips):
            for i in range(n):
                copy(i, 3 + j, *chip, 1 - cc, (x, y, cc)).wait_recv()
        for cp in first + passed:
            cp.wait_send()

    return dict(arrays=list(items), new=[], start=start, finish=finish, in_place=True,
                sems=[pltpu.SemaphoreType.DMA((n, 6)), pltpu.SemaphoreType.DMA((n, 6))])


def hosted_call(body, hook, name, grid, in_specs, out_specs, out_shape, scratch_shapes, sem, args):
    single = not isinstance(out_shape, (list, tuple))
    out_specs_l = [out_specs] if single else list(out_specs)
    out_shape_l = [out_shape] if single else list(out_shape)
    if hook is None:
        res = pl.pallas_call(body, name=name, grid=grid, in_specs=list(in_specs), out_specs=out_specs, out_shape=out_shape,
                             scratch_shapes=list(scratch_shapes), compiler_params=_params(sem))(*args)
        return res, []
    items, new = hook["arrays"], hook["new"]
    k, kn, n_in, n_out, n_scr = len(items), len(new), len(in_specs), len(out_specs_l), len(scratch_shapes)
    ka = k if hook["in_place"] else 0

    def full(*refs):
        ins = refs[:n_in]
        base = n_in + k
        outs = refs[base:base + n_out]
        hrefs = refs[base + n_out:base + n_out + ka] if ka else refs[n_in:base]
        nrefs = refs[base + n_out + ka:base + n_out + ka + kn]
        scr = refs[base + n_out + ka + kn:base + n_out + ka + kn + n_scr]
        sems = refs[base + n_out + ka + kn + n_scr:]
        ids = [pl.program_id(d) for d in range(len(grid))]
        first = functools.reduce(jnp.logical_and, [i == 0 for i in ids])
        last = functools.reduce(jnp.logical_and, [i == g - 1 for i, g in zip(ids, grid)])

        @pl.when(first)
        def _():
            hook["start"](hrefs, nrefs, sems)

        body(*ins, *outs, *scr)

        @pl.when(last)
        def _():
            hook["finish"](hrefs, nrefs, sems)

    res = pl.pallas_call(
        full, name=name, grid=grid, in_specs=list(in_specs) + [ANY] * k, out_specs=out_specs_l + [ANY] * (ka + kn),
        out_shape=out_shape_l + [_sds(a.shape, a.dtype) for a in items[:ka]] + list(new),
        input_output_aliases={n_in + i: n_out + i for i in range(ka)},
        scratch_shapes=list(scratch_shapes) + hook["sems"],
        compiler_params=pltpu.CompilerParams(dimension_semantics=("arbitrary",) * len(grid),
                                             vmem_limit_bytes=VMEM_LIMIT, has_side_effects=True),
    )(*args, *items)
    return (res[0] if single else list(res[:n_out])), list(res[n_out:])


def comm_call(hook, name):
    k, kn = len(hook["arrays"]), len(hook["new"])
    ka = k if hook["in_place"] else 0

    def body(*refs):
        hrefs = refs[k:k + ka] if ka else refs[:k]
        hook["start"](hrefs, refs[k + ka:k + ka + kn], refs[k + ka + kn:])
        hook["finish"](hrefs, refs[k + ka:k + ka + kn], refs[k + ka + kn:])

    return list(pl.pallas_call(
        body, name=name, in_specs=[ANY] * k, out_specs=[ANY] * (ka + kn),
        out_shape=[_sds(a.shape, a.dtype) for a in hook["arrays"][:ka]] + list(hook["new"]),
        input_output_aliases={i: i for i in range(ka)}, scratch_shapes=hook["sems"], compiler_params=COMM,
    )(*hook["arrays"]))


def merge_hooks(hooks):
    hooks = [h for h in hooks if h is not None]
    if len(hooks) < 2:
        return hooks[0] if hooks else None

    def parts(refs, new, sems):
        out, a, b, c = [], 0, 0, 0
        for h in hooks:
            na, nn, ns = len(h["arrays"]), len(h["new"]), len(h["sems"])
            out.append((refs[a:a + na], new[b:b + nn], sems[c:c + ns]))
            a, b, c = a + na, b + nn, c + ns
        return out

    def start(refs, new, sems):
        for h, p in zip(hooks, parts(refs, new, sems)):
            h["start"](*p)

    def finish(refs, new, sems):
        for h, p in zip(hooks, parts(refs, new, sems)):
            h["finish"](*p)

    assert len({h["in_place"] for h in hooks}) == 1
    return dict(arrays=[a for h in hooks for a in h["arrays"]], new=[a for h in hooks for a in h["new"]],
                sems=[a for h in hooks for a in h["sems"]], start=start, finish=finish, in_place=hooks[0]["in_place"])


def split_carried(hooks, carried):
    hooks = [h for h in hooks if h is not None]
    off = sum(len(h["arrays"]) for h in hooks if h["in_place"])
    out = []
    for h in hooks:
        out.append(carried[off:off + len(h["new"])])
        off += len(h["new"])
    return out


def pair_swap_hook(units):
    n = len(units)

    def plan(refs, new, send_sems, recv_sems):
        x, y, cc = _coords()
        return [pltpu.make_async_remote_copy(src_ref=refs[i].at[:, 1 - cc], dst_ref=new[i], send_sem=send_sems.at[i],
                                             recv_sem=recv_sems.at[i], device_id=(x, y, 1 - cc), device_id_type=MESH)
                for i in range(n)]

    def start(refs, new, sems):
        for cp in plan(refs, new, *sems):
            cp.start()

    def finish(refs, new, sems):
        for cp in plan(refs, new, *sems):
            cp.wait()

    return dict(arrays=list(units), new=[_sds((u.shape[0],) + u.shape[2:], u.dtype) for u in units], start=start,
                finish=finish, in_place=False, sems=[pltpu.SemaphoreType.DMA((n,)), pltpu.SemaphoreType.DMA((n,))])


def chip_exchange_hook(units):
    n = len(units)

    def plan(refs, new, send_sems, recv_sems):
        x, y, cc = _coords()
        return [pltpu.make_async_remote_copy(
            src_ref=refs[i].at[2 * px + py], dst_ref=new[i].at[k], send_sem=send_sems.at[i, k],
            recv_sem=recv_sems.at[i, k], device_id=(px, py, cc), device_id_type=MESH)
            for i in range(n) for k, (px, py) in enumerate(_other_chips(x, y))]

    def start(refs, new, sems):
        for cp in plan(refs, new, *sems):
            cp.start()

    def finish(refs, new, sems):
        for cp in plan(refs, new, *sems):
            cp.wait()

    return dict(arrays=list(units), new=[_sds((3,) + u.shape[1:], u.dtype) for u in units], start=start,
                finish=finish, in_place=False, sems=[pltpu.SemaphoreType.DMA((n, 3)), pltpu.SemaphoreType.DMA((n, 3))])


def grad_half_swap(grads):
    n = len(grads)

    def body(*refs):
        outs, send_sems, recv_sems = refs[n:2 * n], refs[2 * n], refs[2 * n + 1]
        x, y, cc = _coords()
        cps = [pltpu.make_async_remote_copy(
            src_ref=outs[i].at[:, cc], dst_ref=outs[i].at[:, cc], send_sem=send_sems.at[i], recv_sem=recv_sems.at[i],
            device_id=(x, y, 1 - cc), device_id_type=MESH) for i in range(n)]
        for cp in cps:
            cp.start()
        for i, cp in enumerate(cps):
            cp.wait_send()
            pltpu.make_async_remote_copy(
                src_ref=outs[i].at[:, 1 - cc], dst_ref=outs[i].at[:, 1 - cc], send_sem=send_sems.at[i],
                recv_sem=recv_sems.at[i], device_id=(x, y, 1 - cc), device_id_type=MESH).wait_recv()

    return pl.pallas_call(
        body, name="grad_half_swap", in_specs=[ANY] * n, out_specs=[ANY] * n,
        out_shape=[_sds(g.shape, g.dtype) for g in grads], input_output_aliases={i: i for i in range(n)},
        scratch_shapes=[pltpu.SemaphoreType.DMA((n,)), pltpu.SemaphoreType.DMA((n,))], compiler_params=COMM,
    )(*grads)


N_CHIPS = 4
PACK_COLS = 1024
BIG = ("ssd_w_in", "ssd_w_out", "sb_w_qkv", "sb_w_out", "ffn_w_in", "ffn_w_out")
CONVW = ("ssd_conv_w", "ffn_conv_w")
COL_SHARDED = ("ssd_w_in", "sb_w_qkv", "ffn_w_in", "ssd_conv_w", "ffn_conv_w")
SMALL = ("mix_norm", "ffn_norm", "final_norm", "ssd_conv_b", "ssd_dt_bias", "ssd_a_log", "ssd_d", "ssd_norm", "ffn_conv_b")
WEIGHTS = ("mix_norm", "ffn_norm", "final_norm", "ssd_w_in", "ssd_conv_w", "ssd_conv_b", "ssd_dt_bias", "ssd_a_log",
           "ssd_d", "ssd_norm", "ssd_w_out", "sb_w_qkv", "sb_w_out", "ffn_w_in", "ffn_conv_w", "ffn_conv_b", "ffn_w_out")


def _to_rows(flat, multiple):
    rows = -(-flat.shape[-1] // PACK_COLS)
    rows = -(-rows // multiple) * multiple
    pad = rows * PACK_COLS - flat.shape[-1]
    return jnp.pad(flat, [(0, pad)]).reshape(rows, PACK_COLS)


def _unshard(name, stacked):
    l, n, a, b = stacked.shape
    if name in COL_SHARDED:
        return jnp.transpose(stacked, (0, 2, 1, 3)).reshape(l, a, n * b)
    return stacked.reshape(l, n * a, b)


def _gather_conv_weights(w):
    flat = jnp.concatenate([w[n].reshape(-1) for n in CONVW])
    rows = _to_rows(flat, 16)
    got = all_gather_8(rows.reshape(2, rows.shape[0] // 2, PACK_COLS), "gather_conv_weights").reshape(N_CHIPS, -1)
    out, off = {}, 0
    for n in CONVW:
        l, a, b = w[n].shape
        out[n] = _unshard(n, jnp.moveaxis(got[:, off:off + w[n].size].reshape(N_CHIPS, l, a, b), 0, 1))
        off += w[n].size
    return out


def _finish_big_grads(pair_sums, from_chips, layout):
    cc = lax.axis_index("c").astype(jnp.int32)
    chip = (2 * lax.axis_index("x") + lax.axis_index("y")).astype(jnp.int32)
    where = jnp.stack([chip, cc])
    nlayers = [1 + max(l for k, l in layout if k == wi) for wi in range(1 + max(k for k, _ in layout))]
    grads = [None] * len(nlayers)
    for (wi, l), p, r in zip(layout, pair_sums, from_chips):
        grads[wi] = chip_sum(p, where, r, l, nlayers[wi], grads[wi])
    return grad_half_swap(grads)


def kernel(x, mix_norm, ffn_norm, final_norm, ssd_w_in, ssd_conv_w, ssd_conv_b, ssd_dt_bias, ssd_a_log, ssd_d, ssd_norm, ssd_w_out, sb_w_qkv, sb_w_out, ffn_w_in, ffn_conv_w, ffn_conv_b, ffn_w_out, loss_target, m_mix_norm, m_ffn_norm, m_final_norm, m_ssd_w_in, m_ssd_conv_w, m_ssd_conv_b, m_ssd_dt_bias, m_ssd_a_log, m_ssd_d, m_ssd_norm, m_ssd_w_out, m_sb_w_qkv, m_sb_w_out, m_ffn_w_in, m_ffn_conv_w, m_ffn_conv_b, m_ffn_w_out, v_mix_norm, v_ffn_norm, v_final_norm, v_ssd_w_in, v_ssd_conv_w, v_ssd_conv_b, v_ssd_dt_bias, v_ssd_a_log, v_ssd_d, v_ssd_norm, v_ssd_w_out, v_sb_w_qkv, v_sb_w_out, v_ffn_w_in, v_ffn_conv_w, v_ffn_conv_b, v_ffn_w_out):
    given = dict(locals())
    w = {n: given[n] for n in WEIGHTS}
    mom = {n: given["m_" + n] for n in WEIGHTS}
    var = {n: given["v_" + n] for n in WEIGHTS}
    chip = 2 * lax.axis_index("x") + lax.axis_index("y")

    chip1 = chip.reshape(1).astype(jnp.int32)
    fw = _gather_conv_weights(w)
    row = lambda v: v.reshape(1, -1)

    def placed(n, l):
        _, a, b = w[n].shape
        return place_cast(w[n], l, chip1).reshape(N_CHIPS, 2, a // 2, b)

    def mixer_items(i):
        return [(n, i // 2) for n in (("ssd_w_in", "ssd_w_out") if i % 2 == 0 else ("sb_w_qkv", "sb_w_out"))]

    def ffn_items(i):
        return [("ffn_w_in", i), ("ffn_w_out", i)]

    def hook_for(items):
        return gather_hook([placed(n, l) for n, l in items]) if items else None

    lw = {}

    def arrived(items, arrays):
        for (n, l), arr in zip(items, arrays):
            g4 = arr.reshape(N_CHIPS, -1, arr.shape[-1])
            if n in COL_SHARDED:
                full = jnp.transpose(g4, (1, 0, 2)).reshape(g4.shape[1], -1)
            else:
                full = g4.reshape(-1, g4.shape[2])
            if n == "ssd_w_in":
                full = jnp.pad(full, ((0, 0), (0, SSD_IN_PAD - SSD_IN_DIM)))
            lw[(n, l)] = full

    first_items = [("ssd_w_in", 0)]
    carry = {
        (0, "mm_in"): [("ssd_w_out", 0), ("ffn_w_in", 0)],
        (0, "scan"): [("ffn_w_out", 0)] + mixer_items(1),
        (0, "ffn_in"): ffn_items(1),
        (1, "sb"): mixer_items(2) + ffn_items(2),
        (2, "mm_in"): mixer_items(3),
        (2, "scan"): ffn_items(3),
    }
    arrived(first_items, comm_call(hook_for(first_items), "gather_first"))

    def carrying(i, slot, call):
        items = carry.get((i, slot), [])
        if not items:
            return call(None)
        out, got = call(hook_for(items))
        arrived(items, got)
        return out

    xcur = x[0]
    saved = []
    for i in range(DEPTH):
        j = i // 2
        h, r = rms_fwd(xcur, row(mix_norm[i]))
        if i % 2 == 0:
            proj = carrying(i, "mm_in", lambda hk: mm(h, lw[("ssd_w_in", j)], tm=2048, tn=896, name="mm_ssd_in", hook=hk))
            items = carry.get((i, "scan"), [])
            y3, core, got = ssd_core_fwd(proj, fw["ssd_conv_w"][j], row(ssd_conv_b[j]), ssd_dt_bias[j], ssd_a_log[j],
                                         ssd_d[j], row(ssd_norm[j]), hook_for(items))
            arrived(items, got)
            x1 = mm(y3, lw[("ssd_w_out", j)], res=xcur, name="mm_ssd_out")
            mix = (proj, y3, core)
        else:
            qkv = mm(h, lw[("sb_w_qkv", j)], out_dtype=MXU_DTYPE, tm=2048, name="mm_sb_qkv")
            o = carrying(i, "sb", lambda hk: sb_fwd(qkv, hk))
            x1 = mm(o, lw[("sb_w_out", j)], res=xcur, name="mm_sb_out")
            mix = (qkv, o)
        h2, r2 = rms_fwd(x1, row(ffn_norm[i]))
        u0 = carrying(i, "ffn_in", lambda hk: mm(h2, lw[("ffn_w_in", i)], tm=2048, name="mm_ffn_in", hook=hk))
        a = ffn_mid_fwd(u0, fw["ffn_conv_w"][i], row(ffn_conv_b[i]))
        x2 = mm(a, lw[("ffn_w_out", i)], res=x1, name="mm_ffn_out")
        saved.append((xcur, h, r, mix, x1, h2, r2, u0, a))
        xcur = x2
    loss_part, dx, dxm, d_final = loss_head(xcur, row(final_norm), loss_target[0])

    gl = {n: [None] * w[n].shape[0] for n in WEIGHTS if n != "final_norm"}
    units = {n: [None] * w[n].shape[0] for n in BIG}

    def unit_of(g4):
        return g4.reshape(N_CHIPS, 2, g4.shape[1] // 2, g4.shape[2])

    where = jnp.stack([chip, lax.axis_index("c")]).astype(jnp.int32)
    pair_f32, wire, from_chips = {}, {}, {}

    def pair_sums(keys, swapped):
        for (n, l), got in zip(keys, swapped):
            pair_f32[(n, l)], wire[(n, l)] = pair_sum(units[n][l], got, where)

    for i in reversed(range(DEPTH)):
        j = i // 2
        x0, h, r, mix, x1, h2, r2, u0, a = saved[i]
        units["ffn_w_out"][i] = unit_of(mm(a, dxm, "tn", tm=1408, name="mm_d_ffn_out").reshape(N_CHIPS, -1, D_MODEL))
        da = mm(dxm, lw[("ffn_w_out", i)], "nt", tn=1408, name="mm_da_ffn")
        du0, gl["ffn_conv_w"][i], dcb = ffn_mid_bwd(u0, da, fw["ffn_conv_w"][i], row(ffn_conv_b[i]))
        gl["ffn_conv_b"][i] = dcb[0]
        units["ffn_w_in"][i] = unit_of(mm(h2, du0, "tn", tn=1408, tm=512, n_split=N_CHIPS, name="mm_d_ffn_in"))
        keys_f = ffn_items(i)
        swap = pair_swap_hook([units[n][l] for n, l in keys_f])
        dh2, carried = mm(du0, lw[("ffn_w_in", i)], "nt", name="mm_dh_ffn", hook=swap)
        pair_sums(keys_f, split_carried([swap], carried)[0])
        dx1, dx1m, dg = rms_bwd(x1, r2, row(ffn_norm[i]), dh2, dx)
        gl["ffn_norm"][i] = dg[0]
        keys_up = mixer_items(i + 1) if i + 1 < DEPTH else []
        exchanges = [chip_exchange_hook([wire[k] for k in keys_f]),
                     chip_exchange_hook([wire[k] for k in keys_up]) if keys_up else None]
        if i % 2 == 0:
            proj, y3, core = mix
            units["ssd_w_out"][j] = unit_of(mm(y3, dx1m, "tn", name="mm_d_ssd_out").reshape(N_CHIPS, -1, D_MODEL))
            dy3 = mm(dx1m, lw[("ssd_w_out", j)], "nt", name="mm_dy3_ssd")
            (dproj, gl["ssd_conv_w"][j], dcb, gl["ssd_dt_bias"][j], gl["ssd_a_log"][j], gl["ssd_d"][j], dnorm,
             carried) = ssd_core_bwd(proj, fw["ssd_conv_w"][j], row(ssd_conv_b[j]), row(ssd_norm[j]), core, dy3,
                                     merge_hooks(exchanges))
            gl["ssd_conv_b"][j] = dcb[0]
            gl["ssd_norm"][j] = dnorm[0]
            dw_in = mm(h, dproj, "tn", tn=896, name="mm_d_ssd_in")[:, :SSD_IN_DIM]
            units["ssd_w_in"][j] = unit_of(jnp.transpose(dw_in.reshape(D_MODEL, N_CHIPS, -1), (1, 0, 2)))
            dmix, w_in, dh_name = dproj, lw[("ssd_w_in", j)], "mm_dh_ssd"
        else:
            qkv, o = mix
            units["sb_w_out"][j] = unit_of(mm(o, dx1m, "tn", name="mm_d_sb_out").reshape(N_CHIPS, -1, D_MODEL))
            do = mm(dx1m, lw[("sb_w_out", j)], "nt", out_dtype=MXU_DTYPE, name="mm_do_sb")
            dqkv, carried = sb_bwd(qkv, do, merge_hooks(exchanges))
            units["sb_w_qkv"][j] = unit_of(mm(h, dqkv, "tn", tn=768, n_split=N_CHIPS, name="mm_d_sb_qkv"))
            dmix, w_in, dh_name = dqkv, lw[("sb_w_qkv", j)], "mm_dh_sb"
        got = split_carried(exchanges, carried)
        from_chips.update(zip(keys_f, got[0]))
        if keys_up:
            from_chips.update(zip(keys_up, got[1]))
        keys_m = mixer_items(i)
        swap = pair_swap_hook([units[n][l] for n, l in keys_m])
        dh, carried = mm(dmix, w_in, "nt", name=dh_name, hook=swap)
        pair_sums(keys_m, split_carried([swap], carried)[0])
        dx, dxm, dg = rms_bwd(x0, r, row(mix_norm[i]), dh, dx1)
        gl["mix_norm"][i] = dg[0]
    last = chip_exchange_hook([wire[k] for k in mixer_items(0)])
    from_chips.update(zip(mixer_items(0), comm_call(last, "grad_exchange_last")))

    layout = [(k, l) for k, n in enumerate(BIG) for l in range(w[n].shape[0])]
    reduced = _finish_big_grads([pair_f32[(BIG[k], l)] for k, l in layout], [from_chips[(BIG[k], l)] for k, l in layout],
                                layout)
    g, delta, new_m, new_v = {}, {}, {}, {}
    two_d = lambda t: t.reshape(-1, t.shape[-1])
    for n, red in zip(BIG, reduced):
        g[n] = red.reshape(w[n].shape)
        d2, m2, v2 = adamw(two_d(w[n]), two_d(g[n]), two_d(mom[n]), two_d(var[n]), name="adamw_" + n)
        delta[n], new_m[n], new_v[n] = d2.reshape(w[n].shape), m2.reshape(w[n].shape), v2.reshape(w[n].shape)

    small_g = {n: jnp.stack(gl[n]) for n in SMALL + CONVW if n != "final_norm"}
    small_g["final_norm"] = d_final[0]
    zeros_of = lambda n: jnp.zeros((small_g[n].size,), F32)

    def small_pack(d, extra):
        parts = [d[n].reshape(-1) for n in SMALL] + [extra]
        parts += [(d[n].reshape(-1) if d is small_g else zeros_of(n)) for n in CONVW]
        return _to_rows(jnp.concatenate(parts), 16)

    part = small_pack(small_g, loss_part[0, 0:1])
    parts = all_gather_8(jnp.stack([part, part]), "gather_small_grads")
    zero = jnp.zeros((1,), F32)
    gs, ds, ms, vs = adamw_small(small_pack(w, zero), parts, small_pack(mom, zero), small_pack(var, zero))
    gs_flat = gs.reshape(-1)
    off = 0
    for n in SMALL:
        size = w[n].size
        for dst, src in ((g, gs), (delta, ds), (new_m, ms), (new_v, vs)):
            dst[n] = src.reshape(-1)[off:off + size].reshape(w[n].shape)
        off += size
    loss = gs_flat[off]
    off += 1
    for n in CONVW:
        size = small_g[n].size
        b = w[n].shape[-1]
        g[n] = lax.dynamic_slice_in_dim(gs_flat[off:off + size].reshape(small_g[n].shape), chip * b, b, axis=2)
        d2, m2, v2 = adamw(two_d(w[n]), two_d(g[n]), two_d(mom[n]), two_d(var[n]), name="adamw_" + n)
        delta[n], new_m[n], new_v[n] = d2.reshape(w[n].shape), m2.reshape(w[n].shape), v2.reshape(w[n].shape)
        off += size

    return (loss, dx[None], *[g[n] for n in WEIGHTS], *[delta[n] for n in WEIGHTS],
            *[new_m[n] for n in WEIGHTS], *[new_v[n] for n in WEIGHTS])
```

```python
import functools

import jax
import jax.numpy as jnp
from jax import lax
from jax.experimental import pallas as pl
from jax.experimental.pallas import tpu as pltpu

F32 = jnp.float32
MXU_DTYPE = jnp.bfloat16
HIGHEST = lax.Precision.HIGHEST

D_MODEL = 1024
DEPTH = 4
NORM_EPS = 1e-6
SSD_D_INNER = 2048
SSD_HEAD_DIM = 64
SSD_HEADS = 32
SSD_GROUPS = 8
SSD_STATE = 128
SSD_CONV = 4
SSD_CHUNK = 128
SSD_CONV_DIM = 4096
SSD_IN_DIM = 6176
SSD_IN_PAD = 6272
SB_HEADS = 16
SB_HEAD_DIM = 64
FFN_D_FF = 2816
FFN_CONV = 3
ADAM_LR, ADAM_B1, ADAM_B2, ADAM_EPS, ADAM_WD, ADAM_STEP = 0.001, 0.9, 0.999, 1e-08, 0.01, 10

LANES = 128
SUBLANES = 8
VMEM_LIMIT = 56 * 1024 * 1024
MESH = pl.DeviceIdType.MESH


def _params(sem=None):
    return pltpu.CompilerParams(dimension_semantics=sem, vmem_limit_bytes=VMEM_LIMIT)


def _sds(shape, dtype):
    return jax.ShapeDtypeStruct(shape, dtype)


def _dot(a, b, dims=(((1,), (0,)), ((), ())), precision=None):
    return lax.dot_general(a, b, dims, precision=precision, preferred_element_type=F32)


_NN = (((1,), (0,)), ((), ()))
_NT = (((1,), (1,)), ((), ()))
_TN = (((0,), (0,)), ((), ()))


def _mx(a):
    return a.astype(MXU_DTYPE)


def _sigmoid(x):
    return 0.5 * jnp.tanh(0.5 * x) + 0.5


def _silu(x):
    return x * _sigmoid(x)


def _silu_and_grad(x):
    s = _sigmoid(x)
    return x * s, s * (1.0 + x * (1.0 - s))


def _pick(n, cands):
    for c in cands:
        if n % c == 0:
            return c
    return n


def rms_fwd(x, g):
    s, d = x.shape
    ts = _pick(s, (512, 256, 128))

    def body(x_ref, g_ref, h_ref, r_ref):
        xv = x_ref[...]
        r = lax.rsqrt(jnp.mean(xv * xv, axis=-1, keepdims=True) + NORM_EPS)
        h_ref[...] = (xv * r * g_ref[...]).astype(h_ref.dtype)
        r_ref[...] = r

    return pl.pallas_call(
        body, name="rms_fwd", grid=(s // ts,),
        in_specs=[pl.BlockSpec((ts, d), lambda i: (i, 0)), pl.BlockSpec((1, d), lambda i: (0, 0))],
        out_specs=[pl.BlockSpec((ts, d), lambda i: (i, 0)), pl.BlockSpec((ts, 1), lambda i: (i, 0))],
        out_shape=[_sds((s, d), MXU_DTYPE), _sds((s, 1), F32)],
        compiler_params=_params(("parallel",)),
    )(x, g)


def rms_bwd(x, r, g, dh, dres):
    s, d = x.shape
    ts = _pick(s, (512, 256, 128))

    def body(x_ref, r_ref, g_ref, dh_ref, dres_ref, dx_ref, dxm_ref, dg_ref):
        xh = x_ref[...] * r_ref[...]
        dhv = dh_ref[...]
        dxh = dhv * g_ref[...]
        dx = dres_ref[...] + r_ref[...] * (dxh - xh * jnp.mean(dxh * xh, axis=-1, keepdims=True))
        dx_ref[...] = dx
        dxm_ref[...] = dx.astype(dxm_ref.dtype)
        part = jnp.sum(dhv * xh, axis=0, keepdims=True)

        @pl.when(pl.program_id(0) == 0)
        def _():
            dg_ref[...] = part

        @pl.when(pl.program_id(0) != 0)
        def _():
            dg_ref[...] += part

    row = pl.BlockSpec((ts, d), lambda i: (i, 0))
    return pl.pallas_call(
        body, name="rms_bwd", grid=(s // ts,),
        in_specs=[row, pl.BlockSpec((ts, 1), lambda i: (i, 0)), pl.BlockSpec((1, d), lambda i: (0, 0)), row, row],
        out_specs=[row, row, pl.BlockSpec((1, d), lambda i: (0, 0))],
        out_shape=[_sds((s, d), F32), _sds((s, d), MXU_DTYPE), _sds((1, d), F32)],
        compiler_params=_params(("arbitrary",)),
    )(x, r, g, dh, dres)


def mm(a, b, mode="nn", res=None, out_dtype=F32, tm=None, tn=None, n_split=1, name="mm", hook=None):
    halves_a = mode == "nt" and a.ndim == 3
    halves_b = mode == "tn" and b.ndim == 3
    if halves_a:
        a_shape = (a.shape[1], 2 * a.shape[2])
    else:
        a_shape = a.shape
    b_shape = (b.shape[1], 2 * b.shape[2]) if halves_b else b.shape
    if mode == "nn":
        (m, k), (_, n) = a_shape, b_shape
    elif mode == "nt":
        (m, k), (n, _) = a_shape, b_shape
    else:
        (k, m), (_, n) = a_shape, b_shape
    tm = min(tm, m) if tm else _pick(m, (1024, 512, 256, 128))
    tn = min(tn, n) if tn else _pick(n, (512, 896, 256, 128))
    assert m % tm == 0 and n % tn == 0, (m, tm, n, tn)
    dims = {"nn": _NN, "nt": _NT, "tn": _TN}[mode]

    def body(*refs):
        a_ref, b_ref = refs[0], refs[1]
        o_ref = refs[-1]
        if halves_a:
            kh = k // 2
            acc = _dot(_mx(a_ref[0]), _mx(b_ref[:, :kh]), dims) + _dot(_mx(a_ref[1]), _mx(b_ref[:, kh:]), dims)
        else:
            acc = _dot(_mx(a_ref[...]), _mx(b_ref[...]), dims)
        if res is not None:
            acc = acc + refs[2][...]
        o_ref[...] = acc.astype(o_ref.dtype)

    a_spec = pl.BlockSpec((k, tm), lambda i, j: (0, i)) if mode == "tn" else pl.BlockSpec((tm, k), lambda i, j: (i, 0))
    b_spec = pl.BlockSpec((tn, k), lambda i, j: (j, 0)) if mode == "nt" else pl.BlockSpec((k, tn), lambda i, j: (0, j))
    if halves_a:
        a_spec = pl.BlockSpec((2, tm, k // 2), lambda i, j: (0, i, 0))
    if halves_b:
        per_half = n // 2 // tn
        assert per_half * tn * 2 == n
        b_spec = pl.BlockSpec((None, k, tn), lambda i, j: (j // per_half, 0, j % per_half))
    o_spec = pl.BlockSpec((tm, tn), lambda i, j: (i, j))
    ins, specs = [a, b], [a_spec, b_spec]
    if res is not None:
        ins.append(res)
        specs.append(o_spec)
    out_shape = _sds((m, n), out_dtype)
    if n_split > 1:
        per = n // n_split // tn
        o_spec = pl.BlockSpec((None, tm, tn), lambda i, j: (j // per, i, j % per))
        out_shape = _sds((n_split, m, n // n_split), out_dtype)
    out, carried = hosted_call(body, hook, name, (m // tm, n // tn), specs, o_spec, out_shape, [],
                               ("parallel", "parallel"), ins)
    return out if hook is None else (out, carried)


CONV_ROWS = 256
CONV_COLS = 128


def _row_iota8(cols):
    return lax.broadcasted_iota(jnp.int32, (SUBLANES, cols), 0)


def _shift_down(cur, prev8, k):
    if k == 0:
        return cur
    rolled = pltpu.roll(cur, k, 0)
    head = jnp.where(_row_iota8(cur.shape[1]) < k, pltpu.roll(prev8, k, 0), rolled[0:SUBLANES])
    return jnp.concatenate([head, rolled[SUBLANES:]], axis=0)


def _shift_up(cur, next8, k):
    if k == 0:
        return cur
    n = cur.shape[0]
    rolled = pltpu.roll(cur, n - k, 0)
    tail = jnp.where(_row_iota8(cur.shape[1]) >= SUBLANES - k, pltpu.roll(next8, SUBLANES - k, 0), rolled[n - SUBLANES:])
    return jnp.concatenate([rolled[:n - SUBLANES], tail], axis=0)


def _load_prev8(ref, i, rows):
    start = pl.multiple_of(jnp.maximum(i * rows - SUBLANES, 0), SUBLANES)
    p = ref[pl.ds(start, SUBLANES), :]
    return jnp.where(i > 0, p, jnp.zeros_like(p))


def _conv_rows(ref, w_ref, b_ref, i, rows, width):
    cur = ref[pl.ds(pl.multiple_of(i * rows, rows), rows), :]
    prev8 = _load_prev8(ref, i, rows)
    shifted = [_shift_down(cur, prev8, k) for k in range(width)]
    acc = b_ref[...] + w_ref[width - 1:width, :] * shifted[0]
    for k in range(1, width):
        acc = acc + w_ref[width - 1 - k:width - k, :] * shifted[k]
    return acc, shifted


def _conv_bwd_rows(du, next8, w_ref, width):
    acc = w_ref[width - 1:width, :] * du
    for k in range(1, width):
        acc = acc + w_ref[width - 1 - k:width - k, :] * _shift_up(du, next8, k)
    return acc


def ffn_mid_fwd(u0, cw, cb):
    s, f2 = u0.shape
    f = f2 // 2
    nt = f // CONV_COLS
    rows = min(CONV_ROWS, s)

    def body(ug_ref, uu_ref, wg_ref, wu_ref, bg_ref, bu_ref, a_ref):
        def step(i, carry):
            g, _ = _conv_rows(ug_ref, wg_ref, bg_ref, i, rows, FFN_CONV)
            u, _ = _conv_rows(uu_ref, wu_ref, bu_ref, i, rows, FFN_CONV)
            a_ref[pl.ds(pl.multiple_of(i * rows, rows), rows), :] = (_silu(g) * u).astype(a_ref.dtype)
            return carry

        lax.fori_loop(0, s // rows, step, 0)

    col = lambda off: pl.BlockSpec((s, CONV_COLS), lambda j: (0, j + off))
    wsp = lambda r, off: pl.BlockSpec((r, CONV_COLS), lambda j: (0, j + off))
    return pl.pallas_call(
        body, name="ffn_mid_fwd", grid=(nt,),
        in_specs=[col(0), col(nt), wsp(FFN_CONV, 0), wsp(FFN_CONV, nt), wsp(1, 0), wsp(1, nt)],
        out_specs=pl.BlockSpec((s, CONV_COLS), lambda j: (0, j)),
        out_shape=_sds((s, f), MXU_DTYPE), compiler_params=_params(("parallel",)),
    )(u0, u0, cw, cw, cb, cb)


def ffn_mid_bwd(u0, da, cw, cb):
    s, f2 = u0.shape
    f = f2 // 2
    nt = f // CONV_COLS
    rows = min(CONV_ROWS, s)
    nsteps = s // rows
    w = FFN_CONV

    def body(ug_ref, uu_ref, da_ref, wg_ref, wu_ref, bg_ref, bu_ref,
             du0_ref, dwg_ref, dwu_ref, dbg_ref, dbu_ref):
        zero8 = jnp.zeros((SUBLANES, CONV_COLS), F32)
        zrow = jnp.zeros((1, CONV_COLS), F32)

        def step(it, carry):
            ng, nu, accs = carry
            i = nsteps - 1 - it
            r0 = pl.multiple_of(i * rows, rows)
            g, sg = _conv_rows(ug_ref, wg_ref, bg_ref, i, rows, w)
            u, su = _conv_rows(uu_ref, wu_ref, bu_ref, i, rows, w)
            dav = da_ref[pl.ds(r0, rows), :]
            sg_val, sg_grad = _silu_and_grad(g)
            dg = dav * u * sg_grad
            du = dav * sg_val
            du0_ref[0, pl.ds(r0, rows), :] = _conv_bwd_rows(dg, ng, wg_ref, w).astype(du0_ref.dtype)
            du0_ref[1, pl.ds(r0, rows), :] = _conv_bwd_rows(du, nu, wu_ref, w).astype(du0_ref.dtype)
            new = []
            for j in range(w):
                new.append(accs[j] + jnp.sum(dg * sg[w - 1 - j], axis=0, keepdims=True))
            for j in range(w):
                new.append(accs[w + j] + jnp.sum(du * su[w - 1 - j], axis=0, keepdims=True))
            new.append(accs[2 * w] + jnp.sum(dg, axis=0, keepdims=True))
            new.append(accs[2 * w + 1] + jnp.sum(du, axis=0, keepdims=True))
            return dg[0:SUBLANES], du[0:SUBLANES], tuple(new)

        _, _, accs = lax.fori_loop(0, nsteps, step, (zero8, zero8, tuple([zrow] * (2 * w + 2))))
        dwg_ref[...] = jnp.concatenate(accs[0:w], axis=0)
        dwu_ref[...] = jnp.concatenate(accs[w:2 * w], axis=0)
        dbg_ref[...] = accs[2 * w]
        dbu_ref[...] = accs[2 * w + 1]

    col = lambda off: pl.BlockSpec((s, CONV_COLS), lambda j: (0, j + off))
    wsp = lambda r, off: pl.BlockSpec((r, CONV_COLS), lambda j: (0, j + off))
    outs = pl.pallas_call(
        body, name="ffn_mid_bwd", grid=(nt,),
        in_specs=[col(0), col(nt), col(0), wsp(w, 0), wsp(w, nt), wsp(1, 0), wsp(1, nt)],
        out_specs=[pl.BlockSpec((2, s, CONV_COLS), lambda j: (0, 0, j)), wsp(w, 0), wsp(w, 0), wsp(1, 0), wsp(1, 0)],
        out_shape=[_sds((2, s, f), MXU_DTYPE), _sds((w, f), F32), _sds((w, f), F32), _sds((1, f), F32), _sds((1, f), F32)],
        compiler_params=_params(("parallel",)),
    )(u0, u0, da, cw, cw, cb, cb)
    du0, dwg, dwu, dbg, dbu = outs
    return du0, jnp.concatenate([dwg, dwu], axis=1), jnp.concatenate([dbg, dbu], axis=1)


SB_BLOCK = 128
SB_DEAD = 88.0
SB_HEADS_PER_STEP = 4


def _split_hi_lo(x):
    hi = x.astype(MXU_DTYPE)
    lo = (x - hi.astype(F32)).astype(MXU_DTYPE)
    return hi, lo


def _dot_exact01(x, tri):
    hi, lo = _split_hi_lo(x)
    return _dot(hi, tri) + _dot(lo, tri)


def _stack_heads(pair, lane_lo):
    zero = jnp.zeros_like(pair)
    return jnp.concatenate([jnp.where(lane_lo, pair, zero), jnp.where(lane_lo, zero, pair)], axis=0)


def _unstack_heads(tall, lane_lo):
    n = tall.shape[0] // 2
    return jnp.where(lane_lo, tall[:n], tall[n:])


def _sb_logits(stacked_q, k_ref, k0, pair_cols, blk):
    z = [_dot(sq, k_ref[pl.ds(k0, blk), cols], _NT) for sq, cols in zip(stacked_q, pair_cols)]
    return jnp.concatenate(z, axis=0) * (SB_HEAD_DIM ** -0.5)


def _sb_logs(z, blk, diagonal):
    t = jnp.log(1.0 + jnp.exp(-jnp.abs(z)))
    lb = jnp.minimum(z, 0.0) - t
    lf = jnp.minimum(-z, 0.0) - t
    if not diagonal:
        return lb, lf, None
    strict = lax.broadcasted_iota(jnp.int32, z.shape, 1) < (lax.broadcasted_iota(jnp.int32, z.shape, 0) & (blk - 1))
    return lb, jnp.where(strict, lf, 0.0), strict


def _keep(strict, x):
    return x if strict is None else jnp.where(strict, x, 0.0)


def _tri(blk, upper):
    r = lax.broadcasted_iota(jnp.int32, (blk, blk), 0)
    c = lax.broadcasted_iota(jnp.int32, (blk, blk), 1)
    return jnp.where((r > c) if upper else (r < c), 1.0, 0.0).astype(MXU_DTYPE)


def _tri_sum(x, tri2):
    hi, lo = _split_hi_lo(x)
    return _dot(jnp.concatenate([hi, lo], axis=1), tri2)


def sb_fwd(qkv, hook=None):
    s = qkv.shape[0]
    blk = min(SB_BLOCK, s)
    nblk = s // blk
    nh = SB_HEADS_PER_STEP
    nstep = SB_HEADS // nh
    dh = SB_HEAD_DIM

    def body(q_ref, k_ref, v_ref, o_ref):
        suffix_tri2 = jnp.concatenate([_tri(blk, True)] * 2, axis=0)
        lane_lo = lax.broadcasted_iota(jnp.int32, (1, LANES), 1) < dh
        pairs = [slice(p * LANES, (p + 1) * LANES) for p in range(nh // 2)]

        def qstep(qi, carry):
            q0 = pl.multiple_of(qi * blk, blk)
            qst = [_stack_heads(q_ref[pl.ds(q0, blk), cols], lane_lo) for cols in pairs]

            def tile(kb, run, accs, diagonal):
                k0 = pl.multiple_of(kb * blk, blk)
                lb, lf, strict = _sb_logs(_sb_logits(qst, k_ref, k0, pairs, blk), blk, diagonal)
                sloc = _tri_sum(lf, suffix_tri2)
                a = _mx(_keep(strict, jnp.exp(lb + sloc + run)))
                accs = tuple(
                    acc + _unstack_heads(_dot(a[2 * blk * p:2 * blk * (p + 1)], v_ref[pl.ds(k0, blk), cols]), lane_lo)
                    for p, (acc, cols) in enumerate(zip(accs, pairs)))
                run = run + sloc[:, 0:1] + lf[:, 0:1]
                return run, accs, jnp.max(run) > -SB_DEAD

            def kstep(st):
                it, run, accs, _ = st
                return (it + 1, *tile(qi - it, run, accs, False))

            first = tile(qi, jnp.zeros((nh * blk, 1), F32), tuple([jnp.zeros((blk, LANES), F32)] * len(pairs)), True)
            _, _, accs, _ = lax.while_loop(lambda st: jnp.logical_and(st[0] <= qi, st[3]), kstep, (jnp.int32(1), *first))
            for acc, cols in zip(accs, pairs):
                o_ref[pl.ds(q0, blk), cols] = acc.astype(o_ref.dtype)
            return carry

        lax.fori_loop(0, nblk, qstep, 0)

    col = lambda off: pl.BlockSpec((s, nh * dh), lambda p: (0, p + off))
    out, carried = hosted_call(body, hook, "sb_fwd", (nstep,), [col(0), col(nstep), col(2 * nstep)], col(0),
                               _sds((s, D_MODEL), MXU_DTYPE), [], ("parallel",), (qkv, qkv, qkv))
    return out if hook is None else (out, carried)


def sb_bwd(qkv, do, hook=None):
    s = qkv.shape[0]
    blk = min(SB_BLOCK, s)
    nblk = s // blk
    nh = SB_HEADS_PER_STEP
    nstep = SB_HEADS // nh
    dh = SB_HEAD_DIM

    def body(q_ref, k_ref, v_ref, do_ref, dq_ref, dk_ref, dv_ref, dk_acc, dv_acc, run_ref):
        suffix_tri2 = jnp.concatenate([_tri(blk, True)] * 2, axis=0)
        prefix_tri2 = jnp.concatenate([_tri(blk, False)] * 2, axis=0)
        dk_acc[...] = jnp.zeros_like(dk_acc)
        dv_acc[...] = jnp.zeros_like(dv_acc)
        lane_lo = lax.broadcasted_iota(jnp.int32, (1, LANES), 1) < dh
        pairs = [slice(p * LANES, (p + 1) * LANES) for p in range(nh // 2)]

        def qstep(qi, carry):
            q0 = pl.multiple_of(qi * blk, blk)
            qst = [_stack_heads(q_ref[pl.ds(q0, blk), cols], lane_lo) for cols in pairs]
            dost = [_stack_heads(do_ref[pl.ds(q0, blk), cols], lane_lo) for cols in pairs]

            def enter(kb, run, diagonal):
                run_ref[kb] = run
                _, lf, _ = _sb_logs(_sb_logits(qst, k_ref, pl.multiple_of(kb * blk, blk), pairs, blk), blk, diagonal)
                run = run + jnp.sum(lf, axis=1, keepdims=True)
                return run, jnp.max(run) > -SB_DEAD

            def sweep1(st):
                it, run, _ = st
                return (it + 1, *enter(qi - it, run, False))

            nlive, _, _ = lax.while_loop(lambda st: jnp.logical_and(st[0] <= qi, st[2]), sweep1,
                                         (jnp.int32(1), *enter(qi, jnp.zeros((nh * blk, 1), F32), True)))

            def tile(kb, pg, dqs, diagonal):
                k0 = pl.multiple_of(kb * blk, blk)
                lb, lf, strict = _sb_logs(_sb_logits(qst, k_ref, k0, pairs, blk), blk, diagonal)
                sloc = _tri_sum(lf, suffix_tri2)
                a = _keep(strict, jnp.exp(lb + sloc + run_ref[kb]))
                da = jnp.concatenate([_dot(d, v_ref[pl.ds(k0, blk), cols], _NT) for d, cols in zip(dost, pairs)], axis=0)
                g = da * a
                p = pg + _tri_sum(g, prefix_tri2)
                sig = jnp.exp(lb)
                dz = _mx(_keep(strict, g * (1.0 - sig) - p * sig) * (dh ** -0.5))
                am = _mx(a)
                new_dqs = []
                for i, cols in enumerate(pairs):
                    rows = slice(2 * blk * i, 2 * blk * (i + 1))
                    new_dqs.append(dqs[i] + _unstack_heads(_dot(dz[rows], k_ref[pl.ds(k0, blk), cols]), lane_lo))
                    dk_acc[pl.ds(k0, blk), cols] += _dot(dz[rows], qst[i], _TN)
                    dv_acc[pl.ds(k0, blk), cols] += _dot(am[rows], dost[i], _TN)
                return pg + jnp.sum(g, axis=1, keepdims=True), tuple(new_dqs)

            pg, dqs = lax.fori_loop(qi + 1 - nlive, qi, lambda kb, st: tile(kb, *st, False),
                                    (jnp.zeros((nh * blk, 1), F32), tuple([jnp.zeros((blk, LANES), F32)] * len(pairs))))
            _, dqs = tile(qi, pg, dqs, True)
            for dq, cols in zip(dqs, pairs):
                dq_ref[pl.ds(q0, blk), cols] = dq.astype(dq_ref.dtype)
            return carry

        lax.fori_loop(0, nblk, qstep, 0)
        dk_ref[...] = dk_acc[...].astype(dk_ref.dtype)
        dv_ref[...] = dv_acc[...].astype(dv_ref.dtype)

    col = lambda off: pl.BlockSpec((s, nh * dh), lambda p: (0, p + off))
    (dq, dk, dv), carried = hosted_call(
        body, hook, "sb_bwd", (nstep,), [col(0), col(nstep), col(2 * nstep), col(0)], [col(0), col(0), col(0)],
        [_sds((s, D_MODEL), MXU_DTYPE)] * 3,
        [pltpu.VMEM((s, nh * dh), F32), pltpu.VMEM((s, nh * dh), F32), pltpu.VMEM((nblk, nh * blk, 1), F32)],
        ("parallel",), (qkv, qkv, qkv, do))
    return jnp.concatenate([dq, dk, dv], axis=1), carried


SSD_XBC_TILE0 = SSD_D_INNER // CONV_COLS


def ssd_conv_fwd(proj, cw, cb):
    s = proj.shape[0]
    rows = min(CONV_ROWS, s)

    def body(u_ref, w_ref, b_ref, o_ref):
        def step(i, carry):
            u, _ = _conv_rows(u_ref, w_ref, b_ref, i, rows, SSD_CONV)
            o_ref[pl.ds(pl.multiple_of(i * rows, rows), rows), :] = _silu(u)
            return carry

        lax.fori_loop(0, s // rows, step, 0)

    return pl.pallas_call(
        body, name="ssd_conv_fwd", grid=(SSD_CONV_DIM // CONV_COLS,),
        in_specs=[pl.BlockSpec((s, CONV_COLS), lambda j: (0, j + SSD_XBC_TILE0)),
                  pl.BlockSpec((SSD_CONV, CONV_COLS), lambda j: (0, j)), pl.BlockSpec((1, CONV_COLS), lambda j: (0, j))],
        out_specs=pl.BlockSpec((s, CONV_COLS), lambda j: (0, j)),
        out_shape=_sds((s, SSD_CONV_DIM), F32), compiler_params=_params(("parallel",)),
    )(proj, cw, cb)


def ssd_conv_bwd(proj, dact, cw, cb):
    s = proj.shape[0]
    rows = min(CONV_ROWS, s)
    nsteps = s // rows
    w = SSD_CONV

    def body(u_ref, da_ref, w_ref, b_ref, du_ref, dw_ref, db_ref):
        def step(it, carry):
            nxt, accs = carry
            i = nsteps - 1 - it
            r0 = pl.multiple_of(i * rows, rows)
            u, sh = _conv_rows(u_ref, w_ref, b_ref, i, rows, w)
            dconv = da_ref[pl.ds(r0, rows), :] * _silu_and_grad(u)[1]
            du_ref[pl.ds(r0, rows), :] = _conv_bwd_rows(dconv, nxt, w_ref, w).astype(du_ref.dtype)
            new = [accs[j] + jnp.sum(dconv * sh[w - 1 - j], axis=0, keepdims=True) for j in range(w)]
            new.append(accs[w] + jnp.sum(dconv, axis=0, keepdims=True))
            return dconv[0:SUBLANES], tuple(new)

        zrow = jnp.zeros((1, CONV_COLS), F32)
        _, accs = lax.fori_loop(0, nsteps, step, (jnp.zeros((SUBLANES, CONV_COLS), F32), tuple([zrow] * (w + 1))))
        dw_ref[...] = jnp.concatenate(accs[0:w], axis=0)
        db_ref[...] = accs[w]

    col = pl.BlockSpec((s, CONV_COLS), lambda j: (0, j))
    return pl.pallas_call(
        body, name="ssd_conv_bwd", grid=(SSD_CONV_DIM // CONV_COLS,),
        in_specs=[pl.BlockSpec((s, CONV_COLS), lambda j: (0, j + SSD_XBC_TILE0)), col,
                  pl.BlockSpec((w, CONV_COLS), lambda j: (0, j)), pl.BlockSpec((1, CONV_COLS), lambda j: (0, j))],
        out_specs=[col, pl.BlockSpec((w, CONV_COLS), lambda j: (0, j)), pl.BlockSpec((1, CONV_COLS), lambda j: (0, j))],
        out_shape=[_sds((s, SSD_CONV_DIM), MXU_DTYPE), _sds((w, SSD_CONV_DIM), F32), _sds((1, SSD_CONV_DIM), F32)],
        compiler_params=_params(("parallel",)),
    )(proj, dact, cw, cb)


def _split3(x):
    hi = x.astype(MXU_DTYPE)
    r1 = x - hi.astype(F32)
    mid = r1.astype(MXU_DTYPE)
    lo = (r1 - mid.astype(F32)).astype(MXU_DTYPE)
    return hi, mid, lo


def _dot01(x, m, dims=_NN, left=False):
    parts = _split3(x)
    if left:
        return _dot(m, parts[0], dims) + _dot(m, parts[1], dims) + _dot(m, parts[2], dims)
    return _dot(parts[0], m, dims) + _dot(parts[1], m, dims) + _dot(parts[2], m, dims)


def _softplus(x):
    return jnp.maximum(x, 0.0) + jnp.log1p(jnp.exp(-jnp.abs(x)))


def _ssd_consts(dt_bias, a_log, d_skip):
    pad = lambda v: jnp.pad(v.reshape(1, SSD_HEADS), ((0, 0), (0, LANES - SSD_HEADS)))
    head_of = jnp.arange(SSD_D_INNER) // SSD_HEAD_DIM
    expand = (jnp.arange(LANES)[:, None] == head_of[None, :]).astype(MXU_DTYPE)
    return dict(bias_w=pad(dt_bias), alog_w=pad(a_log), bias_c=dt_bias.reshape(SSD_HEADS, 1),
                alog_c=a_log.reshape(SSD_HEADS, 1), dskip=jnp.repeat(d_skip, SSD_HEAD_DIM).reshape(1, SSD_D_INNER),
                expand=expand, reduce=expand.T)


def _expand_heads(x):
    first = lax.broadcasted_iota(jnp.int32, (1, LANES), 1) < SSD_HEAD_DIM
    shape = (x.shape[0], LANES)
    tiles = [jnp.where(first, jnp.broadcast_to(x[:, 2 * p:2 * p + 1], shape), jnp.broadcast_to(x[:, 2 * p + 1:2 * p + 2], shape))
             for p in range(SSD_HEADS // 2)]
    return jnp.concatenate(tiles, axis=1)


def _ssd_chunk_prep(dtp, dtp_t, bias_w, alog_w, bias_c, alog_c, expand):
    L = dtp.shape[0]
    r = lax.broadcasted_iota(jnp.int32, (L, L), 0)
    c = lax.broadcasted_iota(jnp.int32, (L, L), 1)
    tril = r >= c
    lower = jnp.where(tril, 1.0, 0.0).astype(MXU_DTYPE)
    upper = jnp.where(r <= c, 1.0, 0.0).astype(MXU_DTYPE)
    dt_col = _softplus(dtp + bias_w)
    a_col = -jnp.exp(alog_w) * dt_col
    a_row = -jnp.exp(alog_c) * _softplus(dtp_t + bias_c)
    acum_col = _dot01(a_col, lower, left=True)
    acum_row = _dot01(a_row, upper)
    acum_full = _expand_heads(acum_col)
    dt_full = _expand_heads(dt_col)
    return dict(tril=tril, lower=lower, upper=upper, dt_col=dt_col, a_col=a_col, acum_col=acum_col,
                acum_row=acum_row, acum_full=acum_full, dt_full=dt_full)


def _head_mask(j):
    lane = lax.broadcasted_iota(jnp.int32, (1, LANES), 1)
    return jnp.where((lane // SSD_HEAD_DIM) == j, 1.0, 0.0)


def _decay(pre, h):
    seg = pre["acum_col"][:, h:h + 1] - pre["acum_row"][h:h + 1, :]
    return jnp.exp(jnp.where(pre["tril"], seg, -1e30))


def _ssd_specs(s, nc, rev):
    L = SSD_CHUNK
    ci = (lambda i: nc - 1 - i) if rev else (lambda i: i)
    const = lambda shape: pl.BlockSpec(shape, lambda i: (0,) * len(shape))
    return dict(
        xbc=pl.BlockSpec((L, SSD_CONV_DIM), lambda i: (ci(i), 0)),
        dtp=pl.BlockSpec((L, LANES), lambda i: (ci(i), SSD_IN_PAD // LANES - 1)),
        dtp_t=pl.BlockSpec((SSD_HEADS, L), lambda i: (0, ci(i))),
        rows=pl.BlockSpec((L, SSD_D_INNER), lambda i: (ci(i), 0)),
        state=pl.BlockSpec((1, SSD_GROUPS, SSD_STATE, 4 * SSD_HEAD_DIM), lambda i: (ci(i), 0, 0, 0)),
        consts=[const((1, LANES)), const((1, LANES)), const((SSD_HEADS, 1)), const((SSD_HEADS, 1)),
                const((1, SSD_D_INNER)), const((LANES, SSD_D_INNER)), const((SSD_D_INNER, LANES))],
    )


def _const_args(cs):
    return [cs["bias_w"], cs["alog_w"], cs["bias_c"], cs["alog_c"], cs["dskip"], cs["expand"], cs["reduce"]]


def ssd_scan_fwd(act, proj, dtp_t, cs, hook=None):
    s = act.shape[0]
    L = SSD_CHUNK
    nc = s // L
    G, N, GW = SSD_GROUPS, SSD_STATE, 4 * SSD_HEAD_DIM

    def body(act_ref, dtp_ref, dtpt_ref, bw_ref, aw_ref, bc_ref, ac_ref, dsk_ref, ex_ref, rd_ref, y_ref, st_out, st):
        @pl.when(pl.program_id(0) == 0)
        def _():
            st[...] = jnp.zeros_like(st)

        st_out[0] = st[...]
        pre = _ssd_chunk_prep(dtp_ref[...], dtpt_ref[...], bw_ref[...], aw_ref[...], bc_ref[...], ac_ref[...], ex_ref[...])
        acum_full = pre["acum_full"]
        last_full = acum_full[L - 1:L, :]
        for g in range(G):
            bg = _mx(act_ref[:, SSD_D_INNER + g * N:SSD_D_INNER + (g + 1) * N])
            cg = _mx(act_ref[:, SSD_D_INNER + G * N + g * N:SSD_D_INNER + G * N + (g + 1) * N])
            cb = _dot(cg, bg, _NT)
            for half in range(2):
                p = 2 * g + half
                cols = slice(p * LANES, (p + 1) * LANES)
                xs = act_ref[:, cols]
                xdt = xs * pre["dt_full"][:, cols]
                yd = jnp.zeros((L, LANES), F32)
                for j in range(2):
                    m = cb * _decay(pre, 2 * p + j)
                    yd = yd + _dot(_mx(m), _mx(xdt * _head_mask(j)))
                yoff = _dot(cg, _mx(st[g, :, half * LANES:(half + 1) * LANES])) * jnp.exp(acum_full[:, cols])
                y_ref[:, cols] = yd + yoff + dsk_ref[:, cols] * xs
                w = jnp.exp(last_full[:, cols] - acum_full[:, cols])
                st[g, :, half * LANES:(half + 1) * LANES] = (
                    st[g, :, half * LANES:(half + 1) * LANES] * jnp.exp(last_full[:, cols]) + _dot(bg, _mx(xdt * w), _TN))

    sp = _ssd_specs(s, nc, False)
    (y, states), carried = hosted_call(
        body, hook, "ssd_scan_fwd", (nc,), [sp["xbc"], sp["dtp"], sp["dtp_t"]] + sp["consts"],
        [sp["rows"], sp["state"]], [_sds((s, SSD_D_INNER), F32), _sds((nc, G, N, GW), F32)],
        [pltpu.VMEM((G, N, GW), F32)], ("arbitrary",), (act, proj, dtp_t, *_const_args(cs)))
    return y, states, carried


def ssd_scan_bwd(act, proj, dtp_t, cs, states, dy, hook=None):
    s = act.shape[0]
    L = SSD_CHUNK
    nc = s // L
    G, N, GW = SSD_GROUPS, SSD_STATE, 4 * SSD_HEAD_DIM

    def body(act_ref, dtp_ref, dtpt_ref, bw_ref, aw_ref, bc_ref, ac_ref, dsk_ref, ex_ref, rd_ref, st_ref, dy_ref,
             dact_ref, ddtp_ref, dalog_ref, dbias_ref, dskip_ref, dst, dxdt_ref, dac_ref):
        first = pl.program_id(0) == 0

        @pl.when(first)
        def _():
            dst[...] = jnp.zeros_like(dst)
            dalog_ref[...] = jnp.zeros_like(dalog_ref)
            dbias_ref[...] = jnp.zeros_like(dbias_ref)
            dskip_ref[...] = jnp.zeros_like(dskip_ref)

        expand, reduce = ex_ref[...], rd_ref[...]
        pre = _ssd_chunk_prep(dtp_ref[...], dtpt_ref[...], bw_ref[...], aw_ref[...], bc_ref[...], ac_ref[...], expand)
        acum_full = pre["acum_full"]
        last_full = acum_full[L - 1:L, :]
        ones = jnp.ones((2 * L, LANES), MXU_DTYPE)
        lane = lax.broadcasted_iota(jnp.int32, (L, LANES), 1)
        dacum_diag = jnp.zeros((L, LANES), F32)
        dlast_parts = []
        for g in range(G):
            bg = _mx(act_ref[:, SSD_D_INNER + g * N:SSD_D_INNER + (g + 1) * N])
            cg = _mx(act_ref[:, SSD_D_INNER + G * N + g * N:SSD_D_INNER + G * N + (g + 1) * N])
            cb = _dot(cg, bg, _NT)
            dcb = jnp.zeros((L, L), F32)
            dcg = jnp.zeros((L, N), F32)
            dbg = jnp.zeros((L, N), F32)
            for half in range(2):
                p = 2 * g + half
                cols = slice(p * LANES, (p + 1) * LANES)
                hcols = slice(half * LANES, (half + 1) * LANES)
                xs = act_ref[:, cols]
                xdt = xs * pre["dt_full"][:, cols]
                dyv = dy_ref[:, cols]
                dxdt = jnp.zeros((L, LANES), F32)
                parts = []
                for j in range(2):
                    h = 2 * p + j
                    dec = _decay(pre, h)
                    m = cb * dec
                    dyh = _mx(dyv * _head_mask(j))
                    dm = _dot(dyh, _mx(xdt), _NT)
                    dxdt = dxdt + _dot(_mx(m), dyh, _TN)
                    parts.append(_split_hi_lo(dm * m))
                    dcb = dcb + dm * dec
                (ahi, alo), (bhi, blo) = parts
                rows = _dot(jnp.concatenate([jnp.concatenate([ahi, alo], axis=1), jnp.concatenate([bhi, blo], axis=1)], axis=0), ones)
                cols_ = _dot(jnp.concatenate([jnp.concatenate([ahi, bhi], axis=1), jnp.concatenate([alo, blo], axis=1)], axis=0),
                             ones, _TN)
                d_pair = rows - cols_
                dacum_diag = jnp.where(lane == 2 * p, d_pair[:L], jnp.where(lane == 2 * p + 1, d_pair[L:], dacum_diag))
                lam = jnp.exp(acum_full[:, cols])
                stv = _mx(st_ref[0, g, :, hcols])
                z = _dot(cg, stv)
                dz = _mx(lam * dyv)
                dcg = dcg + _dot(dz, stv, _NT)
                dst_in = _dot(cg, dz, _TN)
                dsv = dst[g, :, hcols]
                w = jnp.exp(last_full[:, cols] - acum_full[:, cols])
                q = _dot(bg, _mx(dsv))
                wq = w * q
                dxdt = dxdt + wq
                wqx = wq * xdt
                dbg = dbg + _dot(_mx(xdt * w), _mx(dsv), _NT)
                elast = jnp.exp(last_full[:, cols])
                dlast_p = jnp.sum(wqx, axis=0, keepdims=True) + elast * jnp.sum(dsv * st_ref[0, g, :, hcols], axis=0, keepdims=True)
                dac_ref[:, cols] = dyv * z * lam - wqx
                dlast_parts.append(dlast_p)
                dst[g, :, hcols] = dst_in + dsv * elast
                dxdt_ref[:, cols] = dxdt
                dact_ref[:, cols] = dxdt * pre["dt_full"][:, cols] + dsk_ref[:, cols] * dyv
            dcbm = _mx(dcb)
            dact_ref[:, SSD_D_INNER + g * N:SSD_D_INNER + (g + 1) * N] = dbg + _dot(dcbm, cg, _TN)
            dact_ref[:, SSD_D_INNER + G * N + g * N:SSD_D_INNER + G * N + (g + 1) * N] = dcg + _dot(dcbm, bg)

        xs_all = act_ref[:, 0:SSD_D_INNER]
        dacum = dacum_diag + _dot_exact01(dac_ref[...], reduce)
        dlast = _dot_exact01(jnp.concatenate(dlast_parts, axis=1), reduce)
        row = lax.broadcasted_iota(jnp.int32, (L, LANES), 0)
        dacum = dacum + jnp.where(row == L - 1, dlast, 0.0)
        da_col = _dot01(dacum, pre["upper"], left=True)
        a_w = -jnp.exp(aw_ref[...])
        ddt = a_w * da_col + _dot_exact01(dxdt_ref[...] * xs_all, reduce)
        xin = dtp_ref[...] + bw_ref[...]
        ddtp = ddt * (1.0 / (1.0 + jnp.exp(-xin)))
        valid = lane < SSD_HEADS
        ddtp = jnp.where(valid, ddtp, 0.0)
        ddtp_ref[...] = ddtp
        dbias_ref[...] += jnp.sum(ddtp, axis=0, keepdims=True)
        dalog_ref[...] += jnp.sum(jnp.where(valid, da_col * pre["a_col"], 0.0), axis=0, keepdims=True)
        dskip_ref[...] += jnp.sum(_dot_exact01(dy_ref[...] * xs_all, reduce), axis=0, keepdims=True)

    sp = _ssd_specs(s, nc, True)
    acc = pl.BlockSpec((1, LANES), lambda i: (0, 0))
    outs, carried = hosted_call(
        body, hook, "ssd_scan_bwd", (nc,),
        [sp["xbc"], sp["dtp"], sp["dtp_t"]] + sp["consts"] + [sp["state"], sp["rows"]],
        [sp["xbc"], pl.BlockSpec((L, LANES), lambda i: (nc - 1 - i, 0)), acc, acc, acc],
        [_sds((s, SSD_CONV_DIM), F32), _sds((s, LANES), F32)] + [_sds((1, LANES), F32)] * 3,
        [pltpu.VMEM((G, N, GW), F32), pltpu.VMEM((L, SSD_D_INNER), F32), pltpu.VMEM((L, SSD_D_INNER), F32)],
        ("arbitrary",), (act, proj, dtp_t, *_const_args(cs), states, dy))
    return (*outs, carried)


def ssd_post_fwd(y, proj, g):
    s, d = y.shape
    ts = _pick(s, (256, 128))

    def body(y_ref, z_ref, g_ref, o_ref):
        y2 = y_ref[...] * _silu(z_ref[...])
        r = lax.rsqrt(jnp.mean(y2 * y2, axis=-1, keepdims=True) + NORM_EPS)
        o_ref[...] = (y2 * r * g_ref[...]).astype(o_ref.dtype)

    row = pl.BlockSpec((ts, d), lambda i: (i, 0))
    return pl.pallas_call(
        body, name="ssd_post_fwd", grid=(s // ts,), in_specs=[row, row, pl.BlockSpec((1, d), lambda i: (0, 0))],
        out_specs=row, out_shape=_sds((s, d), MXU_DTYPE), compiler_params=_params(("parallel",)),
    )(y, proj, g)


def ssd_post_bwd(y, proj, g, dy3):
    s, d = y.shape
    ts = _pick(s, (256, 128))

    def body(y_ref, z_ref, g_ref, d3_ref, dy_ref, dz_ref, dg_ref):
        yv, zv = y_ref[...], z_ref[...]
        sz, sgrad = _silu_and_grad(zv)
        y2 = yv * sz
        r = lax.rsqrt(jnp.mean(y2 * y2, axis=-1, keepdims=True) + NORM_EPS)
        xh = y2 * r
        d3 = d3_ref[...]
        dxh = d3 * g_ref[...]
        dy2 = r * (dxh - xh * jnp.mean(dxh * xh, axis=-1, keepdims=True))
        dy_ref[...] = dy2 * sz
        dz_ref[...] = (dy2 * yv * sgrad).astype(dz_ref.dtype)
        part = jnp.sum(d3 * xh, axis=0, keepdims=True)

        @pl.when(pl.program_id(0) == 0)
        def _():
            dg_ref[...] = part

        @pl.when(pl.program_id(0) != 0)
        def _():
            dg_ref[...] += part

    row = pl.BlockSpec((ts, d), lambda i: (i, 0))
    vec = pl.BlockSpec((1, d), lambda i: (0, 0))
    return pl.pallas_call(
        body, name="ssd_post_bwd", grid=(s // ts,), in_specs=[row, row, vec, row], out_specs=[row, row, vec],
        out_shape=[_sds((s, d), F32), _sds((s, d), MXU_DTYPE), _sds((1, d), F32)],
        compiler_params=_params(("arbitrary",)),
    )(y, proj, g, dy3)


def dt_transpose(proj):
    s = proj.shape[0]
    ts = _pick(s, (512, 256, 128))

    def body(p_ref, o_ref):
        o_ref[...] = p_ref[...].T

    return pl.pallas_call(
        body, name="dt_transpose", grid=(s // ts,),
        in_specs=[pl.BlockSpec((ts, LANES), lambda i: (i, SSD_IN_PAD // LANES - 1))],
        out_specs=pl.BlockSpec((LANES, ts), lambda i: (0, i)), out_shape=_sds((LANES, s), F32),
        compiler_params=_params(("parallel",)),
    )(proj)


def ssd_core_fwd(proj, cw, cb, dt_bias, a_log, d_skip, norm_g, hook=None):
    cs = _ssd_consts(dt_bias, a_log, d_skip)
    act = ssd_conv_fwd(proj, cw, cb)
    dtp_t = dt_transpose(proj)
    y, states, carried = ssd_scan_fwd(act, proj, dtp_t, cs, hook)
    y3 = ssd_post_fwd(y, proj, norm_g)
    return y3, (cs, act, dtp_t, y, states), carried


def ssd_core_bwd(proj, cw, cb, norm_g, saved, dy3, hook=None):
    cs, act, dtp_t, y, states = saved
    dy, dz, dnorm = ssd_post_bwd(y, proj, norm_g, dy3)
    dact, ddtp, dalog, dbias, dskip, carried = ssd_scan_bwd(act, proj, dtp_t, cs, states, dy, hook)
    dxbc, dcw, dcb = ssd_conv_bwd(proj, dact, cw, cb)
    dproj = jnp.concatenate([dz, dxbc, ddtp.astype(MXU_DTYPE)], axis=1)
    h = SSD_HEADS
    return dproj, dcw, dcb, dbias[0, :h], dalog[0, :h], dskip[0, :h], dnorm, carried


def ssd_core(proj, cw, cb, dt_bias, a_log, d_skip, norm_g, dy3):
    y3, saved, _ = ssd_core_fwd(proj, cw, cb, dt_bias, a_log, d_skip, norm_g)
    return y3, ssd_core_bwd(proj, cw, cb, norm_g, saved, dy3)


def loss_head(x, g, target):
    s, d = x.shape
    ts = _pick(s, (512, 256, 128))

    def body(x_ref, g_ref, t_ref, loss_ref, dx_ref, dxm_ref, dg_ref):
        xv = x_ref[...]
        r = lax.rsqrt(jnp.mean(xv * xv, axis=-1, keepdims=True) + NORM_EPS)
        xh = xv * r
        err = xh * g_ref[...] - t_ref[...]
        dy = err * (1.0 / d)
        dxh = dy * g_ref[...]
        dx = r * (dxh - xh * jnp.mean(dxh * xh, axis=-1, keepdims=True))
        dx_ref[...] = dx
        dxm_ref[...] = dx.astype(dxm_ref.dtype)
        part = jnp.sum(dy * xh, axis=0, keepdims=True)
        lpart = jnp.full((1, LANES), 0.5 * jnp.sum(jnp.mean(err * err, axis=-1, keepdims=True)), F32)

        @pl.when(pl.program_id(0) == 0)
        def _():
            dg_ref[...] = part
            loss_ref[...] = lpart

        @pl.when(pl.program_id(0) != 0)
        def _():
            dg_ref[...] += part
            loss_ref[...] += lpart

    row = pl.BlockSpec((ts, d), lambda i: (i, 0))
    vec = pl.BlockSpec((1, d), lambda i: (0, 0))
    return pl.pallas_call(
        body, name="loss_head", grid=(s // ts,), in_specs=[row, vec, row],
        out_specs=[pl.BlockSpec((1, LANES), lambda i: (0, 0)), row, row, vec],
        out_shape=[_sds((1, LANES), F32), _sds((s, d), F32), _sds((s, d), MXU_DTYPE), _sds((1, d), F32)],
        compiler_params=_params(("arbitrary",)),
    )(x, g, target)


def _adamw_math(w, g, m, v):
    m = ADAM_B1 * m + (1.0 - ADAM_B1) * g
    v = ADAM_B2 * v + (1.0 - ADAM_B2) * (g * g)
    m_hat = m / (1.0 - ADAM_B1 ** ADAM_STEP)
    v_hat = v / (1.0 - ADAM_B2 ** ADAM_STEP)
    return -ADAM_LR * (m_hat / (jnp.sqrt(v_hat) + ADAM_EPS) + ADAM_WD * w), m, v


def adamw(w, g, m, v, name="adamw"):
    r, c = w.shape
    tr = _pick(r, (256, 128, 64, 32, 16, 8))

    def body(w_ref, g_ref, m_ref, v_ref, d_ref, nm_ref, nv_ref):
        d_ref[...], nm_ref[...], nv_ref[...] = _adamw_math(w_ref[...], g_ref[...], m_ref[...], v_ref[...])

    blk = pl.BlockSpec((tr, c), lambda i: (i, 0))
    return pl.pallas_call(
        body, name=name, grid=(r // tr,), in_specs=[blk] * 4, out_specs=[blk] * 3,
        out_shape=[_sds((r, c), F32)] * 3, compiler_params=_params(("parallel",)),
    )(w, g, m, v)


def adamw_small(w, parts, m, v):
    n, r, c = parts.shape

    def body(w_ref, p_ref, m_ref, v_ref, g_ref, d_ref, nm_ref, nv_ref):
        g = p_ref[0]
        for k in range(1, n):
            g = g + p_ref[k]
        g_ref[...] = g
        d_ref[...], nm_ref[...], nv_ref[...] = _adamw_math(w_ref[...], g, m_ref[...], v_ref[...])

    return pl.pallas_call(
        body, name="adamw_small", out_shape=[_sds((r, c), F32)] * 4, compiler_params=_params(),
    )(w, parts, m, v)


def pair_sum(unit, recv, where):
    nchip, _, r, c = unit.shape
    tr = _pick(r, (512, 256, 176, 128, 64, 32, 16))

    def body(w_ref, a_ref, b_ref, o_ref, ob_ref):
        sm = a_ref[0, 0] + b_ref[0]
        ob_ref[0] = sm.astype(ob_ref.dtype)

        @pl.when(pl.program_id(1) == w_ref[0])
        def _():
            o_ref[...] = sm

    blk = pl.BlockSpec((1, tr, c), lambda i, s, w: (s, i, 0))
    return pl.pallas_call(
        body, name="pair_sum",
        grid_spec=pltpu.PrefetchScalarGridSpec(
            num_scalar_prefetch=1, grid=(r // tr, nchip),
            in_specs=[pl.BlockSpec((1, 1, tr, c), lambda i, s, w: (s, w[1], i, 0)), blk],
            out_specs=[pl.BlockSpec((tr, c), lambda i, s, w: (i, 0)), blk]),
        out_shape=[_sds((r, c), F32), _sds((nchip, r, c), jnp.bfloat16)],
        compiler_params=_params(("parallel", "arbitrary")),
    )(where, unit, recv)


def chip_sum(own, where, recv, layer, layers, prev=None):
    r, c = own.shape
    tr = _pick(r, (512, 256, 176, 128, 64, 32, 16))

    def body(s_ref, a_ref, b_ref, *rest):
        rest[-1][...] = a_ref[...] + b_ref[0].astype(F32) + b_ref[1].astype(F32) + b_ref[2].astype(F32)

    in_specs = [pl.BlockSpec((tr, c), lambda i, s: (i, 0)), pl.BlockSpec((3, tr, c), lambda i, s: (0, i, 0))]
    args = [where, own, recv]
    if prev is not None:
        in_specs.append(ANY)
        args.append(prev)
    return pl.pallas_call(
        body, name="chip_sum",
        grid_spec=pltpu.PrefetchScalarGridSpec(
            num_scalar_prefetch=1, grid=(r // tr,), in_specs=in_specs,
            out_specs=pl.BlockSpec((None, None, tr, c), lambda i, s: (layer, s[1], i, 0))),
        out_shape=_sds((layers, 2, r, c), F32), input_output_aliases={} if prev is None else {3: 0},
        compiler_params=_params(("parallel",)),
    )(*args)


def place_cast(w, layer, chip):
    _, a, b = w.shape
    ta = _pick(a, (512, 352, 256, 128))

    def body(c_ref, w_ref, o_ref):
        o_ref[...] = w_ref[...].astype(o_ref.dtype)

    return pl.pallas_call(
        body, name="place_cast",
        grid_spec=pltpu.PrefetchScalarGridSpec(
            num_scalar_prefetch=1, grid=(a // ta,),
            in_specs=[pl.BlockSpec((None, ta, b), lambda i, c: (layer, i, 0))],
            out_specs=pl.BlockSpec((None, ta, b), lambda i, c: (c[0], i, 0))),
        out_shape=_sds((N_CHIPS, a, b), MXU_DTYPE), compiler_params=_params(("parallel",)),
    )(chip, w)


ANY = pl.BlockSpec(memory_space=pl.ANY)
COMM = pltpu.CompilerParams(has_side_effects=True)


def _coords():
    return lax.axis_index("x"), lax.axis_index("y"), lax.axis_index("c")


def _other_chips(x, y):
    return [(1 - x, y), (x, 1 - y), (1 - x, 1 - y)]


def all_gather_8(halves, name):
    _, r, c = halves.shape

    def body(h_ref, out_ref, send_sems, recv_sems, local_sem):
        x, y, cc = _coords()
        _gather_one(h_ref.at[cc], lambda px, py, pc: out_ref.at[4 * px + 2 * py + pc],
                    lambda k: send_sems.at[k], lambda k: recv_sems.at[k], local_sem)

    return pl.pallas_call(
        body, name=name, in_specs=[ANY], out_specs=ANY, out_shape=_sds((8, r, c), halves.dtype),
        scratch_shapes=[pltpu.SemaphoreType.DMA((7,)), pltpu.SemaphoreType.DMA((7,)), pltpu.SemaphoreType.DMA],
        compiler_params=COMM,
    )(halves)


def _gather_plan(x_ref, slot, send_sem, recv_sem, local_sem):
    x, y, cc = _coords()
    me, sibling = (x, y, cc), (x, y, 1 - cc)
    chips = _other_chips(x, y)

    def copy(k, blk, to, src=None):
        return pltpu.make_async_remote_copy(
            src_ref=slot(*blk) if src is None else src, dst_ref=slot(*blk),
            send_sem=send_sem(k), recv_sem=recv_sem(k), device_id=to, device_id_type=MESH)

    mine = pltpu.make_async_copy(x_ref, slot(*me), local_sem)
    first = [copy(0, me, sibling, src=x_ref)] + [copy(1 + j, me, (*chip, cc), src=x_ref) for j, chip in enumerate(chips)]
    passed = [copy(4 + j, (*chip, cc), sibling) for j, chip in enumerate(chips)]
    over_ici = [copy(1 + j, (*chip, cc), me) for j, chip in enumerate(chips)]
    from_sibling = [copy(0, sibling, me)] + [copy(4 + j, (*chip, 1 - cc), me) for j, chip in enumerate(chips)]
    return mine, first, passed, over_ici, from_sibling


def _gather_run(plans):
    for mine, first, _, _, _ in plans:
        mine.start()
        for cp in first:
            cp.start()
    for j in range(3):
        for _, _, passed, over_ici, _ in plans:
            over_ici[j].wait_recv()
            passed[j].start()
    for mine, first, passed, _, from_sibling in plans:
        for cp in from_sibling:
            cp.wait_recv()
        for cp in first + passed:
            cp.wait_send()
        mine.wait()


def _gather_one(x_ref, slot, send_sem, recv_sem, local_sem):
    _gather_run([_gather_plan(x_ref, slot, send_sem, recv_sem, local_sem)])


def gather_hook(items):
    n = len(items)

    def plan(refs, send_sems, recv_sems):
        x, y, cc = _coords()

        def copy(i, k, px, py, pc, to):
            blk = refs[i].at[2 * px + py, pc]
            return pltpu.make_async_remote_copy(src_ref=blk, dst_ref=blk, send_sem=send_sems.at[i, k],
                                                recv_sem=recv_sems.at[i, k], device_id=to, device_id_type=MESH)

        chips = _other_chips(x, y)
        first = [copy(i, j, x, y, cc, (*chip, cc)) for i in range(n) for j, chip in enumerate(chips)]
        return copy, chips, first, (x, y, cc)

    def start(refs, new, sems):
        for cp in plan(refs, *sems)[2]:
            cp.start()

    def finish(refs, new, sems):
        copy, chips, first, (x, y, cc) = plan(refs, *sems)
        passed = []
        for j, chip in enumerate(chips):
            for i in range(n):
                copy(i, j, *chip, cc, (x, y, cc)).wait_recv()
                passed.append(copy(i, 3 + j, *chip, cc, (x, y, 1 - cc)))
                passed[-1].start()
        for j, chip in enumerate(chips):
            for i in range(n):
                copy(i, 3 + j, *chip, 1 - cc, (x, y, cc)).wait_recv()
        for cp in first + passed:
            cp.wait_send()

    return dict(arrays=list(items), new=[], start=start, finish=finish, in_place=True,
                sems=[pltpu.SemaphoreType.DMA((n, 6)), pltpu.SemaphoreType.DMA((n, 6))])


def hosted_call(body, hook, name, grid, in_specs, out_specs, out_shape, scratch_shapes, sem, args):
    single = not isinstance(out_shape, (list, tuple))
    out_specs_l = [out_specs] if single else list(out_specs)
    out_shape_l = [out_shape] if single else list(out_shape)
    if hook is None:
        res = pl.pallas_call(body, name=name, grid=grid, in_specs=list(in_specs), out_specs=out_specs, out_shape=out_shape,
                             scratch_shapes=list(scratch_shapes), compiler_params=_params(sem))(*args)
        return res, []
    items, new = hook["arrays"], hook["new"]
    k, kn, n_in, n_out, n_scr = len(items), len(new), len(in_specs), len(out_specs_l), len(scratch_shapes)
    ka = k if hook["in_place"] else 0

    def full(*refs):
        ins = refs[:n_in]
        base = n_in + k
        outs = refs[base:base + n_out]
        hrefs = refs[base + n_out:base + n_out + ka] if ka else refs[n_in:base]
        nrefs = refs[base + n_out + ka:base + n_out + ka + kn]
        scr = refs[base + n_out + ka + kn:base + n_out + ka + kn + n_scr]
        sems = refs[base + n_out + ka + kn + n_scr:]
        ids = [pl.program_id(d) for d in range(len(grid))]
        first = functools.reduce(jnp.logical_and, [i == 0 for i in ids])
        last = functools.reduce(jnp.logical_and, [i == g - 1 for i, g in zip(ids, grid)])

        @pl.when(first)
        def _():
            hook["start"](hrefs, nrefs, sems)

        body(*ins, *outs, *scr)

        @pl.when(last)
        def _():
            hook["finish"](hrefs, nrefs, sems)

    res = pl.pallas_call(
        full, name=name, grid=grid, in_specs=list(in_specs) + [ANY] * k, out_specs=out_specs_l + [ANY] * (ka + kn),
        out_shape=out_shape_l + [_sds(a.shape, a.dtype) for a in items[:ka]] + list(new),
        input_output_aliases={n_in + i: n_out + i for i in range(ka)},
        scratch_shapes=list(scratch_shapes) + hook["sems"],
        compiler_params=pltpu.CompilerParams(dimension_semantics=("arbitrary",) * len(grid),
                                             vmem_limit_bytes=VMEM_LIMIT, has_side_effects=True),
    )(*args, *items)
    return (res[0] if single else list(res[:n_out])), list(res[n_out:])


def comm_call(hook, name):
    k, kn = len(hook["arrays"]), len(hook["new"])
    ka = k if hook["in_place"] else 0

    def body(*refs):
        hrefs = refs[k:k + ka] if ka else refs[:k]
        hook["start"](hrefs, refs[k + ka:k + ka + kn], refs[k + ka + kn:])
        hook["finish"](hrefs, refs[k + ka:k + ka + kn], refs[k + ka + kn:])

    return list(pl.pallas_call(
        body, name=name, in_specs=[ANY] * k, out_specs=[ANY] * (ka + kn),
        out_shape=[_sds(a.shape, a.dtype) for a in hook["arrays"][:ka]] + list(hook["new"]),
        input_output_aliases={i: i for i in range(ka)}, scratch_shapes=hook["sems"], compiler_params=COMM,
    )(*hook["arrays"]))


def merge_hooks(hooks):
    hooks = [h for h in hooks if h is not None]
    if len(hooks) < 2:
        return hooks[0] if hooks else None

    def parts(refs, new, sems):
        out, a, b, c = [], 0, 0, 0
        for h in hooks:
            na, nn, ns = len(h["arrays"]), len(h["new"]), len(h["sems"])
            out.append((refs[a:a + na], new[b:b + nn], sems[c:c + ns]))
            a, b, c = a + na, b + nn, c + ns
        return out

    def start(refs, new, sems):
        for h, p in zip(hooks, parts(refs, new, sems)):
            h["start"](*p)

    def finish(refs, new, sems):
        for h, p in zip(hooks, parts(refs, new, sems)):
            h["finish"](*p)

    assert len({h["in_place"] for h in hooks}) == 1
    return dict(arrays=[a for h in hooks for a in h["arrays"]], new=[a for h in hooks for a in h["new"]],
                sems=[a for h in hooks for a in h["sems"]], start=start, finish=finish, in_place=hooks[0]["in_place"])


def split_carried(hooks, carried):
    hooks = [h for h in hooks if h is not None]
    off = sum(len(h["arrays"]) for h in hooks if h["in_place"])
    out = []
    for h in hooks:
        out.append(carried[off:off + len(h["new"])])
        off += len(h["new"])
    return out


def pair_swap_hook(units):
    n = len(units)

    def plan(refs, new, send_sems, recv_sems):
        x, y, cc = _coords()
        return [pltpu.make_async_remote_copy(src_ref=refs[i].at[:, 1 - cc], dst_ref=new[i], send_sem=send_sems.at[i],
                                             recv_sem=recv_sems.at[i], device_id=(x, y, 1 - cc), device_id_type=MESH)
                for i in range(n)]

    def start(refs, new, sems):
        for cp in plan(refs, new, *sems):
            cp.start()

    def finish(refs, new, sems):
        for cp in plan(refs, new, *sems):
            cp.wait()

    return dict(arrays=list(units), new=[_sds((u.shape[0],) + u.shape[2:], u.dtype) for u in units], start=start,
                finish=finish, in_place=False, sems=[pltpu.SemaphoreType.DMA((n,)), pltpu.SemaphoreType.DMA((n,))])


def chip_exchange_hook(units):
    n = len(units)

    def plan(refs, new, send_sems, recv_sems):
        x, y, cc = _coords()
        return [pltpu.make_async_remote_copy(
            src_ref=refs[i].at[2 * px + py], dst_ref=new[i].at[k], send_sem=send_sems.at[i, k],
            recv_sem=recv_sems.at[i, k], device_id=(px, py, cc), device_id_type=MESH)
            for i in range(n) for k, (px, py) in enumerate(_other_chips(x, y))]

    def start(refs, new, sems):
        for cp in plan(refs, new, *sems):
            cp.start()

    def finish(refs, new, sems):
        for cp in plan(refs, new, *sems):
            cp.wait()

    return dict(arrays=list(units), new=[_sds((3,) + u.shape[1:], u.dtype) for u in units], start=start,
                finish=finish, in_place=False, sems=[pltpu.SemaphoreType.DMA((n, 3)), pltpu.SemaphoreType.DMA((n, 3))])


def grad_half_swap(grads):
    n = len(grads)

    def body(*refs):
        outs, send_sems, recv_sems = refs[n:2 * n], refs[2 * n], refs[2 * n + 1]
        x, y, cc = _coords()
        cps = [pltpu.make_async_remote_copy(
            src_ref=outs[i].at[:, cc], dst_ref=outs[i].at[:, cc], send_sem=send_sems.at[i], recv_sem=recv_sems.at[i],
            device_id=(x, y, 1 - cc), device_id_type=MESH) for i in range(n)]
        for cp in cps:
            cp.start()
        for i, cp in enumerate(cps):
            cp.wait_send()
            pltpu.make_async_remote_copy(
                src_ref=outs[i].at[:, 1 - cc], dst_ref=outs[i].at[:, 1 - cc], send_sem=send_sems.at[i],
                recv_sem=recv_sems.at[i], device_id=(x, y, 1 - cc), device_id_type=MESH).wait_recv()

    return pl.pallas_call(
        body, name="grad_half_swap", in_specs=[ANY] * n, out_specs=[ANY] * n,
        out_shape=[_sds(g.shape, g.dtype) for g in grads], input_output_aliases={i: i for i in range(n)},
        scratch_shapes=[pltpu.SemaphoreType.DMA((n,)), pltpu.SemaphoreType.DMA((n,))], compiler_params=COMM,
    )(*grads)


N_CHIPS = 4
PACK_COLS = 1024
BIG = ("ssd_w_in", "ssd_w_out", "sb_w_qkv", "sb_w_out", "ffn_w_in", "ffn_w_out")
CONVW = ("ssd_conv_w", "ffn_conv_w")
COL_SHARDED = ("ssd_w_in", "sb_w_qkv", "ffn_w_in", "ssd_conv_w", "ffn_conv_w")
SMALL = ("mix_norm", "ffn_norm", "final_norm", "ssd_conv_b", "ssd_dt_bias", "ssd_a_log", "ssd_d", "ssd_norm", "ffn_conv_b")
WEIGHTS = ("mix_norm", "ffn_norm", "final_norm", "ssd_w_in", "ssd_conv_w", "ssd_conv_b", "ssd_dt_bias", "ssd_a_log",
           "ssd_d", "ssd_norm", "ssd_w_out", "sb_w_qkv", "sb_w_out", "ffn_w_in", "ffn_conv_w", "ffn_conv_b", "ffn_w_out")


def _to_rows(flat, multiple):
    rows = -(-flat.shape[-1] // PACK_COLS)
    rows = -(-rows // multiple) * multiple
    pad = rows * PACK_COLS - flat.shape[-1]
    return jnp.pad(flat, [(0, pad)]).reshape(rows, PACK_COLS)


def _unshard(name, stacked):
    l, n, a, b = stacked.shape
    if name in COL_SHARDED:
        return jnp.transpose(stacked, (0, 2, 1, 3)).reshape(l, a, n * b)
    return stacked.reshape(l, n * a, b)


def _gather_conv_weights(w):
    flat = jnp.concatenate([w[n].reshape(-1) for n in CONVW])
    rows = _to_rows(flat, 16)
    got = all_gather_8(rows.reshape(2, rows.shape[0] // 2, PACK_COLS), "gather_conv_weights").reshape(N_CHIPS, -1)
    out, off = {}, 0
    for n in CONVW:
        l, a, b = w[n].shape
        out[n] = _unshard(n, jnp.moveaxis(got[:, off:off + w[n].size].reshape(N_CHIPS, l, a, b), 0, 1))
        off += w[n].size
    return out


def _finish_big_grads(pair_sums, from_chips, layout):
    cc = lax.axis_index("c").astype(jnp.int32)
    chip = (2 * lax.axis_index("x") + lax.axis_index("y")).astype(jnp.int32)
    where = jnp.stack([chip, cc])
    nlayers = [1 + max(l for k, l in layout if k == wi) for wi in range(1 + max(k for k, _ in layout))]
    grads = [None] * len(nlayers)
    for (wi, l), p, r in zip(layout, pair_sums, from_chips):
        grads[wi] = chip_sum(p, where, r, l, nlayers[wi], grads[wi])
    return grad_half_swap(grads)


def kernel(x, mix_norm, ffn_norm, final_norm, ssd_w_in, ssd_conv_w, ssd_conv_b, ssd_dt_bias, ssd_a_log, ssd_d, ssd_norm, ssd_w_out, sb_w_qkv, sb_w_out, ffn_w_in, ffn_conv_w, ffn_conv_b, ffn_w_out, loss_target, m_mix_norm, m_ffn_norm, m_final_norm, m_ssd_w_in, m_ssd_conv_w, m_ssd_conv_b, m_ssd_dt_bias, m_ssd_a_log, m_ssd_d, m_ssd_norm, m_ssd_w_out, m_sb_w_qkv, m_sb_w_out, m_ffn_w_in, m_ffn_conv_w, m_ffn_conv_b, m_ffn_w_out, v_mix_norm, v_ffn_norm, v_final_norm, v_ssd_w_in, v_ssd_conv_w, v_ssd_conv_b, v_ssd_dt_bias, v_ssd_a_log, v_ssd_d, v_ssd_norm, v_ssd_w_out, v_sb_w_qkv, v_sb_w_out, v_ffn_w_in, v_ffn_conv_w, v_ffn_conv_b, v_ffn_w_out):
    given = dict(locals())
    w = {n: given[n] for n in WEIGHTS}
    mom = {n: given["m_" + n] for n in WEIGHTS}
    var = {n: given["v_" + n] for n in WEIGHTS}
    chip = 2 * lax.axis_index("x") + lax.axis_index("y")

    chip1 = chip.reshape(1).astype(jnp.int32)
    fw = _gather_conv_weights(w)
    row = lambda v: v.reshape(1, -1)

    def placed(n, l):
        _, a, b = w[n].shape
        return place_cast(w[n], l, chip1).reshape(N_CHIPS, 2, a // 2, b)

    def mixer_items(i):
        return [(n, i // 2) for n in (("ssd_w_in", "ssd_w_out") if i % 2 == 0 else ("sb_w_qkv", "sb_w_out"))]

    def ffn_items(i):
        return [("ffn_w_in", i), ("ffn_w_out", i)]

    def hook_for(items):
        return gather_hook([placed(n, l) for n, l in items]) if items else None

    lw = {}

    def arrived(items, arrays):
        for (n, l), arr in zip(items, arrays):
            g4 = arr.reshape(N_CHIPS, -1, arr.shape[-1])
            if n in COL_SHARDED:
                full = jnp.transpose(g4, (1, 0, 2)).reshape(g4.shape[1], -1)
            else:
                full = g4.reshape(-1, g4.shape[2])
            if n == "ssd_w_in":
                full = jnp.pad(full, ((0, 0), (0, SSD_IN_PAD - SSD_IN_DIM)))
            lw[(n, l)] = full

    first_items = [("ssd_w_in", 0)]
    carry = {
        (0, "mm_in"): [("ssd_w_out", 0), ("ffn_w_in", 0)],
        (0, "scan"): [("ffn_w_out", 0)] + mixer_items(1),
        (0, "ffn_in"): ffn_items(1),
        (1, "sb"): mixer_items(2) + ffn_items(2),
        (2, "mm_in"): mixer_items(3),
        (2, "scan"): ffn_items(3),
    }
    arrived(first_items, comm_call(hook_for(first_items), "gather_first"))

    def carrying(i, slot, call):
        items = carry.get((i, slot), [])
        if not items:
            return call(None)
        out, got = call(hook_for(items))
        arrived(items, got)
        return out

    xcur = x[0]
    saved = []
    for i in range(DEPTH):
        j = i // 2
        h, r = rms_fwd(xcur, row(mix_norm[i]))
        if i % 2 == 0:
            proj = carrying(i, "mm_in", lambda hk: mm(h, lw[("ssd_w_in", j)], tm=2048, tn=896, name="mm_ssd_in", hook=hk))
            items = carry.get((i, "scan"), [])
            y3, core, got = ssd_core_fwd(proj, fw["ssd_conv_w"][j], row(ssd_conv_b[j]), ssd_dt_bias[j], ssd_a_log[j],
                                         ssd_d[j], row(ssd_norm[j]), hook_for(items))
            arrived(items, got)
            x1 = mm(y3, lw[("ssd_w_out", j)], res=xcur, name="mm_ssd_out")
            mix = (proj, y3, core)
        else:
            qkv = mm(h, lw[("sb_w_qkv", j)], out_dtype=MXU_DTYPE, tm=2048, name="mm_sb_qkv")
            o = carrying(i, "sb", lambda hk: sb_fwd(qkv, hk))
            x1 = mm(o, lw[("sb_w_out", j)], res=xcur, name="mm_sb_out")
            mix = (qkv, o)
        h2, r2 = rms_fwd(x1, row(ffn_norm[i]))
        u0 = carrying(i, "ffn_in", lambda hk: mm(h2, lw[("ffn_w_in", i)], tm=2048, name="mm_ffn_in", hook=hk))
        a = ffn_mid_fwd(u0, fw["ffn_conv_w"][i], row(ffn_conv_b[i]))
        x2 = mm(a, lw[("ffn_w_out", i)], res=x1, name="mm_ffn_out")
        saved.append((xcur, h, r, mix, x1, h2, r2, u0, a))
        xcur = x2
    loss_part, dx, dxm, d_final = loss_head(xcur, row(final_norm), loss_target[0])

    gl = {n: [None] * w[n].shape[0] for n in WEIGHTS if n != "final_norm"}
    units = {n: [None] * w[n].shape[0] for n in BIG}

    def unit_of(g4):
        return g4.reshape(N_CHIPS, 2, g4.shape[1] // 2, g4.shape[2])

    where = jnp.stack([chip, lax.axis_index("c")]).astype(jnp.int32)
    pair_f32, wire, from_chips = {}, {}, {}

    def pair_sums(keys, swapped):
        for (n, l), got in zip(keys, swapped):
            pair_f32[(n, l)], wire[(n, l)] = pair_sum(units[n][l], got, where)

    for i in reversed(range(DEPTH)):
        j = i // 2
        x0, h, r, mix, x1, h2, r2, u0, a = saved[i]
        units["ffn_w_out"][i] = unit_of(mm(a, dxm, "tn", tm=1408, name="mm_d_ffn_out").reshape(N_CHIPS, -1, D_MODEL))
        da = mm(dxm, lw[("ffn_w_out", i)], "nt", tn=1408, name="mm_da_ffn")
        du0, gl["ffn_conv_w"][i], dcb = ffn_mid_bwd(u0, da, fw["ffn_conv_w"][i], row(ffn_conv_b[i]))
        gl["ffn_conv_b"][i] = dcb[0]
        units["ffn_w_in"][i] = unit_of(mm(h2, du0, "tn", tn=1408, tm=512, n_split=N_CHIPS, name="mm_d_ffn_in"))
        keys_f = ffn_items(i)
        swap = pair_swap_hook([units[n][l] for n, l in keys_f])
        dh2, carried = mm(du0, lw[("ffn_w_in", i)], "nt", name="mm_dh_ffn", hook=swap)
        pair_sums(keys_f, split_carried([swap], carried)[0])
        dx1, dx1m, dg = rms_bwd(x1, r2, row(ffn_norm[i]), dh2, dx)
        gl["ffn_norm"][i] = dg[0]
        keys_up = mixer_items(i + 1) if i + 1 < DEPTH else []
        exchanges = [chip_exchange_hook([wire[k] for k in keys_f]),
                     chip_exchange_hook([wire[k] for k in keys_up]) if keys_up else None]
        if i % 2 == 0:
            proj, y3, core = mix
            units["ssd_w_out"][j] = unit_of(mm(y3, dx1m, "tn", name="mm_d_ssd_out").reshape(N_CHIPS, -1, D_MODEL))
            dy3 = mm(dx1m, lw[("ssd_w_out", j)], "nt", name="mm_dy3_ssd")
            (dproj, gl["ssd_conv_w"][j], dcb, gl["ssd_dt_bias"][j], gl["ssd_a_log"][j], gl["ssd_d"][j], dnorm,
             carried) = ssd_core_bwd(proj, fw["ssd_conv_w"][j], row(ssd_conv_b[j]), row(ssd_norm[j]), core, dy3,
                                     merge_hooks(exchanges))
            gl["ssd_conv_b"][j] = dcb[0]
            gl["ssd_norm"][j] = dnorm[0]
            dw_in = mm(h, dproj, "tn", tn=896, name="mm_d_ssd_in")[:, :SSD_IN_DIM]
            units["ssd_w_in"][j] = unit_of(jnp.transpose(dw_in.reshape(D_MODEL, N_CHIPS, -1), (1, 0, 2)))
            dmix, w_in, dh_name = dproj, lw[("ssd_w_in", j)], "mm_dh_ssd"
        else:
            qkv, o = mix
            units["sb_w_out"][j] = unit_of(mm(o, dx1m, "tn", name="mm_d_sb_out").reshape(N_CHIPS, -1, D_MODEL))
            do = mm(dx1m, lw[("sb_w_out", j)], "nt", out_dtype=MXU_DTYPE, name="mm_do_sb")
            dqkv, carried = sb_bwd(qkv, do, merge_hooks(exchanges))
            units["sb_w_qkv"][j] = unit_of(mm(h, dqkv, "tn", tn=768, n_split=N_CHIPS, name="mm_d_sb_qkv"))
            dmix, w_in, dh_name = dqkv, lw[("sb_w_qkv", j)], "mm_dh_sb"
        got = split_carried(exchanges, carried)
        from_chips.update(zip(keys_f, got[0]))
        if keys_up:
            from_chips.update(zip(keys_up, got[1]))
        keys_m = mixer_items(i)
        swap = pair_swap_hook([units[n][l] for n, l in keys_m])
        dh, carried = mm(dmix, w_in, "nt", name=dh_name, hook=swap)
        pair_sums(keys_m, split_carried([swap], carried)[0])
        dx, dxm, dg = rms_bwd(x0, r, row(mix_norm[i]), dh, dx1)
        gl["mix_norm"][i] = dg[0]
    last = chip_exchange_hook([wire[k] for k in mixer_items(0)])
    from_chips.update(zip(mixer_items(0), comm_call(last, "grad_exchange_last")))

    layout = [(k, l) for k, n in enumerate(BIG) for l in range(w[n].shape[0])]
    reduced = _finish_big_grads([pair_f32[(BIG[k], l)] for k, l in layout], [from_chips[(BIG[k], l)] for k, l in layout],
                                layout)
    g, delta, new_m, new_v = {}, {}, {}, {}
    two_d = lambda t: t.reshape(-1, t.shape[-1])
    for n, red in zip(BIG, reduced):
        g[n] = red.reshape(w[n].shape)
        d2, m2, v2 = adamw(two_d(w[n]), two_d(g[n]), two_d(mom[n]), two_d(var[n]), name="adamw_" + n)
        delta[n], new_m[n], new_v[n] = d2.reshape(w[n].shape), m2.reshape(w[n].shape), v2.reshape(w[n].shape)

    small_g = {n: jnp.stack(gl[n]) for n in SMALL + CONVW if n != "final_norm"}
    small_g["final_norm"] = d_final[0]
    zeros_of = lambda n: jnp.zeros((small_g[n].size,), F32)

    def small_pack(d, extra):
        parts = [d[n].reshape(-1) for n in SMALL] + [extra]
        parts += [(d[n].reshape(-1) if d is small_g else zeros_of(n)) for n in CONVW]
        return _to_rows(jnp.concatenate(parts), 16)

    part = small_pack(small_g, loss_part[0, 0:1])
    parts = all_gather_8(jnp.stack([part, part]), "gather_small_grads")
    zero = jnp.zeros((1,), F32)
    gs, ds, ms, vs = adamw_small(small_pack(w, zero), parts, small_pack(mom, zero), small_pack(var, zero))
    gs_flat = gs.reshape(-1)
    off = 0
    for n in SMALL:
        size = w[n].size
        for dst, src in ((g, gs), (delta, ds), (new_m, ms), (new_v, vs)):
            dst[n] = src.reshape(-1)[off:off + size].reshape(w[n].shape)
        off += size
    loss = gs_flat[off]
    off += 1
    for n in CONVW:
        size = small_g[n].size
        b = w[n].shape[-1]
        g[n] = lax.dynamic_slice_in_dim(gs_flat[off:off + size].reshape(small_g[n].shape), chip * b, b, axis=2)
        d2, m2, v2 = adamw(two_d(w[n]), two_d(g[n]), two_d(mom[n]), two_d(var[n]), name="adamw_" + n)
        delta[n], new_m[n], new_v[n] = d2.reshape(w[n].shape), m2.reshape(w[n].shape), v2.reshape(w[n].shape)
        off += size

    return (loss, dx[None], *[g[n] for n in WEIGHTS], *[delta[n] for n in WEIGHTS],
            *[new_m[n] for n in WEIGHTS], *[new_v[n] for n in WEIGHTS])
```

```python
import functools

import jax
import jax.numpy as jnp
from jax import lax
from jax.experimental import pallas as pl
from jax.experimental.pallas import tpu as pltpu

F32 = jnp.float32
MXU_DTYPE = jnp.bfloat16
HIGHEST = lax.Precision.HIGHEST

D_MODEL = 1024
DEPTH = 4
NORM_EPS = 1e-6
SSD_D_INNER = 2048
SSD_HEAD_DIM = 64
SSD_HEADS = 32
SSD_GROUPS = 8
SSD_STATE = 128
SSD_CONV = 4
SSD_CHUNK = 128
SSD_CONV_DIM = 4096
SSD_IN_DIM = 6176
SSD_IN_PAD = 6272
SB_HEADS = 16
SB_HEAD_DIM = 64
FFN_D_FF = 2816
FFN_CONV = 3
ADAM_LR, ADAM_B1, ADAM_B2, ADAM_EPS, ADAM_WD, ADAM_STEP = 0.001, 0.9, 0.999, 1e-08, 0.01, 10

LANES = 128
SUBLANES = 8
VMEM_LIMIT = 56 * 1024 * 1024
MESH = pl.DeviceIdType.MESH


def _params(sem=None):
    return pltpu.CompilerParams(dimension_semantics=sem, vmem_limit_bytes=VMEM_LIMIT)


def _sds(shape, dtype):
    return jax.ShapeDtypeStruct(shape, dtype)


def _dot(a, b, dims=(((1,), (0,)), ((), ())), precision=None):
    return lax.dot_general(a, b, dims, precision=precision, preferred_element_type=F32)


_NN = (((1,), (0,)), ((), ()))
_NT = (((1,), (1,)), ((), ()))
_TN = (((0,), (0,)), ((), ()))


def _mx(a):
    return a.astype(MXU_DTYPE)


def _sigmoid(x):
    return 0.5 * jnp.tanh(0.5 * x) + 0.5


def _silu(x):
    return x * _sigmoid(x)


def _silu_and_grad(x):
    s = _sigmoid(x)
    return x * s, s * (1.0 + x * (1.0 - s))


def _pick(n, cands):
    for c in cands:
        if n % c == 0:
            return c
    return n


def rms_fwd(x, g):
    s, d = x.shape
    ts = _pick(s, (512, 256, 128))

    def body(x_ref, g_ref, h_ref, r_ref):
        xv = x_ref[...]
        r = lax.rsqrt(jnp.mean(xv * xv, axis=-1, keepdims=True) + NORM_EPS)
        h_ref[...] = (xv * r * g_ref[...]).astype(h_ref.dtype)
        r_ref[...] = r

    return pl.pallas_call(
        body, name="rms_fwd", grid=(s // ts,),
        in_specs=[pl.BlockSpec((ts, d), lambda i: (i, 0)), pl.BlockSpec((1, d), lambda i: (0, 0))],
        out_specs=[pl.BlockSpec((ts, d), lambda i: (i, 0)), pl.BlockSpec((ts, 1), lambda i: (i, 0))],
        out_shape=[_sds((s, d), MXU_DTYPE), _sds((s, 1), F32)],
        compiler_params=_params(("parallel",)),
    )(x, g)


def rms_bwd(x, r, g, dh, dres):
    s, d = x.shape
    ts = _pick(s, (512, 256, 128))

    def body(x_ref, r_ref, g_ref, dh_ref, dres_ref, dx_ref, dxm_ref, dg_ref):
        xh = x_ref[...] * r_ref[...]
        dhv = dh_ref[...]
        dxh = dhv * g_ref[...]
        dx = dres_ref[...] + r_ref[...] * (dxh - xh * jnp.mean(dxh * xh, axis=-1, keepdims=True))
        dx_ref[...] = dx
        dxm_ref[...] = dx.astype(dxm_ref.dtype)
        part = jnp.sum(dhv * xh, axis=0, keepdims=True)

        @pl.when(pl.program_id(0) == 0)
        def _():
            dg_ref[...] = part

        @pl.when(pl.program_id(0) != 0)
        def _():
            dg_ref[...] += part

    row = pl.BlockSpec((ts, d), lambda i: (i, 0))
    return pl.pallas_call(
        body, name="rms_bwd", grid=(s // ts,),
        in_specs=[row, pl.BlockSpec((ts, 1), lambda i: (i, 0)), pl.BlockSpec((1, d), lambda i: (0, 0)), row, row],
        out_specs=[row, row, pl.BlockSpec((1, d), lambda i: (0, 0))],
        out_shape=[_sds((s, d), F32), _sds((s, d), MXU_DTYPE), _sds((1, d), F32)],
        compiler_params=_params(("arbitrary",)),
    )(x, r, g, dh, dres)


def mm(a, b, mode="nn", res=None, out_dtype=F32, tm=None, tn=None, n_split=1, name="mm", hook=None):
    halves_a = mode == "nt" and a.ndim == 3
    halves_b = mode == "tn" and b.ndim == 3
    if halves_a:
        a_shape = (a.shape[1], 2 * a.shape[2])
    else:
        a_shape = a.shape
    b_shape = (b.shape[1], 2 * b.shape[2]) if halves_b else b.shape
    if mode == "nn":
        (m, k), (_, n) = a_shape, b_shape
    elif mode == "nt":
        (m, k), (n, _) = a_shape, b_shape
    else:
        (k, m), (_, n) = a_shape, b_shape
    tm = min(tm, m) if tm else _pick(m, (1024, 512, 256, 128))
    tn = min(tn, n) if tn else _pick(n, (512, 896, 256, 128))
    assert m % tm == 0 and n % tn == 0, (m, tm, n, tn)
    dims = {"nn": _NN, "nt": _NT, "tn": _TN}[mode]

    def body(*refs):
        a_ref, b_ref = refs[0], refs[1]
        o_ref = refs[-1]
        if halves_a:
            kh = k // 2
            acc = _dot(_mx(a_ref[0]), _mx(b_ref[:, :kh]), dims) + _dot(_mx(a_ref[1]), _mx(b_ref[:, kh:]), dims)
        else:
            acc = _dot(_mx(a_ref[...]), _mx(b_ref[...]), dims)
        if res is not None:
            acc = acc + refs[2][...]
        o_ref[...] = acc.astype(o_ref.dtype)

    a_spec = pl.BlockSpec((k, tm), lambda i, j: (0, i)) if mode == "tn" else pl.BlockSpec((tm, k), lambda i, j: (i, 0))
    b_spec = pl.BlockSpec((tn, k), lambda i, j: (j, 0)) if mode == "nt" else pl.BlockSpec((k, tn), lambda i, j: (0, j))
    if halves_a:
        a_spec = pl.BlockSpec((2, tm, k // 2), lambda i, j: (0, i, 0))
    if halves_b:
        per_half = n // 2 // tn
        assert per_half * tn * 2 == n
        b_spec = pl.BlockSpec((None, k, tn), lambda i, j: (j // per_half, 0, j % per_half))
    o_spec = pl.BlockSpec((tm, tn), lambda i, j: (i, j))
    ins, specs = [a, b], [a_spec, b_spec]
    if res is not None:
        ins.append(res)
        specs.append(o_spec)
    out_shape = _sds((m, n), out_dtype)
    if n_split > 1:
        per = n // n_split // tn
        o_spec = pl.BlockSpec((None, tm, tn), lambda i, j: (j // per, i, j % per))
        out_shape = _sds((n_split, m, n // n_split), out_dtype)
    out, carried = hosted_call(body, hook, name, (m // tm, n // tn), specs, o_spec, out_shape, [],
                               ("parallel", "parallel"), ins)
    return out if hook is None else (out, carried)


CONV_ROWS = 256
CONV_COLS = 128


def _row_iota8(cols):
    return lax.broadcasted_iota(jnp.int32, (SUBLANES, cols), 0)


def _shift_down(cur, prev8, k):
    if k == 0:
        return cur
    rolled = pltpu.roll(cur, k, 0)
    head = jnp.where(_row_iota8(cur.shape[1]) < k, pltpu.roll(prev8, k, 0), rolled[0:SUBLANES])
    return jnp.concatenate([head, rolled[SUBLANES:]], axis=0)


def _shift_up(cur, next8, k):
    if k == 0:
        return cur
    n = cur.shape[0]
    rolled = pltpu.roll(cur, n - k, 0)
    tail = jnp.where(_row_iota8(cur.shape[1]) >= SUBLANES - k, pltpu.roll(next8, SUBLANES - k, 0), rolled[n - SUBLANES:])
    return jnp.concatenate([rolled[:n - SUBLANES], tail], axis=0)


def _load_prev8(ref, i, rows):
    start = pl.multiple_of(jnp.maximum(i * rows - SUBLANES, 0), SUBLANES)
    p = ref[pl.ds(start, SUBLANES), :]
    return jnp.where(i > 0, p, jnp.zeros_like(p))


def _conv_rows(ref, w_ref, b_ref, i, rows, width):
    cur = ref[pl.ds(pl.multiple_of(i * rows, rows), rows), :]
    prev8 = _load_prev8(ref, i, rows)
    shifted = [_shift_down(cur, prev8, k) for k in range(width)]
    acc = b_ref[...] + w_ref[width - 1:width, :] * shifted[0]
    for k in range(1, width):
        acc = acc + w_ref[width - 1 - k:width - k, :] * shifted[k]
    return acc, shifted


def _conv_bwd_rows(du, next8, w_ref, width):
    acc = w_ref[width - 1:width, :] * du
    for k in range(1, width):
        acc = acc + w_ref[width - 1 - k:width - k, :] * _shift_up(du, next8, k)
    return acc


def ffn_mid_fwd(u0, cw, cb):
    s, f2 = u0.shape
    f = f2 // 2
    nt = f // CONV_COLS
    rows = min(CONV_ROWS, s)

    def body(ug_ref, uu_ref, wg_ref, wu_ref, bg_ref, bu_ref, a_ref):
        def step(i, carry):
            g, _ = _conv_rows(ug_ref, wg_ref, bg_ref, i, rows, FFN_CONV)
            u, _ = _conv_rows(uu_ref, wu_ref, bu_ref, i, rows, FFN_CONV)
            a_ref[pl.ds(pl.multiple_of(i * rows, rows), rows), :] = (_silu(g) * u).astype(a_ref.dtype)
            return carry

        lax.fori_loop(0, s // rows, step, 0)

    col = lambda off: pl.BlockSpec((s, CONV_COLS), lambda j: (0, j + off))
    wsp = lambda r, off: pl.BlockSpec((r, CONV_COLS), lambda j: (0, j + off))
    return pl.pallas_call(
        body, name="ffn_mid_fwd", grid=(nt,),
        in_specs=[col(0), col(nt), wsp(FFN_CONV, 0), wsp(FFN_CONV, nt), wsp(1, 0), wsp(1, nt)],
        out_specs=pl.BlockSpec((s, CONV_COLS), lambda j: (0, j)),
        out_shape=_sds((s, f), MXU_DTYPE), compiler_params=_params(("parallel",)),
    )(u0, u0, cw, cw, cb, cb)


def ffn_mid_bwd(u0, da, cw, cb):
    s, f2 = u0.shape
    f = f2 // 2
    nt = f // CONV_COLS
    rows = min(CONV_ROWS, s)
    nsteps = s // rows
    w = FFN_CONV

    def body(ug_ref, uu_ref, da_ref, wg_ref, wu_ref, bg_ref, bu_ref,
             du0_ref, dwg_ref, dwu_ref, dbg_ref, dbu_ref):
        zero8 = jnp.zeros((SUBLANES, CONV_COLS), F32)
        zrow = jnp.zeros((1, CONV_COLS), F32)

        def step(it, carry):
            ng, nu, accs = carry
            i = nsteps - 1 - it
            r0 = pl.multiple_of(i * rows, rows)
            g, sg = _conv_rows(ug_ref, wg_ref, bg_ref, i, rows, w)
            u, su = _conv_rows(uu_ref, wu_ref, bu_ref, i, rows, w)
            dav = da_ref[pl.ds(r0, rows), :]
            sg_val, sg_grad = _silu_and_grad(g)
            dg = dav * u * sg_grad
            du = dav * sg_val
            du0_ref[0, pl.ds(r0, rows), :] = _conv_bwd_rows(dg, ng, wg_ref, w).astype(du0_ref.dtype)
            du0_ref[1, pl.ds(r0, rows), :] = _conv_bwd_rows(du, nu, wu_ref, w).astype(du0_ref.dtype)
            new = []
            for j in range(w):
                new.append(accs[j] + jnp.sum(dg * sg[w - 1 - j], axis=0, keepdims=True))
            for j in range(w):
                new.append(accs[w + j] + jnp.sum(du * su[w - 1 - j], axis=0, keepdims=True))
            new.append(accs[2 * w] + jnp.sum(dg, axis=0, keepdims=True))
            new.append(accs[2 * w + 1] + jnp.sum(du, axis=0, keepdims=True))
            return dg[0:SUBLANES], du[0:SUBLANES], tuple(new)

        _, _, accs = lax.fori_loop(0, nsteps, step, (zero8, zero8, tuple([zrow] * (2 * w + 2))))
        dwg_ref[...] = jnp.concatenate(accs[0:w], axis=0)
        dwu_ref[...] = jnp.concatenate(accs[w:2 * w], axis=0)
        dbg_ref[...] = accs[2 * w]
        dbu_ref[...] = accs[2 * w + 1]

    col = lambda off: pl.BlockSpec((s, CONV_COLS), lambda j: (0, j + off))
    wsp = lambda r, off: pl.BlockSpec((r, CONV_COLS), lambda j: (0, j + off))
    outs = pl.pallas_call(
        body, name="ffn_mid_bwd", grid=(nt,),
        in_specs=[col(0), col(nt), col(0), wsp(w, 0), wsp(w, nt), wsp(1, 0), wsp(1, nt)],
        out_specs=[pl.BlockSpec((2, s, CONV_COLS), lambda j: (0, 0, j)), wsp(w, 0), wsp(w, 0), wsp(1, 0), wsp(1, 0)],
        out_shape=[_sds((2, s, f), MXU_DTYPE), _sds((w, f), F32), _sds((w, f), F32), _sds((1, f), F32), _sds((1, f), F32)],
        compiler_params=_params(("parallel",)),
    )(u0, u0, da, cw, cw, cb, cb)
    du0, dwg, dwu, dbg, dbu = outs
    return du0, jnp.concatenate([dwg, dwu], axis=1), jnp.concatenate([dbg, dbu], axis=1)


SB_BLOCK = 128
SB_DEAD = 88.0
SB_HEADS_PER_STEP = 4


def _split_hi_lo(x):
    hi = x.astype(MXU_DTYPE)
    lo = (x - hi.astype(F32)).astype(MXU_DTYPE)
    return hi, lo


def _dot_exact01(x, tri):
    hi, lo = _split_hi_lo(x)
    return _dot(hi, tri) + _dot(lo, tri)


def _stack_heads(pair, lane_lo):
    zero = jnp.zeros_like(pair)
    return jnp.concatenate([jnp.where(lane_lo, pair, zero), jnp.where(lane_lo, zero, pair)], axis=0)


def _unstack_heads(tall, lane_lo):
    n = tall.shape[0] // 2
    return jnp.where(lane_lo, tall[:n], tall[n:])


def _sb_logits(stacked_q, k_ref, k0, pair_cols, blk):
    z = [_dot(sq, k_ref[pl.ds(k0, blk), cols], _NT) for sq, cols in zip(stacked_q, pair_cols)]
    return jnp.concatenate(z, axis=0) * (SB_HEAD_DIM ** -0.5)


def _sb_logs(z, blk, diagonal):
    t = jnp.log(1.0 + jnp.exp(-jnp.abs(z)))
    lb = jnp.minimum(z, 0.0) - t
    lf = jnp.minimum(-z, 0.0) - t
    if not diagonal:
        return lb, lf, None
    strict = lax.broadcasted_iota(jnp.int32, z.shape, 1) < (lax.broadcasted_iota(jnp.int32, z.shape, 0) & (blk - 1))
    return lb, jnp.where(strict, lf, 0.0), strict


def _keep(strict, x):
    return x if strict is None else jnp.where(strict, x, 0.0)


def _tri(blk, upper):
    r = lax.broadcasted_iota(jnp.int32, (blk, blk), 0)
    c = lax.broadcasted_iota(jnp.int32, (blk, blk), 1)
    return jnp.where((r > c) if upper else (r < c), 1.0, 0.0).astype(MXU_DTYPE)


def _tri_sum(x, tri2):
    hi, lo = _split_hi_lo(x)
    return _dot(jnp.concatenate([hi, lo], axis=1), tri2)


def sb_fwd(qkv, hook=None):
    s = qkv.shape[0]
    blk = min(SB_BLOCK, s)
    nblk = s // blk
    nh = SB_HEADS_PER_STEP
    nstep = SB_HEADS // nh
    dh = SB_HEAD_DIM

    def body(q_ref, k_ref, v_ref, o_ref):
        suffix_tri2 = jnp.concatenate([_tri(blk, True)] * 2, axis=0)
        lane_lo = lax.broadcasted_iota(jnp.int32, (1, LANES), 1) < dh
        pairs = [slice(p * LANES, (p + 1) * LANES) for p in range(nh // 2)]

        def qstep(qi, carry):
            q0 = pl.multiple_of(qi * blk, blk)
            qst = [_stack_heads(q_ref[pl.ds(q0, blk), cols], lane_lo) for cols in pairs]

            def tile(kb, run, accs, diagonal):
                k0 = pl.multiple_of(kb * blk, blk)
                lb, lf, strict = _sb_logs(_sb_logits(qst, k_ref, k0, pairs, blk), blk, diagonal)
                sloc = _tri_sum(lf, suffix_tri2)
                a = _mx(_keep(strict, jnp.exp(lb + sloc + run)))
                accs = tuple(
                    acc + _unstack_heads(_dot(a[2 * blk * p:2 * blk * (p + 1)], v_ref[pl.ds(k0, blk), cols]), lane_lo)
                    for p, (acc, cols) in enumerate(zip(accs, pairs)))
                run = run + sloc[:, 0:1] + lf[:, 0:1]
                return run, accs, jnp.max(run) > -SB_DEAD

            def kstep(st):
                it, run, accs, _ = st
                return (it + 1, *tile(qi - it, run, accs, False))

            first = tile(qi, jnp.zeros((nh * blk, 1), F32), tuple([jnp.zeros((blk, LANES), F32)] * len(pairs)), True)
            _, _, accs, _ = lax.while_loop(lambda st: jnp.logical_and(st[0] <= qi, st[3]), kstep, (jnp.int32(1), *first))
            for acc, cols in zip(accs, pairs):
                o_ref[pl.ds(q0, blk), cols] = acc.astype(o_ref.dtype)
            return carry

        lax.fori_loop(0, nblk, qstep, 0)

    col = lambda off: pl.BlockSpec((s, nh * dh), lambda p: (0, p + off))
    out, carried = hosted_call(body, hook, "sb_fwd", (nstep,), [col(0), col(nstep), col(2 * nstep)], col(0),
                               _sds((s, D_MODEL), MXU_DTYPE), [], ("parallel",), (qkv, qkv, qkv))
    return out if hook is None else (out, carried)


def sb_bwd(qkv, do, hook=None):
    s = qkv.shape[0]
    blk = min(SB_BLOCK, s)
    nblk = s // blk
    nh = SB_HEADS_PER_STEP
    nstep = SB_HEADS // nh
    dh = SB_HEAD_DIM

    def body(q_ref, k_ref, v_ref, do_ref, dq_ref, dk_ref, dv_ref, dk_acc, dv_acc, run_ref):
        suffix_tri2 = jnp.concatenate([_tri(blk, True)] * 2, axis=0)
        prefix_tri2 = jnp.concatenate([_tri(blk, False)] * 2, axis=0)
        dk_acc[...] = jnp.zeros_like(dk_acc)
        dv_acc[...] = jnp.zeros_like(dv_acc)
        lane_lo = lax.broadcasted_iota(jnp.int32, (1, LANES), 1) < dh
        pairs = [slice(p * LANES, (p + 1) * LANES) for p in range(nh // 2)]

        def qstep(qi, carry):
            q0 = pl.multiple_of(qi * blk, blk)
            qst = [_stack_heads(q_ref[pl.ds(q0, blk), cols], lane_lo) for cols in pairs]
            dost = [_stack_heads(do_ref[pl.ds(q0, blk), cols], lane_lo) for cols in pairs]

            def enter(kb, run, diagonal):
                run_ref[kb] = run
                _, lf, _ = _sb_logs(_sb_logits(qst, k_ref, pl.multiple_of(kb * blk, blk), pairs, blk), blk, diagonal)
                run = run + jnp.sum(lf, axis=1, keepdims=True)
                return run, jnp.max(run) > -SB_DEAD

            def sweep1(st):
                it, run, _ = st
                return (it + 1, *enter(qi - it, run, False))

            nlive, _, _ = lax.while_loop(lambda st: jnp.logical_and(st[0] <= qi, st[2]), sweep1,
                                         (jnp.int32(1), *enter(qi, jnp.zeros((nh * blk, 1), F32), True)))

            def tile(kb, pg, dqs, diagonal):
                k0 = pl.multiple_of(kb * blk, blk)
                lb, lf, strict = _sb_logs(_sb_logits(qst, k_ref, k0, pairs, blk), blk, diagonal)
                sloc = _tri_sum(lf, suffix_tri2)
                a = _keep(strict, jnp.exp(lb + sloc + run_ref[kb]))
                da = jnp.concatenate([_dot(d, v_ref[pl.ds(k0, blk), cols], _NT) for d, cols in zip(dost, pairs)], axis=0)
                g = da * a
                p = pg + _tri_sum(g, prefix_tri2)
                sig = jnp.exp(lb)
                dz = _mx(_keep(strict, g * (1.0 - sig) - p * sig) * (dh ** -0.5))
                am = _mx(a)
                new_dqs = []
                for i, cols in enumerate(pairs):
                    rows = slice(2 * blk * i, 2 * blk * (i + 1))
                    new_dqs.append(dqs[i] + _unstack_heads(_dot(dz[rows], k_ref[pl.ds(k0, blk), cols]), lane_lo))
                    dk_acc[pl.ds(k0, blk), cols] += _dot(dz[rows], qst[i], _TN)
                    dv_acc[pl.ds(k0, blk), cols] += _dot(am[rows], dost[i], _TN)
                return pg + jnp.sum(g, axis=1, keepdims=True), tuple(new_dqs)

            pg, dqs = lax.fori_loop(qi + 1 - nlive, qi, lambda kb, st: tile(kb, *st, False),
                                    (jnp.zeros((nh * blk, 1), F32), tuple([jnp.zeros((blk, LANES), F32)] * len(pairs))))
            _, dqs = tile(qi, pg, dqs, True)
            for dq, cols in zip(dqs, pairs):
                dq_ref[pl.ds(q0, blk), cols] = dq.astype(dq_ref.dtype)
            return carry

        lax.fori_loop(0, nblk, qstep, 0)
        dk_ref[...] = dk_acc[...].astype(dk_ref.dtype)
        dv_ref[...] = dv_acc[...].astype(dv_ref.dtype)

    col = lambda off: pl.BlockSpec((s, nh * dh), lambda p: (0, p + off))
    (dq, dk, dv), carried = hosted_call(
        body, hook, "sb_bwd", (nstep,), [col(0), col(nstep), col(2 * nstep), col(0)], [col(0), col(0), col(0)],
        [_sds((s, D_MODEL), MXU_DTYPE)] * 3,
        [pltpu.VMEM((s, nh * dh), F32), pltpu.VMEM((s, nh * dh), F32), pltpu.VMEM((nblk, nh * blk, 1), F32)],
        ("parallel",), (qkv, qkv, qkv, do))
    return jnp.concatenate([dq, dk, dv], axis=1), carried


SSD_XBC_TILE0 = SSD_D_INNER // CONV_COLS


def ssd_conv_fwd(proj, cw, cb):
    s = proj.shape[0]
    rows = min(CONV_ROWS, s)

    def body(u_ref, w_ref, b_ref, o_ref):
        def step(i, carry):
            u, _ = _conv_rows(u_ref, w_ref, b_ref, i, rows, SSD_CONV)
            o_ref[pl.ds(pl.multiple_of(i * rows, rows), rows), :] = _silu(u)
            return carry

        lax.fori_loop(0, s // rows, step, 0)

    return pl.pallas_call(
        body, name="ssd_conv_fwd", grid=(SSD_CONV_DIM // CONV_COLS,),
        in_specs=[pl.BlockSpec((s, CONV_COLS), lambda j: (0, j + SSD_XBC_TILE0)),
                  pl.BlockSpec((SSD_CONV, CONV_COLS), lambda j: (0, j)), pl.BlockSpec((1, CONV_COLS), lambda j: (0, j))],
        out_specs=pl.BlockSpec((s, CONV_COLS), lambda j: (0, j)),
        out_shape=_sds((s, SSD_CONV_DIM), F32), compiler_params=_params(("parallel",)),
    )(proj, cw, cb)


def ssd_conv_bwd(proj, dact, cw, cb):
    s = proj.shape[0]
    rows = min(CONV_ROWS, s)
    nsteps = s // rows
    w = SSD_CONV

    def body(u_ref, da_ref, w_ref, b_ref, du_ref, dw_ref, db_ref):
        def step(it, carry):
            nxt, accs = carry
            i = nsteps - 1 - it
            r0 = pl.multiple_of(i * rows, rows)
            u, sh = _conv_rows(u_ref, w_ref, b_ref, i, rows, w)
            dconv = da_ref[pl.ds(r0, rows), :] * _silu_and_grad(u)[1]
            du_ref[pl.ds(r0, rows), :] = _conv_bwd_rows(dconv, nxt, w_ref, w).astype(du_ref.dtype)
            new = [accs[j] + jnp.sum(dconv * sh[w - 1 - j], axis=0, keepdims=True) for j in range(w)]
            new.append(accs[w] + jnp.sum(dconv, axis=0, keepdims=True))
            return dconv[0:SUBLANES], tuple(new)

        zrow = jnp.zeros((1, CONV_COLS), F32)
        _, accs = lax.fori_loop(0, nsteps, step, (jnp.zeros((SUBLANES, CONV_COLS), F32), tuple([zrow] * (w + 1))))
        dw_ref[...] = jnp.concatenate(accs[0:w], axis=0)
        db_ref[...] = accs[w]

    col = pl.BlockSpec((s, CONV_COLS), lambda j: (0, j))
    return pl.pallas_call(
        body, name="ssd_conv_bwd", grid=(SSD_CONV_DIM // CONV_COLS,),
        in_specs=[pl.BlockSpec((s, CONV_COLS), lambda j: (0, j + SSD_XBC_TILE0)), col,
                  pl.BlockSpec((w, CONV_COLS), lambda j: (0, j)), pl.BlockSpec((1, CONV_COLS), lambda j: (0, j))],
        out_specs=[col, pl.BlockSpec((w, CONV_COLS), lambda j: (0, j)), pl.BlockSpec((1, CONV_COLS), lambda j: (0, j))],
        out_shape=[_sds((s, SSD_CONV_DIM), MXU_DTYPE), _sds((w, SSD_CONV_DIM), F32), _sds((1, SSD_CONV_DIM), F32)],
        compiler_params=_params(("parallel",)),
    )(proj, dact, cw, cb)


def _split3(x):
    hi = x.astype(MXU_DTYPE)
    r1 = x - hi.astype(F32)
    mid = r1.astype(MXU_DTYPE)
    lo = (r1 - mid.astype(F32)).astype(MXU_DTYPE)
    return hi, mid, lo


def _dot01(x, m, dims=_NN, left=False):
    parts = _split3(x)
    if left:
        return _dot(m, parts[0], dims) + _dot(m, parts[1], dims) + _dot(m, parts[2], dims)
    return _dot(parts[0], m, dims) + _dot(parts[1], m, dims) + _dot(parts[2], m, dims)


def _softplus(x):
    return jnp.maximum(x, 0.0) + jnp.log1p(jnp.exp(-jnp.abs(x)))


def _ssd_consts(dt_bias, a_log, d_skip):
    pad = lambda v: jnp.pad(v.reshape(1, SSD_HEADS), ((0, 0), (0, LANES - SSD_HEADS)))
    head_of = jnp.arange(SSD_D_INNER) // SSD_HEAD_DIM
    expand = (jnp.arange(LANES)[:, None] == head_of[None, :]).astype(MXU_DTYPE)
    return dict(bias_w=pad(dt_bias), alog_w=pad(a_log), bias_c=dt_bias.reshape(SSD_HEADS, 1),
                alog_c=a_log.reshape(SSD_HEADS, 1), dskip=jnp.repeat(d_skip, SSD_HEAD_DIM).reshape(1, SSD_D_INNER),
                expand=expand, reduce=expand.T)


def _expand_heads(x):
    first = lax.broadcasted_iota(jnp.int32, (1, LANES), 1) < SSD_HEAD_DIM
    shape = (x.shape[0], LANES)
    tiles = [jnp.where(first, jnp.broadcast_to(x[:, 2 * p:2 * p + 1], shape), jnp.broadcast_to(x[:, 2 * p + 1:2 * p + 2], shape))
             for p in range(SSD_HEADS // 2)]
    return jnp.concatenate(tiles, axis=1)


def _ssd_chunk_prep(dtp, dtp_t, bias_w, alog_w, bias_c, alog_c, expand):
    L = dtp.shape[0]
    r = lax.broadcasted_iota(jnp.int32, (L, L), 0)
    c = lax.broadcasted_iota(jnp.int32, (L, L), 1)
    tril = r >= c
    lower = jnp.where(tril, 1.0, 0.0).astype(MXU_DTYPE)
    upper = jnp.where(r <= c, 1.0, 0.0).astype(MXU_DTYPE)
    dt_col = _softplus(dtp + bias_w)
    a_col = -jnp.exp(alog_w) * dt_col
    a_row = -jnp.exp(alog_c) * _softplus(dtp_t + bias_c)
    acum_col = _dot01(a_col, lower, left=True)
    acum_row = _dot01(a_row, upper)
    acum_full = _expand_heads(acum_col)
    dt_full = _expand_heads(dt_col)
    return dict(tril=tril, lower=lower, upper=upper, dt_col=dt_col, a_col=a_col, acum_col=acum_col,
                acum_row=acum_row, acum_full=acum_full, dt_full=dt_full)


def _head_mask(j):
    lane = lax.broadcasted_iota(jnp.int32, (1, LANES), 1)
    return jnp.where((lane // SSD_HEAD_DIM) == j, 1.0, 0.0)


def _decay(pre, h):
    seg = pre["acum_col"][:, h:h + 1] - pre["acum_row"][h:h + 1, :]
    return jnp.exp(jnp.where(pre["tril"], seg, -1e30))


def _ssd_specs(s, nc, rev):
    L = SSD_CHUNK
    ci = (lambda i: nc - 1 - i) if rev else (lambda i: i)
    const = lambda shape: pl.BlockSpec(shape, lambda i: (0,) * len(shape))
    return dict(
        xbc=pl.BlockSpec((L, SSD_CONV_DIM), lambda i: (ci(i), 0)),
        dtp=pl.BlockSpec((L, LANES), lambda i: (ci(i), SSD_IN_PAD // LANES - 1)),
        dtp_t=pl.BlockSpec((SSD_HEADS, L), lambda i: (0, ci(i))),
        rows=pl.BlockSpec((L, SSD_D_INNER), lambda i: (ci(i), 0)),
        state=pl.BlockSpec((1, SSD_GROUPS, SSD_STATE, 4 * SSD_HEAD_DIM), lambda i: (ci(i), 0, 0, 0)),
        consts=[const((1, LANES)), const((1, LANES)), const((SSD_HEADS, 1)), const((SSD_HEADS, 1)),
                const((1, SSD_D_INNER)), const((LANES, SSD_D_INNER)), const((SSD_D_INNER, LANES))],
    )


def _const_args(cs):
    return [cs["bias_w"], cs["alog_w"], cs["bias_c"], cs["alog_c"], cs["dskip"], cs["expand"], cs["reduce"]]


def ssd_scan_fwd(act, proj, dtp_t, cs, hook=None):
    s = act.shape[0]
    L = SSD_CHUNK
    nc = s // L
    G, N, GW = SSD_GROUPS, SSD_STATE, 4 * SSD_HEAD_DIM

    def body(act_ref, dtp_ref, dtpt_ref, bw_ref, aw_ref, bc_ref, ac_ref, dsk_ref, ex_ref, rd_ref, y_ref, st_out, st):
        @pl.when(pl.program_id(0) == 0)
        def _():
            st[...] = jnp.zeros_like(st)

        st_out[0] = st[...]
        pre = _ssd_chunk_prep(dtp_ref[...], dtpt_ref[...], bw_ref[...], aw_ref[...], bc_ref[...], ac_ref[...], ex_ref[...])
        acum_full = pre["acum_full"]
        last_full = acum_full[L - 1:L, :]
        for g in range(G):
            bg = _mx(act_ref[:, SSD_D_INNER + g * N:SSD_D_INNER + (g + 1) * N])
            cg = _mx(act_ref[:, SSD_D_INNER + G * N + g * N:SSD_D_INNER + G * N + (g + 1) * N])
            cb = _dot(cg, bg, _NT)
            for half in range(2):
                p = 2 * g + half
                cols = slice(p * LANES, (p + 1) * LANES)
                xs = act_ref[:, cols]
                xdt = xs * pre["dt_full"][:, cols]
                yd = jnp.zeros((L, LANES), F32)
                for j in range(2):
                    m = cb * _decay(pre, 2 * p + j)
                    yd = yd + _dot(_mx(m), _mx(xdt * _head_mask(j)))
                yoff = _dot(cg, _mx(st[g, :, half * LANES:(half + 1) * LANES])) * jnp.exp(acum_full[:, cols])
                y_ref[:, cols] = yd + yoff + dsk_ref[:, cols] * xs
                w = jnp.exp(last_full[:, cols] - acum_full[:, cols])
                st[g, :, half * LANES:(half + 1) * LANES] = (
                    st[g, :, half * LANES:(half + 1) * LANES] * jnp.exp(last_full[:, cols]) + _dot(bg, _mx(xdt * w), _TN))

    sp = _ssd_specs(s, nc, False)
    (y, states), carried = hosted_call(
        body, hook, "ssd_scan_fwd", (nc,), [sp["xbc"], sp["dtp"], sp["dtp_t"]] + sp["consts"],
        [sp["rows"], sp["state"]], [_sds((s, SSD_D_INNER), F32), _sds((nc, G, N, GW), F32)],
        [pltpu.VMEM((G, N, GW), F32)], ("arbitrary",), (act, proj, dtp_t, *_const_args(cs)))
    return y, states, carried


def ssd_scan_bwd(act, proj, dtp_t, cs, states, dy, hook=None):
    s = act.shape[0]
    L = SSD_CHUNK
    nc = s // L
    G, N, GW = SSD_GROUPS, SSD_STATE, 4 * SSD_HEAD_DIM

    def body(act_ref, dtp_ref, dtpt_ref, bw_ref, aw_ref, bc_ref, ac_ref, dsk_ref, ex_ref, rd_ref, st_ref, dy_ref,
             dact_ref, ddtp_ref, dalog_ref, dbias_ref, dskip_ref, dst, dxdt_ref, dac_ref):
        first = pl.program_id(0) == 0

        @pl.when(first)
        def _():
            dst[...] = jnp.zeros_like(dst)
            dalog_ref[...] = jnp.zeros_like(dalog_ref)
            dbias_ref[...] = jnp.zeros_like(dbias_ref)
            dskip_ref[...] = jnp.zeros_like(dskip_ref)

        expand, reduce = ex_ref[...], rd_ref[...]
        pre = _ssd_chunk_prep(dtp_ref[...], dtpt_ref[...], bw_ref[...], aw_ref[...], bc_ref[...], ac_ref[...], expand)
        acum_full = pre["acum_full"]
        last_full = acum_full[L - 1:L, :]
        ones = jnp.ones((2 * L, LANES), MXU_DTYPE)
        lane = lax.broadcasted_iota(jnp.int32, (L, LANES), 1)
        dacum_diag = jnp.zeros((L, LANES), F32)
        dlast_parts = []
        for g in range(G):
            bg = _mx(act_ref[:, SSD_D_INNER + g * N:SSD_D_INNER + (g + 1) * N])
            cg = _mx(act_ref[:, SSD_D_INNER + G * N + g * N:SSD_D_INNER + G * N + (g + 1) * N])
            cb = _dot(cg, bg, _NT)
            dcb = jnp.zeros((L, L), F32)
            dcg = jnp.zeros((L, N), F32)
            dbg = jnp.zeros((L, N), F32)
            for half in range(2):
                p = 2 * g + half
                cols = slice(p * LANES, (p + 1) * LANES)
                hcols = slice(half * LANES, (half + 1) * LANES)
                xs = act_ref[:, cols]
                xdt = xs * pre["dt_full"][:, cols]
                dyv = dy_ref[:, cols]
                dxdt = jnp.zeros((L, LANES), F32)
                parts = []
                for j in range(2):
                    h = 2 * p + j
                    dec = _decay(pre, h)
                    m = cb * dec
                    dyh = _mx(dyv * _head_mask(j))
                    dm = _dot(dyh, _mx(xdt), _NT)
                    dxdt = dxdt + _dot(_mx(m), dyh, _TN)
                    parts.append(_split_hi_lo(dm * m))
                    dcb = dcb + dm * dec
                (ahi, alo), (bhi, blo) = parts
                rows = _dot(jnp.concatenate([jnp.concatenate([ahi, alo], axis=1), jnp.concatenate([bhi, blo], axis=1)], axis=0), ones)
                cols_ = _dot(jnp.concatenate([jnp.concatenate([ahi, bhi], axis=1), jnp.concatenate([alo, blo], axis=1)], axis=0),
                             ones, _TN)
                d_pair = rows - cols_
                dacum_diag = jnp.where(lane == 2 * p, d_pair[:L], jnp.where(lane == 2 * p + 1, d_pair[L:], dacum_diag))
                lam = jnp.exp(acum_full[:, cols])
                stv = _mx(st_ref[0, g, :, hcols])
                z = _dot(cg, stv)
                dz = _mx(lam * dyv)
                dcg = dcg + _dot(dz, stv, _NT)
                dst_in = _dot(cg, dz, _TN)
                dsv = dst[g, :, hcols]
                w = jnp.exp(last_full[:, cols] - acum_full[:, cols])
                q = _dot(bg, _mx(dsv))
                wq = w * q
                dxdt = dxdt + wq
                wqx = wq * xdt
                dbg = dbg + _dot(_mx(xdt * w), _mx(dsv), _NT)
                elast = jnp.exp(last_full[:, cols])
                dlast_p = jnp.sum(wqx, axis=0, keepdims=True) + elast * jnp.sum(dsv * st_ref[0, g, :, hcols], axis=0, keepdims=True)
                dac_ref[:, cols] = dyv * z * lam - wqx
                dlast_parts.append(dlast_p)
                dst[g, :, hcols] = dst_in + dsv * elast
                dxdt_ref[:, cols] = dxdt
                dact_ref[:, cols] = dxdt * pre["dt_full"][:, cols] + dsk_ref[:, cols] * dyv
            dcbm = _mx(dcb)
            dact_ref[:, SSD_D_INNER + g * N:SSD_D_INNER + (g + 1) * N] = dbg + _dot(dcbm, cg, _TN)
            dact_ref[:, SSD_D_INNER + G * N + g * N:SSD_D_INNER + G * N + (g + 1) * N] = dcg + _dot(dcbm, bg)

        xs_all = act_ref[:, 0:SSD_D_INNER]
        dacum = dacum_diag + _dot_exact01(dac_ref[...], reduce)
        dlast = _dot_exact01(jnp.concatenate(dlast_parts, axis=1), reduce)
        row = lax.broadcasted_iota(jnp.int32, (L, LANES), 0)
        dacum = dacum + jnp.where(row == L - 1, dlast, 0.0)
        da_col = _dot01(dacum, pre["upper"], left=True)
        a_w = -jnp.exp(aw_ref[...])
        ddt = a_w * da_col + _dot_exact01(dxdt_ref[...] * xs_all, reduce)
        xin = dtp_ref[...] + bw_ref[...]
        ddtp = ddt * (1.0 / (1.0 + jnp.exp(-xin)))
        valid = lane < SSD_HEADS
        ddtp = jnp.where(valid, ddtp, 0.0)
        ddtp_ref[...] = ddtp
        dbias_ref[...] += jnp.sum(ddtp, axis=0, keepdims=True)
        dalog_ref[...] += jnp.sum(jnp.where(valid, da_col * pre["a_col"], 0.0), axis=0, keepdims=True)
        dskip_ref[...] += jnp.sum(_dot_exact01(dy_ref[...] * xs_all, reduce), axis=0, keepdims=True)

    sp = _ssd_specs(s, nc, True)
    acc = pl.BlockSpec((1, LANES), lambda i: (0, 0))
    outs, carried = hosted_call(
        body, hook, "ssd_scan_bwd", (nc,),
        [sp["xbc"], sp["dtp"], sp["dtp_t"]] + sp["consts"] + [sp["state"], sp["rows"]],
        [sp["xbc"], pl.BlockSpec((L, LANES), lambda i: (nc - 1 - i, 0)), acc, acc, acc],
        [_sds((s, SSD_CONV_DIM), F32), _sds((s, LANES), F32)] + [_sds((1, LANES), F32)] * 3,
        [pltpu.VMEM((G, N, GW), F32), pltpu.VMEM((L, SSD_D_INNER), F32), pltpu.VMEM((L, SSD_D_INNER), F32)],
        ("arbitrary",), (act, proj, dtp_t, *_const_args(cs), states, dy))
    return (*outs, carried)


def ssd_post_fwd(y, proj, g):
    s, d = y.shape
    ts = _pick(s, (256, 128))

    def body(y_ref, z_ref, g_ref, o_ref):
        y2 = y_ref[...] * _silu(z_ref[...])
        r = lax.rsqrt(jnp.mean(y2 * y2, axis=-1, keepdims=True) + NORM_EPS)
        o_ref[...] = (y2 * r * g_ref[...]).astype(o_ref.dtype)

    row = pl.BlockSpec((ts, d), lambda i: (i, 0))
    return pl.pallas_call(
        body, name="ssd_post_fwd", grid=(s // ts,), in_specs=[row, row, pl.BlockSpec((1, d), lambda i: (0, 0))],
        out_specs=row, out_shape=_sds((s, d), MXU_DTYPE), compiler_params=_params(("parallel",)),
    )(y, proj, g)


def ssd_post_bwd(y, proj, g, dy3):
    s, d = y.shape
    ts = _pick(s, (256, 128))

    def body(y_ref, z_ref, g_ref, d3_ref, dy_ref, dz_ref, dg_ref):
        yv, zv = y_ref[...], z_ref[...]
        sz, sgrad = _silu_and_grad(zv)
        y2 = yv * sz
        r = lax.rsqrt(jnp.mean(y2 * y2, axis=-1, keepdims=True) + NORM_EPS)
        xh = y2 * r
        d3 = d3_ref[...]
        dxh = d3 * g_ref[...]
        dy2 = r * (dxh - xh * jnp.mean(dxh * xh, axis=-1, keepdims=True))
        dy_ref[...] = dy2 * sz
        dz_ref[...] = (dy2 * yv * sgrad).astype(dz_ref.dtype)
        part = jnp.sum(d3 * xh, axis=0, keepdims=True)

        @pl.when(pl.program_id(0) == 0)
        def _():
            dg_ref[...] = part

        @pl.when(pl.program_id(0) != 0)
        def _():
            dg_ref[...] += part

    row = pl.BlockSpec((ts, d), lambda i: (i, 0))
    vec = pl.BlockSpec((1, d), lambda i: (0, 0))
    return pl.pallas_call(
        body, name="ssd_post_bwd", grid=(s // ts,), in_specs=[row, row, vec, row], out_specs=[row, row, vec],
        out_shape=[_sds((s, d), F32), _sds((s, d), MXU_DTYPE), _sds((1, d), F32)],
        compiler_params=_params(("arbitrary",)),
    )(y, proj, g, dy3)


def dt_transpose(proj):
    s = proj.shape[0]
    ts = _pick(s, (512, 256, 128))

    def body(p_ref, o_ref):
        o_ref[...] = p_ref[...].T

    return pl.pallas_call(
        body, name="dt_transpose", grid=(s // ts,),
        in_specs=[pl.BlockSpec((ts, LANES), lambda i: (i, SSD_IN_PAD // LANES - 1))],
        out_specs=pl.BlockSpec((LANES, ts), lambda i: (0, i)), out_shape=_sds((LANES, s), F32),
        compiler_params=_params(("parallel",)),
    )(proj)


def ssd_core_fwd(proj, cw, cb, dt_bias, a_log, d_skip, norm_g, hook=None):
    cs = _ssd_consts(dt_bias, a_log, d_skip)
    act = ssd_conv_fwd(proj, cw, cb)
    dtp_t = dt_transpose(proj)
    y, states, carried = ssd_scan_fwd(act, proj, dtp_t, cs, hook)
    y3 = ssd_post_fwd(y, proj, norm_g)
    return y3, (cs, act, dtp_t, y, states), carried


def ssd_core_bwd(proj, cw, cb, norm_g, saved, dy3, hook=None):
    cs, act, dtp_t, y, states = saved
    dy, dz, dnorm = ssd_post_bwd(y, proj, norm_g, dy3)
    dact, ddtp, dalog, dbias, dskip, carried = ssd_scan_bwd(act, proj, dtp_t, cs, states, dy, hook)
    dxbc, dcw, dcb = ssd_conv_bwd(proj, dact, cw, cb)
    dproj = jnp.concatenate([dz, dxbc, ddtp.astype(MXU_DTYPE)], axis=1)
    h = SSD_HEADS
    return dproj, dcw, dcb, dbias[0, :h], dalog[0, :h], dskip[0, :h], dnorm, carried


def ssd_core(proj, cw, cb, dt_bias, a_log, d_skip, norm_g, dy3):
    y3, saved, _ = ssd_core_fwd(proj, cw, cb, dt_bias, a_log, d_skip, norm_g)
    return y3, ssd_core_bwd(proj, cw, cb, norm_g, saved, dy3)


def loss_head(x, g, target):
    s, d = x.shape
    ts = _pick(s, (512, 256, 128))

    def body(x_ref, g_ref, t_ref, loss_ref, dx_ref, dxm_ref, dg_ref):
        xv = x_ref[...]
        r = lax.rsqrt(jnp.mean(xv * xv, axis=-1, keepdims=True) + NORM_EPS)
        xh = xv * r
        err = xh * g_ref[...] - t_ref[...]
        dy = err * (1.0 / d)
        dxh = dy * g_ref[...]
        dx = r * (dxh - xh * jnp.mean(dxh * xh, axis=-1, keepdims=True))
        dx_ref[...] = dx
        dxm_ref[...] = dx.astype(dxm_ref.dtype)
        part = jnp.sum(dy * xh, axis=0, keepdims=True)
        lpart = jnp.full((1, LANES), 0.5 * jnp.sum(jnp.mean(err * err, axis=-1, keepdims=True)), F32)

        @pl.when(pl.program_id(0) == 0)
        def _():
            dg_ref[...] = part
            loss_ref[...] = lpart

        @pl.when(pl.program_id(0) != 0)
        def _():
            dg_ref[...] += part
            loss_ref[...] += lpart

    row = pl.BlockSpec((ts, d), lambda i: (i, 0))
    vec = pl.BlockSpec((1, d), lambda i: (0, 0))
    return pl.pallas_call(
        body, name="loss_head", grid=(s // ts,), in_specs=[row, vec, row],
        out_specs=[pl.BlockSpec((1, LANES), lambda i: (0, 0)), row, row, vec],
        out_shape=[_sds((1, LANES), F32), _sds((s, d), F32), _sds((s, d), MXU_DTYPE), _sds((1, d), F32)],
        compiler_params=_params(("arbitrary",)),
    )(x, g, target)


def _adamw_math(w, g, m, v):
    m = ADAM_B1 * m + (1.0 - ADAM_B1) * g
    v = ADAM_B2 * v + (1.0 - ADAM_B2) * (g * g)
    m_hat = m / (1.0 - ADAM_B1 ** ADAM_STEP)
    v_hat = v / (1.0 - ADAM_B2 ** ADAM_STEP)
    return -ADAM_LR * (m_hat / (jnp.sqrt(v_hat) + ADAM_EPS) + ADAM_WD * w), m, v


def adamw(w, g, m, v, name="adamw"):
    r, c = w.shape
    tr = _pick(r, (256, 128, 64, 32, 16, 8))

    def body(w_ref, g_ref, m_ref, v_ref, d_ref, nm_ref, nv_ref):
        d_ref[...], nm_ref[...], nv_ref[...] = _adamw_math(w_ref[...], g_ref[...], m_ref[...], v_ref[...])

    blk = pl.BlockSpec((tr, c), lambda i: (i, 0))
    return pl.pallas_call(
        body, name=name, grid=(r // tr,), in_specs=[blk] * 4, out_specs=[blk] * 3,
        out_shape=[_sds((r, c), F32)] * 3, compiler_params=_params(("parallel",)),
    )(w, g, m, v)


def adamw_small(w, parts, m, v):
    n, r, c = parts.shape

    def body(w_ref, p_ref, m_ref, v_ref, g_ref, d_ref, nm_ref, nv_ref):
        g = p_ref[0]
        for k in range(1, n):
            g = g + p_ref[k]
        g_ref[...] = g
        d_ref[...], nm_ref[...], nv_ref[...] = _adamw_math(w_ref[...], g, m_ref[...], v_ref[...])

    return pl.pallas_call(
        body, name="adamw_small", out_shape=[_sds((r, c), F32)] * 4, compiler_params=_params(),
    )(w, parts, m, v)


def pair_sum(unit, recv, where):
    nchip, _, r, c = unit.shape
    tr = _pick(r, (512, 256, 176, 128, 64, 32, 16))

    def body(w_ref, a_ref, b_ref, o_ref, ob_ref):
        sm = a_ref[0, 0] + b_ref[0]
        ob_ref[0] = sm.astype(ob_ref.dtype)

        @pl.when(pl.program_id(1) == w_ref[0])
        def _():
            o_ref[...] = sm

    blk = pl.BlockSpec((1, tr, c), lambda i, s, w: (s, i, 0))
    return pl.pallas_call(
        body, name="pair_sum",
        grid_spec=pltpu.PrefetchScalarGridSpec(
            num_scalar_prefetch=1, grid=(r // tr, nchip),
            in_specs=[pl.BlockSpec((1, 1, tr, c), lambda i, s, w: (s, w[1], i, 0)), blk],
            out_specs=[pl.BlockSpec((tr, c), lambda i, s, w: (i, 0)), blk]),
        out_shape=[_sds((r, c), F32), _sds((nchip, r, c), jnp.bfloat16)],
        compiler_params=_params(("parallel", "arbitrary")),
    )(where, unit, recv)


def chip_sum(own, where, recv, layer, layers, prev=None):
    r, c = own.shape
    tr = _pick(r, (512, 256, 176, 128, 64, 32, 16))

    def body(s_ref, a_ref, b_ref, *rest):
        rest[-1][...] = a_ref[...] + b_ref[0].astype(F32) + b_ref[1].astype(F32) + b_ref[2].astype(F32)

    in_specs = [pl.BlockSpec((tr, c), lambda i, s: (i, 0)), pl.BlockSpec((3, tr, c), lambda i, s: (0, i, 0))]
    args = [where, own, recv]
    if prev is not None:
        in_specs.append(ANY)
        args.append(prev)
    return pl.pallas_call(
        body, name="chip_sum",
        grid_spec=pltpu.PrefetchScalarGridSpec(
            num_scalar_prefetch=1, grid=(r // tr,), in_specs=in_specs,
            out_specs=pl.BlockSpec((None, None, tr, c), lambda i, s: (layer, s[1], i, 0))),
        out_shape=_sds((layers, 2, r, c), F32), input_output_aliases={} if prev is None else {3: 0},
        compiler_params=_params(("parallel",)),
    )(*args)


def place_cast(w, layer, chip):
    _, a, b = w.shape
    ta = _pick(a, (512, 352, 256, 128))

    def body(c_ref, w_ref, o_ref):
        o_ref[...] = w_ref[...].astype(o_ref.dtype)

    return pl.pallas_call(
        body, name="place_cast",
        grid_spec=pltpu.PrefetchScalarGridSpec(
            num_scalar_prefetch=1, grid=(a // ta,),
            in_specs=[pl.BlockSpec((None, ta, b), lambda i, c: (layer, i, 0))],
            out_specs=pl.BlockSpec((None, ta, b), lambda i, c: (c[0], i, 0))),
        out_shape=_sds((N_CHIPS, a, b), MXU_DTYPE), compiler_params=_params(("parallel",)),
    )(chip, w)


ANY = pl.BlockSpec(memory_space=pl.ANY)
COMM = pltpu.CompilerParams(has_side_effects=True)


def _coords():
    return lax.axis_index("x"), lax.axis_index("y"), lax.axis_index("c")


def _other_chips(x, y):
    return [(1 - x, y), (x, 1 - y), (1 - x, 1 - y)]


def all_gather_8(halves, name):
    _, r, c = halves.shape

    def body(h_ref, out_ref, send_sems, recv_sems, local_sem):
        x, y, cc = _coords()
        _gather_one(h_ref.at[cc], lambda px, py, pc: out_ref.at[4 * px + 2 * py + pc],
                    lambda k: send_sems.at[k], lambda k: recv_sems.at[k], local_sem)

    return pl.pallas_call(
        body, name=name, in_specs=[ANY], out_specs=ANY, out_shape=_sds((8, r, c), halves.dtype),
        scratch_shapes=[pltpu.SemaphoreType.DMA((7,)), pltpu.SemaphoreType.DMA((7,)), pltpu.SemaphoreType.DMA],
        compiler_params=COMM,
    )(halves)


def _gather_plan(x_ref, slot, send_sem, recv_sem, local_sem):
    x, y, cc = _coords()
    me, sibling = (x, y, cc), (x, y, 1 - cc)
    chips = _other_chips(x, y)

    def copy(k, blk, to, src=None):
        return pltpu.make_async_remote_copy(
            src_ref=slot(*blk) if src is None else src, dst_ref=slot(*blk),
            send_sem=send_sem(k), recv_sem=recv_sem(k), device_id=to, device_id_type=MESH)

    mine = pltpu.make_async_copy(x_ref, slot(*me), local_sem)
    first = [copy(0, me, sibling, src=x_ref)] + [copy(1 + j, me, (*chip, cc), src=x_ref) for j, chip in enumerate(chips)]
    passed = [copy(4 + j, (*chip, cc), sibling) for j, chip in enumerate(chips)]
    over_ici = [copy(1 + j, (*chip, cc), me) for j, chip in enumerate(chips)]
    from_sibling = [copy(0, sibling, me)] + [copy(4 + j, (*chip, 1 - cc), me) for j, chip in enumerate(chips)]
    return mine, first, passed, over_ici, from_sibling


def _gather_run(plans):
    for mine, first, _, _, _ in plans:
        mine.start()
        for cp in first:
            cp.start()
    for j in range(3):
        for _, _, passed, over_ici, _ in plans:
            over_ici[j].wait_recv()
            passed[j].start()
    for mine, first, passed, _, from_sibling in plans:
        for cp in from_sibling:
            cp.wait_recv()
        for cp in first + passed:
            cp.wait_send()
        mine.wait()


def _gather_one(x_ref, slot, send_sem, recv_sem, local_sem):
    _gather_run([_gather_plan(x_ref, slot, send_sem, recv_sem, local_sem)])


def gather_hook(items):
    n = len(items)

    def plan(refs, send_sems, recv_sems):
        x, y, cc = _coords()

        def copy(i, k, px, py, pc, to):
            blk = refs[i].at[2 * px + py, pc]
            return pltpu.make_async_remote_copy(src_ref=blk, dst_ref=blk, send_sem=send_sems.at[i, k],
                                                recv_sem=recv_sems.at[i, k], device_id=to, device_id_type=MESH)

        chips = _other_chips(x, y)
        first = [copy(i, j, x, y, cc, (*chip, cc)) for i in range(n) for j, chip in enumerate(chips)]
        return copy, chips, first, (x, y, cc)

    def start(refs, new, sems):
        for cp in plan(refs, *sems)[2]:
            cp.start()

    def finish(refs, new, sems):
        copy, chips, first, (x, y, cc) = plan(refs, *sems)
        passed = []
        for j, chip in enumerate(chips):
            for i in range(n):
                copy(i, j, *chip, cc, (x, y, cc)).wait_recv()
                passed.append(copy(i, 3 + j, *chip, cc, (x, y, 1 - cc)))
                passed[-1].start()
        for j, chip in enumerate(chips):
            for i in range(n):
                copy(i, 3 + j, *chip, 1 - cc, (x, y, cc)).wait_recv()
        for cp in first + passed:
            cp.wait_send()

    return dict(arrays=list(items), new=[], start=start, finish=finish, in_place=True,
                sems=[pltpu.SemaphoreType.DMA((n, 6)), pltpu.SemaphoreType.DMA((n, 6))])


def hosted_call(body, hook, name, grid, in_specs, out_specs, out_shape, scratch_shapes, sem, args):
    single = not isinstance(out_shape, (list, tuple))
    out_specs_l = [out_specs] if single else list(out_specs)
    out_shape_l = [out_shape] if single else list(out_shape)
    if hook is None:
        res = pl.pallas_call(body, name=name, grid=grid, in_specs=list(in_specs), out_specs=out_specs, out_shape=out_shape,
                             scratch_shapes=list(scratch_shapes), compiler_params=_params(sem))(*args)
        return res, []
    items, new = hook["arrays"], hook["new"]
    k, kn, n_in, n_out, n_scr = len(items), len(new), len(in_specs), len(out_specs_l), len(scratch_shapes)
    ka = k if hook["in_place"] else 0

    def full(*refs):
        ins = refs[:n_in]
        base = n_in + k
        outs = refs[base:base + n_out]
        hrefs = refs[base + n_out:base + n_out + ka] if ka else refs[n_in:base]
        nrefs = refs[base + n_out + ka:base + n_out + ka + kn]
        scr = refs[base + n_out + ka + kn:base + n_out + ka + kn + n_scr]
        sems = refs[base + n_out + ka + kn + n_scr:]
        ids = [pl.program_id(d) for d in range(len(grid))]
        first = functools.reduce(jnp.logical_and, [i == 0 for i in ids])
        last = functools.reduce(jnp.logical_and, [i == g - 1 for i, g in zip(ids, grid)])

        @pl.when(first)
        def _():
            hook["start"](hrefs, nrefs, sems)

        body(*ins, *outs, *scr)

        @pl.when(last)
        def _():
            hook["finish"](hrefs, nrefs, sems)

    res = pl.pallas_call(
        full, name=name, grid=grid, in_specs=list(in_specs) + [ANY] * k, out_specs=out_specs_l + [ANY] * (ka + kn),
        out_shape=out_shape_l + [_sds(a.shape, a.dtype) for a in items[:ka]] + list(new),
        input_output_aliases={n_in + i: n_out + i for i in range(ka)},
        scratch_shapes=list(scratch_shapes) + hook["sems"],
        compiler_params=pltpu.CompilerParams(dimension_semantics=("arbitrary",) * len(grid),
                                             vmem_limit_bytes=VMEM_LIMIT, has_side_effects=True),
    )(*args, *items)
    return (res[0] if single else list(res[:n_out])), list(res[n_out:])


def comm_call(hook, name):
    k, kn = len(hook["arrays"]), len(hook["new"])
    ka = k if hook["in_place"] else 0

    def body(*refs):
        hrefs = refs[k:k + ka] if ka else refs[:k]
        hook["start"](hrefs, refs[k + ka:k + ka + kn], refs[k + ka + kn:])
        hook["finish"](hrefs, refs[k + ka:k + ka + kn], refs[k + ka + kn:])

    return list(pl.pallas_call(
        body, name=name, in_specs=[ANY] * k, out_specs=[ANY] * (ka + kn),
        out_shape=[_sds(a.shape, a.dtype) for a in hook["arrays"][:ka]] + list(hook["new"]),
        input_output_aliases={i: i for i in range(ka)}, scratch_shapes=hook["sems"], compiler_params=COMM,
    )(*hook["arrays"]))


def merge_hooks(hooks):
    hooks = [h for h in hooks if h is not None]
    if len(hooks) < 2:
        return hooks[0] if hooks else None

    def parts(refs, new, sems):
        out, a, b, c = [], 0, 0, 0
        for h in hooks:
            na, nn, ns = len(h["arrays"]), len(h["new"]), len(h["sems"])
            out.append((refs[a:a + na], new[b:b + nn], sems[c:c + ns]))
            a, b, c = a + na, b + nn, c + ns
        return out

    def start(refs, new, sems):
        for h, p in zip(hooks, parts(refs, new, sems)):
            h["start"](*p)

    def finish(refs, new, sems):
        for h, p in zip(hooks, parts(refs, new, sems)):
            h["finish"](*p)

    assert len({h["in_place"] for h in hooks}) == 1
    return dict(arrays=[a for h in hooks for a in h["arrays"]], new=[a for h in hooks for a in h["new"]],
                sems=[a for h in hooks for a in h["sems"]], start=start, finish=finish, in_place=hooks[0]["in_place"])


def split_carried(hooks, carried):
    hooks = [h for h in hooks if h is not None]
    off = sum(len(h["arrays"]) for h in hooks if h["in_place"])
    out = []
    for h in hooks:
        out.append(carried[off:off + len(h["new"])])
        off += len(h["new"])
    return out


def pair_swap_hook(units):
    n = len(units)

    def plan(refs, new, send_sems, recv_sems):
        x, y, cc = _coords()
        return [pltpu.make_async_remote_copy(src_ref=refs[i].at[:, 1 - cc], dst_ref=new[i], send_sem=send_sems.at[i],
                                             recv_sem=recv_sems.at[i], device_id=(x, y, 1 - cc), device_id_type=MESH)
                for i in range(n)]

    def start(refs, new, sems):
        for cp in plan(refs, new, *sems):
            cp.start()

    def finish(refs, new, sems):
        for cp in plan(refs, new, *sems):
            cp.wait()

    return dict(arrays=list(units), new=[_sds((u.shape[0],) + u.shape[2:], u.dtype) for u in units], start=start,
                finish=finish, in_place=False, sems=[pltpu.SemaphoreType.DMA((n,)), pltpu.SemaphoreType.DMA((n,))])


def chip_exchange_hook(units):
    n = len(units)

    def plan(refs, new, send_sems, recv_sems):
        x, y, cc = _coords()
        return [pltpu.make_async_remote_copy(
            src_ref=refs[i].at[2 * px + py], dst_ref=new[i].at[k], send_sem=send_sems.at[i, k],
            recv_sem=recv_sems.at[i, k], device_id=(px, py, cc), device_id_type=MESH)
            for i in range(n) for k, (px, py) in enumerate(_other_chips(x, y))]

    def start(refs, new, sems):
        for cp in plan(refs, new, *sems):
            cp.start()

    def finish(refs, new, sems):
        for cp in plan(refs, new, *sems):
            cp.wait()

    return dict(arrays=list(units), new=[_sds((3,) + u.shape[1:], u.dtype) for u in units], start=start,
                finish=finish, in_place=False, sems=[pltpu.SemaphoreType.DMA((n, 3)), pltpu.SemaphoreType.DMA((n, 3))])


def grad_half_swap(grads):
    n = len(grads)

    def body(*refs):
        outs, send_sems, recv_sems = refs[n:2 * n], refs[2 * n], refs[2 * n + 1]
        x, y, cc = _coords()
        cps = [pltpu.make_async_remote_copy(
            src_ref=outs[i].at[:, cc], dst_ref=outs[i].at[:, cc], send_sem=send_sems.at[i], recv_sem=recv_sems.at[i],
            device_id=(x, y, 1 - cc), device_id_type=MESH) for i in range(n)]
        for cp in cps:
            cp.start()
        for i, cp in enumerate(cps):
            cp.wait_send()
            pltpu.make_async_remote_copy(
                src_ref=outs[i].at[:, 1 - cc], dst_ref=outs[i].at[:, 1 - cc], send_sem=send_sems.at[i],
                recv_sem=recv_sems.at[i], device_id=(x, y, 1 - cc), device_id_type=MESH).wait_recv()

    return pl.pallas_call(
        body, name="grad_half_swap", in_specs=[ANY] * n, out_specs=[ANY] * n,
        out_shape=[_sds(g.shape, g.dtype) for g in grads], input_output_aliases={i: i for i in range(n)},
        scratch_shapes=[pltpu.SemaphoreType.DMA((n,)), pltpu.SemaphoreType.DMA((n,))], compiler_params=COMM,
    )(*grads)


N_CHIPS = 4
PACK_COLS = 1024
BIG = ("ssd_w_in", "ssd_w_out", "sb_w_qkv", "sb_w_out", "ffn_w_in", "ffn_w_out")
CONVW = ("ssd_conv_w", "ffn_conv_w")
COL_SHARDED = ("ssd_w_in", "sb_w_qkv", "ffn_w_in", "ssd_conv_w", "ffn_conv_w")
SMALL = ("mix_norm", "ffn_norm", "final_norm", "ssd_conv_b", "ssd_dt_bias", "ssd_a_log", "ssd_d", "ssd_norm", "ffn_conv_b")
WEIGHTS = ("mix_norm", "ffn_norm", "final_norm", "ssd_w_in", "ssd_conv_w", "ssd_conv_b", "ssd_dt_bias", "ssd_a_log",
           "ssd_d", "ssd_norm", "ssd_w_out", "sb_w_qkv", "sb_w_out", "ffn_w_in", "ffn_conv_w", "ffn_conv_b", "ffn_w_out")


def _to_rows(flat, multiple):
    rows = -(-flat.shape[-1] // PACK_COLS)
    rows = -(-rows // multiple) * multiple
    pad = rows * PACK_COLS - flat.shape[-1]
    return jnp.pad(flat, [(0, pad)]).reshape(rows, PACK_COLS)


def _unshard(name, stacked):
    l, n, a, b = stacked.shape
    if name in COL_SHARDED:
        return jnp.transpose(stacked, (0, 2, 1, 3)).reshape(l, a, n * b)
    return stacked.reshape(l, n * a, b)


def _conv_weight_item(w, chip):
    rows = _to_rows(jnp.concatenate([w[n].reshape(-1) for n in CONVW]), 32)
    own = rows.reshape(1, 2, rows.shape[0] // 2, PACK_COLS)
    return lax.dynamic_update_slice_in_dim(jnp.zeros((N_CHIPS,) + own.shape[1:], F32), own, chip, axis=0)


def _conv_weights_from(item, w):
    got = item.reshape(N_CHIPS, -1)
    out, off = {}, 0
    for n in CONVW:
        l, a, b = w[n].shape
        out[n] = _unshard(n, jnp.moveaxis(got[:, off:off + w[n].size].reshape(N_CHIPS, l, a, b), 0, 1))
        off += w[n].size
    return out


def _finish_big_grads(pair_sums, from_chips, layout):
    cc = lax.axis_index("c").astype(jnp.int32)
    chip = (2 * lax.axis_index("x") + lax.axis_index("y")).astype(jnp.int32)
    where = jnp.stack([chip, cc])
    nlayers = [1 + max(l for k, l in layout if k == wi) for wi in range(1 + max(k for k, _ in layout))]
    grads = [None] * len(nlayers)
    for (wi, l), p, r in zip(layout, pair_sums, from_chips):
        grads[wi] = chip_sum(p, where, r, l, nlayers[wi], grads[wi])
    return grad_half_swap(grads)


def kernel(x, mix_norm, ffn_norm, final_norm, ssd_w_in, ssd_conv_w, ssd_conv_b, ssd_dt_bias, ssd_a_log, ssd_d, ssd_norm, ssd_w_out, sb_w_qkv, sb_w_out, ffn_w_in, ffn_conv_w, ffn_conv_b, ffn_w_out, loss_target, m_mix_norm, m_ffn_norm, m_final_norm, m_ssd_w_in, m_ssd_conv_w, m_ssd_conv_b, m_ssd_dt_bias, m_ssd_a_log, m_ssd_d, m_ssd_norm, m_ssd_w_out, m_sb_w_qkv, m_sb_w_out, m_ffn_w_in, m_ffn_conv_w, m_ffn_conv_b, m_ffn_w_out, v_mix_norm, v_ffn_norm, v_final_norm, v_ssd_w_in, v_ssd_conv_w, v_ssd_conv_b, v_ssd_dt_bias, v_ssd_a_log, v_ssd_d, v_ssd_norm, v_ssd_w_out, v_sb_w_qkv, v_sb_w_out, v_ffn_w_in, v_ffn_conv_w, v_ffn_conv_b, v_ffn_w_out):
    given = dict(locals())
    w = {n: given[n] for n in WEIGHTS}
    mom = {n: given["m_" + n] for n in WEIGHTS}
    var = {n: given["v_" + n] for n in WEIGHTS}
    chip = 2 * lax.axis_index("x") + lax.axis_index("y")

    chip1 = chip.reshape(1).astype(jnp.int32)
    row = lambda v: v.reshape(1, -1)

    def placed(n, l):
        _, a, b = w[n].shape
        return place_cast(w[n], l, chip1).reshape(N_CHIPS, 2, a // 2, b)

    def mixer_items(i):
        return [(n, i // 2) for n in (("ssd_w_in", "ssd_w_out") if i % 2 == 0 else ("sb_w_qkv", "sb_w_out"))]

    def ffn_items(i):
        return [("ffn_w_in", i), ("ffn_w_out", i)]

    def hook_for(items):
        return gather_hook([placed(n, l) for n, l in items]) if items else None

    lw = {}

    def arrived(items, arrays):
        for (n, l), arr in zip(items, arrays):
            g4 = arr.reshape(N_CHIPS, -1, arr.shape[-1])
            if n in COL_SHARDED:
                full = jnp.transpose(g4, (1, 0, 2)).reshape(g4.shape[1], -1)
            else:
                full = g4.reshape(-1, g4.shape[2])
            if n == "ssd_w_in":
                full = jnp.pad(full, ((0, 0), (0, SSD_IN_PAD - SSD_IN_DIM)))
            lw[(n, l)] = full

    first_items = [("ssd_w_in", 0)]
    carry = {
        (0, "mm_in"): [("ssd_w_out", 0), ("ffn_w_in", 0)],
        (0, "scan"): [("ffn_w_out", 0)] + mixer_items(1),
        (0, "ffn_in"): ffn_items(1),
        (1, "sb"): mixer_items(2) + ffn_items(2),
        (2, "mm_in"): mixer_items(3),
        (2, "scan"): ffn_items(3),
    }
    first = comm_call(gather_hook([placed(n, l) for n, l in first_items] + [_conv_weight_item(w, chip)]), "gather_first")
    arrived(first_items, first[:-1])
    fw = _conv_weights_from(first[-1], w)

    def carrying(i, slot, call):
        items = carry.get((i, slot), [])
        if not items:
            return call(None)
        out, got = call(hook_for(items))
        arrived(items, got)
        return out

    xcur = x[0]
    saved = []
    for i in range(DEPTH):
        j = i // 2
        h, r = rms_fwd(xcur, row(mix_norm[i]))
        if i % 2 == 0:
            proj = carrying(i, "mm_in", lambda hk: mm(h, lw[("ssd_w_in", j)], tm=2048, tn=896, name="mm_ssd_in", hook=hk))
            items = carry.get((i, "scan"), [])
            y3, core, got = ssd_core_fwd(proj, fw["ssd_conv_w"][j], row(ssd_conv_b[j]), ssd_dt_bias[j], ssd_a_log[j],
                                         ssd_d[j], row(ssd_norm[j]), hook_for(items))
            arrived(items, got)
            x1 = mm(y3, lw[("ssd_w_out", j)], res=xcur, name="mm_ssd_out")
            mix = (proj, y3, core)
        else:
            qkv = mm(h, lw[("sb_w_qkv", j)], out_dtype=MXU_DTYPE, tm=2048, name="mm_sb_qkv")
            o = carrying(i, "sb", lambda hk: sb_fwd(qkv, hk))
            x1 = mm(o, lw[("sb_w_out", j)], res=xcur, name="mm_sb_out")
            mix = (qkv, o)
        h2, r2 = rms_fwd(x1, row(ffn_norm[i]))
        u0 = carrying(i, "ffn_in", lambda hk: mm(h2, lw[("ffn_w_in", i)], tm=2048, name="mm_ffn_in", hook=hk))
        a = ffn_mid_fwd(u0, fw["ffn_conv_w"][i], row(ffn_conv_b[i]))
        x2 = mm(a, lw[("ffn_w_out", i)], res=x1, name="mm_ffn_out")
        saved.append((xcur, h, r, mix, x1, h2, r2, u0, a))
        xcur = x2
    loss_part, dx, dxm, d_final = loss_head(xcur, row(final_norm), loss_target[0])

    gl = {n: [None] * w[n].shape[0] for n in WEIGHTS if n != "final_norm"}
    units = {n: [None] * w[n].shape[0] for n in BIG}

    def unit_of(g4):
        return g4.reshape(N_CHIPS, 2, g4.shape[1] // 2, g4.shape[2])

    where = jnp.stack([chip, lax.axis_index("c")]).astype(jnp.int32)
    pair_f32, wire, from_chips = {}, {}, {}

    def pair_sums(keys, swapped):
        for (n, l), got in zip(keys, swapped):
            pair_f32[(n, l)], wire[(n, l)] = pair_sum(units[n][l], got, where)

    for i in reversed(range(DEPTH)):
        j = i // 2
        x0, h, r, mix, x1, h2, r2, u0, a = saved[i]
        units["ffn_w_out"][i] = unit_of(mm(a, dxm, "tn", tm=1408, name="mm_d_ffn_out").reshape(N_CHIPS, -1, D_MODEL))
        da = mm(dxm, lw[("ffn_w_out", i)], "nt", tn=1408, name="mm_da_ffn")
        du0, gl["ffn_conv_w"][i], dcb = ffn_mid_bwd(u0, da, fw["ffn_conv_w"][i], row(ffn_conv_b[i]))
        gl["ffn_conv_b"][i] = dcb[0]
        units["ffn_w_in"][i] = unit_of(mm(h2, du0, "tn", tn=1408, tm=512, n_split=N_CHIPS, name="mm_d_ffn_in"))
        keys_f = ffn_items(i)
        swap = pair_swap_hook([units[n][l] for n, l in keys_f])
        dh2, carried = mm(du0, lw[("ffn_w_in", i)], "nt", name="mm_dh_ffn", hook=swap)
        pair_sums(keys_f, split_carried([swap], carried)[0])
        dx1, dx1m, dg = rms_bwd(x1, r2, row(ffn_norm[i]), dh2, dx)
        gl["ffn_norm"][i] = dg[0]
        keys_up = mixer_items(i + 1) if i + 1 < DEPTH else []
        exchanges = [chip_exchange_hook([wire[k] for k in keys_f]),
                     chip_exchange_hook([wire[k] for k in keys_up]) if keys_up else None]
        if i % 2 == 0:
            proj, y3, core = mix
            units["ssd_w_out"][j] = unit_of(mm(y3, dx1m, "tn", name="mm_d_ssd_out").reshape(N_CHIPS, -1, D_MODEL))
            dy3 = mm(dx1m, lw[("ssd_w_out", j)], "nt", name="mm_dy3_ssd")
            (dproj, gl["ssd_conv_w"][j], dcb, gl["ssd_dt_bias"][j], gl["ssd_a_log"][j], gl["ssd_d"][j], dnorm,
             carried) = ssd_core_bwd(proj, fw["ssd_conv_w"][j], row(ssd_conv_b[j]), row(ssd_norm[j]), core, dy3,
                                     merge_hooks(exchanges))
            gl["ssd_conv_b"][j] = dcb[0]
            gl["ssd_norm"][j] = dnorm[0]
            dw_in = mm(h, dproj, "tn", tn=896, name="mm_d_ssd_in")[:, :SSD_IN_DIM]
            units["ssd_w_in"][j] = unit_of(jnp.transpose(dw_in.reshape(D_MODEL, N_CHIPS, -1), (1, 0, 2)))
            dmix, w_in, dh_name = dproj, lw[("ssd_w_in", j)], "mm_dh_ssd"
        else:
            qkv, o = mix
            units["sb_w_out"][j] = unit_of(mm(o, dx1m, "tn", name="mm_d_sb_out").reshape(N_CHIPS, -1, D_MODEL))
            do = mm(dx1m, lw[("sb_w_out", j)], "nt", out_dtype=MXU_DTYPE, name="mm_do_sb")
            dqkv, carried = sb_bwd(qkv, do, merge_hooks(exchanges))
            units["sb_w_qkv"][j] = unit_of(mm(h, dqkv, "tn", tn=768, n_split=N_CHIPS, name="mm_d_sb_qkv"))
            dmix, w_in, dh_name = dqkv, lw[("sb_w_qkv", j)], "mm_dh_sb"
        got = split_carried(exchanges, carried)
        from_chips.update(zip(keys_f, got[0]))
        if keys_up:
            from_chips.update(zip(keys_up, got[1]))
        keys_m = mixer_items(i)
        swap = pair_swap_hook([units[n][l] for n, l in keys_m])
        dh, carried = mm(dmix, w_in, "nt", name=dh_name, hook=swap)
        pair_sums(keys_m, split_carried([swap], carried)[0])
        dx, dxm, dg = rms_bwd(x0, r, row(mix_norm[i]), dh, dx1)
        gl["mix_norm"][i] = dg[0]
    last = chip_exchange_hook([wire[k] for k in mixer_items(0)])
    from_chips.update(zip(mixer_items(0), comm_call(last, "grad_exchange_last")))

    layout = [(k, l) for k, n in enumerate(BIG) for l in range(w[n].shape[0])]
    reduced = _finish_big_grads([pair_f32[(BIG[k], l)] for k, l in layout], [from_chips[(BIG[k], l)] for k, l in layout],
                                layout)
    g, delta, new_m, new_v = {}, {}, {}, {}
    two_d = lambda t: t.reshape(-1, t.shape[-1])
    for n, red in zip(BIG, reduced):
        g[n] = red.reshape(w[n].shape)
        d2, m2, v2 = adamw(two_d(w[n]), two_d(g[n]), two_d(mom[n]), two_d(var[n]), name="adamw_" + n)
        delta[n], new_m[n], new_v[n] = d2.reshape(w[n].shape), m2.reshape(w[n].shape), v2.reshape(w[n].shape)

    small_g = {n: jnp.stack(gl[n]) for n in SMALL + CONVW if n != "final_norm"}
    small_g["final_norm"] = d_final[0]
    zeros_of = lambda n: jnp.zeros((small_g[n].size,), F32)

    def small_pack(d, extra):
        parts = [d[n].reshape(-1) for n in SMALL] + [extra]
        parts += [(d[n].reshape(-1) if d is small_g else zeros_of(n)) for n in CONVW]
        return _to_rows(jnp.concatenate(parts), 16)

    part = small_pack(small_g, loss_part[0, 0:1])
    parts = all_gather_8(jnp.stack([part, part]), "gather_small_grads")
    zero = jnp.zeros((1,), F32)
    gs, ds, ms, vs = adamw_small(small_pack(w, zero), parts, small_pack(mom, zero), small_pack(var, zero))
    gs_flat = gs.reshape(-1)
    off = 0
    for n in SMALL:
        size = w[n].size
        for dst, src in ((g, gs), (delta, ds), (new_m, ms), (new_v, vs)):
            dst[n] = src.reshape(-1)[off:off + size].reshape(w[n].shape)
        off += size
    loss = gs_flat[off]
    off += 1
    for n in CONVW:
        size = small_g[n].size
        b = w[n].shape[-1]
        g[n] = lax.dynamic_slice_in_dim(gs_flat[off:off + size].reshape(small_g[n].shape), chip * b, b, axis=2)
        d2, m2, v2 = adamw(two_d(w[n]), two_d(g[n]), two_d(mom[n]), two_d(var[n]), name="adamw_" + n)
        delta[n], new_m[n], new_v[n] = d2.reshape(w[n].shape), m2.reshape(w[n].shape), v2.reshape(w[n].shape)
        off += size

    return (loss, dx[None], *[g[n] for n in WEIGHTS], *[delta[n] for n in WEIGHTS],
            *[new_m[n] for n in WEIGHTS], *[new_v[n] for n in WEIGHTS])
```

```python
import functools

import jax
import jax.numpy as jnp
from jax import lax
from jax.experimental import pallas as pl
from jax.experimental.pallas import tpu as pltpu

F32 = jnp.float32
MXU_DTYPE = jnp.bfloat16
HIGHEST = lax.Precision.HIGHEST

D_MODEL = 1024
DEPTH = 4
NORM_EPS = 1e-6
SSD_D_INNER = 2048
SSD_HEAD_DIM = 64
SSD_HEADS = 32
SSD_GROUPS = 8
SSD_STATE = 128
SSD_CONV = 4
SSD_CHUNK = 128
SSD_CONV_DIM = 4096
SSD_IN_DIM = 6176
SSD_IN_PAD = 6272
SB_HEADS = 16
SB_HEAD_DIM = 64
FFN_D_FF = 2816
FFN_CONV = 3
ADAM_LR, ADAM_B1, ADAM_B2, ADAM_EPS, ADAM_WD, ADAM_STEP = 0.001, 0.9, 0.999, 1e-08, 0.01, 10

LANES = 128
SUBLANES = 8
VMEM_LIMIT = 56 * 1024 * 1024
MESH = pl.DeviceIdType.MESH


def _params(sem=None):
    return pltpu.CompilerParams(dimension_semantics=sem, vmem_limit_bytes=VMEM_LIMIT)


def _sds(shape, dtype):
    return jax.ShapeDtypeStruct(shape, dtype)


def _dot(a, b, dims=(((1,), (0,)), ((), ())), precision=None):
    return lax.dot_general(a, b, dims, precision=precision, preferred_element_type=F32)


_NN = (((1,), (0,)), ((), ()))
_NT = (((1,), (1,)), ((), ()))
_TN = (((0,), (0,)), ((), ()))


def _mx(a):
    return a.astype(MXU_DTYPE)


def _sigmoid(x):
    return 0.5 * jnp.tanh(0.5 * x) + 0.5


def _silu(x):
    return x * _sigmoid(x)


def _silu_and_grad(x):
    s = _sigmoid(x)
    return x * s, s * (1.0 + x * (1.0 - s))


def _pick(n, cands):
    for c in cands:
        if n % c == 0:
            return c
    return n


def rms_fwd(x, g):
    s, d = x.shape
    ts = _pick(s, (512, 256, 128))

    def body(x_ref, g_ref, h_ref, r_ref):
        xv = x_ref[...]
        r = lax.rsqrt(jnp.mean(xv * xv, axis=-1, keepdims=True) + NORM_EPS)
        h_ref[...] = (xv * r * g_ref[...]).astype(h_ref.dtype)
        r_ref[...] = r

    return pl.pallas_call(
        body, name="rms_fwd", grid=(s // ts,),
        in_specs=[pl.BlockSpec((ts, d), lambda i: (i, 0)), pl.BlockSpec((1, d), lambda i: (0, 0))],
        out_specs=[pl.BlockSpec((ts, d), lambda i: (i, 0)), pl.BlockSpec((ts, 1), lambda i: (i, 0))],
        out_shape=[_sds((s, d), MXU_DTYPE), _sds((s, 1), F32)],
        compiler_params=_params(("parallel",)),
    )(x, g)


def rms_bwd(x, r, g, dh, dres):
    s, d = x.shape
    ts = _pick(s, (512, 256, 128))

    def body(x_ref, r_ref, g_ref, dh_ref, dres_ref, dx_ref, dxm_ref, dg_ref):
        xh = x_ref[...] * r_ref[...]
        dhv = dh_ref[...]
        dxh = dhv * g_ref[...]
        dx = dres_ref[...] + r_ref[...] * (dxh - xh * jnp.mean(dxh * xh, axis=-1, keepdims=True))
        dx_ref[...] = dx
        dxm_ref[...] = dx.astype(dxm_ref.dtype)
        part = jnp.sum(dhv * xh, axis=0, keepdims=True)

        @pl.when(pl.program_id(0) == 0)
        def _():
            dg_ref[...] = part

        @pl.when(pl.program_id(0) != 0)
        def _():
            dg_ref[...] += part

    row = pl.BlockSpec((ts, d), lambda i: (i, 0))
    return pl.pallas_call(
        body, name="rms_bwd", grid=(s // ts,),
        in_specs=[row, pl.BlockSpec((ts, 1), lambda i: (i, 0)), pl.BlockSpec((1, d), lambda i: (0, 0)), row, row],
        out_specs=[row, row, pl.BlockSpec((1, d), lambda i: (0, 0))],
        out_shape=[_sds((s, d), F32), _sds((s, d), MXU_DTYPE), _sds((1, d), F32)],
        compiler_params=_params(("arbitrary",)),
    )(x, r, g, dh, dres)


def mm(a, b, mode="nn", res=None, out_dtype=F32, tm=None, tn=None, n_split=1, name="mm", hook=None):
    halves_a = mode == "nt" and a.ndim == 3
    halves_b = mode == "tn" and b.ndim == 3
    if halves_a:
        a_shape = (a.shape[1], 2 * a.shape[2])
    else:
        a_shape = a.shape
    b_shape = (b.shape[1], 2 * b.shape[2]) if halves_b else b.shape
    if mode == "nn":
        (m, k), (_, n) = a_shape, b_shape
    elif mode == "nt":
        (m, k), (n, _) = a_shape, b_shape
    else:
        (k, m), (_, n) = a_shape, b_shape
    tm = min(tm, m) if tm else _pick(m, (1024, 512, 256, 128))
    tn = min(tn, n) if tn else _pick(n, (512, 896, 256, 128))
    assert m % tm == 0 and n % tn == 0, (m, tm, n, tn)
    dims = {"nn": _NN, "nt": _NT, "tn": _TN}[mode]

    def body(*refs):
        a_ref, b_ref = refs[0], refs[1]
        o_ref = refs[-1]
        if halves_a:
            kh = k // 2
            acc = _dot(_mx(a_ref[0]), _mx(b_ref[:, :kh]), dims) + _dot(_mx(a_ref[1]), _mx(b_ref[:, kh:]), dims)
        else:
            acc = _dot(_mx(a_ref[...]), _mx(b_ref[...]), dims)
        if res is not None:
            acc = acc + refs[2][...]
        o_ref[...] = acc.astype(o_ref.dtype)

    a_spec = pl.BlockSpec((k, tm), lambda i, j: (0, i)) if mode == "tn" else pl.BlockSpec((tm, k), lambda i, j: (i, 0))
    b_spec = pl.BlockSpec((tn, k), lambda i, j: (j, 0)) if mode == "nt" else pl.BlockSpec((k, tn), lambda i, j: (0, j))
    if halves_a:
        a_spec = pl.BlockSpec((2, tm, k // 2), lambda i, j: (0, i, 0))
    if halves_b:
        per_half = n // 2 // tn
        assert per_half * tn * 2 == n
        b_spec = pl.BlockSpec((None, k, tn), lambda i, j: (j // per_half, 0, j % per_half))
    o_spec = pl.BlockSpec((tm, tn), lambda i, j: (i, j))
    ins, specs = [a, b], [a_spec, b_spec]
    if res is not None:
        ins.append(res)
        specs.append(o_spec)
    out_shape = _sds((m, n), out_dtype)
    if n_split > 1:
        per = n // n_split // tn
        o_spec = pl.BlockSpec((None, tm, tn), lambda i, j: (j // per, i, j % per))
        out_shape = _sds((n_split, m, n // n_split), out_dtype)
    out, carried = hosted_call(body, hook, name, (m // tm, n // tn), specs, o_spec, out_shape, [],
                               ("parallel", "parallel"), ins)
    return out if hook is None else (out, carried)


CONV_ROWS = 256
CONV_COLS = 128


def _row_iota8(cols):
    return lax.broadcasted_iota(jnp.int32, (SUBLANES, cols), 0)


def _shift_down(cur, prev8, k):
    if k == 0:
        return cur
    rolled = pltpu.roll(cur, k, 0)
    head = jnp.where(_row_iota8(cur.shape[1]) < k, pltpu.roll(prev8, k, 0), rolled[0:SUBLANES])
    return jnp.concatenate([head, rolled[SUBLANES:]], axis=0)


def _shift_up(cur, next8, k):
    if k == 0:
        return cur
    n = cur.shape[0]
    rolled = pltpu.roll(cur, n - k, 0)
    tail = jnp.where(_row_iota8(cur.shape[1]) >= SUBLANES - k, pltpu.roll(next8, SUBLANES - k, 0), rolled[n - SUBLANES:])
    return jnp.concatenate([rolled[:n - SUBLANES], tail], axis=0)


def _load_prev8(ref, i, rows):
    start = pl.multiple_of(jnp.maximum(i * rows - SUBLANES, 0), SUBLANES)
    p = ref[pl.ds(start, SUBLANES), :]
    return jnp.where(i > 0, p, jnp.zeros_like(p))


def _conv_rows(ref, w_ref, b_ref, i, rows, width):
    cur = ref[pl.ds(pl.multiple_of(i * rows, rows), rows), :]
    prev8 = _load_prev8(ref, i, rows)
    shifted = [_shift_down(cur, prev8, k) for k in range(width)]
    acc = b_ref[...] + w_ref[width - 1:width, :] * shifted[0]
    for k in range(1, width):
        acc = acc + w_ref[width - 1 - k:width - k, :] * shifted[k]
    return acc, shifted


def _conv_bwd_rows(du, next8, w_ref, width):
    acc = w_ref[width - 1:width, :] * du
    for k in range(1, width):
        acc = acc + w_ref[width - 1 - k:width - k, :] * _shift_up(du, next8, k)
    return acc


def ffn_mid_fwd(u0, cw, cb):
    s, f2 = u0.shape
    f = f2 // 2
    nt = f // CONV_COLS
    rows = min(CONV_ROWS, s)

    def body(ug_ref, uu_ref, wg_ref, wu_ref, bg_ref, bu_ref, a_ref):
        def step(i, carry):
            g, _ = _conv_rows(ug_ref, wg_ref, bg_ref, i, rows, FFN_CONV)
            u, _ = _conv_rows(uu_ref, wu_ref, bu_ref, i, rows, FFN_CONV)
            a_ref[pl.ds(pl.multiple_of(i * rows, rows), rows), :] = (_silu(g) * u).astype(a_ref.dtype)
            return carry

        lax.fori_loop(0, s // rows, step, 0)

    col = lambda off: pl.BlockSpec((s, CONV_COLS), lambda j: (0, j + off))
    wsp = lambda r, off: pl.BlockSpec((r, CONV_COLS), lambda j: (0, j + off))
    return pl.pallas_call(
        body, name="ffn_mid_fwd", grid=(nt,),
        in_specs=[col(0), col(nt), wsp(FFN_CONV, 0), wsp(FFN_CONV, nt), wsp(1, 0), wsp(1, nt)],
        out_specs=pl.BlockSpec((s, CONV_COLS), lambda j: (0, j)),
        out_shape=_sds((s, f), MXU_DTYPE), compiler_params=_params(("parallel",)),
    )(u0, u0, cw, cw, cb, cb)


def ffn_mid_bwd(u0, da, cw, cb):
    s, f2 = u0.shape
    f = f2 // 2
    nt = f // CONV_COLS
    rows = min(CONV_ROWS, s)
    nsteps = s // rows
    w = FFN_CONV

    def body(ug_ref, uu_ref, da_ref, wg_ref, wu_ref, bg_ref, bu_ref,
             du0_ref, dwg_ref, dwu_ref, dbg_ref, dbu_ref):
        zero8 = jnp.zeros((SUBLANES, CONV_COLS), F32)
        zrow = jnp.zeros((1, CONV_COLS), F32)

        def step(it, carry):
            ng, nu, accs = carry
            i = nsteps - 1 - it
            r0 = pl.multiple_of(i * rows, rows)
            g, sg = _conv_rows(ug_ref, wg_ref, bg_ref, i, rows, w)
            u, su = _conv_rows(uu_ref, wu_ref, bu_ref, i, rows, w)
            dav = da_ref[pl.ds(r0, rows), :]
            sg_val, sg_grad = _silu_and_grad(g)
            dg = dav * u * sg_grad
            du = dav * sg_val
            du0_ref[0, pl.ds(r0, rows), :] = _conv_bwd_rows(dg, ng, wg_ref, w).astype(du0_ref.dtype)
            du0_ref[1, pl.ds(r0, rows), :] = _conv_bwd_rows(du, nu, wu_ref, w).astype(du0_ref.dtype)
            new = []
            for j in range(w):
                new.append(accs[j] + jnp.sum(dg * sg[w - 1 - j], axis=0, keepdims=True))
            for j in range(w):
                new.append(accs[w + j] + jnp.sum(du * su[w - 1 - j], axis=0, keepdims=True))
            new.append(accs[2 * w] + jnp.sum(dg, axis=0, keepdims=True))
            new.append(accs[2 * w + 1] + jnp.sum(du, axis=0, keepdims=True))
            return dg[0:SUBLANES], du[0:SUBLANES], tuple(new)

        _, _, accs = lax.fori_loop(0, nsteps, step, (zero8, zero8, tuple([zrow] * (2 * w + 2))))
        dwg_ref[...] = jnp.concatenate(accs[0:w], axis=0)
        dwu_ref[...] = jnp.concatenate(accs[w:2 * w], axis=0)
        dbg_ref[...] = accs[2 * w]
        dbu_ref[...] = accs[2 * w + 1]

    col = lambda off: pl.BlockSpec((s, CONV_COLS), lambda j: (0, j + off))
    wsp = lambda r, off: pl.BlockSpec((r, CONV_COLS), lambda j: (0, j + off))
    outs = pl.pallas_call(
        body, name="ffn_mid_bwd", grid=(nt,),
        in_specs=[col(0), col(nt), col(0), wsp(w, 0), wsp(w, nt), wsp(1, 0), wsp(1, nt)],
        out_specs=[pl.BlockSpec((2, s, CONV_COLS), lambda j: (0, 0, j)), wsp(w, 0), wsp(w, 0), wsp(1, 0), wsp(1, 0)],
        out_shape=[_sds((2, s, f), MXU_DTYPE), _sds((w, f), F32), _sds((w, f), F32), _sds((1, f), F32), _sds((1, f), F32)],
        compiler_params=_params(("parallel",)),
    )(u0, u0, da, cw, cw, cb, cb)
    du0, dwg, dwu, dbg, dbu = outs
    return du0, jnp.concatenate([dwg, dwu], axis=1), jnp.concatenate([dbg, dbu], axis=1)


SB_BLOCK = 128
SB_DEAD = 88.0
SB_HEADS_PER_STEP = 4


def _split_hi_lo(x):
    hi = x.astype(MXU_DTYPE)
    lo = (x - hi.astype(F32)).astype(MXU_DTYPE)
    return hi, lo


def _dot_exact01(x, tri):
    hi, lo = _split_hi_lo(x)
    return _dot(hi, tri) + _dot(lo, tri)


def _stack_heads(pair, lane_lo):
    zero = jnp.zeros_like(pair)
    return jnp.concatenate([jnp.where(lane_lo, pair, zero), jnp.where(lane_lo, zero, pair)], axis=0)


def _unstack_heads(tall, lane_lo):
    n = tall.shape[0] // 2
    return jnp.where(lane_lo, tall[:n], tall[n:])


def _sb_logits(stacked_q, k_ref, k0, pair_cols, blk):
    z = [_dot(sq, k_ref[pl.ds(k0, blk), cols], _NT) for sq, cols in zip(stacked_q, pair_cols)]
    return jnp.concatenate(z, axis=0) * (SB_HEAD_DIM ** -0.5)


def _sb_logs(z, blk, diagonal):
    t = jnp.log(1.0 + jnp.exp(-jnp.abs(z)))
    lb = jnp.minimum(z, 0.0) - t
    lf = jnp.minimum(-z, 0.0) - t
    if not diagonal:
        return lb, lf, None
    strict = lax.broadcasted_iota(jnp.int32, z.shape, 1) < (lax.broadcasted_iota(jnp.int32, z.shape, 0) & (blk - 1))
    return lb, jnp.where(strict, lf, 0.0), strict


def _keep(strict, x):
    return x if strict is None else jnp.where(strict, x, 0.0)


def _tri(blk, upper):
    r = lax.broadcasted_iota(jnp.int32, (blk, blk), 0)
    c = lax.broadcasted_iota(jnp.int32, (blk, blk), 1)
    return jnp.where((r > c) if upper else (r < c), 1.0, 0.0).astype(MXU_DTYPE)


def _tri_sum(x, tri2):
    hi, lo = _split_hi_lo(x)
    return _dot(jnp.concatenate([hi, lo], axis=1), tri2)


def sb_fwd(qkv, hook=None):
    s = qkv.shape[0]
    blk = min(SB_BLOCK, s)
    nblk = s // blk
    nh = SB_HEADS_PER_STEP
    nstep = SB_HEADS // nh
    dh = SB_HEAD_DIM

    def body(q_ref, k_ref, v_ref, o_ref):
        suffix_tri2 = jnp.concatenate([_tri(blk, True)] * 2, axis=0)
        lane_lo = lax.broadcasted_iota(jnp.int32, (1, LANES), 1) < dh
        pairs = [slice(p * LANES, (p + 1) * LANES) for p in range(nh // 2)]

        def qstep(qi, carry):
            q0 = pl.multiple_of(qi * blk, blk)
            qst = [_stack_heads(q_ref[pl.ds(q0, blk), cols], lane_lo) for cols in pairs]

            def tile(kb, run, accs, diagonal):
                k0 = pl.multiple_of(kb * blk, blk)
                lb, lf, strict = _sb_logs(_sb_logits(qst, k_ref, k0, pairs, blk), blk, diagonal)
                sloc = _tri_sum(lf, suffix_tri2)
                a = _mx(_keep(strict, jnp.exp(lb + sloc + run)))
                accs = tuple(
                    acc + _unstack_heads(_dot(a[2 * blk * p:2 * blk * (p + 1)], v_ref[pl.ds(k0, blk), cols]), lane_lo)
                    for p, (acc, cols) in enumerate(zip(accs, pairs)))
                run = run + sloc[:, 0:1] + lf[:, 0:1]
                return run, accs, jnp.max(run) > -SB_DEAD

            def kstep(st):
                it, run, accs, _ = st
                return (it + 1, *tile(qi - it, run, accs, False))

            first = tile(qi, jnp.zeros((nh * blk, 1), F32), tuple([jnp.zeros((blk, LANES), F32)] * len(pairs)), True)
            _, _, accs, _ = lax.while_loop(lambda st: jnp.logical_and(st[0] <= qi, st[3]), kstep, (jnp.int32(1), *first))
            for acc, cols in zip(accs, pairs):
                o_ref[pl.ds(q0, blk), cols] = acc.astype(o_ref.dtype)
            return carry

        lax.fori_loop(0, nblk, qstep, 0)

    col = lambda off: pl.BlockSpec((s, nh * dh), lambda p: (0, p + off))
    out, carried = hosted_call(body, hook, "sb_fwd", (nstep,), [col(0), col(nstep), col(2 * nstep)], col(0),
                               _sds((s, D_MODEL), MXU_DTYPE), [], ("parallel",), (qkv, qkv, qkv))
    return out if hook is None else (out, carried)


def sb_bwd(qkv, do, hook=None):
    s = qkv.shape[0]
    blk = min(SB_BLOCK, s)
    nblk = s // blk
    nh = SB_HEADS_PER_STEP
    nstep = SB_HEADS // nh
    dh = SB_HEAD_DIM

    def body(q_ref, k_ref, v_ref, do_ref, dq_ref, dk_ref, dv_ref, dk_acc, dv_acc, run_ref):
        suffix_tri2 = jnp.concatenate([_tri(blk, True)] * 2, axis=0)
        prefix_tri2 = jnp.concatenate([_tri(blk, False)] * 2, axis=0)
        dk_acc[...] = jnp.zeros_like(dk_acc)
        dv_acc[...] = jnp.zeros_like(dv_acc)
        lane_lo = lax.broadcasted_iota(jnp.int32, (1, LANES), 1) < dh
        pairs = [slice(p * LANES, (p + 1) * LANES) for p in range(nh // 2)]

        def qstep(qi, carry):
            q0 = pl.multiple_of(qi * blk, blk)
            qst = [_stack_heads(q_ref[pl.ds(q0, blk), cols], lane_lo) for cols in pairs]
            dost = [_stack_heads(do_ref[pl.ds(q0, blk), cols], lane_lo) for cols in pairs]

            def enter(kb, run, diagonal):
                run_ref[kb] = run
                _, lf, _ = _sb_logs(_sb_logits(qst, k_ref, pl.multiple_of(kb * blk, blk), pairs, blk), blk, diagonal)
                run = run + jnp.sum(lf, axis=1, keepdims=True)
                return run, jnp.max(run) > -SB_DEAD

            def sweep1(st):
                it, run, _ = st
                return (it + 1, *enter(qi - it, run, False))

            nlive, _, _ = lax.while_loop(lambda st: jnp.logical_and(st[0] <= qi, st[2]), sweep1,
                                         (jnp.int32(1), *enter(qi, jnp.zeros((nh * blk, 1), F32), True)))

            def tile(kb, pg, dqs, diagonal):
                k0 = pl.multiple_of(kb * blk, blk)
                lb, lf, strict = _sb_logs(_sb_logits(qst, k_ref, k0, pairs, blk), blk, diagonal)
                sloc = _tri_sum(lf, suffix_tri2)
                a = _keep(strict, jnp.exp(lb + sloc + run_ref[kb]))
                da = jnp.concatenate([_dot(d, v_ref[pl.ds(k0, blk), cols], _NT) for d, cols in zip(dost, pairs)], axis=0)
                g = da * a
                p = pg + _tri_sum(g, prefix_tri2)
                sig = jnp.exp(lb)
                dz = _mx(_keep(strict, g * (1.0 - sig) - p * sig) * (dh ** -0.5))
                am = _mx(a)
                new_dqs = []
                for i, cols in enumerate(pairs):
                    rows = slice(2 * blk * i, 2 * blk * (i + 1))
                    new_dqs.append(dqs[i] + _unstack_heads(_dot(dz[rows], k_ref[pl.ds(k0, blk), cols]), lane_lo))
                    dk_acc[pl.ds(k0, blk), cols] += _dot(dz[rows], qst[i], _TN)
                    dv_acc[pl.ds(k0, blk), cols] += _dot(am[rows], dost[i], _TN)
                return pg + jnp.sum(g, axis=1, keepdims=True), tuple(new_dqs)

            pg, dqs = lax.fori_loop(qi + 1 - nlive, qi, lambda kb, st: tile(kb, *st, False),
                                    (jnp.zeros((nh * blk, 1), F32), tuple([jnp.zeros((blk, LANES), F32)] * len(pairs))))
            _, dqs = tile(qi, pg, dqs, True)
            for dq, cols in zip(dqs, pairs):
                dq_ref[pl.ds(q0, blk), cols] = dq.astype(dq_ref.dtype)
            return carry

        lax.fori_loop(0, nblk, qstep, 0)
        dk_ref[...] = dk_acc[...].astype(dk_ref.dtype)
        dv_ref[...] = dv_acc[...].astype(dv_ref.dtype)

    col = lambda off: pl.BlockSpec((s, nh * dh), lambda p: (0, p + off))
    (dq, dk, dv), carried = hosted_call(
        body, hook, "sb_bwd", (nstep,), [col(0), col(nstep), col(2 * nstep), col(0)], [col(0), col(0), col(0)],
        [_sds((s, D_MODEL), MXU_DTYPE)] * 3,
        [pltpu.VMEM((s, nh * dh), F32), pltpu.VMEM((s, nh * dh), F32), pltpu.VMEM((nblk, nh * blk, 1), F32)],
        ("parallel",), (qkv, qkv, qkv, do))
    return jnp.concatenate([dq, dk, dv], axis=1), carried


SSD_XBC_TILE0 = SSD_D_INNER // CONV_COLS


def ssd_conv_fwd(proj, cw, cb):
    s = proj.shape[0]
    rows = min(CONV_ROWS, s)

    def body(u_ref, w_ref, b_ref, o_ref):
        def step(i, carry):
            u, _ = _conv_rows(u_ref, w_ref, b_ref, i, rows, SSD_CONV)
            o_ref[pl.ds(pl.multiple_of(i * rows, rows), rows), :] = _silu(u)
            return carry

        lax.fori_loop(0, s // rows, step, 0)

    return pl.pallas_call(
        body, name="ssd_conv_fwd", grid=(SSD_CONV_DIM // CONV_COLS,),
        in_specs=[pl.BlockSpec((s, CONV_COLS), lambda j: (0, j + SSD_XBC_TILE0)),
                  pl.BlockSpec((SSD_CONV, CONV_COLS), lambda j: (0, j)), pl.BlockSpec((1, CONV_COLS), lambda j: (0, j))],
        out_specs=pl.BlockSpec((s, CONV_COLS), lambda j: (0, j)),
        out_shape=_sds((s, SSD_CONV_DIM), F32), compiler_params=_params(("parallel",)),
    )(proj, cw, cb)


def ssd_conv_bwd(proj, dact, cw, cb):
    s = proj.shape[0]
    rows = min(CONV_ROWS, s)
    nsteps = s // rows
    w = SSD_CONV

    def body(u_ref, da_ref, w_ref, b_ref, du_ref, dw_ref, db_ref):
        def step(it, carry):
            nxt, accs = carry
            i = nsteps - 1 - it
            r0 = pl.multiple_of(i * rows, rows)
            u, sh = _conv_rows(u_ref, w_ref, b_ref, i, rows, w)
            dconv = da_ref[pl.ds(r0, rows), :] * _silu_and_grad(u)[1]
            du_ref[pl.ds(r0, rows), :] = _conv_bwd_rows(dconv, nxt, w_ref, w).astype(du_ref.dtype)
            new = [accs[j] + jnp.sum(dconv * sh[w - 1 - j], axis=0, keepdims=True) for j in range(w)]
            new.append(accs[w] + jnp.sum(dconv, axis=0, keepdims=True))
            return dconv[0:SUBLANES], tuple(new)

        zrow = jnp.zeros((1, CONV_COLS), F32)
        _, accs = lax.fori_loop(0, nsteps, step, (jnp.zeros((SUBLANES, CONV_COLS), F32), tuple([zrow] * (w + 1))))
        dw_ref[...] = jnp.concatenate(accs[0:w], axis=0)
        db_ref[...] = accs[w]

    col = pl.BlockSpec((s, CONV_COLS), lambda j: (0, j))
    return pl.pallas_call(
        body, name="ssd_conv_bwd", grid=(SSD_CONV_DIM // CONV_COLS,),
        in_specs=[pl.BlockSpec((s, CONV_COLS), lambda j: (0, j + SSD_XBC_TILE0)), col,
                  pl.BlockSpec((w, CONV_COLS), lambda j: (0, j)), pl.BlockSpec((1, CONV_COLS), lambda j: (0, j))],
        out_specs=[col, pl.BlockSpec((w, CONV_COLS), lambda j: (0, j)), pl.BlockSpec((1, CONV_COLS), lambda j: (0, j))],
        out_shape=[_sds((s, SSD_CONV_DIM), MXU_DTYPE), _sds((w, SSD_CONV_DIM), F32), _sds((1, SSD_CONV_DIM), F32)],
        compiler_params=_params(("parallel",)),
    )(proj, dact, cw, cb)


def _split3(x):
    hi = x.astype(MXU_DTYPE)
    r1 = x - hi.astype(F32)
    mid = r1.astype(MXU_DTYPE)
    lo = (r1 - mid.astype(F32)).astype(MXU_DTYPE)
    return hi, mid, lo


def _dot01(x, m, dims=_NN, left=False):
    parts = _split3(x)
    if left:
        return _dot(m, parts[0], dims) + _dot(m, parts[1], dims) + _dot(m, parts[2], dims)
    return _dot(parts[0], m, dims) + _dot(parts[1], m, dims) + _dot(parts[2], m, dims)


def _softplus(x):
    return jnp.maximum(x, 0.0) + jnp.log1p(jnp.exp(-jnp.abs(x)))


def _ssd_consts(dt_bias, a_log, d_skip):
    pad = lambda v: jnp.pad(v.reshape(1, SSD_HEADS), ((0, 0), (0, LANES - SSD_HEADS)))
    head_of = jnp.arange(SSD_D_INNER) // SSD_HEAD_DIM
    expand = (jnp.arange(LANES)[:, None] == head_of[None, :]).astype(MXU_DTYPE)
    return dict(bias_w=pad(dt_bias), alog_w=pad(a_log), bias_c=dt_bias.reshape(SSD_HEADS, 1),
                alog_c=a_log.reshape(SSD_HEADS, 1), dskip=jnp.repeat(d_skip, SSD_HEAD_DIM).reshape(1, SSD_D_INNER),
                expand=expand, reduce=expand.T)


def _expand_heads(x):
    first = lax.broadcasted_iota(jnp.int32, (1, LANES), 1) < SSD_HEAD_DIM
    shape = (x.shape[0], LANES)
    tiles = [jnp.where(first, jnp.broadcast_to(x[:, 2 * p:2 * p + 1], shape), jnp.broadcast_to(x[:, 2 * p + 1:2 * p + 2], shape))
             for p in range(SSD_HEADS // 2)]
    return jnp.concatenate(tiles, axis=1)


def _ssd_chunk_prep(dtp, dtp_t, bias_w, alog_w, bias_c, alog_c, expand):
    L = dtp.shape[0]
    r = lax.broadcasted_iota(jnp.int32, (L, L), 0)
    c = lax.broadcasted_iota(jnp.int32, (L, L), 1)
    tril = r >= c
    lower = jnp.where(tril, 1.0, 0.0).astype(MXU_DTYPE)
    upper = jnp.where(r <= c, 1.0, 0.0).astype(MXU_DTYPE)
    dt_col = _softplus(dtp + bias_w)
    a_col = -jnp.exp(alog_w) * dt_col
    a_row = -jnp.exp(alog_c) * _softplus(dtp_t + bias_c)
    acum_col = _dot01(a_col, lower, left=True)
    acum_row = _dot01(a_row, upper)
    acum_full = _expand_heads(acum_col)
    dt_full = _expand_heads(dt_col)
    return dict(tril=tril, lower=lower, upper=upper, dt_col=dt_col, a_col=a_col, acum_col=acum_col,
                acum_row=acum_row, acum_full=acum_full, dt_full=dt_full)


def _head_mask(j):
    lane = lax.broadcasted_iota(jnp.int32, (1, LANES), 1)
    return jnp.where((lane // SSD_HEAD_DIM) == j, 1.0, 0.0)


def _decay(pre, h):
    seg = pre["acum_col"][:, h:h + 1] - pre["acum_row"][h:h + 1, :]
    return jnp.exp(jnp.where(pre["tril"], seg, -1e30))


def _ssd_specs(s, nc, rev):
    L = SSD_CHUNK
    ci = (lambda i: nc - 1 - i) if rev else (lambda i: i)
    const = lambda shape: pl.BlockSpec(shape, lambda i: (0,) * len(shape))
    return dict(
        xbc=pl.BlockSpec((L, SSD_CONV_DIM), lambda i: (ci(i), 0)),
        dtp=pl.BlockSpec((L, LANES), lambda i: (ci(i), SSD_IN_PAD // LANES - 1)),
        dtp_t=pl.BlockSpec((SSD_HEADS, L), lambda i: (0, ci(i))),
        rows=pl.BlockSpec((L, SSD_D_INNER), lambda i: (ci(i), 0)),
        state=pl.BlockSpec((1, SSD_GROUPS, SSD_STATE, 4 * SSD_HEAD_DIM), lambda i: (ci(i), 0, 0, 0)),
        consts=[const((1, LANES)), const((1, LANES)), const((SSD_HEADS, 1)), const((SSD_HEADS, 1)),
                const((1, SSD_D_INNER)), const((LANES, SSD_D_INNER)), const((SSD_D_INNER, LANES))],
    )


def _const_args(cs):
    return [cs["bias_w"], cs["alog_w"], cs["bias_c"], cs["alog_c"], cs["dskip"], cs["expand"], cs["reduce"]]


def ssd_scan_fwd(act, proj, dtp_t, cs, hook=None):
    s = act.shape[0]
    L = SSD_CHUNK
    nc = s // L
    G, N, GW = SSD_GROUPS, SSD_STATE, 4 * SSD_HEAD_DIM

    def body(act_ref, dtp_ref, dtpt_ref, bw_ref, aw_ref, bc_ref, ac_ref, dsk_ref, ex_ref, rd_ref, y_ref, st_out, st):
        @pl.when(pl.program_id(0) == 0)
        def _():
            st[...] = jnp.zeros_like(st)

        st_out[0] = st[...]
        pre = _ssd_chunk_prep(dtp_ref[...], dtpt_ref[...], bw_ref[...], aw_ref[...], bc_ref[...], ac_ref[...], ex_ref[...])
        acum_full = pre["acum_full"]
        last_full = acum_full[L - 1:L, :]
        for g in range(G):
            bg = _mx(act_ref[:, SSD_D_INNER + g * N:SSD_D_INNER + (g + 1) * N])
            cg = _mx(act_ref[:, SSD_D_INNER + G * N + g * N:SSD_D_INNER + G * N + (g + 1) * N])
            cb = _dot(cg, bg, _NT)
            for half in range(2):
                p = 2 * g + half
                cols = slice(p * LANES, (p + 1) * LANES)
                xs = act_ref[:, cols]
                xdt = xs * pre["dt_full"][:, cols]
                yd = jnp.zeros((L, LANES), F32)
                for j in range(2):
                    m = cb * _decay(pre, 2 * p + j)
                    yd = yd + _dot(_mx(m), _mx(xdt * _head_mask(j)))
                yoff = _dot(cg, _mx(st[g, :, half * LANES:(half + 1) * LANES])) * jnp.exp(acum_full[:, cols])
                y_ref[:, cols] = yd + yoff + dsk_ref[:, cols] * xs
                w = jnp.exp(last_full[:, cols] - acum_full[:, cols])
                st[g, :, half * LANES:(half + 1) * LANES] = (
                    st[g, :, half * LANES:(half + 1) * LANES] * jnp.exp(last_full[:, cols]) + _dot(bg, _mx(xdt * w), _TN))

    sp = _ssd_specs(s, nc, False)
    (y, states), carried = hosted_call(
        body, hook, "ssd_scan_fwd", (nc,), [sp["xbc"], sp["dtp"], sp["dtp_t"]] + sp["consts"],
        [sp["rows"], sp["state"]], [_sds((s, SSD_D_INNER), F32), _sds((nc, G, N, GW), F32)],
        [pltpu.VMEM((G, N, GW), F32)], ("arbitrary",), (act, proj, dtp_t, *_const_args(cs)))
    return y, states, carried


def ssd_scan_bwd(act, proj, dtp_t, cs, states, dy, hook=None):
    s = act.shape[0]
    L = SSD_CHUNK
    nc = s // L
    G, N, GW = SSD_GROUPS, SSD_STATE, 4 * SSD_HEAD_DIM

    def body(act_ref, dtp_ref, dtpt_ref, bw_ref, aw_ref, bc_ref, ac_ref, dsk_ref, ex_ref, rd_ref, st_ref, dy_ref,
             dact_ref, ddtp_ref, dalog_ref, dbias_ref, dskip_ref, dst, dxdt_ref, dac_ref):
        first = pl.program_id(0) == 0

        @pl.when(first)
        def _():
            dst[...] = jnp.zeros_like(dst)
            dalog_ref[...] = jnp.zeros_like(dalog_ref)
            dbias_ref[...] = jnp.zeros_like(dbias_ref)
            dskip_ref[...] = jnp.zeros_like(dskip_ref)

        expand, reduce = ex_ref[...], rd_ref[...]
        pre = _ssd_chunk_prep(dtp_ref[...], dtpt_ref[...], bw_ref[...], aw_ref[...], bc_ref[...], ac_ref[...], expand)
        acum_full = pre["acum_full"]
        last_full = acum_full[L - 1:L, :]
        ones = jnp.ones((2 * L, LANES), MXU_DTYPE)
        lane = lax.broadcasted_iota(jnp.int32, (L, LANES), 1)
        dacum_diag = jnp.zeros((L, LANES), F32)
        dlast_parts = []
        for g in range(G):
            bg = _mx(act_ref[:, SSD_D_INNER + g * N:SSD_D_INNER + (g + 1) * N])
            cg = _mx(act_ref[:, SSD_D_INNER + G * N + g * N:SSD_D_INNER + G * N + (g + 1) * N])
            cb = _dot(cg, bg, _NT)
            dcb = jnp.zeros((L, L), F32)
            dcg = jnp.zeros((L, N), F32)
            dbg = jnp.zeros((L, N), F32)
            for half in range(2):
                p = 2 * g + half
                cols = slice(p * LANES, (p + 1) * LANES)
                hcols = slice(half * LANES, (half + 1) * LANES)
                xs = act_ref[:, cols]
                xdt = xs * pre["dt_full"][:, cols]
                dyv = dy_ref[:, cols]
                dxdt = jnp.zeros((L, LANES), F32)
                parts = []
                for j in range(2):
                    h = 2 * p + j
                    dec = _decay(pre, h)
                    m = cb * dec
                    dyh = _mx(dyv * _head_mask(j))
                    dm = _dot(dyh, _mx(xdt), _NT)
                    dxdt = dxdt + _dot(_mx(m), dyh, _TN)
                    parts.append(_split_hi_lo(dm * m))
                    dcb = dcb + dm * dec
                (ahi, alo), (bhi, blo) = parts
                rows = _dot(jnp.concatenate([jnp.concatenate([ahi, alo], axis=1), jnp.concatenate([bhi, blo], axis=1)], axis=0), ones)
                cols_ = _dot(jnp.concatenate([jnp.concatenate([ahi, bhi], axis=1), jnp.concatenate([alo, blo], axis=1)], axis=0),
                             ones, _TN)
                d_pair = rows - cols_
                dacum_diag = jnp.where(lane == 2 * p, d_pair[:L], jnp.where(lane == 2 * p + 1, d_pair[L:], dacum_diag))
                lam = jnp.exp(acum_full[:, cols])
                stv = _mx(st_ref[0, g, :, hcols])
                z = _dot(cg, stv)
                dz = _mx(lam * dyv)
                dcg = dcg + _dot(dz, stv, _NT)
                dst_in = _dot(cg, dz, _TN)
                dsv = dst[g, :, hcols]
                w = jnp.exp(last_full[:, cols] - acum_full[:, cols])
                q = _dot(bg, _mx(dsv))
                wq = w * q
                dxdt = dxdt + wq
                wqx = wq * xdt
                dbg = dbg + _dot(_mx(xdt * w), _mx(dsv), _NT)
                elast = jnp.exp(last_full[:, cols])
                dlast_p = jnp.sum(wqx, axis=0, keepdims=True) + elast * jnp.sum(dsv * st_ref[0, g, :, hcols], axis=0, keepdims=True)
                dac_ref[:, cols] = dyv * z * lam - wqx
                dlast_parts.append(dlast_p)
                dst[g, :, hcols] = dst_in + dsv * elast
                dxdt_ref[:, cols] = dxdt
                dact_ref[:, cols] = dxdt * pre["dt_full"][:, cols] + dsk_ref[:, cols] * dyv
            dcbm = _mx(dcb)
            dact_ref[:, SSD_D_INNER + g * N:SSD_D_INNER + (g + 1) * N] = dbg + _dot(dcbm, cg, _TN)
            dact_ref[:, SSD_D_INNER + G * N + g * N:SSD_D_INNER + G * N + (g + 1) * N] = dcg + _dot(dcbm, bg)

        xs_all = act_ref[:, 0:SSD_D_INNER]
        dacum = dacum_diag + _dot_exact01(dac_ref[...], reduce)
        dlast = _dot_exact01(jnp.concatenate(dlast_parts, axis=1), reduce)
        row = lax.broadcasted_iota(jnp.int32, (L, LANES), 0)
        dacum = dacum + jnp.where(row == L - 1, dlast, 0.0)
        da_col = _dot01(dacum, pre["upper"], left=True)
        a_w = -jnp.exp(aw_ref[...])
        ddt = a_w * da_col + _dot_exact01(dxdt_ref[...] * xs_all, reduce)
        xin = dtp_ref[...] + bw_ref[...]
        ddtp = ddt * (1.0 / (1.0 + jnp.exp(-xin)))
        valid = lane < SSD_HEADS
        ddtp = jnp.where(valid, ddtp, 0.0)
        ddtp_ref[...] = ddtp
        dbias_ref[...] += jnp.sum(ddtp, axis=0, keepdims=True)
        dalog_ref[...] += jnp.sum(jnp.where(valid, da_col * pre["a_col"], 0.0), axis=0, keepdims=True)
        dskip_ref[...] += jnp.sum(_dot_exact01(dy_ref[...] * xs_all, reduce), axis=0, keepdims=True)

    sp = _ssd_specs(s, nc, True)
    acc = pl.BlockSpec((1, LANES), lambda i: (0, 0))
    outs, carried = hosted_call(
        body, hook, "ssd_scan_bwd", (nc,),
        [sp["xbc"], sp["dtp"], sp["dtp_t"]] + sp["consts"] + [sp["state"], sp["rows"]],
        [sp["xbc"], pl.BlockSpec((L, LANES), lambda i: (nc - 1 - i, 0)), acc, acc, acc],
        [_sds((s, SSD_CONV_DIM), F32), _sds((s, LANES), F32)] + [_sds((1, LANES), F32)] * 3,
        [pltpu.VMEM((G, N, GW), F32), pltpu.VMEM((L, SSD_D_INNER), F32), pltpu.VMEM((L, SSD_D_INNER), F32)],
        ("arbitrary",), (act, proj, dtp_t, *_const_args(cs), states, dy))
    return (*outs, carried)


def ssd_post_fwd(y, proj, g):
    s, d = y.shape
    ts = _pick(s, (256, 128))

    def body(y_ref, z_ref, g_ref, o_ref):
        y2 = y_ref[...] * _silu(z_ref[...])
        r = lax.rsqrt(jnp.mean(y2 * y2, axis=-1, keepdims=True) + NORM_EPS)
        o_ref[...] = (y2 * r * g_ref[...]).astype(o_ref.dtype)

    row = pl.BlockSpec((ts, d), lambda i: (i, 0))
    return pl.pallas_call(
        body, name="ssd_post_fwd", grid=(s // ts,), in_specs=[row, row, pl.BlockSpec((1, d), lambda i: (0, 0))],
        out_specs=row, out_shape=_sds((s, d), MXU_DTYPE), compiler_params=_params(("parallel",)),
    )(y, proj, g)


def ssd_post_bwd(y, proj, g, dy3):
    s, d = y.shape
    ts = _pick(s, (256, 128))

    def body(y_ref, z_ref, g_ref, d3_ref, dy_ref, dz_ref, dg_ref):
        yv, zv = y_ref[...], z_ref[...]
        sz, sgrad = _silu_and_grad(zv)
        y2 = yv * sz
        r = lax.rsqrt(jnp.mean(y2 * y2, axis=-1, keepdims=True) + NORM_EPS)
        xh = y2 * r
        d3 = d3_ref[...]
        dxh = d3 * g_ref[...]
        dy2 = r * (dxh - xh * jnp.mean(dxh * xh, axis=-1, keepdims=True))
        dy_ref[...] = dy2 * sz
        dz_ref[...] = (dy2 * yv * sgrad).astype(dz_ref.dtype)
        part = jnp.sum(d3 * xh, axis=0, keepdims=True)

        @pl.when(pl.program_id(0) == 0)
        def _():
            dg_ref[...] = part

        @pl.when(pl.program_id(0) != 0)
        def _():
            dg_ref[...] += part

    row = pl.BlockSpec((ts, d), lambda i: (i, 0))
    vec = pl.BlockSpec((1, d), lambda i: (0, 0))
    return pl.pallas_call(
        body, name="ssd_post_bwd", grid=(s // ts,), in_specs=[row, row, vec, row], out_specs=[row, row, vec],
        out_shape=[_sds((s, d), F32), _sds((s, d), MXU_DTYPE), _sds((1, d), F32)],
        compiler_params=_params(("arbitrary",)),
    )(y, proj, g, dy3)


def dt_transpose(proj):
    s = proj.shape[0]
    ts = _pick(s, (512, 256, 128))

    def body(p_ref, o_ref):
        o_ref[...] = p_ref[...].T

    return pl.pallas_call(
        body, name="dt_transpose", grid=(s // ts,),
        in_specs=[pl.BlockSpec((ts, LANES), lambda i: (i, SSD_IN_PAD // LANES - 1))],
        out_specs=pl.BlockSpec((LANES, ts), lambda i: (0, i)), out_shape=_sds((LANES, s), F32),
        compiler_params=_params(("parallel",)),
    )(proj)


def ssd_core_fwd(proj, cw, cb, dt_bias, a_log, d_skip, norm_g, hook=None):
    cs = _ssd_consts(dt_bias, a_log, d_skip)
    act = ssd_conv_fwd(proj, cw, cb)
    dtp_t = dt_transpose(proj)
    y, states, carried = ssd_scan_fwd(act, proj, dtp_t, cs, hook)
    y3 = ssd_post_fwd(y, proj, norm_g)
    return y3, (cs, act, dtp_t, y, states), carried


def ssd_core_bwd(proj, cw, cb, norm_g, saved, dy3, hook=None):
    cs, act, dtp_t, y, states = saved
    dy, dz, dnorm = ssd_post_bwd(y, proj, norm_g, dy3)
    dact, ddtp, dalog, dbias, dskip, carried = ssd_scan_bwd(act, proj, dtp_t, cs, states, dy, hook)
    dxbc, dcw, dcb = ssd_conv_bwd(proj, dact, cw, cb)
    dproj = jnp.concatenate([dz, dxbc, ddtp.astype(MXU_DTYPE)], axis=1)
    h = SSD_HEADS
    return dproj, dcw, dcb, dbias[0, :h], dalog[0, :h], dskip[0, :h], dnorm, carried


def ssd_core(proj, cw, cb, dt_bias, a_log, d_skip, norm_g, dy3):
    y3, saved, _ = ssd_core_fwd(proj, cw, cb, dt_bias, a_log, d_skip, norm_g)
    return y3, ssd_core_bwd(proj, cw, cb, norm_g, saved, dy3)


def loss_head(x, g, target):
    s, d = x.shape
    ts = _pick(s, (512, 256, 128))

    def body(x_ref, g_ref, t_ref, loss_ref, dx_ref, dxm_ref, dg_ref):
        xv = x_ref[...]
        r = lax.rsqrt(jnp.mean(xv * xv, axis=-1, keepdims=True) + NORM_EPS)
        xh = xv * r
        err = xh * g_ref[...] - t_ref[...]
        dy = err * (1.0 / d)
        dxh = dy * g_ref[...]
        dx = r * (dxh - xh * jnp.mean(dxh * xh, axis=-1, keepdims=True))
        dx_ref[...] = dx
        dxm_ref[...] = dx.astype(dxm_ref.dtype)
        part = jnp.sum(dy * xh, axis=0, keepdims=True)
        lpart = jnp.full((1, LANES), 0.5 * jnp.sum(jnp.mean(err * err, axis=-1, keepdims=True)), F32)

        @pl.when(pl.program_id(0) == 0)
        def _():
            dg_ref[...] = part
            loss_ref[...] = lpart

        @pl.when(pl.program_id(0) != 0)
        def _():
            dg_ref[...] += part
            loss_ref[...] += lpart

    row = pl.BlockSpec((ts, d), lambda i: (i, 0))
    vec = pl.BlockSpec((1, d), lambda i: (0, 0))
    return pl.pallas_call(
        body, name="loss_head", grid=(s // ts,), in_specs=[row, vec, row],
        out_specs=[pl.BlockSpec((1, LANES), lambda i: (0, 0)), row, row, vec],
        out_shape=[_sds((1, LANES), F32), _sds((s, d), F32), _sds((s, d), MXU_DTYPE), _sds((1, d), F32)],
        compiler_params=_params(("arbitrary",)),
    )(x, g, target)


def _adamw_math(w, g, m, v):
    m = ADAM_B1 * m + (1.0 - ADAM_B1) * g
    v = ADAM_B2 * v + (1.0 - ADAM_B2) * (g * g)
    m_hat = m / (1.0 - ADAM_B1 ** ADAM_STEP)
    v_hat = v / (1.0 - ADAM_B2 ** ADAM_STEP)
    return -ADAM_LR * (m_hat / (jnp.sqrt(v_hat) + ADAM_EPS) + ADAM_WD * w), m, v


def adamw(w, g, m, v, name="adamw"):
    r, c = w.shape
    tr = _pick(r, (256, 128, 64, 32, 16, 8))

    def body(w_ref, g_ref, m_ref, v_ref, d_ref, nm_ref, nv_ref):
        d_ref[...], nm_ref[...], nv_ref[...] = _adamw_math(w_ref[...], g_ref[...], m_ref[...], v_ref[...])

    blk = pl.BlockSpec((tr, c), lambda i: (i, 0))
    return pl.pallas_call(
        body, name=name, grid=(r // tr,), in_specs=[blk] * 4, out_specs=[blk] * 3,
        out_shape=[_sds((r, c), F32)] * 3, compiler_params=_params(("parallel",)),
    )(w, g, m, v)


def adamw_small(w, parts, m, v):
    n, r, c = parts.shape

    def body(w_ref, p_ref, m_ref, v_ref, g_ref, d_ref, nm_ref, nv_ref):
        g = p_ref[0]
        for k in range(1, n):
            g = g + p_ref[k]
        g_ref[...] = g
        d_ref[...], nm_ref[...], nv_ref[...] = _adamw_math(w_ref[...], g, m_ref[...], v_ref[...])

    return pl.pallas_call(
        body, name="adamw_small", out_shape=[_sds((r, c), F32)] * 4, compiler_params=_params(),
    )(w, parts, m, v)


def pair_sum(unit, recv, where):
    nchip, _, r, c = unit.shape
    tr = _pick(r, (512, 256, 176, 128, 64, 32, 16))

    def body(w_ref, a_ref, b_ref, o_ref, ob_ref):
        sm = a_ref[0, 0] + b_ref[0]
        ob_ref[0] = sm.astype(ob_ref.dtype)

        @pl.when(pl.program_id(1) == w_ref[0])
        def _():
            o_ref[...] = sm

    blk = pl.BlockSpec((1, tr, c), lambda i, s, w: (s, i, 0))
    return pl.pallas_call(
        body, name="pair_sum",
        grid_spec=pltpu.PrefetchScalarGridSpec(
            num_scalar_prefetch=1, grid=(r // tr, nchip),
            in_specs=[pl.BlockSpec((1, 1, tr, c), lambda i, s, w: (s, w[1], i, 0)), blk],
            out_specs=[pl.BlockSpec((tr, c), lambda i, s, w: (i, 0)), blk]),
        out_shape=[_sds((r, c), F32), _sds((nchip, r, c), jnp.bfloat16)],
        compiler_params=_params(("parallel", "arbitrary")),
    )(where, unit, recv)


def chip_sum(own, where, recv, layer, layers, prev=None):
    r, c = own.shape
    tr = _pick(r, (512, 256, 176, 128, 64, 32, 16))

    def body(s_ref, a_ref, b_ref, *rest):
        rest[-1][...] = a_ref[...] + b_ref[0].astype(F32) + b_ref[1].astype(F32) + b_ref[2].astype(F32)

    in_specs = [pl.BlockSpec((tr, c), lambda i, s: (i, 0)), pl.BlockSpec((3, tr, c), lambda i, s: (0, i, 0))]
    args = [where, own, recv]
    if prev is not None:
        in_specs.append(ANY)
        args.append(prev)
    return pl.pallas_call(
        body, name="chip_sum",
        grid_spec=pltpu.PrefetchScalarGridSpec(
            num_scalar_prefetch=1, grid=(r // tr,), in_specs=in_specs,
            out_specs=pl.BlockSpec((None, None, tr, c), lambda i, s: (layer, s[1], i, 0))),
        out_shape=_sds((layers, 2, r, c), F32), input_output_aliases={} if prev is None else {3: 0},
        compiler_params=_params(("parallel",)),
    )(*args)


def place_cast(w, layer, chip):
    _, a, b = w.shape
    ta = _pick(a, (512, 352, 256, 128))

    def body(c_ref, w_ref, o_ref):
        o_ref[...] = w_ref[...].astype(o_ref.dtype)

    return pl.pallas_call(
        body, name="place_cast",
        grid_spec=pltpu.PrefetchScalarGridSpec(
            num_scalar_prefetch=1, grid=(a // ta,),
            in_specs=[pl.BlockSpec((None, ta, b), lambda i, c: (layer, i, 0))],
            out_specs=pl.BlockSpec((None, ta, b), lambda i, c: (c[0], i, 0))),
        out_shape=_sds((N_CHIPS, a, b), MXU_DTYPE), compiler_params=_params(("parallel",)),
    )(chip, w)


ANY = pl.BlockSpec(memory_space=pl.ANY)
COMM = pltpu.CompilerParams(has_side_effects=True)


def _coords():
    return lax.axis_index("x"), lax.axis_index("y"), lax.axis_index("c")


def _other_chips(x, y):
    return [(1 - x, y), (x, 1 - y), (1 - x, 1 - y)]


def all_gather_8(halves, name):
    _, r, c = halves.shape

    def body(h_ref, out_ref, send_sems, recv_sems, local_sem):
        x, y, cc = _coords()
        _gather_one(h_ref.at[cc], lambda px, py, pc: out_ref.at[4 * px + 2 * py + pc],
                    lambda k: send_sems.at[k], lambda k: recv_sems.at[k], local_sem)

    return pl.pallas_call(
        body, name=name, in_specs=[ANY], out_specs=ANY, out_shape=_sds((8, r, c), halves.dtype),
        scratch_shapes=[pltpu.SemaphoreType.DMA((7,)), pltpu.SemaphoreType.DMA((7,)), pltpu.SemaphoreType.DMA],
        compiler_params=COMM,
    )(halves)


def _gather_plan(x_ref, slot, send_sem, recv_sem, local_sem):
    x, y, cc = _coords()
    me, sibling = (x, y, cc), (x, y, 1 - cc)
    chips = _other_chips(x, y)

    def copy(k, blk, to, src=None):
        return pltpu.make_async_remote_copy(
            src_ref=slot(*blk) if src is None else src, dst_ref=slot(*blk),
            send_sem=send_sem(k), recv_sem=recv_sem(k), device_id=to, device_id_type=MESH)

    mine = pltpu.make_async_copy(x_ref, slot(*me), local_sem)
    first = [copy(0, me, sibling, src=x_ref)] + [copy(1 + j, me, (*chip, cc), src=x_ref) for j, chip in enumerate(chips)]
    passed = [copy(4 + j, (*chip, cc), sibling) for j, chip in enumerate(chips)]
    over_ici = [copy(1 + j, (*chip, cc), me) for j, chip in enumerate(chips)]
    from_sibling = [copy(0, sibling, me)] + [copy(4 + j, (*chip, 1 - cc), me) for j, chip in enumerate(chips)]
    return mine, first, passed, over_ici, from_sibling


def _gather_run(plans):
    for mine, first, _, _, _ in plans:
        mine.start()
        for cp in first:
            cp.start()
    for j in range(3):
        for _, _, passed, over_ici, _ in plans:
            over_ici[j].wait_recv()
            passed[j].start()
    for mine, first, passed, _, from_sibling in plans:
        for cp in from_sibling:
            cp.wait_recv()
        for cp in first + passed:
            cp.wait_send()
        mine.wait()


def _gather_one(x_ref, slot, send_sem, recv_sem, local_sem):
    _gather_run([_gather_plan(x_ref, slot, send_sem, recv_sem, local_sem)])


def gather_hook(items):
    n = len(items)

    def plan(refs, send_sems, recv_sems):
        x, y, cc = _coords()

        def copy(i, k, px, py, pc, to):
            blk = refs[i].at[2 * px + py, pc]
            return pltpu.make_async_remote_copy(src_ref=blk, dst_ref=blk, send_sem=send_sems.at[i, k],
                                                recv_sem=recv_sems.at[i, k], device_id=to, device_id_type=MESH)

        chips = _other_chips(x, y)
        first = [copy(i, j, x, y, cc, (*chip, cc)) for i in range(n) for j, chip in enumerate(chips)]
        return copy, chips, first, (x, y, cc)

    def start(refs, new, sems):
        for cp in plan(refs, *sems)[2]:
            cp.start()

    def finish(refs, new, sems):
        copy, chips, first, (x, y, cc) = plan(refs, *sems)
        passed = []
        for j, chip in enumerate(chips):
            for i in range(n):
                copy(i, j, *chip, cc, (x, y, cc)).wait_recv()
                passed.append(copy(i, 3 + j, *chip, cc, (x, y, 1 - cc)))
                passed[-1].start()
        for j, chip in enumerate(chips):
            for i in range(n):
                copy(i, 3 + j, *chip, 1 - cc, (x, y, cc)).wait_recv()
        for cp in first + passed:
            cp.wait_send()

    return dict(arrays=list(items), new=[], start=start, finish=finish, in_place=True,
                sems=[pltpu.SemaphoreType.DMA((n, 6)), pltpu.SemaphoreType.DMA((n, 6))])


def hosted_call(body, hook, name, grid, in_specs, out_specs, out_shape, scratch_shapes, sem, args):
    single = not isinstance(out_shape, (list, tuple))
    out_specs_l = [out_specs] if single else list(out_specs)
    out_shape_l = [out_shape] if single else list(out_shape)
    if hook is None:
        res = pl.pallas_call(body, name=name, grid=grid, in_specs=list(in_specs), out_specs=out_specs, out_shape=out_shape,
                             scratch_shapes=list(scratch_shapes), compiler_params=_params(sem))(*args)
        return res, []
    items, new = hook["arrays"], hook["new"]
    k, kn, n_in, n_out, n_scr = len(items), len(new), len(in_specs), len(out_specs_l), len(scratch_shapes)
    ka = k if hook["in_place"] else 0

    def full(*refs):
        ins = refs[:n_in]
        base = n_in + k
        outs = refs[base:base + n_out]
        hrefs = refs[base + n_out:base + n_out + ka] if ka else refs[n_in:base]
        nrefs = refs[base + n_out + ka:base + n_out + ka + kn]
        scr = refs[base + n_out + ka + kn:base + n_out + ka + kn + n_scr]
        sems = refs[base + n_out + ka + kn + n_scr:]
        ids = [pl.program_id(d) for d in range(len(grid))]
        first = functools.reduce(jnp.logical_and, [i == 0 for i in ids])
        last = functools.reduce(jnp.logical_and, [i == g - 1 for i, g in zip(ids, grid)])

        @pl.when(first)
        def _():
            hook["start"](hrefs, nrefs, sems)

        body(*ins, *outs, *scr)

        @pl.when(last)
        def _():
            hook["finish"](hrefs, nrefs, sems)

    res = pl.pallas_call(
        full, name=name, grid=grid, in_specs=list(in_specs) + [ANY] * k, out_specs=out_specs_l + [ANY] * (ka + kn),
        out_shape=out_shape_l + [_sds(a.shape, a.dtype) for a in items[:ka]] + list(new),
        input_output_aliases={n_in + i: n_out + i for i in range(ka)},
        scratch_shapes=list(scratch_shapes) + hook["sems"],
        compiler_params=pltpu.CompilerParams(dimension_semantics=("arbitrary",) * len(grid),
                                             vmem_limit_bytes=VMEM_LIMIT, has_side_effects=True),
    )(*args, *items)
    return (res[0] if single else list(res[:n_out])), list(res[n_out:])


def comm_call(hook, name):
    k, kn = len(hook["arrays"]), len(hook["new"])
    ka = k if hook["in_place"] else 0

    def body(*refs):
        hrefs = refs[k:k + ka] if ka else refs[:k]
        hook["start"](hrefs, refs[k + ka:k + ka + kn], refs[k + ka + kn:])
        hook["finish"](hrefs, refs[k + ka:k + ka + kn], refs[k + ka + kn:])

    return list(pl.pallas_call(
        body, name=name, in_specs=[ANY] * k, out_specs=[ANY] * (ka + kn),
        out_shape=[_sds(a.shape, a.dtype) for a in hook["arrays"][:ka]] + list(hook["new"]),
        input_output_aliases={i: i for i in range(ka)}, scratch_shapes=hook["sems"], compiler_params=COMM,
    )(*hook["arrays"]))


def merge_hooks(hooks):
    hooks = [h for h in hooks if h is not None]
    if len(hooks) < 2:
        return hooks[0] if hooks else None

    def parts(refs, new, sems):
        out, a, b, c = [], 0, 0, 0
        for h in hooks:
            na, nn, ns = len(h["arrays"]), len(h["new"]), len(h["sems"])
            out.append((refs[a:a + na], new[b:b + nn], sems[c:c + ns]))
            a, b, c = a + na, b + nn, c + ns
        return out

    def start(refs, new, sems):
        for h, p in zip(hooks, parts(refs, new, sems)):
            h["start"](*p)

    def finish(refs, new, sems):
        for h, p in zip(hooks, parts(refs, new, sems)):
            h["finish"](*p)

    assert len({h["in_place"] for h in hooks}) == 1
    return dict(arrays=[a for h in hooks for a in h["arrays"]], new=[a for h in hooks for a in h["new"]],
                sems=[a for h in hooks for a in h["sems"]], start=start, finish=finish, in_place=hooks[0]["in_place"])


def split_carried(hooks, carried):
    hooks = [h for h in hooks if h is not None]
    off = sum(len(h["arrays"]) for h in hooks if h["in_place"])
    out = []
    for h in hooks:
        out.append(carried[off:off + len(h["new"])])
        off += len(h["new"])
    return out


def pair_swap_hook(units):
    n = len(units)

    def plan(refs, new, send_sems, recv_sems):
        x, y, cc = _coords()
        return [pltpu.make_async_remote_copy(src_ref=refs[i].at[:, 1 - cc], dst_ref=new[i], send_sem=send_sems.at[i],
                                             recv_sem=recv_sems.at[i], device_id=(x, y, 1 - cc), device_id_type=MESH)
                for i in range(n)]

    def start(refs, new, sems):
        for cp in plan(refs, new, *sems):
            cp.start()

    def finish(refs, new, sems):
        for cp in plan(refs, new, *sems):
            cp.wait()

    return dict(arrays=list(units), new=[_sds((u.shape[0],) + u.shape[2:], u.dtype) for u in units], start=start,
                finish=finish, in_place=False, sems=[pltpu.SemaphoreType.DMA((n,)), pltpu.SemaphoreType.DMA((n,))])


def chip_exchange_hook(units):
    n = len(units)

    def plan(refs, new, send_sems, recv_sems):
        x, y, cc = _coords()
        return [pltpu.make_async_remote_copy(
            src_ref=refs[i].at[2 * px + py], dst_ref=new[i].at[k], send_sem=send_sems.at[i, k],
            recv_sem=recv_sems.at[i, k], device_id=(px, py, cc), device_id_type=MESH)
            for i in range(n) for k, (px, py) in enumerate(_other_chips(x, y))]

    def start(refs, new, sems):
        for cp in plan(refs, new, *sems):
            cp.start()

    def finish(refs, new, sems):
        for cp in plan(refs, new, *sems):
            cp.wait()

    return dict(arrays=list(units), new=[_sds((3,) + u.shape[1:], u.dtype) for u in units], start=start,
                finish=finish, in_place=False, sems=[pltpu.SemaphoreType.DMA((n, 3)), pltpu.SemaphoreType.DMA((n, 3))])


def grad_half_swap(grads):
    n = len(grads)

    def body(*refs):
        outs, send_sems, recv_sems = refs[n:2 * n], refs[2 * n], refs[2 * n + 1]
        x, y, cc = _coords()
        cps = [pltpu.make_async_remote_copy(
            src_ref=outs[i].at[:, cc], dst_ref=outs[i].at[:, cc], send_sem=send_sems.at[i], recv_sem=recv_sems.at[i],
            device_id=(x, y, 1 - cc), device_id_type=MESH) for i in range(n)]
        for cp in cps:
            cp.start()
        for i, cp in enumerate(cps):
            cp.wait_send()
            pltpu.make_async_remote_copy(
                src_ref=outs[i].at[:, 1 - cc], dst_ref=outs[i].at[:, 1 - cc], send_sem=send_sems.at[i],
                recv_sem=recv_sems.at[i], device_id=(x, y, 1 - cc), device_id_type=MESH).wait_recv()

    return pl.pallas_call(
        body, name="grad_half_swap", in_specs=[ANY] * n, out_specs=[ANY] * n,
        out_shape=[_sds(g.shape, g.dtype) for g in grads], input_output_aliases={i: i for i in range(n)},
        scratch_shapes=[pltpu.SemaphoreType.DMA((n,)), pltpu.SemaphoreType.DMA((n,))], compiler_params=COMM,
    )(*grads)


N_CHIPS = 4
PACK_COLS = 1024
BIG = ("ssd_w_in", "ssd_w_out", "sb_w_qkv", "sb_w_out", "ffn_w_in", "ffn_w_out")
CONVW = ("ssd_conv_w", "ffn_conv_w")
COL_SHARDED = ("ssd_w_in", "sb_w_qkv", "ffn_w_in", "ssd_conv_w", "ffn_conv_w")
SMALL = ("mix_norm", "ffn_norm", "final_norm", "ssd_conv_b", "ssd_dt_bias", "ssd_a_log", "ssd_d", "ssd_norm", "ffn_conv_b")
WEIGHTS = ("mix_norm", "ffn_norm", "final_norm", "ssd_w_in", "ssd_conv_w", "ssd_conv_b", "ssd_dt_bias", "ssd_a_log",
           "ssd_d", "ssd_norm", "ssd_w_out", "sb_w_qkv", "sb_w_out", "ffn_w_in", "ffn_conv_w", "ffn_conv_b", "ffn_w_out")


def _to_rows(flat, multiple):
    rows = -(-flat.shape[-1] // PACK_COLS)
    rows = -(-rows // multiple) * multiple
    pad = rows * PACK_COLS - flat.shape[-1]
    return jnp.pad(flat, [(0, pad)]).reshape(rows, PACK_COLS)


def _unshard(name, stacked):
    l, n, a, b = stacked.shape
    if name in COL_SHARDED:
        return jnp.transpose(stacked, (0, 2, 1, 3)).reshape(l, a, n * b)
    return stacked.reshape(l, n * a, b)


def _gather_conv_weights(w):
    flat = jnp.concatenate([w[n].reshape(-1) for n in CONVW])
    rows = _to_rows(flat, 16)
    got = all_gather_8(rows.reshape(2, rows.shape[0] // 2, PACK_COLS), "gather_conv_weights").reshape(N_CHIPS, -1)
    out, off = {}, 0
    for n in CONVW:
        l, a, b = w[n].shape
        out[n] = _unshard(n, jnp.moveaxis(got[:, off:off + w[n].size].reshape(N_CHIPS, l, a, b), 0, 1))
        off += w[n].size
    return out


def _finish_big_grads(pair_sums, from_chips, layout):
    cc = lax.axis_index("c").astype(jnp.int32)
    chip = (2 * lax.axis_index("x") + lax.axis_index("y")).astype(jnp.int32)
    where = jnp.stack([chip, cc])
    nlayers = [1 + max(l for k, l in layout if k == wi) for wi in range(1 + max(k for k, _ in layout))]
    grads = [None] * len(nlayers)
    for (wi, l), p, r in zip(layout, pair_sums, from_chips):
        grads[wi] = chip_sum(p, where, r, l, nlayers[wi], grads[wi])
    return grad_half_swap(grads)


def kernel(x, mix_norm, ffn_norm, final_norm, ssd_w_in, ssd_conv_w, ssd_conv_b, ssd_dt_bias, ssd_a_log, ssd_d, ssd_norm, ssd_w_out, sb_w_qkv, sb_w_out, ffn_w_in, ffn_conv_w, ffn_conv_b, ffn_w_out, loss_target, m_mix_norm, m_ffn_norm, m_final_norm, m_ssd_w_in, m_ssd_conv_w, m_ssd_conv_b, m_ssd_dt_bias, m_ssd_a_log, m_ssd_d, m_ssd_norm, m_ssd_w_out, m_sb_w_qkv, m_sb_w_out, m_ffn_w_in, m_ffn_conv_w, m_ffn_conv_b, m_ffn_w_out, v_mix_norm, v_ffn_norm, v_final_norm, v_ssd_w_in, v_ssd_conv_w, v_ssd_conv_b, v_ssd_dt_bias, v_ssd_a_log, v_ssd_d, v_ssd_norm, v_ssd_w_out, v_sb_w_qkv, v_sb_w_out, v_ffn_w_in, v_ffn_conv_w, v_ffn_conv_b, v_ffn_w_out):
    given = dict(locals())
    w = {n: given[n] for n in WEIGHTS}
    mom = {n: given["m_" + n] for n in WEIGHTS}
    var = {n: given["v_" + n] for n in WEIGHTS}
    chip = 2 * lax.axis_index("x") + lax.axis_index("y")

    chip1 = chip.reshape(1).astype(jnp.int32)
    fw = _gather_conv_weights(w)
    row = lambda v: v.reshape(1, -1)

    def placed(n, l):
        _, a, b = w[n].shape
        return place_cast(w[n], l, chip1).reshape(N_CHIPS, 2, a // 2, b)

    def mixer_items(i):
        return [(n, i // 2) for n in (("ssd_w_in", "ssd_w_out") if i % 2 == 0 else ("sb_w_qkv", "sb_w_out"))]

    def ffn_items(i):
        return [("ffn_w_in", i), ("ffn_w_out", i)]

    def hook_for(items):
        return gather_hook([placed(n, l) for n, l in items]) if items else None

    lw = {}

    def arrived(items, arrays):
        for (n, l), arr in zip(items, arrays):
            g4 = arr.reshape(N_CHIPS, -1, arr.shape[-1])
            if n in COL_SHARDED:
                full = jnp.transpose(g4, (1, 0, 2)).reshape(g4.shape[1], -1)
            else:
                full = g4.reshape(-1, g4.shape[2])
            if n == "ssd_w_in":
                full = jnp.pad(full, ((0, 0), (0, SSD_IN_PAD - SSD_IN_DIM)))
            lw[(n, l)] = full

    first_items = [("ssd_w_in", 0)]
    carry = {
        (0, "mm_in"): [("ssd_w_out", 0), ("ffn_w_in", 0)],
        (0, "scan"): [("ffn_w_out", 0)] + mixer_items(1),
        (0, "ffn_in"): ffn_items(1),
        (1, "sb"): mixer_items(2) + ffn_items(2),
        (2, "mm_in"): mixer_items(3),
        (2, "scan"): ffn_items(3),
    }
    arrived(first_items, comm_call(hook_for(first_items), "gather_first"))

    def carrying(i, slot, call):
        items = carry.get((i, slot), [])
        if not items:
            return call(None)
        out, got = call(hook_for(items))
        arrived(items, got)
        return out

    xcur = x[0]
    saved = []
    for i in range(DEPTH):
        j = i // 2
        h, r = rms_fwd(xcur, row(mix_norm[i]))
        if i % 2 == 0:
            proj = carrying(i, "mm_in", lambda hk: mm(h, lw[("ssd_w_in", j)], tm=2048, tn=896, name="mm_ssd_in", hook=hk))
            items = carry.get((i, "scan"), [])
            y3, core, got = ssd_core_fwd(proj, fw["ssd_conv_w"][j], row(ssd_conv_b[j]), ssd_dt_bias[j], ssd_a_log[j],
                                         ssd_d[j], row(ssd_norm[j]), hook_for(items))
            arrived(items, got)
            x1 = mm(y3, lw[("ssd_w_out", j)], res=xcur, name="mm_ssd_out")
            mix = (proj, y3, core)
        else:
            qkv = mm(h, lw[("sb_w_qkv", j)], out_dtype=MXU_DTYPE, tm=2048, name="mm_sb_qkv")
            o = carrying(i, "sb", lambda hk: sb_fwd(qkv, hk))
            x1 = mm(o, lw[("sb_w_out", j)], res=xcur, name="mm_sb_out")
            mix = (qkv, o)
        h2, r2 = rms_fwd(x1, row(ffn_norm[i]))
        u0 = carrying(i, "ffn_in", lambda hk: mm(h2, lw[("ffn_w_in", i)], tm=2048, tn=1408, name="mm_ffn_in", hook=hk))
        a = ffn_mid_fwd(u0, fw["ffn_conv_w"][i], row(ffn_conv_b[i]))
        x2 = mm(a, lw[("ffn_w_out", i)], res=x1, name="mm_ffn_out")
        saved.append((xcur, h, r, mix, x1, h2, r2, u0, a))
        xcur = x2
    loss_part, dx, dxm, d_final = loss_head(xcur, row(final_norm), loss_target[0])

    gl = {n: [None] * w[n].shape[0] for n in WEIGHTS if n != "final_norm"}
    units = {n: [None] * w[n].shape[0] for n in BIG}

    def unit_of(g4):
        return g4.reshape(N_CHIPS, 2, g4.shape[1] // 2, g4.shape[2])

    where = jnp.stack([chip, lax.axis_index("c")]).astype(jnp.int32)
    pair_f32, wire, from_chips = {}, {}, {}

    def pair_sums(keys, swapped):
        for (n, l), got in zip(keys, swapped):
            pair_f32[(n, l)], wire[(n, l)] = pair_sum(units[n][l], got, where)

    for i in reversed(range(DEPTH)):
        j = i // 2
        x0, h, r, mix, x1, h2, r2, u0, a = saved[i]
        units["ffn_w_out"][i] = unit_of(mm(a, dxm, "tn", tm=1408, name="mm_d_ffn_out").reshape(N_CHIPS, -1, D_MODEL))
        da = mm(dxm, lw[("ffn_w_out", i)], "nt", tn=1408, name="mm_da_ffn")
        du0, gl["ffn_conv_w"][i], dcb = ffn_mid_bwd(u0, da, fw["ffn_conv_w"][i], row(ffn_conv_b[i]))
        gl["ffn_conv_b"][i] = dcb[0]
        units["ffn_w_in"][i] = unit_of(mm(h2, du0, "tn", tn=1408, tm=512, n_split=N_CHIPS, name="mm_d_ffn_in"))
        keys_f = ffn_items(i)
        swap = pair_swap_hook([units[n][l] for n, l in keys_f])
        dh2, carried = mm(du0, lw[("ffn_w_in", i)], "nt", name="mm_dh_ffn", hook=swap)
        pair_sums(keys_f, split_carried([swap], carried)[0])
        dx1, dx1m, dg = rms_bwd(x1, r2, row(ffn_norm[i]), dh2, dx)
        gl["ffn_norm"][i] = dg[0]
        keys_up = mixer_items(i + 1) if i + 1 < DEPTH else []
        exchanges = [chip_exchange_hook([wire[k] for k in keys_f]),
                     chip_exchange_hook([wire[k] for k in keys_up]) if keys_up else None]
        if i % 2 == 0:
            proj, y3, core = mix
            units["ssd_w_out"][j] = unit_of(mm(y3, dx1m, "tn", name="mm_d_ssd_out").reshape(N_CHIPS, -1, D_MODEL))
            dy3 = mm(dx1m, lw[("ssd_w_out", j)], "nt", name="mm_dy3_ssd")
            (dproj, gl["ssd_conv_w"][j], dcb, gl["ssd_dt_bias"][j], gl["ssd_a_log"][j], gl["ssd_d"][j], dnorm,
             carried) = ssd_core_bwd(proj, fw["ssd_conv_w"][j], row(ssd_conv_b[j]), row(ssd_norm[j]), core, dy3,
                                     merge_hooks(exchanges))
            gl["ssd_conv_b"][j] = dcb[0]
            gl["ssd_norm"][j] = dnorm[0]
            dw_in = mm(h, dproj, "tn", tn=896, name="mm_d_ssd_in")[:, :SSD_IN_DIM]
            units["ssd_w_in"][j] = unit_of(jnp.transpose(dw_in.reshape(D_MODEL, N_CHIPS, -1), (1, 0, 2)))
            dmix, w_in, dh_name = dproj, lw[("ssd_w_in", j)], "mm_dh_ssd"
        else:
            qkv, o = mix
            units["sb_w_out"][j] = unit_of(mm(o, dx1m, "tn", name="mm_d_sb_out").reshape(N_CHIPS, -1, D_MODEL))
            do = mm(dx1m, lw[("sb_w_out", j)], "nt", out_dtype=MXU_DTYPE, name="mm_do_sb")
            dqkv, carried = sb_bwd(qkv, do, merge_hooks(exchanges))
            units["sb_w_qkv"][j] = unit_of(mm(h, dqkv, "tn", tn=768, n_split=N_CHIPS, name="mm_d_sb_qkv"))
            dmix, w_in, dh_name = dqkv, lw[("sb_w_qkv", j)], "mm_dh_sb"
        got = split_carried(exchanges, carried)
        from_chips.update(zip(keys_f, got[0]))
        if keys_up:
            from_chips.update(zip(keys_up, got[1]))
        keys_m = mixer_items(i)
        swap = pair_swap_hook([units[n][l] for n, l in keys_m])
        dh, carried = mm(dmix, w_in, "nt", name=dh_name, hook=swap)
        pair_sums(keys_m, split_carried([swap], carried)[0])
        dx, dxm, dg = rms_bwd(x0, r, row(mix_norm[i]), dh, dx1)
        gl["mix_norm"][i] = dg[0]
    last = chip_exchange_hook([wire[k] for k in mixer_items(0)])
    from_chips.update(zip(mixer_items(0), comm_call(last, "grad_exchange_last")))

    layout = [(k, l) for k, n in enumerate(BIG) for l in range(w[n].shape[0])]
    reduced = _finish_big_grads([pair_f32[(BIG[k], l)] for k, l in layout], [from_chips[(BIG[k], l)] for k, l in layout],
                                layout)
    g, delta, new_m, new_v = {}, {}, {}, {}
    two_d = lambda t: t.reshape(-1, t.shape[-1])
    for n, red in zip(BIG, reduced):
        g[n] = red.reshape(w[n].shape)
        d2, m2, v2 = adamw(two_d(w[n]), two_d(g[n]), two_d(mom[n]), two_d(var[n]), name="adamw_" + n)
        delta[n], new_m[n], new_v[n] = d2.reshape(w[n].shape), m2.reshape(w[n].shape), v2.reshape(w[n].shape)

    small_g = {n: jnp.stack(gl[n]) for n in SMALL + CONVW if n != "final_norm"}
    small_g["final_norm"] = d_final[0]
    zeros_of = lambda n: jnp.zeros((small_g[n].size,), F32)

    def small_pack(d, extra):
        parts = [d[n].reshape(-1) for n in SMALL] + [extra]
        parts += [(d[n].reshape(-1) if d is small_g else zeros_of(n)) for n in CONVW]
        return _to_rows(jnp.concatenate(parts), 16)

    part = small_pack(small_g, loss_part[0, 0:1])
    parts = all_gather_8(jnp.stack([part, part]), "gather_small_grads")
    zero = jnp.zeros((1,), F32)
    gs, ds, ms, vs = adamw_small(small_pack(w, zero), parts, small_pack(mom, zero), small_pack(var, zero))
    gs_flat = gs.reshape(-1)
    off = 0
    for n in SMALL:
        size = w[n].size
        for dst, src in ((g, gs), (delta, ds), (new_m, ms), (new_v, vs)):
            dst[n] = src.reshape(-1)[off:off + size].reshape(w[n].shape)
        off += size
    loss = gs_flat[off]
    off += 1
    for n in CONVW:
        size = small_g[n].size
        b = w[n].shape[-1]
        g[n] = lax.dynamic_slice_in_dim(gs_flat[off:off + size].reshape(small_g[n].shape), chip * b, b, axis=2)
        d2, m2, v2 = adamw(two_d(w[n]), two_d(g[n]), two_d(mom[n]), two_d(var[n]), name="adamw_" + n)
        delta[n], new_m[n], new_v[n] = d2.reshape(w[n].shape), m2.reshape(w[n].shape), v2.reshape(w[n].shape)
        off += size

    return (loss, dx[None], *[g[n] for n in WEIGHTS], *[delta[n] for n in WEIGHTS],
            *[new_m[n] for n in WEIGHTS], *[new_v[n] for n in WEIGHTS])
```
